```python
import math
import jax
import jax.numpy as jnp
from jax import lax
import numpy as np

D_MODEL = 2048
BATCH = 1
SEQ = 16384
DEPTH = 2

HEAD_DIM = 128
N_HEADS_TOTAL = D_MODEL // HEAD_DIM
N_HEADS_A = N_HEADS_TOTAL // 2
N_HEADS_B = N_HEADS_TOTAL - N_HEADS_A
N_HEADS_C = N_HEADS_TOTAL // 2
N_HEADS_D = N_HEADS_TOTAL - N_HEADS_C
DIFF_QK_DIM = HEAD_DIM // 2
Q_BLOCK = 128
MOBA_BLOCK = 256
MOBA_TOPK = 3
MOBA_Q_CHUNK = 32
N_BUCKETS = 32
MAX_EXACT = N_BUCKETS // 2
MAX_DISTANCE = 1024
N_EXPERTS = 16
N_GROUPS = 4
EXPERTS_PER_GROUP = N_EXPERTS // N_GROUPS
TOP_K = 2
D_EXPERT = D_MODEL // 2
ALPHA = (2.0 * DEPTH) ** 0.25
BETA = (8.0 * DEPTH) ** -0.25
LN_EPS = 1e-5
NEG_INF = -1e30
N_EVEN = (DEPTH + 1) // 2
N_ODD = DEPTH // 2
EVEN_SIZES = (N_HEADS_A * HEAD_DIM,) * 3 + (N_HEADS_A,) + (N_HEADS_B * HEAD_DIM,) * 3
ODD_SIZES = (N_HEADS_C * HEAD_DIM,) * 3 + (N_HEADS_D * HEAD_DIM,) * 3

kernel_name = "hybrid_fox_moba_diff_stickbreak_grouped_moe"


def _split_points(sizes):
    return [int(v) for v in np.cumsum(np.asarray(sizes))[:-1]]


def _value_col_scale(sizes, value_slots):
    parts = [np.full((n,), BETA if i in value_slots else 1.0, np.float32) for i, n in enumerate(sizes)]
    return jnp.asarray(np.concatenate(parts))


def layer_norm(x, g, b):
    xf = x.astype(jnp.float32)
    mu = jnp.mean(xf, axis=-1, keepdims=True)
    var = jnp.mean(jnp.square(xf - mu), axis=-1, keepdims=True)
    return ((xf - mu) * lax.rsqrt(var + LN_EPS) * g + b).astype(x.dtype)


def to_heads(t, n_heads):
    b, s, _ = t.shape
    return t.reshape(b, s, n_heads, -1).transpose(0, 2, 1, 3)


def merge_heads(o):
    b, h, s, d = o.shape
    return o.transpose(0, 2, 1, 3).reshape(b, s, h * d)


def t5_bucket(dist):
    n = jnp.maximum(dist, 0)
    nf = jnp.maximum(n, 1).astype(jnp.float32)
    large = MAX_EXACT + (jnp.log(nf / MAX_EXACT) / math.log(MAX_DISTANCE / MAX_EXACT)
                         * (N_BUCKETS - MAX_EXACT)).astype(jnp.int32)
    large = jnp.minimum(large, N_BUCKETS - 1)
    return jnp.where(n < MAX_EXACT, n, large)


def map_query_blocks(fn, q_like, block):
    s = q_like[0].shape[2]
    nblk = s // block

    def split(t):
        t = t.reshape(t.shape[0], t.shape[1], nblk, block, *t.shape[3:])
        return jnp.moveaxis(t, 2, 0)

    xs = (jnp.arange(nblk, dtype=jnp.int32) * block,) + tuple(split(t) for t in q_like)
    out = lax.map(lambda a: fn(*a), xs)
    out = jnp.moveaxis(out, 0, 2)
    return out.reshape(out.shape[0], out.shape[1], s, out.shape[-1])


def forgetting_attention(q, k, v, log_f):
    s = k.shape[2]
    c = lax.cumsum(log_f, axis=2)
    kpos = jnp.arange(s)
    scale = HEAD_DIM ** -0.5

    def block(start, qb, cb):
        qpos = start + jnp.arange(Q_BLOCK)
        logits = jnp.einsum('bhqd,bhkd->bhqk', qb, k, preferred_element_type=jnp.float32) * scale
        logits = logits + cb[..., :, None] - c[..., None, :]
        logits = jnp.where(kpos[None, :] <= qpos[:, None], logits, NEG_INF)
        p = jax.nn.softmax(logits, axis=-1)
        return jnp.einsum('bhqk,bhkd->bhqd', p.astype(v.dtype), v)

    return map_query_blocks(block, (q, c), Q_BLOCK)


def moba_attention(q, k, v, rel_table):
    b, h, s, d = q.shape
    nblk = max(-(-s // MOBA_BLOCK), MOBA_TOPK)
    s_pad = nblk * MOBA_BLOCK
    pad = ((0, 0), (0, 0), (0, s_pad - s), (0, 0))
    kp = jnp.pad(k, pad)
    vp = jnp.pad(v, pad)
    kb = kp.reshape(b, h, nblk, MOBA_BLOCK, d)
    vb = vp.reshape(b, h, nblk, MOBA_BLOCK, d)
    kmean = jnp.mean(kb.astype(jnp.float32), axis=3)
    table_t = rel_table.T
    scale = HEAD_DIM ** -0.5
    bidx = jnp.arange(b)[:, None, None, None]
    hidx = jnp.arange(h)[None, :, None, None]
    inner = jnp.arange(MOBA_BLOCK)
    blk_ids = jnp.arange(nblk)

    def chunk(start, qc):
        qpos = start + jnp.arange(MOBA_Q_CHUNK)
        own = start // MOBA_BLOCK
        gate = jnp.einsum('bhqd,bhnd->bhqn', qc.astype(jnp.float32), kmean)
        gate = jnp.where(blk_ids < own, gate, NEG_INF)
        _, sel = lax.top_k(gate, MOBA_TOPK)
        valid = jnp.arange(MOBA_TOPK) < own
        ks = kb[bidx, hidx, sel]
        vs = vb[bidx, hidx, sel]
        k_own = lax.dynamic_slice_in_dim(kp, own * MOBA_BLOCK, MOBA_BLOCK, axis=2)
        v_own = lax.dynamic_slice_in_dim(vp, own * MOBA_BLOCK, MOBA_BLOCK, axis=2)
        pos_sel = sel[..., None] * MOBA_BLOCK + inner
        pos_own = own * MOBA_BLOCK + inner
        bias_sel = table_t[hidx[..., None], t5_bucket(qpos[None, None, :, None, None] - pos_sel)]
        bias_own = jnp.take(table_t, t5_bucket(qpos[:, None] - pos_own[None, :]), axis=1)
        l_sel = jnp.einsum('bhqd,bhqnkd->bhqnk', qc, ks, preferred_element_type=jnp.float32) * scale
        l_sel = jnp.where(valid[:, None], l_sel + bias_sel, NEG_INF)
        l_own = jnp.einsum('bhqd,bhkd->bhqk', qc, k_own, preferred_element_type=jnp.float32) * scale
        l_own = jnp.where(pos_own[None, :] <= qpos[:, None], l_own + bias_own[None], NEG_INF)
        logits = jnp.concatenate([l_sel.reshape(b, h, MOBA_Q_CHUNK, MOBA_TOPK * MOBA_BLOCK), l_own], axis=-1)
        p = jax.nn.softmax(logits, axis=-1).astype(v.dtype)
        p_sel = p[..., :MOBA_TOPK * MOBA_BLOCK].reshape(b, h, MOBA_Q_CHUNK, MOBA_TOPK, MOBA_BLOCK)
        p_own = p[..., MOBA_TOPK * MOBA_BLOCK:]
        return (jnp.einsum('bhqnk,bhqnkd->bhqd', p_sel, vs)
                + jnp.einsum('bhqk,bhkd->bhqd', p_own, v_own))

    return map_query_blocks(chunk, (q,), MOBA_Q_CHUNK)


def differential_attention(q, k, v, lam, rel_table, subln_g, lambda_init):
    s = k.shape[2]
    kpos = jnp.arange(s)
    table_t = rel_table.T
    scale = DIFF_QK_DIM ** -0.5

    def block(start, qb):
        qpos = start + jnp.arange(Q_BLOCK)
        dist = qpos[:, None] - kpos[None, :]
        bias = jnp.take(table_t, t5_bucket(dist), axis=1)
        logits = jnp.einsum('bhqid,bhkid->bhiqk', qb, k, preferred_element_type=jnp.float32) * scale
        logits = jnp.where(dist >= 0, logits + bias[None, :, None], NEG_INF)
        p = jax.nn.softmax(logits, axis=-1)
        attn = p[:, :, 0] - lam * p[:, :, 1]
        return jnp.einsum('bhqk,bhkd->bhqd', attn.astype(v.dtype), v)

    o = map_query_blocks(block, (q,), Q_BLOCK).astype(jnp.float32)
    o = o * lax.rsqrt(jnp.mean(o * o, axis=-1, keepdims=True) + LN_EPS) * subln_g
    return (o * (1.0 - lambda_init)).astype(v.dtype)


def stick_breaking_attention(q, k, v):
    s = k.shape[2]
    kpos = jnp.arange(s)
    scale = HEAD_DIM ** -0.5

    def block(start, qb):
        qpos = start + jnp.arange(Q_BLOCK)
        mask = kpos[None, :] < qpos[:, None]
        z = jnp.einsum('bhqd,bhkd->bhqk', qb, k, preferred_element_type=jnp.float32) * scale
        log_rem = jnp.where(mask, jax.nn.log_sigmoid(-z), 0.0)
        after = lax.cumsum(log_rem, axis=3, reverse=True) - log_rem
        a = jnp.where(mask, jnp.exp(jax.nn.log_sigmoid(z) + after), 0.0)
        return jnp.einsum('bhqk,bhkd->bhqd', a.astype(v.dtype), v)

    return map_query_blocks(block, (q,), Q_BLOCK)


def grouped_moe(x, w_router, b_router, w_gate, w_up, w_down):
    b, s, _ = x.shape
    logits = jnp.einsum('bsd,de->bse', x, w_router, preferred_element_type=jnp.float32) + b_router
    aff = jax.nn.softmax(logits.astype(jnp.float32), axis=-1)
    grp = aff.reshape(b, s, N_GROUPS, EXPERTS_PER_GROUP)
    g_sel = jnp.argmax(jnp.max(grp, axis=-1), axis=-1)
    g_onehot = jax.nn.one_hot(g_sel, N_GROUPS, dtype=grp.dtype)
    in_group = jnp.sum(grp * g_onehot[..., None], axis=2)
    top_val, top_loc = lax.top_k(in_group, TOP_K)
    top_w = top_val / jnp.sum(top_val, axis=-1, keepdims=True)
    expert_id = g_sel[..., None] * EXPERTS_PER_GROUP + top_loc
    gates = jnp.sum(jax.nn.one_hot(expert_id, N_EXPERTS, dtype=jnp.float32) * top_w[..., None], axis=-2)
    gates = gates.astype(x.dtype)
    out = jnp.zeros_like(x)
    for e in range(N_EXPERTS):
        hid = jax.nn.silu(x @ w_gate[e]) * (x @ w_up[e])
        out = out + gates[..., e:e + 1] * (hid @ w_down[e])
    return out


def setup_inputs(seed: int = 0) -> dict:
    key = jax.random.key(seed)
    ks = jax.random.split(key, 17)
    f32 = jnp.float32

    def nrm(k, shape, sc):
        return jax.random.normal(k, shape, f32) * sc

    mix_width = N_HEADS_TOTAL * HEAD_DIM
    x = nrm(ks[0], (BATCH, SEQ, D_MODEL), 1.0)
    w_in_even = nrm(ks[1], (N_EVEN, D_MODEL, sum(EVEN_SIZES)), D_MODEL ** -0.5) * _value_col_scale(EVEN_SIZES, (2, 6))
    w_out_even = nrm(ks[2], (N_EVEN, mix_width, D_MODEL), mix_width ** -0.5 * BETA)
    forget_bias = jax.random.uniform(ks[3], (N_EVEN, N_HEADS_A), f32, 1.0, 5.0)
    w_in_odd = nrm(ks[4], (N_ODD, D_MODEL, sum(ODD_SIZES)), D_MODEL ** -0.5) * _value_col_scale(ODD_SIZES, (2, 5))
    w_out_odd = nrm(ks[5], (N_ODD, mix_width, D_MODEL), mix_width ** -0.5 * BETA)
    lambda_q = nrm(ks[6], (N_ODD, 2, DIFF_QK_DIM), 0.1)
    lambda_k = nrm(ks[7], (N_ODD, 2, DIFF_QK_DIM), 0.1)
    subln_gain = 1.0 + nrm(ks[8], (N_ODD, HEAD_DIM), 0.02)
    rel_bias = nrm(ks[9], (N_BUCKETS, N_HEADS_B), 0.5)
    w_router = nrm(ks[10], (D_MODEL, N_EXPERTS), D_MODEL ** -0.5)
    b_router = nrm(ks[11], (N_EXPERTS,), 0.01)
    w_gate = nrm(ks[12], (DEPTH, N_EXPERTS, D_MODEL, D_EXPERT), D_MODEL ** -0.5)
    w_up = nrm(ks[13], (DEPTH, N_EXPERTS, D_MODEL, D_EXPERT), D_MODEL ** -0.5)
    w_down = nrm(ks[14], (DEPTH, N_EXPERTS, D_EXPERT, D_MODEL), D_EXPERT ** -0.5 * BETA)
    ln_gain = 1.0 + nrm(ks[15], (DEPTH, 2, D_MODEL), 0.02)
    ln_bias = nrm(ks[16], (DEPTH, 2, D_MODEL), 0.02)
    return {"x": x, "w_in_even": w_in_even, "w_out_even": w_out_even, "forget_bias": forget_bias,
            "w_in_odd": w_in_odd, "w_out_odd": w_out_odd, "lambda_q": lambda_q, "lambda_k": lambda_k,
            "subln_gain": subln_gain, "rel_bias": rel_bias, "w_router": w_router, "b_router": b_router,
            "w_gate": w_gate, "w_up": w_up, "w_down": w_down, "ln_gain": ln_gain, "ln_bias": ln_bias}


def reference(x, w_in_even, w_out_even, forget_bias, w_in_odd, w_out_odd, lambda_q, lambda_k,
              subln_gain, rel_bias, w_router, b_router, w_gate, w_up, w_down, ln_gain, ln_bias):
    b, s, _ = x.shape
    for layer in range(DEPTH):
        i = layer // 2
        if layer % 2 == 0:
            h = jnp.einsum('bsd,dc->bsc', x, w_in_even[i])
            qa, ka, va, fa, qb, kb, vb = jnp.split(h, _split_points(EVEN_SIZES), axis=-1)
            log_f = jax.nn.log_sigmoid((fa + forget_bias[i]).astype(jnp.float32)).transpose(0, 2, 1)
            oa = forgetting_attention(to_heads(qa, N_HEADS_A), to_heads(ka, N_HEADS_A),
                                      to_heads(va, N_HEADS_A), log_f)
            ob = moba_attention(to_heads(qb, N_HEADS_B), to_heads(kb, N_HEADS_B),
                                to_heads(vb, N_HEADS_B), rel_bias)
            o = jnp.concatenate([oa, ob], axis=1)
            y = jnp.einsum('bsc,cd->bsd', merge_heads(o), w_out_even[i])
        else:
            h = jnp.einsum('bsd,dc->bsc', x, w_in_odd[i])
            qc, kc, vc, qd, kd, vd = jnp.split(h, _split_points(ODD_SIZES), axis=-1)
            qc = to_heads(qc, N_HEADS_C).reshape(b, N_HEADS_C, s, 2, DIFF_QK_DIM)
            kc = to_heads(kc, N_HEADS_C).reshape(b, N_HEADS_C, s, 2, DIFF_QK_DIM)
            lambda_init = 0.8 - 0.6 * math.exp(-0.3 * layer)
            lq = lambda_q[i].astype(jnp.float32)
            lk = lambda_k[i].astype(jnp.float32)
            lam = jnp.exp(jnp.sum(lq[0] * lk[0])) - jnp.exp(jnp.sum(lq[1] * lk[1])) + lambda_init
            oc = differential_attention(qc, kc, to_heads(vc, N_HEADS_C), lam, rel_bias,
                                        subln_gain[i], lambda_init)
            od = stick_breaking_attention(to_heads(qd, N_HEADS_D), to_heads(kd, N_HEADS_D),
                                          to_heads(vd, N_HEADS_D))
            o = jnp.concatenate([oc, od], axis=1)
            y = jnp.einsum('bsc,cd->bsd', merge_heads(o), w_out_odd[i])
        x = layer_norm(ALPHA * x + y, ln_gain[layer, 0], ln_bias[layer, 0])
        ff = grouped_moe(x, w_router, b_router, w_gate[layer], w_up[layer], w_down[layer])
        x = layer_norm(ALPHA * x + ff, ln_gain[layer, 1], ln_bias[layer, 1])
    return x
```

```python
import functools
import math

import numpy as np
import jax
import jax.numpy as jnp
from jax import lax
from jax.experimental import pallas as pl
from jax.experimental.pallas import tpu as pltpu

F32 = jnp.float32
BF16 = jnp.bfloat16

D_MODEL = 2048
DEPTH = 2
HEAD_DIM = 128
N_HEADS_GROUP = 8
GROUP_WIDTH = N_HEADS_GROUP * HEAD_DIM
DIFF_QK_DIM = HEAD_DIM // 2
MOBA_BLOCK = 256
MOBA_TOPK = 3
N_BUCKETS = 32
MAX_EXACT = N_BUCKETS // 2
MAX_DISTANCE = 1024
N_EXPERTS = 16
N_GROUPS = 4
EXPERTS_PER_GROUP = N_EXPERTS // N_GROUPS
D_EXPERT = D_MODEL // 2
ALPHA = (2.0 * DEPTH) ** 0.25
LN_EPS = 1e-5
NEG_INF = -1e30
LANES = 128
ATT_TILE = 256
SB_CUTOFF = -104.0
VMEM_LIMIT = 56 * 1024 * 1024


def _cparams(n_axes, vmem=None):
    return pltpu.CompilerParams(dimension_semantics=("arbitrary",) * n_axes,
                                vmem_limit_bytes=vmem)


def _nt_dot(a, b):
    return lax.dot_general(a, b, (((1,), (1,)), ((), ())), preferred_element_type=F32)


def _split_bf16(x):
    hi = x.astype(BF16)
    lo = (x - hi.astype(F32)).astype(BF16)
    return hi, lo


def _layer_norm(z, g, b):
    mu = jnp.mean(z, axis=-1, keepdims=True)
    zc = z - mu
    var = jnp.mean(zc * zc, axis=-1, keepdims=True)
    return zc * lax.rsqrt(var + LN_EPS) * g + b


def _inproj_kernel(x_ref, w_ref, o_ref, xb_ref):
    @pl.when(pl.program_id(1) == 0)
    def _():
        xb_ref[...] = x_ref[...].astype(BF16)

    o_ref[...] = jnp.dot(xb_ref[...], w_ref[...], preferred_element_type=F32).astype(o_ref.dtype)


def _in_proj(x, w, out_dtype, tm, tn):
    s, d = x.shape
    n = w.shape[1]
    return pl.pallas_call(
        _inproj_kernel,
        out_shape=jax.ShapeDtypeStruct((s, n), out_dtype),
        grid=(s // tm, n // tn),
        in_specs=[pl.BlockSpec((tm, d), lambda i, j: (i, 0)),
                  pl.BlockSpec((d, tn), lambda i, j: (0, j))],
        out_specs=pl.BlockSpec((tm, tn), lambda i, j: (i, j)),
        scratch_shapes=[pltpu.VMEM((tm, d), BF16)],
        compiler_params=_cparams(2, VMEM_LIMIT),
        name="in_proj",
    )(x, w)


def _outproj_ln_kernel(oa_ref, ob_ref, wa_ref, wb_ref, x_ref, g_ref, b_ref, o_ref):
    y = jnp.dot(oa_ref[...], wa_ref[...], preferred_element_type=F32)
    y = y + jnp.dot(ob_ref[...], wb_ref[...], preferred_element_type=F32)
    o_ref[...] = _layer_norm(ALPHA * x_ref[...] + y, g_ref[...], b_ref[...])


def _out_proj_ln(oa, ob, w, x, g, b, tm):
    s, d = x.shape
    gw = oa.shape[1]
    return pl.pallas_call(
        _outproj_ln_kernel,
        out_shape=jax.ShapeDtypeStruct((s, d), F32),
        grid=(s // tm,),
        in_specs=[pl.BlockSpec((tm, gw), lambda i: (i, 0)),
                  pl.BlockSpec((tm, gw), lambda i: (i, 0)),
                  pl.BlockSpec((gw, d), lambda i: (0, 0)),
                  pl.BlockSpec((gw, d), lambda i: (1, 0)),
                  pl.BlockSpec((tm, d), lambda i: (i, 0)),
                  pl.BlockSpec((1, d), lambda i: (0, 0)),
                  pl.BlockSpec((1, d), lambda i: (0, 0))],
        out_specs=pl.BlockSpec((tm, d), lambda i: (i, 0)),
        compiler_params=_cparams(1, VMEM_LIMIT),
        name="out_proj_ln",
    )(oa, ob, w, w, x, g.reshape(1, d), b.reshape(1, d))


def _decay_cumsum_kernel(f_ref, b_ref, c_ref):
    nh, s = f_ref.shape
    rows = lax.broadcasted_iota(jnp.int32, (LANES, LANES), 0)
    cols = lax.broadcasted_iota(jnp.int32, (LANES, LANES), 1)
    upper = jnp.where(rows <= cols, 1.0, 0.0).astype(F32)
    bias = b_ref[...]

    def body(n, carry):
        off = pl.multiple_of(n * LANES, LANES)
        z = f_ref[:, pl.ds(off, LANES)] + bias
        logf = jnp.minimum(z, 0.0) - jnp.log1p(jnp.exp(-jnp.abs(z)))
        c = jnp.dot(logf, upper, preferred_element_type=F32,
                    precision=lax.Precision.HIGHEST) + carry
        c_ref[:, pl.ds(off, LANES)] = c
        return carry + jnp.sum(logf, axis=-1, keepdims=True)

    lax.fori_loop(0, s // LANES, body, jnp.zeros((nh, 1), F32))


def _decay_cumsum(f_t, bias):
    nh, s = f_t.shape
    return pl.pallas_call(
        _decay_cumsum_kernel,
        out_shape=jax.ShapeDtypeStruct((nh, s), F32),
        name="decay_cumsum",
    )(f_t, bias.reshape(nh, 1))


def _t5_bucket_np(dist):
    n = np.maximum(dist, 0)
    nf = np.maximum(n, 1).astype(np.float32)
    ratio = np.log(nf / np.float32(MAX_EXACT)) / np.float32(math.log(MAX_DISTANCE / MAX_EXACT))
    large = MAX_EXACT + (ratio.astype(np.float32) * np.float32(N_BUCKETS - MAX_EXACT)).astype(np.int32)
    large = np.minimum(large, N_BUCKETS - 1)
    return np.where(n < MAX_EXACT, n, large).astype(np.int32)


def _near_tile_count(t):
    d = np.arange(0, 4 * MAX_DISTANCE, dtype=np.int64)
    not_last = np.nonzero(_t5_bucket_np(d) != N_BUCKETS - 1)[0]
    d_sat = int(not_last.max()) + 1
    n = 1
    while (n - 1) * t + 1 < d_sat:
        n += 1
    return n


def _bucket_tiles_np(t, n_near):
    r = np.arange(t)[:, None]
    c = np.arange(t)[None, :]
    tiles = []
    for delta in range(n_near):
        dist = delta * t + r - c
        tiles.append(np.where(dist >= 0, _t5_bucket_np(dist), -1))
    return np.stack(tiles).astype(np.int32)


def _bias_tiles_kernel(tab_ref, idx_ref, o_ref):
    h = pl.program_id(0)
    idx = idx_ref[...]
    out = jnp.full(idx.shape, NEG_INF, F32)
    for b in range(N_BUCKETS):
        out = jnp.where(idx == b, tab_ref[b, h], out)
    o_ref[...] = out


def _bias_tiles(rel_bias, t, n_near):
    nh = rel_bias.shape[1]
    idx = jnp.asarray(_bucket_tiles_np(t, n_near))
    return pl.pallas_call(
        _bias_tiles_kernel,
        out_shape=jax.ShapeDtypeStruct((nh, n_near, t, t), F32),
        grid=(nh, n_near),
        in_specs=[pl.BlockSpec(memory_space=pltpu.SMEM),
                  pl.BlockSpec((None, t, t), lambda h, n: (n, 0, 0))],
        out_specs=pl.BlockSpec((None, None, t, t), lambda h, n: (h, n, 0, 0)),
        compiler_params=_cparams(2),
        name="bias_tiles",
    )(rel_bias, idx)


def _online_softmax_step(s, v, carry):
    m, l, acc = carry
    m_new = jnp.maximum(m, jnp.max(s, axis=-1, keepdims=True))
    alpha = jnp.exp(m - m_new)
    p = jnp.exp(s - m_new)
    l = alpha * l + jnp.sum(p, axis=-1, keepdims=True)
    acc = alpha * acc + jnp.dot(p.astype(BF16), v, preferred_element_type=F32)
    return m_new, l, acc


def _softmax_init(rows):
    return (jnp.full((rows, 1), NEG_INF, F32), jnp.zeros((rows, 1), F32),
            jnp.zeros((rows, HEAD_DIM), F32))


def _kv_tile(k_ref, v_ref, j, t):
    off = pl.multiple_of(j * t, t)
    return k_ref[pl.ds(off, t), :], v_ref[pl.ds(off, t), :], off


def _fox_kernel(q_ref, k_ref, v_ref, c_ref, o_ref, *, t, scale):
    i = pl.program_id(1)
    q = q_ref[...]
    rows = lax.broadcasted_iota(jnp.int32, (t, t), 0)
    cols = lax.broadcasted_iota(jnp.int32, (t, t), 1)

    def step(j, carry, diag):
        k, v, off = _kv_tile(k_ref, v_ref, j, t)
        s = _nt_dot(q, k) * scale - c_ref[:, pl.ds(off, t)]
        if diag:
            s = jnp.where(cols <= rows, s, NEG_INF)
        return _online_softmax_step(s, v, carry)

    carry = step(i, _softmax_init(t), True)
    _, l, acc = lax.fori_loop(0, i, lambda j, c: step(j, c, False), carry)
    o_ref[...] = (acc / l).astype(o_ref.dtype)


def _moba_kernel(tab_ref, q_ref, k_ref, v_ref, bias_ref, o_ref, kmean_ref, *, t, nblk, n_near, scale):
    h = pl.program_id(0)
    i = pl.program_id(1)
    nbp = kmean_ref.shape[0]

    @pl.when(i == 0)
    def _():
        kmean_ref[...] = jnp.zeros(kmean_ref.shape, F32)

        def mean_body(n, _):
            off = pl.multiple_of(n * t, t)
            kb = k_ref[pl.ds(off, t), :].astype(F32)
            kmean_ref[pl.ds(n, 1), :] = jnp.sum(kb, axis=0, keepdims=True) * (1.0 / t)
            return 0

        lax.fori_loop(0, nblk, mean_body, 0)

    q = q_ref[...]
    km_hi, km_lo = _split_bf16(kmean_ref[...])
    gate = _nt_dot(q, km_hi) + _nt_dot(q, km_lo)
    blk = lax.broadcasted_iota(jnp.int32, (t, nbp), 1)
    eligible = blk < i
    sel = jnp.zeros((t, nbp), F32)
    for r in range(MOBA_TOPK):
        cand = jnp.where(eligible, jnp.where(sel > 0.0, -jnp.inf, gate), -jnp.inf)
        vmax = jnp.max(cand, axis=-1, keepdims=True)
        first = jnp.min(jnp.where(cand == vmax, blk, nbp), axis=-1, keepdims=True)
        take = jnp.where(r < i, 1.0, 0.0)
        sel = jnp.maximum(sel, jnp.where(blk == first, take, 0.0))

    def step(j, carry, bias, masked):
        k, v, _ = _kv_tile(k_ref, v_ref, j, t)
        s = _nt_dot(q, k) * scale + bias
        if masked:
            chosen = jnp.max(jnp.where(blk == j, sel, 0.0), axis=-1, keepdims=True)
            s = jnp.where(chosen > 0.0, s, NEG_INF)
        return _online_softmax_step(s, v, carry)

    carry = step(i, _softmax_init(t), bias_ref[0], False)
    far_end = jnp.maximum(i - (n_near - 1), 0)
    b_far = tab_ref[N_BUCKETS - 1, h]
    carry = lax.fori_loop(0, far_end, lambda j, c: step(j, c, b_far, True), carry)
    carry = lax.fori_loop(far_end, i, lambda j, c: step(j, c, bias_ref[i - j], True), carry)
    _, l, acc = carry
    o_ref[...] = (acc / l).astype(o_ref.dtype)


def _diff_kernel(tab_ref, q_ref, k_ref, v_ref, bias_ref, lq_ref, lk_ref, g_ref, o_ref,
                 *, t, n_near, scale, lambda_init):
    h = pl.program_id(0)
    i = pl.program_id(1)
    q = q_ref[...]
    lane = lax.broadcasted_iota(jnp.int32, (t, HEAD_DIM), 1)
    zero = jnp.zeros_like(q)
    q2 = jnp.concatenate([jnp.where(lane < DIFF_QK_DIM, q, zero),
                          jnp.where(lane >= DIFF_QK_DIM, q, zero)], axis=0)

    def step(j, carry, bias):
        k, v, _ = _kv_tile(k_ref, v_ref, j, t)
        return _online_softmax_step(_nt_dot(q2, k) * scale + bias, v, carry)

    def tile_bias(delta):
        b = bias_ref[delta]
        return jnp.concatenate([b, b], axis=0)

    carry = step(i, _softmax_init(2 * t), tile_bias(0))
    far_end = jnp.maximum(i - (n_near - 1), 0)
    b_far = tab_ref[N_BUCKETS - 1, h]
    carry = lax.fori_loop(0, far_end, lambda j, c: step(j, c, b_far), carry)
    carry = lax.fori_loop(far_end, i, lambda j, c: step(j, c, tile_bias(i - j)), carry)
    _, l, acc = carry
    o12 = acc / l
    lam_e = jnp.exp(jnp.sum(lq_ref[...] * lk_ref[...], axis=-1, keepdims=True))
    lam = lam_e[0:1, :] - lam_e[1:2, :] + lambda_init
    o = o12[:t, :] - lam * o12[t:, :]
    o = o * lax.rsqrt(jnp.mean(o * o, axis=-1, keepdims=True) + LN_EPS) * g_ref[...]
    o_ref[...] = (o * (1.0 - lambda_init)).astype(o_ref.dtype)


def _sb_kernel(q_ref, k_ref, v_ref, o_ref, *, t, scale):
    i = pl.program_id(1)
    q = q_ref[...]
    rows = lax.broadcasted_iota(jnp.int32, (t, t), 0)
    cols = lax.broadcasted_iota(jnp.int32, (t, t), 1)
    later = jnp.where(rows > cols, 1.0, 0.0).astype(BF16)

    def block(j, rem, acc, diag):
        k, v, _ = _kv_tile(k_ref, v_ref, j, t)
        z = _nt_dot(q, k) * scale
        soft = jnp.log1p(jnp.exp(-jnp.abs(z)))
        log_sig = jnp.minimum(z, 0.0) - soft
        log_rem = jnp.minimum(-z, 0.0) - soft
        if diag:
            mask = cols < rows
            log_rem = jnp.where(mask, log_rem, 0.0)
        hi, lo = _split_bf16(log_rem)
        after = (jnp.dot(hi, later, preferred_element_type=F32)
                 + jnp.dot(lo, later, preferred_element_type=F32))
        a = jnp.exp(log_sig + after + rem)
        if diag:
            a = jnp.where(mask, a, 0.0)
        acc = acc + jnp.dot(a.astype(BF16), v, preferred_element_type=F32)
        rem = rem + jnp.sum(log_rem, axis=-1, keepdims=True)
        return rem, acc

    rem, acc = block(i, jnp.zeros((t, 1), F32), jnp.zeros((t, HEAD_DIM), F32), True)

    def live(rem):
        return (jnp.max(rem) > SB_CUTOFF).astype(jnp.int32)

    def cond(c):
        j, go, _, _ = c
        return jnp.logical_and(j >= 0, go > 0)

    def body(c):
        j, _, rem, acc = c
        rem, acc = block(j, rem, acc, False)
        return j - 1, live(rem), rem, acc

    _, _, _, acc = lax.while_loop(cond, body, (i - 1, live(rem), rem, acc))
    o_ref[...] = acc.astype(o_ref.dtype)


def _attention_call(kernel, h, out_cols, col0, extra_in, extra_specs, scratch, name, t):
    s = h.shape[0]
    nh = N_HEADS_GROUP
    in_specs = list(extra_specs[0]) + [
        pl.BlockSpec((t, HEAD_DIM), lambda hd, i: (i, col0 + hd)),
        pl.BlockSpec((s, HEAD_DIM), lambda hd, i: (0, col0 + nh + hd)),
        pl.BlockSpec((s, HEAD_DIM), lambda hd, i: (0, col0 + 2 * nh + hd)),
    ] + list(extra_specs[1])
    args = list(extra_in[0]) + [h, h, h] + list(extra_in[1])
    return pl.pallas_call(
        kernel,
        out_shape=jax.ShapeDtypeStruct((s, out_cols), BF16),
        grid=(nh, s // t),
        in_specs=in_specs,
        out_specs=pl.BlockSpec((t, HEAD_DIM), lambda hd, i: (i, hd)),
        scratch_shapes=scratch,
        compiler_params=_cparams(2, VMEM_LIMIT),
        name=name,
    )(*args)


def _router_kernel(x_ref, w_ref, b_ref, o_ref):
    logits = jnp.dot(x_ref[...], w_ref[...], preferred_element_type=F32,
                     precision=lax.Precision.HIGHEST) + b_ref[...]
    tm, n = logits.shape
    lane = lax.broadcasted_iota(jnp.int32, (tm, n), 1)
    real = lane < N_EXPERTS
    logits = jnp.where(real, logits, -jnp.inf)
    e = jnp.exp(logits - jnp.max(logits, axis=-1, keepdims=True))
    aff = e / jnp.sum(e, axis=-1, keepdims=True)
    group = lane // EXPERTS_PER_GROUP
    best = jnp.full((tm, 1), -jnp.inf, F32)
    g_sel = jnp.zeros((tm, 1), jnp.int32)
    for g in range(N_GROUPS):
        gmax = jnp.max(jnp.where(group == g, aff, -jnp.inf), axis=-1, keepdims=True)
        better = gmax > best
        g_sel = jnp.where(better, g, g_sel)
        best = jnp.where(better, gmax, best)
    in_group = jnp.logical_and(group == g_sel, real)
    cand = jnp.where(in_group, aff, -jnp.inf)
    v1 = jnp.max(cand, axis=-1, keepdims=True)
    i1 = jnp.min(jnp.where(cand == v1, lane, n), axis=-1, keepdims=True)
    cand2 = jnp.where(lane == i1, -jnp.inf, cand)
    v2 = jnp.max(cand2, axis=-1, keepdims=True)
    i2 = jnp.min(jnp.where(cand2 == v2, lane, n), axis=-1, keepdims=True)
    tot = v1 + v2
    o_ref[...] = jnp.where(lane == i1, v1 / tot, jnp.where(lane == i2, v2 / tot, 0.0))


def _router(x, w_router, b_router, tm):
    s, d = x.shape
    w = jnp.zeros((d, LANES), F32).at[:, :N_EXPERTS].set(w_router)
    b = jnp.zeros((1, LANES), F32).at[0, :N_EXPERTS].set(b_router)
    return pl.pallas_call(
        _router_kernel,
        out_shape=jax.ShapeDtypeStruct((s, LANES), F32),
        grid=(s // tm,),
        in_specs=[pl.BlockSpec((tm, d), lambda i: (i, 0)),
                  pl.BlockSpec((d, LANES), lambda i: (0, 0)),
                  pl.BlockSpec((1, LANES), lambda i: (0, 0))],
        out_specs=pl.BlockSpec((tm, LANES), lambda i: (i, 0)),
        compiler_params=_cparams(1, VMEM_LIMIT),
        name="router",
    )(x, w, b)


def _moe_dense_kernel(x_ref, gates_ref, wg_ref, wu_ref, wd_ref, g_ref, b_ref, o_ref, xb_ref):
    e = pl.program_id(1)
    f = pl.program_id(2)
    first = jnp.logical_and(e == 0, f == 0)
    last = jnp.logical_and(e == pl.num_programs(1) - 1, f == pl.num_programs(2) - 1)

    @pl.when(first)
    def _():
        xb_ref[...] = x_ref[...].astype(BF16)
        o_ref[...] = jnp.zeros(o_ref.shape, F32)

    xb = xb_ref[...]
    gates = gates_ref[...]
    lane = lax.broadcasted_iota(jnp.int32, gates.shape, 1)
    gate = jnp.sum(jnp.where(lane == e, gates, 0.0), axis=-1, keepdims=True)
    a = jnp.dot(xb, wg_ref[...], preferred_element_type=F32)
    u = jnp.dot(xb, wu_ref[...], preferred_element_type=F32)
    hid = (a * jax.nn.sigmoid(a)) * u * gate
    o_ref[...] += jnp.dot(hid.astype(BF16), wd_ref[...], preferred_element_type=F32)

    @pl.when(last)
    def _():
        o_ref[...] = _layer_norm(ALPHA * x_ref[...] + o_ref[...], g_ref[...], b_ref[...])


def _moe_dense_ln(x, gates, wg, wu, wd, g, b, tm, tf):
    s, d = x.shape
    ne, _, fdim = wg.shape
    return pl.pallas_call(
        _moe_dense_kernel,
        out_shape=jax.ShapeDtypeStruct((s, d), F32),
        grid=(s // tm, ne, fdim // tf),
        in_specs=[pl.BlockSpec((tm, d), lambda i, e, f: (i, 0)),
                  pl.BlockSpec((tm, LANES), lambda i, e, f: (i, 0)),
                  pl.BlockSpec((None, d, tf), lambda i, e, f: (e, 0, f)),
                  pl.BlockSpec((None, d, tf), lambda i, e, f: (e, 0, f)),
                  pl.BlockSpec((None, tf, d), lambda i, e, f: (e, f, 0)),
                  pl.BlockSpec((1, d), lambda i, e, f: (0, 0)),
                  pl.BlockSpec((1, d), lambda i, e, f: (0, 0))],
        out_specs=pl.BlockSpec((tm, d), lambda i, e, f: (i, 0)),
        scratch_shapes=[pltpu.VMEM((tm, d), BF16)],
        compiler_params=_cparams(3, VMEM_LIMIT),
        name="moe_dense_ln",
    )(x, gates, wg, wu, wd, g.reshape(1, d), b.reshape(1, d))


def kernel(x, w_in_even, w_out_even, forget_bias, w_in_odd, w_out_odd, lambda_q, lambda_k,
           subln_gain, rel_bias, w_router, b_router, w_gate, w_up, w_down, ln_gain, ln_bias):
    bsz, s, d = x.shape
    assert bsz == 1 and d == D_MODEL and s % MOBA_BLOCK == 0
    t = ATT_TILE
    nh = N_HEADS_GROUP
    gw = GROUP_WIDTH
    tm_proj = min(1024, s)
    tm_row = min(512, s)
    n_near = _near_tile_count(t)
    nblk = s // MOBA_BLOCK
    nbp = -(-nblk // LANES) * LANES

    x2 = x.reshape(s, d)
    bias_tiles = _bias_tiles(rel_bias, t, n_near)
    tab_spec = pl.BlockSpec(memory_space=pltpu.SMEM)
    bias_spec = pl.BlockSpec((None, n_near, t, t), lambda hd, i: (hd, 0, 0, 0))

    for layer in range(DEPTH):
        li = layer // 2
        if layer % 2 == 0:
            w = w_in_even[li]
            fcol = 3 * gw
            w_main = jnp.concatenate([w[:, :fcol], w[:, fcol + nh:]], axis=1).astype(BF16)
            w_f = jnp.zeros((d, LANES), BF16).at[:, :nh].set(w[:, fcol:fcol + nh].astype(BF16))
            h = _in_proj(x2, w_main, BF16, tm_proj, 512)
            f = _in_proj(x2, w_f, F32, tm_proj, LANES)
            c = _decay_cumsum(f[:, :nh].T, forget_bias[li])
            oa = _attention_call(
                functools.partial(_fox_kernel, t=t, scale=HEAD_DIM ** -0.5), h, gw, 0,
                ([], [c.reshape(nh, 1, s)]),
                ([], [pl.BlockSpec((None, 1, s), lambda hd, i: (hd, 0, 0))]),
                [], "fox_attention", t)
            ob = _attention_call(
                functools.partial(_moba_kernel, t=t, nblk=nblk, n_near=n_near, scale=HEAD_DIM ** -0.5),
                h, gw, 3 * nh,
                ([rel_bias], [bias_tiles]), ([tab_spec], [bias_spec]),
                [pltpu.VMEM((nbp, HEAD_DIM), F32)], "moba_attention", t)
            w_out = w_out_even[li]
        else:
            lambda_init = 0.8 - 0.6 * math.exp(-0.3 * layer)
            h = _in_proj(x2, w_in_odd[li].astype(BF16), BF16, tm_proj, 512)
            small = lambda shape: pl.BlockSpec(shape, lambda hd, i: (0, 0))
            oa = _attention_call(
                functools.partial(_diff_kernel, t=t, n_near=n_near, scale=DIFF_QK_DIM ** -0.5,
                                  lambda_init=lambda_init),
                h, gw, 0,
                ([rel_bias], [bias_tiles, lambda_q[li], lambda_k[li], subln_gain[li].reshape(1, HEAD_DIM)]),
                ([tab_spec], [bias_spec, small((2, DIFF_QK_DIM)), small((2, DIFF_QK_DIM)),
                              small((1, HEAD_DIM))]),
                [], "diff_attention", t)
            ob = _attention_call(
                functools.partial(_sb_kernel, t=t, scale=HEAD_DIM ** -0.5), h, gw, 3 * nh,
                ([], []), ([], []), [], "stickbreak_attention", t)
            w_out = w_out_odd[li]
        x2 = _out_proj_ln(oa, ob, w_out.astype(BF16), x2, ln_gain[layer, 0], ln_bias[layer, 0], tm_row)
        gates = _router(x2, w_router, b_router, tm_row)
        x2 = _moe_dense_ln(x2, gates, w_gate[layer].astype(BF16), w_up[layer].astype(BF16),
                           w_down[layer].astype(BF16), ln_gain[layer, 1], ln_bias[layer, 1],
                           tm_row, D_EXPERT // 2)
    return x2.reshape(bsz, s, d)
```

```python
import functools
import math

import numpy as np
import jax
import jax.numpy as jnp
from jax import lax
from jax.experimental import pallas as pl
from jax.experimental.pallas import tpu as pltpu

F32 = jnp.float32
BF16 = jnp.bfloat16

D_MODEL = 2048
DEPTH = 2
HEAD_DIM = 128
N_HEADS_GROUP = 8
GROUP_WIDTH = N_HEADS_GROUP * HEAD_DIM
DIFF_QK_DIM = HEAD_DIM // 2
MOBA_BLOCK = 256
MOBA_TOPK = 3
N_BUCKETS = 32
MAX_EXACT = N_BUCKETS // 2
MAX_DISTANCE = 1024
N_EXPERTS = 16
N_GROUPS = 4
EXPERTS_PER_GROUP = N_EXPERTS // N_GROUPS
D_EXPERT = D_MODEL // 2
ALPHA = (2.0 * DEPTH) ** 0.25
LN_EPS = 1e-5
NEG_INF = -1e30
LOG2E = math.log2(math.e)
LANES = 128
ATT_TILE = 256
ATT_KEYS = 1024
FOX_TQ = 512
FOX_TK = 1024
SB_CUTOFF = -104.0
VMEM_LIMIT = 56 * 1024 * 1024


def _cparams(n_axes, vmem=None):
    return pltpu.CompilerParams(dimension_semantics=("arbitrary",) * n_axes,
                                vmem_limit_bytes=vmem)


def _nt_dot(a, b):
    return lax.dot_general(a, b, (((1,), (1,)), ((), ())), preferred_element_type=F32)


def _split_bf16(x):
    hi = x.astype(BF16)
    lo = (x - hi.astype(F32)).astype(BF16)
    return hi, lo


def _layer_norm(z, g, b):
    mu = jnp.mean(z, axis=-1, keepdims=True)
    zc = z - mu
    var = jnp.mean(zc * zc, axis=-1, keepdims=True)
    return zc * lax.rsqrt(var + LN_EPS) * g + b


def _inproj_kernel(x_ref, w_ref, o_ref, xb_ref):
    @pl.when(pl.program_id(1) == 0)
    def _():
        xb_ref[...] = x_ref[...].astype(BF16)

    o_ref[...] = jnp.dot(xb_ref[...], w_ref[...], preferred_element_type=F32).astype(o_ref.dtype)


def _in_proj(x, w, out_dtype, tm, tn):
    s, d = x.shape
    n = w.shape[1]
    return pl.pallas_call(
        _inproj_kernel,
        out_shape=jax.ShapeDtypeStruct((s, n), out_dtype),
        grid=(s // tm, n // tn),
        in_specs=[pl.BlockSpec((tm, d), lambda i, j: (i, 0)),
                  pl.BlockSpec((d, tn), lambda i, j: (0, j))],
        out_specs=pl.BlockSpec((tm, tn), lambda i, j: (i, j)),
        scratch_shapes=[pltpu.VMEM((tm, d), BF16)],
        compiler_params=_cparams(2, VMEM_LIMIT),
        name="in_proj",
    )(x, w)


def _outproj_ln_kernel(oa_ref, ob_ref, wa_ref, wb_ref, x_ref, g_ref, b_ref, o_ref):
    y = jnp.dot(oa_ref[...], wa_ref[...], preferred_element_type=F32)
    y = y + jnp.dot(ob_ref[...], wb_ref[...], preferred_element_type=F32)
    o_ref[...] = _layer_norm(ALPHA * x_ref[...] + y, g_ref[...], b_ref[...])


def _out_proj_ln(oa, ob, w, x, g, b, tm):
    s, d = x.shape
    gw = oa.shape[1]
    return pl.pallas_call(
        _outproj_ln_kernel,
        out_shape=jax.ShapeDtypeStruct((s, d), F32),
        grid=(s // tm,),
        in_specs=[pl.BlockSpec((tm, gw), lambda i: (i, 0)),
                  pl.BlockSpec((tm, gw), lambda i: (i, 0)),
                  pl.BlockSpec((gw, d), lambda i: (0, 0)),
                  pl.BlockSpec((gw, d), lambda i: (1, 0)),
                  pl.BlockSpec((tm, d), lambda i: (i, 0)),
                  pl.BlockSpec((1, d), lambda i: (0, 0)),
                  pl.BlockSpec((1, d), lambda i: (0, 0))],
        out_specs=pl.BlockSpec((tm, d), lambda i: (i, 0)),
        compiler_params=_cparams(1, VMEM_LIMIT),
        name="out_proj_ln",
    )(oa, ob, w, w, x, g.reshape(1, d), b.reshape(1, d))


def _decay_cumsum_kernel(f_ref, b_ref, c_ref):
    nh, s = f_ref.shape
    rows = lax.broadcasted_iota(jnp.int32, (LANES, LANES), 0)
    cols = lax.broadcasted_iota(jnp.int32, (LANES, LANES), 1)
    upper = jnp.where(rows <= cols, 1.0, 0.0).astype(F32)
    bias = b_ref[...]

    def body(n, carry):
        off = pl.multiple_of(n * LANES, LANES)
        z = f_ref[:, pl.ds(off, LANES)] + bias
        logf = jnp.minimum(z, 0.0) - jnp.log1p(jnp.exp(-jnp.abs(z)))
        c = jnp.dot(logf, upper, preferred_element_type=F32,
                    precision=lax.Precision.HIGHEST) + carry
        c_ref[:, pl.ds(off, LANES)] = c * LOG2E
        return carry + jnp.sum(logf, axis=-1, keepdims=True)

    lax.fori_loop(0, s // LANES, body, jnp.zeros((nh, 1), F32))


def _decay_cumsum(f_t, bias):
    nh, s = f_t.shape
    return pl.pallas_call(
        _decay_cumsum_kernel,
        out_shape=jax.ShapeDtypeStruct((nh, s), F32),
        name="decay_cumsum",
    )(f_t, bias.reshape(nh, 1))


def _t5_bucket_np(dist):
    n = np.maximum(dist, 0)
    nf = np.maximum(n, 1).astype(np.float32)
    ratio = np.log(nf / np.float32(MAX_EXACT)) / np.float32(math.log(MAX_DISTANCE / MAX_EXACT))
    large = MAX_EXACT + (ratio.astype(np.float32) * np.float32(N_BUCKETS - MAX_EXACT)).astype(np.int32)
    large = np.minimum(large, N_BUCKETS - 1)
    return np.where(n < MAX_EXACT, n, large).astype(np.int32)


def _near_tile_count(t):
    d = np.arange(0, 4 * MAX_DISTANCE, dtype=np.int64)
    not_last = np.nonzero(_t5_bucket_np(d) != N_BUCKETS - 1)[0]
    d_sat = int(not_last.max()) + 1
    n = 1
    while (n - 1) * t + 1 < d_sat:
        n += 1
    return n


def _bucket_tiles_np(t, n_near):
    r = np.arange(t)[:, None]
    c = np.arange(t)[None, :]
    tiles = []
    for delta in range(n_near):
        dist = delta * t + r - c
        tiles.append(np.where(dist >= 0, _t5_bucket_np(dist), -1))
    tiles.append(np.full((t, t), N_BUCKETS - 1))
    tiles.append(np.full((t, t), -1))
    return np.stack(tiles).astype(np.int32)


def _bias_tiles_kernel(tab_ref, idx_ref, o_ref):
    h = pl.program_id(0)
    idx = idx_ref[...]
    out = jnp.full(idx.shape, NEG_INF, F32)
    for b in range(N_BUCKETS):
        out = jnp.where(idx == b, tab_ref[b, h] * LOG2E, out)
    o_ref[...] = out


def _bias_tiles(rel_bias, t, n_near):
    nh = rel_bias.shape[1]
    idx = jnp.asarray(_bucket_tiles_np(t, n_near))
    n_tiles = idx.shape[0]
    return pl.pallas_call(
        _bias_tiles_kernel,
        out_shape=jax.ShapeDtypeStruct((nh, n_tiles, t, t), F32),
        grid=(nh, n_tiles),
        in_specs=[pl.BlockSpec(memory_space=pltpu.SMEM),
                  pl.BlockSpec((None, t, t), lambda h, n: (n, 0, 0))],
        out_specs=pl.BlockSpec((None, None, t, t), lambda h, n: (h, n, 0, 0)),
        compiler_params=_cparams(2),
        name="bias_tiles",
    )(rel_bias, idx)


def _online_softmax_step(s, v, carry):
    m, l, acc = carry
    m_new = jnp.maximum(m, jnp.max(s, axis=-1, keepdims=True))
    alpha = jnp.exp2(m - m_new)
    p = jnp.exp2(s - m_new)
    l = alpha * l + jnp.sum(p, axis=-1, keepdims=True)
    acc = alpha * acc + jnp.dot(p.astype(BF16), v, preferred_element_type=F32)
    return m_new, l, acc


def _prescale(q, c):
    return (q.astype(F32) * c).astype(BF16)


def _near_bias(bias_ref, i, j, t, tk, n_near):
    nsub = tk // t
    tiles = []
    for b in range(nsub):
        delta = i - (j * nsub + b)
        tiles.append(bias_ref[jnp.where(delta < 0, n_near + 1, jnp.minimum(delta, n_near))])
    return jnp.concatenate(tiles, axis=1)


def _biased_sweep(i, t, tk, step, init):
    jd = (i * t) // tk
    carry = step(jd, init, True)
    lo = jnp.maximum(jd - 1, 0)
    carry = lax.fori_loop(lo, jd, lambda j, c: step(j, c, True), carry)
    return lax.fori_loop(0, lo, lambda j, c: step(j, c, False), carry)


def _softmax_init(rows):
    return (jnp.full((rows, 1), NEG_INF, F32), jnp.zeros((rows, 1), F32),
            jnp.zeros((rows, HEAD_DIM), F32))


def _kv_tile(k_ref, v_ref, j, t):
    off = pl.multiple_of(j * t, t)
    return k_ref[pl.ds(off, t), :], v_ref[pl.ds(off, t), :], off


def _fox_kernel(q_ref, k_ref, v_ref, c_ref, o_ref, *, t, tk, scale):
    i = pl.program_id(1)
    q = _prescale(q_ref[...], scale * LOG2E)
    rows = lax.broadcasted_iota(jnp.int32, (t, tk), 0)
    cols = lax.broadcasted_iota(jnp.int32, (t, tk), 1)

    def step(j, carry, diag):
        k, v, off = _kv_tile(k_ref, v_ref, j, tk)
        s = _nt_dot(q, k) - c_ref[:, pl.ds(off, tk)]
        if diag:
            s = jnp.where(cols <= rows + (i * t - off), s, NEG_INF)
        return _online_softmax_step(s, v, carry)

    jd = (i * t) // tk
    carry = step(jd, _softmax_init(t), True)
    _, l, acc = lax.fori_loop(0, jd, lambda j, c: step(j, c, False), carry)
    o_ref[...] = (acc / l).astype(o_ref.dtype)


def _moba_kernel(tab_ref, q_ref, k_ref, v_ref, bias_ref, et_ref, o_ref, kmean_ref,
                 *, t, tk, nblk, n_near, scale):
    h = pl.program_id(0)
    i = pl.program_id(1)
    nbp = kmean_ref.shape[0]

    @pl.when(i == 0)
    def _():
        kmean_ref[...] = jnp.zeros(kmean_ref.shape, F32)

        def mean_body(n, _):
            off = pl.multiple_of(n * t, t)
            kb = k_ref[pl.ds(off, t), :].astype(F32)
            kmean_ref[pl.ds(n, 1), :] = jnp.sum(kb, axis=0, keepdims=True) * (1.0 / t)
            return 0

        lax.fori_loop(0, nblk, mean_body, 0)

    q = q_ref[...]
    km_hi, km_lo = _split_bf16(kmean_ref[...])
    gate = _nt_dot(q, km_hi) + _nt_dot(q, km_lo)
    blk = lax.broadcasted_iota(jnp.int32, (t, nbp), 1)
    eligible = blk < i
    sel = jnp.zeros((t, nbp), F32)
    for r in range(MOBA_TOPK):
        cand = jnp.where(eligible, jnp.where(sel > 0.0, -jnp.inf, gate), -jnp.inf)
        vmax = jnp.max(cand, axis=-1, keepdims=True)
        first = jnp.min(jnp.where(cand == vmax, blk, nbp), axis=-1, keepdims=True)
        take = jnp.where(r < i, 1.0, 0.0)
        sel = jnp.maximum(sel, jnp.where(blk == first, take, 0.0))

    sel = jnp.maximum(sel, jnp.where(blk == i, 1.0, 0.0))
    penalty = jnp.where(sel > 0.0, 0.0, NEG_INF).astype(BF16)
    q_aug = jnp.concatenate([_prescale(q, scale * LOG2E), penalty], axis=1)
    b_far = tab_ref[N_BUCKETS - 1, h] * LOG2E

    def step(j, carry, near):
        off = pl.multiple_of(j * tk, tk)
        k_aug = jnp.concatenate([k_ref[pl.ds(off, tk), :], et_ref[pl.ds(off, tk), :]], axis=1)
        bias = _near_bias(bias_ref, i, j, t, tk, n_near) if near else b_far
        return _online_softmax_step(_nt_dot(q_aug, k_aug) + bias, v_ref[pl.ds(off, tk), :], carry)

    _, l, acc = _biased_sweep(i, t, tk, step, _softmax_init(t))
    o_ref[...] = (acc / l).astype(o_ref.dtype)


def _diff_kernel(tab_ref, q_ref, k_ref, v_ref, bias_ref, lq_ref, lk_ref, g_ref, o_ref,
                 *, t, tk, n_near, scale, lambda_init):
    h = pl.program_id(0)
    i = pl.program_id(1)
    q = _prescale(q_ref[...], scale * LOG2E)
    lane = lax.broadcasted_iota(jnp.int32, (t, HEAD_DIM), 1)
    zero = jnp.zeros_like(q)
    q2 = jnp.concatenate([jnp.where(lane < DIFF_QK_DIM, q, zero),
                          jnp.where(lane >= DIFF_QK_DIM, q, zero)], axis=0)
    b_far = tab_ref[N_BUCKETS - 1, h] * LOG2E

    def step(j, carry, near):
        k, v, _ = _kv_tile(k_ref, v_ref, j, tk)
        if near:
            b = _near_bias(bias_ref, i, j, t, tk, n_near)
            bias = jnp.concatenate([b, b], axis=0)
        else:
            bias = b_far
        return _online_softmax_step(_nt_dot(q2, k) + bias, v, carry)

    _, l, acc = _biased_sweep(i, t, tk, step, _softmax_init(2 * t))
    o12 = acc / l
    lam_e = jnp.exp(jnp.sum(lq_ref[...] * lk_ref[...], axis=-1, keepdims=True))
    lam = lam_e[0:1, :] - lam_e[1:2, :] + lambda_init
    o = o12[:t, :] - lam * o12[t:, :]
    o = o * lax.rsqrt(jnp.mean(o * o, axis=-1, keepdims=True) + LN_EPS) * g_ref[...]
    o_ref[...] = (o * (1.0 - lambda_init)).astype(o_ref.dtype)


def _sb_kernel(q_ref, k_ref, v_ref, o_ref, *, t, scale):
    i = pl.program_id(1)
    q = q_ref[...]
    rows = lax.broadcasted_iota(jnp.int32, (t, t), 0)
    cols = lax.broadcasted_iota(jnp.int32, (t, t), 1)
    later = jnp.where(rows > cols, 1.0, 0.0).astype(BF16)

    def block(j, rem, acc, diag):
        k, v, _ = _kv_tile(k_ref, v_ref, j, t)
        z = _nt_dot(q, k) * scale
        soft = jnp.log1p(jnp.exp(-jnp.abs(z)))
        log_sig = jnp.minimum(z, 0.0) - soft
        log_rem = jnp.minimum(-z, 0.0) - soft
        if diag:
            mask = cols < rows
            log_rem = jnp.where(mask, log_rem, 0.0)
        hi, lo = _split_bf16(log_rem)
        after = (jnp.dot(hi, later, preferred_element_type=F32)
                 + jnp.dot(lo, later, preferred_element_type=F32))
        a = jnp.exp(log_sig + after + rem)
        if diag:
            a = jnp.where(mask, a, 0.0)
        acc = acc + jnp.dot(a.astype(BF16), v, preferred_element_type=F32)
        rem = rem + jnp.sum(log_rem, axis=-1, keepdims=True)
        return rem, acc

    rem, acc = block(i, jnp.zeros((t, 1), F32), jnp.zeros((t, HEAD_DIM), F32), True)

    def live(rem):
        return (jnp.max(rem) > SB_CUTOFF).astype(jnp.int32)

    def cond(c):
        j, go, _, _ = c
        return jnp.logical_and(j >= 0, go > 0)

    def body(c):
        j, _, rem, acc = c
        rem, acc = block(j, rem, acc, False)
        return j - 1, live(rem), rem, acc

    _, _, _, acc = lax.while_loop(cond, body, (i - 1, live(rem), rem, acc))
    o_ref[...] = acc.astype(o_ref.dtype)


def _attention_call(kernel, h, out_cols, col0, extra_in, extra_specs, scratch, name, t):
    s = h.shape[0]
    nh = N_HEADS_GROUP
    in_specs = list(extra_specs[0]) + [
        pl.BlockSpec((t, HEAD_DIM), lambda hd, i: (i, col0 + hd)),
        pl.BlockSpec((s, HEAD_DIM), lambda hd, i: (0, col0 + nh + hd)),
        pl.BlockSpec((s, HEAD_DIM), lambda hd, i: (0, col0 + 2 * nh + hd)),
    ] + list(extra_specs[1])
    args = list(extra_in[0]) + [h, h, h] + list(extra_in[1])
    return pl.pallas_call(
        kernel,
        out_shape=jax.ShapeDtypeStruct((s, out_cols), BF16),
        grid=(nh, s // t),
        in_specs=in_specs,
        out_specs=pl.BlockSpec((t, HEAD_DIM), lambda hd, i: (i, hd)),
        scratch_shapes=scratch,
        compiler_params=_cparams(2, VMEM_LIMIT),
        name=name,
    )(*args)


def _router_kernel(x_ref, w_ref, b_ref, o_ref):
    logits = jnp.dot(x_ref[...], w_ref[...], preferred_element_type=F32,
                     precision=lax.Precision.HIGHEST) + b_ref[...]
    tm, n = logits.shape
    lane = lax.broadcasted_iota(jnp.int32, (tm, n), 1)
    real = lane < N_EXPERTS
    logits = jnp.where(real, logits, -jnp.inf)
    e = jnp.exp(logits - jnp.max(logits, axis=-1, keepdims=True))
    aff = e / jnp.sum(e, axis=-1, keepdims=True)
    group = lane // EXPERTS_PER_GROUP
    best = jnp.full((tm, 1), -jnp.inf, F32)
    g_sel = jnp.zeros((tm, 1), jnp.int32)
    for g in range(N_GROUPS):
        gmax = jnp.max(jnp.where(group == g, aff, -jnp.inf), axis=-1, keepdims=True)
        better = gmax > best
        g_sel = jnp.where(better, g, g_sel)
        best = jnp.where(better, gmax, best)
    in_group = jnp.logical_and(group == g_sel, real)
    cand = jnp.where(in_group, aff, -jnp.inf)
    v1 = jnp.max(cand, axis=-1, keepdims=True)
    i1 = jnp.min(jnp.where(cand == v1, lane, n), axis=-1, keepdims=True)
    cand2 = jnp.where(lane == i1, -jnp.inf, cand)
    v2 = jnp.max(cand2, axis=-1, keepdims=True)
    i2 = jnp.min(jnp.where(cand2 == v2, lane, n), axis=-1, keepdims=True)
    tot = v1 + v2
    o_ref[...] = jnp.where(lane == i1, v1 / tot, jnp.where(lane == i2, v2 / tot, 0.0))


def _router(x, w_router, b_router, tm):
    s, d = x.shape
    w = jnp.zeros((d, LANES), F32).at[:, :N_EXPERTS].set(w_router)
    b = jnp.zeros((1, LANES), F32).at[0, :N_EXPERTS].set(b_router)
    return pl.pallas_call(
        _router_kernel,
        out_shape=jax.ShapeDtypeStruct((s, LANES), F32),
        grid=(s // tm,),
        in_specs=[pl.BlockSpec((tm, d), lambda i: (i, 0)),
                  pl.BlockSpec((d, LANES), lambda i: (0, 0)),
                  pl.BlockSpec((1, LANES), lambda i: (0, 0))],
        out_specs=pl.BlockSpec((tm, LANES), lambda i: (i, 0)),
        compiler_params=_cparams(1, VMEM_LIMIT),
        name="router",
    )(x, w, b)


def _moe_dense_kernel(x_ref, gates_ref, wg_ref, wu_ref, wd_ref, g_ref, b_ref, o_ref, xb_ref):
    e = pl.program_id(1)
    f = pl.program_id(2)
    first = jnp.logical_and(e == 0, f == 0)
    last = jnp.logical_and(e == pl.num_programs(1) - 1, f == pl.num_programs(2) - 1)

    @pl.when(first)
    def _():
        xb_ref[...] = x_ref[...].astype(BF16)
        o_ref[...] = jnp.zeros(o_ref.shape, F32)

    xb = xb_ref[...]
    gates = gates_ref[...]
    lane = lax.broadcasted_iota(jnp.int32, gates.shape, 1)
    gate = jnp.sum(jnp.where(lane == e, gates, 0.0), axis=-1, keepdims=True)
    a = jnp.dot(xb, wg_ref[...], preferred_element_type=F32)
    u = jnp.dot(xb, wu_ref[...], preferred_element_type=F32)
    hid = (a * jax.nn.sigmoid(a)) * u * gate
    o_ref[...] += jnp.dot(hid.astype(BF16), wd_ref[...], preferred_element_type=F32)

    @pl.when(last)
    def _():
        o_ref[...] = _layer_norm(ALPHA * x_ref[...] + o_ref[...], g_ref[...], b_ref[...])


def _moe_dense_ln(x, gates, wg, wu, wd, g, b, tm, tf):
    s, d = x.shape
    ne, _, fdim = wg.shape
    return pl.pallas_call(
        _moe_dense_kernel,
        out_shape=jax.ShapeDtypeStruct((s, d), F32),
        grid=(s // tm, ne, fdim // tf),
        in_specs=[pl.BlockSpec((tm, d), lambda i, e, f: (i, 0)),
                  pl.BlockSpec((tm, LANES), lambda i, e, f: (i, 0)),
                  pl.BlockSpec((None, d, tf), lambda i, e, f: (e, 0, f)),
                  pl.BlockSpec((None, d, tf), lambda i, e, f: (e, 0, f)),
                  pl.BlockSpec((None, tf, d), lambda i, e, f: (e, f, 0)),
                  pl.BlockSpec((1, d), lambda i, e, f: (0, 0)),
                  pl.BlockSpec((1, d), lambda i, e, f: (0, 0))],
        out_specs=pl.BlockSpec((tm, d), lambda i, e, f: (i, 0)),
        scratch_shapes=[pltpu.VMEM((tm, d), BF16)],
        compiler_params=_cparams(3, VMEM_LIMIT),
        name="moe_dense_ln",
    )(x, gates, wg, wu, wd, g.reshape(1, d), b.reshape(1, d))


def kernel(x, w_in_even, w_out_even, forget_bias, w_in_odd, w_out_odd, lambda_q, lambda_k,
           subln_gain, rel_bias, w_router, b_router, w_gate, w_up, w_down, ln_gain, ln_bias):
    bsz, s, d = x.shape
    assert bsz == 1 and d == D_MODEL and s % MOBA_BLOCK == 0
    t = ATT_TILE
    nh = N_HEADS_GROUP
    gw = GROUP_WIDTH
    tm_proj = min(1024, s)
    tm_row = min(512, s)
    tk = min(ATT_KEYS, s)
    n_near = _near_tile_count(t)
    assert s % tk == 0 and tk // t + 1 >= n_near
    nblk = s // MOBA_BLOCK
    nbp = -(-nblk // LANES) * LANES

    x2 = x.reshape(s, d)
    bias_tiles = _bias_tiles(rel_bias, t, n_near)
    tab_spec = pl.BlockSpec(memory_space=pltpu.SMEM)
    bias_spec = pl.BlockSpec((None, n_near + 2, t, t), lambda hd, i: (hd, 0, 0, 0))
    block_onehot = (jnp.arange(s, dtype=jnp.int32)[:, None] // MOBA_BLOCK
                    == jnp.arange(nbp, dtype=jnp.int32)[None, :]).astype(BF16)

    for layer in range(DEPTH):
        li = layer // 2
        if layer % 2 == 0:
            w = w_in_even[li]
            fcol = 3 * gw
            w_main = jnp.concatenate([w[:, :fcol], w[:, fcol + nh:]], axis=1).astype(BF16)
            w_f = jnp.zeros((d, LANES), BF16).at[:, :nh].set(w[:, fcol:fcol + nh].astype(BF16))
            h = _in_proj(x2, w_main, BF16, tm_proj, 512)
            f = _in_proj(x2, w_f, F32, tm_proj, LANES)
            c = _decay_cumsum(f[:, :nh].T, forget_bias[li])
            oa = _attention_call(
                functools.partial(_fox_kernel, t=FOX_TQ, tk=min(FOX_TK, s), scale=HEAD_DIM ** -0.5), h, gw, 0,
                ([], [c.reshape(nh, 1, s)]),
                ([], [pl.BlockSpec((None, 1, s), lambda hd, i: (hd, 0, 0))]),
                [], "fox_attention", FOX_TQ)
            ob = _attention_call(
                functools.partial(_moba_kernel, t=t, tk=tk, nblk=nblk, n_near=n_near,
                                  scale=HEAD_DIM ** -0.5),
                h, gw, 3 * nh,
                ([rel_bias], [bias_tiles, block_onehot]),
                ([tab_spec], [bias_spec, pl.BlockSpec((s, nbp), lambda hd, i: (0, 0))]),
                [pltpu.VMEM((nbp, HEAD_DIM), F32)], "moba_attention", t)
            w_out = w_out_even[li]
        else:
            lambda_init = 0.8 - 0.6 * math.exp(-0.3 * layer)
            h = _in_proj(x2, w_in_odd[li].astype(BF16), BF16, tm_proj, 512)
            small = lambda shape: pl.BlockSpec(shape, lambda hd, i: (0, 0))
            oa = _attention_call(
                functools.partial(_diff_kernel, t=t, tk=tk, n_near=n_near, scale=DIFF_QK_DIM ** -0.5,
                                  lambda_init=lambda_init),
                h, gw, 0,
                ([rel_bias], [bias_tiles, lambda_q[li], lambda_k[li], subln_gain[li].reshape(1, HEAD_DIM)]),
                ([tab_spec], [bias_spec, small((2, DIFF_QK_DIM)), small((2, DIFF_QK_DIM)),
                              small((1, HEAD_DIM))]),
                [], "diff_attention", t)
            ob = _attention_call(
                functools.partial(_sb_kernel, t=t, scale=HEAD_DIM ** -0.5), h, gw, 3 * nh,
                ([], []), ([], []), [], "stickbreak_attention", t)
            w_out = w_out_odd[li]
        x2 = _out_proj_ln(oa, ob, w_out.astype(BF16), x2, ln_gain[layer, 0], ln_bias[layer, 0], tm_row)
        gates = _router(x2, w_router, b_router, tm_row)
        x2 = _moe_dense_ln(x2, gates, w_gate[layer].astype(BF16), w_up[layer].astype(BF16),
                           w_down[layer].astype(BF16), ln_gain[layer, 1], ln_bias[layer, 1],
                           tm_row, D_EXPERT // 2)
    return x2.reshape(bsz, s, d)
```

```python
import functools
import math

import numpy as np
import jax
import jax.numpy as jnp
from jax import lax
from jax.experimental import pallas as pl
from jax.experimental.pallas import tpu as pltpu

F32 = jnp.float32
BF16 = jnp.bfloat16

D_MODEL = 2048
DEPTH = 2
HEAD_DIM = 128
N_HEADS_GROUP = 8
GROUP_WIDTH = N_HEADS_GROUP * HEAD_DIM
DIFF_QK_DIM = HEAD_DIM // 2
MOBA_BLOCK = 256
MOBA_TOPK = 3
N_BUCKETS = 32
MAX_EXACT = N_BUCKETS // 2
MAX_DISTANCE = 1024
N_EXPERTS = 16
N_GROUPS = 4
EXPERTS_PER_GROUP = N_EXPERTS // N_GROUPS
MOE_TOP_K = 2
MOE_GATHER_CHUNK = 2048
D_EXPERT = D_MODEL // 2
ALPHA = (2.0 * DEPTH) ** 0.25
LN_EPS = 1e-5
NEG_INF = -1e30
LOG2E = math.log2(math.e)
LANES = 128
ATT_TILE = 256
ATT_KEYS = 1024
FOX_TQ = 512
FOX_TK = 1024
SB_CUTOFF = -104.0
VMEM_LIMIT = 56 * 1024 * 1024


def _cparams(n_axes, vmem=None):
    return pltpu.CompilerParams(dimension_semantics=("arbitrary",) * n_axes,
                                vmem_limit_bytes=vmem)


def _nt_dot(a, b):
    return lax.dot_general(a, b, (((1,), (1,)), ((), ())), preferred_element_type=F32)


def _split_bf16(x):
    hi = x.astype(BF16)
    lo = (x - hi.astype(F32)).astype(BF16)
    return hi, lo


def _layer_norm(z, g, b):
    mu = jnp.mean(z, axis=-1, keepdims=True)
    zc = z - mu
    var = jnp.mean(zc * zc, axis=-1, keepdims=True)
    return zc * lax.rsqrt(var + LN_EPS) * g + b


def _inproj_kernel(x_ref, w_ref, o_ref, xb_ref):
    @pl.when(pl.program_id(1) == 0)
    def _():
        xb_ref[...] = x_ref[...].astype(BF16)

    o_ref[...] = jnp.dot(xb_ref[...], w_ref[...], preferred_element_type=F32).astype(o_ref.dtype)


def _in_proj(x, w, out_dtype, tm, tn):
    s, d = x.shape
    n = w.shape[1]
    return pl.pallas_call(
        _inproj_kernel,
        out_shape=jax.ShapeDtypeStruct((s, n), out_dtype),
        grid=(s // tm, n // tn),
        in_specs=[pl.BlockSpec((tm, d), lambda i, j: (i, 0)),
                  pl.BlockSpec((d, tn), lambda i, j: (0, j))],
        out_specs=pl.BlockSpec((tm, tn), lambda i, j: (i, j)),
        scratch_shapes=[pltpu.VMEM((tm, d), BF16)],
        compiler_params=_cparams(2, VMEM_LIMIT),
        name="in_proj",
    )(x, w)


def _outproj_ln_kernel(oa_ref, ob_ref, wa_ref, wb_ref, x_ref, g_ref, b_ref, o_ref):
    y = jnp.dot(oa_ref[...], wa_ref[...], preferred_element_type=F32)
    y = y + jnp.dot(ob_ref[...], wb_ref[...], preferred_element_type=F32)
    o_ref[...] = _layer_norm(ALPHA * x_ref[...] + y, g_ref[...], b_ref[...])


def _out_proj_ln(oa, ob, w, x, g, b, tm):
    s, d = x.shape
    gw = oa.shape[1]
    return pl.pallas_call(
        _outproj_ln_kernel,
        out_shape=jax.ShapeDtypeStruct((s, d), F32),
        grid=(s // tm,),
        in_specs=[pl.BlockSpec((tm, gw), lambda i: (i, 0)),
                  pl.BlockSpec((tm, gw), lambda i: (i, 0)),
                  pl.BlockSpec((gw, d), lambda i: (0, 0)),
                  pl.BlockSpec((gw, d), lambda i: (1, 0)),
                  pl.BlockSpec((tm, d), lambda i: (i, 0)),
                  pl.BlockSpec((1, d), lambda i: (0, 0)),
                  pl.BlockSpec((1, d), lambda i: (0, 0))],
        out_specs=pl.BlockSpec((tm, d), lambda i: (i, 0)),
        compiler_params=_cparams(1, VMEM_LIMIT),
        name="out_proj_ln",
    )(oa, ob, w, w, x, g.reshape(1, d), b.reshape(1, d))


def _decay_cumsum_kernel(f_ref, b_ref, c_ref):
    nh, s = f_ref.shape
    rows = lax.broadcasted_iota(jnp.int32, (LANES, LANES), 0)
    cols = lax.broadcasted_iota(jnp.int32, (LANES, LANES), 1)
    upper = jnp.where(rows <= cols, 1.0, 0.0).astype(F32)
    bias = b_ref[...]

    def body(n, carry):
        off = pl.multiple_of(n * LANES, LANES)
        z = f_ref[:, pl.ds(off, LANES)] + bias
        logf = jnp.minimum(z, 0.0) - jnp.log1p(jnp.exp(-jnp.abs(z)))
        c = jnp.dot(logf, upper, preferred_element_type=F32,
                    precision=lax.Precision.HIGHEST) + carry
        c_ref[:, pl.ds(off, LANES)] = c * LOG2E
        return carry + jnp.sum(logf, axis=-1, keepdims=True)

    lax.fori_loop(0, s // LANES, body, jnp.zeros((nh, 1), F32))


def _decay_cumsum(f_t, bias):
    nh, s = f_t.shape
    return pl.pallas_call(
        _decay_cumsum_kernel,
        out_shape=jax.ShapeDtypeStruct((nh, s), F32),
        name="decay_cumsum",
    )(f_t, bias.reshape(nh, 1))


def _t5_bucket_np(dist):
    n = np.maximum(dist, 0)
    nf = np.maximum(n, 1).astype(np.float32)
    ratio = np.log(nf / np.float32(MAX_EXACT)) / np.float32(math.log(MAX_DISTANCE / MAX_EXACT))
    large = MAX_EXACT + (ratio.astype(np.float32) * np.float32(N_BUCKETS - MAX_EXACT)).astype(np.int32)
    large = np.minimum(large, N_BUCKETS - 1)
    return np.where(n < MAX_EXACT, n, large).astype(np.int32)


def _near_tile_count(t):
    d = np.arange(0, 4 * MAX_DISTANCE, dtype=np.int64)
    not_last = np.nonzero(_t5_bucket_np(d) != N_BUCKETS - 1)[0]
    d_sat = int(not_last.max()) + 1
    n = 1
    while (n - 1) * t + 1 < d_sat:
        n += 1
    return n


def _bucket_tiles_np(t, n_near):
    r = np.arange(t)[:, None]
    c = np.arange(t)[None, :]
    tiles = []
    for delta in range(n_near):
        dist = delta * t + r - c
        tiles.append(np.where(dist >= 0, _t5_bucket_np(dist), -1))
    tiles.append(np.full((t, t), N_BUCKETS - 1))
    tiles.append(np.full((t, t), -1))
    return np.stack(tiles).astype(np.int32)


def _bias_tiles_kernel(tab_ref, idx_ref, o_ref):
    h = pl.program_id(0)
    idx = idx_ref[...]
    out = jnp.full(idx.shape, NEG_INF, F32)
    for b in range(N_BUCKETS):
        out = jnp.where(idx == b, tab_ref[b, h] * LOG2E, out)
    o_ref[...] = out


def _bias_tiles(rel_bias, t, n_near):
    nh = rel_bias.shape[1]
    idx = jnp.asarray(_bucket_tiles_np(t, n_near))
    n_tiles = idx.shape[0]
    return pl.pallas_call(
        _bias_tiles_kernel,
        out_shape=jax.ShapeDtypeStruct((nh, n_tiles, t, t), F32),
        grid=(nh, n_tiles),
        in_specs=[pl.BlockSpec(memory_space=pltpu.SMEM),
                  pl.BlockSpec((None, t, t), lambda h, n: (n, 0, 0))],
        out_specs=pl.BlockSpec((None, None, t, t), lambda h, n: (h, n, 0, 0)),
        compiler_params=_cparams(2),
        name="bias_tiles",
    )(rel_bias, idx)


def _online_softmax_step(s, v, carry):
    m, l, acc = carry
    m_new = jnp.maximum(m, jnp.max(s, axis=-1, keepdims=True))
    alpha = jnp.exp2(m - m_new)
    p = jnp.exp2(s - m_new)
    l = alpha * l + jnp.sum(p, axis=-1, keepdims=True)
    acc = alpha * acc + jnp.dot(p.astype(BF16), v, preferred_element_type=F32)
    return m_new, l, acc


def _prescale(q, c):
    return (q.astype(F32) * c).astype(BF16)


def _near_bias(bias_ref, i, j, t, tk, n_near):
    nsub = tk // t
    tiles = []
    for b in range(nsub):
        delta = i - (j * nsub + b)
        tiles.append(bias_ref[jnp.where(delta < 0, n_near + 1, jnp.minimum(delta, n_near))])
    return jnp.concatenate(tiles, axis=1)


def _biased_sweep(i, t, tk, step, init):
    jd = (i * t) // tk
    carry = step(jd, init, True)
    lo = jnp.maximum(jd - 1, 0)
    carry = lax.fori_loop(lo, jd, lambda j, c: step(j, c, True), carry)
    return lax.fori_loop(0, lo, lambda j, c: step(j, c, False), carry)


def _softmax_init(rows):
    return (jnp.full((rows, 1), NEG_INF, F32), jnp.zeros((rows, 1), F32),
            jnp.zeros((rows, HEAD_DIM), F32))


def _kv_tile(k_ref, v_ref, j, t):
    off = pl.multiple_of(j * t, t)
    return k_ref[pl.ds(off, t), :], v_ref[pl.ds(off, t), :], off


def _fox_kernel(q_ref, k_ref, v_ref, c_ref, o_ref, *, t, tk, scale):
    i = pl.program_id(1)
    q = _prescale(q_ref[...], scale * LOG2E)
    rows = lax.broadcasted_iota(jnp.int32, (t, tk), 0)
    cols = lax.broadcasted_iota(jnp.int32, (t, tk), 1)

    def step(j, carry, diag):
        k, v, off = _kv_tile(k_ref, v_ref, j, tk)
        s = _nt_dot(q, k) - c_ref[:, pl.ds(off, tk)]
        if diag:
            s = jnp.where(cols <= rows + (i * t - off), s, NEG_INF)
        return _online_softmax_step(s, v, carry)

    jd = (i * t) // tk
    carry = step(jd, _softmax_init(t), True)
    _, l, acc = lax.fori_loop(0, jd, lambda j, c: step(j, c, False), carry)
    o_ref[...] = (acc / l).astype(o_ref.dtype)


def _moba_kernel(tab_ref, q_ref, k_ref, v_ref, bias_ref, et_ref, o_ref, kmean_ref,
                 *, t, tk, nblk, n_near, scale):
    h = pl.program_id(0)
    i = pl.program_id(1)
    nbp = kmean_ref.shape[0]

    @pl.when(i == 0)
    def _():
        kmean_ref[...] = jnp.zeros(kmean_ref.shape, F32)

        def mean_body(n, _):
            off = pl.multiple_of(n * t, t)
            kb = k_ref[pl.ds(off, t), :].astype(F32)
            kmean_ref[pl.ds(n, 1), :] = jnp.sum(kb, axis=0, keepdims=True) * (1.0 / t)
            return 0

        lax.fori_loop(0, nblk, mean_body, 0)

    q = q_ref[...]
    km_hi, km_lo = _split_bf16(kmean_ref[...])
    gate = _nt_dot(q, km_hi) + _nt_dot(q, km_lo)
    blk = lax.broadcasted_iota(jnp.int32, (t, nbp), 1)
    eligible = blk < i
    sel = jnp.zeros((t, nbp), F32)
    for r in range(MOBA_TOPK):
        cand = jnp.where(eligible, jnp.where(sel > 0.0, -jnp.inf, gate), -jnp.inf)
        vmax = jnp.max(cand, axis=-1, keepdims=True)
        first = jnp.min(jnp.where(cand == vmax, blk, nbp), axis=-1, keepdims=True)
        take = jnp.where(r < i, 1.0, 0.0)
        sel = jnp.maximum(sel, jnp.where(blk == first, take, 0.0))

    sel = jnp.maximum(sel, jnp.where(blk == i, 1.0, 0.0))
    penalty = jnp.where(sel > 0.0, 0.0, NEG_INF).astype(BF16)
    q_aug = jnp.concatenate([_prescale(q, scale * LOG2E), penalty], axis=1)
    b_far = tab_ref[N_BUCKETS - 1, h] * LOG2E

    def step(j, carry, near):
        off = pl.multiple_of(j * tk, tk)
        k_aug = jnp.concatenate([k_ref[pl.ds(off, tk), :], et_ref[pl.ds(off, tk), :]], axis=1)
        bias = _near_bias(bias_ref, i, j, t, tk, n_near) if near else b_far
        return _online_softmax_step(_nt_dot(q_aug, k_aug) + bias, v_ref[pl.ds(off, tk), :], carry)

    _, l, acc = _biased_sweep(i, t, tk, step, _softmax_init(t))
    o_ref[...] = (acc / l).astype(o_ref.dtype)


def _diff_kernel(tab_ref, q_ref, k_ref, v_ref, bias_ref, lq_ref, lk_ref, g_ref, o_ref,
                 *, t, tk, n_near, scale, lambda_init):
    h = pl.program_id(0)
    i = pl.program_id(1)
    q = _prescale(q_ref[...], scale * LOG2E)
    lane = lax.broadcasted_iota(jnp.int32, (t, HEAD_DIM), 1)
    zero = jnp.zeros_like(q)
    q2 = jnp.concatenate([jnp.where(lane < DIFF_QK_DIM, q, zero),
                          jnp.where(lane >= DIFF_QK_DIM, q, zero)], axis=0)
    b_far = tab_ref[N_BUCKETS - 1, h] * LOG2E

    def step(j, carry, near):
        k, v, _ = _kv_tile(k_ref, v_ref, j, tk)
        if near:
            b = _near_bias(bias_ref, i, j, t, tk, n_near)
            bias = jnp.concatenate([b, b], axis=0)
        else:
            bias = b_far
        return _online_softmax_step(_nt_dot(q2, k) + bias, v, carry)

    _, l, acc = _biased_sweep(i, t, tk, step, _softmax_init(2 * t))
    o12 = acc / l
    lam_e = jnp.exp(jnp.sum(lq_ref[...] * lk_ref[...], axis=-1, keepdims=True))
    lam = lam_e[0:1, :] - lam_e[1:2, :] + lambda_init
    o = o12[:t, :] - lam * o12[t:, :]
    o = o * lax.rsqrt(jnp.mean(o * o, axis=-1, keepdims=True) + LN_EPS) * g_ref[...]
    o_ref[...] = (o * (1.0 - lambda_init)).astype(o_ref.dtype)


def _sb_kernel(q_ref, k_ref, v_ref, o_ref, *, t, scale):
    i = pl.program_id(1)
    q = q_ref[...]
    rows = lax.broadcasted_iota(jnp.int32, (t, t), 0)
    cols = lax.broadcasted_iota(jnp.int32, (t, t), 1)
    later = jnp.where(rows > cols, 1.0, 0.0).astype(BF16)

    def block(j, rem, acc, diag):
        k, v, _ = _kv_tile(k_ref, v_ref, j, t)
        z = _nt_dot(q, k) * scale
        soft = jnp.log1p(jnp.exp(-jnp.abs(z)))
        log_sig = jnp.minimum(z, 0.0) - soft
        log_rem = jnp.minimum(-z, 0.0) - soft
        if diag:
            mask = cols < rows
            log_rem = jnp.where(mask, log_rem, 0.0)
        hi, lo = _split_bf16(log_rem)
        after = (jnp.dot(hi, later, preferred_element_type=F32)
                 + jnp.dot(lo, later, preferred_element_type=F32))
        a = jnp.exp(log_sig + after + rem)
        if diag:
            a = jnp.where(mask, a, 0.0)
        acc = acc + jnp.dot(a.astype(BF16), v, preferred_element_type=F32)
        rem = rem + jnp.sum(log_rem, axis=-1, keepdims=True)
        return rem, acc

    rem, acc = block(i, jnp.zeros((t, 1), F32), jnp.zeros((t, HEAD_DIM), F32), True)

    def live(rem):
        return (jnp.max(rem) > SB_CUTOFF).astype(jnp.int32)

    def cond(c):
        j, go, _, _ = c
        return jnp.logical_and(j >= 0, go > 0)

    def body(c):
        j, _, rem, acc = c
        rem, acc = block(j, rem, acc, False)
        return j - 1, live(rem), rem, acc

    _, _, _, acc = lax.while_loop(cond, body, (i - 1, live(rem), rem, acc))
    o_ref[...] = acc.astype(o_ref.dtype)


def _attention_call(kernel, h, out_cols, col0, extra_in, extra_specs, scratch, name, t):
    s = h.shape[0]
    nh = N_HEADS_GROUP
    in_specs = list(extra_specs[0]) + [
        pl.BlockSpec((t, HEAD_DIM), lambda hd, i: (i, col0 + hd)),
        pl.BlockSpec((s, HEAD_DIM), lambda hd, i: (0, col0 + nh + hd)),
        pl.BlockSpec((s, HEAD_DIM), lambda hd, i: (0, col0 + 2 * nh + hd)),
    ] + list(extra_specs[1])
    args = list(extra_in[0]) + [h, h, h] + list(extra_in[1])
    return pl.pallas_call(
        kernel,
        out_shape=jax.ShapeDtypeStruct((s, out_cols), BF16),
        grid=(nh, s // t),
        in_specs=in_specs,
        out_specs=pl.BlockSpec((t, HEAD_DIM), lambda hd, i: (i, hd)),
        scratch_shapes=scratch,
        compiler_params=_cparams(2, VMEM_LIMIT),
        name=name,
    )(*args)


def _router_kernel(x_ref, w_ref, b_ref, o_ref):
    logits = jnp.dot(x_ref[...], w_ref[...], preferred_element_type=F32,
                     precision=lax.Precision.HIGHEST) + b_ref[...]
    tm, n = logits.shape
    lane = lax.broadcasted_iota(jnp.int32, (tm, n), 1)
    real = lane < N_EXPERTS
    logits = jnp.where(real, logits, -jnp.inf)
    e = jnp.exp(logits - jnp.max(logits, axis=-1, keepdims=True))
    aff = e / jnp.sum(e, axis=-1, keepdims=True)
    group = lane // EXPERTS_PER_GROUP
    best = jnp.full((tm, 1), -jnp.inf, F32)
    g_sel = jnp.zeros((tm, 1), jnp.int32)
    for g in range(N_GROUPS):
        gmax = jnp.max(jnp.where(group == g, aff, -jnp.inf), axis=-1, keepdims=True)
        better = gmax > best
        g_sel = jnp.where(better, g, g_sel)
        best = jnp.where(better, gmax, best)
    in_group = jnp.logical_and(group == g_sel, real)
    cand = jnp.where(in_group, aff, -jnp.inf)
    v1 = jnp.max(cand, axis=-1, keepdims=True)
    i1 = jnp.min(jnp.where(cand == v1, lane, n), axis=-1, keepdims=True)
    cand2 = jnp.where(lane == i1, -jnp.inf, cand)
    v2 = jnp.max(cand2, axis=-1, keepdims=True)
    i2 = jnp.min(jnp.where(cand2 == v2, lane, n), axis=-1, keepdims=True)
    tot = v1 + v2
    o_ref[...] = jnp.where(lane == 0, v1 / tot,
                           jnp.where(lane == 1, v2 / tot,
                                     jnp.where(lane == 2, i1.astype(F32),
                                               jnp.where(lane == 3, i2.astype(F32), 0.0))))


def _router(x, w_router, b_router, tm):
    s, d = x.shape
    w = jnp.zeros((d, LANES), F32).at[:, :N_EXPERTS].set(w_router)
    b = jnp.zeros((1, LANES), F32).at[0, :N_EXPERTS].set(b_router)
    return pl.pallas_call(
        _router_kernel,
        out_shape=jax.ShapeDtypeStruct((s, LANES), F32),
        grid=(s // tm,),
        in_specs=[pl.BlockSpec((tm, d), lambda i: (i, 0)),
                  pl.BlockSpec((d, LANES), lambda i: (0, 0)),
                  pl.BlockSpec((1, LANES), lambda i: (0, 0))],
        out_specs=pl.BlockSpec((tm, LANES), lambda i: (i, 0)),
        compiler_params=_cparams(1, VMEM_LIMIT),
        name="router",
    )(x, w, b)


def _dispatch_plan(route, tm):
    s = route.shape[0]
    n_tiles = (MOE_TOP_K * s) // tm + N_EXPERTS
    expert = route[:, 2:2 + MOE_TOP_K].astype(jnp.int32).reshape(-1)
    onehot = (expert[:, None] == jnp.arange(N_EXPERTS, dtype=jnp.int32)[None, :]).astype(jnp.int32)
    running = jnp.cumsum(onehot, axis=0)
    rank = jnp.take_along_axis(running, expert[:, None], axis=1)[:, 0] - 1
    padded = -(-running[-1] // tm) * tm
    seg_end = jnp.cumsum(padded)
    pair_row = (seg_end - padded)[expert] + rank
    tile_start = jnp.arange(n_tiles, dtype=jnp.int32) * tm
    tile_expert = jnp.minimum(jnp.searchsorted(seg_end, tile_start, side="right"),
                              N_EXPERTS - 1).astype(jnp.int32)
    tile_live = (tile_start < seg_end[-1]).astype(jnp.int32)
    row_token = jnp.zeros((n_tiles * tm,), jnp.int32).at[pair_row].set(
        jnp.arange(MOE_TOP_K * s, dtype=jnp.int32) // MOE_TOP_K)
    return pair_row.astype(jnp.int32), row_token, tile_expert, tile_live


def _row_gather_kernel(idx_ref, src_ref, dst_ref, sem, *, chunk):
    base = pl.program_id(0) * chunk

    def issue(r, _):
        pltpu.make_async_copy(src_ref.at[pl.ds(idx_ref[0, r], 1)],
                              dst_ref.at[pl.ds(base + r, 1)], sem).start()
        return 0

    lax.fori_loop(0, chunk, issue, 0, unroll=8)
    pltpu.make_async_copy(src_ref.at[pl.ds(0, chunk)], dst_ref.at[pl.ds(base, chunk)], sem).wait()


def _row_gather(src, idx, chunk):
    n = idx.shape[0]
    return pl.pallas_call(
        functools.partial(_row_gather_kernel, chunk=chunk),
        out_shape=jax.ShapeDtypeStruct((n, src.shape[1]), src.dtype),
        grid=(n // chunk,),
        in_specs=[pl.BlockSpec((None, 1, chunk), lambda c: (c, 0, 0), memory_space=pltpu.SMEM),
                  pl.BlockSpec(memory_space=pl.ANY)],
        out_specs=pl.BlockSpec(memory_space=pl.ANY),
        scratch_shapes=[pltpu.SemaphoreType.DMA(())],
        compiler_params=pltpu.CompilerParams(dimension_semantics=("arbitrary",),
                                             has_side_effects=True),
        name="row_gather",
    )(idx.reshape(n // chunk, 1, chunk), src)


def _moe_group_kernel(te_ref, live_ref, x_ref, wg_ref, wu_ref, wd_ref, o_ref, xb_ref):
    g = pl.program_id(0)
    f = pl.program_id(1)
    live = live_ref[g] > 0

    @pl.when(jnp.logical_and(live, f == 0))
    def _():
        xb_ref[...] = x_ref[...].astype(BF16)

    @pl.when(jnp.logical_and(jnp.logical_not(live), f == 0))
    def _():
        o_ref[...] = jnp.zeros(o_ref.shape, F32)

    @pl.when(live)
    def _():
        xb = xb_ref[...]
        a = jnp.dot(xb, wg_ref[...], preferred_element_type=F32)
        u = jnp.dot(xb, wu_ref[...], preferred_element_type=F32)
        hid = ((a * jax.nn.sigmoid(a)) * u).astype(BF16)
        y = jnp.dot(hid, wd_ref[...], preferred_element_type=F32)

        @pl.when(f == 0)
        def _():
            o_ref[...] = y

        @pl.when(f > 0)
        def _():
            o_ref[...] += y


def _moe_group_ffn(xs, tile_expert, tile_live, wg, wu, wd, tm, tf):
    rows, d = xs.shape
    fdim = wg.shape[2]
    grid_spec = pltpu.PrefetchScalarGridSpec(
        num_scalar_prefetch=2,
        grid=(rows // tm, fdim // tf),
        in_specs=[pl.BlockSpec((tm, d), lambda g, f, te, lv: (g, 0)),
                  pl.BlockSpec((None, d, tf), lambda g, f, te, lv: (te[g], 0, f)),
                  pl.BlockSpec((None, d, tf), lambda g, f, te, lv: (te[g], 0, f)),
                  pl.BlockSpec((None, tf, d), lambda g, f, te, lv: (te[g], f, 0))],
        out_specs=pl.BlockSpec((tm, d), lambda g, f, te, lv: (g, 0)),
        scratch_shapes=[pltpu.VMEM((tm, d), BF16)])
    return pl.pallas_call(
        _moe_group_kernel,
        out_shape=jax.ShapeDtypeStruct((rows, d), F32),
        grid_spec=grid_spec,
        compiler_params=_cparams(2, VMEM_LIMIT),
        name="moe_group_ffn",
    )(tile_expert, tile_live, xs, wg, wu, wd)


def _moe_combine_ln_kernel(x_ref, r_ref, y0_ref, y1_ref, g_ref, b_ref, o_ref):
    r = r_ref[...]
    ff = r[:, 0:1] * y0_ref[...] + r[:, 1:2] * y1_ref[...]
    o_ref[...] = _layer_norm(ALPHA * x_ref[...] + ff, g_ref[...], b_ref[...])


def _moe_combine_ln(x, route, y, g, b, tm):
    s, d = x.shape
    nt = s // tm
    return pl.pallas_call(
        _moe_combine_ln_kernel,
        out_shape=jax.ShapeDtypeStruct((s, d), F32),
        grid=(nt,),
        in_specs=[pl.BlockSpec((tm, d), lambda i: (i, 0)),
                  pl.BlockSpec((tm, LANES), lambda i: (i, 0)),
                  pl.BlockSpec((tm, d), lambda i: (i, 0)),
                  pl.BlockSpec((tm, d), lambda i: (i + nt, 0)),
                  pl.BlockSpec((1, d), lambda i: (0, 0)),
                  pl.BlockSpec((1, d), lambda i: (0, 0))],
        out_specs=pl.BlockSpec((tm, d), lambda i: (i, 0)),
        compiler_params=_cparams(1, VMEM_LIMIT),
        name="moe_combine_ln",
    )(x, route, y, y, g.reshape(1, d), b.reshape(1, d))


def _moe_ln(x, route, wg, wu, wd, g, b, tm):
    s = x.shape[0]
    pair_row, row_token, tile_expert, tile_live = _dispatch_plan(route, tm)
    xs = _row_gather(x, row_token, MOE_GATHER_CHUNK)
    ys = _moe_group_ffn(xs, tile_expert, tile_live, wg, wu, wd, tm, D_EXPERT // 2)
    slot_major = pair_row.reshape(s, MOE_TOP_K).T.reshape(-1)
    y = _row_gather(ys, slot_major, MOE_GATHER_CHUNK)
    return _moe_combine_ln(x, route, y, g, b, tm)


def kernel(x, w_in_even, w_out_even, forget_bias, w_in_odd, w_out_odd, lambda_q, lambda_k,
           subln_gain, rel_bias, w_router, b_router, w_gate, w_up, w_down, ln_gain, ln_bias):
    bsz, s, d = x.shape
    assert bsz == 1 and d == D_MODEL and s % MOBA_BLOCK == 0
    t = ATT_TILE
    nh = N_HEADS_GROUP
    gw = GROUP_WIDTH
    tm_proj = min(1024, s)
    tm_row = min(512, s)
    tk = min(ATT_KEYS, s)
    n_near = _near_tile_count(t)
    assert s % tk == 0 and tk // t + 1 >= n_near
    nblk = s // MOBA_BLOCK
    nbp = -(-nblk // LANES) * LANES

    x2 = x.reshape(s, d)
    bias_tiles = _bias_tiles(rel_bias, t, n_near)
    tab_spec = pl.BlockSpec(memory_space=pltpu.SMEM)
    bias_spec = pl.BlockSpec((None, n_near + 2, t, t), lambda hd, i: (hd, 0, 0, 0))
    block_onehot = (jnp.arange(s, dtype=jnp.int32)[:, None] // MOBA_BLOCK
                    == jnp.arange(nbp, dtype=jnp.int32)[None, :]).astype(BF16)

    for layer in range(DEPTH):
        li = layer // 2
        if layer % 2 == 0:
            w = w_in_even[li]
            fcol = 3 * gw
            w_main = jnp.concatenate([w[:, :fcol], w[:, fcol + nh:]], axis=1).astype(BF16)
            w_f = jnp.zeros((d, LANES), BF16).at[:, :nh].set(w[:, fcol:fcol + nh].astype(BF16))
            h = _in_proj(x2, w_main, BF16, tm_proj, 512)
            f = _in_proj(x2, w_f, F32, tm_proj, LANES)
            c = _decay_cumsum(f[:, :nh].T, forget_bias[li])
            oa = _attention_call(
                functools.partial(_fox_kernel, t=FOX_TQ, tk=min(FOX_TK, s), scale=HEAD_DIM ** -0.5), h, gw, 0,
                ([], [c.reshape(nh, 1, s)]),
                ([], [pl.BlockSpec((None, 1, s), lambda hd, i: (hd, 0, 0))]),
                [], "fox_attention", FOX_TQ)
            ob = _attention_call(
                functools.partial(_moba_kernel, t=t, tk=tk, nblk=nblk, n_near=n_near,
                                  scale=HEAD_DIM ** -0.5),
                h, gw, 3 * nh,
                ([rel_bias], [bias_tiles, block_onehot]),
                ([tab_spec], [bias_spec, pl.BlockSpec((s, nbp), lambda hd, i: (0, 0))]),
                [pltpu.VMEM((nbp, HEAD_DIM), F32)], "moba_attention", t)
            w_out = w_out_even[li]
        else:
            lambda_init = 0.8 - 0.6 * math.exp(-0.3 * layer)
            h = _in_proj(x2, w_in_odd[li].astype(BF16), BF16, tm_proj, 512)
            small = lambda shape: pl.BlockSpec(shape, lambda hd, i: (0, 0))
            oa = _attention_call(
                functools.partial(_diff_kernel, t=t, tk=tk, n_near=n_near, scale=DIFF_QK_DIM ** -0.5,
                                  lambda_init=lambda_init),
                h, gw, 0,
                ([rel_bias], [bias_tiles, lambda_q[li], lambda_k[li], subln_gain[li].reshape(1, HEAD_DIM)]),
                ([tab_spec], [bias_spec, small((2, DIFF_QK_DIM)), small((2, DIFF_QK_DIM)),
                              small((1, HEAD_DIM))]),
                [], "diff_attention", t)
            ob = _attention_call(
                functools.partial(_sb_kernel, t=t, scale=HEAD_DIM ** -0.5), h, gw, 3 * nh,
                ([], []), ([], []), [], "stickbreak_attention", t)
            w_out = w_out_odd[li]
        x2 = _out_proj_ln(oa, ob, w_out.astype(BF16), x2, ln_gain[layer, 0], ln_bias[layer, 0], tm_row)
        route = _router(x2, w_router, b_router, tm_row)
        x2 = _moe_ln(x2, route, w_gate[layer].astype(BF16), w_up[layer].astype(BF16),
                     w_down[layer].astype(BF16), ln_gain[layer, 1], ln_bias[layer, 1], tm_row)
    return x2.reshape(bsz, s, d)
```

```python
import functools
import math

import numpy as np
import jax
import jax.numpy as jnp
from jax import lax
from jax.experimental import pallas as pl
from jax.experimental.pallas import tpu as pltpu

F32 = jnp.float32
BF16 = jnp.bfloat16

D_MODEL = 2048
DEPTH = 2
HEAD_DIM = 128
N_HEADS_GROUP = 8
GROUP_WIDTH = N_HEADS_GROUP * HEAD_DIM
DIFF_QK_DIM = HEAD_DIM // 2
MOBA_BLOCK = 256
MOBA_TOPK = 3
N_BUCKETS = 32
MAX_EXACT = N_BUCKETS // 2
MAX_DISTANCE = 1024
N_EXPERTS = 16
N_GROUPS = 4
EXPERTS_PER_GROUP = N_EXPERTS // N_GROUPS
MOE_TOP_K = 2
MOE_GATHER_CHUNK = 512
D_EXPERT = D_MODEL // 2
ALPHA = (2.0 * DEPTH) ** 0.25
LN_EPS = 1e-5
NEG_INF = -1e30
LOG2E = math.log2(math.e)
LANES = 128
ATT_TILE = 256
ATT_KEYS = 1024
FOX_TQ = 512
FOX_TK = 1024
SB_CUTOFF = -104.0
VMEM_LIMIT = 56 * 1024 * 1024


def _cparams(n_axes, vmem=None):
    return pltpu.CompilerParams(dimension_semantics=("arbitrary",) * n_axes,
                                vmem_limit_bytes=vmem)


def _nt_dot(a, b):
    return lax.dot_general(a, b, (((1,), (1,)), ((), ())), preferred_element_type=F32)


def _split_bf16(x):
    hi = x.astype(BF16)
    lo = (x - hi.astype(F32)).astype(BF16)
    return hi, lo


def _layer_norm(z, g, b):
    mu = jnp.mean(z, axis=-1, keepdims=True)
    zc = z - mu
    var = jnp.mean(zc * zc, axis=-1, keepdims=True)
    return zc * lax.rsqrt(var + LN_EPS) * g + b


def _inproj_kernel(x_ref, w_ref, o_ref, xb_ref):
    @pl.when(pl.program_id(1) == 0)
    def _():
        xb_ref[...] = x_ref[...].astype(BF16)

    o_ref[...] = jnp.dot(xb_ref[...], w_ref[...], preferred_element_type=F32).astype(o_ref.dtype)


def _in_proj(x, w, out_dtype, tm, tn):
    s, d = x.shape
    n = w.shape[1]
    return pl.pallas_call(
        _inproj_kernel,
        out_shape=jax.ShapeDtypeStruct((s, n), out_dtype),
        grid=(s // tm, n // tn),
        in_specs=[pl.BlockSpec((tm, d), lambda i, j: (i, 0)),
                  pl.BlockSpec((d, tn), lambda i, j: (0, j))],
        out_specs=pl.BlockSpec((tm, tn), lambda i, j: (i, j)),
        scratch_shapes=[pltpu.VMEM((tm, d), BF16)],
        compiler_params=_cparams(2, VMEM_LIMIT),
        name="in_proj",
    )(x, w)


def _outproj_ln_kernel(oa_ref, ob_ref, wa_ref, wb_ref, x_ref, g_ref, b_ref, o_ref):
    y = jnp.dot(oa_ref[...], wa_ref[...], preferred_element_type=F32)
    y = y + jnp.dot(ob_ref[...], wb_ref[...], preferred_element_type=F32)
    o_ref[...] = _layer_norm(ALPHA * x_ref[...] + y, g_ref[...], b_ref[...])


def _out_proj_ln(oa, ob, w, x, g, b, tm):
    s, d = x.shape
    gw = oa.shape[1]
    return pl.pallas_call(
        _outproj_ln_kernel,
        out_shape=jax.ShapeDtypeStruct((s, d), F32),
        grid=(s // tm,),
        in_specs=[pl.BlockSpec((tm, gw), lambda i: (i, 0)),
                  pl.BlockSpec((tm, gw), lambda i: (i, 0)),
                  pl.BlockSpec((gw, d), lambda i: (0, 0)),
                  pl.BlockSpec((gw, d), lambda i: (1, 0)),
                  pl.BlockSpec((tm, d), lambda i: (i, 0)),
                  pl.BlockSpec((1, d), lambda i: (0, 0)),
                  pl.BlockSpec((1, d), lambda i: (0, 0))],
        out_specs=pl.BlockSpec((tm, d), lambda i: (i, 0)),
        compiler_params=_cparams(1, VMEM_LIMIT),
        name="out_proj_ln",
    )(oa, ob, w, w, x, g.reshape(1, d), b.reshape(1, d))


def _decay_cumsum_kernel(f_ref, b_ref, c_ref):
    nh, s = f_ref.shape
    rows = lax.broadcasted_iota(jnp.int32, (LANES, LANES), 0)
    cols = lax.broadcasted_iota(jnp.int32, (LANES, LANES), 1)
    upper = jnp.where(rows <= cols, 1.0, 0.0).astype(F32)
    bias = b_ref[...]

    def body(n, carry):
        off = pl.multiple_of(n * LANES, LANES)
        z = f_ref[:, pl.ds(off, LANES)] + bias
        logf = jnp.minimum(z, 0.0) - jnp.log1p(jnp.exp(-jnp.abs(z)))
        c = jnp.dot(logf, upper, preferred_element_type=F32,
                    precision=lax.Precision.HIGHEST) + carry
        c_ref[:, pl.ds(off, LANES)] = c * LOG2E
        return carry + jnp.sum(logf, axis=-1, keepdims=True)

    lax.fori_loop(0, s // LANES, body, jnp.zeros((nh, 1), F32))


def _decay_cumsum(f_t, bias):
    nh, s = f_t.shape
    return pl.pallas_call(
        _decay_cumsum_kernel,
        out_shape=jax.ShapeDtypeStruct((nh, s), F32),
        name="decay_cumsum",
    )(f_t, bias.reshape(nh, 1))


def _t5_bucket_np(dist):
    n = np.maximum(dist, 0)
    nf = np.maximum(n, 1).astype(np.float32)
    ratio = np.log(nf / np.float32(MAX_EXACT)) / np.float32(math.log(MAX_DISTANCE / MAX_EXACT))
    large = MAX_EXACT + (ratio.astype(np.float32) * np.float32(N_BUCKETS - MAX_EXACT)).astype(np.int32)
    large = np.minimum(large, N_BUCKETS - 1)
    return np.where(n < MAX_EXACT, n, large).astype(np.int32)


def _near_tile_count(t):
    d = np.arange(0, 4 * MAX_DISTANCE, dtype=np.int64)
    not_last = np.nonzero(_t5_bucket_np(d) != N_BUCKETS - 1)[0]
    d_sat = int(not_last.max()) + 1
    n = 1
    while (n - 1) * t + 1 < d_sat:
        n += 1
    return n


def _bucket_tiles_np(t, n_near):
    r = np.arange(t)[:, None]
    c = np.arange(t)[None, :]
    tiles = []
    for delta in range(n_near):
        dist = delta * t + r - c
        tiles.append(np.where(dist >= 0, _t5_bucket_np(dist), -1))
    tiles.append(np.full((t, t), N_BUCKETS - 1))
    tiles.append(np.full((t, t), -1))
    return np.stack(tiles).astype(np.int32)


def _bias_tiles_kernel(tab_ref, idx_ref, o_ref):
    h = pl.program_id(0)
    idx = idx_ref[...]
    out = jnp.full(idx.shape, NEG_INF, F32)
    for b in range(N_BUCKETS):
        out = jnp.where(idx == b, tab_ref[b, h] * LOG2E, out)
    o_ref[...] = out


def _bias_tiles(rel_bias, t, n_near):
    nh = rel_bias.shape[1]
    idx = jnp.asarray(_bucket_tiles_np(t, n_near))
    n_tiles = idx.shape[0]
    return pl.pallas_call(
        _bias_tiles_kernel,
        out_shape=jax.ShapeDtypeStruct((nh, n_tiles, t, t), F32),
        grid=(nh, n_tiles),
        in_specs=[pl.BlockSpec(memory_space=pltpu.SMEM),
                  pl.BlockSpec((None, t, t), lambda h, n: (n, 0, 0))],
        out_specs=pl.BlockSpec((None, None, t, t), lambda h, n: (h, n, 0, 0)),
        compiler_params=_cparams(2),
        name="bias_tiles",
    )(rel_bias, idx)


def _online_softmax_step(s, v, carry):
    m, l, acc = carry
    m_new = jnp.maximum(m, jnp.max(s, axis=-1, keepdims=True))
    alpha = jnp.exp2(m - m_new)
    p = jnp.exp2(s - m_new)
    l = alpha * l + jnp.sum(p, axis=-1, keepdims=True)
    acc = alpha * acc + jnp.dot(p.astype(BF16), v, preferred_element_type=F32)
    return m_new, l, acc


def _prescale(q, c):
    return (q.astype(F32) * c).astype(BF16)


def _near_bias(bias_ref, i, j, t, tk, n_near):
    nsub = tk // t
    tiles = []
    for b in range(nsub):
        delta = i - (j * nsub + b)
        tiles.append(bias_ref[jnp.where(delta < 0, n_near + 1, jnp.minimum(delta, n_near))])
    return jnp.concatenate(tiles, axis=1)


def _biased_sweep(i, t, tk, step, init):
    jd = (i * t) // tk
    carry = step(jd, init, True)
    lo = jnp.maximum(jd - 1, 0)
    carry = lax.fori_loop(lo, jd, lambda j, c: step(j, c, True), carry)
    return lax.fori_loop(0, lo, lambda j, c: step(j, c, False), carry)


def _softmax_init(rows):
    return (jnp.full((rows, 1), NEG_INF, F32), jnp.zeros((rows, 1), F32),
            jnp.zeros((rows, HEAD_DIM), F32))


def _kv_tile(k_ref, v_ref, j, t):
    off = pl.multiple_of(j * t, t)
    return k_ref[pl.ds(off, t), :], v_ref[pl.ds(off, t), :], off


def _fox_kernel(q_ref, k_ref, v_ref, c_ref, o_ref, *, t, tk, scale):
    i = pl.program_id(1)
    q = _prescale(q_ref[...], scale * LOG2E)
    rows = lax.broadcasted_iota(jnp.int32, (t, tk), 0)
    cols = lax.broadcasted_iota(jnp.int32, (t, tk), 1)

    def step(j, carry, diag):
        k, v, off = _kv_tile(k_ref, v_ref, j, tk)
        s = _nt_dot(q, k) - c_ref[:, pl.ds(off, tk)]
        if diag:
            s = jnp.where(cols <= rows + (i * t - off), s, NEG_INF)
        return _online_softmax_step(s, v, carry)

    jd = (i * t) // tk
    carry = step(jd, _softmax_init(t), True)
    _, l, acc = lax.fori_loop(0, jd, lambda j, c: step(j, c, False), carry)
    o_ref[...] = (acc / l).astype(o_ref.dtype)


def _moba_kernel(tab_ref, q_ref, k_ref, v_ref, bias_ref, et_ref, o_ref, kmean_ref,
                 *, t, tk, nblk, n_near, scale):
    h = pl.program_id(0)
    i = pl.program_id(1)
    nbp = kmean_ref.shape[0]

    @pl.when(i == 0)
    def _():
        kmean_ref[...] = jnp.zeros(kmean_ref.shape, F32)

        def mean_body(n, _):
            off = pl.multiple_of(n * t, t)
            kb = k_ref[pl.ds(off, t), :].astype(F32)
            kmean_ref[pl.ds(n, 1), :] = jnp.sum(kb, axis=0, keepdims=True) * (1.0 / t)
            return 0

        lax.fori_loop(0, nblk, mean_body, 0)

    q = q_ref[...]
    km_hi, km_lo = _split_bf16(kmean_ref[...])
    gate = _nt_dot(q, km_hi) + _nt_dot(q, km_lo)
    blk = lax.broadcasted_iota(jnp.int32, (t, nbp), 1)
    eligible = blk < i
    sel = jnp.zeros((t, nbp), F32)
    for r in range(MOBA_TOPK):
        cand = jnp.where(eligible, jnp.where(sel > 0.0, -jnp.inf, gate), -jnp.inf)
        vmax = jnp.max(cand, axis=-1, keepdims=True)
        first = jnp.min(jnp.where(cand == vmax, blk, nbp), axis=-1, keepdims=True)
        take = jnp.where(r < i, 1.0, 0.0)
        sel = jnp.maximum(sel, jnp.where(blk == first, take, 0.0))

    sel = jnp.maximum(sel, jnp.where(blk == i, 1.0, 0.0))
    penalty = jnp.where(sel > 0.0, 0.0, NEG_INF).astype(BF16)
    q_aug = jnp.concatenate([_prescale(q, scale * LOG2E), penalty], axis=1)
    b_far = tab_ref[N_BUCKETS - 1, h] * LOG2E

    def step(j, carry, near):
        off = pl.multiple_of(j * tk, tk)
        k_aug = jnp.concatenate([k_ref[pl.ds(off, tk), :], et_ref[pl.ds(off, tk), :]], axis=1)
        bias = _near_bias(bias_ref, i, j, t, tk, n_near) if near else b_far
        return _online_softmax_step(_nt_dot(q_aug, k_aug) + bias, v_ref[pl.ds(off, tk), :], carry)

    _, l, acc = _biased_sweep(i, t, tk, step, _softmax_init(t))
    o_ref[...] = (acc / l).astype(o_ref.dtype)


def _diff_kernel(tab_ref, q_ref, k_ref, v_ref, bias_ref, lq_ref, lk_ref, g_ref, o_ref,
                 *, t, tk, n_near, scale, lambda_init):
    h = pl.program_id(0)
    i = pl.program_id(1)
    q = _prescale(q_ref[...], scale * LOG2E)
    lane = lax.broadcasted_iota(jnp.int32, (t, HEAD_DIM), 1)
    zero = jnp.zeros_like(q)
    q2 = jnp.concatenate([jnp.where(lane < DIFF_QK_DIM, q, zero),
                          jnp.where(lane >= DIFF_QK_DIM, q, zero)], axis=0)
    b_far = tab_ref[N_BUCKETS - 1, h] * LOG2E

    def step(j, carry, near):
        k, v, _ = _kv_tile(k_ref, v_ref, j, tk)
        if near:
            b = _near_bias(bias_ref, i, j, t, tk, n_near)
            bias = jnp.concatenate([b, b], axis=0)
        else:
            bias = b_far
        return _online_softmax_step(_nt_dot(q2, k) + bias, v, carry)

    _, l, acc = _biased_sweep(i, t, tk, step, _softmax_init(2 * t))
    o12 = acc / l
    lam_e = jnp.exp(jnp.sum(lq_ref[...] * lk_ref[...], axis=-1, keepdims=True))
    lam = lam_e[0:1, :] - lam_e[1:2, :] + lambda_init
    o = o12[:t, :] - lam * o12[t:, :]
    o = o * lax.rsqrt(jnp.mean(o * o, axis=-1, keepdims=True) + LN_EPS) * g_ref[...]
    o_ref[...] = (o * (1.0 - lambda_init)).astype(o_ref.dtype)


def _sb_kernel(q_ref, k_ref, v_ref, o_ref, *, t, scale):
    i = pl.program_id(1)
    q = q_ref[...]
    rows = lax.broadcasted_iota(jnp.int32, (t, t), 0)
    cols = lax.broadcasted_iota(jnp.int32, (t, t), 1)
    later = jnp.where(rows > cols, 1.0, 0.0).astype(BF16)

    def block(j, rem, acc, diag):
        k, v, _ = _kv_tile(k_ref, v_ref, j, t)
        z = _nt_dot(q, k) * scale
        soft = jnp.log1p(jnp.exp(-jnp.abs(z)))
        log_sig = jnp.minimum(z, 0.0) - soft
        log_rem = jnp.minimum(-z, 0.0) - soft
        if diag:
            mask = cols < rows
            log_rem = jnp.where(mask, log_rem, 0.0)
        hi, lo = _split_bf16(log_rem)
        after = (jnp.dot(hi, later, preferred_element_type=F32)
                 + jnp.dot(lo, later, preferred_element_type=F32))
        a = jnp.exp(log_sig + after + rem)
        if diag:
            a = jnp.where(mask, a, 0.0)
        acc = acc + jnp.dot(a.astype(BF16), v, preferred_element_type=F32)
        rem = rem + jnp.sum(log_rem, axis=-1, keepdims=True)
        return rem, acc

    rem, acc = block(i, jnp.zeros((t, 1), F32), jnp.zeros((t, HEAD_DIM), F32), True)

    def live(rem):
        return (jnp.max(rem) > SB_CUTOFF).astype(jnp.int32)

    def cond(c):
        j, go, _, _ = c
        return jnp.logical_and(j >= 0, go > 0)

    def body(c):
        j, _, rem, acc = c
        rem, acc = block(j, rem, acc, False)
        return j - 1, live(rem), rem, acc

    _, _, _, acc = lax.while_loop(cond, body, (i - 1, live(rem), rem, acc))
    o_ref[...] = acc.astype(o_ref.dtype)


def _attention_call(kernel, h, out_cols, col0, extra_in, extra_specs, scratch, name, t):
    s = h.shape[0]
    nh = N_HEADS_GROUP
    in_specs = list(extra_specs[0]) + [
        pl.BlockSpec((t, HEAD_DIM), lambda hd, i: (i, col0 + hd)),
        pl.BlockSpec((s, HEAD_DIM), lambda hd, i: (0, col0 + nh + hd)),
        pl.BlockSpec((s, HEAD_DIM), lambda hd, i: (0, col0 + 2 * nh + hd)),
    ] + list(extra_specs[1])
    args = list(extra_in[0]) + [h, h, h] + list(extra_in[1])
    return pl.pallas_call(
        kernel,
        out_shape=jax.ShapeDtypeStruct((s, out_cols), BF16),
        grid=(nh, s // t),
        in_specs=in_specs,
        out_specs=pl.BlockSpec((t, HEAD_DIM), lambda hd, i: (i, hd)),
        scratch_shapes=scratch,
        compiler_params=_cparams(2, VMEM_LIMIT),
        name=name,
    )(*args)


def _router_kernel(x_ref, w_ref, b_ref, o_ref):
    logits = jnp.dot(x_ref[...], w_ref[...], preferred_element_type=F32,
                     precision=lax.Precision.HIGHEST) + b_ref[...]
    tm, n = logits.shape
    lane = lax.broadcasted_iota(jnp.int32, (tm, n), 1)
    real = lane < N_EXPERTS
    logits = jnp.where(real, logits, -jnp.inf)
    e = jnp.exp(logits - jnp.max(logits, axis=-1, keepdims=True))
    aff = e / jnp.sum(e, axis=-1, keepdims=True)
    group = lane // EXPERTS_PER_GROUP
    best = jnp.full((tm, 1), -jnp.inf, F32)
    g_sel = jnp.zeros((tm, 1), jnp.int32)
    for g in range(N_GROUPS):
        gmax = jnp.max(jnp.where(group == g, aff, -jnp.inf), axis=-1, keepdims=True)
        better = gmax > best
        g_sel = jnp.where(better, g, g_sel)
        best = jnp.where(better, gmax, best)
    in_group = jnp.logical_and(group == g_sel, real)
    cand = jnp.where(in_group, aff, -jnp.inf)
    v1 = jnp.max(cand, axis=-1, keepdims=True)
    i1 = jnp.min(jnp.where(cand == v1, lane, n), axis=-1, keepdims=True)
    cand2 = jnp.where(lane == i1, -jnp.inf, cand)
    v2 = jnp.max(cand2, axis=-1, keepdims=True)
    i2 = jnp.min(jnp.where(cand2 == v2, lane, n), axis=-1, keepdims=True)
    tot = v1 + v2
    o_ref[...] = jnp.where(lane == 0, v1 / tot,
                           jnp.where(lane == 1, v2 / tot,
                                     jnp.where(lane == 2, i1.astype(F32),
                                               jnp.where(lane == 3, i2.astype(F32), 0.0))))


def _router(x, w_router, b_router, tm):
    s, d = x.shape
    w = jnp.zeros((d, LANES), F32).at[:, :N_EXPERTS].set(w_router)
    b = jnp.zeros((1, LANES), F32).at[0, :N_EXPERTS].set(b_router)
    return pl.pallas_call(
        _router_kernel,
        out_shape=jax.ShapeDtypeStruct((s, LANES), F32),
        grid=(s // tm,),
        in_specs=[pl.BlockSpec((tm, d), lambda i: (i, 0)),
                  pl.BlockSpec((d, LANES), lambda i: (0, 0)),
                  pl.BlockSpec((1, LANES), lambda i: (0, 0))],
        out_specs=pl.BlockSpec((tm, LANES), lambda i: (i, 0)),
        compiler_params=_cparams(1, VMEM_LIMIT),
        name="router",
    )(x, w, b)


def _dispatch_plan(route, tm):
    s = route.shape[0]
    n_tiles = (MOE_TOP_K * s) // tm + N_EXPERTS
    expert = route[:, 2:2 + MOE_TOP_K].astype(jnp.int32).reshape(-1)
    onehot = (expert[:, None] == jnp.arange(N_EXPERTS, dtype=jnp.int32)[None, :]).astype(jnp.int32)
    running = jnp.cumsum(onehot, axis=0)
    rank = jnp.take_along_axis(running, expert[:, None], axis=1)[:, 0] - 1
    padded = -(-running[-1] // tm) * tm
    seg_end = jnp.cumsum(padded)
    pair_row = (seg_end - padded)[expert] + rank
    tile_start = jnp.arange(n_tiles, dtype=jnp.int32) * tm
    tile_expert = jnp.minimum(jnp.searchsorted(seg_end, tile_start, side="right"),
                              N_EXPERTS - 1).astype(jnp.int32)
    tile_live = (tile_start < seg_end[-1]).astype(jnp.int32)
    row_token = jnp.zeros((n_tiles * tm,), jnp.int32).at[pair_row].set(
        jnp.arange(MOE_TOP_K * s, dtype=jnp.int32) // MOE_TOP_K)
    return pair_row.astype(jnp.int32), row_token, tile_expert, tile_live


def _row_gather_kernel(idx_ref, src_ref, o_ref, sem, *, chunk):
    def issue(r, _):
        pltpu.make_async_copy(src_ref.at[pl.ds(idx_ref[0, r], 1)],
                              o_ref.at[pl.ds(r, 1)], sem).start()
        return 0

    lax.fori_loop(0, chunk, issue, 0, unroll=8)
    pltpu.make_async_copy(src_ref.at[pl.ds(0, chunk)], o_ref, sem).wait()


def _row_gather(src, idx, chunk):
    n = idx.shape[0]
    d = src.shape[1]
    return pl.pallas_call(
        functools.partial(_row_gather_kernel, chunk=chunk),
        out_shape=jax.ShapeDtypeStruct((n, d), src.dtype),
        grid=(n // chunk,),
        in_specs=[pl.BlockSpec((None, 1, chunk), lambda c: (c, 0, 0), memory_space=pltpu.SMEM),
                  pl.BlockSpec(memory_space=pl.ANY)],
        out_specs=pl.BlockSpec((chunk, d), lambda c: (c, 0)),
        scratch_shapes=[pltpu.SemaphoreType.DMA(())],
        compiler_params=_cparams(1, VMEM_LIMIT),
        name="row_gather",
    )(idx.reshape(n // chunk, 1, chunk), src)


def _moe_group_kernel(te_ref, live_ref, x_ref, wg_ref, wu_ref, wd_ref, o_ref, xb_ref):
    g = pl.program_id(0)
    f = pl.program_id(1)
    live = live_ref[g] > 0

    @pl.when(jnp.logical_and(live, f == 0))
    def _():
        xb_ref[...] = x_ref[...].astype(BF16)

    @pl.when(jnp.logical_and(jnp.logical_not(live), f == 0))
    def _():
        o_ref[...] = jnp.zeros(o_ref.shape, F32)

    @pl.when(live)
    def _():
        xb = xb_ref[...]
        a = jnp.dot(xb, wg_ref[...], preferred_element_type=F32)
        u = jnp.dot(xb, wu_ref[...], preferred_element_type=F32)
        hid = ((a * jax.nn.sigmoid(a)) * u).astype(BF16)
        y = jnp.dot(hid, wd_ref[...], preferred_element_type=F32)

        @pl.when(f == 0)
        def _():
            o_ref[...] = y

        @pl.when(f > 0)
        def _():
            o_ref[...] += y


def _moe_group_ffn(xs, tile_expert, tile_live, wg, wu, wd, tm, tf):
    rows, d = xs.shape
    fdim = wg.shape[2]
    grid_spec = pltpu.PrefetchScalarGridSpec(
        num_scalar_prefetch=2,
        grid=(rows // tm, fdim // tf),
        in_specs=[pl.BlockSpec((tm, d), lambda g, f, te, lv: (g, 0)),
                  pl.BlockSpec((None, d, tf), lambda g, f, te, lv: (te[g], 0, f)),
                  pl.BlockSpec((None, d, tf), lambda g, f, te, lv: (te[g], 0, f)),
                  pl.BlockSpec((None, tf, d), lambda g, f, te, lv: (te[g], f, 0))],
        out_specs=pl.BlockSpec((tm, d), lambda g, f, te, lv: (g, 0)),
        scratch_shapes=[pltpu.VMEM((tm, d), BF16)])
    return pl.pallas_call(
        _moe_group_kernel,
        out_shape=jax.ShapeDtypeStruct((rows, d), F32),
        grid_spec=grid_spec,
        compiler_params=_cparams(2, VMEM_LIMIT),
        name="moe_group_ffn",
    )(tile_expert, tile_live, xs, wg, wu, wd)


def _moe_combine_ln_kernel(x_ref, r_ref, y0_ref, y1_ref, g_ref, b_ref, o_ref):
    r = r_ref[...]
    ff = r[:, 0:1] * y0_ref[...] + r[:, 1:2] * y1_ref[...]
    o_ref[...] = _layer_norm(ALPHA * x_ref[...] + ff, g_ref[...], b_ref[...])


def _moe_combine_ln(x, route, y, g, b, tm):
    s, d = x.shape
    nt = s // tm
    return pl.pallas_call(
        _moe_combine_ln_kernel,
        out_shape=jax.ShapeDtypeStruct((s, d), F32),
        grid=(nt,),
        in_specs=[pl.BlockSpec((tm, d), lambda i: (i, 0)),
                  pl.BlockSpec((tm, LANES), lambda i: (i, 0)),
                  pl.BlockSpec((tm, d), lambda i: (i, 0)),
                  pl.BlockSpec((tm, d), lambda i: (i + nt, 0)),
                  pl.BlockSpec((1, d), lambda i: (0, 0)),
                  pl.BlockSpec((1, d), lambda i: (0, 0))],
        out_specs=pl.BlockSpec((tm, d), lambda i: (i, 0)),
        compiler_params=_cparams(1, VMEM_LIMIT),
        name="moe_combine_ln",
    )(x, route, y, y, g.reshape(1, d), b.reshape(1, d))


def _moe_ln(x, route, wg, wu, wd, g, b, tm):
    s = x.shape[0]
    pair_row, row_token, tile_expert, tile_live = _dispatch_plan(route, tm)
    xs = _row_gather(x, row_token, MOE_GATHER_CHUNK)
    ys = _moe_group_ffn(xs, tile_expert, tile_live, wg, wu, wd, tm, D_EXPERT // 2)
    slot_major = pair_row.reshape(s, MOE_TOP_K).T.reshape(-1)
    y = _row_gather(ys, slot_major, MOE_GATHER_CHUNK)
    return _moe_combine_ln(x, route, y, g, b, tm)


def kernel(x, w_in_even, w_out_even, forget_bias, w_in_odd, w_out_odd, lambda_q, lambda_k,
           subln_gain, rel_bias, w_router, b_router, w_gate, w_up, w_down, ln_gain, ln_bias):
    bsz, s, d = x.shape
    assert bsz == 1 and d == D_MODEL and s % MOBA_BLOCK == 0
    t = ATT_TILE
    nh = N_HEADS_GROUP
    gw = GROUP_WIDTH
    tm_proj = min(1024, s)
    tm_row = min(512, s)
    tk = min(ATT_KEYS, s)
    n_near = _near_tile_count(t)
    assert s % tk == 0 and tk // t + 1 >= n_near
    nblk = s // MOBA_BLOCK
    nbp = -(-nblk // LANES) * LANES

    x2 = x.reshape(s, d)
    bias_tiles = _bias_tiles(rel_bias, t, n_near)
    tab_spec = pl.BlockSpec(memory_space=pltpu.SMEM)
    bias_spec = pl.BlockSpec((None, n_near + 2, t, t), lambda hd, i: (hd, 0, 0, 0))
    block_onehot = (jnp.arange(s, dtype=jnp.int32)[:, None] // MOBA_BLOCK
                    == jnp.arange(nbp, dtype=jnp.int32)[None, :]).astype(BF16)

    for layer in range(DEPTH):
        li = layer // 2
        if layer % 2 == 0:
            w = w_in_even[li]
            fcol = 3 * gw
            w_main = jnp.concatenate([w[:, :fcol], w[:, fcol + nh:]], axis=1).astype(BF16)
            w_f = jnp.zeros((d, LANES), BF16).at[:, :nh].set(w[:, fcol:fcol + nh].astype(BF16))
            h = _in_proj(x2, w_main, BF16, tm_proj, 512)
            f = _in_proj(x2, w_f, F32, tm_proj, LANES)
            c = _decay_cumsum(f[:, :nh].T, forget_bias[li])
            oa = _attention_call(
                functools.partial(_fox_kernel, t=FOX_TQ, tk=min(FOX_TK, s), scale=HEAD_DIM ** -0.5), h, gw, 0,
                ([], [c.reshape(nh, 1, s)]),
                ([], [pl.BlockSpec((None, 1, s), lambda hd, i: (hd, 0, 0))]),
                [], "fox_attention", FOX_TQ)
            ob = _attention_call(
                functools.partial(_moba_kernel, t=t, tk=tk, nblk=nblk, n_near=n_near,
                                  scale=HEAD_DIM ** -0.5),
                h, gw, 3 * nh,
                ([rel_bias], [bias_tiles, block_onehot]),
                ([tab_spec], [bias_spec, pl.BlockSpec((s, nbp), lambda hd, i: (0, 0))]),
                [pltpu.VMEM((nbp, HEAD_DIM), F32)], "moba_attention", t)
            w_out = w_out_even[li]
        else:
            lambda_init = 0.8 - 0.6 * math.exp(-0.3 * layer)
            h = _in_proj(x2, w_in_odd[li].astype(BF16), BF16, tm_proj, 512)
            small = lambda shape: pl.BlockSpec(shape, lambda hd, i: (0, 0))
            oa = _attention_call(
                functools.partial(_diff_kernel, t=t, tk=tk, n_near=n_near, scale=DIFF_QK_DIM ** -0.5,
                                  lambda_init=lambda_init),
                h, gw, 0,
                ([rel_bias], [bias_tiles, lambda_q[li], lambda_k[li], subln_gain[li].reshape(1, HEAD_DIM)]),
                ([tab_spec], [bias_spec, small((2, DIFF_QK_DIM)), small((2, DIFF_QK_DIM)),
                              small((1, HEAD_DIM))]),
                [], "diff_attention", t)
            ob = _attention_call(
                functools.partial(_sb_kernel, t=t, scale=HEAD_DIM ** -0.5), h, gw, 3 * nh,
                ([], []), ([], []), [], "stickbreak_attention", t)
            w_out = w_out_odd[li]
        x2 = _out_proj_ln(oa, ob, w_out.astype(BF16), x2, ln_gain[layer, 0], ln_bias[layer, 0], tm_row)
        route = _router(x2, w_router, b_router, tm_row)
        x2 = _moe_ln(x2, route, w_gate[layer].astype(BF16), w_up[layer].astype(BF16),
                     w_down[layer].astype(BF16), ln_gain[layer, 1], ln_bias[layer, 1], tm_row)
    return x2.reshape(bsz, s, d)
```

```python
import functools
import math

import numpy as np
import jax
import jax.numpy as jnp
from jax import lax
from jax.experimental import pallas as pl
from jax.experimental.pallas import tpu as pltpu

F32 = jnp.float32
BF16 = jnp.bfloat16

D_MODEL = 2048
DEPTH = 2
HEAD_DIM = 128
N_HEADS_GROUP = 8
GROUP_WIDTH = N_HEADS_GROUP * HEAD_DIM
DIFF_QK_DIM = HEAD_DIM // 2
MOBA_BLOCK = 256
MOBA_TOPK = 3
N_BUCKETS = 32
MAX_EXACT = N_BUCKETS // 2
MAX_DISTANCE = 1024
N_EXPERTS = 16
N_GROUPS = 4
EXPERTS_PER_GROUP = N_EXPERTS // N_GROUPS
MOE_TOP_K = 2
MOE_GATHER_CHUNK = 512
D_EXPERT = D_MODEL // 2
ALPHA = (2.0 * DEPTH) ** 0.25
LN_EPS = 1e-5
NEG_INF = -1e30
LOG2E = math.log2(math.e)
LANES = 128
ATT_TILE = 256
ATT_KEYS = 1024
FOX_TQ = 512
FOX_TK = 1024
SB_CUTOFF = -104.0
VMEM_LIMIT = 56 * 1024 * 1024


def _cparams(n_axes, vmem=None):
    return pltpu.CompilerParams(dimension_semantics=("arbitrary",) * n_axes,
                                vmem_limit_bytes=vmem)


def _nt_dot(a, b):
    return lax.dot_general(a, b, (((1,), (1,)), ((), ())), preferred_element_type=F32)


def _split_bf16(x):
    hi = x.astype(BF16)
    lo = (x - hi.astype(F32)).astype(BF16)
    return hi, lo


def _layer_norm(z, g, b):
    mu = jnp.mean(z, axis=-1, keepdims=True)
    zc = z - mu
    var = jnp.mean(zc * zc, axis=-1, keepdims=True)
    return zc * lax.rsqrt(var + LN_EPS) * g + b


def _inproj_kernel(x_ref, w_ref, o_ref, xb_ref):
    @pl.when(pl.program_id(1) == 0)
    def _():
        xb_ref[...] = x_ref[...].astype(BF16)

    o_ref[...] = jnp.dot(xb_ref[...], w_ref[...], preferred_element_type=F32).astype(o_ref.dtype)


def _in_proj(x, w, out_dtype, tm, tn):
    s, d = x.shape
    n = w.shape[1]
    return pl.pallas_call(
        _inproj_kernel,
        out_shape=jax.ShapeDtypeStruct((s, n), out_dtype),
        grid=(s // tm, n // tn),
        in_specs=[pl.BlockSpec((tm, d), lambda i, j: (i, 0)),
                  pl.BlockSpec((d, tn), lambda i, j: (0, j))],
        out_specs=pl.BlockSpec((tm, tn), lambda i, j: (i, j)),
        scratch_shapes=[pltpu.VMEM((tm, d), BF16)],
        compiler_params=_cparams(2, VMEM_LIMIT),
        name="in_proj",
    )(x, w)


def _outproj_ln_kernel(oa_ref, ob_ref, wa_ref, wb_ref, x_ref, g_ref, b_ref, o_ref):
    y = jnp.dot(oa_ref[...], wa_ref[...], preferred_element_type=F32)
    y = y + jnp.dot(ob_ref[...], wb_ref[...], preferred_element_type=F32)
    o_ref[...] = _layer_norm(ALPHA * x_ref[...] + y, g_ref[...], b_ref[...])


def _out_proj_ln(oa, ob, w, x, g, b, tm):
    s, d = x.shape
    gw = oa.shape[1]
    return pl.pallas_call(
        _outproj_ln_kernel,
        out_shape=jax.ShapeDtypeStruct((s, d), F32),
        grid=(s // tm,),
        in_specs=[pl.BlockSpec((tm, gw), lambda i: (i, 0)),
                  pl.BlockSpec((tm, gw), lambda i: (i, 0)),
                  pl.BlockSpec((gw, d), lambda i: (0, 0)),
                  pl.BlockSpec((gw, d), lambda i: (1, 0)),
                  pl.BlockSpec((tm, d), lambda i: (i, 0)),
                  pl.BlockSpec((1, d), lambda i: (0, 0)),
                  pl.BlockSpec((1, d), lambda i: (0, 0))],
        out_specs=pl.BlockSpec((tm, d), lambda i: (i, 0)),
        compiler_params=_cparams(1, VMEM_LIMIT),
        name="out_proj_ln",
    )(oa, ob, w, w, x, g.reshape(1, d), b.reshape(1, d))


def _decay_cumsum_kernel(f_ref, b_ref, c_ref):
    nh, s = f_ref.shape
    rows = lax.broadcasted_iota(jnp.int32, (LANES, LANES), 0)
    cols = lax.broadcasted_iota(jnp.int32, (LANES, LANES), 1)
    upper = jnp.where(rows <= cols, 1.0, 0.0).astype(F32)
    bias = b_ref[...]

    def body(n, carry):
        off = pl.multiple_of(n * LANES, LANES)
        z = f_ref[:, pl.ds(off, LANES)] + bias
        logf = jnp.minimum(z, 0.0) - jnp.log1p(jnp.exp(-jnp.abs(z)))
        c = jnp.dot(logf, upper, preferred_element_type=F32,
                    precision=lax.Precision.HIGHEST) + carry
        c_ref[:, pl.ds(off, LANES)] = c * LOG2E
        return carry + jnp.sum(logf, axis=-1, keepdims=True)

    lax.fori_loop(0, s // LANES, body, jnp.zeros((nh, 1), F32))


def _decay_cumsum(f_t, bias):
    nh, s = f_t.shape
    return pl.pallas_call(
        _decay_cumsum_kernel,
        out_shape=jax.ShapeDtypeStruct((nh, s), F32),
        name="decay_cumsum",
    )(f_t, bias.reshape(nh, 1))


def _t5_bucket_np(dist):
    n = np.maximum(dist, 0)
    nf = np.maximum(n, 1).astype(np.float32)
    ratio = np.log(nf / np.float32(MAX_EXACT)) / np.float32(math.log(MAX_DISTANCE / MAX_EXACT))
    large = MAX_EXACT + (ratio.astype(np.float32) * np.float32(N_BUCKETS - MAX_EXACT)).astype(np.int32)
    large = np.minimum(large, N_BUCKETS - 1)
    return np.where(n < MAX_EXACT, n, large).astype(np.int32)


def _near_tile_count(t):
    d = np.arange(0, 4 * MAX_DISTANCE, dtype=np.int64)
    not_last = np.nonzero(_t5_bucket_np(d) != N_BUCKETS - 1)[0]
    d_sat = int(not_last.max()) + 1
    n = 1
    while (n - 1) * t + 1 < d_sat:
        n += 1
    return n


def _bucket_tiles_np(t, n_near):
    r = np.arange(t)[:, None]
    c = np.arange(t)[None, :]
    tiles = []
    for delta in range(n_near):
        dist = delta * t + r - c
        tiles.append(np.where(dist >= 0, _t5_bucket_np(dist), -1))
    tiles.append(np.full((t, t), N_BUCKETS - 1))
    tiles.append(np.full((t, t), -1))
    return np.stack(tiles).astype(np.int32)


def _bias_tiles_kernel(tab_ref, idx_ref, o_ref):
    h = pl.program_id(0)
    idx = idx_ref[...]
    out = jnp.full(idx.shape, NEG_INF, F32)
    for b in range(N_BUCKETS):
        out = jnp.where(idx == b, tab_ref[b, h] * LOG2E, out)
    o_ref[...] = out


def _bias_tiles(rel_bias, t, n_near):
    nh = rel_bias.shape[1]
    idx = jnp.asarray(_bucket_tiles_np(t, n_near))
    n_tiles = idx.shape[0]
    return pl.pallas_call(
        _bias_tiles_kernel,
        out_shape=jax.ShapeDtypeStruct((nh, n_tiles, t, t), F32),
        grid=(nh, n_tiles),
        in_specs=[pl.BlockSpec(memory_space=pltpu.SMEM),
                  pl.BlockSpec((None, t, t), lambda h, n: (n, 0, 0))],
        out_specs=pl.BlockSpec((None, None, t, t), lambda h, n: (h, n, 0, 0)),
        compiler_params=_cparams(2),
        name="bias_tiles",
    )(rel_bias, idx)


def _online_softmax_step(s, v, carry):
    m, l, acc = carry
    m_new = jnp.maximum(m, jnp.max(s, axis=-1, keepdims=True))
    alpha = jnp.exp2(m - m_new)
    p = jnp.exp2(s - m_new)
    l = alpha * l + jnp.sum(p, axis=-1, keepdims=True)
    acc = alpha * acc + jnp.dot(p.astype(BF16), v, preferred_element_type=F32)
    return m_new, l, acc


def _prescale(q, c):
    return (q.astype(F32) * c).astype(BF16)


def _near_bias(bias_ref, i, j, t, tk, n_near):
    nsub = tk // t
    tiles = []
    for b in range(nsub):
        delta = i - (j * nsub + b)
        tiles.append(bias_ref[jnp.where(delta < 0, n_near + 1, jnp.minimum(delta, n_near))])
    return jnp.concatenate(tiles, axis=1)


def _biased_sweep(i, t, tk, scores, process, sa_ref, sb_ref, init):
    jd = (i * t) // tk
    carry = process(scores(jd), jd, init, True)
    lo = jnp.maximum(jd - 1, 0)
    carry = lax.fori_loop(lo, jd, lambda j, c: process(scores(j), j, c, True), carry)
    return _pipelined_sweep(lo, scores, lambda s, j, c: process(s, j, c, False), sa_ref, sb_ref, carry)


def _softmax_init(rows):
    return (jnp.full((rows, 1), NEG_INF, F32), jnp.zeros((rows, 1), F32),
            jnp.zeros((rows, HEAD_DIM), F32))


def _kv_tile(k_ref, v_ref, j, t):
    off = pl.multiple_of(j * t, t)
    return k_ref[pl.ds(off, t), :], v_ref[pl.ds(off, t), :], off


def _pipelined_sweep(n, scores, process, sa_ref, sb_ref, carry):
    last = jnp.maximum(n - 1, 0)
    sa_ref[...] = scores(0)

    def pair(p, c):
        j = 2 * p
        sb_ref[...] = scores(jnp.minimum(j + 1, last))
        c = process(sa_ref[...], j, c)
        sa_ref[...] = scores(jnp.minimum(j + 2, last))
        return process(sb_ref[...], j + 1, c)

    carry = lax.fori_loop(0, n // 2, pair, carry)
    return lax.fori_loop(0, n % 2, lambda _, c: process(sa_ref[...], n - 1, c), carry)


def _fox_kernel(q_ref, k_ref, v_ref, c_ref, o_ref, sa_ref, sb_ref, *, t, tk, scale):
    i = pl.program_id(1)
    q = _prescale(q_ref[...], scale * LOG2E)
    rows = lax.broadcasted_iota(jnp.int32, (t, tk), 0)
    cols = lax.broadcasted_iota(jnp.int32, (t, tk), 1)

    def scores(j):
        off = pl.multiple_of(j * tk, tk)
        return _nt_dot(q, k_ref[pl.ds(off, tk), :]) - c_ref[:, pl.ds(off, tk)]

    def values(j):
        return v_ref[pl.ds(pl.multiple_of(j * tk, tk), tk), :]

    def process(s, j, carry):
        return _online_softmax_step(s, values(j), carry)

    jd = (i * t) // tk
    s_diag = jnp.where(cols <= rows + (i * t - jd * tk), scores(jd), NEG_INF)
    carry = _online_softmax_step(s_diag, values(jd), _softmax_init(t))
    _, l, acc = _pipelined_sweep(jd, scores, process, sa_ref, sb_ref, carry)
    o_ref[...] = (acc / l).astype(o_ref.dtype)


def _moba_kernel(tab_ref, q_ref, k_ref, v_ref, bias_ref, et_ref, o_ref, kmean_ref, sa_ref, sb_ref,
                 *, t, tk, nblk, n_near, scale):
    h = pl.program_id(0)
    i = pl.program_id(1)
    nbp = kmean_ref.shape[0]

    @pl.when(i == 0)
    def _():
        kmean_ref[...] = jnp.zeros(kmean_ref.shape, F32)

        def mean_body(n, _):
            off = pl.multiple_of(n * t, t)
            kb = k_ref[pl.ds(off, t), :].astype(F32)
            kmean_ref[pl.ds(n, 1), :] = jnp.sum(kb, axis=0, keepdims=True) * (1.0 / t)
            return 0

        lax.fori_loop(0, nblk, mean_body, 0)

    q = q_ref[...]
    km_hi, km_lo = _split_bf16(kmean_ref[...])
    gate = _nt_dot(q, km_hi) + _nt_dot(q, km_lo)
    blk = lax.broadcasted_iota(jnp.int32, (t, nbp), 1)
    eligible = blk < i
    sel = jnp.zeros((t, nbp), F32)
    for r in range(MOBA_TOPK):
        cand = jnp.where(eligible, jnp.where(sel > 0.0, -jnp.inf, gate), -jnp.inf)
        vmax = jnp.max(cand, axis=-1, keepdims=True)
        first = jnp.min(jnp.where(cand == vmax, blk, nbp), axis=-1, keepdims=True)
        take = jnp.where(r < i, 1.0, 0.0)
        sel = jnp.maximum(sel, jnp.where(blk == first, take, 0.0))

    sel = jnp.maximum(sel, jnp.where(blk == i, 1.0, 0.0))
    penalty = jnp.where(sel > 0.0, 0.0, NEG_INF).astype(BF16)
    q_aug = jnp.concatenate([_prescale(q, scale * LOG2E), penalty], axis=1)
    b_far = tab_ref[N_BUCKETS - 1, h] * LOG2E

    def scores(j):
        off = pl.multiple_of(j * tk, tk)
        k_aug = jnp.concatenate([k_ref[pl.ds(off, tk), :], et_ref[pl.ds(off, tk), :]], axis=1)
        return _nt_dot(q_aug, k_aug)

    def process(s, j, carry, near):
        bias = _near_bias(bias_ref, i, j, t, tk, n_near) if near else b_far
        v = v_ref[pl.ds(pl.multiple_of(j * tk, tk), tk), :]
        return _online_softmax_step(s + bias, v, carry)

    _, l, acc = _biased_sweep(i, t, tk, scores, process, sa_ref, sb_ref, _softmax_init(t))
    o_ref[...] = (acc / l).astype(o_ref.dtype)


def _diff_kernel(tab_ref, q_ref, k_ref, v_ref, bias_ref, lq_ref, lk_ref, g_ref, o_ref, sa_ref, sb_ref,
                 *, t, tk, n_near, scale, lambda_init):
    h = pl.program_id(0)
    i = pl.program_id(1)
    q = _prescale(q_ref[...], scale * LOG2E)
    lane = lax.broadcasted_iota(jnp.int32, (t, HEAD_DIM), 1)
    zero = jnp.zeros_like(q)
    q2 = jnp.concatenate([jnp.where(lane < DIFF_QK_DIM, q, zero),
                          jnp.where(lane >= DIFF_QK_DIM, q, zero)], axis=0)
    b_far = tab_ref[N_BUCKETS - 1, h] * LOG2E

    def scores(j):
        return _nt_dot(q2, k_ref[pl.ds(pl.multiple_of(j * tk, tk), tk), :])

    def process(s, j, carry, near):
        if near:
            b = _near_bias(bias_ref, i, j, t, tk, n_near)
            bias = jnp.concatenate([b, b], axis=0)
        else:
            bias = b_far
        v = v_ref[pl.ds(pl.multiple_of(j * tk, tk), tk), :]
        return _online_softmax_step(s + bias, v, carry)

    _, l, acc = _biased_sweep(i, t, tk, scores, process, sa_ref, sb_ref, _softmax_init(2 * t))
    o12 = acc / l
    lam_e = jnp.exp(jnp.sum(lq_ref[...] * lk_ref[...], axis=-1, keepdims=True))
    lam = lam_e[0:1, :] - lam_e[1:2, :] + lambda_init
    o = o12[:t, :] - lam * o12[t:, :]
    o = o * lax.rsqrt(jnp.mean(o * o, axis=-1, keepdims=True) + LN_EPS) * g_ref[...]
    o_ref[...] = (o * (1.0 - lambda_init)).astype(o_ref.dtype)


def _sb_kernel(q_ref, k_ref, v_ref, o_ref, *, t, scale):
    i = pl.program_id(1)
    q = q_ref[...]
    rows = lax.broadcasted_iota(jnp.int32, (t, t), 0)
    cols = lax.broadcasted_iota(jnp.int32, (t, t), 1)
    later = jnp.where(rows > cols, 1.0, 0.0).astype(BF16)

    def block(j, rem, acc, diag):
        k, v, _ = _kv_tile(k_ref, v_ref, j, t)
        z = _nt_dot(q, k) * scale
        soft = jnp.log1p(jnp.exp(-jnp.abs(z)))
        log_sig = jnp.minimum(z, 0.0) - soft
        log_rem = jnp.minimum(-z, 0.0) - soft
        if diag:
            mask = cols < rows
            log_rem = jnp.where(mask, log_rem, 0.0)
        hi, lo = _split_bf16(log_rem)
        after = (jnp.dot(hi, later, preferred_element_type=F32)
                 + jnp.dot(lo, later, preferred_element_type=F32))
        a = jnp.exp(log_sig + after + rem)
        if diag:
            a = jnp.where(mask, a, 0.0)
        acc = acc + jnp.dot(a.astype(BF16), v, preferred_element_type=F32)
        rem = rem + jnp.sum(log_rem, axis=-1, keepdims=True)
        return rem, acc

    rem, acc = block(i, jnp.zeros((t, 1), F32), jnp.zeros((t, HEAD_DIM), F32), True)

    def live(rem):
        return (jnp.max(rem) > SB_CUTOFF).astype(jnp.int32)

    def cond(c):
        j, go, _, _ = c
        return jnp.logical_and(j >= 0, go > 0)

    def body(c):
        j, _, rem, acc = c
        rem, acc = block(j, rem, acc, False)
        return j - 1, live(rem), rem, acc

    _, _, _, acc = lax.while_loop(cond, body, (i - 1, live(rem), rem, acc))
    o_ref[...] = acc.astype(o_ref.dtype)


def _attention_call(kernel, h, out_cols, col0, extra_in, extra_specs, scratch, name, t):
    s = h.shape[0]
    nh = N_HEADS_GROUP
    in_specs = list(extra_specs[0]) + [
        pl.BlockSpec((t, HEAD_DIM), lambda hd, i: (i, col0 + hd)),
        pl.BlockSpec((s, HEAD_DIM), lambda hd, i: (0, col0 + nh + hd)),
        pl.BlockSpec((s, HEAD_DIM), lambda hd, i: (0, col0 + 2 * nh + hd)),
    ] + list(extra_specs[1])
    args = list(extra_in[0]) + [h, h, h] + list(extra_in[1])
    return pl.pallas_call(
        kernel,
        out_shape=jax.ShapeDtypeStruct((s, out_cols), BF16),
        grid=(nh, s // t),
        in_specs=in_specs,
        out_specs=pl.BlockSpec((t, HEAD_DIM), lambda hd, i: (i, hd)),
        scratch_shapes=scratch,
        compiler_params=_cparams(2, VMEM_LIMIT),
        name=name,
    )(*args)


def _router_kernel(x_ref, w_ref, b_ref, o_ref):
    logits = jnp.dot(x_ref[...], w_ref[...], preferred_element_type=F32,
                     precision=lax.Precision.HIGHEST) + b_ref[...]
    tm, n = logits.shape
    lane = lax.broadcasted_iota(jnp.int32, (tm, n), 1)
    real = lane < N_EXPERTS
    logits = jnp.where(real, logits, -jnp.inf)
    e = jnp.exp(logits - jnp.max(logits, axis=-1, keepdims=True))
    aff = e / jnp.sum(e, axis=-1, keepdims=True)
    group = lane // EXPERTS_PER_GROUP
    best = jnp.full((tm, 1), -jnp.inf, F32)
    g_sel = jnp.zeros((tm, 1), jnp.int32)
    for g in range(N_GROUPS):
        gmax = jnp.max(jnp.where(group == g, aff, -jnp.inf), axis=-1, keepdims=True)
        better = gmax > best
        g_sel = jnp.where(better, g, g_sel)
        best = jnp.where(better, gmax, best)
    in_group = jnp.logical_and(group == g_sel, real)
    cand = jnp.where(in_group, aff, -jnp.inf)
    v1 = jnp.max(cand, axis=-1, keepdims=True)
    i1 = jnp.min(jnp.where(cand == v1, lane, n), axis=-1, keepdims=True)
    cand2 = jnp.where(lane == i1, -jnp.inf, cand)
    v2 = jnp.max(cand2, axis=-1, keepdims=True)
    i2 = jnp.min(jnp.where(cand2 == v2, lane, n), axis=-1, keepdims=True)
    tot = v1 + v2
    o_ref[...] = jnp.where(lane == 0, v1 / tot,
                           jnp.where(lane == 1, v2 / tot,
                                     jnp.where(lane == 2, i1.astype(F32),
                                               jnp.where(lane == 3, i2.astype(F32), 0.0))))


def _router(x, w_router, b_router, tm):
    s, d = x.shape
    w = jnp.zeros((d, LANES), F32).at[:, :N_EXPERTS].set(w_router)
    b = jnp.zeros((1, LANES), F32).at[0, :N_EXPERTS].set(b_router)
    return pl.pallas_call(
        _router_kernel,
        out_shape=jax.ShapeDtypeStruct((s, LANES), F32),
        grid=(s // tm,),
        in_specs=[pl.BlockSpec((tm, d), lambda i: (i, 0)),
                  pl.BlockSpec((d, LANES), lambda i: (0, 0)),
                  pl.BlockSpec((1, LANES), lambda i: (0, 0))],
        out_specs=pl.BlockSpec((tm, LANES), lambda i: (i, 0)),
        compiler_params=_cparams(1, VMEM_LIMIT),
        name="router",
    )(x, w, b)


def _dispatch_plan(route, tm):
    s = route.shape[0]
    n_tiles = (MOE_TOP_K * s) // tm + N_EXPERTS
    expert = route[:, 2:2 + MOE_TOP_K].astype(jnp.int32).reshape(-1)
    onehot = (expert[:, None] == jnp.arange(N_EXPERTS, dtype=jnp.int32)[None, :]).astype(jnp.int32)
    running = jnp.cumsum(onehot, axis=0)
    rank = jnp.take_along_axis(running, expert[:, None], axis=1)[:, 0] - 1
    padded = -(-running[-1] // tm) * tm
    seg_end = jnp.cumsum(padded)
    pair_row = (seg_end - padded)[expert] + rank
    tile_start = jnp.arange(n_tiles, dtype=jnp.int32) * tm
    tile_expert = jnp.minimum(jnp.searchsorted(seg_end, tile_start, side="right"),
                              N_EXPERTS - 1).astype(jnp.int32)
    tile_live = (tile_start < seg_end[-1]).astype(jnp.int32)
    row_token = (jnp.arange(n_tiles * tm, dtype=jnp.int32) % s).at[pair_row].set(
        jnp.arange(MOE_TOP_K * s, dtype=jnp.int32) // MOE_TOP_K)
    return pair_row.astype(jnp.int32), row_token, tile_expert, tile_live


def _row_gather_kernel(idx_ref, src_ref, o_ref, sem, *, chunk):
    def issue(r, _):
        pltpu.make_async_copy(src_ref.at[pl.ds(idx_ref[0, r], 1)],
                              o_ref.at[pl.ds(r, 1)], sem).start()
        return 0

    lax.fori_loop(0, chunk, issue, 0, unroll=8)
    pltpu.make_async_copy(src_ref.at[pl.ds(0, chunk)], o_ref, sem).wait()


def _row_gather(src, idx, chunk):
    n = idx.shape[0]
    d = src.shape[1]
    return pl.pallas_call(
        functools.partial(_row_gather_kernel, chunk=chunk),
        out_shape=jax.ShapeDtypeStruct((n, d), src.dtype),
        grid=(n // chunk,),
        in_specs=[pl.BlockSpec((None, 1, chunk), lambda c: (c, 0, 0), memory_space=pltpu.SMEM),
                  pl.BlockSpec(memory_space=pl.ANY)],
        out_specs=pl.BlockSpec((chunk, d), lambda c: (c, 0)),
        scratch_shapes=[pltpu.SemaphoreType.DMA(())],
        compiler_params=_cparams(1, VMEM_LIMIT),
        name="row_gather",
    )(idx.reshape(n // chunk, 1, chunk), src)


def _moe_group_kernel(te_ref, live_ref, x_ref, wg_ref, wu_ref, wd_ref, o_ref, xb_ref):
    g = pl.program_id(0)
    f = pl.program_id(1)
    live = live_ref[g] > 0

    @pl.when(jnp.logical_and(live, f == 0))
    def _():
        xb_ref[...] = x_ref[...].astype(BF16)

    @pl.when(jnp.logical_and(jnp.logical_not(live), f == 0))
    def _():
        o_ref[...] = jnp.zeros(o_ref.shape, F32)

    @pl.when(live)
    def _():
        xb = xb_ref[...]
        a = jnp.dot(xb, wg_ref[...], preferred_element_type=F32)
        u = jnp.dot(xb, wu_ref[...], preferred_element_type=F32)
        hid = ((a * jax.nn.sigmoid(a)) * u).astype(BF16)
        y = jnp.dot(hid, wd_ref[...], preferred_element_type=F32)

        @pl.when(f == 0)
        def _():
            o_ref[...] = y

        @pl.when(f > 0)
        def _():
            o_ref[...] += y


def _moe_group_ffn(xs, tile_expert, tile_live, wg, wu, wd, tm, tf):
    rows, d = xs.shape
    fdim = wg.shape[2]
    grid_spec = pltpu.PrefetchScalarGridSpec(
        num_scalar_prefetch=2,
        grid=(rows // tm, fdim // tf),
        in_specs=[pl.BlockSpec((tm, d), lambda g, f, te, lv: (g, 0)),
                  pl.BlockSpec((None, d, tf), lambda g, f, te, lv: (te[g], 0, f)),
                  pl.BlockSpec((None, d, tf), lambda g, f, te, lv: (te[g], 0, f)),
                  pl.BlockSpec((None, tf, d), lambda g, f, te, lv: (te[g], f, 0))],
        out_specs=pl.BlockSpec((tm, d), lambda g, f, te, lv: (g, 0)),
        scratch_shapes=[pltpu.VMEM((tm, d), BF16)])
    return pl.pallas_call(
        _moe_group_kernel,
        out_shape=jax.ShapeDtypeStruct((rows, d), F32),
        grid_spec=grid_spec,
        compiler_params=_cparams(2, VMEM_LIMIT),
        name="moe_group_ffn",
    )(tile_expert, tile_live, xs, wg, wu, wd)


def _moe_combine_ln_kernel(x_ref, r_ref, y0_ref, y1_ref, g_ref, b_ref, o_ref):
    r = r_ref[...]
    ff = r[:, 0:1] * y0_ref[...] + r[:, 1:2] * y1_ref[...]
    o_ref[...] = _layer_norm(ALPHA * x_ref[...] + ff, g_ref[...], b_ref[...])


def _moe_combine_ln(x, route, y, g, b, tm):
    s, d = x.shape
    nt = s // tm
    return pl.pallas_call(
        _moe_combine_ln_kernel,
        out_shape=jax.ShapeDtypeStruct((s, d), F32),
        grid=(nt,),
        in_specs=[pl.BlockSpec((tm, d), lambda i: (i, 0)),
                  pl.BlockSpec((tm, LANES), lambda i: (i, 0)),
                  pl.BlockSpec((tm, d), lambda i: (i, 0)),
                  pl.BlockSpec((tm, d), lambda i: (i + nt, 0)),
                  pl.BlockSpec((1, d), lambda i: (0, 0)),
                  pl.BlockSpec((1, d), lambda i: (0, 0))],
        out_specs=pl.BlockSpec((tm, d), lambda i: (i, 0)),
        compiler_params=_cparams(1, VMEM_LIMIT),
        name="moe_combine_ln",
    )(x, route, y, y, g.reshape(1, d), b.reshape(1, d))


def _moe_ln(x, route, wg, wu, wd, g, b, tm):
    s = x.shape[0]
    pair_row, row_token, tile_expert, tile_live = _dispatch_plan(route, tm)
    xs = _row_gather(x, row_token, MOE_GATHER_CHUNK)
    ys = _moe_group_ffn(xs, tile_expert, tile_live, wg, wu, wd, tm, D_EXPERT // 2)
    slot_major = pair_row.reshape(s, MOE_TOP_K).T.reshape(-1)
    y = _row_gather(ys, slot_major, MOE_GATHER_CHUNK)
    return _moe_combine_ln(x, route, y, g, b, tm)


def kernel(x, w_in_even, w_out_even, forget_bias, w_in_odd, w_out_odd, lambda_q, lambda_k,
           subln_gain, rel_bias, w_router, b_router, w_gate, w_up, w_down, ln_gain, ln_bias):
    bsz, s, d = x.shape
    assert bsz == 1 and d == D_MODEL and s % MOBA_BLOCK == 0
    t = ATT_TILE
    nh = N_HEADS_GROUP
    gw = GROUP_WIDTH
    tm_proj = min(1024, s)
    tm_row = min(512, s)
    tk = min(ATT_KEYS, s)
    n_near = _near_tile_count(t)
    assert s % tk == 0 and tk // t + 1 >= n_near
    nblk = s // MOBA_BLOCK
    nbp = -(-nblk // LANES) * LANES

    x2 = x.reshape(s, d)
    bias_tiles = _bias_tiles(rel_bias, t, n_near)
    tab_spec = pl.BlockSpec(memory_space=pltpu.SMEM)
    bias_spec = pl.BlockSpec((None, n_near + 2, t, t), lambda hd, i: (hd, 0, 0, 0))
    block_onehot = (jnp.arange(s, dtype=jnp.int32)[:, None] // MOBA_BLOCK
                    == jnp.arange(nbp, dtype=jnp.int32)[None, :]).astype(BF16)

    for layer in range(DEPTH):
        li = layer // 2
        if layer % 2 == 0:
            w = w_in_even[li]
            fcol = 3 * gw
            w_main = jnp.concatenate([w[:, :fcol], w[:, fcol + nh:]], axis=1).astype(BF16)
            w_f = jnp.zeros((d, LANES), BF16).at[:, :nh].set(w[:, fcol:fcol + nh].astype(BF16))
            h = _in_proj(x2, w_main, BF16, tm_proj, 512)
            f = _in_proj(x2, w_f, F32, tm_proj, LANES)
            c = _decay_cumsum(f[:, :nh].T, forget_bias[li])
            oa = _attention_call(
                functools.partial(_fox_kernel, t=FOX_TQ, tk=min(FOX_TK, s), scale=HEAD_DIM ** -0.5), h, gw, 0,
                ([], [c.reshape(nh, 1, s)]),
                ([], [pl.BlockSpec((None, 1, s), lambda hd, i: (hd, 0, 0))]),
                [pltpu.VMEM((FOX_TQ, min(FOX_TK, s)), F32)] * 2, "fox_attention", FOX_TQ)
            ob = _attention_call(
                functools.partial(_moba_kernel, t=t, tk=tk, nblk=nblk, n_near=n_near,
                                  scale=HEAD_DIM ** -0.5),
                h, gw, 3 * nh,
                ([rel_bias], [bias_tiles, block_onehot]),
                ([tab_spec], [bias_spec, pl.BlockSpec((s, nbp), lambda hd, i: (0, 0))]),
                [pltpu.VMEM((nbp, HEAD_DIM), F32)] + [pltpu.VMEM((t, tk), F32)] * 2, "moba_attention", t)
            w_out = w_out_even[li]
        else:
            lambda_init = 0.8 - 0.6 * math.exp(-0.3 * layer)
            h = _in_proj(x2, w_in_odd[li].astype(BF16), BF16, tm_proj, 512)
            small = lambda shape: pl.BlockSpec(shape, lambda hd, i: (0, 0))
            oa = _attention_call(
                functools.partial(_diff_kernel, t=t, tk=tk, n_near=n_near, scale=DIFF_QK_DIM ** -0.5,
                                  lambda_init=lambda_init),
                h, gw, 0,
                ([rel_bias], [bias_tiles, lambda_q[li], lambda_k[li], subln_gain[li].reshape(1, HEAD_DIM)]),
                ([tab_spec], [bias_spec, small((2, DIFF_QK_DIM)), small((2, DIFF_QK_DIM)),
                              small((1, HEAD_DIM))]),
                [pltpu.VMEM((2 * t, tk), F32)] * 2, "diff_attention", t)
            ob = _attention_call(
                functools.partial(_sb_kernel, t=t, scale=HEAD_DIM ** -0.5), h, gw, 3 * nh,
                ([], []), ([], []), [], "stickbreak_attention", t)
            w_out = w_out_odd[li]
        x2 = _out_proj_ln(oa, ob, w_out.astype(BF16), x2, ln_gain[layer, 0], ln_bias[layer, 0], tm_row)
        route = _router(x2, w_router, b_router, tm_row)
        x2 = _moe_ln(x2, route, w_gate[layer].astype(BF16), w_up[layer].astype(BF16),
                     w_down[layer].astype(BF16), ln_gain[layer, 1], ln_bias[layer, 1], tm_row)
    return x2.reshape(bsz, s, d)
```

```python
import functools
import math

import numpy as np
import jax
import jax.numpy as jnp
from jax import lax
from jax.experimental import pallas as pl
from jax.experimental.pallas import tpu as pltpu

F32 = jnp.float32
BF16 = jnp.bfloat16

D_MODEL = 2048
DEPTH = 2
HEAD_DIM = 128
N_HEADS_GROUP = 8
GROUP_WIDTH = N_HEADS_GROUP * HEAD_DIM
DIFF_QK_DIM = HEAD_DIM // 2
MOBA_BLOCK = 256
MOBA_TOPK = 3
N_BUCKETS = 32
MAX_EXACT = N_BUCKETS // 2
MAX_DISTANCE = 1024
N_EXPERTS = 16
N_GROUPS = 4
EXPERTS_PER_GROUP = N_EXPERTS // N_GROUPS
MOE_TOP_K = 2
MOE_GATHER_CHUNK = 512
D_EXPERT = D_MODEL // 2
ALPHA = (2.0 * DEPTH) ** 0.25
LN_EPS = 1e-5
NEG_INF = -1e30
LOG2E = math.log2(math.e)
LANES = 128
ATT_TILE = 256
ATT_KEYS = 1024
FOX_TQ = 512
FOX_TK = 1024
SHIFT_SLACK = 96.0
FLUSH_EXPONENT = -130.0
SB_CUTOFF = -104.0
VMEM_LIMIT = 56 * 1024 * 1024


def _cparams(n_axes, vmem=None):
    return pltpu.CompilerParams(dimension_semantics=("arbitrary",) * n_axes,
                                vmem_limit_bytes=vmem)


def _nt_dot(a, b):
    return lax.dot_general(a, b, (((1,), (1,)), ((), ())), preferred_element_type=F32)


def _split_bf16(x):
    hi = x.astype(BF16)
    lo = (x - hi.astype(F32)).astype(BF16)
    return hi, lo


def _layer_norm(z, g, b):
    mu = jnp.mean(z, axis=-1, keepdims=True)
    zc = z - mu
    var = jnp.mean(zc * zc, axis=-1, keepdims=True)
    return zc * lax.rsqrt(var + LN_EPS) * g + b


def _inproj_kernel(x_ref, w_ref, o_ref, xb_ref):
    @pl.when(pl.program_id(1) == 0)
    def _():
        xb_ref[...] = x_ref[...].astype(BF16)

    o_ref[...] = jnp.dot(xb_ref[...], w_ref[...], preferred_element_type=F32).astype(o_ref.dtype)


def _in_proj(x, w, out_dtype, tm, tn):
    s, d = x.shape
    n = w.shape[1]
    return pl.pallas_call(
        _inproj_kernel,
        out_shape=jax.ShapeDtypeStruct((s, n), out_dtype),
        grid=(s // tm, n // tn),
        in_specs=[pl.BlockSpec((tm, d), lambda i, j: (i, 0)),
                  pl.BlockSpec((d, tn), lambda i, j: (0, j))],
        out_specs=pl.BlockSpec((tm, tn), lambda i, j: (i, j)),
        scratch_shapes=[pltpu.VMEM((tm, d), BF16)],
        compiler_params=_cparams(2, VMEM_LIMIT),
        name="in_proj",
    )(x, w)


def _outproj_ln_kernel(oa_ref, ob_ref, wa_ref, wb_ref, x_ref, g_ref, b_ref, o_ref):
    y = jnp.dot(oa_ref[...], wa_ref[...], preferred_element_type=F32)
    y = y + jnp.dot(ob_ref[...], wb_ref[...], preferred_element_type=F32)
    o_ref[...] = _layer_norm(ALPHA * x_ref[...] + y, g_ref[...], b_ref[...])


def _out_proj_ln(oa, ob, w, x, g, b, tm):
    s, d = x.shape
    gw = oa.shape[1]
    return pl.pallas_call(
        _outproj_ln_kernel,
        out_shape=jax.ShapeDtypeStruct((s, d), F32),
        grid=(s // tm,),
        in_specs=[pl.BlockSpec((tm, gw), lambda i: (i, 0)),
                  pl.BlockSpec((tm, gw), lambda i: (i, 0)),
                  pl.BlockSpec((gw, d), lambda i: (0, 0)),
                  pl.BlockSpec((gw, d), lambda i: (1, 0)),
                  pl.BlockSpec((tm, d), lambda i: (i, 0)),
                  pl.BlockSpec((1, d), lambda i: (0, 0)),
                  pl.BlockSpec((1, d), lambda i: (0, 0))],
        out_specs=pl.BlockSpec((tm, d), lambda i: (i, 0)),
        compiler_params=_cparams(1, VMEM_LIMIT),
        name="out_proj_ln",
    )(oa, ob, w, w, x, g.reshape(1, d), b.reshape(1, d))


def _decay_cumsum_kernel(f_ref, b_ref, c_ref):
    nh, s = f_ref.shape
    rows = lax.broadcasted_iota(jnp.int32, (LANES, LANES), 0)
    cols = lax.broadcasted_iota(jnp.int32, (LANES, LANES), 1)
    upper = jnp.where(rows <= cols, 1.0, 0.0).astype(F32)
    bias = b_ref[...]

    def body(n, carry):
        off = pl.multiple_of(n * LANES, LANES)
        z = f_ref[:, pl.ds(off, LANES)] + bias
        logf = jnp.minimum(z, 0.0) - jnp.log1p(jnp.exp(-jnp.abs(z)))
        c = jnp.dot(logf, upper, preferred_element_type=F32,
                    precision=lax.Precision.HIGHEST) + carry
        c_ref[:, pl.ds(off, LANES)] = c * LOG2E
        return carry + jnp.sum(logf, axis=-1, keepdims=True)

    lax.fori_loop(0, s // LANES, body, jnp.zeros((nh, 1), F32))


def _decay_cumsum(f_t, bias):
    nh, s = f_t.shape
    return pl.pallas_call(
        _decay_cumsum_kernel,
        out_shape=jax.ShapeDtypeStruct((nh, s), F32),
        name="decay_cumsum",
    )(f_t, bias.reshape(nh, 1))


def _t5_bucket_np(dist):
    n = np.maximum(dist, 0)
    nf = np.maximum(n, 1).astype(np.float32)
    ratio = np.log(nf / np.float32(MAX_EXACT)) / np.float32(math.log(MAX_DISTANCE / MAX_EXACT))
    large = MAX_EXACT + (ratio.astype(np.float32) * np.float32(N_BUCKETS - MAX_EXACT)).astype(np.int32)
    large = np.minimum(large, N_BUCKETS - 1)
    return np.where(n < MAX_EXACT, n, large).astype(np.int32)


def _near_tile_count(t):
    d = np.arange(0, 4 * MAX_DISTANCE, dtype=np.int64)
    not_last = np.nonzero(_t5_bucket_np(d) != N_BUCKETS - 1)[0]
    d_sat = int(not_last.max()) + 1
    n = 1
    while (n - 1) * t + 1 < d_sat:
        n += 1
    return n


def _bucket_tiles_np(t, n_near):
    r = np.arange(t)[:, None]
    c = np.arange(t)[None, :]
    tiles = []
    for delta in range(n_near):
        dist = delta * t + r - c
        tiles.append(np.where(dist >= 0, _t5_bucket_np(dist), -1))
    tiles.append(np.full((t, t), N_BUCKETS - 1))
    tiles.append(np.full((t, t), -1))
    return np.stack(tiles).astype(np.int32)


def _bias_tiles_kernel(tab_ref, idx_ref, o_ref):
    h = pl.program_id(0)
    idx = idx_ref[...]
    out = jnp.full(idx.shape, NEG_INF, F32)
    for b in range(N_BUCKETS):
        out = jnp.where(idx == b, tab_ref[b, h] * LOG2E, out)
    o_ref[...] = out


def _bias_tiles(rel_bias, t, n_near):
    nh = rel_bias.shape[1]
    idx = jnp.asarray(_bucket_tiles_np(t, n_near))
    n_tiles = idx.shape[0]
    return pl.pallas_call(
        _bias_tiles_kernel,
        out_shape=jax.ShapeDtypeStruct((nh, n_tiles, t, t), F32),
        grid=(nh, n_tiles),
        in_specs=[pl.BlockSpec(memory_space=pltpu.SMEM),
                  pl.BlockSpec((None, t, t), lambda h, n: (n, 0, 0))],
        out_specs=pl.BlockSpec((None, None, t, t), lambda h, n: (h, n, 0, 0)),
        compiler_params=_cparams(2),
        name="bias_tiles",
    )(rel_bias, idx)


def _online_softmax_step(s, v, carry):
    m, l, acc = carry
    m_new = jnp.maximum(m, jnp.max(s, axis=-1, keepdims=True))
    alpha = jnp.exp2(m - m_new)
    p = jnp.exp2(s - m_new)
    l = alpha * l + jnp.sum(p, axis=-1, keepdims=True)
    acc = alpha * acc + jnp.dot(p.astype(BF16), v, preferred_element_type=F32)
    return m_new, l, acc


def _prescale(q, c):
    return (q.astype(F32) * c).astype(BF16)


def _near_bias(bias_ref, i, j, t, tk, n_near):
    nsub = tk // t
    tiles = []
    for b in range(nsub):
        delta = i - (j * nsub + b)
        tiles.append(bias_ref[jnp.where(delta < 0, n_near + 1, jnp.minimum(delta, n_near))])
    return jnp.concatenate(tiles, axis=1)


def _biased_sweep(i, t, tk, scores, process, sa_ref, sb_ref, init):
    jd = (i * t) // tk
    carry = process(scores(jd), jd, init, True)
    lo = jnp.maximum(jd - 1, 0)
    carry = lax.fori_loop(lo, jd, lambda j, c: process(scores(j), j, c, True), carry)
    return _pipelined_sweep(lo, scores, lambda s, j, c: process(s, j, c, False), sa_ref, sb_ref, carry)


def _softmax_init(rows):
    return (jnp.full((rows, 1), NEG_INF, F32), jnp.zeros((rows, 1), F32),
            jnp.zeros((rows, HEAD_DIM), F32))


def _kv_tile(k_ref, v_ref, j, t):
    off = pl.multiple_of(j * t, t)
    return k_ref[pl.ds(off, t), :], v_ref[pl.ds(off, t), :], off


def _pipelined_sweep(n, scores, process, sa_ref, sb_ref, carry):
    last = jnp.maximum(n - 1, 0)
    sa_ref[...] = scores(0)

    def pair(p, c):
        j = 2 * p
        sb_ref[...] = scores(jnp.minimum(j + 1, last))
        c = process(sa_ref[...], j, c)
        sa_ref[...] = scores(jnp.minimum(j + 2, last))
        return process(sb_ref[...], j + 1, c)

    carry = lax.fori_loop(0, n // 2, pair, carry)
    return lax.fori_loop(0, n % 2, lambda _, c: process(sa_ref[...], n - 1, c), carry)


def _fox_kernel(cend_ref, cstart_ref, q_ref, k_ref, v_ref, c_ref, ccol_ref, o_ref, sa_ref, sb_ref,
                knorm_ref, *, t, tk, scale):
    h = pl.program_id(0)
    i = pl.program_id(1)

    @pl.when(i == 0)
    def _():
        knorm_ref[...] = jnp.broadcast_to(_max_key_norm(k_ref, tk, 0, HEAD_DIM), knorm_ref.shape)

    q = _prescale(q_ref[...], scale * LOG2E)
    rows = lax.broadcasted_iota(jnp.int32, (t, tk), 0)
    cols = lax.broadcasted_iota(jnp.int32, (t, tk), 1)

    def scores(j):
        off = pl.multiple_of(j * tk, tk)
        return _nt_dot(q, k_ref[pl.ds(off, tk), :]) - c_ref[:, pl.ds(off, tk)]

    def values(j):
        return v_ref[pl.ds(pl.multiple_of(j * tk, tk), tk), :]

    def process(s, j, carry):
        return _online_softmax_step(s, values(j), carry)

    jd = (i * t) // tk
    s_diag = jnp.where(cols <= rows + (i * t - jd * tk), scores(jd), NEG_INF)

    qf = q.astype(F32)
    qnorm = jnp.sqrt(jnp.sum(qf * qf, axis=-1, keepdims=True))
    bound = qnorm * knorm_ref[0:1, 0:1] * (1.0 + 2.0 ** -9) + 2.0 ** -6 - ccol_ref[...]
    slack = jnp.max(bound - jnp.max(s_diag, axis=-1, keepdims=True))

    def shifted_sweep(_):
        def step(s, j, carry):
            l, acc = carry
            p = jnp.exp2(s - bound)
            return (l + jnp.sum(p, axis=-1, keepdims=True),
                    acc + jnp.dot(p.astype(BF16), values(j), preferred_element_type=F32))

        c_first_row = cstart_ref[h, i]
        first_live = jnp.int32(0)
        for j in range(cend_ref.shape[1]):
            dead = jnp.logical_and(j < jd, c_first_row - cend_ref[h, j] < FLUSH_EXPONENT)
            first_live = first_live + dead.astype(jnp.int32)
        carry = step(s_diag, jd, _softmax_init(t)[1:])
        return _paired_loop(jd - first_live, lambda n, c: step(scores(first_live + n), first_live + n, c),
                            carry)

    def running_max_sweep(_):
        carry = _online_softmax_step(s_diag, values(jd), _softmax_init(t))
        return _pipelined_sweep(jd, scores, process, sa_ref, sb_ref, carry)[1:]

    l, acc = lax.cond(slack <= SHIFT_SLACK, shifted_sweep, running_max_sweep, 0)
    o_ref[...] = (acc / l).astype(o_ref.dtype)


def _moba_kernel(tab_ref, q_ref, k_ref, v_ref, bias_ref, et_ref, o_ref, kmean_ref, sa_ref, sb_ref,
                 knorm_ref, *, t, tk, nblk, n_near, scale):
    h = pl.program_id(0)
    i = pl.program_id(1)
    nbp = kmean_ref.shape[0]

    @pl.when(i == 0)
    def _():
        kmean_ref[...] = jnp.zeros(kmean_ref.shape, F32)

        def mean_body(n, _):
            off = pl.multiple_of(n * t, t)
            kb = k_ref[pl.ds(off, t), :].astype(F32)
            kmean_ref[pl.ds(n, 1), :] = jnp.sum(kb, axis=0, keepdims=True) * (1.0 / t)
            return 0

        lax.fori_loop(0, nblk, mean_body, 0)
        knorm_ref[...] = jnp.broadcast_to(_max_key_norm(k_ref, tk, 0, HEAD_DIM), knorm_ref.shape)

    q = q_ref[...]
    km_hi, km_lo = _split_bf16(kmean_ref[...])
    gate = _nt_dot(q, km_hi) + _nt_dot(q, km_lo)
    blk = lax.broadcasted_iota(jnp.int32, (t, nbp), 1)
    eligible = blk < i
    sel = jnp.zeros((t, nbp), F32)
    for r in range(MOBA_TOPK):
        cand = jnp.where(eligible, jnp.where(sel > 0.0, -jnp.inf, gate), -jnp.inf)
        vmax = jnp.max(cand, axis=-1, keepdims=True)
        first = jnp.min(jnp.where(cand == vmax, blk, nbp), axis=-1, keepdims=True)
        take = jnp.where(r < i, 1.0, 0.0)
        sel = jnp.maximum(sel, jnp.where(blk == first, take, 0.0))

    sel = jnp.maximum(sel, jnp.where(blk == i, 1.0, 0.0))
    penalty = jnp.where(sel > 0.0, 0.0, NEG_INF).astype(BF16)
    q_aug = jnp.concatenate([_prescale(q, scale * LOG2E), penalty], axis=1)
    b_far = tab_ref[N_BUCKETS - 1, h] * LOG2E

    def scores(j):
        off = pl.multiple_of(j * tk, tk)
        k_aug = jnp.concatenate([k_ref[pl.ds(off, tk), :], et_ref[pl.ds(off, tk), :]], axis=1)
        return _nt_dot(q_aug, k_aug)

    def values(j):
        return v_ref[pl.ds(pl.multiple_of(j * tk, tk), tk), :]

    def near_bias(j):
        return _near_bias(bias_ref, i, j, t, tk, n_near)

    def process(s, j, carry, near):
        return _online_softmax_step(s + (near_bias(j) if near else b_far), values(j), carry)

    qf = q_aug[:, :HEAD_DIM].astype(F32)
    qnorm = jnp.sqrt(jnp.sum(qf * qf, axis=-1, keepdims=True))
    bound = _logit_bound(qnorm, knorm_ref[0:1, 0:1], tab_ref, h)
    l, acc = _shifted_or_running_sweep(i, t, tk, scores, near_bias, values, process, bound, b_far,
                                       sa_ref, sb_ref, t)
    o_ref[...] = (acc / l).astype(o_ref.dtype)


def _paired_loop(n, body, carry):
    carry = lax.fori_loop(0, n // 2, lambda p, c: body(2 * p + 1, body(2 * p, c)), carry)
    return lax.fori_loop(0, n % 2, lambda _, c: body(n - 1, c), carry)


def _logit_bound(qnorm, knorm, tab_ref, h):
    b_max = tab_ref[0, h]
    for b in range(1, N_BUCKETS):
        b_max = jnp.maximum(b_max, tab_ref[b, h])
    return qnorm * knorm * (1.0 + 2.0 ** -9) + (b_max * LOG2E + 2.0 ** -6)


def _shifted_or_running_sweep(i, t, tk, scores, near_bias, values, process, bound, b_far,
                              sa_ref, sb_ref, rows):
    jd = (i * t) // tk
    s_diag = scores(jd) + near_bias(jd)
    slack = jnp.max(bound - jnp.max(s_diag, axis=-1, keepdims=True))

    def shifted_sweep(_):
        def step(s, j, carry, shift):
            l, acc = carry
            p = jnp.exp2(s - shift)
            return (l + jnp.sum(p, axis=-1, keepdims=True),
                    acc + jnp.dot(p.astype(BF16), values(j), preferred_element_type=F32))

        carry = step(s_diag, jd, _softmax_init(rows)[1:], bound)
        lo = jnp.maximum(jd - 1, 0)
        carry = lax.fori_loop(lo, jd, lambda j, c: step(scores(j) + near_bias(j), j, c, bound), carry)
        far_shift = bound - b_far
        return _paired_loop(lo, lambda j, c: step(scores(j), j, c, far_shift), carry)

    def running_max_sweep(_):
        return _biased_sweep(i, t, tk, scores, process, sa_ref, sb_ref, _softmax_init(rows))[1:]

    return lax.cond(slack <= SHIFT_SLACK, shifted_sweep, running_max_sweep, 0)


def _max_key_norm(k_ref, tk, lane_lo, lane_hi):
    klane = lax.broadcasted_iota(jnp.int32, (tk, HEAD_DIM), 1)
    keep = jnp.logical_and(klane >= lane_lo, klane < lane_hi)

    def body(n, c):
        kt = k_ref[pl.ds(pl.multiple_of(n * tk, tk), tk), :].astype(F32)
        sq = jnp.sum(jnp.where(keep, kt * kt, 0.0), axis=-1, keepdims=True)
        return jnp.maximum(c, jnp.max(sq, axis=0, keepdims=True))

    return jnp.sqrt(lax.fori_loop(0, k_ref.shape[0] // tk, body, jnp.zeros((1, 1), F32)))


def _diff_kernel(tab_ref, q_ref, k_ref, v_ref, bias_ref, lq_ref, lk_ref, g_ref, o_ref, sa_ref, sb_ref,
                 knorm_ref, *, t, tk, n_near, scale, lambda_init):
    h = pl.program_id(0)
    i = pl.program_id(1)
    q = _prescale(q_ref[...], scale * LOG2E)
    lane = lax.broadcasted_iota(jnp.int32, (t, HEAD_DIM), 1)
    zero = jnp.zeros_like(q)
    q2 = jnp.concatenate([jnp.where(lane < DIFF_QK_DIM, q, zero),
                          jnp.where(lane >= DIFF_QK_DIM, q, zero)], axis=0)
    b_far = tab_ref[N_BUCKETS - 1, h] * LOG2E

    @pl.when(i == 0)
    def _():
        knorm_ref[0:1, :] = jnp.broadcast_to(_max_key_norm(k_ref, tk, 0, DIFF_QK_DIM), (1, LANES))
        knorm_ref[1:2, :] = jnp.broadcast_to(_max_key_norm(k_ref, tk, DIFF_QK_DIM, HEAD_DIM), (1, LANES))

    def scores(j):
        return _nt_dot(q2, k_ref[pl.ds(pl.multiple_of(j * tk, tk), tk), :])

    def values(j):
        return v_ref[pl.ds(pl.multiple_of(j * tk, tk), tk), :]

    def near_bias(j):
        b = _near_bias(bias_ref, i, j, t, tk, n_near)
        return jnp.concatenate([b, b], axis=0)

    def process(s, j, carry, near):
        return _online_softmax_step(s + (near_bias(j) if near else b_far), values(j), carry)

    q2f = q2.astype(F32)
    qnorm = jnp.sqrt(jnp.sum(q2f * q2f, axis=-1, keepdims=True))
    knorm = jnp.concatenate([jnp.broadcast_to(knorm_ref[0:1, 0:1], (t, 1)),
                             jnp.broadcast_to(knorm_ref[1:2, 0:1], (t, 1))], axis=0)
    bound = _logit_bound(qnorm, knorm, tab_ref, h)
    l, acc = _shifted_or_running_sweep(i, t, tk, scores, near_bias, values, process, bound, b_far,
                                       sa_ref, sb_ref, 2 * t)
    o12 = acc / l
    lam_e = jnp.exp(jnp.sum(lq_ref[...] * lk_ref[...], axis=-1, keepdims=True))
    lam = lam_e[0:1, :] - lam_e[1:2, :] + lambda_init
    o = o12[:t, :] - lam * o12[t:, :]
    o = o * lax.rsqrt(jnp.mean(o * o, axis=-1, keepdims=True) + LN_EPS) * g_ref[...]
    o_ref[...] = (o * (1.0 - lambda_init)).astype(o_ref.dtype)


def _sb_kernel(q_ref, k_ref, v_ref, o_ref, *, t, scale):
    i = pl.program_id(1)
    q = q_ref[...]
    rows = lax.broadcasted_iota(jnp.int32, (t, t), 0)
    cols = lax.broadcasted_iota(jnp.int32, (t, t), 1)
    later = jnp.where(rows > cols, 1.0, 0.0).astype(BF16)

    def block(j, rem, acc, diag):
        k, v, _ = _kv_tile(k_ref, v_ref, j, t)
        z = _nt_dot(q, k) * scale
        soft = jnp.log1p(jnp.exp(-jnp.abs(z)))
        log_sig = jnp.minimum(z, 0.0) - soft
        log_rem = jnp.minimum(-z, 0.0) - soft
        if diag:
            mask = cols < rows
            log_rem = jnp.where(mask, log_rem, 0.0)
        hi, lo = _split_bf16(log_rem)
        after = (jnp.dot(hi, later, preferred_element_type=F32)
                 + jnp.dot(lo, later, preferred_element_type=F32))
        a = jnp.exp(log_sig + after + rem)
        if diag:
            a = jnp.where(mask, a, 0.0)
        acc = acc + jnp.dot(a.astype(BF16), v, preferred_element_type=F32)
        rem = rem + jnp.sum(log_rem, axis=-1, keepdims=True)
        return rem, acc

    rem, acc = block(i, jnp.zeros((t, 1), F32), jnp.zeros((t, HEAD_DIM), F32), True)

    def live(rem):
        return (jnp.max(rem) > SB_CUTOFF).astype(jnp.int32)

    def cond(c):
        j, go, _, _ = c
        return jnp.logical_and(j >= 0, go > 0)

    def body(c):
        j, _, rem, acc = c
        rem, acc = block(j, rem, acc, False)
        return j - 1, live(rem), rem, acc

    _, _, _, acc = lax.while_loop(cond, body, (i - 1, live(rem), rem, acc))
    o_ref[...] = acc.astype(o_ref.dtype)


def _attention_call(kernel, h, out_cols, col0, extra_in, extra_specs, scratch, name, t):
    s = h.shape[0]
    nh = N_HEADS_GROUP
    in_specs = list(extra_specs[0]) + [
        pl.BlockSpec((t, HEAD_DIM), lambda hd, i: (i, col0 + hd)),
        pl.BlockSpec((s, HEAD_DIM), lambda hd, i: (0, col0 + nh + hd)),
        pl.BlockSpec((s, HEAD_DIM), lambda hd, i: (0, col0 + 2 * nh + hd)),
    ] + list(extra_specs[1])
    args = list(extra_in[0]) + [h, h, h] + list(extra_in[1])
    return pl.pallas_call(
        kernel,
        out_shape=jax.ShapeDtypeStruct((s, out_cols), BF16),
        grid=(nh, s // t),
        in_specs=in_specs,
        out_specs=pl.BlockSpec((t, HEAD_DIM), lambda hd, i: (i, hd)),
        scratch_shapes=scratch,
        compiler_params=_cparams(2, VMEM_LIMIT),
        name=name,
    )(*args)


def _router_kernel(x_ref, w_ref, b_ref, o_ref):
    logits = jnp.dot(x_ref[...], w_ref[...], preferred_element_type=F32,
                     precision=lax.Precision.HIGHEST) + b_ref[...]
    tm, n = logits.shape
    lane = lax.broadcasted_iota(jnp.int32, (tm, n), 1)
    real = lane < N_EXPERTS
    logits = jnp.where(real, logits, -jnp.inf)
    e = jnp.exp(logits - jnp.max(logits, axis=-1, keepdims=True))
    aff = e / jnp.sum(e, axis=-1, keepdims=True)
    group = lane // EXPERTS_PER_GROUP
    best = jnp.full((tm, 1), -jnp.inf, F32)
    g_sel = jnp.zeros((tm, 1), jnp.int32)
    for g in range(N_GROUPS):
        gmax = jnp.max(jnp.where(group == g, aff, -jnp.inf), axis=-1, keepdims=True)
        better = gmax > best
        g_sel = jnp.where(better, g, g_sel)
        best = jnp.where(better, gmax, best)
    in_group = jnp.logical_and(group == g_sel, real)
    cand = jnp.where(in_group, aff, -jnp.inf)
    v1 = jnp.max(cand, axis=-1, keepdims=True)
    i1 = jnp.min(jnp.where(cand == v1, lane, n), axis=-1, keepdims=True)
    cand2 = jnp.where(lane == i1, -jnp.inf, cand)
    v2 = jnp.max(cand2, axis=-1, keepdims=True)
    i2 = jnp.min(jnp.where(cand2 == v2, lane, n), axis=-1, keepdims=True)
    tot = v1 + v2
    o_ref[...] = jnp.where(lane == 0, v1 / tot,
                           jnp.where(lane == 1, v2 / tot,
                                     jnp.where(lane == 2, i1.astype(F32),
                                               jnp.where(lane == 3, i2.astype(F32), 0.0))))


def _router(x, w_router, b_router, tm):
    s, d = x.shape
    w = jnp.zeros((d, LANES), F32).at[:, :N_EXPERTS].set(w_router)
    b = jnp.zeros((1, LANES), F32).at[0, :N_EXPERTS].set(b_router)
    return pl.pallas_call(
        _router_kernel,
        out_shape=jax.ShapeDtypeStruct((s, LANES), F32),
        grid=(s // tm,),
        in_specs=[pl.BlockSpec((tm, d), lambda i: (i, 0)),
                  pl.BlockSpec((d, LANES), lambda i: (0, 0)),
                  pl.BlockSpec((1, LANES), lambda i: (0, 0))],
        out_specs=pl.BlockSpec((tm, LANES), lambda i: (i, 0)),
        compiler_params=_cparams(1, VMEM_LIMIT),
        name="router",
    )(x, w, b)


def _dispatch_plan(route, tm):
    s = route.shape[0]
    n_tiles = (MOE_TOP_K * s) // tm + N_EXPERTS
    expert = route[:, 2:2 + MOE_TOP_K].astype(jnp.int32).reshape(-1)
    onehot = (expert[:, None] == jnp.arange(N_EXPERTS, dtype=jnp.int32)[None, :]).astype(jnp.int32)
    running = jnp.cumsum(onehot, axis=0)
    rank = jnp.take_along_axis(running, expert[:, None], axis=1)[:, 0] - 1
    padded = -(-running[-1] // tm) * tm
    seg_end = jnp.cumsum(padded)
    pair_row = (seg_end - padded)[expert] + rank
    tile_start = jnp.arange(n_tiles, dtype=jnp.int32) * tm
    tile_expert = jnp.minimum(jnp.searchsorted(seg_end, tile_start, side="right"),
                              N_EXPERTS - 1).astype(jnp.int32)
    tile_live = (tile_start < seg_end[-1]).astype(jnp.int32)
    row_token = (jnp.arange(n_tiles * tm, dtype=jnp.int32) % s).at[pair_row].set(
        jnp.arange(MOE_TOP_K * s, dtype=jnp.int32) // MOE_TOP_K)
    return pair_row.astype(jnp.int32), row_token, tile_expert, tile_live


def _row_gather_kernel(idx_ref, src_ref, o_ref, sem, *, chunk):
    def issue(r, _):
        pltpu.make_async_copy(src_ref.at[pl.ds(idx_ref[0, r], 1)],
                              o_ref.at[pl.ds(r, 1)], sem).start()
        return 0

    lax.fori_loop(0, chunk, issue, 0, unroll=8)
    pltpu.make_async_copy(src_ref.at[pl.ds(0, chunk)], o_ref, sem).wait()


def _row_gather(src, idx, chunk):
    n = idx.shape[0]
    d = src.shape[1]
    return pl.pallas_call(
        functools.partial(_row_gather_kernel, chunk=chunk),
        out_shape=jax.ShapeDtypeStruct((n, d), src.dtype),
        grid=(n // chunk,),
        in_specs=[pl.BlockSpec((None, 1, chunk), lambda c: (c, 0, 0), memory_space=pltpu.SMEM),
                  pl.BlockSpec(memory_space=pl.ANY)],
        out_specs=pl.BlockSpec((chunk, d), lambda c: (c, 0)),
        scratch_shapes=[pltpu.SemaphoreType.DMA(())],
        compiler_params=_cparams(1, VMEM_LIMIT),
        name="row_gather",
    )(idx.reshape(n // chunk, 1, chunk), src)


def _moe_group_kernel(te_ref, live_ref, x_ref, wg_ref, wu_ref, wd_ref, o_ref, xb_ref):
    g = pl.program_id(0)
    f = pl.program_id(1)
    live = live_ref[g] > 0

    @pl.when(jnp.logical_and(live, f == 0))
    def _():
        xb_ref[...] = x_ref[...].astype(BF16)

    @pl.when(jnp.logical_and(jnp.logical_not(live), f == 0))
    def _():
        o_ref[...] = jnp.zeros(o_ref.shape, F32)

    @pl.when(live)
    def _():
        xb = xb_ref[...]
        a = jnp.dot(xb, wg_ref[...], preferred_element_type=F32)
        u = jnp.dot(xb, wu_ref[...], preferred_element_type=F32)
        hid = ((a * jax.nn.sigmoid(a)) * u).astype(BF16)
        y = jnp.dot(hid, wd_ref[...], preferred_element_type=F32)

        @pl.when(f == 0)
        def _():
            o_ref[...] = y

        @pl.when(f > 0)
        def _():
            o_ref[...] += y


def _moe_group_ffn(xs, tile_expert, tile_live, wg, wu, wd, tm, tf):
    rows, d = xs.shape
    fdim = wg.shape[2]
    grid_spec = pltpu.PrefetchScalarGridSpec(
        num_scalar_prefetch=2,
        grid=(rows // tm, fdim // tf),
        in_specs=[pl.BlockSpec((tm, d), lambda g, f, te, lv: (g, 0)),
                  pl.BlockSpec((None, d, tf), lambda g, f, te, lv: (te[g], 0, f)),
                  pl.BlockSpec((None, d, tf), lambda g, f, te, lv: (te[g], 0, f)),
                  pl.BlockSpec((None, tf, d), lambda g, f, te, lv: (te[g], f, 0))],
        out_specs=pl.BlockSpec((tm, d), lambda g, f, te, lv: (g, 0)),
        scratch_shapes=[pltpu.VMEM((tm, d), BF16)])
    return pl.pallas_call(
        _moe_group_kernel,
        out_shape=jax.ShapeDtypeStruct((rows, d), F32),
        grid_spec=grid_spec,
        compiler_params=_cparams(2, VMEM_LIMIT),
        name="moe_group_ffn",
    )(tile_expert, tile_live, xs, wg, wu, wd)


def _moe_combine_ln_kernel(x_ref, r_ref, y0_ref, y1_ref, g_ref, b_ref, o_ref):
    r = r_ref[...]
    ff = r[:, 0:1] * y0_ref[...] + r[:, 1:2] * y1_ref[...]
    o_ref[...] = _layer_norm(ALPHA * x_ref[...] + ff, g_ref[...], b_ref[...])


def _moe_combine_ln(x, route, y, g, b, tm):
    s, d = x.shape
    nt = s // tm
    return pl.pallas_call(
        _moe_combine_ln_kernel,
        out_shape=jax.ShapeDtypeStruct((s, d), F32),
        grid=(nt,),
        in_specs=[pl.BlockSpec((tm, d), lambda i: (i, 0)),
                  pl.BlockSpec((tm, LANES), lambda i: (i, 0)),
                  pl.BlockSpec((tm, d), lambda i: (i, 0)),
                  pl.BlockSpec((tm, d), lambda i: (i + nt, 0)),
                  pl.BlockSpec((1, d), lambda i: (0, 0)),
                  pl.BlockSpec((1, d), lambda i: (0, 0))],
        out_specs=pl.BlockSpec((tm, d), lambda i: (i, 0)),
        compiler_params=_cparams(1, VMEM_LIMIT),
        name="moe_combine_ln",
    )(x, route, y, y, g.reshape(1, d), b.reshape(1, d))


def _moe_ln(x, route, wg, wu, wd, g, b, tm):
    s = x.shape[0]
    pair_row, row_token, tile_expert, tile_live = _dispatch_plan(route, tm)
    xs = _row_gather(x, row_token, MOE_GATHER_CHUNK)
    ys = _moe_group_ffn(xs, tile_expert, tile_live, wg, wu, wd, tm, D_EXPERT // 2)
    slot_major = pair_row.reshape(s, MOE_TOP_K).T.reshape(-1)
    y = _row_gather(ys, slot_major, MOE_GATHER_CHUNK)
    return _moe_combine_ln(x, route, y, g, b, tm)


def kernel(x, w_in_even, w_out_even, forget_bias, w_in_odd, w_out_odd, lambda_q, lambda_k,
           subln_gain, rel_bias, w_router, b_router, w_gate, w_up, w_down, ln_gain, ln_bias):
    bsz, s, d = x.shape
    assert bsz == 1 and d == D_MODEL and s % MOBA_BLOCK == 0
    t = ATT_TILE
    nh = N_HEADS_GROUP
    gw = GROUP_WIDTH
    tm_proj = min(1024, s)
    tm_row = min(512, s)
    tk = min(ATT_KEYS, s)
    n_near = _near_tile_count(t)
    assert s % tk == 0 and tk // t + 1 >= n_near
    nblk = s // MOBA_BLOCK
    nbp = -(-nblk // LANES) * LANES

    x2 = x.reshape(s, d)
    bias_tiles = _bias_tiles(rel_bias, t, n_near)
    tab_spec = pl.BlockSpec(memory_space=pltpu.SMEM)
    bias_spec = pl.BlockSpec((None, n_near + 2, t, t), lambda hd, i: (hd, 0, 0, 0))
    block_onehot = (jnp.arange(s, dtype=jnp.int32)[:, None] // MOBA_BLOCK
                    == jnp.arange(nbp, dtype=jnp.int32)[None, :]).astype(BF16)

    for layer in range(DEPTH):
        li = layer // 2
        if layer % 2 == 0:
            w = w_in_even[li]
            fcol = 3 * gw
            w_main = jnp.concatenate([w[:, :fcol], w[:, fcol + nh:]], axis=1).astype(BF16)
            w_f = jnp.zeros((d, LANES), BF16).at[:, :nh].set(w[:, fcol:fcol + nh].astype(BF16))
            h = _in_proj(x2, w_main, BF16, tm_proj, 512)
            f = _in_proj(x2, w_f, F32, tm_proj, LANES)
            c = _decay_cumsum(f[:, :nh].T, forget_bias[li])
            oa = _attention_call(
                functools.partial(_fox_kernel, t=FOX_TQ, tk=min(FOX_TK, s), scale=HEAD_DIM ** -0.5), h, gw, 0,
                ([c[:, min(FOX_TK, s) - 1::min(FOX_TK, s)], c[:, ::FOX_TQ]],
                 [c.reshape(nh, 1, s), c.reshape(nh, s, 1)]),
                ([tab_spec, tab_spec],
                 [pl.BlockSpec((None, 1, s), lambda hd, i: (hd, 0, 0)),
                  pl.BlockSpec((None, FOX_TQ, 1), lambda hd, i: (hd, i, 0))]),
                [pltpu.VMEM((FOX_TQ, min(FOX_TK, s)), F32)] * 2 + [pltpu.VMEM((8, LANES), F32)],
                "fox_attention", FOX_TQ)
            ob = _attention_call(
                functools.partial(_moba_kernel, t=t, tk=tk, nblk=nblk, n_near=n_near,
                                  scale=HEAD_DIM ** -0.5),
                h, gw, 3 * nh,
                ([rel_bias], [bias_tiles, block_onehot]),
                ([tab_spec], [bias_spec, pl.BlockSpec((s, nbp), lambda hd, i: (0, 0))]),
                [pltpu.VMEM((nbp, HEAD_DIM), F32)] + [pltpu.VMEM((t, tk), F32)] * 2
                + [pltpu.VMEM((8, LANES), F32)], "moba_attention", t)
            w_out = w_out_even[li]
        else:
            lambda_init = 0.8 - 0.6 * math.exp(-0.3 * layer)
            h = _in_proj(x2, w_in_odd[li].astype(BF16), BF16, tm_proj, 512)
            small = lambda shape: pl.BlockSpec(shape, lambda hd, i: (0, 0))
            oa = _attention_call(
                functools.partial(_diff_kernel, t=t, tk=tk, n_near=n_near, scale=DIFF_QK_DIM ** -0.5,
                                  lambda_init=lambda_init),
                h, gw, 0,
                ([rel_bias], [bias_tiles, lambda_q[li], lambda_k[li], subln_gain[li].reshape(1, HEAD_DIM)]),
                ([tab_spec], [bias_spec, small((2, DIFF_QK_DIM)), small((2, DIFF_QK_DIM)),
                              small((1, HEAD_DIM))]),
                [pltpu.VMEM((2 * t, tk), F32)] * 2 + [pltpu.VMEM((8, LANES), F32)], "diff_attention", t)
            ob = _attention_call(
                functools.partial(_sb_kernel, t=t, scale=HEAD_DIM ** -0.5), h, gw, 3 * nh,
                ([], []), ([], []), [], "stickbreak_attention", t)
            w_out = w_out_odd[li]
        x2 = _out_proj_ln(oa, ob, w_out.astype(BF16), x2, ln_gain[layer, 0], ln_bias[layer, 0], tm_row)
        route = _router(x2, w_router, b_router, tm_row)
        x2 = _moe_ln(x2, route, w_gate[layer].astype(BF16), w_up[layer].astype(BF16),
                     w_down[layer].astype(BF16), ln_gain[layer, 1], ln_bias[layer, 1], tm_row)
    return x2.reshape(bsz, s, d)
```

```python
import functools
import math

import numpy as np
import jax
import jax.numpy as jnp
from jax import lax
from jax.experimental import pallas as pl
from jax.experimental.pallas import tpu as pltpu

F32 = jnp.float32
BF16 = jnp.bfloat16

D_MODEL = 2048
DEPTH = 2
HEAD_DIM = 128
N_HEADS_GROUP = 8
GROUP_WIDTH = N_HEADS_GROUP * HEAD_DIM
DIFF_QK_DIM = HEAD_DIM // 2
MOBA_BLOCK = 256
MOBA_TOPK = 3
N_BUCKETS = 32
MAX_EXACT = N_BUCKETS // 2
MAX_DISTANCE = 1024
N_EXPERTS = 16
N_GROUPS = 4
EXPERTS_PER_GROUP = N_EXPERTS // N_GROUPS
MOE_TOP_K = 2
MOE_GATHER_CHUNK = 512
SUBLANES = 8
D_EXPERT = D_MODEL // 2
ALPHA = (2.0 * DEPTH) ** 0.25
LN_EPS = 1e-5
NEG_INF = -1e30
LOG2E = math.log2(math.e)
LANES = 128
ATT_TILE = 256
ATT_KEYS = 1024
FOX_TQ = 512
FOX_TK = 1024
SHIFT_SLACK = 96.0
FLUSH_EXPONENT = -130.0
SB_CUTOFF = -104.0
VMEM_LIMIT = 56 * 1024 * 1024


def _cparams(n_axes, vmem=None):
    return pltpu.CompilerParams(dimension_semantics=("arbitrary",) * n_axes,
                                vmem_limit_bytes=vmem)


def _nt_dot(a, b):
    return lax.dot_general(a, b, (((1,), (1,)), ((), ())), preferred_element_type=F32)


def _split_bf16(x):
    hi = x.astype(BF16)
    lo = (x - hi.astype(F32)).astype(BF16)
    return hi, lo


def _layer_norm(z, g, b):
    mu = jnp.mean(z, axis=-1, keepdims=True)
    zc = z - mu
    var = jnp.mean(zc * zc, axis=-1, keepdims=True)
    return zc * lax.rsqrt(var + LN_EPS) * g + b


def _inproj_kernel(x_ref, w_ref, o_ref, xb_ref):
    @pl.when(pl.program_id(1) == 0)
    def _():
        xb_ref[...] = x_ref[...].astype(BF16)

    o_ref[...] = jnp.dot(xb_ref[...], w_ref[...], preferred_element_type=F32).astype(o_ref.dtype)


def _in_proj(x, w, out_dtype, tm, tn):
    s, d = x.shape
    n = w.shape[1]
    return pl.pallas_call(
        _inproj_kernel,
        out_shape=jax.ShapeDtypeStruct((s, n), out_dtype),
        grid=(s // tm, n // tn),
        in_specs=[pl.BlockSpec((tm, d), lambda i, j: (i, 0)),
                  pl.BlockSpec((d, tn), lambda i, j: (0, j))],
        out_specs=pl.BlockSpec((tm, tn), lambda i, j: (i, j)),
        scratch_shapes=[pltpu.VMEM((tm, d), BF16)],
        compiler_params=_cparams(2, VMEM_LIMIT),
        name="in_proj",
    )(x, w)


def _outproj_ln_kernel(oa_ref, ob_ref, wa_ref, wb_ref, x_ref, g_ref, b_ref, o_ref):
    y = jnp.dot(oa_ref[...], wa_ref[...], preferred_element_type=F32)
    y = y + jnp.dot(ob_ref[...], wb_ref[...], preferred_element_type=F32)
    o_ref[...] = _layer_norm(ALPHA * x_ref[...] + y, g_ref[...], b_ref[...])


def _out_proj_ln(oa, ob, w, x, g, b, tm):
    s, d = x.shape
    gw = oa.shape[1]
    return pl.pallas_call(
        _outproj_ln_kernel,
        out_shape=jax.ShapeDtypeStruct((s, d), F32),
        grid=(s // tm,),
        in_specs=[pl.BlockSpec((tm, gw), lambda i: (i, 0)),
                  pl.BlockSpec((tm, gw), lambda i: (i, 0)),
                  pl.BlockSpec((gw, d), lambda i: (0, 0)),
                  pl.BlockSpec((gw, d), lambda i: (1, 0)),
                  pl.BlockSpec((tm, d), lambda i: (i, 0)),
                  pl.BlockSpec((1, d), lambda i: (0, 0)),
                  pl.BlockSpec((1, d), lambda i: (0, 0))],
        out_specs=pl.BlockSpec((tm, d), lambda i: (i, 0)),
        compiler_params=_cparams(1, VMEM_LIMIT),
        name="out_proj_ln",
    )(oa, ob, w, w, x, g.reshape(1, d), b.reshape(1, d))


def _router_kernel(x_ref, w_ref, b_ref, route_ref, count_ref):
    route, counts = _route(x_ref[...], w_ref[...], b_ref[...])
    route_ref[...] = route
    count_ref[...] = jnp.broadcast_to(counts, count_ref.shape)


def _router(x, w_router, b_router, tm):
    s, d = x.shape
    wr = jnp.zeros((d, LANES), F32).at[:, :N_EXPERTS].set(w_router)
    br = jnp.zeros((1, LANES), F32).at[0, :N_EXPERTS].set(b_router)
    return pl.pallas_call(
        _router_kernel,
        out_shape=(jax.ShapeDtypeStruct((s, LANES), F32),
                   jax.ShapeDtypeStruct((s // tm, SUBLANES, LANES), F32)),
        grid=(s // tm,),
        in_specs=[pl.BlockSpec((tm, d), lambda i: (i, 0)),
                  pl.BlockSpec((d, LANES), lambda i: (0, 0)),
                  pl.BlockSpec((1, LANES), lambda i: (0, 0))],
        out_specs=(pl.BlockSpec((tm, LANES), lambda i: (i, 0)),
                   pl.BlockSpec((None, SUBLANES, LANES), lambda i: (i, 0, 0))),
        compiler_params=_cparams(1, VMEM_LIMIT),
        name="router",
    )(x, wr, br)


def _decay_cumsum_kernel(f_ref, b_ref, c_ref):
    nh, s = f_ref.shape
    rows = lax.broadcasted_iota(jnp.int32, (LANES, LANES), 0)
    cols = lax.broadcasted_iota(jnp.int32, (LANES, LANES), 1)
    upper = jnp.where(rows <= cols, 1.0, 0.0).astype(F32)
    bias = b_ref[...]

    def body(n, carry):
        off = pl.multiple_of(n * LANES, LANES)
        z = f_ref[:, pl.ds(off, LANES)] + bias
        logf = jnp.minimum(z, 0.0) - jnp.log1p(jnp.exp(-jnp.abs(z)))
        c = jnp.dot(logf, upper, preferred_element_type=F32,
                    precision=lax.Precision.HIGHEST) + carry
        c_ref[:, pl.ds(off, LANES)] = c * LOG2E
        return carry + jnp.sum(logf, axis=-1, keepdims=True)

    lax.fori_loop(0, s // LANES, body, jnp.zeros((nh, 1), F32))


def _decay_cumsum(f_t, bias):
    nh, s = f_t.shape
    return pl.pallas_call(
        _decay_cumsum_kernel,
        out_shape=jax.ShapeDtypeStruct((nh, s), F32),
        name="decay_cumsum",
    )(f_t, bias.reshape(nh, 1))


def _t5_bucket_np(dist):
    n = np.maximum(dist, 0)
    nf = np.maximum(n, 1).astype(np.float32)
    ratio = np.log(nf / np.float32(MAX_EXACT)) / np.float32(math.log(MAX_DISTANCE / MAX_EXACT))
    large = MAX_EXACT + (ratio.astype(np.float32) * np.float32(N_BUCKETS - MAX_EXACT)).astype(np.int32)
    large = np.minimum(large, N_BUCKETS - 1)
    return np.where(n < MAX_EXACT, n, large).astype(np.int32)


def _near_tile_count(t):
    d = np.arange(0, 4 * MAX_DISTANCE, dtype=np.int64)
    not_last = np.nonzero(_t5_bucket_np(d) != N_BUCKETS - 1)[0]
    d_sat = int(not_last.max()) + 1
    n = 1
    while (n - 1) * t + 1 < d_sat:
        n += 1
    return n


def _bucket_tiles_np(t, n_near):
    r = np.arange(t)[:, None]
    c = np.arange(t)[None, :]
    tiles = []
    for delta in range(n_near):
        dist = delta * t + r - c
        tiles.append(np.where(dist >= 0, _t5_bucket_np(dist), -1))
    tiles.append(np.full((t, t), N_BUCKETS - 1))
    tiles.append(np.full((t, t), -1))
    return np.stack(tiles).astype(np.int32)


def _bias_tiles_kernel(tab_ref, idx_ref, o_ref):
    h = pl.program_id(0)
    idx = idx_ref[...]
    out = jnp.full(idx.shape, NEG_INF, F32)
    for b in range(N_BUCKETS):
        out = jnp.where(idx == b, tab_ref[b, h] * LOG2E, out)
    o_ref[...] = out


def _bias_tiles(rel_bias, t, n_near):
    nh = rel_bias.shape[1]
    idx = jnp.asarray(_bucket_tiles_np(t, n_near))
    n_tiles = idx.shape[0]
    return pl.pallas_call(
        _bias_tiles_kernel,
        out_shape=jax.ShapeDtypeStruct((nh, n_tiles, t, t), F32),
        grid=(nh, n_tiles),
        in_specs=[pl.BlockSpec(memory_space=pltpu.SMEM),
                  pl.BlockSpec((None, t, t), lambda h, n: (n, 0, 0))],
        out_specs=pl.BlockSpec((None, None, t, t), lambda h, n: (h, n, 0, 0)),
        compiler_params=_cparams(2),
        name="bias_tiles",
    )(rel_bias, idx)


def _online_softmax_step(s, v, carry):
    m, l, acc = carry
    m_new = jnp.maximum(m, jnp.max(s, axis=-1, keepdims=True))
    alpha = jnp.exp2(m - m_new)
    p = jnp.exp2(s - m_new)
    l = alpha * l + jnp.sum(p, axis=-1, keepdims=True)
    acc = alpha * acc + jnp.dot(p.astype(BF16), v, preferred_element_type=F32)
    return m_new, l, acc


def _prescale(q, c):
    return (q.astype(F32) * c).astype(BF16)


def _near_bias(bias_ref, i, j, t, tk, n_near):
    nsub = tk // t
    tiles = []
    for b in range(nsub):
        delta = i - (j * nsub + b)
        tiles.append(bias_ref[jnp.where(delta < 0, n_near + 1, jnp.minimum(delta, n_near))])
    return jnp.concatenate(tiles, axis=1)


def _biased_sweep(i, t, tk, scores, process, sa_ref, sb_ref, init):
    jd = (i * t) // tk
    carry = process(scores(jd), jd, init, True)
    lo = jnp.maximum(jd - 1, 0)
    carry = lax.fori_loop(lo, jd, lambda j, c: process(scores(j), j, c, True), carry)
    return _pipelined_sweep(lo, scores, lambda s, j, c: process(s, j, c, False), sa_ref, sb_ref, carry)


def _softmax_init(rows):
    return (jnp.full((rows, 1), NEG_INF, F32), jnp.zeros((rows, 1), F32),
            jnp.zeros((rows, HEAD_DIM), F32))


def _kv_tile(k_ref, v_ref, j, t):
    off = pl.multiple_of(j * t, t)
    return k_ref[pl.ds(off, t), :], v_ref[pl.ds(off, t), :], off


def _pipelined_sweep(n, scores, process, sa_ref, sb_ref, carry):
    last = jnp.maximum(n - 1, 0)
    sa_ref[...] = scores(0)

    def pair(p, c):
        j = 2 * p
        sb_ref[...] = scores(jnp.minimum(j + 1, last))
        c = process(sa_ref[...], j, c)
        sa_ref[...] = scores(jnp.minimum(j + 2, last))
        return process(sb_ref[...], j + 1, c)

    carry = lax.fori_loop(0, n // 2, pair, carry)
    return lax.fori_loop(0, n % 2, lambda _, c: process(sa_ref[...], n - 1, c), carry)


def _fox_kernel(cend_ref, cstart_ref, q_ref, k_ref, v_ref, c_ref, ccol_ref, o_ref, sa_ref, sb_ref,
                knorm_ref, *, t, tk, scale):
    h = pl.program_id(0)
    i = pl.program_id(1)

    @pl.when(i == 0)
    def _():
        knorm_ref[...] = jnp.broadcast_to(_max_key_norm(k_ref, tk, 0, HEAD_DIM), knorm_ref.shape)

    q = _prescale(q_ref[...], scale * LOG2E)
    rows = lax.broadcasted_iota(jnp.int32, (t, tk), 0)
    cols = lax.broadcasted_iota(jnp.int32, (t, tk), 1)

    def scores(j):
        off = pl.multiple_of(j * tk, tk)
        return _nt_dot(q, k_ref[pl.ds(off, tk), :]) - c_ref[:, pl.ds(off, tk)]

    def values(j):
        return v_ref[pl.ds(pl.multiple_of(j * tk, tk), tk), :]

    def process(s, j, carry):
        return _online_softmax_step(s, values(j), carry)

    jd = (i * t) // tk
    s_diag = jnp.where(cols <= rows + (i * t - jd * tk), scores(jd), NEG_INF)

    qf = q.astype(F32)
    qnorm = jnp.sqrt(jnp.sum(qf * qf, axis=-1, keepdims=True))
    bound = qnorm * knorm_ref[0:1, 0:1] * (1.0 + 2.0 ** -9) + 2.0 ** -6 - ccol_ref[...]
    slack = jnp.max(bound - jnp.max(s_diag, axis=-1, keepdims=True))

    def shifted_sweep(_):
        def step(s, j, carry):
            l, acc = carry
            p = jnp.exp2(s - bound)
            return (l + jnp.sum(p, axis=-1, keepdims=True),
                    acc + jnp.dot(p.astype(BF16), values(j), preferred_element_type=F32))

        c_first_row = cstart_ref[h, i]
        first_live = jnp.int32(0)
        for j in range(cend_ref.shape[1]):
            dead = jnp.logical_and(j < jd, c_first_row - cend_ref[h, j] < FLUSH_EXPONENT)
            first_live = first_live + dead.astype(jnp.int32)
        carry = step(s_diag, jd, _softmax_init(t)[1:])
        return _paired_loop(jd - first_live, lambda n, c: step(scores(first_live + n), first_live + n, c),
                            carry)

    def running_max_sweep(_):
        carry = _online_softmax_step(s_diag, values(jd), _softmax_init(t))
        return _pipelined_sweep(jd, scores, process, sa_ref, sb_ref, carry)[1:]

    l, acc = lax.cond(slack <= SHIFT_SLACK, shifted_sweep, running_max_sweep, 0)
    o_ref[...] = (acc / l).astype(o_ref.dtype)


def _moba_kernel(tab_ref, q_ref, k_ref, v_ref, bias_ref, et_ref, o_ref, kmean_ref, sa_ref, sb_ref,
                 knorm_ref, *, t, tk, nblk, n_near, scale):
    h = pl.program_id(0)
    i = pl.program_id(1)
    nbp = kmean_ref.shape[0]

    @pl.when(i == 0)
    def _():
        kmean_ref[...] = jnp.zeros(kmean_ref.shape, F32)

        def mean_body(n, _):
            off = pl.multiple_of(n * t, t)
            kb = k_ref[pl.ds(off, t), :].astype(F32)
            kmean_ref[pl.ds(n, 1), :] = jnp.sum(kb, axis=0, keepdims=True) * (1.0 / t)
            return 0

        lax.fori_loop(0, nblk, mean_body, 0)
        knorm_ref[...] = jnp.broadcast_to(_max_key_norm(k_ref, tk, 0, HEAD_DIM), knorm_ref.shape)

    q = q_ref[...]
    km_hi, km_lo = _split_bf16(kmean_ref[...])
    gate = _nt_dot(q, km_hi) + _nt_dot(q, km_lo)
    blk = lax.broadcasted_iota(jnp.int32, (t, nbp), 1)
    eligible = blk < i
    sel = jnp.zeros((t, nbp), F32)
    for r in range(MOBA_TOPK):
        cand = jnp.where(eligible, jnp.where(sel > 0.0, -jnp.inf, gate), -jnp.inf)
        vmax = jnp.max(cand, axis=-1, keepdims=True)
        first = jnp.min(jnp.where(cand == vmax, blk, nbp), axis=-1, keepdims=True)
        take = jnp.where(r < i, 1.0, 0.0)
        sel = jnp.maximum(sel, jnp.where(blk == first, take, 0.0))

    sel = jnp.maximum(sel, jnp.where(blk == i, 1.0, 0.0))
    penalty = jnp.where(sel > 0.0, 0.0, NEG_INF).astype(BF16)
    q_aug = jnp.concatenate([_prescale(q, scale * LOG2E), penalty], axis=1)
    b_far = tab_ref[N_BUCKETS - 1, h] * LOG2E

    def scores(j):
        off = pl.multiple_of(j * tk, tk)
        k_aug = jnp.concatenate([k_ref[pl.ds(off, tk), :], et_ref[pl.ds(off, tk), :]], axis=1)
        return _nt_dot(q_aug, k_aug)

    def values(j):
        return v_ref[pl.ds(pl.multiple_of(j * tk, tk), tk), :]

    def near_bias(j):
        return _near_bias(bias_ref, i, j, t, tk, n_near)

    def process(s, j, carry, near):
        return _online_softmax_step(s + (near_bias(j) if near else b_far), values(j), carry)

    qf = q_aug[:, :HEAD_DIM].astype(F32)
    qnorm = jnp.sqrt(jnp.sum(qf * qf, axis=-1, keepdims=True))
    bound = _logit_bound(qnorm, knorm_ref[0:1, 0:1], tab_ref, h)
    l, acc = _shifted_or_running_sweep(i, t, tk, scores, near_bias, values, process, bound, b_far,
                                       sa_ref, sb_ref, t)
    o_ref[...] = (acc / l).astype(o_ref.dtype)


def _paired_loop(n, body, carry):
    carry = lax.fori_loop(0, n // 2, lambda p, c: body(2 * p + 1, body(2 * p, c)), carry)
    return lax.fori_loop(0, n % 2, lambda _, c: body(n - 1, c), carry)


def _logit_bound(qnorm, knorm, tab_ref, h):
    b_max = tab_ref[0, h]
    for b in range(1, N_BUCKETS):
        b_max = jnp.maximum(b_max, tab_ref[b, h])
    return qnorm * knorm * (1.0 + 2.0 ** -9) + (b_max * LOG2E + 2.0 ** -6)


def _shifted_or_running_sweep(i, t, tk, scores, near_bias, values, process, bound, b_far,
                              sa_ref, sb_ref, rows):
    jd = (i * t) // tk
    s_diag = scores(jd) + near_bias(jd)
    slack = jnp.max(bound - jnp.max(s_diag, axis=-1, keepdims=True))

    def shifted_sweep(_):
        def step(s, j, carry, shift):
            l, acc = carry
            p = jnp.exp2(s - shift)
            return (l + jnp.sum(p, axis=-1, keepdims=True),
                    acc + jnp.dot(p.astype(BF16), values(j), preferred_element_type=F32))

        carry = step(s_diag, jd, _softmax_init(rows)[1:], bound)
        lo = jnp.maximum(jd - 1, 0)
        carry = lax.fori_loop(lo, jd, lambda j, c: step(scores(j) + near_bias(j), j, c, bound), carry)
        far_shift = bound - b_far
        return _paired_loop(lo, lambda j, c: step(scores(j), j, c, far_shift), carry)

    def running_max_sweep(_):
        return _biased_sweep(i, t, tk, scores, process, sa_ref, sb_ref, _softmax_init(rows))[1:]

    return lax.cond(slack <= SHIFT_SLACK, shifted_sweep, running_max_sweep, 0)


def _max_key_norm(k_ref, tk, lane_lo, lane_hi):
    klane = lax.broadcasted_iota(jnp.int32, (tk, HEAD_DIM), 1)
    keep = jnp.logical_and(klane >= lane_lo, klane < lane_hi)

    def body(n, c):
        kt = k_ref[pl.ds(pl.multiple_of(n * tk, tk), tk), :].astype(F32)
        sq = jnp.sum(jnp.where(keep, kt * kt, 0.0), axis=-1, keepdims=True)
        return jnp.maximum(c, jnp.max(sq, axis=0, keepdims=True))

    return jnp.sqrt(lax.fori_loop(0, k_ref.shape[0] // tk, body, jnp.zeros((1, 1), F32)))


def _diff_kernel(tab_ref, q_ref, k_ref, v_ref, bias_ref, lq_ref, lk_ref, g_ref, o_ref, sa_ref, sb_ref,
                 knorm_ref, *, t, tk, n_near, scale, lambda_init):
    h = pl.program_id(0)
    i = pl.program_id(1)
    q = _prescale(q_ref[...], scale * LOG2E)
    lane = lax.broadcasted_iota(jnp.int32, (t, HEAD_DIM), 1)
    zero = jnp.zeros_like(q)
    q2 = jnp.concatenate([jnp.where(lane < DIFF_QK_DIM, q, zero),
                          jnp.where(lane >= DIFF_QK_DIM, q, zero)], axis=0)
    b_far = tab_ref[N_BUCKETS - 1, h] * LOG2E

    @pl.when(i == 0)
    def _():
        knorm_ref[0:1, :] = jnp.broadcast_to(_max_key_norm(k_ref, tk, 0, DIFF_QK_DIM), (1, LANES))
        knorm_ref[1:2, :] = jnp.broadcast_to(_max_key_norm(k_ref, tk, DIFF_QK_DIM, HEAD_DIM), (1, LANES))

    def scores(j):
        return _nt_dot(q2, k_ref[pl.ds(pl.multiple_of(j * tk, tk), tk), :])

    def values(j):
        return v_ref[pl.ds(pl.multiple_of(j * tk, tk), tk), :]

    def near_bias(j):
        b = _near_bias(bias_ref, i, j, t, tk, n_near)
        return jnp.concatenate([b, b], axis=0)

    def process(s, j, carry, near):
        return _online_softmax_step(s + (near_bias(j) if near else b_far), values(j), carry)

    q2f = q2.astype(F32)
    qnorm = jnp.sqrt(jnp.sum(q2f * q2f, axis=-1, keepdims=True))
    knorm = jnp.concatenate([jnp.broadcast_to(knorm_ref[0:1, 0:1], (t, 1)),
                             jnp.broadcast_to(knorm_ref[1:2, 0:1], (t, 1))], axis=0)
    bound = _logit_bound(qnorm, knorm, tab_ref, h)
    l, acc = _shifted_or_running_sweep(i, t, tk, scores, near_bias, values, process, bound, b_far,
                                       sa_ref, sb_ref, 2 * t)
    o12 = acc / l
    lam_e = jnp.exp(jnp.sum(lq_ref[...] * lk_ref[...], axis=-1, keepdims=True))
    lam = lam_e[0:1, :] - lam_e[1:2, :] + lambda_init
    o = o12[:t, :] - lam * o12[t:, :]
    o = o * lax.rsqrt(jnp.mean(o * o, axis=-1, keepdims=True) + LN_EPS) * g_ref[...]
    o_ref[...] = (o * (1.0 - lambda_init)).astype(o_ref.dtype)


def _sb_kernel(q_ref, k_ref, v_ref, o_ref, *, t, scale):
    i = pl.program_id(1)
    q = q_ref[...]

    def later_matrix(n):
        r = lax.broadcasted_iota(jnp.int32, (n, n), 0)
        c = lax.broadcasted_iota(jnp.int32, (n, n), 1)
        return jnp.where(r > c, 1.0, 0.0).astype(BF16)

    def block(off, nk, rem, acc, shift):
        k = k_ref[pl.ds(off, nk), :]
        v = v_ref[pl.ds(off, nk), :]
        z = _nt_dot(q, k) * scale
        soft = jnp.log1p(jnp.exp(-jnp.abs(z)))
        log_sig = jnp.minimum(z, 0.0) - soft
        log_rem = jnp.minimum(-z, 0.0) - soft
        if shift is not None:
            rows = lax.broadcasted_iota(jnp.int32, (t, nk), 0)
            cols = lax.broadcasted_iota(jnp.int32, (t, nk), 1)
            mask = cols < rows + shift
            log_rem = jnp.where(mask, log_rem, 0.0)
        hi, lo = _split_bf16(log_rem)
        later = later_matrix(nk)
        after = (jnp.dot(hi, later, preferred_element_type=F32)
                 + jnp.dot(lo, later, preferred_element_type=F32))
        a = jnp.exp(log_sig + after + rem)
        if shift is not None:
            a = jnp.where(mask, a, 0.0)
        acc = acc + jnp.dot(a.astype(BF16), v, preferred_element_type=F32)
        rem = rem + jnp.sum(log_rem, axis=-1, keepdims=True)
        return rem, acc

    first = jnp.maximum(i - 1, 0)
    rem, acc = block(pl.multiple_of(first * t, t), 2 * t, jnp.zeros((t, 1), F32),
                     jnp.zeros((t, HEAD_DIM), F32), (i - first) * t)

    def live(rem):
        return (jnp.max(rem) > SB_CUTOFF).astype(jnp.int32)

    def cond(c):
        j, go, _, _ = c
        return jnp.logical_and(j >= 0, go > 0)

    def body(c):
        j, _, rem, acc = c
        rem, acc = block(pl.multiple_of(j * t, t), t, rem, acc, None)
        return j - 1, live(rem), rem, acc

    _, _, _, acc = lax.while_loop(cond, body, (i - 2, live(rem), rem, acc))
    o_ref[...] = acc.astype(o_ref.dtype)


def _attention_call(kernel, h, out_cols, col0, extra_in, extra_specs, scratch, name, t):
    s = h.shape[0]
    nh = N_HEADS_GROUP
    in_specs = list(extra_specs[0]) + [
        pl.BlockSpec((t, HEAD_DIM), lambda hd, i: (i, col0 + hd)),
        pl.BlockSpec((s, HEAD_DIM), lambda hd, i: (0, col0 + nh + hd)),
        pl.BlockSpec((s, HEAD_DIM), lambda hd, i: (0, col0 + 2 * nh + hd)),
    ] + list(extra_specs[1])
    args = list(extra_in[0]) + [h, h, h] + list(extra_in[1])
    return pl.pallas_call(
        kernel,
        out_shape=jax.ShapeDtypeStruct((s, out_cols), BF16),
        grid=(nh, s // t),
        in_specs=in_specs,
        out_specs=pl.BlockSpec((t, HEAD_DIM), lambda hd, i: (i, hd)),
        scratch_shapes=scratch,
        compiler_params=_cparams(2, VMEM_LIMIT),
        name=name,
    )(*args)


def _route(x, w, b):
    logits = jnp.dot(x, w, preferred_element_type=F32, precision=lax.Precision.HIGHEST) + b
    tm, n = logits.shape
    lane = lax.broadcasted_iota(jnp.int32, (tm, n), 1)
    real = lane < N_EXPERTS
    logits = jnp.where(real, logits, -jnp.inf)
    e = jnp.exp(logits - jnp.max(logits, axis=-1, keepdims=True))
    aff = e / jnp.sum(e, axis=-1, keepdims=True)
    group = lane // EXPERTS_PER_GROUP
    best = jnp.full((tm, 1), -jnp.inf, F32)
    g_sel = jnp.zeros((tm, 1), jnp.int32)
    for g in range(N_GROUPS):
        gmax = jnp.max(jnp.where(group == g, aff, -jnp.inf), axis=-1, keepdims=True)
        better = gmax > best
        g_sel = jnp.where(better, g, g_sel)
        best = jnp.where(better, gmax, best)
    in_group = jnp.logical_and(group == g_sel, real)
    cand = jnp.where(in_group, aff, -jnp.inf)
    v1 = jnp.max(cand, axis=-1, keepdims=True)
    i1 = jnp.min(jnp.where(cand == v1, lane, n), axis=-1, keepdims=True)
    cand2 = jnp.where(lane == i1, -jnp.inf, cand)
    v2 = jnp.max(cand2, axis=-1, keepdims=True)
    i2 = jnp.min(jnp.where(cand2 == v2, lane, n), axis=-1, keepdims=True)
    tot = v1 + v2
    chosen = jnp.where(jnp.logical_or(lane == i1, lane == i2), 1.0, 0.0)
    r = lax.broadcasted_iota(jnp.int32, (tm, tm), 0)
    c = lax.broadcasted_iota(jnp.int32, (tm, tm), 1)
    earlier = jnp.where(c < r, 1.0, 0.0).astype(BF16)
    before = jnp.dot(earlier, chosen.astype(BF16), preferred_element_type=F32)
    rank1 = jnp.sum(jnp.where(lane == i1, before, 0.0), axis=-1, keepdims=True)
    rank2 = jnp.sum(jnp.where(lane == i2, before, 0.0), axis=-1, keepdims=True)
    route = jnp.where(lane == 0, v1 / tot,
                      jnp.where(lane == 1, v2 / tot,
                                jnp.where(lane == 2, i1.astype(F32),
                                          jnp.where(lane == 3, i2.astype(F32),
                                                    jnp.where(lane == 4, rank1,
                                                              jnp.where(lane == 5, rank2, 0.0))))))
    return route, jnp.sum(chosen, axis=0, keepdims=True)


def _dispatch_plan(route, counts, tm):
    s = route.shape[0]
    nt = s // tm
    n_tiles = (MOE_TOP_K * s) // tm + N_EXPERTS
    ids = jnp.arange(N_EXPERTS, dtype=jnp.int32)
    expert = route[:, 2:2 + MOE_TOP_K].astype(jnp.int32).reshape(nt, tm, MOE_TOP_K)
    rank = route[:, 4:4 + MOE_TOP_K].astype(jnp.int32).reshape(nt, tm, MOE_TOP_K)
    cnt = counts[:, 0, :N_EXPERTS].astype(jnp.int32)
    tile_base = jnp.cumsum(cnt, axis=0) - cnt
    padded = -(-jnp.sum(cnt, axis=0) // tm) * tm
    seg_end = jnp.cumsum(padded)
    base = (seg_end - padded)[None, :] + tile_base
    pick = expert[..., None] == ids
    pair_row = (jnp.sum(jnp.where(pick, base[:, None, None, :], 0), axis=-1) + rank).reshape(s, MOE_TOP_K)
    tile_start = jnp.arange(n_tiles, dtype=jnp.int32) * tm
    tile_expert = jnp.minimum(jnp.sum((tile_start[:, None] >= seg_end[None, :]).astype(jnp.int32), axis=1),
                              N_EXPERTS - 1).astype(jnp.int32)
    tile_live = (tile_start < seg_end[-1]).astype(jnp.int32)
    row_token = (jnp.arange(n_tiles * tm, dtype=jnp.int32) % s).at[pair_row.reshape(-1)].set(
        jnp.arange(MOE_TOP_K * s, dtype=jnp.int32) // MOE_TOP_K)
    return pair_row, row_token, tile_expert, tile_live


def _row_gather_kernel(idx_ref, src_ref, o_ref, sem, *, chunk):
    def issue(g, _):
        base = pl.multiple_of(g * SUBLANES, SUBLANES)
        for u in range(SUBLANES):
            tok = idx_ref[0, base + u]
            pltpu.make_async_copy(
                src_ref.at[lax.shift_right_logical(tok, 3), pl.ds(jnp.bitwise_and(tok, SUBLANES - 1), 1)],
                o_ref.at[g, pl.ds(u, 1)], sem).start()
        return 0

    lax.fori_loop(0, chunk // SUBLANES, issue, 0)
    pltpu.make_async_copy(src_ref.at[pl.ds(0, chunk // SUBLANES)], o_ref, sem).wait()


def _row_gather(src, idx, chunk):
    n = idx.shape[0]
    rows, d = src.shape
    out = pl.pallas_call(
        functools.partial(_row_gather_kernel, chunk=chunk),
        out_shape=jax.ShapeDtypeStruct((n // SUBLANES, SUBLANES, d), src.dtype),
        grid=(n // chunk,),
        in_specs=[pl.BlockSpec((None, 1, chunk), lambda c: (c, 0, 0), memory_space=pltpu.SMEM),
                  pl.BlockSpec(memory_space=pl.ANY)],
        out_specs=pl.BlockSpec((chunk // SUBLANES, SUBLANES, d), lambda c: (c, 0, 0)),
        scratch_shapes=[pltpu.SemaphoreType.DMA(())],
        compiler_params=_cparams(1, VMEM_LIMIT),
        name="row_gather",
    )(idx.reshape(n // chunk, 1, chunk), src.reshape(rows // SUBLANES, SUBLANES, d))
    return out.reshape(n, d)


def _moe_group_kernel(te_ref, live_ref, x_ref, wg_ref, wu_ref, wd_ref, o_ref, xb_ref):
    g = pl.program_id(0)
    f = pl.program_id(1)
    live = live_ref[g] > 0

    @pl.when(jnp.logical_and(live, f == 0))
    def _():
        xb_ref[...] = x_ref[...].astype(BF16)

    @pl.when(jnp.logical_and(jnp.logical_not(live), f == 0))
    def _():
        o_ref[...] = jnp.zeros(o_ref.shape, F32)

    @pl.when(live)
    def _():
        xb = xb_ref[...]
        a = jnp.dot(xb, wg_ref[...], preferred_element_type=F32)
        u = jnp.dot(xb, wu_ref[...], preferred_element_type=F32)
        hid = ((a * jax.nn.sigmoid(a)) * u).astype(BF16)
        y = jnp.dot(hid, wd_ref[...], preferred_element_type=F32)

        @pl.when(f == 0)
        def _():
            o_ref[...] = y

        @pl.when(f > 0)
        def _():
            o_ref[...] += y


def _moe_group_ffn(xs, tile_expert, tile_live, wg, wu, wd, tm, tf):
    rows, d = xs.shape
    fdim = wg.shape[2]
    grid_spec = pltpu.PrefetchScalarGridSpec(
        num_scalar_prefetch=2,
        grid=(rows // tm, fdim // tf),
        in_specs=[pl.BlockSpec((tm, d), lambda g, f, te, lv: (g, 0)),
                  pl.BlockSpec((None, d, tf), lambda g, f, te, lv: (te[g], 0, f)),
                  pl.BlockSpec((None, d, tf), lambda g, f, te, lv: (te[g], 0, f)),
                  pl.BlockSpec((None, tf, d), lambda g, f, te, lv: (te[g], f, 0))],
        out_specs=pl.BlockSpec((tm, d), lambda g, f, te, lv: (g, 0)),
        scratch_shapes=[pltpu.VMEM((tm, d), BF16)])
    return pl.pallas_call(
        _moe_group_kernel,
        out_shape=jax.ShapeDtypeStruct((rows, d), F32),
        grid_spec=grid_spec,
        compiler_params=_cparams(2, VMEM_LIMIT),
        name="moe_group_ffn",
    )(tile_expert, tile_live, xs, wg, wu, wd)


def _moe_combine_ln_kernel(x_ref, r_ref, y0_ref, y1_ref, g_ref, b_ref, o_ref):
    r = r_ref[...]
    ff = r[:, 0:1] * y0_ref[...] + r[:, 1:2] * y1_ref[...]
    o_ref[...] = _layer_norm(ALPHA * x_ref[...] + ff, g_ref[...], b_ref[...])


def _moe_combine_ln(x, route, y, g, b, tm):
    s, d = x.shape
    nt = s // tm
    return pl.pallas_call(
        _moe_combine_ln_kernel,
        out_shape=jax.ShapeDtypeStruct((s, d), F32),
        grid=(nt,),
        in_specs=[pl.BlockSpec((tm, d), lambda i: (i, 0)),
                  pl.BlockSpec((tm, LANES), lambda i: (i, 0)),
                  pl.BlockSpec((tm, d), lambda i: (i, 0)),
                  pl.BlockSpec((tm, d), lambda i: (i + nt, 0)),
                  pl.BlockSpec((1, d), lambda i: (0, 0)),
                  pl.BlockSpec((1, d), lambda i: (0, 0))],
        out_specs=pl.BlockSpec((tm, d), lambda i: (i, 0)),
        compiler_params=_cparams(1, VMEM_LIMIT),
        name="moe_combine_ln",
    )(x, route, y, y, g.reshape(1, d), b.reshape(1, d))


def _moe_ln(x, route, counts, wg, wu, wd, g, b, tm):
    pair_row, row_token, tile_expert, tile_live = _dispatch_plan(route, counts, tm)
    xs = _row_gather(x, row_token, MOE_GATHER_CHUNK)
    ys = _moe_group_ffn(xs, tile_expert, tile_live, wg, wu, wd, tm, D_EXPERT // 2)
    slot_major = pair_row.T.reshape(-1)
    y = _row_gather(ys, slot_major, MOE_GATHER_CHUNK)
    return _moe_combine_ln(x, route, y, g, b, tm)


def kernel(x, w_in_even, w_out_even, forget_bias, w_in_odd, w_out_odd, lambda_q, lambda_k,
           subln_gain, rel_bias, w_router, b_router, w_gate, w_up, w_down, ln_gain, ln_bias):
    bsz, s, d = x.shape
    assert bsz == 1 and d == D_MODEL and s % MOBA_BLOCK == 0
    t = ATT_TILE
    nh = N_HEADS_GROUP
    gw = GROUP_WIDTH
    tm_proj = min(1024, s)
    tm_row = min(512, s)
    tk = min(ATT_KEYS, s)
    n_near = _near_tile_count(t)
    assert s % tk == 0 and tk // t + 1 >= n_near
    nblk = s // MOBA_BLOCK
    nbp = -(-nblk // LANES) * LANES

    x2 = x.reshape(s, d)
    bias_tiles = _bias_tiles(rel_bias, t, n_near)
    tab_spec = pl.BlockSpec(memory_space=pltpu.SMEM)
    bias_spec = pl.BlockSpec((None, n_near + 2, t, t), lambda hd, i: (hd, 0, 0, 0))
    block_onehot = (jnp.arange(s, dtype=jnp.int32)[:, None] // MOBA_BLOCK
                    == jnp.arange(nbp, dtype=jnp.int32)[None, :]).astype(BF16)

    for layer in range(DEPTH):
        li = layer // 2
        if layer % 2 == 0:
            w = w_in_even[li]
            fcol = 3 * gw
            w_main = jnp.concatenate([w[:, :fcol], w[:, fcol + nh:]], axis=1).astype(BF16)
            w_f = jnp.zeros((d, LANES), BF16).at[:, :nh].set(w[:, fcol:fcol + nh].astype(BF16))
            h = _in_proj(x2, w_main, BF16, tm_proj, 512)
            f = _in_proj(x2, w_f, F32, tm_proj, LANES)
            c = _decay_cumsum(f[:, :nh].T, forget_bias[li])
            oa = _attention_call(
                functools.partial(_fox_kernel, t=FOX_TQ, tk=min(FOX_TK, s), scale=HEAD_DIM ** -0.5), h, gw, 0,
                ([c[:, min(FOX_TK, s) - 1::min(FOX_TK, s)], c[:, ::FOX_TQ]],
                 [c.reshape(nh, 1, s), c.reshape(nh, s, 1)]),
                ([tab_spec, tab_spec],
                 [pl.BlockSpec((None, 1, s), lambda hd, i: (hd, 0, 0)),
                  pl.BlockSpec((None, FOX_TQ, 1), lambda hd, i: (hd, i, 0))]),
                [pltpu.VMEM((FOX_TQ, min(FOX_TK, s)), F32)] * 2 + [pltpu.VMEM((8, LANES), F32)],
                "fox_attention", FOX_TQ)
            ob = _attention_call(
                functools.partial(_moba_kernel, t=t, tk=tk, nblk=nblk, n_near=n_near,
                                  scale=HEAD_DIM ** -0.5),
                h, gw, 3 * nh,
                ([rel_bias], [bias_tiles, block_onehot]),
                ([tab_spec], [bias_spec, pl.BlockSpec((s, nbp), lambda hd, i: (0, 0))]),
                [pltpu.VMEM((nbp, HEAD_DIM), F32)] + [pltpu.VMEM((t, tk), F32)] * 2
                + [pltpu.VMEM((8, LANES), F32)], "moba_attention", t)
            w_out = w_out_even[li]
        else:
            lambda_init = 0.8 - 0.6 * math.exp(-0.3 * layer)
            h = _in_proj(x2, w_in_odd[li].astype(BF16), BF16, tm_proj, 512)
            small = lambda shape: pl.BlockSpec(shape, lambda hd, i: (0, 0))
            oa = _attention_call(
                functools.partial(_diff_kernel, t=t, tk=tk, n_near=n_near, scale=DIFF_QK_DIM ** -0.5,
                                  lambda_init=lambda_init),
                h, gw, 0,
                ([rel_bias], [bias_tiles, lambda_q[li], lambda_k[li], subln_gain[li].reshape(1, HEAD_DIM)]),
                ([tab_spec], [bias_spec, small((2, DIFF_QK_DIM)), small((2, DIFF_QK_DIM)),
                              small((1, HEAD_DIM))]),
                [pltpu.VMEM((2 * t, tk), F32)] * 2 + [pltpu.VMEM((8, LANES), F32)], "diff_attention", t)
            ob = _attention_call(
                functools.partial(_sb_kernel, t=t, scale=HEAD_DIM ** -0.5), h, gw, 3 * nh,
                ([], []), ([], []), [], "stickbreak_attention", t)
            w_out = w_out_odd[li]
        x2 = _out_proj_ln(oa, ob, w_out.astype(BF16), x2, ln_gain[layer, 0], ln_bias[layer, 0], tm_row)
        route, counts = _router(x2, w_router, b_router, tm_row)
        x2 = _moe_ln(x2, route, counts, w_gate[layer].astype(BF16), w_up[layer].astype(BF16),
                     w_down[layer].astype(BF16), ln_gain[layer, 1], ln_bias[layer, 1], tm_row)
    return x2.reshape(bsz, s, d)
```

```python
import functools
import math

import numpy as np
import jax
import jax.numpy as jnp
from jax import lax
from jax.experimental import pallas as pl
from jax.experimental.pallas import tpu as pltpu

F32 = jnp.float32
BF16 = jnp.bfloat16

D_MODEL = 2048
DEPTH = 2
HEAD_DIM = 128
N_HEADS_GROUP = 8
GROUP_WIDTH = N_HEADS_GROUP * HEAD_DIM
DIFF_QK_DIM = HEAD_DIM // 2
MOBA_BLOCK = 256
MOBA_TOPK = 3
N_BUCKETS = 32
MAX_EXACT = N_BUCKETS // 2
MAX_DISTANCE = 1024
N_EXPERTS = 16
N_GROUPS = 4
EXPERTS_PER_GROUP = N_EXPERTS // N_GROUPS
MOE_TOP_K = 2
MOE_GATHER_CHUNK = 512
SUBLANES = 8
D_EXPERT = D_MODEL // 2
ALPHA = (2.0 * DEPTH) ** 0.25
LN_EPS = 1e-5
NEG_INF = -1e30
LOG2E = math.log2(math.e)
LANES = 128
ATT_TILE = 256
ATT_KEYS = 1024
FOX_TQ = 512
FOX_TK = 1024
MOBA_Q_BLOCKS = 2
DIFF_Q_BLOCKS = 2
SHIFT_SLACK = 96.0
FLUSH_EXPONENT = -130.0
SB_CUTOFF = -104.0
VMEM_LIMIT = 56 * 1024 * 1024


def _cparams(n_axes, vmem=None):
    return pltpu.CompilerParams(dimension_semantics=("arbitrary",) * n_axes,
                                vmem_limit_bytes=vmem)


def _nt_dot(a, b):
    return lax.dot_general(a, b, (((1,), (1,)), ((), ())), preferred_element_type=F32)


def _split_bf16(x):
    hi = x.astype(BF16)
    lo = (x - hi.astype(F32)).astype(BF16)
    return hi, lo


def _layer_norm(z, g, b):
    mu = jnp.mean(z, axis=-1, keepdims=True)
    zc = z - mu
    var = jnp.mean(zc * zc, axis=-1, keepdims=True)
    return zc * lax.rsqrt(var + LN_EPS) * g + b


def _inproj_kernel(x_ref, w_ref, o_ref, xb_ref):
    @pl.when(pl.program_id(1) == 0)
    def _():
        xb_ref[...] = x_ref[...].astype(BF16)

    o_ref[...] = jnp.dot(xb_ref[...], w_ref[...], preferred_element_type=F32).astype(o_ref.dtype)


def _in_proj(x, w, out_dtype, tm, tn):
    s, d = x.shape
    n = w.shape[1]
    return pl.pallas_call(
        _inproj_kernel,
        out_shape=jax.ShapeDtypeStruct((s, n), out_dtype),
        grid=(s // tm, n // tn),
        in_specs=[pl.BlockSpec((tm, d), lambda i, j: (i, 0)),
                  pl.BlockSpec((d, tn), lambda i, j: (0, j))],
        out_specs=pl.BlockSpec((tm, tn), lambda i, j: (i, j)),
        scratch_shapes=[pltpu.VMEM((tm, d), BF16)],
        compiler_params=_cparams(2, VMEM_LIMIT),
        name="in_proj",
    )(x, w)


def _outproj_ln_kernel(oa_ref, ob_ref, wa_ref, wb_ref, x_ref, g_ref, b_ref, o_ref):
    y = jnp.dot(oa_ref[...], wa_ref[...], preferred_element_type=F32)
    y = y + jnp.dot(ob_ref[...], wb_ref[...], preferred_element_type=F32)
    o_ref[...] = _layer_norm(ALPHA * x_ref[...] + y, g_ref[...], b_ref[...])


def _out_proj_ln(oa, ob, w, x, g, b, tm):
    s, d = x.shape
    gw = oa.shape[1]
    return pl.pallas_call(
        _outproj_ln_kernel,
        out_shape=jax.ShapeDtypeStruct((s, d), F32),
        grid=(s // tm,),
        in_specs=[pl.BlockSpec((tm, gw), lambda i: (i, 0)),
                  pl.BlockSpec((tm, gw), lambda i: (i, 0)),
                  pl.BlockSpec((gw, d), lambda i: (0, 0)),
                  pl.BlockSpec((gw, d), lambda i: (1, 0)),
                  pl.BlockSpec((tm, d), lambda i: (i, 0)),
                  pl.BlockSpec((1, d), lambda i: (0, 0)),
                  pl.BlockSpec((1, d), lambda i: (0, 0))],
        out_specs=pl.BlockSpec((tm, d), lambda i: (i, 0)),
        compiler_params=_cparams(1, VMEM_LIMIT),
        name="out_proj_ln",
    )(oa, ob, w, w, x, g.reshape(1, d), b.reshape(1, d))


def _router_kernel(x_ref, w_ref, b_ref, route_ref, count_ref):
    route, counts = _route(x_ref[...], w_ref[...], b_ref[...])
    route_ref[...] = route
    count_ref[...] = jnp.broadcast_to(counts, count_ref.shape)


def _router(x, w_router, b_router, tm):
    s, d = x.shape
    wr = jnp.zeros((d, LANES), F32).at[:, :N_EXPERTS].set(w_router)
    br = jnp.zeros((1, LANES), F32).at[0, :N_EXPERTS].set(b_router)
    return pl.pallas_call(
        _router_kernel,
        out_shape=(jax.ShapeDtypeStruct((s, LANES), F32),
                   jax.ShapeDtypeStruct((s // tm, SUBLANES, LANES), F32)),
        grid=(s // tm,),
        in_specs=[pl.BlockSpec((tm, d), lambda i: (i, 0)),
                  pl.BlockSpec((d, LANES), lambda i: (0, 0)),
                  pl.BlockSpec((1, LANES), lambda i: (0, 0))],
        out_specs=(pl.BlockSpec((tm, LANES), lambda i: (i, 0)),
                   pl.BlockSpec((None, SUBLANES, LANES), lambda i: (i, 0, 0))),
        compiler_params=_cparams(1, VMEM_LIMIT),
        name="router",
    )(x, wr, br)


def _decay_cumsum_kernel(f_ref, b_ref, c_ref):
    nh, s = f_ref.shape
    rows = lax.broadcasted_iota(jnp.int32, (LANES, LANES), 0)
    cols = lax.broadcasted_iota(jnp.int32, (LANES, LANES), 1)
    upper = jnp.where(rows <= cols, 1.0, 0.0).astype(F32)
    bias = b_ref[...]

    def body(n, carry):
        off = pl.multiple_of(n * LANES, LANES)
        z = f_ref[:, pl.ds(off, LANES)] + bias
        logf = jnp.minimum(z, 0.0) - jnp.log1p(jnp.exp(-jnp.abs(z)))
        c = jnp.dot(logf, upper, preferred_element_type=F32,
                    precision=lax.Precision.HIGHEST) + carry
        c_ref[:, pl.ds(off, LANES)] = c * LOG2E
        return carry + jnp.sum(logf, axis=-1, keepdims=True)

    lax.fori_loop(0, s // LANES, body, jnp.zeros((nh, 1), F32))


def _decay_cumsum(f_t, bias):
    nh, s = f_t.shape
    return pl.pallas_call(
        _decay_cumsum_kernel,
        out_shape=jax.ShapeDtypeStruct((nh, s), F32),
        name="decay_cumsum",
    )(f_t, bias.reshape(nh, 1))


def _t5_bucket_np(dist):
    n = np.maximum(dist, 0)
    nf = np.maximum(n, 1).astype(np.float32)
    ratio = np.log(nf / np.float32(MAX_EXACT)) / np.float32(math.log(MAX_DISTANCE / MAX_EXACT))
    large = MAX_EXACT + (ratio.astype(np.float32) * np.float32(N_BUCKETS - MAX_EXACT)).astype(np.int32)
    large = np.minimum(large, N_BUCKETS - 1)
    return np.where(n < MAX_EXACT, n, large).astype(np.int32)


def _near_tile_count(t):
    d = np.arange(0, 4 * MAX_DISTANCE, dtype=np.int64)
    not_last = np.nonzero(_t5_bucket_np(d) != N_BUCKETS - 1)[0]
    d_sat = int(not_last.max()) + 1
    n = 1
    while (n - 1) * t + 1 < d_sat:
        n += 1
    return n


def _bucket_tiles_np(t, n_near):
    r = np.arange(t)[:, None]
    c = np.arange(t)[None, :]
    tiles = []
    for delta in range(n_near):
        dist = delta * t + r - c
        tiles.append(np.where(dist >= 0, _t5_bucket_np(dist), -1))
    tiles.append(np.full((t, t), N_BUCKETS - 1))
    tiles.append(np.full((t, t), -1))
    return np.stack(tiles).astype(np.int32)


def _bias_tiles_kernel(tab_ref, idx_ref, o_ref):
    h = pl.program_id(0)
    idx = idx_ref[...]
    out = jnp.full(idx.shape, NEG_INF, F32)
    for b in range(N_BUCKETS):
        out = jnp.where(idx == b, tab_ref[b, h] * LOG2E, out)
    o_ref[...] = out


def _bias_tiles(rel_bias, t, n_near):
    nh = rel_bias.shape[1]
    idx = jnp.asarray(_bucket_tiles_np(t, n_near))
    n_tiles = idx.shape[0]
    return pl.pallas_call(
        _bias_tiles_kernel,
        out_shape=jax.ShapeDtypeStruct((nh, n_tiles, t, t), F32),
        grid=(nh, n_tiles),
        in_specs=[pl.BlockSpec(memory_space=pltpu.SMEM),
                  pl.BlockSpec((None, t, t), lambda h, n: (n, 0, 0))],
        out_specs=pl.BlockSpec((None, None, t, t), lambda h, n: (h, n, 0, 0)),
        compiler_params=_cparams(2),
        name="bias_tiles",
    )(rel_bias, idx)


def _online_softmax_step(s, v, carry):
    m, l, acc = carry
    m_new = jnp.maximum(m, jnp.max(s, axis=-1, keepdims=True))
    alpha = jnp.exp2(m - m_new)
    p = jnp.exp2(s - m_new)
    l = alpha * l + jnp.sum(p, axis=-1, keepdims=True)
    acc = alpha * acc + jnp.dot(p.astype(BF16), v, preferred_element_type=F32)
    return m_new, l, acc


def _prescale(q, c):
    return (q.astype(F32) * c).astype(BF16)


def _near_bias(bias_ref, blocks, j, tb, tk, n_near):
    nsub = tk // tb
    rows = []
    for i in blocks:
        tiles = []
        for b in range(nsub):
            delta = i - (j * nsub + b)
            tiles.append(bias_ref[jnp.where(delta < 0, n_near + 1, jnp.minimum(delta, n_near))])
        rows.append(jnp.concatenate(tiles, axis=1))
    return rows[0] if len(rows) == 1 else jnp.concatenate(rows, axis=0)


def _biased_sweep(i, t, tk, scores, process, sa_ref, sb_ref, init):
    jd = (i * t) // tk
    carry = process(scores(jd), jd, init, True)
    lo = jnp.maximum(jd - 1, 0)
    carry = lax.fori_loop(lo, jd, lambda j, c: process(scores(j), j, c, True), carry)
    return _pipelined_sweep(lo, scores, lambda s, j, c: process(s, j, c, False), sa_ref, sb_ref, carry)


def _softmax_init(rows):
    return (jnp.full((rows, 1), NEG_INF, F32), jnp.zeros((rows, 1), F32),
            jnp.zeros((rows, HEAD_DIM), F32))


def _kv_tile(k_ref, v_ref, j, t):
    off = pl.multiple_of(j * t, t)
    return k_ref[pl.ds(off, t), :], v_ref[pl.ds(off, t), :], off


def _pipelined_sweep(n, scores, process, sa_ref, sb_ref, carry):
    last = jnp.maximum(n - 1, 0)
    sa_ref[...] = scores(0)

    def pair(p, c):
        j = 2 * p
        sb_ref[...] = scores(jnp.minimum(j + 1, last))
        c = process(sa_ref[...], j, c)
        sa_ref[...] = scores(jnp.minimum(j + 2, last))
        return process(sb_ref[...], j + 1, c)

    carry = lax.fori_loop(0, n // 2, pair, carry)
    return lax.fori_loop(0, n % 2, lambda _, c: process(sa_ref[...], n - 1, c), carry)


def _fox_kernel(cend_ref, cstart_ref, q_ref, k_ref, v_ref, c_ref, ccol_ref, o_ref, sa_ref, sb_ref,
                knorm_ref, *, t, tk, scale):
    h = pl.program_id(0)
    i = pl.program_id(1)

    @pl.when(i == 0)
    def _():
        knorm_ref[...] = jnp.broadcast_to(_max_key_norm(k_ref, tk, 0, HEAD_DIM), knorm_ref.shape)

    q = _prescale(q_ref[...], scale * LOG2E)
    rows = lax.broadcasted_iota(jnp.int32, (t, tk), 0)
    cols = lax.broadcasted_iota(jnp.int32, (t, tk), 1)

    def scores(j):
        off = pl.multiple_of(j * tk, tk)
        return _nt_dot(q, k_ref[pl.ds(off, tk), :]) - c_ref[:, pl.ds(off, tk)]

    def values(j):
        return v_ref[pl.ds(pl.multiple_of(j * tk, tk), tk), :]

    def process(s, j, carry):
        return _online_softmax_step(s, values(j), carry)

    jd = (i * t) // tk
    s_diag = jnp.where(cols <= rows + (i * t - jd * tk), scores(jd), NEG_INF)

    qf = q.astype(F32)
    qnorm = jnp.sqrt(jnp.sum(qf * qf, axis=-1, keepdims=True))
    bound = qnorm * knorm_ref[0:1, 0:1] * (1.0 + 2.0 ** -9) + 2.0 ** -6 - ccol_ref[...]
    slack = jnp.max(bound - jnp.max(s_diag, axis=-1, keepdims=True))

    def shifted_sweep(_):
        def step(s, j, carry):
            l, acc = carry
            p = jnp.exp2(s - bound)
            return (l + jnp.sum(p, axis=-1, keepdims=True),
                    acc + jnp.dot(p.astype(BF16), values(j), preferred_element_type=F32))

        c_first_row = cstart_ref[h, i]
        first_live = jnp.int32(0)
        for j in range(cend_ref.shape[1]):
            dead = jnp.logical_and(j < jd, c_first_row - cend_ref[h, j] < FLUSH_EXPONENT)
            first_live = first_live + dead.astype(jnp.int32)
        carry = step(s_diag, jd, _softmax_init(t)[1:])
        return _paired_loop(jd - first_live, lambda n, c: step(scores(first_live + n), first_live + n, c),
                            carry)

    def running_max_sweep(_):
        carry = _online_softmax_step(s_diag, values(jd), _softmax_init(t))
        return _pipelined_sweep(jd, scores, process, sa_ref, sb_ref, carry)[1:]

    l, acc = lax.cond(slack <= SHIFT_SLACK, shifted_sweep, running_max_sweep, 0)
    o_ref[...] = (acc / l).astype(o_ref.dtype)


def _moba_kernel(tab_ref, q_ref, k_ref, v_ref, bias_ref, et_ref, o_ref, kmean_ref, sa_ref, sb_ref,
                 knorm_ref, *, tb, nb, tk, nblk, n_near, scale):
    h = pl.program_id(0)
    g = pl.program_id(1)
    t = nb * tb
    nbp = kmean_ref.shape[0]

    @pl.when(g == 0)
    def _():
        kmean_ref[...] = jnp.zeros(kmean_ref.shape, F32)

        def mean_body(n, _):
            off = pl.multiple_of(n * tb, tb)
            kb = k_ref[pl.ds(off, tb), :].astype(F32)
            kmean_ref[pl.ds(n, 1), :] = jnp.sum(kb, axis=0, keepdims=True) * (1.0 / tb)
            return 0

        lax.fori_loop(0, nblk, mean_body, 0)
        knorm_ref[...] = jnp.broadcast_to(_max_key_norm(k_ref, tk, 0, HEAD_DIM), knorm_ref.shape)

    q = q_ref[...]
    km_hi, km_lo = _split_bf16(kmean_ref[...])
    gate = _nt_dot(q, km_hi) + _nt_dot(q, km_lo)
    blk = lax.broadcasted_iota(jnp.int32, (t, nbp), 1)
    own = g * nb + lax.broadcasted_iota(jnp.int32, (t, 1), 0) // tb
    eligible = blk < own
    sel = jnp.zeros((t, nbp), F32)
    for r in range(MOBA_TOPK):
        cand = jnp.where(eligible, jnp.where(sel > 0.0, -jnp.inf, gate), -jnp.inf)
        vmax = jnp.max(cand, axis=-1, keepdims=True)
        first = jnp.min(jnp.where(cand == vmax, blk, nbp), axis=-1, keepdims=True)
        take = jnp.where(r < own, 1.0, 0.0)
        sel = jnp.maximum(sel, jnp.where(blk == first, take, 0.0))

    sel = jnp.maximum(sel, jnp.where(blk == own, 1.0, 0.0))
    penalty = jnp.where(sel > 0.0, 0.0, NEG_INF).astype(BF16)
    q_aug = jnp.concatenate([_prescale(q, scale * LOG2E), penalty], axis=1)
    b_far = tab_ref[N_BUCKETS - 1, h] * LOG2E

    def scores(j):
        off = pl.multiple_of(j * tk, tk)
        k_aug = jnp.concatenate([k_ref[pl.ds(off, tk), :], et_ref[pl.ds(off, tk), :]], axis=1)
        return _nt_dot(q_aug, k_aug)

    def values(j):
        return v_ref[pl.ds(pl.multiple_of(j * tk, tk), tk), :]

    def near_bias(j):
        return _near_bias(bias_ref, [g * nb + b for b in range(nb)], j, tb, tk, n_near)

    def process(s, j, carry, near):
        return _online_softmax_step(s + (near_bias(j) if near else b_far), values(j), carry)

    qf = q_aug[:, :HEAD_DIM].astype(F32)
    qnorm = jnp.sqrt(jnp.sum(qf * qf, axis=-1, keepdims=True))
    bound = _logit_bound(qnorm, knorm_ref[0:1, 0:1], tab_ref, h)
    l, acc = _shifted_or_running_sweep(g, t, tk, scores, near_bias, values, process, bound, b_far,
                                       sa_ref, sb_ref, t)
    o_ref[...] = (acc / l).astype(o_ref.dtype)


def _paired_loop(n, body, carry):
    carry = lax.fori_loop(0, n // 2, lambda p, c: body(2 * p + 1, body(2 * p, c)), carry)
    return lax.fori_loop(0, n % 2, lambda _, c: body(n - 1, c), carry)


def _logit_bound(qnorm, knorm, tab_ref, h):
    b_max = tab_ref[0, h]
    for b in range(1, N_BUCKETS):
        b_max = jnp.maximum(b_max, tab_ref[b, h])
    return qnorm * knorm * (1.0 + 2.0 ** -9) + (b_max * LOG2E + 2.0 ** -6)


def _shifted_or_running_sweep(i, t, tk, scores, near_bias, values, process, bound, b_far,
                              sa_ref, sb_ref, rows):
    jd = (i * t) // tk
    s_diag = scores(jd) + near_bias(jd)
    slack = jnp.max(bound - jnp.max(s_diag, axis=-1, keepdims=True))

    def shifted_sweep(_):
        def step(s, j, carry, shift):
            l, acc = carry
            p = jnp.exp2(s - shift)
            return (l + jnp.sum(p, axis=-1, keepdims=True),
                    acc + jnp.dot(p.astype(BF16), values(j), preferred_element_type=F32))

        carry = step(s_diag, jd, _softmax_init(rows)[1:], bound)
        lo = jnp.maximum(jd - 1, 0)
        carry = lax.fori_loop(lo, jd, lambda j, c: step(scores(j) + near_bias(j), j, c, bound), carry)
        far_shift = bound - b_far
        return _paired_loop(lo, lambda j, c: step(scores(j), j, c, far_shift), carry)

    def running_max_sweep(_):
        return _biased_sweep(i, t, tk, scores, process, sa_ref, sb_ref, _softmax_init(rows))[1:]

    return lax.cond(slack <= SHIFT_SLACK, shifted_sweep, running_max_sweep, 0)


def _max_key_norm(k_ref, tk, lane_lo, lane_hi):
    klane = lax.broadcasted_iota(jnp.int32, (tk, HEAD_DIM), 1)
    keep = jnp.logical_and(klane >= lane_lo, klane < lane_hi)

    def body(n, c):
        kt = k_ref[pl.ds(pl.multiple_of(n * tk, tk), tk), :].astype(F32)
        sq = jnp.sum(jnp.where(keep, kt * kt, 0.0), axis=-1, keepdims=True)
        return jnp.maximum(c, jnp.max(sq, axis=0, keepdims=True))

    return jnp.sqrt(lax.fori_loop(0, k_ref.shape[0] // tk, body, jnp.zeros((1, 1), F32)))


def _diff_kernel(tab_ref, q_ref, k_ref, v_ref, bias_ref, lq_ref, lk_ref, g_ref, o_ref, sa_ref, sb_ref,
                 knorm_ref, *, tb, nb, tk, n_near, scale, lambda_init):
    h = pl.program_id(0)
    i = pl.program_id(1)
    t = nb * tb
    q = _prescale(q_ref[...], scale * LOG2E)
    lane = lax.broadcasted_iota(jnp.int32, (t, HEAD_DIM), 1)
    zero = jnp.zeros_like(q)
    q2 = jnp.concatenate([jnp.where(lane < DIFF_QK_DIM, q, zero),
                          jnp.where(lane >= DIFF_QK_DIM, q, zero)], axis=0)
    b_far = tab_ref[N_BUCKETS - 1, h] * LOG2E

    @pl.when(i == 0)
    def _():
        knorm_ref[0:1, :] = jnp.broadcast_to(_max_key_norm(k_ref, tk, 0, DIFF_QK_DIM), (1, LANES))
        knorm_ref[1:2, :] = jnp.broadcast_to(_max_key_norm(k_ref, tk, DIFF_QK_DIM, HEAD_DIM), (1, LANES))

    def scores(j):
        return _nt_dot(q2, k_ref[pl.ds(pl.multiple_of(j * tk, tk), tk), :])

    def values(j):
        return v_ref[pl.ds(pl.multiple_of(j * tk, tk), tk), :]

    def near_bias(j):
        b = _near_bias(bias_ref, [i * nb + b for b in range(nb)], j, tb, tk, n_near)
        return jnp.concatenate([b, b], axis=0)

    def process(s, j, carry, near):
        return _online_softmax_step(s + (near_bias(j) if near else b_far), values(j), carry)

    q2f = q2.astype(F32)
    qnorm = jnp.sqrt(jnp.sum(q2f * q2f, axis=-1, keepdims=True))
    knorm = jnp.concatenate([jnp.broadcast_to(knorm_ref[0:1, 0:1], (t, 1)),
                             jnp.broadcast_to(knorm_ref[1:2, 0:1], (t, 1))], axis=0)
    bound = _logit_bound(qnorm, knorm, tab_ref, h)
    l, acc = _shifted_or_running_sweep(i, t, tk, scores, near_bias, values, process, bound, b_far,
                                       sa_ref, sb_ref, 2 * t)
    o12 = acc / l
    lam_e = jnp.exp(jnp.sum(lq_ref[...] * lk_ref[...], axis=-1, keepdims=True))
    lam = lam_e[0:1, :] - lam_e[1:2, :] + lambda_init
    o = o12[:t, :] - lam * o12[t:, :]
    o = o * lax.rsqrt(jnp.mean(o * o, axis=-1, keepdims=True) + LN_EPS) * g_ref[...]
    o_ref[...] = (o * (1.0 - lambda_init)).astype(o_ref.dtype)


def _sb_kernel(q_ref, k_ref, v_ref, o_ref, *, t, scale):
    i = pl.program_id(1)
    q = q_ref[...]

    def later_matrix(n):
        r = lax.broadcasted_iota(jnp.int32, (n, n), 0)
        c = lax.broadcasted_iota(jnp.int32, (n, n), 1)
        return jnp.where(r > c, 1.0, 0.0).astype(BF16)

    def block(off, nk, rem, acc, shift):
        k = k_ref[pl.ds(off, nk), :]
        v = v_ref[pl.ds(off, nk), :]
        z = _nt_dot(q, k) * scale
        soft = jnp.log1p(jnp.exp(-jnp.abs(z)))
        log_sig = jnp.minimum(z, 0.0) - soft
        log_rem = jnp.minimum(-z, 0.0) - soft
        if shift is not None:
            rows = lax.broadcasted_iota(jnp.int32, (t, nk), 0)
            cols = lax.broadcasted_iota(jnp.int32, (t, nk), 1)
            mask = cols < rows + shift
            log_rem = jnp.where(mask, log_rem, 0.0)
        hi, lo = _split_bf16(log_rem)
        later = later_matrix(nk)
        after = (jnp.dot(hi, later, preferred_element_type=F32)
                 + jnp.dot(lo, later, preferred_element_type=F32))
        a = jnp.exp(log_sig + after + rem)
        if shift is not None:
            a = jnp.where(mask, a, 0.0)
        acc = acc + jnp.dot(a.astype(BF16), v, preferred_element_type=F32)
        rem = rem + jnp.sum(log_rem, axis=-1, keepdims=True)
        return rem, acc

    first = jnp.maximum(i - 1, 0)
    rem, acc = block(pl.multiple_of(first * t, t), 2 * t, jnp.zeros((t, 1), F32),
                     jnp.zeros((t, HEAD_DIM), F32), (i - first) * t)

    def live(rem):
        return (jnp.max(rem) > SB_CUTOFF).astype(jnp.int32)

    def cond(c):
        j, go, _, _ = c
        return jnp.logical_and(j >= 0, go > 0)

    def body(c):
        j, _, rem, acc = c
        rem, acc = block(pl.multiple_of(j * t, t), t, rem, acc, None)
        return j - 1, live(rem), rem, acc

    _, _, _, acc = lax.while_loop(cond, body, (i - 2, live(rem), rem, acc))
    o_ref[...] = acc.astype(o_ref.dtype)


def _attention_call(kernel, h, out_cols, col0, extra_in, extra_specs, scratch, name, t):
    s = h.shape[0]
    nh = N_HEADS_GROUP
    in_specs = list(extra_specs[0]) + [
        pl.BlockSpec((t, HEAD_DIM), lambda hd, i: (i, col0 + hd)),
        pl.BlockSpec((s, HEAD_DIM), lambda hd, i: (0, col0 + nh + hd)),
        pl.BlockSpec((s, HEAD_DIM), lambda hd, i: (0, col0 + 2 * nh + hd)),
    ] + list(extra_specs[1])
    args = list(extra_in[0]) + [h, h, h] + list(extra_in[1])
    return pl.pallas_call(
        kernel,
        out_shape=jax.ShapeDtypeStruct((s, out_cols), BF16),
        grid=(nh, s // t),
        in_specs=in_specs,
        out_specs=pl.BlockSpec((t, HEAD_DIM), lambda hd, i: (i, hd)),
        scratch_shapes=scratch,
        compiler_params=_cparams(2, VMEM_LIMIT),
        name=name,
    )(*args)


def _route(x, w, b):
    logits = jnp.dot(x, w, preferred_element_type=F32, precision=lax.Precision.HIGHEST) + b
    tm, n = logits.shape
    lane = lax.broadcasted_iota(jnp.int32, (tm, n), 1)
    real = lane < N_EXPERTS
    logits = jnp.where(real, logits, -jnp.inf)
    e = jnp.exp(logits - jnp.max(logits, axis=-1, keepdims=True))
    aff = e / jnp.sum(e, axis=-1, keepdims=True)
    group = lane // EXPERTS_PER_GROUP
    best = jnp.full((tm, 1), -jnp.inf, F32)
    g_sel = jnp.zeros((tm, 1), jnp.int32)
    for g in range(N_GROUPS):
        gmax = jnp.max(jnp.where(group == g, aff, -jnp.inf), axis=-1, keepdims=True)
        better = gmax > best
        g_sel = jnp.where(better, g, g_sel)
        best = jnp.where(better, gmax, best)
    in_group = jnp.logical_and(group == g_sel, real)
    cand = jnp.where(in_group, aff, -jnp.inf)
    v1 = jnp.max(cand, axis=-1, keepdims=True)
    i1 = jnp.min(jnp.where(cand == v1, lane, n), axis=-1, keepdims=True)
    cand2 = jnp.where(lane == i1, -jnp.inf, cand)
    v2 = jnp.max(cand2, axis=-1, keepdims=True)
    i2 = jnp.min(jnp.where(cand2 == v2, lane, n), axis=-1, keepdims=True)
    tot = v1 + v2
    chosen = jnp.where(jnp.logical_or(lane == i1, lane == i2), 1.0, 0.0)
    r = lax.broadcasted_iota(jnp.int32, (tm, tm), 0)
    c = lax.broadcasted_iota(jnp.int32, (tm, tm), 1)
    earlier = jnp.where(c < r, 1.0, 0.0).astype(BF16)
    before = jnp.dot(earlier, chosen.astype(BF16), preferred_element_type=F32)
    rank1 = jnp.sum(jnp.where(lane == i1, before, 0.0), axis=-1, keepdims=True)
    rank2 = jnp.sum(jnp.where(lane == i2, before, 0.0), axis=-1, keepdims=True)
    route = jnp.where(lane == 0, v1 / tot,
                      jnp.where(lane == 1, v2 / tot,
                                jnp.where(lane == 2, i1.astype(F32),
                                          jnp.where(lane == 3, i2.astype(F32),
                                                    jnp.where(lane == 4, rank1,
                                                              jnp.where(lane == 5, rank2, 0.0))))))
    return route, jnp.sum(chosen, axis=0, keepdims=True)


def _dispatch_plan(route, counts, tm):
    s = route.shape[0]
    nt = s // tm
    n_tiles = (MOE_TOP_K * s) // tm + N_EXPERTS
    ids = jnp.arange(N_EXPERTS, dtype=jnp.int32)
    expert = route[:, 2:2 + MOE_TOP_K].astype(jnp.int32).reshape(nt, tm, MOE_TOP_K)
    rank = route[:, 4:4 + MOE_TOP_K].astype(jnp.int32).reshape(nt, tm, MOE_TOP_K)
    cnt = counts[:, 0, :N_EXPERTS].astype(jnp.int32)
    tile_base = jnp.cumsum(cnt, axis=0) - cnt
    padded = -(-jnp.sum(cnt, axis=0) // tm) * tm
    seg_end = jnp.cumsum(padded)
    base = (seg_end - padded)[None, :] + tile_base
    pick = expert[..., None] == ids
    pair_row = (jnp.sum(jnp.where(pick, base[:, None, None, :], 0), axis=-1) + rank).reshape(s, MOE_TOP_K)
    tile_start = jnp.arange(n_tiles, dtype=jnp.int32) * tm
    tile_expert = jnp.minimum(jnp.sum((tile_start[:, None] >= seg_end[None, :]).astype(jnp.int32), axis=1),
                              N_EXPERTS - 1).astype(jnp.int32)
    tile_live = (tile_start < seg_end[-1]).astype(jnp.int32)
    row_token = (jnp.arange(n_tiles * tm, dtype=jnp.int32) % s).at[pair_row.reshape(-1)].set(
        jnp.arange(MOE_TOP_K * s, dtype=jnp.int32) // MOE_TOP_K)
    return pair_row, row_token, tile_expert, tile_live


def _row_gather_kernel(idx_ref, src_ref, o_ref, sem, *, chunk):
    def issue(g, _):
        base = pl.multiple_of(g * SUBLANES, SUBLANES)
        for u in range(SUBLANES):
            tok = idx_ref[0, base + u]
            pltpu.make_async_copy(
                src_ref.at[lax.shift_right_logical(tok, 3), pl.ds(jnp.bitwise_and(tok, SUBLANES - 1), 1)],
                o_ref.at[g, pl.ds(u, 1)], sem).start()
        return 0

    lax.fori_loop(0, chunk // SUBLANES, issue, 0)
    pltpu.make_async_copy(src_ref.at[pl.ds(0, chunk // SUBLANES)], o_ref, sem).wait()


def _row_gather(src, idx, chunk):
    n = idx.shape[0]
    rows, d = src.shape
    out = pl.pallas_call(
        functools.partial(_row_gather_kernel, chunk=chunk),
        out_shape=jax.ShapeDtypeStruct((n // SUBLANES, SUBLANES, d), src.dtype),
        grid=(n // chunk,),
        in_specs=[pl.BlockSpec((None, 1, chunk), lambda c: (c, 0, 0), memory_space=pltpu.SMEM),
                  pl.BlockSpec(memory_space=pl.ANY)],
        out_specs=pl.BlockSpec((chunk // SUBLANES, SUBLANES, d), lambda c: (c, 0, 0)),
        scratch_shapes=[pltpu.SemaphoreType.DMA(())],
        compiler_params=_cparams(1, VMEM_LIMIT),
        name="row_gather",
    )(idx.reshape(n // chunk, 1, chunk), src.reshape(rows // SUBLANES, SUBLANES, d))
    return out.reshape(n, d)


def _moe_group_kernel(te_ref, live_ref, x_ref, wg_ref, wu_ref, wd_ref, o_ref, xb_ref):
    g = pl.program_id(0)
    f = pl.program_id(1)
    live = live_ref[g] > 0

    @pl.when(jnp.logical_and(live, f == 0))
    def _():
        xb_ref[...] = x_ref[...].astype(BF16)

    @pl.when(jnp.logical_and(jnp.logical_not(live), f == 0))
    def _():
        o_ref[...] = jnp.zeros(o_ref.shape, F32)

    @pl.when(live)
    def _():
        xb = xb_ref[...]
        a = jnp.dot(xb, wg_ref[...], preferred_element_type=F32)
        u = jnp.dot(xb, wu_ref[...], preferred_element_type=F32)
        hid = ((a * jax.nn.sigmoid(a)) * u).astype(BF16)
        y = jnp.dot(hid, wd_ref[...], preferred_element_type=F32)

        @pl.when(f == 0)
        def _():
            o_ref[...] = y

        @pl.when(f > 0)
        def _():
            o_ref[...] += y


def _moe_group_ffn(xs, tile_expert, tile_live, wg, wu, wd, tm, tf):
    rows, d = xs.shape
    fdim = wg.shape[2]
    grid_spec = pltpu.PrefetchScalarGridSpec(
        num_scalar_prefetch=2,
        grid=(rows // tm, fdim // tf),
        in_specs=[pl.BlockSpec((tm, d), lambda g, f, te, lv: (g, 0)),
                  pl.BlockSpec((None, d, tf), lambda g, f, te, lv: (te[g], 0, f)),
                  pl.BlockSpec((None, d, tf), lambda g, f, te, lv: (te[g], 0, f)),
                  pl.BlockSpec((None, tf, d), lambda g, f, te, lv: (te[g], f, 0))],
        out_specs=pl.BlockSpec((tm, d), lambda g, f, te, lv: (g, 0)),
        scratch_shapes=[pltpu.VMEM((tm, d), BF16)])
    return pl.pallas_call(
        _moe_group_kernel,
        out_shape=jax.ShapeDtypeStruct((rows, d), F32),
        grid_spec=grid_spec,
        compiler_params=_cparams(2, VMEM_LIMIT),
        name="moe_group_ffn",
    )(tile_expert, tile_live, xs, wg, wu, wd)


def _moe_combine_ln_kernel(x_ref, r_ref, y0_ref, y1_ref, g_ref, b_ref, o_ref):
    r = r_ref[...]
    ff = r[:, 0:1] * y0_ref[...] + r[:, 1:2] * y1_ref[...]
    o_ref[...] = _layer_norm(ALPHA * x_ref[...] + ff, g_ref[...], b_ref[...])


def _moe_combine_ln(x, route, y, g, b, tm):
    s, d = x.shape
    nt = s // tm
    return pl.pallas_call(
        _moe_combine_ln_kernel,
        out_shape=jax.ShapeDtypeStruct((s, d), F32),
        grid=(nt,),
        in_specs=[pl.BlockSpec((tm, d), lambda i: (i, 0)),
                  pl.BlockSpec((tm, LANES), lambda i: (i, 0)),
                  pl.BlockSpec((tm, d), lambda i: (i, 0)),
                  pl.BlockSpec((tm, d), lambda i: (i + nt, 0)),
                  pl.BlockSpec((1, d), lambda i: (0, 0)),
                  pl.BlockSpec((1, d), lambda i: (0, 0))],
        out_specs=pl.BlockSpec((tm, d), lambda i: (i, 0)),
        compiler_params=_cparams(1, VMEM_LIMIT),
        name="moe_combine_ln",
    )(x, route, y, y, g.reshape(1, d), b.reshape(1, d))


def _moe_ln(x, route, counts, wg, wu, wd, g, b, tm):
    pair_row, row_token, tile_expert, tile_live = _dispatch_plan(route, counts, tm)
    xs = _row_gather(x, row_token, MOE_GATHER_CHUNK)
    ys = _moe_group_ffn(xs, tile_expert, tile_live, wg, wu, wd, tm, D_EXPERT // 2)
    slot_major = pair_row.T.reshape(-1)
    y = _row_gather(ys, slot_major, MOE_GATHER_CHUNK)
    return _moe_combine_ln(x, route, y, g, b, tm)


def kernel(x, w_in_even, w_out_even, forget_bias, w_in_odd, w_out_odd, lambda_q, lambda_k,
           subln_gain, rel_bias, w_router, b_router, w_gate, w_up, w_down, ln_gain, ln_bias):
    bsz, s, d = x.shape
    assert bsz == 1 and d == D_MODEL and s % MOBA_BLOCK == 0
    t = ATT_TILE
    nh = N_HEADS_GROUP
    gw = GROUP_WIDTH
    tm_proj = min(1024, s)
    tm_row = min(512, s)
    tk = min(ATT_KEYS, s)
    n_near = _near_tile_count(t)
    assert s % tk == 0 and tk // t + 1 >= n_near
    nblk = s // MOBA_BLOCK
    nbp = -(-nblk // LANES) * LANES

    x2 = x.reshape(s, d)
    bias_tiles = _bias_tiles(rel_bias, t, n_near)
    tab_spec = pl.BlockSpec(memory_space=pltpu.SMEM)
    bias_spec = pl.BlockSpec((None, n_near + 2, t, t), lambda hd, i: (hd, 0, 0, 0))
    block_onehot = (jnp.arange(s, dtype=jnp.int32)[:, None] // MOBA_BLOCK
                    == jnp.arange(nbp, dtype=jnp.int32)[None, :]).astype(BF16)

    for layer in range(DEPTH):
        li = layer // 2
        if layer % 2 == 0:
            w = w_in_even[li]
            fcol = 3 * gw
            w_main = jnp.concatenate([w[:, :fcol], w[:, fcol + nh:]], axis=1).astype(BF16)
            w_f = jnp.zeros((d, LANES), BF16).at[:, :nh].set(w[:, fcol:fcol + nh].astype(BF16))
            h = _in_proj(x2, w_main, BF16, tm_proj, 512)
            f = _in_proj(x2, w_f, F32, tm_proj, LANES)
            c = _decay_cumsum(f[:, :nh].T, forget_bias[li])
            oa = _attention_call(
                functools.partial(_fox_kernel, t=FOX_TQ, tk=min(FOX_TK, s), scale=HEAD_DIM ** -0.5), h, gw, 0,
                ([c[:, min(FOX_TK, s) - 1::min(FOX_TK, s)], c[:, ::FOX_TQ]],
                 [c.reshape(nh, 1, s), c.reshape(nh, s, 1)]),
                ([tab_spec, tab_spec],
                 [pl.BlockSpec((None, 1, s), lambda hd, i: (hd, 0, 0)),
                  pl.BlockSpec((None, FOX_TQ, 1), lambda hd, i: (hd, i, 0))]),
                [pltpu.VMEM((FOX_TQ, min(FOX_TK, s)), F32)] * 2 + [pltpu.VMEM((8, LANES), F32)],
                "fox_attention", FOX_TQ)
            ob = _attention_call(
                functools.partial(_moba_kernel, tb=t, nb=MOBA_Q_BLOCKS, tk=tk, nblk=nblk, n_near=n_near,
                                  scale=HEAD_DIM ** -0.5),
                h, gw, 3 * nh,
                ([rel_bias], [bias_tiles, block_onehot]),
                ([tab_spec], [bias_spec, pl.BlockSpec((s, nbp), lambda hd, i: (0, 0))]),
                [pltpu.VMEM((nbp, HEAD_DIM), F32)] + [pltpu.VMEM((MOBA_Q_BLOCKS * t, tk), F32)] * 2
                + [pltpu.VMEM((8, LANES), F32)], "moba_attention", MOBA_Q_BLOCKS * t)
            w_out = w_out_even[li]
        else:
            lambda_init = 0.8 - 0.6 * math.exp(-0.3 * layer)
            h = _in_proj(x2, w_in_odd[li].astype(BF16), BF16, tm_proj, 512)
            small = lambda shape: pl.BlockSpec(shape, lambda hd, i: (0, 0))
            oa = _attention_call(
                functools.partial(_diff_kernel, tb=t, nb=DIFF_Q_BLOCKS, tk=tk, n_near=n_near,
                                  scale=DIFF_QK_DIM ** -0.5, lambda_init=lambda_init),
                h, gw, 0,
                ([rel_bias], [bias_tiles, lambda_q[li], lambda_k[li], subln_gain[li].reshape(1, HEAD_DIM)]),
                ([tab_spec], [bias_spec, small((2, DIFF_QK_DIM)), small((2, DIFF_QK_DIM)),
                              small((1, HEAD_DIM))]),
                [pltpu.VMEM((2 * DIFF_Q_BLOCKS * t, tk), F32)] * 2 + [pltpu.VMEM((8, LANES), F32)],
                "diff_attention", DIFF_Q_BLOCKS * t)
            ob = _attention_call(
                functools.partial(_sb_kernel, t=t, scale=HEAD_DIM ** -0.5), h, gw, 3 * nh,
                ([], []), ([], []), [], "stickbreak_attention", t)
            w_out = w_out_odd[li]
        x2 = _out_proj_ln(oa, ob, w_out.astype(BF16), x2, ln_gain[layer, 0], ln_bias[layer, 0], tm_row)
        route, counts = _router(x2, w_router, b_router, tm_row)
        x2 = _moe_ln(x2, route, counts, w_gate[layer].astype(BF16), w_up[layer].astype(BF16),
                     w_down[layer].astype(BF16), ln_gain[layer, 1], ln_bias[layer, 1], tm_row)
    return x2.reshape(bsz, s, d)
```

```python
import functools
import math

import numpy as np
import jax
import jax.numpy as jnp
from jax import lax
from jax.experimental import pallas as pl
from jax.experimental.pallas import tpu as pltpu

F32 = jnp.float32
BF16 = jnp.bfloat16

D_MODEL = 2048
DEPTH = 2
HEAD_DIM = 128
N_HEADS_GROUP = 8
GROUP_WIDTH = N_HEADS_GROUP * HEAD_DIM
DIFF_QK_DIM = HEAD_DIM // 2
MOBA_BLOCK = 256
MOBA_TOPK = 3
N_BUCKETS = 32
MAX_EXACT = N_BUCKETS // 2
MAX_DISTANCE = 1024
N_EXPERTS = 16
N_GROUPS = 4
EXPERTS_PER_GROUP = N_EXPERTS // N_GROUPS
MOE_TOP_K = 2
MOE_GATHER_CHUNK = 512
SUBLANES = 8
D_EXPERT = D_MODEL // 2
ALPHA = (2.0 * DEPTH) ** 0.25
LN_EPS = 1e-5
NEG_INF = -1e30
LOG2E = math.log2(math.e)
LANES = 128
ATT_TILE = 256
ATT_KEYS = 1024
FOX_TQ = 1024
FOX_TK = 1024
MOBA_Q_BLOCKS = 4
SB_Q_TILES = 2
DIFF_Q_BLOCKS = 2
SHIFT_SLACK = 96.0
FLUSH_EXPONENT = -130.0
SB_CUTOFF = -104.0
VMEM_LIMIT = 56 * 1024 * 1024


def _cparams(n_axes, vmem=None):
    return pltpu.CompilerParams(dimension_semantics=("arbitrary",) * n_axes,
                                vmem_limit_bytes=vmem)


def _nt_dot(a, b):
    return lax.dot_general(a, b, (((1,), (1,)), ((), ())), preferred_element_type=F32)


def _split_bf16(x):
    hi = x.astype(BF16)
    lo = (x - hi.astype(F32)).astype(BF16)
    return hi, lo


def _layer_norm(z, g, b):
    mu = jnp.mean(z, axis=-1, keepdims=True)
    zc = z - mu
    var = jnp.mean(zc * zc, axis=-1, keepdims=True)
    return zc * lax.rsqrt(var + LN_EPS) * g + b


def _inproj_kernel(x_ref, w_ref, o_ref, xb_ref):
    @pl.when(pl.program_id(1) == 0)
    def _():
        xb_ref[...] = x_ref[...].astype(BF16)

    o_ref[...] = jnp.dot(xb_ref[...], w_ref[...], preferred_element_type=F32).astype(o_ref.dtype)


def _in_proj(x, w, out_dtype, tm, tn):
    s, d = x.shape
    n = w.shape[1]
    return pl.pallas_call(
        _inproj_kernel,
        out_shape=jax.ShapeDtypeStruct((s, n), out_dtype),
        grid=(s // tm, n // tn),
        in_specs=[pl.BlockSpec((tm, d), lambda i, j: (i, 0)),
                  pl.BlockSpec((d, tn), lambda i, j: (0, j))],
        out_specs=pl.BlockSpec((tm, tn), lambda i, j: (i, j)),
        scratch_shapes=[pltpu.VMEM((tm, d), BF16)],
        compiler_params=_cparams(2, VMEM_LIMIT),
        name="in_proj",
    )(x, w)


def _outproj_ln_kernel(oa_ref, ob_ref, wa_ref, wb_ref, x_ref, g_ref, b_ref, o_ref):
    y = jnp.dot(oa_ref[...], wa_ref[...], preferred_element_type=F32)
    y = y + jnp.dot(ob_ref[...], wb_ref[...], preferred_element_type=F32)
    o_ref[...] = _layer_norm(ALPHA * x_ref[...] + y, g_ref[...], b_ref[...])


def _out_proj_ln(oa, ob, w, x, g, b, tm):
    s, d = x.shape
    gw = oa.shape[1]
    return pl.pallas_call(
        _outproj_ln_kernel,
        out_shape=jax.ShapeDtypeStruct((s, d), F32),
        grid=(s // tm,),
        in_specs=[pl.BlockSpec((tm, gw), lambda i: (i, 0)),
                  pl.BlockSpec((tm, gw), lambda i: (i, 0)),
                  pl.BlockSpec((gw, d), lambda i: (0, 0)),
                  pl.BlockSpec((gw, d), lambda i: (1, 0)),
                  pl.BlockSpec((tm, d), lambda i: (i, 0)),
                  pl.BlockSpec((1, d), lambda i: (0, 0)),
                  pl.BlockSpec((1, d), lambda i: (0, 0))],
        out_specs=pl.BlockSpec((tm, d), lambda i: (i, 0)),
        compiler_params=_cparams(1, VMEM_LIMIT),
        name="out_proj_ln",
    )(oa, ob, w, w, x, g.reshape(1, d), b.reshape(1, d))


def _router_kernel(x_ref, w_ref, b_ref, route_ref, count_ref):
    route, counts = _route(x_ref[...], w_ref[...], b_ref[...])
    route_ref[...] = route
    count_ref[...] = jnp.broadcast_to(counts, count_ref.shape)


def _router(x, w_router, b_router, tm):
    s, d = x.shape
    wr = jnp.zeros((d, LANES), F32).at[:, :N_EXPERTS].set(w_router)
    br = jnp.zeros((1, LANES), F32).at[0, :N_EXPERTS].set(b_router)
    return pl.pallas_call(
        _router_kernel,
        out_shape=(jax.ShapeDtypeStruct((s, LANES), F32),
                   jax.ShapeDtypeStruct((s // tm, SUBLANES, LANES), F32)),
        grid=(s // tm,),
        in_specs=[pl.BlockSpec((tm, d), lambda i: (i, 0)),
                  pl.BlockSpec((d, LANES), lambda i: (0, 0)),
                  pl.BlockSpec((1, LANES), lambda i: (0, 0))],
        out_specs=(pl.BlockSpec((tm, LANES), lambda i: (i, 0)),
                   pl.BlockSpec((None, SUBLANES, LANES), lambda i: (i, 0, 0))),
        compiler_params=_cparams(1, VMEM_LIMIT),
        name="router",
    )(x, wr, br)


def _decay_cumsum_kernel(f_ref, b_ref, c_ref):
    nh, s = f_ref.shape
    rows = lax.broadcasted_iota(jnp.int32, (LANES, LANES), 0)
    cols = lax.broadcasted_iota(jnp.int32, (LANES, LANES), 1)
    upper = jnp.where(rows <= cols, 1.0, 0.0).astype(F32)
    bias = b_ref[...]

    def body(n, carry):
        off = pl.multiple_of(n * LANES, LANES)
        z = f_ref[:, pl.ds(off, LANES)] + bias
        logf = jnp.minimum(z, 0.0) - jnp.log1p(jnp.exp(-jnp.abs(z)))
        c = jnp.dot(logf, upper, preferred_element_type=F32,
                    precision=lax.Precision.HIGHEST) + carry
        c_ref[:, pl.ds(off, LANES)] = c * LOG2E
        return carry + jnp.sum(logf, axis=-1, keepdims=True)

    lax.fori_loop(0, s // LANES, body, jnp.zeros((nh, 1), F32))


def _decay_cumsum(f_t, bias):
    nh, s = f_t.shape
    return pl.pallas_call(
        _decay_cumsum_kernel,
        out_shape=jax.ShapeDtypeStruct((nh, s), F32),
        name="decay_cumsum",
    )(f_t, bias.reshape(nh, 1))


def _t5_bucket_np(dist):
    n = np.maximum(dist, 0)
    nf = np.maximum(n, 1).astype(np.float32)
    ratio = np.log(nf / np.float32(MAX_EXACT)) / np.float32(math.log(MAX_DISTANCE / MAX_EXACT))
    large = MAX_EXACT + (ratio.astype(np.float32) * np.float32(N_BUCKETS - MAX_EXACT)).astype(np.int32)
    large = np.minimum(large, N_BUCKETS - 1)
    return np.where(n < MAX_EXACT, n, large).astype(np.int32)


def _near_tile_count(t):
    d = np.arange(0, 4 * MAX_DISTANCE, dtype=np.int64)
    not_last = np.nonzero(_t5_bucket_np(d) != N_BUCKETS - 1)[0]
    d_sat = int(not_last.max()) + 1
    n = 1
    while (n - 1) * t + 1 < d_sat:
        n += 1
    return n


def _bucket_tiles_np(t, n_near):
    r = np.arange(t)[:, None]
    c = np.arange(t)[None, :]
    tiles = []
    for delta in range(n_near):
        dist = delta * t + r - c
        tiles.append(np.where(dist >= 0, _t5_bucket_np(dist), -1))
    tiles.append(np.full((t, t), N_BUCKETS - 1))
    tiles.append(np.full((t, t), -1))
    return np.stack(tiles).astype(np.int32)


def _bias_tiles_kernel(tab_ref, idx_ref, o_ref):
    h = pl.program_id(0)
    idx = idx_ref[...]
    out = jnp.full(idx.shape, NEG_INF, F32)
    for b in range(N_BUCKETS):
        out = jnp.where(idx == b, tab_ref[b, h] * LOG2E, out)
    o_ref[...] = out


def _bias_tiles(rel_bias, t, n_near):
    nh = rel_bias.shape[1]
    idx = jnp.asarray(_bucket_tiles_np(t, n_near))
    n_tiles = idx.shape[0]
    return pl.pallas_call(
        _bias_tiles_kernel,
        out_shape=jax.ShapeDtypeStruct((nh, n_tiles, t, t), F32),
        grid=(nh, n_tiles),
        in_specs=[pl.BlockSpec(memory_space=pltpu.SMEM),
                  pl.BlockSpec((None, t, t), lambda h, n: (n, 0, 0))],
        out_specs=pl.BlockSpec((None, None, t, t), lambda h, n: (h, n, 0, 0)),
        compiler_params=_cparams(2),
        name="bias_tiles",
    )(rel_bias, idx)


def _online_softmax_step(s, v, carry):
    m, l, acc = carry
    m_new = jnp.maximum(m, jnp.max(s, axis=-1, keepdims=True))
    alpha = jnp.exp2(m - m_new)
    p = jnp.exp2(s - m_new)
    l = alpha * l + jnp.sum(p, axis=-1, keepdims=True)
    acc = alpha * acc + jnp.dot(p.astype(BF16), v, preferred_element_type=F32)
    return m_new, l, acc


def _prescale(q, c):
    return (q.astype(F32) * c).astype(BF16)


def _near_bias(bias_ref, blocks, j, tb, tk, n_near):
    nsub = tk // tb
    rows = []
    for i in blocks:
        tiles = []
        for b in range(nsub):
            delta = i - (j * nsub + b)
            tiles.append(bias_ref[jnp.where(delta < 0, n_near + 1, jnp.minimum(delta, n_near))])
        rows.append(jnp.concatenate(tiles, axis=1))
    return rows[0] if len(rows) == 1 else jnp.concatenate(rows, axis=0)


def _biased_sweep(i, t, tk, scores, process, sa_ref, sb_ref, init):
    jd = (i * t) // tk
    carry = process(scores(jd), jd, init, True)
    lo = jnp.maximum(jd - 1, 0)
    carry = lax.fori_loop(lo, jd, lambda j, c: process(scores(j), j, c, True), carry)
    return _pipelined_sweep(lo, scores, lambda s, j, c: process(s, j, c, False), sa_ref, sb_ref, carry)


def _softmax_init(rows):
    return (jnp.full((rows, 1), NEG_INF, F32), jnp.zeros((rows, 1), F32),
            jnp.zeros((rows, HEAD_DIM), F32))


def _kv_tile(k_ref, v_ref, j, t):
    off = pl.multiple_of(j * t, t)
    return k_ref[pl.ds(off, t), :], v_ref[pl.ds(off, t), :], off


def _pipelined_sweep(n, scores, process, sa_ref, sb_ref, carry):
    last = jnp.maximum(n - 1, 0)
    sa_ref[...] = scores(0)

    def pair(p, c):
        j = 2 * p
        sb_ref[...] = scores(jnp.minimum(j + 1, last))
        c = process(sa_ref[...], j, c)
        sa_ref[...] = scores(jnp.minimum(j + 2, last))
        return process(sb_ref[...], j + 1, c)

    carry = lax.fori_loop(0, n // 2, pair, carry)
    return lax.fori_loop(0, n % 2, lambda _, c: process(sa_ref[...], n - 1, c), carry)


def _fox_kernel(cend_ref, cstart_ref, q_ref, k_ref, v_ref, c_ref, ccol_ref, o_ref, sa_ref, sb_ref,
                knorm_ref, *, t, tk, scale):
    h = pl.program_id(0)
    i = pl.program_id(1)

    @pl.when(i == 0)
    def _():
        knorm_ref[...] = jnp.broadcast_to(_max_key_norm(k_ref, tk, 0, HEAD_DIM), knorm_ref.shape)

    q = _prescale(q_ref[...], scale * LOG2E)
    rows = lax.broadcasted_iota(jnp.int32, (t, tk), 0)
    cols = lax.broadcasted_iota(jnp.int32, (t, tk), 1)

    def scores(j):
        off = pl.multiple_of(j * tk, tk)
        return _nt_dot(q, k_ref[pl.ds(off, tk), :]) - c_ref[:, pl.ds(off, tk)]

    def values(j):
        return v_ref[pl.ds(pl.multiple_of(j * tk, tk), tk), :]

    def process(s, j, carry):
        return _online_softmax_step(s, values(j), carry)

    jd = (i * t) // tk
    s_diag = jnp.where(cols <= rows + (i * t - jd * tk), scores(jd), NEG_INF)

    qf = q.astype(F32)
    qnorm = jnp.sqrt(jnp.sum(qf * qf, axis=-1, keepdims=True))
    bound = qnorm * knorm_ref[0:1, 0:1] * (1.0 + 2.0 ** -9) + 2.0 ** -6 - ccol_ref[...]
    slack = jnp.max(bound - jnp.max(s_diag, axis=-1, keepdims=True))

    def shifted_sweep(_):
        def step(s, j, carry):
            l, acc = carry
            p = jnp.exp2(s - bound)
            return (l + jnp.sum(p, axis=-1, keepdims=True),
                    acc + jnp.dot(p.astype(BF16), values(j), preferred_element_type=F32))

        c_first_row = cstart_ref[h, i]
        first_live = jnp.int32(0)
        for j in range(cend_ref.shape[1]):
            dead = jnp.logical_and(j < jd, c_first_row - cend_ref[h, j] < FLUSH_EXPONENT)
            first_live = first_live + dead.astype(jnp.int32)
        carry = step(s_diag, jd, _softmax_init(t)[1:])
        return _paired_loop(jd - first_live, lambda n, c: step(scores(first_live + n), first_live + n, c),
                            carry)

    def running_max_sweep(_):
        carry = _online_softmax_step(s_diag, values(jd), _softmax_init(t))
        return _pipelined_sweep(jd, scores, process, sa_ref, sb_ref, carry)[1:]

    l, acc = lax.cond(slack <= SHIFT_SLACK, shifted_sweep, running_max_sweep, 0)
    o_ref[...] = (acc / l).astype(o_ref.dtype)


def _moba_kernel(tab_ref, q_ref, k_ref, v_ref, bias_ref, et_ref, o_ref, kmean_ref, sa_ref, sb_ref,
                 knorm_ref, *, tb, nb, tk, nblk, n_near, scale):
    h = pl.program_id(0)
    g = pl.program_id(1)
    t = nb * tb
    nbp = kmean_ref.shape[0]

    @pl.when(g == 0)
    def _():
        kmean_ref[...] = jnp.zeros(kmean_ref.shape, F32)

        def mean_body(n, _):
            off = pl.multiple_of(n * tb, tb)
            kb = k_ref[pl.ds(off, tb), :].astype(F32)
            kmean_ref[pl.ds(n, 1), :] = jnp.sum(kb, axis=0, keepdims=True) * (1.0 / tb)
            return 0

        lax.fori_loop(0, nblk, mean_body, 0)
        knorm_ref[...] = jnp.broadcast_to(_max_key_norm(k_ref, tk, 0, HEAD_DIM), knorm_ref.shape)

    q = q_ref[...]
    km_hi, km_lo = _split_bf16(kmean_ref[...])
    gate = _nt_dot(q, km_hi) + _nt_dot(q, km_lo)
    blk = lax.broadcasted_iota(jnp.int32, (t, nbp), 1)
    own = g * nb + lax.broadcasted_iota(jnp.int32, (t, 1), 0) // tb
    eligible = blk < own
    sel = jnp.zeros((t, nbp), F32)
    for r in range(MOBA_TOPK):
        cand = jnp.where(eligible, jnp.where(sel > 0.0, -jnp.inf, gate), -jnp.inf)
        vmax = jnp.max(cand, axis=-1, keepdims=True)
        first = jnp.min(jnp.where(cand == vmax, blk, nbp), axis=-1, keepdims=True)
        take = jnp.where(r < own, 1.0, 0.0)
        sel = jnp.maximum(sel, jnp.where(blk == first, take, 0.0))

    sel = jnp.maximum(sel, jnp.where(blk == own, 1.0, 0.0))
    penalty = jnp.where(sel > 0.0, 0.0, NEG_INF).astype(BF16)
    q_aug = jnp.concatenate([_prescale(q, scale * LOG2E), penalty], axis=1)
    b_far = tab_ref[N_BUCKETS - 1, h] * LOG2E

    def scores(j):
        off = pl.multiple_of(j * tk, tk)
        k_aug = jnp.concatenate([k_ref[pl.ds(off, tk), :], et_ref[pl.ds(off, tk), :]], axis=1)
        return _nt_dot(q_aug, k_aug)

    def values(j):
        return v_ref[pl.ds(pl.multiple_of(j * tk, tk), tk), :]

    def near_bias(j):
        return _near_bias(bias_ref, [g * nb + b for b in range(nb)], j, tb, tk, n_near)

    def process(s, j, carry, near):
        return _online_softmax_step(s + (near_bias(j) if near else b_far), values(j), carry)

    qf = q_aug[:, :HEAD_DIM].astype(F32)
    qnorm = jnp.sqrt(jnp.sum(qf * qf, axis=-1, keepdims=True))
    bound = _logit_bound(qnorm, knorm_ref[0:1, 0:1], tab_ref, h)
    l, acc = _shifted_or_running_sweep(g, t, tk, scores, near_bias, values, process, bound, b_far,
                                       sa_ref, sb_ref, t)
    o_ref[...] = (acc / l).astype(o_ref.dtype)


def _paired_loop(n, body, carry):
    carry = lax.fori_loop(0, n // 2, lambda p, c: body(2 * p + 1, body(2 * p, c)), carry)
    return lax.fori_loop(0, n % 2, lambda _, c: body(n - 1, c), carry)


def _logit_bound(qnorm, knorm, tab_ref, h):
    b_max = tab_ref[0, h]
    for b in range(1, N_BUCKETS):
        b_max = jnp.maximum(b_max, tab_ref[b, h])
    return qnorm * knorm * (1.0 + 2.0 ** -9) + (b_max * LOG2E + 2.0 ** -6)


def _shifted_or_running_sweep(i, t, tk, scores, near_bias, values, process, bound, b_far,
                              sa_ref, sb_ref, rows):
    jd = (i * t) // tk
    s_diag = scores(jd) + near_bias(jd)
    slack = jnp.max(bound - jnp.max(s_diag, axis=-1, keepdims=True))

    def shifted_sweep(_):
        def step(s, j, carry, shift):
            l, acc = carry
            p = jnp.exp2(s - shift)
            return (l + jnp.sum(p, axis=-1, keepdims=True),
                    acc + jnp.dot(p.astype(BF16), values(j), preferred_element_type=F32))

        carry = step(s_diag, jd, _softmax_init(rows)[1:], bound)
        lo = jnp.maximum(jd - 1, 0)
        carry = lax.fori_loop(lo, jd, lambda j, c: step(scores(j) + near_bias(j), j, c, bound), carry)
        far_shift = bound - b_far
        return _paired_loop(lo, lambda j, c: step(scores(j), j, c, far_shift), carry)

    def running_max_sweep(_):
        return _biased_sweep(i, t, tk, scores, process, sa_ref, sb_ref, _softmax_init(rows))[1:]

    return lax.cond(slack <= SHIFT_SLACK, shifted_sweep, running_max_sweep, 0)


def _max_key_norm(k_ref, tk, lane_lo, lane_hi):
    klane = lax.broadcasted_iota(jnp.int32, (tk, HEAD_DIM), 1)
    keep = jnp.logical_and(klane >= lane_lo, klane < lane_hi)

    def body(n, c):
        kt = k_ref[pl.ds(pl.multiple_of(n * tk, tk), tk), :].astype(F32)
        sq = jnp.sum(jnp.where(keep, kt * kt, 0.0), axis=-1, keepdims=True)
        return jnp.maximum(c, jnp.max(sq, axis=0, keepdims=True))

    return jnp.sqrt(lax.fori_loop(0, k_ref.shape[0] // tk, body, jnp.zeros((1, 1), F32)))


def _diff_kernel(tab_ref, q_ref, k_ref, v_ref, bias_ref, lq_ref, lk_ref, g_ref, o_ref, sa_ref, sb_ref,
                 knorm_ref, *, tb, nb, tk, n_near, scale, lambda_init):
    h = pl.program_id(0)
    i = pl.program_id(1)
    t = nb * tb
    q = _prescale(q_ref[...], scale * LOG2E)
    lane = lax.broadcasted_iota(jnp.int32, (t, HEAD_DIM), 1)
    zero = jnp.zeros_like(q)
    q2 = jnp.concatenate([jnp.where(lane < DIFF_QK_DIM, q, zero),
                          jnp.where(lane >= DIFF_QK_DIM, q, zero)], axis=0)
    b_far = tab_ref[N_BUCKETS - 1, h] * LOG2E

    @pl.when(i == 0)
    def _():
        knorm_ref[0:1, :] = jnp.broadcast_to(_max_key_norm(k_ref, tk, 0, DIFF_QK_DIM), (1, LANES))
        knorm_ref[1:2, :] = jnp.broadcast_to(_max_key_norm(k_ref, tk, DIFF_QK_DIM, HEAD_DIM), (1, LANES))

    def scores(j):
        return _nt_dot(q2, k_ref[pl.ds(pl.multiple_of(j * tk, tk), tk), :])

    def values(j):
        return v_ref[pl.ds(pl.multiple_of(j * tk, tk), tk), :]

    def near_bias(j):
        b = _near_bias(bias_ref, [i * nb + b for b in range(nb)], j, tb, tk, n_near)
        return jnp.concatenate([b, b], axis=0)

    def process(s, j, carry, near):
        return _online_softmax_step(s + (near_bias(j) if near else b_far), values(j), carry)

    q2f = q2.astype(F32)
    qnorm = jnp.sqrt(jnp.sum(q2f * q2f, axis=-1, keepdims=True))
    knorm = jnp.concatenate([jnp.broadcast_to(knorm_ref[0:1, 0:1], (t, 1)),
                             jnp.broadcast_to(knorm_ref[1:2, 0:1], (t, 1))], axis=0)
    bound = _logit_bound(qnorm, knorm, tab_ref, h)
    l, acc = _shifted_or_running_sweep(i, t, tk, scores, near_bias, values, process, bound, b_far,
                                       sa_ref, sb_ref, 2 * t)
    o12 = acc / l
    lam_e = jnp.exp(jnp.sum(lq_ref[...] * lk_ref[...], axis=-1, keepdims=True))
    lam = lam_e[0:1, :] - lam_e[1:2, :] + lambda_init
    o = o12[:t, :] - lam * o12[t:, :]
    o = o * lax.rsqrt(jnp.mean(o * o, axis=-1, keepdims=True) + LN_EPS) * g_ref[...]
    o_ref[...] = (o * (1.0 - lambda_init)).astype(o_ref.dtype)


def _sb_kernel(q_ref, k_ref, v_ref, o_ref, *, t, nq, scale):
    g = pl.program_id(1)

    def later_matrix(n):
        r = lax.broadcasted_iota(jnp.int32, (n, n), 0)
        c = lax.broadcasted_iota(jnp.int32, (n, n), 1)
        return jnp.where(r > c, 1.0, 0.0).astype(BF16)

    def block(q, off, nk, rem, acc, shift):
        k = k_ref[pl.ds(off, nk), :]
        v = v_ref[pl.ds(off, nk), :]
        z = _nt_dot(q, k) * scale
        soft = jnp.log1p(jnp.exp(-jnp.abs(z)))
        log_sig = jnp.minimum(z, 0.0) - soft
        log_rem = jnp.minimum(-z, 0.0) - soft
        if shift is not None:
            rows = lax.broadcasted_iota(jnp.int32, (t, nk), 0)
            cols = lax.broadcasted_iota(jnp.int32, (t, nk), 1)
            mask = cols < rows + shift
            log_rem = jnp.where(mask, log_rem, 0.0)
        hi, lo = _split_bf16(log_rem)
        later = later_matrix(nk)
        after = (jnp.dot(hi, later, preferred_element_type=F32)
                 + jnp.dot(lo, later, preferred_element_type=F32))
        a = jnp.exp(log_sig + after + rem)
        if shift is not None:
            a = jnp.where(mask, a, 0.0)
        acc = acc + jnp.dot(a.astype(BF16), v, preferred_element_type=F32)
        rem = rem + jnp.sum(log_rem, axis=-1, keepdims=True)
        return rem, acc

    def live(rem):
        return (jnp.max(rem) > SB_CUTOFF).astype(jnp.int32)

    state = []
    for u in range(nq):
        i = g * nq + u
        q = q_ref[u * t:(u + 1) * t, :]
        first = jnp.maximum(i - 1, 0)
        rem, acc = block(q, pl.multiple_of(first * t, t), 2 * t, jnp.zeros((t, 1), F32),
                         jnp.zeros((t, HEAD_DIM), F32), (i - first) * t)
        state.append((i, q, rem, acc))

    for u, (i, q, rem, acc) in enumerate(state):
        def cond(c):
            j, go, _, _ = c
            return jnp.logical_and(j >= 0, go > 0)

        def body(c, q=q):
            j, _, rem, acc = c
            rem, acc = block(q, pl.multiple_of(j * t, t), t, rem, acc, None)
            return j - 1, live(rem), rem, acc

        _, _, _, acc = lax.while_loop(cond, body, (i - 2, live(rem), rem, acc))
        o_ref[u * t:(u + 1) * t, :] = acc.astype(o_ref.dtype)


def _attention_call(kernel, h, out_cols, col0, extra_in, extra_specs, scratch, name, t):
    s = h.shape[0]
    nh = N_HEADS_GROUP
    in_specs = list(extra_specs[0]) + [
        pl.BlockSpec((t, HEAD_DIM), lambda hd, i: (i, col0 + hd)),
        pl.BlockSpec((s, HEAD_DIM), lambda hd, i: (0, col0 + nh + hd)),
        pl.BlockSpec((s, HEAD_DIM), lambda hd, i: (0, col0 + 2 * nh + hd)),
    ] + list(extra_specs[1])
    args = list(extra_in[0]) + [h, h, h] + list(extra_in[1])
    return pl.pallas_call(
        kernel,
        out_shape=jax.ShapeDtypeStruct((s, out_cols), BF16),
        grid=(nh, s // t),
        in_specs=in_specs,
        out_specs=pl.BlockSpec((t, HEAD_DIM), lambda hd, i: (i, hd)),
        scratch_shapes=scratch,
        compiler_params=_cparams(2, VMEM_LIMIT),
        name=name,
    )(*args)


def _route(x, w, b):
    logits = jnp.dot(x, w, preferred_element_type=F32, precision=lax.Precision.HIGHEST) + b
    tm, n = logits.shape
    lane = lax.broadcasted_iota(jnp.int32, (tm, n), 1)
    real = lane < N_EXPERTS
    logits = jnp.where(real, logits, -jnp.inf)
    e = jnp.exp(logits - jnp.max(logits, axis=-1, keepdims=True))
    aff = e / jnp.sum(e, axis=-1, keepdims=True)
    group = lane // EXPERTS_PER_GROUP
    best = jnp.full((tm, 1), -jnp.inf, F32)
    g_sel = jnp.zeros((tm, 1), jnp.int32)
    for g in range(N_GROUPS):
        gmax = jnp.max(jnp.where(group == g, aff, -jnp.inf), axis=-1, keepdims=True)
        better = gmax > best
        g_sel = jnp.where(better, g, g_sel)
        best = jnp.where(better, gmax, best)
    in_group = jnp.logical_and(group == g_sel, real)
    cand = jnp.where(in_group, aff, -jnp.inf)
    v1 = jnp.max(cand, axis=-1, keepdims=True)
    i1 = jnp.min(jnp.where(cand == v1, lane, n), axis=-1, keepdims=True)
    cand2 = jnp.where(lane == i1, -jnp.inf, cand)
    v2 = jnp.max(cand2, axis=-1, keepdims=True)
    i2 = jnp.min(jnp.where(cand2 == v2, lane, n), axis=-1, keepdims=True)
    tot = v1 + v2
    chosen = jnp.where(jnp.logical_or(lane == i1, lane == i2), 1.0, 0.0)
    r = lax.broadcasted_iota(jnp.int32, (tm, tm), 0)
    c = lax.broadcasted_iota(jnp.int32, (tm, tm), 1)
    earlier = jnp.where(c < r, 1.0, 0.0).astype(BF16)
    before = jnp.dot(earlier, chosen.astype(BF16), preferred_element_type=F32)
    rank1 = jnp.sum(jnp.where(lane == i1, before, 0.0), axis=-1, keepdims=True)
    rank2 = jnp.sum(jnp.where(lane == i2, before, 0.0), axis=-1, keepdims=True)
    route = jnp.where(lane == 0, v1 / tot,
                      jnp.where(lane == 1, v2 / tot,
                                jnp.where(lane == 2, i1.astype(F32),
                                          jnp.where(lane == 3, i2.astype(F32),
                                                    jnp.where(lane == 4, rank1,
                                                              jnp.where(lane == 5, rank2, 0.0))))))
    return route, jnp.sum(chosen, axis=0, keepdims=True)


def _dispatch_plan(route, counts, tm):
    s = route.shape[0]
    nt = s // tm
    n_tiles = (MOE_TOP_K * s) // tm + N_EXPERTS
    ids = jnp.arange(N_EXPERTS, dtype=jnp.int32)
    expert = route[:, 2:2 + MOE_TOP_K].astype(jnp.int32).reshape(nt, tm, MOE_TOP_K)
    rank = route[:, 4:4 + MOE_TOP_K].astype(jnp.int32).reshape(nt, tm, MOE_TOP_K)
    cnt = counts[:, 0, :N_EXPERTS].astype(jnp.int32)
    tile_base = jnp.cumsum(cnt, axis=0) - cnt
    padded = -(-jnp.sum(cnt, axis=0) // tm) * tm
    seg_end = jnp.cumsum(padded)
    base = (seg_end - padded)[None, :] + tile_base
    pick = expert[..., None] == ids
    pair_row = (jnp.sum(jnp.where(pick, base[:, None, None, :], 0), axis=-1) + rank).reshape(s, MOE_TOP_K)
    tile_start = jnp.arange(n_tiles, dtype=jnp.int32) * tm
    tile_expert = jnp.minimum(jnp.sum((tile_start[:, None] >= seg_end[None, :]).astype(jnp.int32), axis=1),
                              N_EXPERTS - 1).astype(jnp.int32)
    tile_live = (tile_start < seg_end[-1]).astype(jnp.int32)
    row_token = (jnp.arange(n_tiles * tm, dtype=jnp.int32) % s).at[pair_row.reshape(-1)].set(
        jnp.arange(MOE_TOP_K * s, dtype=jnp.int32) // MOE_TOP_K)
    return pair_row, row_token, tile_expert, tile_live


def _row_gather_kernel(idx_ref, src_ref, o_ref, sem, *, chunk):
    def issue(g, _):
        base = pl.multiple_of(g * SUBLANES, SUBLANES)
        for u in range(SUBLANES):
            tok = idx_ref[0, base + u]
            pltpu.make_async_copy(
                src_ref.at[lax.shift_right_logical(tok, 3), pl.ds(jnp.bitwise_and(tok, SUBLANES - 1), 1)],
                o_ref.at[g, pl.ds(u, 1)], sem).start()
        return 0

    lax.fori_loop(0, chunk // SUBLANES, issue, 0)
    pltpu.make_async_copy(src_ref.at[pl.ds(0, chunk // SUBLANES)], o_ref, sem).wait()


def _row_gather(src, idx, chunk):
    n = idx.shape[0]
    rows, d = src.shape
    out = pl.pallas_call(
        functools.partial(_row_gather_kernel, chunk=chunk),
        out_shape=jax.ShapeDtypeStruct((n // SUBLANES, SUBLANES, d), src.dtype),
        grid=(n // chunk,),
        in_specs=[pl.BlockSpec((None, 1, chunk), lambda c: (c, 0, 0), memory_space=pltpu.SMEM),
                  pl.BlockSpec(memory_space=pl.ANY)],
        out_specs=pl.BlockSpec((chunk // SUBLANES, SUBLANES, d), lambda c: (c, 0, 0)),
        scratch_shapes=[pltpu.SemaphoreType.DMA(())],
        compiler_params=_cparams(1, VMEM_LIMIT),
        name="row_gather",
    )(idx.reshape(n // chunk, 1, chunk), src.reshape(rows // SUBLANES, SUBLANES, d))
    return out.reshape(n, d)


def _moe_group_kernel(te_ref, live_ref, x_ref, wg_ref, wu_ref, wd_ref, o_ref, xb_ref):
    g = pl.program_id(0)
    f = pl.program_id(1)
    live = live_ref[g] > 0

    @pl.when(jnp.logical_and(live, f == 0))
    def _():
        xb_ref[...] = x_ref[...].astype(BF16)

    @pl.when(jnp.logical_and(jnp.logical_not(live), f == 0))
    def _():
        o_ref[...] = jnp.zeros(o_ref.shape, F32)

    @pl.when(live)
    def _():
        xb = xb_ref[...]
        a = jnp.dot(xb, wg_ref[...], preferred_element_type=F32)
        u = jnp.dot(xb, wu_ref[...], preferred_element_type=F32)
        hid = ((a * jax.nn.sigmoid(a)) * u).astype(BF16)
        y = jnp.dot(hid, wd_ref[...], preferred_element_type=F32)

        @pl.when(f == 0)
        def _():
            o_ref[...] = y

        @pl.when(f > 0)
        def _():
            o_ref[...] += y


def _moe_group_ffn(xs, tile_expert, tile_live, wg, wu, wd, tm, tf):
    rows, d = xs.shape
    fdim = wg.shape[2]
    grid_spec = pltpu.PrefetchScalarGridSpec(
        num_scalar_prefetch=2,
        grid=(rows // tm, fdim // tf),
        in_specs=[pl.BlockSpec((tm, d), lambda g, f, te, lv: (g, 0)),
                  pl.BlockSpec((None, d, tf), lambda g, f, te, lv: (te[g], 0, f)),
                  pl.BlockSpec((None, d, tf), lambda g, f, te, lv: (te[g], 0, f)),
                  pl.BlockSpec((None, tf, d), lambda g, f, te, lv: (te[g], f, 0))],
        out_specs=pl.BlockSpec((tm, d), lambda g, f, te, lv: (g, 0)),
        scratch_shapes=[pltpu.VMEM((tm, d), BF16)])
    return pl.pallas_call(
        _moe_group_kernel,
        out_shape=jax.ShapeDtypeStruct((rows, d), F32),
        grid_spec=grid_spec,
        compiler_params=_cparams(2, VMEM_LIMIT),
        name="moe_group_ffn",
    )(tile_expert, tile_live, xs, wg, wu, wd)


def _moe_combine_ln_kernel(x_ref, r_ref, y0_ref, y1_ref, g_ref, b_ref, o_ref):
    r = r_ref[...]
    ff = r[:, 0:1] * y0_ref[...] + r[:, 1:2] * y1_ref[...]
    o_ref[...] = _layer_norm(ALPHA * x_ref[...] + ff, g_ref[...], b_ref[...])


def _moe_combine_ln(x, route, y, g, b, tm):
    s, d = x.shape
    nt = s // tm
    return pl.pallas_call(
        _moe_combine_ln_kernel,
        out_shape=jax.ShapeDtypeStruct((s, d), F32),
        grid=(nt,),
        in_specs=[pl.BlockSpec((tm, d), lambda i: (i, 0)),
                  pl.BlockSpec((tm, LANES), lambda i: (i, 0)),
                  pl.BlockSpec((tm, d), lambda i: (i, 0)),
                  pl.BlockSpec((tm, d), lambda i: (i + nt, 0)),
                  pl.BlockSpec((1, d), lambda i: (0, 0)),
                  pl.BlockSpec((1, d), lambda i: (0, 0))],
        out_specs=pl.BlockSpec((tm, d), lambda i: (i, 0)),
        compiler_params=_cparams(1, VMEM_LIMIT),
        name="moe_combine_ln",
    )(x, route, y, y, g.reshape(1, d), b.reshape(1, d))


def _moe_ln(x, route, counts, wg, wu, wd, g, b, tm):
    pair_row, row_token, tile_expert, tile_live = _dispatch_plan(route, counts, tm)
    xs = _row_gather(x, row_token, MOE_GATHER_CHUNK)
    ys = _moe_group_ffn(xs, tile_expert, tile_live, wg, wu, wd, tm, D_EXPERT // 2)
    slot_major = pair_row.T.reshape(-1)
    y = _row_gather(ys, slot_major, MOE_GATHER_CHUNK)
    return _moe_combine_ln(x, route, y, g, b, tm)


def kernel(x, w_in_even, w_out_even, forget_bias, w_in_odd, w_out_odd, lambda_q, lambda_k,
           subln_gain, rel_bias, w_router, b_router, w_gate, w_up, w_down, ln_gain, ln_bias):
    bsz, s, d = x.shape
    assert bsz == 1 and d == D_MODEL and s % MOBA_BLOCK == 0
    t = ATT_TILE
    nh = N_HEADS_GROUP
    gw = GROUP_WIDTH
    tm_proj = min(1024, s)
    tm_row = min(512, s)
    tk = min(ATT_KEYS, s)
    n_near = _near_tile_count(t)
    assert s % tk == 0 and tk // t + 1 >= n_near
    nblk = s // MOBA_BLOCK
    nbp = -(-nblk // LANES) * LANES

    x2 = x.reshape(s, d)
    bias_tiles = _bias_tiles(rel_bias, t, n_near)
    tab_spec = pl.BlockSpec(memory_space=pltpu.SMEM)
    bias_spec = pl.BlockSpec((None, n_near + 2, t, t), lambda hd, i: (hd, 0, 0, 0))
    block_onehot = (jnp.arange(s, dtype=jnp.int32)[:, None] // MOBA_BLOCK
                    == jnp.arange(nbp, dtype=jnp.int32)[None, :]).astype(BF16)

    for layer in range(DEPTH):
        li = layer // 2
        if layer % 2 == 0:
            w = w_in_even[li]
            fcol = 3 * gw
            w_main = jnp.concatenate([w[:, :fcol], w[:, fcol + nh:]], axis=1).astype(BF16)
            w_f = jnp.zeros((d, LANES), BF16).at[:, :nh].set(w[:, fcol:fcol + nh].astype(BF16))
            h = _in_proj(x2, w_main, BF16, tm_proj, 512)
            f = _in_proj(x2, w_f, F32, tm_proj, LANES)
            c = _decay_cumsum(f[:, :nh].T, forget_bias[li])
            oa = _attention_call(
                functools.partial(_fox_kernel, t=FOX_TQ, tk=min(FOX_TK, s), scale=HEAD_DIM ** -0.5), h, gw, 0,
                ([c[:, min(FOX_TK, s) - 1::min(FOX_TK, s)], c[:, ::FOX_TQ]],
                 [c.reshape(nh, 1, s), c.reshape(nh, s, 1)]),
                ([tab_spec, tab_spec],
                 [pl.BlockSpec((None, 1, s), lambda hd, i: (hd, 0, 0)),
                  pl.BlockSpec((None, FOX_TQ, 1), lambda hd, i: (hd, i, 0))]),
                [pltpu.VMEM((FOX_TQ, min(FOX_TK, s)), F32)] * 2 + [pltpu.VMEM((8, LANES), F32)],
                "fox_attention", FOX_TQ)
            ob = _attention_call(
                functools.partial(_moba_kernel, tb=t, nb=MOBA_Q_BLOCKS, tk=tk, nblk=nblk, n_near=n_near,
                                  scale=HEAD_DIM ** -0.5),
                h, gw, 3 * nh,
                ([rel_bias], [bias_tiles, block_onehot]),
                ([tab_spec], [bias_spec, pl.BlockSpec((s, nbp), lambda hd, i: (0, 0))]),
                [pltpu.VMEM((nbp, HEAD_DIM), F32)] + [pltpu.VMEM((MOBA_Q_BLOCKS * t, tk), F32)] * 2
                + [pltpu.VMEM((8, LANES), F32)], "moba_attention", MOBA_Q_BLOCKS * t)
            w_out = w_out_even[li]
        else:
            lambda_init = 0.8 - 0.6 * math.exp(-0.3 * layer)
            h = _in_proj(x2, w_in_odd[li].astype(BF16), BF16, tm_proj, 512)
            small = lambda shape: pl.BlockSpec(shape, lambda hd, i: (0, 0))
            oa = _attention_call(
                functools.partial(_diff_kernel, tb=t, nb=DIFF_Q_BLOCKS, tk=tk, n_near=n_near,
                                  scale=DIFF_QK_DIM ** -0.5, lambda_init=lambda_init),
                h, gw, 0,
                ([rel_bias], [bias_tiles, lambda_q[li], lambda_k[li], subln_gain[li].reshape(1, HEAD_DIM)]),
                ([tab_spec], [bias_spec, small((2, DIFF_QK_DIM)), small((2, DIFF_QK_DIM)),
                              small((1, HEAD_DIM))]),
                [pltpu.VMEM((2 * DIFF_Q_BLOCKS * t, tk), F32)] * 2 + [pltpu.VMEM((8, LANES), F32)],
                "diff_attention", DIFF_Q_BLOCKS * t)
            ob = _attention_call(
                functools.partial(_sb_kernel, t=t, nq=SB_Q_TILES, scale=HEAD_DIM ** -0.5), h, gw, 3 * nh,
                ([], []), ([], []), [], "stickbreak_attention", SB_Q_TILES * t)
            w_out = w_out_odd[li]
        x2 = _out_proj_ln(oa, ob, w_out.astype(BF16), x2, ln_gain[layer, 0], ln_bias[layer, 0], tm_row)
        route, counts = _router(x2, w_router, b_router, tm_row)
        x2 = _moe_ln(x2, route, counts, w_gate[layer].astype(BF16), w_up[layer].astype(BF16),
                     w_down[layer].astype(BF16), ln_gain[layer, 1], ln_bias[layer, 1], tm_row)
    return x2.reshape(bsz, s, d)
```

```python
import functools
import math

import numpy as np
import jax
import jax.numpy as jnp
from jax import lax
from jax.experimental import pallas as pl
from jax.experimental.pallas import tpu as pltpu
from jax.experimental.pallas import tpu_sc as plsc

F32 = jnp.float32
BF16 = jnp.bfloat16

D_MODEL = 2048
DEPTH = 2
HEAD_DIM = 128
N_HEADS_GROUP = 8
GROUP_WIDTH = N_HEADS_GROUP * HEAD_DIM
DIFF_QK_DIM = HEAD_DIM // 2
MOBA_BLOCK = 256
MOBA_TOPK = 3
N_BUCKETS = 32
MAX_EXACT = N_BUCKETS // 2
MAX_DISTANCE = 1024
N_EXPERTS = 16
N_GROUPS = 4
EXPERTS_PER_GROUP = N_EXPERTS // N_GROUPS
MOE_TOP_K = 2
MOE_GATHER_CHUNK = 512
SC_CORES = 2
SC_SUBCORES = 16
SC_GATHER_ROWS = 32
SUBLANES = 8
D_EXPERT = D_MODEL // 2
ALPHA = (2.0 * DEPTH) ** 0.25
LN_EPS = 1e-5
NEG_INF = -1e30
LOG2E = math.log2(math.e)
LANES = 128
ATT_TILE = 256
ATT_KEYS = 1024
FOX_TQ = 1024
FOX_TK = 1024
MOBA_Q_BLOCKS = 4
SB_Q_TILES = 2
DIFF_Q_BLOCKS = 2
SHIFT_SLACK = 96.0
FLUSH_EXPONENT = -130.0
SB_CUTOFF = -104.0
VMEM_LIMIT = 56 * 1024 * 1024


def _cparams(n_axes, vmem=None):
    return pltpu.CompilerParams(dimension_semantics=("arbitrary",) * n_axes,
                                vmem_limit_bytes=vmem)


def _nt_dot(a, b):
    return lax.dot_general(a, b, (((1,), (1,)), ((), ())), preferred_element_type=F32)


def _split_bf16(x):
    hi = x.astype(BF16)
    lo = (x - hi.astype(F32)).astype(BF16)
    return hi, lo


def _layer_norm(z, g, b):
    mu = jnp.mean(z, axis=-1, keepdims=True)
    zc = z - mu
    var = jnp.mean(zc * zc, axis=-1, keepdims=True)
    return zc * lax.rsqrt(var + LN_EPS) * g + b


def _inproj_kernel(x_ref, w_ref, o_ref, xb_ref):
    @pl.when(pl.program_id(1) == 0)
    def _():
        xb_ref[...] = x_ref[...].astype(BF16)

    o_ref[...] = jnp.dot(xb_ref[...], w_ref[...], preferred_element_type=F32).astype(o_ref.dtype)


def _in_proj(x, w, out_dtype, tm, tn):
    s, d = x.shape
    n = w.shape[1]
    return pl.pallas_call(
        _inproj_kernel,
        out_shape=jax.ShapeDtypeStruct((s, n), out_dtype),
        grid=(s // tm, n // tn),
        in_specs=[pl.BlockSpec((tm, d), lambda i, j: (i, 0)),
                  pl.BlockSpec((d, tn), lambda i, j: (0, j))],
        out_specs=pl.BlockSpec((tm, tn), lambda i, j: (i, j)),
        scratch_shapes=[pltpu.VMEM((tm, d), BF16)],
        compiler_params=_cparams(2, VMEM_LIMIT),
        name="in_proj",
    )(x, w)


def _outproj_ln_kernel(oa_ref, ob_ref, wa_ref, wb_ref, x_ref, g_ref, b_ref, o_ref):
    y = jnp.dot(oa_ref[...], wa_ref[...], preferred_element_type=F32)
    y = y + jnp.dot(ob_ref[...], wb_ref[...], preferred_element_type=F32)
    o_ref[...] = _layer_norm(ALPHA * x_ref[...] + y, g_ref[...], b_ref[...])


def _out_proj_ln(oa, ob, w, x, g, b, tm):
    s, d = x.shape
    gw = oa.shape[1]
    return pl.pallas_call(
        _outproj_ln_kernel,
        out_shape=jax.ShapeDtypeStruct((s, d), F32),
        grid=(s // tm,),
        in_specs=[pl.BlockSpec((tm, gw), lambda i: (i, 0)),
                  pl.BlockSpec((tm, gw), lambda i: (i, 0)),
                  pl.BlockSpec((gw, d), lambda i: (0, 0)),
                  pl.BlockSpec((gw, d), lambda i: (1, 0)),
                  pl.BlockSpec((tm, d), lambda i: (i, 0)),
                  pl.BlockSpec((1, d), lambda i: (0, 0)),
                  pl.BlockSpec((1, d), lambda i: (0, 0))],
        out_specs=pl.BlockSpec((tm, d), lambda i: (i, 0)),
        compiler_params=_cparams(1, VMEM_LIMIT),
        name="out_proj_ln",
    )(oa, ob, w, w, x, g.reshape(1, d), b.reshape(1, d))


def _router_kernel(x_ref, w_ref, b_ref, route_ref, count_ref):
    route, counts = _route(x_ref[...], w_ref[...], b_ref[...])
    route_ref[...] = route
    count_ref[...] = jnp.broadcast_to(counts, count_ref.shape)


def _router(x, w_router, b_router, tm):
    s, d = x.shape
    wr = jnp.zeros((d, LANES), F32).at[:, :N_EXPERTS].set(w_router)
    br = jnp.zeros((1, LANES), F32).at[0, :N_EXPERTS].set(b_router)
    return pl.pallas_call(
        _router_kernel,
        out_shape=(jax.ShapeDtypeStruct((s, LANES), F32),
                   jax.ShapeDtypeStruct((s // tm, SUBLANES, LANES), F32)),
        grid=(s // tm,),
        in_specs=[pl.BlockSpec((tm, d), lambda i: (i, 0)),
                  pl.BlockSpec((d, LANES), lambda i: (0, 0)),
                  pl.BlockSpec((1, LANES), lambda i: (0, 0))],
        out_specs=(pl.BlockSpec((tm, LANES), lambda i: (i, 0)),
                   pl.BlockSpec((None, SUBLANES, LANES), lambda i: (i, 0, 0))),
        compiler_params=_cparams(1, VMEM_LIMIT),
        name="router",
    )(x, wr, br)


def _decay_cumsum_kernel(f_ref, b_ref, c_ref):
    nh, s = f_ref.shape
    rows = lax.broadcasted_iota(jnp.int32, (LANES, LANES), 0)
    cols = lax.broadcasted_iota(jnp.int32, (LANES, LANES), 1)
    upper = jnp.where(rows <= cols, 1.0, 0.0).astype(F32)
    bias = b_ref[...]

    def body(n, carry):
        off = pl.multiple_of(n * LANES, LANES)
        z = f_ref[:, pl.ds(off, LANES)] + bias
        logf = jnp.minimum(z, 0.0) - jnp.log1p(jnp.exp(-jnp.abs(z)))
        c = jnp.dot(logf, upper, preferred_element_type=F32,
                    precision=lax.Precision.HIGHEST) + carry
        c_ref[:, pl.ds(off, LANES)] = c * LOG2E
        return carry + jnp.sum(logf, axis=-1, keepdims=True)

    lax.fori_loop(0, s // LANES, body, jnp.zeros((nh, 1), F32))


def _decay_cumsum(f_t, bias):
    nh, s = f_t.shape
    return pl.pallas_call(
        _decay_cumsum_kernel,
        out_shape=jax.ShapeDtypeStruct((nh, s), F32),
        name="decay_cumsum",
    )(f_t, bias.reshape(nh, 1))


def _t5_bucket_np(dist):
    n = np.maximum(dist, 0)
    nf = np.maximum(n, 1).astype(np.float32)
    ratio = np.log(nf / np.float32(MAX_EXACT)) / np.float32(math.log(MAX_DISTANCE / MAX_EXACT))
    large = MAX_EXACT + (ratio.astype(np.float32) * np.float32(N_BUCKETS - MAX_EXACT)).astype(np.int32)
    large = np.minimum(large, N_BUCKETS - 1)
    return np.where(n < MAX_EXACT, n, large).astype(np.int32)


def _near_tile_count(t):
    d = np.arange(0, 4 * MAX_DISTANCE, dtype=np.int64)
    not_last = np.nonzero(_t5_bucket_np(d) != N_BUCKETS - 1)[0]
    d_sat = int(not_last.max()) + 1
    n = 1
    while (n - 1) * t + 1 < d_sat:
        n += 1
    return n


def _bucket_tiles_np(t, n_near):
    r = np.arange(t)[:, None]
    c = np.arange(t)[None, :]
    tiles = []
    for delta in range(n_near):
        dist = delta * t + r - c
        tiles.append(np.where(dist >= 0, _t5_bucket_np(dist), -1))
    tiles.append(np.full((t, t), N_BUCKETS - 1))
    tiles.append(np.full((t, t), -1))
    return np.stack(tiles).astype(np.int32)


def _bias_tiles_kernel(tab_ref, idx_ref, o_ref):
    h = pl.program_id(0)
    idx = idx_ref[...]
    out = jnp.full(idx.shape, NEG_INF, F32)
    for b in range(N_BUCKETS):
        out = jnp.where(idx == b, tab_ref[b, h] * LOG2E, out)
    o_ref[...] = out


def _bias_tiles(rel_bias, t, n_near):
    nh = rel_bias.shape[1]
    idx = jnp.asarray(_bucket_tiles_np(t, n_near))
    n_tiles = idx.shape[0]
    return pl.pallas_call(
        _bias_tiles_kernel,
        out_shape=jax.ShapeDtypeStruct((nh, n_tiles, t, t), F32),
        grid=(nh, n_tiles),
        in_specs=[pl.BlockSpec(memory_space=pltpu.SMEM),
                  pl.BlockSpec((None, t, t), lambda h, n: (n, 0, 0))],
        out_specs=pl.BlockSpec((None, None, t, t), lambda h, n: (h, n, 0, 0)),
        compiler_params=_cparams(2),
        name="bias_tiles",
    )(rel_bias, idx)


def _online_softmax_step(s, v, carry):
    m, l, acc = carry
    m_new = jnp.maximum(m, jnp.max(s, axis=-1, keepdims=True))
    alpha = jnp.exp2(m - m_new)
    p = jnp.exp2(s - m_new)
    l = alpha * l + jnp.sum(p, axis=-1, keepdims=True)
    acc = alpha * acc + jnp.dot(p.astype(BF16), v, preferred_element_type=F32)
    return m_new, l, acc


def _prescale(q, c):
    return (q.astype(F32) * c).astype(BF16)


def _near_bias(bias_ref, blocks, j, tb, tk, n_near):
    nsub = tk // tb
    rows = []
    for i in blocks:
        tiles = []
        for b in range(nsub):
            delta = i - (j * nsub + b)
            tiles.append(bias_ref[jnp.where(delta < 0, n_near + 1, jnp.minimum(delta, n_near))])
        rows.append(jnp.concatenate(tiles, axis=1))
    return rows[0] if len(rows) == 1 else jnp.concatenate(rows, axis=0)


def _biased_sweep(i, t, tk, scores, process, sa_ref, sb_ref, init):
    jd = (i * t) // tk
    carry = process(scores(jd), jd, init, True)
    lo = jnp.maximum(jd - 1, 0)
    carry = lax.fori_loop(lo, jd, lambda j, c: process(scores(j), j, c, True), carry)
    return _pipelined_sweep(lo, scores, lambda s, j, c: process(s, j, c, False), sa_ref, sb_ref, carry)


def _softmax_init(rows):
    return (jnp.full((rows, 1), NEG_INF, F32), jnp.zeros((rows, 1), F32),
            jnp.zeros((rows, HEAD_DIM), F32))


def _kv_tile(k_ref, v_ref, j, t):
    off = pl.multiple_of(j * t, t)
    return k_ref[pl.ds(off, t), :], v_ref[pl.ds(off, t), :], off


def _pipelined_sweep(n, scores, process, sa_ref, sb_ref, carry):
    last = jnp.maximum(n - 1, 0)
    sa_ref[...] = scores(0)

    def pair(p, c):
        j = 2 * p
        sb_ref[...] = scores(jnp.minimum(j + 1, last))
        c = process(sa_ref[...], j, c)
        sa_ref[...] = scores(jnp.minimum(j + 2, last))
        return process(sb_ref[...], j + 1, c)

    carry = lax.fori_loop(0, n // 2, pair, carry)
    return lax.fori_loop(0, n % 2, lambda _, c: process(sa_ref[...], n - 1, c), carry)


def _fox_kernel(cend_ref, cstart_ref, q_ref, k_ref, v_ref, c_ref, ccol_ref, o_ref, sa_ref, sb_ref,
                knorm_ref, *, t, tk, scale):
    h = pl.program_id(0)
    i = pl.program_id(1)

    @pl.when(i == 0)
    def _():
        knorm_ref[...] = jnp.broadcast_to(_max_key_norm(k_ref, tk, 0, HEAD_DIM), knorm_ref.shape)

    q = _prescale(q_ref[...], scale * LOG2E)
    rows = lax.broadcasted_iota(jnp.int32, (t, tk), 0)
    cols = lax.broadcasted_iota(jnp.int32, (t, tk), 1)

    def scores(j):
        off = pl.multiple_of(j * tk, tk)
        return _nt_dot(q, k_ref[pl.ds(off, tk), :]) - c_ref[:, pl.ds(off, tk)]

    def values(j):
        return v_ref[pl.ds(pl.multiple_of(j * tk, tk), tk), :]

    def process(s, j, carry):
        return _online_softmax_step(s, values(j), carry)

    jd = (i * t) // tk
    s_diag = jnp.where(cols <= rows + (i * t - jd * tk), scores(jd), NEG_INF)

    qf = q.astype(F32)
    qnorm = jnp.sqrt(jnp.sum(qf * qf, axis=-1, keepdims=True))
    bound = qnorm * knorm_ref[0:1, 0:1] * (1.0 + 2.0 ** -9) + 2.0 ** -6 - ccol_ref[...]
    slack = jnp.max(bound - jnp.max(s_diag, axis=-1, keepdims=True))

    def shifted_sweep(_):
        def step(s, j, carry):
            l, acc = carry
            p = jnp.exp2(s - bound)
            return (l + jnp.sum(p, axis=-1, keepdims=True),
                    acc + jnp.dot(p.astype(BF16), values(j), preferred_element_type=F32))

        c_first_row = cstart_ref[h, i]
        first_live = jnp.int32(0)
        for j in range(cend_ref.shape[1]):
            dead = jnp.logical_and(j < jd, c_first_row - cend_ref[h, j] < FLUSH_EXPONENT)
            first_live = first_live + dead.astype(jnp.int32)
        carry = step(s_diag, jd, _softmax_init(t)[1:])
        return _paired_loop(jd - first_live, lambda n, c: step(scores(first_live + n), first_live + n, c),
                            carry)

    def running_max_sweep(_):
        carry = _online_softmax_step(s_diag, values(jd), _softmax_init(t))
        return _pipelined_sweep(jd, scores, process, sa_ref, sb_ref, carry)[1:]

    l, acc = lax.cond(slack <= SHIFT_SLACK, shifted_sweep, running_max_sweep, 0)
    o_ref[...] = (acc / l).astype(o_ref.dtype)


def _moba_kernel(tab_ref, q_ref, k_ref, v_ref, bias_ref, et_ref, o_ref, kmean_ref, sa_ref, sb_ref,
                 knorm_ref, *, tb, nb, tk, nblk, n_near, scale):
    h = pl.program_id(0)
    g = pl.program_id(1)
    t = nb * tb
    nbp = kmean_ref.shape[0]

    @pl.when(g == 0)
    def _():
        kmean_ref[...] = jnp.zeros(kmean_ref.shape, F32)

        def mean_body(n, _):
            off = pl.multiple_of(n * tb, tb)
            kb = k_ref[pl.ds(off, tb), :].astype(F32)
            kmean_ref[pl.ds(n, 1), :] = jnp.sum(kb, axis=0, keepdims=True) * (1.0 / tb)
            return 0

        lax.fori_loop(0, nblk, mean_body, 0)
        knorm_ref[...] = jnp.broadcast_to(_max_key_norm(k_ref, tk, 0, HEAD_DIM), knorm_ref.shape)

    q = q_ref[...]
    km_hi, km_lo = _split_bf16(kmean_ref[...])
    gate = _nt_dot(q, km_hi) + _nt_dot(q, km_lo)
    blk = lax.broadcasted_iota(jnp.int32, (t, nbp), 1)
    own = g * nb + lax.broadcasted_iota(jnp.int32, (t, 1), 0) // tb
    eligible = blk < own
    sel = jnp.zeros((t, nbp), F32)
    for r in range(MOBA_TOPK):
        cand = jnp.where(eligible, jnp.where(sel > 0.0, -jnp.inf, gate), -jnp.inf)
        vmax = jnp.max(cand, axis=-1, keepdims=True)
        first = jnp.min(jnp.where(cand == vmax, blk, nbp), axis=-1, keepdims=True)
        take = jnp.where(r < own, 1.0, 0.0)
        sel = jnp.maximum(sel, jnp.where(blk == first, take, 0.0))

    sel = jnp.maximum(sel, jnp.where(blk == own, 1.0, 0.0))
    penalty = jnp.where(sel > 0.0, 0.0, NEG_INF).astype(BF16)
    q_aug = jnp.concatenate([_prescale(q, scale * LOG2E), penalty], axis=1)
    b_far = tab_ref[N_BUCKETS - 1, h] * LOG2E

    def scores(j):
        off = pl.multiple_of(j * tk, tk)
        k_aug = jnp.concatenate([k_ref[pl.ds(off, tk), :], et_ref[pl.ds(off, tk), :]], axis=1)
        return _nt_dot(q_aug, k_aug)

    def values(j):
        return v_ref[pl.ds(pl.multiple_of(j * tk, tk), tk), :]

    def near_bias(j):
        return _near_bias(bias_ref, [g * nb + b for b in range(nb)], j, tb, tk, n_near)

    def process(s, j, carry, near):
        return _online_softmax_step(s + (near_bias(j) if near else b_far), values(j), carry)

    qf = q_aug[:, :HEAD_DIM].astype(F32)
    qnorm = jnp.sqrt(jnp.sum(qf * qf, axis=-1, keepdims=True))
    bound = _logit_bound(qnorm, knorm_ref[0:1, 0:1], tab_ref, h)
    l, acc = _shifted_or_running_sweep(g, t, tk, scores, near_bias, values, process, bound, b_far,
                                       sa_ref, sb_ref, t)
    o_ref[...] = (acc / l).astype(o_ref.dtype)


def _paired_loop(n, body, carry):
    carry = lax.fori_loop(0, n // 2, lambda p, c: body(2 * p + 1, body(2 * p, c)), carry)
    return lax.fori_loop(0, n % 2, lambda _, c: body(n - 1, c), carry)


def _logit_bound(qnorm, knorm, tab_ref, h):
    b_max = tab_ref[0, h]
    for b in range(1, N_BUCKETS):
        b_max = jnp.maximum(b_max, tab_ref[b, h])
    return qnorm * knorm * (1.0 + 2.0 ** -9) + (b_max * LOG2E + 2.0 ** -6)


def _shifted_or_running_sweep(i, t, tk, scores, near_bias, values, process, bound, b_far,
                              sa_ref, sb_ref, rows):
    jd = (i * t) // tk
    s_diag = scores(jd) + near_bias(jd)
    slack = jnp.max(bound - jnp.max(s_diag, axis=-1, keepdims=True))

    def shifted_sweep(_):
        def step(s, j, carry, shift):
            l, acc = carry
            p = jnp.exp2(s - shift)
            return (l + jnp.sum(p, axis=-1, keepdims=True),
                    acc + jnp.dot(p.astype(BF16), values(j), preferred_element_type=F32))

        carry = step(s_diag, jd, _softmax_init(rows)[1:], bound)
        lo = jnp.maximum(jd - 1, 0)
        carry = lax.fori_loop(lo, jd, lambda j, c: step(scores(j) + near_bias(j), j, c, bound), carry)
        far_shift = bound - b_far
        return _paired_loop(lo, lambda j, c: step(scores(j), j, c, far_shift), carry)

    def running_max_sweep(_):
        return _biased_sweep(i, t, tk, scores, process, sa_ref, sb_ref, _softmax_init(rows))[1:]

    return lax.cond(slack <= SHIFT_SLACK, shifted_sweep, running_max_sweep, 0)


def _max_key_norm(k_ref, tk, lane_lo, lane_hi):
    klane = lax.broadcasted_iota(jnp.int32, (tk, HEAD_DIM), 1)
    keep = jnp.logical_and(klane >= lane_lo, klane < lane_hi)

    def body(n, c):
        kt = k_ref[pl.ds(pl.multiple_of(n * tk, tk), tk), :].astype(F32)
        sq = jnp.sum(jnp.where(keep, kt * kt, 0.0), axis=-1, keepdims=True)
        return jnp.maximum(c, jnp.max(sq, axis=0, keepdims=True))

    return jnp.sqrt(lax.fori_loop(0, k_ref.shape[0] // tk, body, jnp.zeros((1, 1), F32)))


def _diff_kernel(tab_ref, q_ref, k_ref, v_ref, bias_ref, lq_ref, lk_ref, g_ref, o_ref, sa_ref, sb_ref,
                 knorm_ref, *, tb, nb, tk, n_near, scale, lambda_init):
    h = pl.program_id(0)
    i = pl.program_id(1)
    t = nb * tb
    q = _prescale(q_ref[...], scale * LOG2E)
    lane = lax.broadcasted_iota(jnp.int32, (t, HEAD_DIM), 1)
    zero = jnp.zeros_like(q)
    q2 = jnp.concatenate([jnp.where(lane < DIFF_QK_DIM, q, zero),
                          jnp.where(lane >= DIFF_QK_DIM, q, zero)], axis=0)
    b_far = tab_ref[N_BUCKETS - 1, h] * LOG2E

    @pl.when(i == 0)
    def _():
        knorm_ref[0:1, :] = jnp.broadcast_to(_max_key_norm(k_ref, tk, 0, DIFF_QK_DIM), (1, LANES))
        knorm_ref[1:2, :] = jnp.broadcast_to(_max_key_norm(k_ref, tk, DIFF_QK_DIM, HEAD_DIM), (1, LANES))

    def scores(j):
        return _nt_dot(q2, k_ref[pl.ds(pl.multiple_of(j * tk, tk), tk), :])

    def values(j):
        return v_ref[pl.ds(pl.multiple_of(j * tk, tk), tk), :]

    def near_bias(j):
        b = _near_bias(bias_ref, [i * nb + b for b in range(nb)], j, tb, tk, n_near)
        return jnp.concatenate([b, b], axis=0)

    def process(s, j, carry, near):
        return _online_softmax_step(s + (near_bias(j) if near else b_far), values(j), carry)

    q2f = q2.astype(F32)
    qnorm = jnp.sqrt(jnp.sum(q2f * q2f, axis=-1, keepdims=True))
    knorm = jnp.concatenate([jnp.broadcast_to(knorm_ref[0:1, 0:1], (t, 1)),
                             jnp.broadcast_to(knorm_ref[1:2, 0:1], (t, 1))], axis=0)
    bound = _logit_bound(qnorm, knorm, tab_ref, h)
    l, acc = _shifted_or_running_sweep(i, t, tk, scores, near_bias, values, process, bound, b_far,
                                       sa_ref, sb_ref, 2 * t)
    o12 = acc / l
    lam_e = jnp.exp(jnp.sum(lq_ref[...] * lk_ref[...], axis=-1, keepdims=True))
    lam = lam_e[0:1, :] - lam_e[1:2, :] + lambda_init
    o = o12[:t, :] - lam * o12[t:, :]
    o = o * lax.rsqrt(jnp.mean(o * o, axis=-1, keepdims=True) + LN_EPS) * g_ref[...]
    o_ref[...] = (o * (1.0 - lambda_init)).astype(o_ref.dtype)


def _sb_kernel(q_ref, k_ref, v_ref, o_ref, *, t, nq, scale):
    g = pl.program_id(1)

    def later_matrix(n):
        r = lax.broadcasted_iota(jnp.int32, (n, n), 0)
        c = lax.broadcasted_iota(jnp.int32, (n, n), 1)
        return jnp.where(r > c, 1.0, 0.0).astype(BF16)

    def block(q, off, nk, rem, acc, shift):
        k = k_ref[pl.ds(off, nk), :]
        v = v_ref[pl.ds(off, nk), :]
        z = _nt_dot(q, k) * scale
        soft = jnp.log1p(jnp.exp(-jnp.abs(z)))
        log_sig = jnp.minimum(z, 0.0) - soft
        log_rem = jnp.minimum(-z, 0.0) - soft
        if shift is not None:
            rows = lax.broadcasted_iota(jnp.int32, (t, nk), 0)
            cols = lax.broadcasted_iota(jnp.int32, (t, nk), 1)
            mask = cols < rows + shift
            log_rem = jnp.where(mask, log_rem, 0.0)
        hi, lo = _split_bf16(log_rem)
        later = later_matrix(nk)
        after = (jnp.dot(hi, later, preferred_element_type=F32)
                 + jnp.dot(lo, later, preferred_element_type=F32))
        a = jnp.exp(log_sig + after + rem)
        if shift is not None:
            a = jnp.where(mask, a, 0.0)
        acc = acc + jnp.dot(a.astype(BF16), v, preferred_element_type=F32)
        rem = rem + jnp.sum(log_rem, axis=-1, keepdims=True)
        return rem, acc

    def live(rem):
        return (jnp.max(rem) > SB_CUTOFF).astype(jnp.int32)

    state = []
    for u in range(nq):
        i = g * nq + u
        q = q_ref[u * t:(u + 1) * t, :]
        first = jnp.maximum(i - 1, 0)
        rem, acc = block(q, pl.multiple_of(first * t, t), 2 * t, jnp.zeros((t, 1), F32),
                         jnp.zeros((t, HEAD_DIM), F32), (i - first) * t)
        state.append((i, q, rem, acc))

    for u, (i, q, rem, acc) in enumerate(state):
        def cond(c):
            j, go, _, _ = c
            return jnp.logical_and(j >= 0, go > 0)

        def body(c, q=q):
            j, _, rem, acc = c
            rem, acc = block(q, pl.multiple_of(j * t, t), t, rem, acc, None)
            return j - 1, live(rem), rem, acc

        _, _, _, acc = lax.while_loop(cond, body, (i - 2, live(rem), rem, acc))
        o_ref[u * t:(u + 1) * t, :] = acc.astype(o_ref.dtype)


def _attention_call(kernel, h, out_cols, col0, extra_in, extra_specs, scratch, name, t):
    s = h.shape[0]
    nh = N_HEADS_GROUP
    in_specs = list(extra_specs[0]) + [
        pl.BlockSpec((t, HEAD_DIM), lambda hd, i: (i, col0 + hd)),
        pl.BlockSpec((s, HEAD_DIM), lambda hd, i: (0, col0 + nh + hd)),
        pl.BlockSpec((s, HEAD_DIM), lambda hd, i: (0, col0 + 2 * nh + hd)),
    ] + list(extra_specs[1])
    args = list(extra_in[0]) + [h, h, h] + list(extra_in[1])
    return pl.pallas_call(
        kernel,
        out_shape=jax.ShapeDtypeStruct((s, out_cols), BF16),
        grid=(nh, s // t),
        in_specs=in_specs,
        out_specs=pl.BlockSpec((t, HEAD_DIM), lambda hd, i: (i, hd)),
        scratch_shapes=scratch,
        compiler_params=_cparams(2, VMEM_LIMIT),
        name=name,
    )(*args)


def _route(x, w, b):
    logits = jnp.dot(x, w, preferred_element_type=F32, precision=lax.Precision.HIGHEST) + b
    tm, n = logits.shape
    lane = lax.broadcasted_iota(jnp.int32, (tm, n), 1)
    real = lane < N_EXPERTS
    logits = jnp.where(real, logits, -jnp.inf)
    e = jnp.exp(logits - jnp.max(logits, axis=-1, keepdims=True))
    aff = e / jnp.sum(e, axis=-1, keepdims=True)
    group = lane // EXPERTS_PER_GROUP
    best = jnp.full((tm, 1), -jnp.inf, F32)
    g_sel = jnp.zeros((tm, 1), jnp.int32)
    for g in range(N_GROUPS):
        gmax = jnp.max(jnp.where(group == g, aff, -jnp.inf), axis=-1, keepdims=True)
        better = gmax > best
        g_sel = jnp.where(better, g, g_sel)
        best = jnp.where(better, gmax, best)
    in_group = jnp.logical_and(group == g_sel, real)
    cand = jnp.where(in_group, aff, -jnp.inf)
    v1 = jnp.max(cand, axis=-1, keepdims=True)
    i1 = jnp.min(jnp.where(cand == v1, lane, n), axis=-1, keepdims=True)
    cand2 = jnp.where(lane == i1, -jnp.inf, cand)
    v2 = jnp.max(cand2, axis=-1, keepdims=True)
    i2 = jnp.min(jnp.where(cand2 == v2, lane, n), axis=-1, keepdims=True)
    tot = v1 + v2
    chosen = jnp.where(jnp.logical_or(lane == i1, lane == i2), 1.0, 0.0)
    r = lax.broadcasted_iota(jnp.int32, (tm, tm), 0)
    c = lax.broadcasted_iota(jnp.int32, (tm, tm), 1)
    earlier = jnp.where(c < r, 1.0, 0.0).astype(BF16)
    before = jnp.dot(earlier, chosen.astype(BF16), preferred_element_type=F32)
    rank1 = jnp.sum(jnp.where(lane == i1, before, 0.0), axis=-1, keepdims=True)
    rank2 = jnp.sum(jnp.where(lane == i2, before, 0.0), axis=-1, keepdims=True)
    route = jnp.where(lane == 0, v1 / tot,
                      jnp.where(lane == 1, v2 / tot,
                                jnp.where(lane == 2, i1.astype(F32),
                                          jnp.where(lane == 3, i2.astype(F32),
                                                    jnp.where(lane == 4, rank1,
                                                              jnp.where(lane == 5, rank2, 0.0))))))
    return route, jnp.sum(chosen, axis=0, keepdims=True)


def _dispatch_plan(route, counts, tm):
    s = route.shape[0]
    nt = s // tm
    n_tiles = (MOE_TOP_K * s) // tm + N_EXPERTS
    ids = jnp.arange(N_EXPERTS, dtype=jnp.int32)
    expert = route[:, 2:2 + MOE_TOP_K].astype(jnp.int32).reshape(nt, tm, MOE_TOP_K)
    rank = route[:, 4:4 + MOE_TOP_K].astype(jnp.int32).reshape(nt, tm, MOE_TOP_K)
    cnt = counts[:, 0, :N_EXPERTS].astype(jnp.int32)
    tile_base = jnp.cumsum(cnt, axis=0) - cnt
    padded = -(-jnp.sum(cnt, axis=0) // tm) * tm
    seg_end = jnp.cumsum(padded)
    base = (seg_end - padded)[None, :] + tile_base
    pick = expert[..., None] == ids
    pair_row = (jnp.sum(jnp.where(pick, base[:, None, None, :], 0), axis=-1) + rank).reshape(s, MOE_TOP_K)
    tile_start = jnp.arange(n_tiles, dtype=jnp.int32) * tm
    tile_expert = jnp.minimum(jnp.sum((tile_start[:, None] >= seg_end[None, :]).astype(jnp.int32), axis=1),
                              N_EXPERTS - 1).astype(jnp.int32)
    tile_live = (tile_start < seg_end[-1]).astype(jnp.int32)
    row_token = (jnp.arange(n_tiles * tm, dtype=jnp.int32) % s).at[pair_row.reshape(-1)].set(
        jnp.arange(MOE_TOP_K * s, dtype=jnp.int32) // MOE_TOP_K)
    return pair_row, row_token, tile_expert, tile_live


def _row_gather_kernel(idx_ref, src_ref, o_ref, sem, *, chunk):
    def issue(g, _):
        base = pl.multiple_of(g * SUBLANES, SUBLANES)
        for u in range(SUBLANES):
            tok = idx_ref[0, base + u]
            pltpu.make_async_copy(
                src_ref.at[lax.shift_right_logical(tok, 3), pl.ds(jnp.bitwise_and(tok, SUBLANES - 1), 1)],
                o_ref.at[g, pl.ds(u, 1)], sem).start()
        return 0

    lax.fori_loop(0, chunk // SUBLANES, issue, 0)
    pltpu.make_async_copy(src_ref.at[pl.ds(0, chunk // SUBLANES)], o_ref, sem).wait()


def _row_gather(src, idx, chunk):
    n = idx.shape[0]
    rows, d = src.shape
    out = pl.pallas_call(
        functools.partial(_row_gather_kernel, chunk=chunk),
        out_shape=jax.ShapeDtypeStruct((n // SUBLANES, SUBLANES, d), src.dtype),
        grid=(n // chunk,),
        in_specs=[pl.BlockSpec((None, 1, chunk), lambda c: (c, 0, 0), memory_space=pltpu.SMEM),
                  pl.BlockSpec(memory_space=pl.ANY)],
        out_specs=pl.BlockSpec((chunk // SUBLANES, SUBLANES, d), lambda c: (c, 0, 0)),
        scratch_shapes=[pltpu.SemaphoreType.DMA(())],
        compiler_params=_cparams(1, VMEM_LIMIT),
        name="row_gather",
    )(idx.reshape(n // chunk, 1, chunk), src.reshape(rows // SUBLANES, SUBLANES, d))
    return out.reshape(n, d)


def _moe_group_kernel(te_ref, live_ref, x_ref, wg_ref, wu_ref, wd_ref, o_ref, xb_ref):
    g = pl.program_id(0)
    f = pl.program_id(1)
    live = live_ref[g] > 0

    @pl.when(jnp.logical_and(live, f == 0))
    def _():
        xb_ref[...] = x_ref[...].astype(BF16)

    @pl.when(jnp.logical_and(jnp.logical_not(live), f == 0))
    def _():
        o_ref[...] = jnp.zeros(o_ref.shape, F32)

    @pl.when(live)
    def _():
        xb = xb_ref[...]
        a = jnp.dot(xb, wg_ref[...], preferred_element_type=F32)
        u = jnp.dot(xb, wu_ref[...], preferred_element_type=F32)
        hid = ((a * jax.nn.sigmoid(a)) * u).astype(BF16)
        y = jnp.dot(hid, wd_ref[...], preferred_element_type=F32)

        @pl.when(f == 0)
        def _():
            o_ref[...] = y

        @pl.when(f > 0)
        def _():
            o_ref[...] += y


def _moe_group_ffn(xs, tile_expert, tile_live, wg, wu, wd, tm, tf):
    rows, d = xs.shape
    fdim = wg.shape[2]
    grid_spec = pltpu.PrefetchScalarGridSpec(
        num_scalar_prefetch=2,
        grid=(rows // tm, fdim // tf),
        in_specs=[pl.BlockSpec((tm, d), lambda g, f, te, lv: (g, 0)),
                  pl.BlockSpec((None, d, tf), lambda g, f, te, lv: (te[g], 0, f)),
                  pl.BlockSpec((None, d, tf), lambda g, f, te, lv: (te[g], 0, f)),
                  pl.BlockSpec((None, tf, d), lambda g, f, te, lv: (te[g], f, 0))],
        out_specs=pl.BlockSpec((tm, d), lambda g, f, te, lv: (g, 0)),
        scratch_shapes=[pltpu.VMEM((tm, d), BF16)])
    return pl.pallas_call(
        _moe_group_kernel,
        out_shape=jax.ShapeDtypeStruct((rows, d), F32),
        grid_spec=grid_spec,
        compiler_params=_cparams(2, VMEM_LIMIT),
        name="moe_group_ffn",
    )(tile_expert, tile_live, xs, wg, wu, wd)


def _moe_combine_ln_kernel(x_ref, r_ref, y0_ref, y1_ref, g_ref, b_ref, o_ref):
    r = r_ref[...]
    ff = r[:, 0:1] * y0_ref[...] + r[:, 1:2] * y1_ref[...]
    o_ref[...] = _layer_norm(ALPHA * x_ref[...] + ff, g_ref[...], b_ref[...])


def _moe_combine_ln(x, route, y, g, b, tm):
    s, d = x.shape
    nt = s // tm
    return pl.pallas_call(
        _moe_combine_ln_kernel,
        out_shape=jax.ShapeDtypeStruct((s, d), F32),
        grid=(nt,),
        in_specs=[pl.BlockSpec((tm, d), lambda i: (i, 0)),
                  pl.BlockSpec((tm, LANES), lambda i: (i, 0)),
                  pl.BlockSpec((tm, d), lambda i: (i, 0)),
                  pl.BlockSpec((tm, d), lambda i: (i + nt, 0)),
                  pl.BlockSpec((1, d), lambda i: (0, 0)),
                  pl.BlockSpec((1, d), lambda i: (0, 0))],
        out_specs=pl.BlockSpec((tm, d), lambda i: (i, 0)),
        compiler_params=_cparams(1, VMEM_LIMIT),
        name="moe_combine_ln",
    )(x, route, y, y, g.reshape(1, d), b.reshape(1, d))


def _sc_row_gather(src, idx):
    n = idx.shape[0]
    d = src.shape[1]
    workers = SC_CORES * SC_SUBCORES
    per_worker = n // workers
    assert n % (workers * SC_GATHER_ROWS) == 0
    mesh = plsc.VectorSubcoreMesh(core_axis_name="c", subcore_axis_name="s")

    @functools.partial(
        pl.kernel, mesh=mesh, out_type=jax.ShapeDtypeStruct((n, d), src.dtype),
        scratch_types=[pltpu.VMEM((SC_GATHER_ROWS,), jnp.int32),
                       pltpu.VMEM((SC_GATHER_ROWS, d), src.dtype),
                       pltpu.SemaphoreType.DMA],
        name="sc_row_gather")
    def gather(src_hbm, idx_hbm, out_hbm, idx_v, rows_v, sem):
        base = (lax.axis_index("s") * SC_CORES + lax.axis_index("c")) * per_worker

        @pl.loop(0, per_worker, step=SC_GATHER_ROWS)
        def _(off):
            pltpu.sync_copy(idx_hbm.at[pl.ds(base + off, SC_GATHER_ROWS)], idx_v)
            pltpu.async_copy(src_hbm.at[idx_v], rows_v, sem).wait()
            pltpu.sync_copy(rows_v, out_hbm.at[pl.ds(base + off, SC_GATHER_ROWS)])

    return gather(src, idx)


def _moe_ln(x, route, counts, wg, wu, wd, g, b, tm):
    pair_row, row_token, tile_expert, tile_live = _dispatch_plan(route, counts, tm)
    xs = _sc_row_gather(x, row_token)
    ys = _moe_group_ffn(xs, tile_expert, tile_live, wg, wu, wd, tm, D_EXPERT // 2)
    slot_major = pair_row.T.reshape(-1)
    y = _sc_row_gather(ys, slot_major)
    return _moe_combine_ln(x, route, y, g, b, tm)


def kernel(x, w_in_even, w_out_even, forget_bias, w_in_odd, w_out_odd, lambda_q, lambda_k,
           subln_gain, rel_bias, w_router, b_router, w_gate, w_up, w_down, ln_gain, ln_bias):
    bsz, s, d = x.shape
    assert bsz == 1 and d == D_MODEL and s % MOBA_BLOCK == 0
    t = ATT_TILE
    nh = N_HEADS_GROUP
    gw = GROUP_WIDTH
    tm_proj = min(1024, s)
    tm_row = min(512, s)
    tk = min(ATT_KEYS, s)
    n_near = _near_tile_count(t)
    assert s % tk == 0 and tk // t + 1 >= n_near
    nblk = s // MOBA_BLOCK
    nbp = -(-nblk // LANES) * LANES

    x2 = x.reshape(s, d)
    bias_tiles = _bias_tiles(rel_bias, t, n_near)
    tab_spec = pl.BlockSpec(memory_space=pltpu.SMEM)
    bias_spec = pl.BlockSpec((None, n_near + 2, t, t), lambda hd, i: (hd, 0, 0, 0))
    block_onehot = (jnp.arange(s, dtype=jnp.int32)[:, None] // MOBA_BLOCK
                    == jnp.arange(nbp, dtype=jnp.int32)[None, :]).astype(BF16)

    for layer in range(DEPTH):
        li = layer // 2
        if layer % 2 == 0:
            w = w_in_even[li]
            fcol = 3 * gw
            w_main = jnp.concatenate([w[:, :fcol], w[:, fcol + nh:]], axis=1).astype(BF16)
            w_f = jnp.zeros((d, LANES), BF16).at[:, :nh].set(w[:, fcol:fcol + nh].astype(BF16))
            h = _in_proj(x2, w_main, BF16, tm_proj, 512)
            f = _in_proj(x2, w_f, F32, tm_proj, LANES)
            c = _decay_cumsum(f[:, :nh].T, forget_bias[li])
            oa = _attention_call(
                functools.partial(_fox_kernel, t=FOX_TQ, tk=min(FOX_TK, s), scale=HEAD_DIM ** -0.5), h, gw, 0,
                ([c[:, min(FOX_TK, s) - 1::min(FOX_TK, s)], c[:, ::FOX_TQ]],
                 [c.reshape(nh, 1, s), c.reshape(nh, s, 1)]),
                ([tab_spec, tab_spec],
                 [pl.BlockSpec((None, 1, s), lambda hd, i: (hd, 0, 0)),
                  pl.BlockSpec((None, FOX_TQ, 1), lambda hd, i: (hd, i, 0))]),
                [pltpu.VMEM((FOX_TQ, min(FOX_TK, s)), F32)] * 2 + [pltpu.VMEM((8, LANES), F32)],
                "fox_attention", FOX_TQ)
            ob = _attention_call(
                functools.partial(_moba_kernel, tb=t, nb=MOBA_Q_BLOCKS, tk=tk, nblk=nblk, n_near=n_near,
                                  scale=HEAD_DIM ** -0.5),
                h, gw, 3 * nh,
                ([rel_bias], [bias_tiles, block_onehot]),
                ([tab_spec], [bias_spec, pl.BlockSpec((s, nbp), lambda hd, i: (0, 0))]),
                [pltpu.VMEM((nbp, HEAD_DIM), F32)] + [pltpu.VMEM((MOBA_Q_BLOCKS * t, tk), F32)] * 2
                + [pltpu.VMEM((8, LANES), F32)], "moba_attention", MOBA_Q_BLOCKS * t)
            w_out = w_out_even[li]
        else:
            lambda_init = 0.8 - 0.6 * math.exp(-0.3 * layer)
            h = _in_proj(x2, w_in_odd[li].astype(BF16), BF16, tm_proj, 512)
            small = lambda shape: pl.BlockSpec(shape, lambda hd, i: (0, 0))
            oa = _attention_call(
                functools.partial(_diff_kernel, tb=t, nb=DIFF_Q_BLOCKS, tk=tk, n_near=n_near,
                                  scale=DIFF_QK_DIM ** -0.5, lambda_init=lambda_init),
                h, gw, 0,
                ([rel_bias], [bias_tiles, lambda_q[li], lambda_k[li], subln_gain[li].reshape(1, HEAD_DIM)]),
                ([tab_spec], [bias_spec, small((2, DIFF_QK_DIM)), small((2, DIFF_QK_DIM)),
                              small((1, HEAD_DIM))]),
                [pltpu.VMEM((2 * DIFF_Q_BLOCKS * t, tk), F32)] * 2 + [pltpu.VMEM((8, LANES), F32)],
                "diff_attention", DIFF_Q_BLOCKS * t)
            ob = _attention_call(
                functools.partial(_sb_kernel, t=t, nq=SB_Q_TILES, scale=HEAD_DIM ** -0.5), h, gw, 3 * nh,
                ([], []), ([], []), [], "stickbreak_attention", SB_Q_TILES * t)
            w_out = w_out_odd[li]
        x2 = _out_proj_ln(oa, ob, w_out.astype(BF16), x2, ln_gain[layer, 0], ln_bias[layer, 0], tm_row)
        route, counts = _router(x2, w_router, b_router, tm_row)
        x2 = _moe_ln(x2, route, counts, w_gate[layer].astype(BF16), w_up[layer].astype(BF16),
                     w_down[layer].astype(BF16), ln_gain[layer, 1], ln_bias[layer, 1], tm_row)
    return x2.reshape(bsz, s, d)
```

```python
import functools
import math

import numpy as np
import jax
import jax.numpy as jnp
from jax import lax
from jax.experimental import pallas as pl
from jax.experimental.pallas import tpu as pltpu
from jax.experimental.pallas import tpu_sc as plsc

F32 = jnp.float32
BF16 = jnp.bfloat16

D_MODEL = 2048
DEPTH = 2
HEAD_DIM = 128
N_HEADS_GROUP = 8
GROUP_WIDTH = N_HEADS_GROUP * HEAD_DIM
DIFF_QK_DIM = HEAD_DIM // 2
MOBA_BLOCK = 256
MOBA_TOPK = 3
N_BUCKETS = 32
MAX_EXACT = N_BUCKETS // 2
MAX_DISTANCE = 1024
N_EXPERTS = 16
N_GROUPS = 4
EXPERTS_PER_GROUP = N_EXPERTS // N_GROUPS
MOE_TOP_K = 2
SC_CORES = 2
SC_SUBCORES = 16
SC_GATHER_ROWS = 32
SUBLANES = 8
D_EXPERT = D_MODEL // 2
ALPHA = (2.0 * DEPTH) ** 0.25
LN_EPS = 1e-5
NEG_INF = -1e30
LOG2E = math.log2(math.e)
LANES = 128
ATT_TILE = 256
ATT_KEYS = 1024
FOX_TQ = 1024
FOX_TK = 1024
MOBA_Q_BLOCKS = 4
SB_Q_TILES = 2
DIFF_Q_BLOCKS = 4
SHIFT_SLACK = 96.0
FLUSH_EXPONENT = -130.0
SB_CUTOFF = -104.0
VMEM_LIMIT = 56 * 1024 * 1024


def _cparams(n_axes, vmem=None):
    return pltpu.CompilerParams(dimension_semantics=("arbitrary",) * n_axes,
                                vmem_limit_bytes=vmem)


def _nt_dot(a, b):
    return lax.dot_general(a, b, (((1,), (1,)), ((), ())), preferred_element_type=F32)


def _split_bf16(x):
    hi = x.astype(BF16)
    lo = (x - hi.astype(F32)).astype(BF16)
    return hi, lo


def _layer_norm(z, g, b):
    mu = jnp.mean(z, axis=-1, keepdims=True)
    zc = z - mu
    var = jnp.mean(zc * zc, axis=-1, keepdims=True)
    return zc * lax.rsqrt(var + LN_EPS) * g + b


def _inproj_kernel(x_ref, w_ref, o_ref, xb_ref):
    @pl.when(pl.program_id(1) == 0)
    def _():
        xb_ref[...] = x_ref[...].astype(BF16)

    o_ref[...] = jnp.dot(xb_ref[...], w_ref[...], preferred_element_type=F32).astype(o_ref.dtype)


def _in_proj(x, w, out_dtype, tm, tn):
    s, d = x.shape
    n = w.shape[1]
    return pl.pallas_call(
        _inproj_kernel,
        out_shape=jax.ShapeDtypeStruct((s, n), out_dtype),
        grid=(s // tm, n // tn),
        in_specs=[pl.BlockSpec((tm, d), lambda i, j: (i, 0)),
                  pl.BlockSpec((d, tn), lambda i, j: (0, j))],
        out_specs=pl.BlockSpec((tm, tn), lambda i, j: (i, j)),
        scratch_shapes=[pltpu.VMEM((tm, d), BF16)],
        compiler_params=_cparams(2, VMEM_LIMIT),
        name="in_proj",
    )(x, w)


def _outproj_ln_kernel(oa_ref, ob_ref, wa_ref, wb_ref, x_ref, g_ref, b_ref, o_ref):
    y = jnp.dot(oa_ref[...], wa_ref[...], preferred_element_type=F32)
    y = y + jnp.dot(ob_ref[...], wb_ref[...], preferred_element_type=F32)
    o_ref[...] = _layer_norm(ALPHA * x_ref[...] + y, g_ref[...], b_ref[...])


def _out_proj_ln(oa, ob, w, x, g, b, tm):
    s, d = x.shape
    gw = oa.shape[1]
    return pl.pallas_call(
        _outproj_ln_kernel,
        out_shape=jax.ShapeDtypeStruct((s, d), F32),
        grid=(s // tm,),
        in_specs=[pl.BlockSpec((tm, gw), lambda i: (i, 0)),
                  pl.BlockSpec((tm, gw), lambda i: (i, 0)),
                  pl.BlockSpec((gw, d), lambda i: (0, 0)),
                  pl.BlockSpec((gw, d), lambda i: (1, 0)),
                  pl.BlockSpec((tm, d), lambda i: (i, 0)),
                  pl.BlockSpec((1, d), lambda i: (0, 0)),
                  pl.BlockSpec((1, d), lambda i: (0, 0))],
        out_specs=pl.BlockSpec((tm, d), lambda i: (i, 0)),
        compiler_params=_cparams(1, VMEM_LIMIT),
        name="out_proj_ln",
    )(oa, ob, w, w, x, g.reshape(1, d), b.reshape(1, d))


def _router_kernel(x_ref, w_ref, b_ref, route_ref, count_ref):
    route, counts = _route(x_ref[...], w_ref[...], b_ref[...])
    route_ref[...] = route
    count_ref[...] = jnp.broadcast_to(counts, count_ref.shape)


def _router(x, w_router, b_router, tm):
    s, d = x.shape
    wr = jnp.zeros((d, LANES), F32).at[:, :N_EXPERTS].set(w_router)
    br = jnp.zeros((1, LANES), F32).at[0, :N_EXPERTS].set(b_router)
    return pl.pallas_call(
        _router_kernel,
        out_shape=(jax.ShapeDtypeStruct((s, LANES), F32),
                   jax.ShapeDtypeStruct((s // tm, SUBLANES, LANES), F32)),
        grid=(s // tm,),
        in_specs=[pl.BlockSpec((tm, d), lambda i: (i, 0)),
                  pl.BlockSpec((d, LANES), lambda i: (0, 0)),
                  pl.BlockSpec((1, LANES), lambda i: (0, 0))],
        out_specs=(pl.BlockSpec((tm, LANES), lambda i: (i, 0)),
                   pl.BlockSpec((None, SUBLANES, LANES), lambda i: (i, 0, 0))),
        compiler_params=_cparams(1, VMEM_LIMIT),
        name="router",
    )(x, wr, br)


def _decay_cumsum_kernel(f_ref, b_ref, c_ref):
    nh, s = f_ref.shape
    rows = lax.broadcasted_iota(jnp.int32, (LANES, LANES), 0)
    cols = lax.broadcasted_iota(jnp.int32, (LANES, LANES), 1)
    upper = jnp.where(rows <= cols, 1.0, 0.0).astype(F32)
    bias = b_ref[...]

    def body(n, carry):
        off = pl.multiple_of(n * LANES, LANES)
        z = f_ref[:, pl.ds(off, LANES)] + bias
        logf = jnp.minimum(z, 0.0) - jnp.log1p(jnp.exp(-jnp.abs(z)))
        c = jnp.dot(logf, upper, preferred_element_type=F32,
                    precision=lax.Precision.HIGHEST) + carry
        c_ref[:, pl.ds(off, LANES)] = c * LOG2E
        return carry + jnp.sum(logf, axis=-1, keepdims=True)

    lax.fori_loop(0, s // LANES, body, jnp.zeros((nh, 1), F32))


def _decay_cumsum(f_t, bias):
    nh, s = f_t.shape
    return pl.pallas_call(
        _decay_cumsum_kernel,
        out_shape=jax.ShapeDtypeStruct((nh, s), F32),
        name="decay_cumsum",
    )(f_t, bias.reshape(nh, 1))


def _t5_bucket_np(dist):
    n = np.maximum(dist, 0)
    nf = np.maximum(n, 1).astype(np.float32)
    ratio = np.log(nf / np.float32(MAX_EXACT)) / np.float32(math.log(MAX_DISTANCE / MAX_EXACT))
    large = MAX_EXACT + (ratio.astype(np.float32) * np.float32(N_BUCKETS - MAX_EXACT)).astype(np.int32)
    large = np.minimum(large, N_BUCKETS - 1)
    return np.where(n < MAX_EXACT, n, large).astype(np.int32)


def _near_tile_count(t):
    d = np.arange(0, 4 * MAX_DISTANCE, dtype=np.int64)
    not_last = np.nonzero(_t5_bucket_np(d) != N_BUCKETS - 1)[0]
    d_sat = int(not_last.max()) + 1
    n = 1
    while (n - 1) * t + 1 < d_sat:
        n += 1
    return n


def _bucket_tiles_np(t, n_near):
    r = np.arange(t)[:, None]
    c = np.arange(t)[None, :]
    tiles = []
    for delta in range(n_near):
        dist = delta * t + r - c
        tiles.append(np.where(dist >= 0, _t5_bucket_np(dist), -1))
    tiles.append(np.full((t, t), N_BUCKETS - 1))
    tiles.append(np.full((t, t), -1))
    return np.stack(tiles).astype(np.int32)


def _bias_tiles_kernel(tab_ref, idx_ref, o_ref):
    h = pl.program_id(0)
    idx = idx_ref[...]
    out = jnp.full(idx.shape, NEG_INF, F32)
    for b in range(N_BUCKETS):
        out = jnp.where(idx == b, tab_ref[b, h] * LOG2E, out)
    o_ref[...] = out


def _bias_tiles(rel_bias, t, n_near):
    nh = rel_bias.shape[1]
    idx = jnp.asarray(_bucket_tiles_np(t, n_near))
    n_tiles = idx.shape[0]
    return pl.pallas_call(
        _bias_tiles_kernel,
        out_shape=jax.ShapeDtypeStruct((nh, n_tiles, t, t), F32),
        grid=(nh, n_tiles),
        in_specs=[pl.BlockSpec(memory_space=pltpu.SMEM),
                  pl.BlockSpec((None, t, t), lambda h, n: (n, 0, 0))],
        out_specs=pl.BlockSpec((None, None, t, t), lambda h, n: (h, n, 0, 0)),
        compiler_params=_cparams(2),
        name="bias_tiles",
    )(rel_bias, idx)


def _online_softmax_step(s, v, carry):
    m, l, acc = carry
    m_new = jnp.maximum(m, jnp.max(s, axis=-1, keepdims=True))
    alpha = jnp.exp2(m - m_new)
    p = jnp.exp2(s - m_new)
    l = alpha * l + jnp.sum(p, axis=-1, keepdims=True)
    acc = alpha * acc + jnp.dot(p.astype(BF16), v, preferred_element_type=F32)
    return m_new, l, acc


def _prescale(q, c):
    return (q.astype(F32) * c).astype(BF16)


def _near_bias(bias_ref, blocks, j, tb, tk, n_near):
    nsub = tk // tb
    rows = []
    for i in blocks:
        tiles = []
        for b in range(nsub):
            delta = i - (j * nsub + b)
            tiles.append(bias_ref[jnp.where(delta < 0, n_near + 1, jnp.minimum(delta, n_near))])
        rows.append(jnp.concatenate(tiles, axis=1))
    return rows[0] if len(rows) == 1 else jnp.concatenate(rows, axis=0)


def _biased_sweep(i, t, tk, scores, process, init):
    jd = (i * t) // tk
    carry = process(scores(jd), jd, init, True)
    lo = jnp.maximum(jd - 1, 0)
    carry = lax.fori_loop(lo, jd, lambda j, c: process(scores(j), j, c, True), carry)
    return lax.fori_loop(0, lo, lambda j, c: process(scores(j), j, c, False), carry)


def _softmax_init(rows):
    return (jnp.full((rows, 1), NEG_INF, F32), jnp.zeros((rows, 1), F32),
            jnp.zeros((rows, HEAD_DIM), F32))


def _fox_kernel(cend_ref, cstart_ref, q_ref, k_ref, v_ref, c_ref, ccol_ref, o_ref, knorm_ref,
                *, t, tk, scale):
    h = pl.program_id(0)
    i = pl.program_id(1)

    @pl.when(i == 0)
    def _():
        knorm_ref[...] = jnp.broadcast_to(_max_key_norm(k_ref, tk, 0, HEAD_DIM), knorm_ref.shape)

    q = _prescale(q_ref[...], scale * LOG2E)
    rows = lax.broadcasted_iota(jnp.int32, (t, tk), 0)
    cols = lax.broadcasted_iota(jnp.int32, (t, tk), 1)

    def scores(j):
        off = pl.multiple_of(j * tk, tk)
        return _nt_dot(q, k_ref[pl.ds(off, tk), :]) - c_ref[:, pl.ds(off, tk)]

    def values(j):
        return v_ref[pl.ds(pl.multiple_of(j * tk, tk), tk), :]

    def process(s, j, carry):
        return _online_softmax_step(s, values(j), carry)

    jd = (i * t) // tk
    s_diag = jnp.where(cols <= rows + (i * t - jd * tk), scores(jd), NEG_INF)

    qf = q.astype(F32)
    qnorm = jnp.sqrt(jnp.sum(qf * qf, axis=-1, keepdims=True))
    bound = qnorm * knorm_ref[0:1, 0:1] * (1.0 + 2.0 ** -9) + 2.0 ** -6 - ccol_ref[...]
    slack = jnp.max(bound - jnp.max(s_diag, axis=-1, keepdims=True))

    def shifted_sweep(_):
        def step(s, j, carry):
            l, acc = carry
            p = jnp.exp2(s - bound)
            return (l + jnp.sum(p, axis=-1, keepdims=True),
                    acc + jnp.dot(p.astype(BF16), values(j), preferred_element_type=F32))

        c_first_row = cstart_ref[h, i]
        first_live = jnp.int32(0)
        for j in range(cend_ref.shape[1]):
            dead = jnp.logical_and(j < jd, c_first_row - cend_ref[h, j] < FLUSH_EXPONENT)
            first_live = first_live + dead.astype(jnp.int32)
        carry = step(s_diag, jd, _softmax_init(t)[1:])
        return _paired_loop(jd - first_live, lambda n, c: step(scores(first_live + n), first_live + n, c),
                            carry)

    def running_max_sweep(_):
        carry = _online_softmax_step(s_diag, values(jd), _softmax_init(t))
        return lax.fori_loop(0, jd, lambda j, c: process(scores(j), j, c), carry)[1:]

    l, acc = lax.cond(slack <= SHIFT_SLACK, shifted_sweep, running_max_sweep, 0)
    o_ref[...] = (acc / l).astype(o_ref.dtype)


def _moba_kernel(tab_ref, q_ref, k_ref, v_ref, bias_ref, et_ref, o_ref, kmean_ref, knorm_ref,
                 *, tb, nb, tk, nblk, n_near, scale):
    h = pl.program_id(0)
    g = pl.program_id(1)
    t = nb * tb
    nbp = kmean_ref.shape[0]

    @pl.when(g == 0)
    def _():
        kmean_ref[...] = jnp.zeros(kmean_ref.shape, F32)

        def mean_body(n, _):
            off = pl.multiple_of(n * tb, tb)
            kb = k_ref[pl.ds(off, tb), :].astype(F32)
            kmean_ref[pl.ds(n, 1), :] = jnp.sum(kb, axis=0, keepdims=True) * (1.0 / tb)
            return 0

        lax.fori_loop(0, nblk, mean_body, 0)
        knorm_ref[...] = jnp.broadcast_to(_max_key_norm(k_ref, tk, 0, HEAD_DIM), knorm_ref.shape)

    q = q_ref[...]
    km_hi, km_lo = _split_bf16(kmean_ref[...])
    gate = _nt_dot(q, km_hi) + _nt_dot(q, km_lo)
    blk = lax.broadcasted_iota(jnp.int32, (t, nbp), 1)
    own = g * nb + lax.broadcasted_iota(jnp.int32, (t, 1), 0) // tb
    eligible = blk < own
    sel = jnp.zeros((t, nbp), F32)
    for r in range(MOBA_TOPK):
        cand = jnp.where(eligible, jnp.where(sel > 0.0, -jnp.inf, gate), -jnp.inf)
        vmax = jnp.max(cand, axis=-1, keepdims=True)
        first = jnp.min(jnp.where(cand == vmax, blk, nbp), axis=-1, keepdims=True)
        take = jnp.where(r < own, 1.0, 0.0)
        sel = jnp.maximum(sel, jnp.where(blk == first, take, 0.0))

    sel = jnp.maximum(sel, jnp.where(blk == own, 1.0, 0.0))
    penalty = jnp.where(sel > 0.0, 0.0, NEG_INF).astype(BF16)
    q_aug = jnp.concatenate([_prescale(q, scale * LOG2E), penalty], axis=1)
    b_far = tab_ref[N_BUCKETS - 1, h] * LOG2E

    def scores(j):
        off = pl.multiple_of(j * tk, tk)
        k_aug = jnp.concatenate([k_ref[pl.ds(off, tk), :], et_ref[pl.ds(off, tk), :]], axis=1)
        return _nt_dot(q_aug, k_aug)

    def values(j):
        return v_ref[pl.ds(pl.multiple_of(j * tk, tk), tk), :]

    def near_bias(j):
        return _near_bias(bias_ref, [g * nb + b for b in range(nb)], j, tb, tk, n_near)

    def process(s, j, carry, near):
        return _online_softmax_step(s + (near_bias(j) if near else b_far), values(j), carry)

    qf = q_aug[:, :HEAD_DIM].astype(F32)
    qnorm = jnp.sqrt(jnp.sum(qf * qf, axis=-1, keepdims=True))
    bound = _logit_bound(qnorm, knorm_ref[0:1, 0:1], tab_ref, h)
    l, acc = _shifted_or_running_sweep(g, t, tk, scores, near_bias, values, process, bound, b_far, t)
    o_ref[...] = (acc / l).astype(o_ref.dtype)


def _paired_loop(n, body, carry):
    carry = lax.fori_loop(0, n // 2, lambda p, c: body(2 * p + 1, body(2 * p, c)), carry)
    return lax.fori_loop(0, n % 2, lambda _, c: body(n - 1, c), carry)


def _logit_bound(qnorm, knorm, tab_ref, h):
    b_max = tab_ref[0, h]
    for b in range(1, N_BUCKETS):
        b_max = jnp.maximum(b_max, tab_ref[b, h])
    return qnorm * knorm * (1.0 + 2.0 ** -9) + (b_max * LOG2E + 2.0 ** -6)


def _shifted_or_running_sweep(i, t, tk, scores, near_bias, values, process, bound, b_far, rows):
    jd = (i * t) // tk
    s_diag = scores(jd) + near_bias(jd)
    slack = jnp.max(bound - jnp.max(s_diag, axis=-1, keepdims=True))

    def shifted_sweep(_):
        def step(s, j, carry, shift):
            l, acc = carry
            p = jnp.exp2(s - shift)
            return (l + jnp.sum(p, axis=-1, keepdims=True),
                    acc + jnp.dot(p.astype(BF16), values(j), preferred_element_type=F32))

        carry = step(s_diag, jd, _softmax_init(rows)[1:], bound)
        lo = jnp.maximum(jd - 1, 0)
        carry = lax.fori_loop(lo, jd, lambda j, c: step(scores(j) + near_bias(j), j, c, bound), carry)
        far_shift = bound - b_far
        return _paired_loop(lo, lambda j, c: step(scores(j), j, c, far_shift), carry)

    def running_max_sweep(_):
        return _biased_sweep(i, t, tk, scores, process, _softmax_init(rows))[1:]

    return lax.cond(slack <= SHIFT_SLACK, shifted_sweep, running_max_sweep, 0)


def _max_key_norm(k_ref, tk, lane_lo, lane_hi):
    klane = lax.broadcasted_iota(jnp.int32, (tk, HEAD_DIM), 1)
    keep = jnp.logical_and(klane >= lane_lo, klane < lane_hi)

    def body(n, c):
        kt = k_ref[pl.ds(pl.multiple_of(n * tk, tk), tk), :].astype(F32)
        sq = jnp.sum(jnp.where(keep, kt * kt, 0.0), axis=-1, keepdims=True)
        return jnp.maximum(c, jnp.max(sq, axis=0, keepdims=True))

    return jnp.sqrt(lax.fori_loop(0, k_ref.shape[0] // tk, body, jnp.zeros((1, 1), F32)))


def _diff_kernel(tab_ref, q_ref, k_ref, v_ref, bias_ref, lq_ref, lk_ref, g_ref, o_ref, knorm_ref,
                 *, tb, nb, tk, n_near, scale, lambda_init):
    h = pl.program_id(0)
    i = pl.program_id(1)
    t = nb * tb
    q = _prescale(q_ref[...], scale * LOG2E)
    lane = lax.broadcasted_iota(jnp.int32, (t, HEAD_DIM), 1)
    zero = jnp.zeros_like(q)
    q2 = jnp.concatenate([jnp.where(lane < DIFF_QK_DIM, q, zero),
                          jnp.where(lane >= DIFF_QK_DIM, q, zero)], axis=0)
    b_far = tab_ref[N_BUCKETS - 1, h] * LOG2E

    @pl.when(i == 0)
    def _():
        knorm_ref[0:1, :] = jnp.broadcast_to(_max_key_norm(k_ref, tk, 0, DIFF_QK_DIM), (1, LANES))
        knorm_ref[1:2, :] = jnp.broadcast_to(_max_key_norm(k_ref, tk, DIFF_QK_DIM, HEAD_DIM), (1, LANES))

    def scores(j):
        return _nt_dot(q2, k_ref[pl.ds(pl.multiple_of(j * tk, tk), tk), :])

    def values(j):
        return v_ref[pl.ds(pl.multiple_of(j * tk, tk), tk), :]

    def near_bias(j):
        b = _near_bias(bias_ref, [i * nb + b for b in range(nb)], j, tb, tk, n_near)
        return jnp.concatenate([b, b], axis=0)

    def process(s, j, carry, near):
        return _online_softmax_step(s + (near_bias(j) if near else b_far), values(j), carry)

    q2f = q2.astype(F32)
    qnorm = jnp.sqrt(jnp.sum(q2f * q2f, axis=-1, keepdims=True))
    knorm = jnp.concatenate([jnp.broadcast_to(knorm_ref[0:1, 0:1], (t, 1)),
                             jnp.broadcast_to(knorm_ref[1:2, 0:1], (t, 1))], axis=0)
    bound = _logit_bound(qnorm, knorm, tab_ref, h)
    l, acc = _shifted_or_running_sweep(i, t, tk, scores, near_bias, values, process, bound, b_far, 2 * t)
    o12 = acc / l
    lam_e = jnp.exp(jnp.sum(lq_ref[...] * lk_ref[...], axis=-1, keepdims=True))
    lam = lam_e[0:1, :] - lam_e[1:2, :] + lambda_init
    o = o12[:t, :] - lam * o12[t:, :]
    o = o * lax.rsqrt(jnp.mean(o * o, axis=-1, keepdims=True) + LN_EPS) * g_ref[...]
    o_ref[...] = (o * (1.0 - lambda_init)).astype(o_ref.dtype)


def _sb_kernel(q_ref, k_ref, v_ref, o_ref, *, t, nq, scale):
    g = pl.program_id(1)

    def later_matrix(n):
        r = lax.broadcasted_iota(jnp.int32, (n, n), 0)
        c = lax.broadcasted_iota(jnp.int32, (n, n), 1)
        return jnp.where(r > c, 1.0, 0.0).astype(BF16)

    def block(q, off, nk, rem, acc, shift):
        k = k_ref[pl.ds(off, nk), :]
        v = v_ref[pl.ds(off, nk), :]
        z = _nt_dot(q, k) * scale
        soft = jnp.log1p(jnp.exp(-jnp.abs(z)))
        log_sig = jnp.minimum(z, 0.0) - soft
        log_rem = jnp.minimum(-z, 0.0) - soft
        if shift is not None:
            rows = lax.broadcasted_iota(jnp.int32, (t, nk), 0)
            cols = lax.broadcasted_iota(jnp.int32, (t, nk), 1)
            mask = cols < rows + shift
            log_rem = jnp.where(mask, log_rem, 0.0)
        hi, lo = _split_bf16(log_rem)
        later = later_matrix(nk)
        after = (jnp.dot(hi, later, preferred_element_type=F32)
                 + jnp.dot(lo, later, preferred_element_type=F32))
        a = jnp.exp(log_sig + after + rem)
        if shift is not None:
            a = jnp.where(mask, a, 0.0)
        acc = acc + jnp.dot(a.astype(BF16), v, preferred_element_type=F32)
        rem = rem + jnp.sum(log_rem, axis=-1, keepdims=True)
        return rem, acc

    def live(rem):
        return (jnp.max(rem) > SB_CUTOFF).astype(jnp.int32)

    state = []
    for u in range(nq):
        i = g * nq + u
        q = q_ref[u * t:(u + 1) * t, :]
        first = jnp.maximum(i - 1, 0)
        rem, acc = block(q, pl.multiple_of(first * t, t), 2 * t, jnp.zeros((t, 1), F32),
                         jnp.zeros((t, HEAD_DIM), F32), (i - first) * t)
        state.append((i, q, rem, acc))

    for u, (i, q, rem, acc) in enumerate(state):
        def cond(c):
            j, go, _, _ = c
            return jnp.logical_and(j >= 0, go > 0)

        def body(c, q=q):
            j, _, rem, acc = c
            rem, acc = block(q, pl.multiple_of(j * t, t), t, rem, acc, None)
            return j - 1, live(rem), rem, acc

        _, _, _, acc = lax.while_loop(cond, body, (i - 2, live(rem), rem, acc))
        o_ref[u * t:(u + 1) * t, :] = acc.astype(o_ref.dtype)


def _attention_call(kernel, h, out_cols, col0, extra_in, extra_specs, scratch, name, t):
    s = h.shape[0]
    nh = N_HEADS_GROUP
    in_specs = list(extra_specs[0]) + [
        pl.BlockSpec((t, HEAD_DIM), lambda hd, i: (i, col0 + hd)),
        pl.BlockSpec((s, HEAD_DIM), lambda hd, i: (0, col0 + nh + hd)),
        pl.BlockSpec((s, HEAD_DIM), lambda hd, i: (0, col0 + 2 * nh + hd)),
    ] + list(extra_specs[1])
    args = list(extra_in[0]) + [h, h, h] + list(extra_in[1])
    return pl.pallas_call(
        kernel,
        out_shape=jax.ShapeDtypeStruct((s, out_cols), BF16),
        grid=(nh, s // t),
        in_specs=in_specs,
        out_specs=pl.BlockSpec((t, HEAD_DIM), lambda hd, i: (i, hd)),
        scratch_shapes=scratch,
        compiler_params=_cparams(2, VMEM_LIMIT),
        name=name,
    )(*args)


def _route(x, w, b):
    logits = jnp.dot(x, w, preferred_element_type=F32, precision=lax.Precision.HIGHEST) + b
    tm, n = logits.shape
    lane = lax.broadcasted_iota(jnp.int32, (tm, n), 1)
    real = lane < N_EXPERTS
    logits = jnp.where(real, logits, -jnp.inf)
    e = jnp.exp(logits - jnp.max(logits, axis=-1, keepdims=True))
    aff = e / jnp.sum(e, axis=-1, keepdims=True)
    group = lane // EXPERTS_PER_GROUP
    best = jnp.full((tm, 1), -jnp.inf, F32)
    g_sel = jnp.zeros((tm, 1), jnp.int32)
    for g in range(N_GROUPS):
        gmax = jnp.max(jnp.where(group == g, aff, -jnp.inf), axis=-1, keepdims=True)
        better = gmax > best
        g_sel = jnp.where(better, g, g_sel)
        best = jnp.where(better, gmax, best)
    in_group = jnp.logical_and(group == g_sel, real)
    cand = jnp.where(in_group, aff, -jnp.inf)
    v1 = jnp.max(cand, axis=-1, keepdims=True)
    i1 = jnp.min(jnp.where(cand == v1, lane, n), axis=-1, keepdims=True)
    cand2 = jnp.where(lane == i1, -jnp.inf, cand)
    v2 = jnp.max(cand2, axis=-1, keepdims=True)
    i2 = jnp.min(jnp.where(cand2 == v2, lane, n), axis=-1, keepdims=True)
    tot = v1 + v2
    chosen = jnp.where(jnp.logical_or(lane == i1, lane == i2), 1.0, 0.0)
    r = lax.broadcasted_iota(jnp.int32, (tm, tm), 0)
    c = lax.broadcasted_iota(jnp.int32, (tm, tm), 1)
    earlier = jnp.where(c < r, 1.0, 0.0).astype(BF16)
    before = jnp.dot(earlier, chosen.astype(BF16), preferred_element_type=F32)
    rank1 = jnp.sum(jnp.where(lane == i1, before, 0.0), axis=-1, keepdims=True)
    rank2 = jnp.sum(jnp.where(lane == i2, before, 0.0), axis=-1, keepdims=True)
    route = jnp.where(lane == 0, v1 / tot,
                      jnp.where(lane == 1, v2 / tot,
                                jnp.where(lane == 2, i1.astype(F32),
                                          jnp.where(lane == 3, i2.astype(F32),
                                                    jnp.where(lane == 4, rank1,
                                                              jnp.where(lane == 5, rank2, 0.0))))))
    return route, jnp.sum(chosen, axis=0, keepdims=True)


def _dispatch_plan(route, counts, tm):
    s = route.shape[0]
    nt = s // tm
    n_tiles = (MOE_TOP_K * s) // tm + N_EXPERTS
    ids = jnp.arange(N_EXPERTS, dtype=jnp.int32)
    expert = route[:, 2:2 + MOE_TOP_K].astype(jnp.int32).reshape(nt, tm, MOE_TOP_K)
    rank = route[:, 4:4 + MOE_TOP_K].astype(jnp.int32).reshape(nt, tm, MOE_TOP_K)
    cnt = counts[:, 0, :N_EXPERTS].astype(jnp.int32)
    tile_base = jnp.cumsum(cnt, axis=0) - cnt
    padded = -(-jnp.sum(cnt, axis=0) // tm) * tm
    seg_end = jnp.cumsum(padded)
    base = (seg_end - padded)[None, :] + tile_base
    pick = expert[..., None] == ids
    pair_row = (jnp.sum(jnp.where(pick, base[:, None, None, :], 0), axis=-1) + rank).reshape(s, MOE_TOP_K)
    tile_start = jnp.arange(n_tiles, dtype=jnp.int32) * tm
    tile_expert = jnp.minimum(jnp.sum((tile_start[:, None] >= seg_end[None, :]).astype(jnp.int32), axis=1),
                              N_EXPERTS - 1).astype(jnp.int32)
    tile_live = (tile_start < seg_end[-1]).astype(jnp.int32)
    row_token = (jnp.arange(n_tiles * tm, dtype=jnp.int32) % s).at[pair_row.reshape(-1)].set(
        jnp.arange(MOE_TOP_K * s, dtype=jnp.int32) // MOE_TOP_K)
    return pair_row, row_token, tile_expert, tile_live


def _moe_group_kernel(te_ref, live_ref, x_ref, wg_ref, wu_ref, wd_ref, o_ref, xb_ref):
    g = pl.program_id(0)
    f = pl.program_id(1)
    live = live_ref[g] > 0

    @pl.when(jnp.logical_and(live, f == 0))
    def _():
        xb_ref[...] = x_ref[...].astype(BF16)

    @pl.when(jnp.logical_and(jnp.logical_not(live), f == 0))
    def _():
        o_ref[...] = jnp.zeros(o_ref.shape, F32)

    @pl.when(live)
    def _():
        xb = xb_ref[...]
        a = jnp.dot(xb, wg_ref[...], preferred_element_type=F32)
        u = jnp.dot(xb, wu_ref[...], preferred_element_type=F32)
        hid = ((a * jax.nn.sigmoid(a)) * u).astype(BF16)
        y = jnp.dot(hid, wd_ref[...], preferred_element_type=F32)

        @pl.when(f == 0)
        def _():
            o_ref[...] = y

        @pl.when(f > 0)
        def _():
            o_ref[...] += y


def _moe_group_ffn(xs, tile_expert, tile_live, wg, wu, wd, tm, tf):
    rows, d = xs.shape
    fdim = wg.shape[2]
    grid_spec = pltpu.PrefetchScalarGridSpec(
        num_scalar_prefetch=2,
        grid=(rows // tm, fdim // tf),
        in_specs=[pl.BlockSpec((tm, d), lambda g, f, te, lv: (g, 0)),
                  pl.BlockSpec((None, d, tf), lambda g, f, te, lv: (te[g], 0, f)),
                  pl.BlockSpec((None, d, tf), lambda g, f, te, lv: (te[g], 0, f)),
                  pl.BlockSpec((None, tf, d), lambda g, f, te, lv: (te[g], f, 0))],
        out_specs=pl.BlockSpec((tm, d), lambda g, f, te, lv: (g, 0)),
        scratch_shapes=[pltpu.VMEM((tm, d), BF16)])
    return pl.pallas_call(
        _moe_group_kernel,
        out_shape=jax.ShapeDtypeStruct((rows, d), F32),
        grid_spec=grid_spec,
        compiler_params=_cparams(2, VMEM_LIMIT),
        name="moe_group_ffn",
    )(tile_expert, tile_live, xs, wg, wu, wd)


def _moe_combine_ln_kernel(x_ref, r_ref, y0_ref, y1_ref, g_ref, b_ref, o_ref):
    r = r_ref[...]
    ff = r[:, 0:1] * y0_ref[...] + r[:, 1:2] * y1_ref[...]
    o_ref[...] = _layer_norm(ALPHA * x_ref[...] + ff, g_ref[...], b_ref[...])


def _moe_combine_ln(x, route, y, g, b, tm):
    s, d = x.shape
    nt = s // tm
    return pl.pallas_call(
        _moe_combine_ln_kernel,
        out_shape=jax.ShapeDtypeStruct((s, d), F32),
        grid=(nt,),
        in_specs=[pl.BlockSpec((tm, d), lambda i: (i, 0)),
                  pl.BlockSpec((tm, LANES), lambda i: (i, 0)),
                  pl.BlockSpec((tm, d), lambda i: (i, 0)),
                  pl.BlockSpec((tm, d), lambda i: (i + nt, 0)),
                  pl.BlockSpec((1, d), lambda i: (0, 0)),
                  pl.BlockSpec((1, d), lambda i: (0, 0))],
        out_specs=pl.BlockSpec((tm, d), lambda i: (i, 0)),
        compiler_params=_cparams(1, VMEM_LIMIT),
        name="moe_combine_ln",
    )(x, route, y, y, g.reshape(1, d), b.reshape(1, d))


def _sc_row_gather(src, idx):
    n = idx.shape[0]
    d = src.shape[1]
    workers = SC_CORES * SC_SUBCORES
    per_worker = n // workers
    assert n % (workers * SC_GATHER_ROWS) == 0
    mesh = plsc.VectorSubcoreMesh(core_axis_name="c", subcore_axis_name="s")

    @functools.partial(
        pl.kernel, mesh=mesh, out_type=jax.ShapeDtypeStruct((n, d), src.dtype),
        scratch_types=[pltpu.VMEM((SC_GATHER_ROWS,), jnp.int32),
                       pltpu.VMEM((SC_GATHER_ROWS, d), src.dtype),
                       pltpu.SemaphoreType.DMA],
        name="sc_row_gather")
    def gather(src_hbm, idx_hbm, out_hbm, idx_v, rows_v, sem):
        base = (lax.axis_index("s") * SC_CORES + lax.axis_index("c")) * per_worker

        @pl.loop(0, per_worker, step=SC_GATHER_ROWS)
        def _(off):
            pltpu.sync_copy(idx_hbm.at[pl.ds(base + off, SC_GATHER_ROWS)], idx_v)
            pltpu.async_copy(src_hbm.at[idx_v], rows_v, sem).wait()
            pltpu.sync_copy(rows_v, out_hbm.at[pl.ds(base + off, SC_GATHER_ROWS)])

    return gather(src, idx)


def _moe_ln(x, route, counts, wg, wu, wd, g, b, tm):
    pair_row, row_token, tile_expert, tile_live = _dispatch_plan(route, counts, tm)
    xs = _sc_row_gather(x, row_token)
    ys = _moe_group_ffn(xs, tile_expert, tile_live, wg, wu, wd, tm, D_EXPERT // 2)
    slot_major = pair_row.T.reshape(-1)
    y = _sc_row_gather(ys, slot_major)
    return _moe_combine_ln(x, route, y, g, b, tm)


def kernel(x, w_in_even, w_out_even, forget_bias, w_in_odd, w_out_odd, lambda_q, lambda_k,
           subln_gain, rel_bias, w_router, b_router, w_gate, w_up, w_down, ln_gain, ln_bias):
    bsz, s, d = x.shape
    assert bsz == 1 and d == D_MODEL and s % MOBA_BLOCK == 0
    t = ATT_TILE
    nh = N_HEADS_GROUP
    gw = GROUP_WIDTH
    tm_proj = min(1024, s)
    tm_row = min(512, s)
    tk = min(ATT_KEYS, s)
    n_near = _near_tile_count(t)
    assert s % tk == 0 and tk // t + 1 >= n_near
    nblk = s // MOBA_BLOCK
    nbp = -(-nblk // LANES) * LANES

    x2 = x.reshape(s, d)
    bias_tiles = _bias_tiles(rel_bias, t, n_near)
    tab_spec = pl.BlockSpec(memory_space=pltpu.SMEM)
    bias_spec = pl.BlockSpec((None, n_near + 2, t, t), lambda hd, i: (hd, 0, 0, 0))
    block_onehot = (jnp.arange(s, dtype=jnp.int32)[:, None] // MOBA_BLOCK
                    == jnp.arange(nbp, dtype=jnp.int32)[None, :]).astype(BF16)

    for layer in range(DEPTH):
        li = layer // 2
        if layer % 2 == 0:
            w = w_in_even[li]
            fcol = 3 * gw
            w_main = jnp.concatenate([w[:, :fcol], w[:, fcol + nh:]], axis=1).astype(BF16)
            w_f = jnp.zeros((d, LANES), BF16).at[:, :nh].set(w[:, fcol:fcol + nh].astype(BF16))
            h = _in_proj(x2, w_main, BF16, tm_proj, 512)
            f = _in_proj(x2, w_f, F32, tm_proj, LANES)
            c = _decay_cumsum(f[:, :nh].T, forget_bias[li])
            oa = _attention_call(
                functools.partial(_fox_kernel, t=FOX_TQ, tk=min(FOX_TK, s), scale=HEAD_DIM ** -0.5), h, gw, 0,
                ([c[:, min(FOX_TK, s) - 1::min(FOX_TK, s)], c[:, ::FOX_TQ]],
                 [c.reshape(nh, 1, s), c.reshape(nh, s, 1)]),
                ([tab_spec, tab_spec],
                 [pl.BlockSpec((None, 1, s), lambda hd, i: (hd, 0, 0)),
                  pl.BlockSpec((None, FOX_TQ, 1), lambda hd, i: (hd, i, 0))]),
                [pltpu.VMEM((SUBLANES, LANES), F32)],
                "fox_attention", FOX_TQ)
            ob = _attention_call(
                functools.partial(_moba_kernel, tb=t, nb=MOBA_Q_BLOCKS, tk=tk, nblk=nblk, n_near=n_near,
                                  scale=HEAD_DIM ** -0.5),
                h, gw, 3 * nh,
                ([rel_bias], [bias_tiles, block_onehot]),
                ([tab_spec], [bias_spec, pl.BlockSpec((s, nbp), lambda hd, i: (0, 0))]),
                [pltpu.VMEM((nbp, HEAD_DIM), F32), pltpu.VMEM((SUBLANES, LANES), F32)],
                "moba_attention", MOBA_Q_BLOCKS * t)
            w_out = w_out_even[li]
        else:
            lambda_init = 0.8 - 0.6 * math.exp(-0.3 * layer)
            h = _in_proj(x2, w_in_odd[li].astype(BF16), BF16, tm_proj, 512)
            small = lambda shape: pl.BlockSpec(shape, lambda hd, i: (0, 0))
            oa = _attention_call(
                functools.partial(_diff_kernel, tb=t, nb=DIFF_Q_BLOCKS, tk=tk, n_near=n_near,
                                  scale=DIFF_QK_DIM ** -0.5, lambda_init=lambda_init),
                h, gw, 0,
                ([rel_bias], [bias_tiles, lambda_q[li], lambda_k[li], subln_gain[li].reshape(1, HEAD_DIM)]),
                ([tab_spec], [bias_spec, small((2, DIFF_QK_DIM)), small((2, DIFF_QK_DIM)),
                              small((1, HEAD_DIM))]),
                [pltpu.VMEM((SUBLANES, LANES), F32)],
                "diff_attention", DIFF_Q_BLOCKS * t)
            ob = _attention_call(
                functools.partial(_sb_kernel, t=t, nq=SB_Q_TILES, scale=HEAD_DIM ** -0.5), h, gw, 3 * nh,
                ([], []), ([], []), [], "stickbreak_attention", SB_Q_TILES * t)
            w_out = w_out_odd[li]
        x2 = _out_proj_ln(oa, ob, w_out.astype(BF16), x2, ln_gain[layer, 0], ln_bias[layer, 0], tm_row)
        route, counts = _router(x2, w_router, b_router, tm_row)
        x2 = _moe_ln(x2, route, counts, w_gate[layer].astype(BF16), w_up[layer].astype(BF16),
                     w_down[layer].astype(BF16), ln_gain[layer, 1], ln_bias[layer, 1], tm_row)
    return x2.reshape(bsz, s, d)
```

```python
import functools
import math

import numpy as np
import jax
import jax.numpy as jnp
from jax import lax
from jax.experimental import pallas as pl
from jax.experimental.pallas import tpu as pltpu
from jax.experimental.pallas import tpu_sc as plsc

F32 = jnp.float32
BF16 = jnp.bfloat16

D_MODEL = 2048
DEPTH = 2
HEAD_DIM = 128
N_HEADS_GROUP = 8
GROUP_WIDTH = N_HEADS_GROUP * HEAD_DIM
DIFF_QK_DIM = HEAD_DIM // 2
MOBA_BLOCK = 256
MOBA_TOPK = 3
N_BUCKETS = 32
MAX_EXACT = N_BUCKETS // 2
MAX_DISTANCE = 1024
N_EXPERTS = 16
N_GROUPS = 4
EXPERTS_PER_GROUP = N_EXPERTS // N_GROUPS
MOE_TOP_K = 2
SC_CORES = 2
SC_SUBCORES = 16
SC_GATHER_ROWS = 32
SUBLANES = 8
D_EXPERT = D_MODEL // 2
ALPHA = (2.0 * DEPTH) ** 0.25
LN_EPS = 1e-5
NEG_INF = -1e30
LOG2E = math.log2(math.e)
LANES = 128
ATT_TILE = 256
ATT_KEYS = 1024
FOX_TQ = 1024
FOX_TK = 1024
MOBA_Q_BLOCKS = 4
SB_Q_TILES = 2
DIFF_Q_BLOCKS = 4
SHIFT_SLACK = 96.0
FLUSH_EXPONENT = -130.0
SB_CUTOFF = -104.0
VMEM_LIMIT = 56 * 1024 * 1024


def _cparams(n_axes, vmem=None):
    return pltpu.CompilerParams(dimension_semantics=("arbitrary",) * n_axes,
                                vmem_limit_bytes=vmem)


def _nt_dot(a, b):
    return lax.dot_general(a, b, (((1,), (1,)), ((), ())), preferred_element_type=F32)


def _split_bf16(x):
    hi = x.astype(BF16)
    lo = (x - hi.astype(F32)).astype(BF16)
    return hi, lo


def _layer_norm(z, g, b):
    mu = jnp.mean(z, axis=-1, keepdims=True)
    zc = z - mu
    var = jnp.mean(zc * zc, axis=-1, keepdims=True)
    return zc * lax.rsqrt(var + LN_EPS) * g + b


def _inproj_kernel(x_ref, w_ref, o_ref, xb_ref):
    @pl.when(pl.program_id(1) == 0)
    def _():
        xb_ref[...] = x_ref[...].astype(BF16)

    o_ref[...] = jnp.dot(xb_ref[...], w_ref[...], preferred_element_type=F32).astype(o_ref.dtype)


def _in_proj(x, w, out_dtype, tm, tn):
    s, d = x.shape
    n = w.shape[1]
    return pl.pallas_call(
        _inproj_kernel,
        out_shape=jax.ShapeDtypeStruct((s, n), out_dtype),
        grid=(s // tm, n // tn),
        in_specs=[pl.BlockSpec((tm, d), lambda i, j: (i, 0)),
                  pl.BlockSpec((d, tn), lambda i, j: (0, j))],
        out_specs=pl.BlockSpec((tm, tn), lambda i, j: (i, j)),
        scratch_shapes=[pltpu.VMEM((tm, d), BF16)],
        compiler_params=_cparams(2, VMEM_LIMIT),
        name="in_proj",
    )(x, w)


def _outproj_ln_kernel(oa_ref, ob_ref, wa_ref, wb_ref, x_ref, g_ref, b_ref, o_ref):
    y = jnp.dot(oa_ref[...], wa_ref[...], preferred_element_type=F32)
    y = y + jnp.dot(ob_ref[...], wb_ref[...], preferred_element_type=F32)
    o_ref[...] = _layer_norm(ALPHA * x_ref[...] + y, g_ref[...], b_ref[...])


def _out_proj_ln(oa, ob, w, x, g, b, tm):
    s, d = x.shape
    gw = oa.shape[1]
    return pl.pallas_call(
        _outproj_ln_kernel,
        out_shape=jax.ShapeDtypeStruct((s, d), F32),
        grid=(s // tm,),
        in_specs=[pl.BlockSpec((tm, gw), lambda i: (i, 0)),
                  pl.BlockSpec((tm, gw), lambda i: (i, 0)),
                  pl.BlockSpec((gw, d), lambda i: (0, 0)),
                  pl.BlockSpec((gw, d), lambda i: (1, 0)),
                  pl.BlockSpec((tm, d), lambda i: (i, 0)),
                  pl.BlockSpec((1, d), lambda i: (0, 0)),
                  pl.BlockSpec((1, d), lambda i: (0, 0))],
        out_specs=pl.BlockSpec((tm, d), lambda i: (i, 0)),
        compiler_params=_cparams(1, VMEM_LIMIT),
        name="out_proj_ln",
    )(oa, ob, w, w, x, g.reshape(1, d), b.reshape(1, d))


def _router_kernel(x_ref, w_ref, b_ref, route_ref, count_ref):
    route, counts = _route(x_ref[...], w_ref[...], b_ref[...])
    route_ref[...] = route
    count_ref[...] = jnp.broadcast_to(counts, count_ref.shape)


def _router(x, w_router, b_router, tm):
    s, d = x.shape
    wr = jnp.zeros((d, LANES), F32).at[:, :N_EXPERTS].set(w_router)
    br = jnp.zeros((1, LANES), F32).at[0, :N_EXPERTS].set(b_router)
    return pl.pallas_call(
        _router_kernel,
        out_shape=(jax.ShapeDtypeStruct((s, LANES), F32),
                   jax.ShapeDtypeStruct((s // tm, SUBLANES, LANES), F32)),
        grid=(s // tm,),
        in_specs=[pl.BlockSpec((tm, d), lambda i: (i, 0)),
                  pl.BlockSpec((d, LANES), lambda i: (0, 0)),
                  pl.BlockSpec((1, LANES), lambda i: (0, 0))],
        out_specs=(pl.BlockSpec((tm, LANES), lambda i: (i, 0)),
                   pl.BlockSpec((None, SUBLANES, LANES), lambda i: (i, 0, 0))),
        compiler_params=_cparams(1, VMEM_LIMIT),
        name="router",
    )(x, wr, br)


def _decay_cumsum_kernel(f_ref, b_ref, c_ref):
    nh, s = f_ref.shape
    rows = lax.broadcasted_iota(jnp.int32, (LANES, LANES), 0)
    cols = lax.broadcasted_iota(jnp.int32, (LANES, LANES), 1)
    upper = jnp.where(rows <= cols, 1.0, 0.0).astype(F32)
    bias = b_ref[...]

    def body(n, carry):
        off = pl.multiple_of(n * LANES, LANES)
        z = f_ref[:, pl.ds(off, LANES)] + bias
        logf = jnp.minimum(z, 0.0) - jnp.log1p(jnp.exp(-jnp.abs(z)))
        c = jnp.dot(logf, upper, preferred_element_type=F32,
                    precision=lax.Precision.HIGHEST) + carry
        c_ref[:, pl.ds(off, LANES)] = c * LOG2E
        return carry + jnp.sum(logf, axis=-1, keepdims=True)

    lax.fori_loop(0, s // LANES, body, jnp.zeros((nh, 1), F32))


def _decay_cumsum(f_t, bias):
    nh, s = f_t.shape
    return pl.pallas_call(
        _decay_cumsum_kernel,
        out_shape=jax.ShapeDtypeStruct((nh, s), F32),
        name="decay_cumsum",
    )(f_t, bias.reshape(nh, 1))


def _t5_bucket_np(dist):
    n = np.maximum(dist, 0)
    nf = np.maximum(n, 1).astype(np.float32)
    ratio = np.log(nf / np.float32(MAX_EXACT)) / np.float32(math.log(MAX_DISTANCE / MAX_EXACT))
    large = MAX_EXACT + (ratio.astype(np.float32) * np.float32(N_BUCKETS - MAX_EXACT)).astype(np.int32)
    large = np.minimum(large, N_BUCKETS - 1)
    return np.where(n < MAX_EXACT, n, large).astype(np.int32)


def _near_tile_count(t):
    d = np.arange(0, 4 * MAX_DISTANCE, dtype=np.int64)
    not_last = np.nonzero(_t5_bucket_np(d) != N_BUCKETS - 1)[0]
    d_sat = int(not_last.max()) + 1
    n = 1
    while (n - 1) * t + 1 < d_sat:
        n += 1
    return n


def _bucket_tiles_np(t, n_near):
    r = np.arange(t)[:, None]
    c = np.arange(t)[None, :]
    tiles = []
    for delta in range(n_near):
        dist = delta * t + r - c
        tiles.append(np.where(dist >= 0, _t5_bucket_np(dist), -1))
    tiles.append(np.full((t, t), N_BUCKETS - 1))
    tiles.append(np.full((t, t), -1))
    return np.stack(tiles).astype(np.int32)


def _bias_tiles_kernel(tab_ref, idx_ref, o_ref):
    h = pl.program_id(0)
    idx = idx_ref[...]
    out = jnp.full(idx.shape, NEG_INF, F32)
    for b in range(N_BUCKETS):
        out = jnp.where(idx == b, tab_ref[b, h] * LOG2E, out)
    o_ref[...] = out


def _bias_tiles(rel_bias, t, n_near):
    nh = rel_bias.shape[1]
    idx = jnp.asarray(_bucket_tiles_np(t, n_near))
    n_tiles = idx.shape[0]
    return pl.pallas_call(
        _bias_tiles_kernel,
        out_shape=jax.ShapeDtypeStruct((nh, n_tiles, t, t), F32),
        grid=(nh, n_tiles),
        in_specs=[pl.BlockSpec(memory_space=pltpu.SMEM),
                  pl.BlockSpec((None, t, t), lambda h, n: (n, 0, 0))],
        out_specs=pl.BlockSpec((None, None, t, t), lambda h, n: (h, n, 0, 0)),
        compiler_params=_cparams(2),
        name="bias_tiles",
    )(rel_bias, idx)


def _online_softmax_step(s, v, carry):
    m, l, acc = carry
    m_new = jnp.maximum(m, jnp.max(s, axis=-1, keepdims=True))
    alpha = jnp.exp2(m - m_new)
    p = jnp.exp2(s - m_new)
    l = alpha * l + jnp.sum(p, axis=-1, keepdims=True)
    acc = alpha * acc + jnp.dot(p.astype(BF16), v, preferred_element_type=F32)
    return m_new, l, acc


def _prescale(q, c):
    return (q.astype(F32) * c).astype(BF16)


def _near_bias(bias_ref, blocks, j, tb, tk, n_near):
    nsub = tk // tb
    rows = []
    for i in blocks:
        tiles = []
        for b in range(nsub):
            delta = i - (j * nsub + b)
            tiles.append(bias_ref[jnp.where(delta < 0, n_near + 1, jnp.minimum(delta, n_near))])
        rows.append(jnp.concatenate(tiles, axis=1))
    return rows[0] if len(rows) == 1 else jnp.concatenate(rows, axis=0)


def _biased_sweep(i, t, tk, scores, process, init):
    jd = (i * t) // tk
    carry = process(scores(jd), jd, init, True)
    lo = jnp.maximum(jd - 1, 0)
    carry = lax.fori_loop(lo, jd, lambda j, c: process(scores(j), j, c, True), carry)
    return lax.fori_loop(0, lo, lambda j, c: process(scores(j), j, c, False), carry)


def _softmax_init(rows):
    return (jnp.full((rows, 1), NEG_INF, F32), jnp.zeros((rows, 1), F32),
            jnp.zeros((rows, HEAD_DIM), F32))


def _fox_kernel(cend_ref, cstart_ref, q_ref, k_ref, v_ref, c_ref, ccol_ref, o_ref, knorm_ref,
                *, t, tk, scale):
    h = pl.program_id(0)
    i = pl.program_id(1)

    @pl.when(i == 0)
    def _():
        knorm_ref[...] = jnp.broadcast_to(_max_key_norm(k_ref, tk, 0, HEAD_DIM), knorm_ref.shape)

    q = _prescale(q_ref[...], scale * LOG2E)
    rows = lax.broadcasted_iota(jnp.int32, (t, tk), 0)
    cols = lax.broadcasted_iota(jnp.int32, (t, tk), 1)

    def scores(j):
        off = pl.multiple_of(j * tk, tk)
        return _nt_dot(q, k_ref[pl.ds(off, tk), :]) - c_ref[:, pl.ds(off, tk)]

    def values(j):
        return v_ref[pl.ds(pl.multiple_of(j * tk, tk), tk), :]

    def process(s, j, carry):
        return _online_softmax_step(s, values(j), carry)

    jd = (i * t) // tk
    s_diag = jnp.where(cols <= rows + (i * t - jd * tk), scores(jd), NEG_INF)

    qf = q.astype(F32)
    qnorm = jnp.sqrt(jnp.sum(qf * qf, axis=-1, keepdims=True))
    bound = qnorm * knorm_ref[0:1, 0:1] * (1.0 + 2.0 ** -9) + 2.0 ** -6 - ccol_ref[...]
    slack = jnp.max(bound - jnp.max(s_diag, axis=-1, keepdims=True))

    def shifted_sweep(_):
        def step(s, j, carry):
            l, acc = carry
            p = jnp.exp2(s - bound)
            return (l + jnp.sum(p, axis=-1, keepdims=True),
                    acc + jnp.dot(p.astype(BF16), values(j), preferred_element_type=F32))

        c_first_row = cstart_ref[h, i]
        first_live = jnp.int32(0)
        for j in range(cend_ref.shape[1]):
            dead = jnp.logical_and(j < jd, c_first_row - cend_ref[h, j] < FLUSH_EXPONENT)
            first_live = first_live + dead.astype(jnp.int32)
        carry = step(s_diag, jd, _softmax_init(t)[1:])
        return _paired_loop(jd - first_live, lambda n, c: step(scores(first_live + n), first_live + n, c),
                            carry)

    def running_max_sweep(_):
        carry = _online_softmax_step(s_diag, values(jd), _softmax_init(t))
        return lax.fori_loop(0, jd, lambda j, c: process(scores(j), j, c), carry)[1:]

    l, acc = lax.cond(slack <= SHIFT_SLACK, shifted_sweep, running_max_sweep, 0)
    o_ref[...] = (acc / l).astype(o_ref.dtype)


def _moba_kernel(tab_ref, q_ref, k_ref, v_ref, bias_ref, et_ref, o_ref, kmean_ref, knorm_ref,
                 *, tb, nb, tk, nblk, n_near, scale):
    h = pl.program_id(0)
    g = pl.program_id(1)
    t = nb * tb
    nbp = kmean_ref.shape[0]

    @pl.when(g == 0)
    def _():
        kmean_ref[...] = jnp.zeros(kmean_ref.shape, F32)

        def mean_body(n, _):
            off = pl.multiple_of(n * tb, tb)
            kb = k_ref[pl.ds(off, tb), :].astype(F32)
            kmean_ref[pl.ds(n, 1), :] = jnp.sum(kb, axis=0, keepdims=True) * (1.0 / tb)
            return 0

        lax.fori_loop(0, nblk, mean_body, 0)
        knorm_ref[...] = jnp.broadcast_to(_max_key_norm(k_ref, tk, 0, HEAD_DIM), knorm_ref.shape)

    q = q_ref[...]
    km_hi, km_lo = _split_bf16(kmean_ref[...])
    gate = _nt_dot(q, km_hi) + _nt_dot(q, km_lo)
    blk = lax.broadcasted_iota(jnp.int32, (t, nbp), 1)
    own = g * nb + lax.broadcasted_iota(jnp.int32, (t, 1), 0) // tb
    eligible = blk < own
    sel = jnp.zeros((t, nbp), F32)
    for r in range(MOBA_TOPK):
        cand = jnp.where(eligible, jnp.where(sel > 0.0, -jnp.inf, gate), -jnp.inf)
        vmax = jnp.max(cand, axis=-1, keepdims=True)
        first = jnp.min(jnp.where(cand == vmax, blk, nbp), axis=-1, keepdims=True)
        take = jnp.where(r < own, 1.0, 0.0)
        sel = jnp.maximum(sel, jnp.where(blk == first, take, 0.0))

    sel = jnp.maximum(sel, jnp.where(blk == own, 1.0, 0.0))
    penalty = jnp.where(sel > 0.0, 0.0, NEG_INF).astype(BF16)
    q_aug = jnp.concatenate([_prescale(q, scale * LOG2E), penalty], axis=1)
    b_far = tab_ref[N_BUCKETS - 1, h] * LOG2E

    def scores(j):
        off = pl.multiple_of(j * tk, tk)
        k_aug = jnp.concatenate([k_ref[pl.ds(off, tk), :], et_ref[pl.ds(off, tk), :]], axis=1)
        return _nt_dot(q_aug, k_aug)

    def values(j):
        return v_ref[pl.ds(pl.multiple_of(j * tk, tk), tk), :]

    def near_bias(j):
        return _near_bias(bias_ref, [g * nb + b for b in range(nb)], j, tb, tk, n_near)

    def process(s, j, carry, near):
        return _online_softmax_step(s + (near_bias(j) if near else b_far), values(j), carry)

    qf = q_aug[:, :HEAD_DIM].astype(F32)
    qnorm = jnp.sqrt(jnp.sum(qf * qf, axis=-1, keepdims=True))
    bound = _logit_bound(qnorm, knorm_ref[0:1, 0:1], tab_ref, h)
    l, acc = _shifted_or_running_sweep(g, t, tk, scores, near_bias, values, process, bound, b_far, t)
    o_ref[...] = (acc / l).astype(o_ref.dtype)


def _paired_loop(n, body, carry):
    carry = lax.fori_loop(0, n // 2, lambda p, c: body(2 * p + 1, body(2 * p, c)), carry)
    return lax.fori_loop(0, n % 2, lambda _, c: body(n - 1, c), carry)


def _logit_bound(qnorm, knorm, tab_ref, h):
    b_max = tab_ref[0, h]
    for b in range(1, N_BUCKETS):
        b_max = jnp.maximum(b_max, tab_ref[b, h])
    return qnorm * knorm * (1.0 + 2.0 ** -9) + (b_max * LOG2E + 2.0 ** -6)


def _shifted_or_running_sweep(i, t, tk, scores, near_bias, values, process, bound, b_far, rows):
    jd = (i * t) // tk
    s_diag = scores(jd) + near_bias(jd)
    slack = jnp.max(bound - jnp.max(s_diag, axis=-1, keepdims=True))

    def shifted_sweep(_):
        def step(s, j, carry, shift):
            l, acc = carry
            p = jnp.exp2(s - shift)
            return (l + jnp.sum(p, axis=-1, keepdims=True),
                    acc + jnp.dot(p.astype(BF16), values(j), preferred_element_type=F32))

        carry = step(s_diag, jd, _softmax_init(rows)[1:], bound)
        lo = jnp.maximum(jd - 1, 0)
        carry = lax.fori_loop(lo, jd, lambda j, c: step(scores(j) + near_bias(j), j, c, bound), carry)
        far_shift = bound - b_far
        return _paired_loop(lo, lambda j, c: step(scores(j), j, c, far_shift), carry)

    def running_max_sweep(_):
        return _biased_sweep(i, t, tk, scores, process, _softmax_init(rows))[1:]

    return lax.cond(slack <= SHIFT_SLACK, shifted_sweep, running_max_sweep, 0)


def _max_key_norm(k_ref, tk, lane_lo, lane_hi):
    klane = lax.broadcasted_iota(jnp.int32, (tk, HEAD_DIM), 1)
    keep = jnp.logical_and(klane >= lane_lo, klane < lane_hi)

    def body(n, c):
        kt = k_ref[pl.ds(pl.multiple_of(n * tk, tk), tk), :].astype(F32)
        sq = jnp.sum(jnp.where(keep, kt * kt, 0.0), axis=-1, keepdims=True)
        return jnp.maximum(c, jnp.max(sq, axis=0, keepdims=True))

    return jnp.sqrt(lax.fori_loop(0, k_ref.shape[0] // tk, body, jnp.zeros((1, 1), F32)))


def _diff_kernel(tab_ref, q_ref, k_ref, v_ref, bias_ref, lq_ref, lk_ref, g_ref, o_ref, knorm_ref,
                 *, tb, nb, tk, n_near, scale, lambda_init):
    h = pl.program_id(0)
    i = pl.program_id(1)
    t = nb * tb
    q = _prescale(q_ref[...], scale * LOG2E)
    lane = lax.broadcasted_iota(jnp.int32, (t, HEAD_DIM), 1)
    zero = jnp.zeros_like(q)
    q2 = jnp.concatenate([jnp.where(lane < DIFF_QK_DIM, q, zero),
                          jnp.where(lane >= DIFF_QK_DIM, q, zero)], axis=0)
    b_far = tab_ref[N_BUCKETS - 1, h] * LOG2E

    @pl.when(i == 0)
    def _():
        knorm_ref[0:1, :] = jnp.broadcast_to(_max_key_norm(k_ref, tk, 0, DIFF_QK_DIM), (1, LANES))
        knorm_ref[1:2, :] = jnp.broadcast_to(_max_key_norm(k_ref, tk, DIFF_QK_DIM, HEAD_DIM), (1, LANES))

    def scores(j):
        return _nt_dot(q2, k_ref[pl.ds(pl.multiple_of(j * tk, tk), tk), :])

    def values(j):
        return v_ref[pl.ds(pl.multiple_of(j * tk, tk), tk), :]

    def near_bias(j):
        b = _near_bias(bias_ref, [i * nb + b for b in range(nb)], j, tb, tk, n_near)
        return jnp.concatenate([b, b], axis=0)

    def process(s, j, carry, near):
        return _online_softmax_step(s + (near_bias(j) if near else b_far), values(j), carry)

    q2f = q2.astype(F32)
    qnorm = jnp.sqrt(jnp.sum(q2f * q2f, axis=-1, keepdims=True))
    knorm = jnp.concatenate([jnp.broadcast_to(knorm_ref[0:1, 0:1], (t, 1)),
                             jnp.broadcast_to(knorm_ref[1:2, 0:1], (t, 1))], axis=0)
    bound = _logit_bound(qnorm, knorm, tab_ref, h)
    l, acc = _shifted_or_running_sweep(i, t, tk, scores, near_bias, values, process, bound, b_far, 2 * t)
    o12 = acc / l
    lam_e = jnp.exp(jnp.sum(lq_ref[...] * lk_ref[...], axis=-1, keepdims=True))
    lam = lam_e[0:1, :] - lam_e[1:2, :] + lambda_init
    o = o12[:t, :] - lam * o12[t:, :]
    o = o * lax.rsqrt(jnp.mean(o * o, axis=-1, keepdims=True) + LN_EPS) * g_ref[...]
    o_ref[...] = (o * (1.0 - lambda_init)).astype(o_ref.dtype)


def _sb_kernel(q_ref, k_ref, v_ref, o_ref, *, t, nq, scale):
    g = pl.program_id(1)

    def later_matrix(n):
        r = lax.broadcasted_iota(jnp.int32, (n, n), 0)
        c = lax.broadcasted_iota(jnp.int32, (n, n), 1)
        return jnp.where(r > c, 1.0, 0.0).astype(BF16)

    def block(q, off, nk, rem, acc, shift):
        k = k_ref[pl.ds(off, nk), :]
        v = v_ref[pl.ds(off, nk), :]
        z = _nt_dot(q, k) * scale
        soft = jnp.log1p(jnp.exp(-jnp.abs(z)))
        log_sig = jnp.minimum(z, 0.0) - soft
        log_rem = jnp.minimum(-z, 0.0) - soft
        if shift is not None:
            rows = lax.broadcasted_iota(jnp.int32, (t, nk), 0)
            cols = lax.broadcasted_iota(jnp.int32, (t, nk), 1)
            mask = cols < rows + shift
            log_rem = jnp.where(mask, log_rem, 0.0)
        hi, lo = _split_bf16(log_rem)
        later = later_matrix(nk)
        after = (jnp.dot(hi, later, preferred_element_type=F32)
                 + jnp.dot(lo, later, preferred_element_type=F32))
        a = jnp.exp(log_sig + after + rem)
        if shift is not None:
            a = jnp.where(mask, a, 0.0)
        acc = acc + jnp.dot(a.astype(BF16), v, preferred_element_type=F32)
        rem = rem + jnp.sum(log_rem, axis=-1, keepdims=True)
        return rem, acc

    def live(rem):
        return (jnp.max(rem) > SB_CUTOFF).astype(jnp.int32)

    state = []
    for u in range(nq):
        i = g * nq + u
        q = q_ref[u * t:(u + 1) * t, :]
        first = jnp.maximum(i - 1, 0)
        rem, acc = block(q, pl.multiple_of(first * t, t), 2 * t, jnp.zeros((t, 1), F32),
                         jnp.zeros((t, HEAD_DIM), F32), (i - first) * t)
        state.append((i, q, rem, acc))

    for u, (i, q, rem, acc) in enumerate(state):
        def cond(c):
            j, go, _, _ = c
            return jnp.logical_and(j >= 0, go > 0)

        def body(c, q=q):
            j, _, rem, acc = c
            rem, acc = block(q, pl.multiple_of(j * t, t), t, rem, acc, None)
            return j - 1, live(rem), rem, acc

        _, _, _, acc = lax.while_loop(cond, body, (i - 2, live(rem), rem, acc))
        o_ref[u * t:(u + 1) * t, :] = acc.astype(o_ref.dtype)


def _attention_call(kernel, h, out_cols, col0, extra_in, extra_specs, scratch, name, t):
    s = h.shape[0]
    nh = N_HEADS_GROUP
    in_specs = list(extra_specs[0]) + [
        pl.BlockSpec((t, HEAD_DIM), lambda hd, i: (i, col0 + hd)),
        pl.BlockSpec((s, HEAD_DIM), lambda hd, i: (0, col0 + nh + hd)),
        pl.BlockSpec((s, HEAD_DIM), lambda hd, i: (0, col0 + 2 * nh + hd)),
    ] + list(extra_specs[1])
    args = list(extra_in[0]) + [h, h, h] + list(extra_in[1])
    return pl.pallas_call(
        kernel,
        out_shape=jax.ShapeDtypeStruct((s, out_cols), BF16),
        grid=(nh, s // t),
        in_specs=in_specs,
        out_specs=pl.BlockSpec((t, HEAD_DIM), lambda hd, i: (i, hd)),
        scratch_shapes=scratch,
        compiler_params=_cparams(2, VMEM_LIMIT),
        name=name,
    )(*args)


def _route(x, w, b):
    logits = jnp.dot(x, w, preferred_element_type=F32, precision=lax.Precision.HIGHEST) + b
    tm, n = logits.shape
    lane = lax.broadcasted_iota(jnp.int32, (tm, n), 1)
    real = lane < N_EXPERTS
    logits = jnp.where(real, logits, -jnp.inf)
    e = jnp.exp(logits - jnp.max(logits, axis=-1, keepdims=True))
    aff = e / jnp.sum(e, axis=-1, keepdims=True)
    group = lane // EXPERTS_PER_GROUP
    best = jnp.full((tm, 1), -jnp.inf, F32)
    g_sel = jnp.zeros((tm, 1), jnp.int32)
    for g in range(N_GROUPS):
        gmax = jnp.max(jnp.where(group == g, aff, -jnp.inf), axis=-1, keepdims=True)
        better = gmax > best
        g_sel = jnp.where(better, g, g_sel)
        best = jnp.where(better, gmax, best)
    in_group = jnp.logical_and(group == g_sel, real)
    cand = jnp.where(in_group, aff, -jnp.inf)
    v1 = jnp.max(cand, axis=-1, keepdims=True)
    i1 = jnp.min(jnp.where(cand == v1, lane, n), axis=-1, keepdims=True)
    cand2 = jnp.where(lane == i1, -jnp.inf, cand)
    v2 = jnp.max(cand2, axis=-1, keepdims=True)
    i2 = jnp.min(jnp.where(cand2 == v2, lane, n), axis=-1, keepdims=True)
    tot = v1 + v2
    chosen = jnp.where(jnp.logical_or(lane == i1, lane == i2), 1.0, 0.0)
    r = lax.broadcasted_iota(jnp.int32, (tm, tm), 0)
    c = lax.broadcasted_iota(jnp.int32, (tm, tm), 1)
    earlier = jnp.where(c < r, 1.0, 0.0).astype(BF16)
    before = jnp.dot(earlier, chosen.astype(BF16), preferred_element_type=F32)
    rank1 = jnp.sum(jnp.where(lane == i1, before, 0.0), axis=-1, keepdims=True)
    rank2 = jnp.sum(jnp.where(lane == i2, before, 0.0), axis=-1, keepdims=True)
    route = jnp.where(lane == 0, v1 / tot,
                      jnp.where(lane == 1, v2 / tot,
                                jnp.where(lane == 2, i1.astype(F32),
                                          jnp.where(lane == 3, i2.astype(F32),
                                                    jnp.where(lane == 4, rank1,
                                                              jnp.where(lane == 5, rank2, 0.0))))))
    return route, jnp.sum(chosen, axis=0, keepdims=True)


def _dispatch_plan(route, counts, tm):
    s = route.shape[0]
    nt = s // tm
    n_tiles = (MOE_TOP_K * s) // tm + N_EXPERTS
    ids = jnp.arange(N_EXPERTS, dtype=jnp.int32)
    expert = route[:, 2:2 + MOE_TOP_K].astype(jnp.int32).reshape(nt, tm, MOE_TOP_K)
    rank = route[:, 4:4 + MOE_TOP_K].astype(jnp.int32).reshape(nt, tm, MOE_TOP_K)
    cnt = counts[:, 0, :N_EXPERTS].astype(jnp.int32)
    tile_base = jnp.cumsum(cnt, axis=0) - cnt
    padded = -(-jnp.sum(cnt, axis=0) // tm) * tm
    seg_end = jnp.cumsum(padded)
    base = (seg_end - padded)[None, :] + tile_base
    pick = expert[..., None] == ids
    pair_row = (jnp.sum(jnp.where(pick, base[:, None, None, :], 0), axis=-1) + rank).reshape(s, MOE_TOP_K)
    tile_start = jnp.arange(n_tiles, dtype=jnp.int32) * tm
    tile_expert = jnp.minimum(jnp.sum((tile_start[:, None] >= seg_end[None, :]).astype(jnp.int32), axis=1),
                              N_EXPERTS - 1).astype(jnp.int32)
    tile_live = (tile_start < seg_end[-1]).astype(jnp.int32)
    row_token = (jnp.arange(n_tiles * tm, dtype=jnp.int32) % s).at[pair_row.reshape(-1)].set(
        jnp.arange(MOE_TOP_K * s, dtype=jnp.int32) // MOE_TOP_K)
    return pair_row, row_token, tile_expert, tile_live


def _moe_group_kernel(te_ref, live_ref, x_ref, wg_ref, wu_ref, wd_ref, o_ref, xb_ref):
    g = pl.program_id(0)
    f = pl.program_id(1)
    live = live_ref[g] > 0

    @pl.when(jnp.logical_and(live, f == 0))
    def _():
        xb_ref[...] = x_ref[...].astype(BF16)

    @pl.when(jnp.logical_and(jnp.logical_not(live), f == 0))
    def _():
        o_ref[...] = jnp.zeros(o_ref.shape, F32)

    @pl.when(live)
    def _():
        xb = xb_ref[...]
        a = jnp.dot(xb, wg_ref[...].astype(BF16), preferred_element_type=F32)
        u = jnp.dot(xb, wu_ref[...].astype(BF16), preferred_element_type=F32)
        hid = ((a * jax.nn.sigmoid(a)) * u).astype(BF16)
        y = jnp.dot(hid, wd_ref[...].astype(BF16), preferred_element_type=F32)

        @pl.when(f == 0)
        def _():
            o_ref[...] = y

        @pl.when(f > 0)
        def _():
            o_ref[...] += y


def _moe_group_ffn(xs, tile_expert, tile_live, wg, wu, wd, tm, tf):
    rows, d = xs.shape
    fdim = wg.shape[2]
    grid_spec = pltpu.PrefetchScalarGridSpec(
        num_scalar_prefetch=2,
        grid=(rows // tm, fdim // tf),
        in_specs=[pl.BlockSpec((tm, d), lambda g, f, te, lv: (g, 0)),
                  pl.BlockSpec((None, d, tf), lambda g, f, te, lv: (te[g], 0, f)),
                  pl.BlockSpec((None, d, tf), lambda g, f, te, lv: (te[g], 0, f)),
                  pl.BlockSpec((None, tf, d), lambda g, f, te, lv: (te[g], f, 0))],
        out_specs=pl.BlockSpec((tm, d), lambda g, f, te, lv: (g, 0)),
        scratch_shapes=[pltpu.VMEM((tm, d), BF16)])
    return pl.pallas_call(
        _moe_group_kernel,
        out_shape=jax.ShapeDtypeStruct((rows, d), F32),
        grid_spec=grid_spec,
        compiler_params=_cparams(2, VMEM_LIMIT),
        name="moe_group_ffn",
    )(tile_expert, tile_live, xs, wg, wu, wd)


def _moe_combine_ln_kernel(x_ref, r_ref, y0_ref, y1_ref, g_ref, b_ref, o_ref):
    r = r_ref[...]
    ff = r[:, 0:1] * y0_ref[...] + r[:, 1:2] * y1_ref[...]
    o_ref[...] = _layer_norm(ALPHA * x_ref[...] + ff, g_ref[...], b_ref[...])


def _moe_combine_ln(x, route, y, g, b, tm):
    s, d = x.shape
    nt = s // tm
    return pl.pallas_call(
        _moe_combine_ln_kernel,
        out_shape=jax.ShapeDtypeStruct((s, d), F32),
        grid=(nt,),
        in_specs=[pl.BlockSpec((tm, d), lambda i: (i, 0)),
                  pl.BlockSpec((tm, LANES), lambda i: (i, 0)),
                  pl.BlockSpec((tm, d), lambda i: (i, 0)),
                  pl.BlockSpec((tm, d), lambda i: (i + nt, 0)),
                  pl.BlockSpec((1, d), lambda i: (0, 0)),
                  pl.BlockSpec((1, d), lambda i: (0, 0))],
        out_specs=pl.BlockSpec((tm, d), lambda i: (i, 0)),
        compiler_params=_cparams(1, VMEM_LIMIT),
        name="moe_combine_ln",
    )(x, route, y, y, g.reshape(1, d), b.reshape(1, d))


def _sc_row_gather(src, idx):
    n = idx.shape[0]
    d = src.shape[1]
    workers = SC_CORES * SC_SUBCORES
    per_worker = n // workers
    assert n % (workers * SC_GATHER_ROWS) == 0
    mesh = plsc.VectorSubcoreMesh(core_axis_name="c", subcore_axis_name="s")

    @functools.partial(
        pl.kernel, mesh=mesh, out_type=jax.ShapeDtypeStruct((n, d), src.dtype),
        scratch_types=[pltpu.VMEM((SC_GATHER_ROWS,), jnp.int32),
                       pltpu.VMEM((SC_GATHER_ROWS, d), src.dtype),
                       pltpu.SemaphoreType.DMA],
        name="sc_row_gather")
    def gather(src_hbm, idx_hbm, out_hbm, idx_v, rows_v, sem):
        base = (lax.axis_index("s") * SC_CORES + lax.axis_index("c")) * per_worker

        @pl.loop(0, per_worker, step=SC_GATHER_ROWS)
        def _(off):
            pltpu.sync_copy(idx_hbm.at[pl.ds(base + off, SC_GATHER_ROWS)], idx_v)
            pltpu.async_copy(src_hbm.at[idx_v], rows_v, sem).wait()
            pltpu.sync_copy(rows_v, out_hbm.at[pl.ds(base + off, SC_GATHER_ROWS)])

    return gather(src, idx)


def _moe_ln(x, route, counts, wg, wu, wd, g, b, tm):
    pair_row, row_token, tile_expert, tile_live = _dispatch_plan(route, counts, tm)
    xs = _sc_row_gather(x, row_token)
    ys = _moe_group_ffn(xs, tile_expert, tile_live, wg, wu, wd, tm, D_EXPERT // 2)
    slot_major = pair_row.T.reshape(-1)
    y = _sc_row_gather(ys, slot_major)
    return _moe_combine_ln(x, route, y, g, b, tm)


def kernel(x, w_in_even, w_out_even, forget_bias, w_in_odd, w_out_odd, lambda_q, lambda_k,
           subln_gain, rel_bias, w_router, b_router, w_gate, w_up, w_down, ln_gain, ln_bias):
    bsz, s, d = x.shape
    assert bsz == 1 and d == D_MODEL and s % MOBA_BLOCK == 0
    t = ATT_TILE
    nh = N_HEADS_GROUP
    gw = GROUP_WIDTH
    tm_proj = min(1024, s)
    tm_row = min(512, s)
    tk = min(ATT_KEYS, s)
    n_near = _near_tile_count(t)
    assert s % tk == 0 and tk // t + 1 >= n_near
    nblk = s // MOBA_BLOCK
    nbp = -(-nblk // LANES) * LANES

    x2 = x.reshape(s, d)
    bias_tiles = _bias_tiles(rel_bias, t, n_near)
    tab_spec = pl.BlockSpec(memory_space=pltpu.SMEM)
    bias_spec = pl.BlockSpec((None, n_near + 2, t, t), lambda hd, i: (hd, 0, 0, 0))
    block_onehot = (jnp.arange(s, dtype=jnp.int32)[:, None] // MOBA_BLOCK
                    == jnp.arange(nbp, dtype=jnp.int32)[None, :]).astype(BF16)

    for layer in range(DEPTH):
        li = layer // 2
        if layer % 2 == 0:
            w = w_in_even[li]
            fcol = 3 * gw
            w_main = jnp.concatenate([w[:, :fcol], w[:, fcol + nh:]], axis=1).astype(BF16)
            w_f = jnp.zeros((d, LANES), BF16).at[:, :nh].set(w[:, fcol:fcol + nh].astype(BF16))
            h = _in_proj(x2, w_main, BF16, tm_proj, 512)
            f = _in_proj(x2, w_f, F32, tm_proj, LANES)
            c = _decay_cumsum(f[:, :nh].T, forget_bias[li])
            oa = _attention_call(
                functools.partial(_fox_kernel, t=FOX_TQ, tk=min(FOX_TK, s), scale=HEAD_DIM ** -0.5), h, gw, 0,
                ([c[:, min(FOX_TK, s) - 1::min(FOX_TK, s)], c[:, ::FOX_TQ]],
                 [c.reshape(nh, 1, s), c.reshape(nh, s, 1)]),
                ([tab_spec, tab_spec],
                 [pl.BlockSpec((None, 1, s), lambda hd, i: (hd, 0, 0)),
                  pl.BlockSpec((None, FOX_TQ, 1), lambda hd, i: (hd, i, 0))]),
                [pltpu.VMEM((SUBLANES, LANES), F32)],
                "fox_attention", FOX_TQ)
            ob = _attention_call(
                functools.partial(_moba_kernel, tb=t, nb=MOBA_Q_BLOCKS, tk=tk, nblk=nblk, n_near=n_near,
                                  scale=HEAD_DIM ** -0.5),
                h, gw, 3 * nh,
                ([rel_bias], [bias_tiles, block_onehot]),
                ([tab_spec], [bias_spec, pl.BlockSpec((s, nbp), lambda hd, i: (0, 0))]),
                [pltpu.VMEM((nbp, HEAD_DIM), F32), pltpu.VMEM((SUBLANES, LANES), F32)],
                "moba_attention", MOBA_Q_BLOCKS * t)
            w_out = w_out_even[li]
        else:
            lambda_init = 0.8 - 0.6 * math.exp(-0.3 * layer)
            h = _in_proj(x2, w_in_odd[li].astype(BF16), BF16, tm_proj, 512)
            small = lambda shape: pl.BlockSpec(shape, lambda hd, i: (0, 0))
            oa = _attention_call(
                functools.partial(_diff_kernel, tb=t, nb=DIFF_Q_BLOCKS, tk=tk, n_near=n_near,
                                  scale=DIFF_QK_DIM ** -0.5, lambda_init=lambda_init),
                h, gw, 0,
                ([rel_bias], [bias_tiles, lambda_q[li], lambda_k[li], subln_gain[li].reshape(1, HEAD_DIM)]),
                ([tab_spec], [bias_spec, small((2, DIFF_QK_DIM)), small((2, DIFF_QK_DIM)),
                              small((1, HEAD_DIM))]),
                [pltpu.VMEM((SUBLANES, LANES), F32)],
                "diff_attention", DIFF_Q_BLOCKS * t)
            ob = _attention_call(
                functools.partial(_sb_kernel, t=t, nq=SB_Q_TILES, scale=HEAD_DIM ** -0.5), h, gw, 3 * nh,
                ([], []), ([], []), [], "stickbreak_attention", SB_Q_TILES * t)
            w_out = w_out_odd[li]
        x2 = _out_proj_ln(oa, ob, w_out.astype(BF16), x2, ln_gain[layer, 0], ln_bias[layer, 0], tm_row)
        route, counts = _router(x2, w_router, b_router, tm_row)
        x2 = _moe_ln(x2, route, counts, w_gate[layer], w_up[layer], w_down[layer],
                     ln_gain[layer, 1], ln_bias[layer, 1], tm_row)
    return x2.reshape(bsz, s, d)
```

```python
import functools
import math

import numpy as np
import jax
import jax.numpy as jnp
from jax import lax
from jax.experimental import pallas as pl
from jax.experimental.pallas import tpu as pltpu
from jax.experimental.pallas import tpu_sc as plsc

F32 = jnp.float32
BF16 = jnp.bfloat16

D_MODEL = 2048
DEPTH = 2
HEAD_DIM = 128
N_HEADS_GROUP = 8
GROUP_WIDTH = N_HEADS_GROUP * HEAD_DIM
DIFF_QK_DIM = HEAD_DIM // 2
MOBA_BLOCK = 256
MOBA_TOPK = 3
N_BUCKETS = 32
MAX_EXACT = N_BUCKETS // 2
MAX_DISTANCE = 1024
N_EXPERTS = 16
N_GROUPS = 4
EXPERTS_PER_GROUP = N_EXPERTS // N_GROUPS
MOE_TOP_K = 2
SC_CORES = 2
SC_SUBCORES = 16
SC_GATHER_ROWS = 16
SUBLANES = 8
D_EXPERT = D_MODEL // 2
ALPHA = (2.0 * DEPTH) ** 0.25
LN_EPS = 1e-5
NEG_INF = -1e30
LOG2E = math.log2(math.e)
LANES = 128
ATT_TILE = 256
ATT_KEYS = 1024
FOX_TQ = 1024
FOX_TK = 1024
MOBA_Q_BLOCKS = 4
SB_Q_TILES = 2
DIFF_Q_BLOCKS = 4
SHIFT_SLACK = 96.0
FLUSH_EXPONENT = -130.0
SB_CUTOFF = -104.0
VMEM_LIMIT = 56 * 1024 * 1024


def _cparams(n_axes, vmem=None):
    return pltpu.CompilerParams(dimension_semantics=("arbitrary",) * n_axes,
                                vmem_limit_bytes=vmem)


def _nt_dot(a, b):
    return lax.dot_general(a, b, (((1,), (1,)), ((), ())), preferred_element_type=F32)


def _split_bf16(x):
    hi = x.astype(BF16)
    lo = (x - hi.astype(F32)).astype(BF16)
    return hi, lo


def _layer_norm(z, g, b):
    mu = jnp.mean(z, axis=-1, keepdims=True)
    zc = z - mu
    var = jnp.mean(zc * zc, axis=-1, keepdims=True)
    return zc * lax.rsqrt(var + LN_EPS) * g + b


def _inproj_kernel(x_ref, w_ref, o_ref, xb_ref):
    @pl.when(pl.program_id(1) == 0)
    def _():
        xb_ref[...] = x_ref[...].astype(BF16)

    o_ref[...] = jnp.dot(xb_ref[...], w_ref[...], preferred_element_type=F32).astype(o_ref.dtype)


def _in_proj(x, w, out_dtype, tm, tn):
    s, d = x.shape
    n = w.shape[1]
    return pl.pallas_call(
        _inproj_kernel,
        out_shape=jax.ShapeDtypeStruct((s, n), out_dtype),
        grid=(s // tm, n // tn),
        in_specs=[pl.BlockSpec((tm, d), lambda i, j: (i, 0)),
                  pl.BlockSpec((d, tn), lambda i, j: (0, j))],
        out_specs=pl.BlockSpec((tm, tn), lambda i, j: (i, j)),
        scratch_shapes=[pltpu.VMEM((tm, d), BF16)],
        compiler_params=_cparams(2, VMEM_LIMIT),
        name="in_proj",
    )(x, w)


def _outproj_ln_kernel(oa_ref, ob_ref, wa_ref, wb_ref, x_ref, g_ref, b_ref, o_ref):
    y = jnp.dot(oa_ref[...], wa_ref[...], preferred_element_type=F32)
    y = y + jnp.dot(ob_ref[...], wb_ref[...], preferred_element_type=F32)
    o_ref[...] = _layer_norm(ALPHA * x_ref[...] + y, g_ref[...], b_ref[...])


def _out_proj_ln(oa, ob, w, x, g, b, tm):
    s, d = x.shape
    gw = oa.shape[1]
    return pl.pallas_call(
        _outproj_ln_kernel,
        out_shape=jax.ShapeDtypeStruct((s, d), F32),
        grid=(s // tm,),
        in_specs=[pl.BlockSpec((tm, gw), lambda i: (i, 0)),
                  pl.BlockSpec((tm, gw), lambda i: (i, 0)),
                  pl.BlockSpec((gw, d), lambda i: (0, 0)),
                  pl.BlockSpec((gw, d), lambda i: (1, 0)),
                  pl.BlockSpec((tm, d), lambda i: (i, 0)),
                  pl.BlockSpec((1, d), lambda i: (0, 0)),
                  pl.BlockSpec((1, d), lambda i: (0, 0))],
        out_specs=pl.BlockSpec((tm, d), lambda i: (i, 0)),
        compiler_params=_cparams(1, VMEM_LIMIT),
        name="out_proj_ln",
    )(oa, ob, w, w, x, g.reshape(1, d), b.reshape(1, d))


def _router_kernel(x_ref, w_ref, b_ref, route_ref, count_ref):
    route, counts = _route(x_ref[...], w_ref[...], b_ref[...])
    route_ref[...] = route
    count_ref[...] = jnp.broadcast_to(counts, count_ref.shape)


def _router(x, w_router, b_router, tm):
    s, d = x.shape
    wr = jnp.zeros((d, LANES), F32).at[:, :N_EXPERTS].set(w_router)
    br = jnp.zeros((1, LANES), F32).at[0, :N_EXPERTS].set(b_router)
    return pl.pallas_call(
        _router_kernel,
        out_shape=(jax.ShapeDtypeStruct((s, LANES), F32),
                   jax.ShapeDtypeStruct((s // tm, SUBLANES, LANES), F32)),
        grid=(s // tm,),
        in_specs=[pl.BlockSpec((tm, d), lambda i: (i, 0)),
                  pl.BlockSpec((d, LANES), lambda i: (0, 0)),
                  pl.BlockSpec((1, LANES), lambda i: (0, 0))],
        out_specs=(pl.BlockSpec((tm, LANES), lambda i: (i, 0)),
                   pl.BlockSpec((None, SUBLANES, LANES), lambda i: (i, 0, 0))),
        compiler_params=_cparams(1, VMEM_LIMIT),
        name="router",
    )(x, wr, br)


def _decay_cumsum_kernel(f_ref, b_ref, c_ref):
    nh, s = f_ref.shape
    rows = lax.broadcasted_iota(jnp.int32, (LANES, LANES), 0)
    cols = lax.broadcasted_iota(jnp.int32, (LANES, LANES), 1)
    upper = jnp.where(rows <= cols, 1.0, 0.0).astype(F32)
    bias = b_ref[...]

    def body(n, carry):
        off = pl.multiple_of(n * LANES, LANES)
        z = f_ref[:, pl.ds(off, LANES)] + bias
        logf = jnp.minimum(z, 0.0) - jnp.log1p(jnp.exp(-jnp.abs(z)))
        c = jnp.dot(logf, upper, preferred_element_type=F32,
                    precision=lax.Precision.HIGHEST) + carry
        c_ref[:, pl.ds(off, LANES)] = c * LOG2E
        return carry + jnp.sum(logf, axis=-1, keepdims=True)

    lax.fori_loop(0, s // LANES, body, jnp.zeros((nh, 1), F32))


def _decay_cumsum(f_t, bias):
    nh, s = f_t.shape
    return pl.pallas_call(
        _decay_cumsum_kernel,
        out_shape=jax.ShapeDtypeStruct((nh, s), F32),
        name="decay_cumsum",
    )(f_t, bias.reshape(nh, 1))


def _t5_bucket_np(dist):
    n = np.maximum(dist, 0)
    nf = np.maximum(n, 1).astype(np.float32)
    ratio = np.log(nf / np.float32(MAX_EXACT)) / np.float32(math.log(MAX_DISTANCE / MAX_EXACT))
    large = MAX_EXACT + (ratio.astype(np.float32) * np.float32(N_BUCKETS - MAX_EXACT)).astype(np.int32)
    large = np.minimum(large, N_BUCKETS - 1)
    return np.where(n < MAX_EXACT, n, large).astype(np.int32)


def _near_tile_count(t):
    d = np.arange(0, 4 * MAX_DISTANCE, dtype=np.int64)
    not_last = np.nonzero(_t5_bucket_np(d) != N_BUCKETS - 1)[0]
    d_sat = int(not_last.max()) + 1
    n = 1
    while (n - 1) * t + 1 < d_sat:
        n += 1
    return n


def _bucket_tiles_np(t, n_near):
    r = np.arange(t)[:, None]
    c = np.arange(t)[None, :]
    tiles = []
    for delta in range(n_near):
        dist = delta * t + r - c
        tiles.append(np.where(dist >= 0, _t5_bucket_np(dist), -1))
    tiles.append(np.full((t, t), N_BUCKETS - 1))
    tiles.append(np.full((t, t), -1))
    return np.stack(tiles).astype(np.int32)


def _bias_tiles_kernel(tab_ref, idx_ref, o_ref):
    h = pl.program_id(0)
    idx = idx_ref[...]
    out = jnp.full(idx.shape, NEG_INF, F32)
    for b in range(N_BUCKETS):
        out = jnp.where(idx == b, tab_ref[b, h] * LOG2E, out)
    o_ref[...] = out


def _bias_tiles(rel_bias, t, n_near):
    nh = rel_bias.shape[1]
    idx = jnp.asarray(_bucket_tiles_np(t, n_near))
    n_tiles = idx.shape[0]
    return pl.pallas_call(
        _bias_tiles_kernel,
        out_shape=jax.ShapeDtypeStruct((nh, n_tiles, t, t), F32),
        grid=(nh, n_tiles),
        in_specs=[pl.BlockSpec(memory_space=pltpu.SMEM),
                  pl.BlockSpec((None, t, t), lambda h, n: (n, 0, 0))],
        out_specs=pl.BlockSpec((None, None, t, t), lambda h, n: (h, n, 0, 0)),
        compiler_params=_cparams(2),
        name="bias_tiles",
    )(rel_bias, idx)


def _online_softmax_step(s, v, carry):
    m, l, acc = carry
    m_new = jnp.maximum(m, jnp.max(s, axis=-1, keepdims=True))
    alpha = jnp.exp2(m - m_new)
    p = jnp.exp2(s - m_new)
    l = alpha * l + jnp.sum(p, axis=-1, keepdims=True)
    acc = alpha * acc + jnp.dot(p.astype(BF16), v, preferred_element_type=F32)
    return m_new, l, acc


def _prescale(q, c):
    return (q.astype(F32) * c).astype(BF16)


def _near_bias(bias_ref, blocks, j, tb, tk, n_near):
    nsub = tk // tb
    rows = []
    for i in blocks:
        tiles = []
        for b in range(nsub):
            delta = i - (j * nsub + b)
            tiles.append(bias_ref[jnp.where(delta < 0, n_near + 1, jnp.minimum(delta, n_near))])
        rows.append(jnp.concatenate(tiles, axis=1))
    return rows[0] if len(rows) == 1 else jnp.concatenate(rows, axis=0)


def _biased_sweep(i, t, tk, scores, process, init):
    jd = (i * t) // tk
    carry = process(scores(jd), jd, init, True)
    lo = jnp.maximum(jd - 1, 0)
    carry = lax.fori_loop(lo, jd, lambda j, c: process(scores(j), j, c, True), carry)
    return lax.fori_loop(0, lo, lambda j, c: process(scores(j), j, c, False), carry)


def _softmax_init(rows):
    return (jnp.full((rows, 1), NEG_INF, F32), jnp.zeros((rows, 1), F32),
            jnp.zeros((rows, HEAD_DIM), F32))


def _fox_kernel(cend_ref, cstart_ref, q_ref, k_ref, v_ref, c_ref, ccol_ref, o_ref, knorm_ref,
                *, t, tk, scale):
    h = pl.program_id(0)
    i = pl.program_id(1)

    @pl.when(i == 0)
    def _():
        knorm_ref[...] = jnp.broadcast_to(_max_key_norm(k_ref, tk, 0, HEAD_DIM), knorm_ref.shape)

    q = _prescale(q_ref[...], scale * LOG2E)
    rows = lax.broadcasted_iota(jnp.int32, (t, tk), 0)
    cols = lax.broadcasted_iota(jnp.int32, (t, tk), 1)

    def scores(j):
        off = pl.multiple_of(j * tk, tk)
        return _nt_dot(q, k_ref[pl.ds(off, tk), :]) - c_ref[:, pl.ds(off, tk)]

    def values(j):
        return v_ref[pl.ds(pl.multiple_of(j * tk, tk), tk), :]

    def process(s, j, carry):
        return _online_softmax_step(s, values(j), carry)

    jd = (i * t) // tk
    s_diag = jnp.where(cols <= rows + (i * t - jd * tk), scores(jd), NEG_INF)

    qf = q.astype(F32)
    qnorm = jnp.sqrt(jnp.sum(qf * qf, axis=-1, keepdims=True))
    bound = qnorm * knorm_ref[0:1, 0:1] * (1.0 + 2.0 ** -9) + 2.0 ** -6 - ccol_ref[...]
    slack = jnp.max(bound - jnp.max(s_diag, axis=-1, keepdims=True))

    def shifted_sweep(_):
        def step(s, j, carry):
            l, acc = carry
            p = jnp.exp2(s - bound)
            return (l + jnp.sum(p, axis=-1, keepdims=True),
                    acc + jnp.dot(p.astype(BF16), values(j), preferred_element_type=F32))

        c_first_row = cstart_ref[h, i]
        first_live = jnp.int32(0)
        for j in range(cend_ref.shape[1]):
            dead = jnp.logical_and(j < jd, c_first_row - cend_ref[h, j] < FLUSH_EXPONENT)
            first_live = first_live + dead.astype(jnp.int32)
        carry = step(s_diag, jd, _softmax_init(t)[1:])
        return _paired_loop(jd - first_live, lambda n, c: step(scores(first_live + n), first_live + n, c),
                            carry)

    def running_max_sweep(_):
        carry = _online_softmax_step(s_diag, values(jd), _softmax_init(t))
        return lax.fori_loop(0, jd, lambda j, c: process(scores(j), j, c), carry)[1:]

    l, acc = lax.cond(slack <= SHIFT_SLACK, shifted_sweep, running_max_sweep, 0)
    o_ref[...] = (acc / l).astype(o_ref.dtype)


def _moba_kernel(tab_ref, q_ref, k_ref, v_ref, bias_ref, et_ref, o_ref, kmean_ref, knorm_ref,
                 *, tb, nb, tk, nblk, n_near, scale):
    h = pl.program_id(0)
    g = pl.program_id(1)
    t = nb * tb
    nbp = kmean_ref.shape[0]

    @pl.when(g == 0)
    def _():
        kmean_ref[...] = jnp.zeros(kmean_ref.shape, F32)

        def mean_body(n, _):
            off = pl.multiple_of(n * tb, tb)
            kb = k_ref[pl.ds(off, tb), :].astype(F32)
            kmean_ref[pl.ds(n, 1), :] = jnp.sum(kb, axis=0, keepdims=True) * (1.0 / tb)
            return 0

        lax.fori_loop(0, nblk, mean_body, 0)
        knorm_ref[...] = jnp.broadcast_to(_max_key_norm(k_ref, tk, 0, HEAD_DIM), knorm_ref.shape)

    q = q_ref[...]
    km_hi, km_lo = _split_bf16(kmean_ref[...])
    gate = _nt_dot(q, km_hi) + _nt_dot(q, km_lo)
    blk = lax.broadcasted_iota(jnp.int32, (t, nbp), 1)
    own = g * nb + lax.broadcasted_iota(jnp.int32, (t, 1), 0) // tb
    eligible = blk < own
    sel = jnp.zeros((t, nbp), F32)
    for r in range(MOBA_TOPK):
        cand = jnp.where(eligible, jnp.where(sel > 0.0, -jnp.inf, gate), -jnp.inf)
        vmax = jnp.max(cand, axis=-1, keepdims=True)
        first = jnp.min(jnp.where(cand == vmax, blk, nbp), axis=-1, keepdims=True)
        take = jnp.where(r < own, 1.0, 0.0)
        sel = jnp.maximum(sel, jnp.where(blk == first, take, 0.0))

    sel = jnp.maximum(sel, jnp.where(blk == own, 1.0, 0.0))
    penalty = jnp.where(sel > 0.0, 0.0, NEG_INF).astype(BF16)
    q_aug = jnp.concatenate([_prescale(q, scale * LOG2E), penalty], axis=1)
    b_far = tab_ref[N_BUCKETS - 1, h] * LOG2E

    def scores(j):
        off = pl.multiple_of(j * tk, tk)
        k_aug = jnp.concatenate([k_ref[pl.ds(off, tk), :], et_ref[pl.ds(off, tk), :]], axis=1)
        return _nt_dot(q_aug, k_aug)

    def values(j):
        return v_ref[pl.ds(pl.multiple_of(j * tk, tk), tk), :]

    def near_bias(j):
        return _near_bias(bias_ref, [g * nb + b for b in range(nb)], j, tb, tk, n_near)

    def process(s, j, carry, near):
        return _online_softmax_step(s + (near_bias(j) if near else b_far), values(j), carry)

    qf = q_aug[:, :HEAD_DIM].astype(F32)
    qnorm = jnp.sqrt(jnp.sum(qf * qf, axis=-1, keepdims=True))
    bound = _logit_bound(qnorm, knorm_ref[0:1, 0:1], tab_ref, h)
    l, acc = _shifted_or_running_sweep(g, t, tk, scores, near_bias, values, process, bound, b_far, t)
    o_ref[...] = (acc / l).astype(o_ref.dtype)


def _paired_loop(n, body, carry):
    carry = lax.fori_loop(0, n // 2, lambda p, c: body(2 * p + 1, body(2 * p, c)), carry)
    return lax.fori_loop(0, n % 2, lambda _, c: body(n - 1, c), carry)


def _logit_bound(qnorm, knorm, tab_ref, h):
    b_max = tab_ref[0, h]
    for b in range(1, N_BUCKETS):
        b_max = jnp.maximum(b_max, tab_ref[b, h])
    return qnorm * knorm * (1.0 + 2.0 ** -9) + (b_max * LOG2E + 2.0 ** -6)


def _shifted_or_running_sweep(i, t, tk, scores, near_bias, values, process, bound, b_far, rows):
    jd = (i * t) // tk
    s_diag = scores(jd) + near_bias(jd)
    slack = jnp.max(bound - jnp.max(s_diag, axis=-1, keepdims=True))

    def shifted_sweep(_):
        def step(s, j, carry, shift):
            l, acc = carry
            p = jnp.exp2(s - shift)
            return (l + jnp.sum(p, axis=-1, keepdims=True),
                    acc + jnp.dot(p.astype(BF16), values(j), preferred_element_type=F32))

        carry = step(s_diag, jd, _softmax_init(rows)[1:], bound)
        lo = jnp.maximum(jd - 1, 0)
        carry = lax.fori_loop(lo, jd, lambda j, c: step(scores(j) + near_bias(j), j, c, bound), carry)
        far_shift = bound - b_far
        return _paired_loop(lo, lambda j, c: step(scores(j), j, c, far_shift), carry)

    def running_max_sweep(_):
        return _biased_sweep(i, t, tk, scores, process, _softmax_init(rows))[1:]

    return lax.cond(slack <= SHIFT_SLACK, shifted_sweep, running_max_sweep, 0)


def _max_key_norm(k_ref, tk, lane_lo, lane_hi):
    klane = lax.broadcasted_iota(jnp.int32, (tk, HEAD_DIM), 1)
    keep = jnp.logical_and(klane >= lane_lo, klane < lane_hi)

    def body(n, c):
        kt = k_ref[pl.ds(pl.multiple_of(n * tk, tk), tk), :].astype(F32)
        sq = jnp.sum(jnp.where(keep, kt * kt, 0.0), axis=-1, keepdims=True)
        return jnp.maximum(c, jnp.max(sq, axis=0, keepdims=True))

    return jnp.sqrt(lax.fori_loop(0, k_ref.shape[0] // tk, body, jnp.zeros((1, 1), F32)))


def _diff_kernel(tab_ref, q_ref, k_ref, v_ref, bias_ref, lq_ref, lk_ref, g_ref, o_ref, knorm_ref,
                 *, tb, nb, tk, n_near, scale, lambda_init):
    h = pl.program_id(0)
    i = pl.program_id(1)
    t = nb * tb
    q = _prescale(q_ref[...], scale * LOG2E)
    lane = lax.broadcasted_iota(jnp.int32, (t, HEAD_DIM), 1)
    zero = jnp.zeros_like(q)
    q2 = jnp.concatenate([jnp.where(lane < DIFF_QK_DIM, q, zero),
                          jnp.where(lane >= DIFF_QK_DIM, q, zero)], axis=0)
    b_far = tab_ref[N_BUCKETS - 1, h] * LOG2E

    @pl.when(i == 0)
    def _():
        knorm_ref[0:1, :] = jnp.broadcast_to(_max_key_norm(k_ref, tk, 0, DIFF_QK_DIM), (1, LANES))
        knorm_ref[1:2, :] = jnp.broadcast_to(_max_key_norm(k_ref, tk, DIFF_QK_DIM, HEAD_DIM), (1, LANES))

    def scores(j):
        return _nt_dot(q2, k_ref[pl.ds(pl.multiple_of(j * tk, tk), tk), :])

    def values(j):
        return v_ref[pl.ds(pl.multiple_of(j * tk, tk), tk), :]

    def near_bias(j):
        b = _near_bias(bias_ref, [i * nb + b for b in range(nb)], j, tb, tk, n_near)
        return jnp.concatenate([b, b], axis=0)

    def process(s, j, carry, near):
        return _online_softmax_step(s + (near_bias(j) if near else b_far), values(j), carry)

    q2f = q2.astype(F32)
    qnorm = jnp.sqrt(jnp.sum(q2f * q2f, axis=-1, keepdims=True))
    knorm = jnp.concatenate([jnp.broadcast_to(knorm_ref[0:1, 0:1], (t, 1)),
                             jnp.broadcast_to(knorm_ref[1:2, 0:1], (t, 1))], axis=0)
    bound = _logit_bound(qnorm, knorm, tab_ref, h)
    l, acc = _shifted_or_running_sweep(i, t, tk, scores, near_bias, values, process, bound, b_far, 2 * t)
    o12 = acc / l
    lam_e = jnp.exp(jnp.sum(lq_ref[...] * lk_ref[...], axis=-1, keepdims=True))
    lam = lam_e[0:1, :] - lam_e[1:2, :] + lambda_init
    o = o12[:t, :] - lam * o12[t:, :]
    o = o * lax.rsqrt(jnp.mean(o * o, axis=-1, keepdims=True) + LN_EPS) * g_ref[...]
    o_ref[...] = (o * (1.0 - lambda_init)).astype(o_ref.dtype)


def _sb_kernel(q_ref, k_ref, v_ref, o_ref, *, t, nq, scale):
    g = pl.program_id(1)

    def later_matrix(n):
        r = lax.broadcasted_iota(jnp.int32, (n, n), 0)
        c = lax.broadcasted_iota(jnp.int32, (n, n), 1)
        return jnp.where(r > c, 1.0, 0.0).astype(BF16)

    def block(q, off, nk, rem, acc, shift):
        k = k_ref[pl.ds(off, nk), :]
        v = v_ref[pl.ds(off, nk), :]
        z = _nt_dot(q, k) * scale
        soft = jnp.log1p(jnp.exp(-jnp.abs(z)))
        log_sig = jnp.minimum(z, 0.0) - soft
        log_rem = jnp.minimum(-z, 0.0) - soft
        if shift is not None:
            rows = lax.broadcasted_iota(jnp.int32, (t, nk), 0)
            cols = lax.broadcasted_iota(jnp.int32, (t, nk), 1)
            mask = cols < rows + shift
            log_rem = jnp.where(mask, log_rem, 0.0)
        hi, lo = _split_bf16(log_rem)
        later = later_matrix(nk)
        after = (jnp.dot(hi, later, preferred_element_type=F32)
                 + jnp.dot(lo, later, preferred_element_type=F32))
        a = jnp.exp(log_sig + after + rem)
        if shift is not None:
            a = jnp.where(mask, a, 0.0)
        acc = acc + jnp.dot(a.astype(BF16), v, preferred_element_type=F32)
        rem = rem + jnp.sum(log_rem, axis=-1, keepdims=True)
        return rem, acc

    def live(rem):
        return (jnp.max(rem) > SB_CUTOFF).astype(jnp.int32)

    state = []
    for u in range(nq):
        i = g * nq + u
        q = q_ref[u * t:(u + 1) * t, :]
        first = jnp.maximum(i - 1, 0)
        rem, acc = block(q, pl.multiple_of(first * t, t), 2 * t, jnp.zeros((t, 1), F32),
                         jnp.zeros((t, HEAD_DIM), F32), (i - first) * t)
        state.append((i, q, rem, acc))

    for u, (i, q, rem, acc) in enumerate(state):
        def cond(c):
            j, go, _, _ = c
            return jnp.logical_and(j >= 0, go > 0)

        def body(c, q=q):
            j, _, rem, acc = c
            rem, acc = block(q, pl.multiple_of(j * t, t), t, rem, acc, None)
            return j - 1, live(rem), rem, acc

        _, _, _, acc = lax.while_loop(cond, body, (i - 2, live(rem), rem, acc))
        o_ref[u * t:(u + 1) * t, :] = acc.astype(o_ref.dtype)


def _attention_call(kernel, h, out_cols, col0, extra_in, extra_specs, scratch, name, t):
    s = h.shape[0]
    nh = N_HEADS_GROUP
    in_specs = list(extra_specs[0]) + [
        pl.BlockSpec((t, HEAD_DIM), lambda hd, i: (i, col0 + hd)),
        pl.BlockSpec((s, HEAD_DIM), lambda hd, i: (0, col0 + nh + hd)),
        pl.BlockSpec((s, HEAD_DIM), lambda hd, i: (0, col0 + 2 * nh + hd)),
    ] + list(extra_specs[1])
    args = list(extra_in[0]) + [h, h, h] + list(extra_in[1])
    return pl.pallas_call(
        kernel,
        out_shape=jax.ShapeDtypeStruct((s, out_cols), BF16),
        grid=(nh, s // t),
        in_specs=in_specs,
        out_specs=pl.BlockSpec((t, HEAD_DIM), lambda hd, i: (i, hd)),
        scratch_shapes=scratch,
        compiler_params=_cparams(2, VMEM_LIMIT),
        name=name,
    )(*args)


def _route(x, w, b):
    logits = jnp.dot(x, w, preferred_element_type=F32, precision=lax.Precision.HIGHEST) + b
    tm, n = logits.shape
    lane = lax.broadcasted_iota(jnp.int32, (tm, n), 1)
    real = lane < N_EXPERTS
    logits = jnp.where(real, logits, -jnp.inf)
    e = jnp.exp(logits - jnp.max(logits, axis=-1, keepdims=True))
    aff = e / jnp.sum(e, axis=-1, keepdims=True)
    group = lane // EXPERTS_PER_GROUP
    best = jnp.full((tm, 1), -jnp.inf, F32)
    g_sel = jnp.zeros((tm, 1), jnp.int32)
    for g in range(N_GROUPS):
        gmax = jnp.max(jnp.where(group == g, aff, -jnp.inf), axis=-1, keepdims=True)
        better = gmax > best
        g_sel = jnp.where(better, g, g_sel)
        best = jnp.where(better, gmax, best)
    in_group = jnp.logical_and(group == g_sel, real)
    cand = jnp.where(in_group, aff, -jnp.inf)
    v1 = jnp.max(cand, axis=-1, keepdims=True)
    i1 = jnp.min(jnp.where(cand == v1, lane, n), axis=-1, keepdims=True)
    cand2 = jnp.where(lane == i1, -jnp.inf, cand)
    v2 = jnp.max(cand2, axis=-1, keepdims=True)
    i2 = jnp.min(jnp.where(cand2 == v2, lane, n), axis=-1, keepdims=True)
    tot = v1 + v2
    chosen = jnp.where(jnp.logical_or(lane == i1, lane == i2), 1.0, 0.0)
    r = lax.broadcasted_iota(jnp.int32, (tm, tm), 0)
    c = lax.broadcasted_iota(jnp.int32, (tm, tm), 1)
    earlier = jnp.where(c < r, 1.0, 0.0).astype(BF16)
    before = jnp.dot(earlier, chosen.astype(BF16), preferred_element_type=F32)
    rank1 = jnp.sum(jnp.where(lane == i1, before, 0.0), axis=-1, keepdims=True)
    rank2 = jnp.sum(jnp.where(lane == i2, before, 0.0), axis=-1, keepdims=True)
    route = jnp.where(lane == 0, v1 / tot,
                      jnp.where(lane == 1, v2 / tot,
                                jnp.where(lane == 2, i1.astype(F32),
                                          jnp.where(lane == 3, i2.astype(F32),
                                                    jnp.where(lane == 4, rank1,
                                                              jnp.where(lane == 5, rank2, 0.0))))))
    return route, jnp.sum(chosen, axis=0, keepdims=True)


def _dispatch_plan(route, counts, tm):
    s = route.shape[0]
    nt = s // tm
    n_tiles = (MOE_TOP_K * s) // tm + N_EXPERTS
    ids = jnp.arange(N_EXPERTS, dtype=jnp.int32)
    expert = route[:, 2:2 + MOE_TOP_K].astype(jnp.int32).reshape(nt, tm, MOE_TOP_K)
    rank = route[:, 4:4 + MOE_TOP_K].astype(jnp.int32).reshape(nt, tm, MOE_TOP_K)
    cnt = counts[:, 0, :N_EXPERTS].astype(jnp.int32)
    tile_base = jnp.cumsum(cnt, axis=0) - cnt
    padded = -(-jnp.sum(cnt, axis=0) // tm) * tm
    seg_end = jnp.cumsum(padded)
    base = (seg_end - padded)[None, :] + tile_base
    pick = expert[..., None] == ids
    pair_row = (jnp.sum(jnp.where(pick, base[:, None, None, :], 0), axis=-1) + rank).reshape(s, MOE_TOP_K)
    tile_start = jnp.arange(n_tiles, dtype=jnp.int32) * tm
    tile_expert = jnp.minimum(jnp.sum((tile_start[:, None] >= seg_end[None, :]).astype(jnp.int32), axis=1),
                              N_EXPERTS - 1).astype(jnp.int32)
    tile_live = (tile_start < seg_end[-1]).astype(jnp.int32)
    row_token = (jnp.arange(n_tiles * tm, dtype=jnp.int32) % s).at[pair_row.reshape(-1)].set(
        jnp.arange(MOE_TOP_K * s, dtype=jnp.int32) // MOE_TOP_K)
    return pair_row, row_token, tile_expert, tile_live


def _moe_group_kernel(te_ref, live_ref, x_ref, wg_ref, wu_ref, wd_ref, o_ref, xb_ref):
    g = pl.program_id(0)
    f = pl.program_id(1)
    live = live_ref[g] > 0

    @pl.when(jnp.logical_and(live, f == 0))
    def _():
        xb_ref[...] = x_ref[...].astype(BF16)

    @pl.when(jnp.logical_and(jnp.logical_not(live), f == 0))
    def _():
        o_ref[...] = jnp.zeros(o_ref.shape, F32)

    @pl.when(live)
    def _():
        xb = xb_ref[...]
        a = jnp.dot(xb, wg_ref[...], preferred_element_type=F32)
        u = jnp.dot(xb, wu_ref[...], preferred_element_type=F32)
        hid = ((a * jax.nn.sigmoid(a)) * u).astype(BF16)
        y = jnp.dot(hid, wd_ref[...], preferred_element_type=F32)

        @pl.when(f == 0)
        def _():
            o_ref[...] = y

        @pl.when(f > 0)
        def _():
            o_ref[...] += y


def _moe_group_ffn(xs, tile_expert, tile_live, wg, wu, wd, tm, tf):
    rows, d = xs.shape
    fdim = wg.shape[2]
    grid_spec = pltpu.PrefetchScalarGridSpec(
        num_scalar_prefetch=2,
        grid=(rows // tm, fdim // tf),
        in_specs=[pl.BlockSpec((tm, d), lambda g, f, te, lv: (g, 0)),
                  pl.BlockSpec((None, d, tf), lambda g, f, te, lv: (te[g], 0, f)),
                  pl.BlockSpec((None, d, tf), lambda g, f, te, lv: (te[g], 0, f)),
                  pl.BlockSpec((None, tf, d), lambda g, f, te, lv: (te[g], f, 0))],
        out_specs=pl.BlockSpec((tm, d), lambda g, f, te, lv: (g, 0)),
        scratch_shapes=[pltpu.VMEM((tm, d), BF16)])
    return pl.pallas_call(
        _moe_group_kernel,
        out_shape=jax.ShapeDtypeStruct((rows, d), F32),
        grid_spec=grid_spec,
        compiler_params=_cparams(2, VMEM_LIMIT),
        name="moe_group_ffn",
    )(tile_expert, tile_live, xs, wg, wu, wd)


def _moe_combine_ln_kernel(x_ref, r_ref, y0_ref, y1_ref, g_ref, b_ref, o_ref):
    r = r_ref[...]
    ff = r[:, 0:1] * y0_ref[...] + r[:, 1:2] * y1_ref[...]
    o_ref[...] = _layer_norm(ALPHA * x_ref[...] + ff, g_ref[...], b_ref[...])


def _moe_combine_ln(x, route, y, g, b, tm):
    s, d = x.shape
    nt = s // tm
    return pl.pallas_call(
        _moe_combine_ln_kernel,
        out_shape=jax.ShapeDtypeStruct((s, d), F32),
        grid=(nt,),
        in_specs=[pl.BlockSpec((tm, d), lambda i: (i, 0)),
                  pl.BlockSpec((tm, LANES), lambda i: (i, 0)),
                  pl.BlockSpec((tm, d), lambda i: (i, 0)),
                  pl.BlockSpec((tm, d), lambda i: (i + nt, 0)),
                  pl.BlockSpec((1, d), lambda i: (0, 0)),
                  pl.BlockSpec((1, d), lambda i: (0, 0))],
        out_specs=pl.BlockSpec((tm, d), lambda i: (i, 0)),
        compiler_params=_cparams(1, VMEM_LIMIT),
        name="moe_combine_ln",
    )(x, route, y, y, g.reshape(1, d), b.reshape(1, d))


def _sc_row_gather(src, idx):
    n = idx.shape[0]
    d = src.shape[1]
    workers = SC_CORES * SC_SUBCORES
    per_worker = n // workers
    assert n % (workers * SC_GATHER_ROWS) == 0
    mesh = plsc.VectorSubcoreMesh(core_axis_name="c", subcore_axis_name="s")

    ch = SC_GATHER_ROWS
    n_chunks = per_worker // ch
    assert n_chunks % 2 == 0

    @functools.partial(
        pl.kernel, mesh=mesh, out_type=jax.ShapeDtypeStruct((n, d), src.dtype),
        scratch_types=[pltpu.VMEM((per_worker,), jnp.int32),
                       pltpu.VMEM((ch, d), src.dtype), pltpu.VMEM((ch, d), src.dtype),
                       pltpu.SemaphoreType.DMA, pltpu.SemaphoreType.DMA],
        name="sc_row_gather")
    def gather(src_hbm, idx_hbm, out_hbm, idx_v, rows_a, rows_b, sem_a, sem_b):
        base = (lax.axis_index("s") * SC_CORES + lax.axis_index("c")) * per_worker
        pltpu.sync_copy(idx_hbm.at[pl.ds(base, per_worker)], idx_v)

        def start(c, rows, sem):
            pltpu.async_copy(src_hbm.at[idx_v.at[pl.ds(c * ch, ch)]], rows, sem)

        def finish(c, rows, sem):
            pltpu.make_async_copy(src_hbm.at[pl.ds(0, ch)], rows, sem).wait()
            pltpu.sync_copy(rows, out_hbm.at[pl.ds(base + c * ch, ch)])

        start(0, rows_a, sem_a)

        @pl.loop(0, n_chunks, step=2)
        def _(c):
            start(c + 1, rows_b, sem_b)
            finish(c, rows_a, sem_a)

            @pl.when(c + 2 < n_chunks)
            def _():
                start(c + 2, rows_a, sem_a)

            finish(c + 1, rows_b, sem_b)

    return gather(src, idx)


def _moe_ln(x, route, counts, wg, wu, wd, g, b, tm):
    pair_row, row_token, tile_expert, tile_live = _dispatch_plan(route, counts, tm)
    xs = _sc_row_gather(x, row_token)
    ys = _moe_group_ffn(xs, tile_expert, tile_live, wg, wu, wd, tm, D_EXPERT // 2)
    slot_major = pair_row.T.reshape(-1)
    y = _sc_row_gather(ys, slot_major)
    return _moe_combine_ln(x, route, y, g, b, tm)


def kernel(x, w_in_even, w_out_even, forget_bias, w_in_odd, w_out_odd, lambda_q, lambda_k,
           subln_gain, rel_bias, w_router, b_router, w_gate, w_up, w_down, ln_gain, ln_bias):
    bsz, s, d = x.shape
    assert bsz == 1 and d == D_MODEL and s % MOBA_BLOCK == 0
    t = ATT_TILE
    nh = N_HEADS_GROUP
    gw = GROUP_WIDTH
    tm_proj = min(1024, s)
    tm_row = min(512, s)
    tk = min(ATT_KEYS, s)
    n_near = _near_tile_count(t)
    assert s % tk == 0 and tk // t + 1 >= n_near
    nblk = s // MOBA_BLOCK
    nbp = -(-nblk // LANES) * LANES

    x2 = x.reshape(s, d)
    bias_tiles = _bias_tiles(rel_bias, t, n_near)
    tab_spec = pl.BlockSpec(memory_space=pltpu.SMEM)
    bias_spec = pl.BlockSpec((None, n_near + 2, t, t), lambda hd, i: (hd, 0, 0, 0))
    block_onehot = (jnp.arange(s, dtype=jnp.int32)[:, None] // MOBA_BLOCK
                    == jnp.arange(nbp, dtype=jnp.int32)[None, :]).astype(BF16)

    for layer in range(DEPTH):
        li = layer // 2
        if layer % 2 == 0:
            w = w_in_even[li]
            fcol = 3 * gw
            w_main = jnp.concatenate([w[:, :fcol], w[:, fcol + nh:]], axis=1).astype(BF16)
            w_f = jnp.zeros((d, LANES), BF16).at[:, :nh].set(w[:, fcol:fcol + nh].astype(BF16))
            h = _in_proj(x2, w_main, BF16, tm_proj, 512)
            f = _in_proj(x2, w_f, F32, tm_proj, LANES)
            c = _decay_cumsum(f[:, :nh].T, forget_bias[li])
            oa = _attention_call(
                functools.partial(_fox_kernel, t=FOX_TQ, tk=min(FOX_TK, s), scale=HEAD_DIM ** -0.5), h, gw, 0,
                ([c[:, min(FOX_TK, s) - 1::min(FOX_TK, s)], c[:, ::FOX_TQ]],
                 [c.reshape(nh, 1, s), c.reshape(nh, s, 1)]),
                ([tab_spec, tab_spec],
                 [pl.BlockSpec((None, 1, s), lambda hd, i: (hd, 0, 0)),
                  pl.BlockSpec((None, FOX_TQ, 1), lambda hd, i: (hd, i, 0))]),
                [pltpu.VMEM((SUBLANES, LANES), F32)],
                "fox_attention", FOX_TQ)
            ob = _attention_call(
                functools.partial(_moba_kernel, tb=t, nb=MOBA_Q_BLOCKS, tk=tk, nblk=nblk, n_near=n_near,
                                  scale=HEAD_DIM ** -0.5),
                h, gw, 3 * nh,
                ([rel_bias], [bias_tiles, block_onehot]),
                ([tab_spec], [bias_spec, pl.BlockSpec((s, nbp), lambda hd, i: (0, 0))]),
                [pltpu.VMEM((nbp, HEAD_DIM), F32), pltpu.VMEM((SUBLANES, LANES), F32)],
                "moba_attention", MOBA_Q_BLOCKS * t)
            w_out = w_out_even[li]
        else:
            lambda_init = 0.8 - 0.6 * math.exp(-0.3 * layer)
            h = _in_proj(x2, w_in_odd[li].astype(BF16), BF16, tm_proj, 512)
            small = lambda shape: pl.BlockSpec(shape, lambda hd, i: (0, 0))
            oa = _attention_call(
                functools.partial(_diff_kernel, tb=t, nb=DIFF_Q_BLOCKS, tk=tk, n_near=n_near,
                                  scale=DIFF_QK_DIM ** -0.5, lambda_init=lambda_init),
                h, gw, 0,
                ([rel_bias], [bias_tiles, lambda_q[li], lambda_k[li], subln_gain[li].reshape(1, HEAD_DIM)]),
                ([tab_spec], [bias_spec, small((2, DIFF_QK_DIM)), small((2, DIFF_QK_DIM)),
                              small((1, HEAD_DIM))]),
                [pltpu.VMEM((SUBLANES, LANES), F32)],
                "diff_attention", DIFF_Q_BLOCKS * t)
            ob = _attention_call(
                functools.partial(_sb_kernel, t=t, nq=SB_Q_TILES, scale=HEAD_DIM ** -0.5), h, gw, 3 * nh,
                ([], []), ([], []), [], "stickbreak_attention", SB_Q_TILES * t)
            w_out = w_out_odd[li]
        x2 = _out_proj_ln(oa, ob, w_out.astype(BF16), x2, ln_gain[layer, 0], ln_bias[layer, 0], tm_row)
        route, counts = _router(x2, w_router, b_router, tm_row)
        x2 = _moe_ln(x2, route, counts, w_gate[layer].astype(BF16), w_up[layer].astype(BF16),
                     w_down[layer].astype(BF16), ln_gain[layer, 1], ln_bias[layer, 1], tm_row)
    return x2.reshape(bsz, s, d)
```

```python
import functools
import math

import numpy as np
import jax
import jax.numpy as jnp
from jax import lax
from jax.experimental import pallas as pl
from jax.experimental.pallas import tpu as pltpu
from jax.experimental.pallas import tpu_sc as plsc

F32 = jnp.float32
BF16 = jnp.bfloat16

D_MODEL = 2048
DEPTH = 2
HEAD_DIM = 128
N_HEADS_GROUP = 8
GROUP_WIDTH = N_HEADS_GROUP * HEAD_DIM
DIFF_QK_DIM = HEAD_DIM // 2
MOBA_BLOCK = 256
MOBA_TOPK = 3
N_BUCKETS = 32
MAX_EXACT = N_BUCKETS // 2
MAX_DISTANCE = 1024
N_EXPERTS = 16
N_GROUPS = 4
EXPERTS_PER_GROUP = N_EXPERTS // N_GROUPS
MOE_TOP_K = 2
SC_CORES = 2
SC_SUBCORES = 16
SC_GATHER_ROWS = 32
SUBLANES = 8
D_EXPERT = D_MODEL // 2
ALPHA = (2.0 * DEPTH) ** 0.25
LN_EPS = 1e-5
NEG_INF = -1e30
LOG2E = math.log2(math.e)
LANES = 128
ATT_TILE = 256
ATT_KEYS = 1024
FOX_TQ = 1024
FOX_TK = 1024
MOBA_Q_BLOCKS = 4
SB_Q_TILES = 2
DIFF_Q_BLOCKS = 4
SHIFT_SLACK = 96.0
FLUSH_EXPONENT = -130.0
SB_CUTOFF = -104.0
VMEM_LIMIT = 56 * 1024 * 1024


def _cparams(n_axes, vmem=None):
    return pltpu.CompilerParams(dimension_semantics=("arbitrary",) * n_axes,
                                vmem_limit_bytes=vmem)


def _nt_dot(a, b):
    return lax.dot_general(a, b, (((1,), (1,)), ((), ())), preferred_element_type=F32)


def _split_bf16(x):
    hi = x.astype(BF16)
    lo = (x - hi.astype(F32)).astype(BF16)
    return hi, lo


def _pack_bf16_pairs(x):
    n = x.shape[1] // 2
    lo = pltpu.bitcast(x[:, :n].astype(BF16).astype(F32), jnp.uint32)
    hi = pltpu.bitcast(x[:, n:].astype(BF16).astype(F32), jnp.uint32)
    return jnp.bitwise_or(lax.shift_right_logical(lo, jnp.uint32(16)),
                          jnp.bitwise_and(hi, jnp.uint32(0xFFFF0000)))


def _unpack_bf16_pairs(w):
    lo = pltpu.bitcast(lax.shift_left(w, jnp.uint32(16)), F32)
    hi = pltpu.bitcast(jnp.bitwise_and(w, jnp.uint32(0xFFFF0000)), F32)
    return lo, hi


def _layer_norm(z, g, b):
    mu = jnp.mean(z, axis=-1, keepdims=True)
    zc = z - mu
    var = jnp.mean(zc * zc, axis=-1, keepdims=True)
    return zc * lax.rsqrt(var + LN_EPS) * g + b


def _inproj_kernel(x_ref, w_ref, o_ref, xb_ref):
    @pl.when(pl.program_id(1) == 0)
    def _():
        xb_ref[...] = x_ref[...].astype(BF16)

    o_ref[...] = jnp.dot(xb_ref[...], w_ref[...], preferred_element_type=F32).astype(o_ref.dtype)


def _in_proj(x, w, out_dtype, tm, tn):
    s, d = x.shape
    n = w.shape[1]
    return pl.pallas_call(
        _inproj_kernel,
        out_shape=jax.ShapeDtypeStruct((s, n), out_dtype),
        grid=(s // tm, n // tn),
        in_specs=[pl.BlockSpec((tm, d), lambda i, j: (i, 0)),
                  pl.BlockSpec((d, tn), lambda i, j: (0, j))],
        out_specs=pl.BlockSpec((tm, tn), lambda i, j: (i, j)),
        scratch_shapes=[pltpu.VMEM((tm, d), BF16)],
        compiler_params=_cparams(2, VMEM_LIMIT),
        name="in_proj",
    )(x, w)


def _outproj_ln_kernel(oa_ref, ob_ref, wa_ref, wb_ref, x_ref, g_ref, b_ref, o_ref, packed_ref):
    y = jnp.dot(oa_ref[...], wa_ref[...], preferred_element_type=F32)
    y = y + jnp.dot(ob_ref[...], wb_ref[...], preferred_element_type=F32)
    x1 = _layer_norm(ALPHA * x_ref[...] + y, g_ref[...], b_ref[...])
    o_ref[...] = x1
    packed_ref[...] = _pack_bf16_pairs(x1)


def _out_proj_ln(oa, ob, w, x, g, b, tm):
    s, d = x.shape
    gw = oa.shape[1]
    return pl.pallas_call(
        _outproj_ln_kernel,
        out_shape=(jax.ShapeDtypeStruct((s, d), F32), jax.ShapeDtypeStruct((s, d // 2), jnp.uint32)),
        grid=(s // tm,),
        in_specs=[pl.BlockSpec((tm, gw), lambda i: (i, 0)),
                  pl.BlockSpec((tm, gw), lambda i: (i, 0)),
                  pl.BlockSpec((gw, d), lambda i: (0, 0)),
                  pl.BlockSpec((gw, d), lambda i: (1, 0)),
                  pl.BlockSpec((tm, d), lambda i: (i, 0)),
                  pl.BlockSpec((1, d), lambda i: (0, 0)),
                  pl.BlockSpec((1, d), lambda i: (0, 0))],
        out_specs=(pl.BlockSpec((tm, d), lambda i: (i, 0)),
                   pl.BlockSpec((tm, d // 2), lambda i: (i, 0))),
        compiler_params=_cparams(1, VMEM_LIMIT),
        name="out_proj_ln",
    )(oa, ob, w, w, x, g.reshape(1, d), b.reshape(1, d))


def _router_kernel(x_ref, w_ref, b_ref, route_ref, count_ref):
    route, counts = _route(x_ref[...], w_ref[...], b_ref[...])
    route_ref[...] = route
    count_ref[...] = jnp.broadcast_to(counts, count_ref.shape)


def _router(x, w_router, b_router, tm):
    s, d = x.shape
    wr = jnp.zeros((d, LANES), F32).at[:, :N_EXPERTS].set(w_router)
    br = jnp.zeros((1, LANES), F32).at[0, :N_EXPERTS].set(b_router)
    return pl.pallas_call(
        _router_kernel,
        out_shape=(jax.ShapeDtypeStruct((s, LANES), F32),
                   jax.ShapeDtypeStruct((s // tm, SUBLANES, LANES), F32)),
        grid=(s // tm,),
        in_specs=[pl.BlockSpec((tm, d), lambda i: (i, 0)),
                  pl.BlockSpec((d, LANES), lambda i: (0, 0)),
                  pl.BlockSpec((1, LANES), lambda i: (0, 0))],
        out_specs=(pl.BlockSpec((tm, LANES), lambda i: (i, 0)),
                   pl.BlockSpec((None, SUBLANES, LANES), lambda i: (i, 0, 0))),
        compiler_params=_cparams(1, VMEM_LIMIT),
        name="router",
    )(x, wr, br)


def _decay_cumsum_kernel(f_ref, b_ref, c_ref):
    nh, s = f_ref.shape
    rows = lax.broadcasted_iota(jnp.int32, (LANES, LANES), 0)
    cols = lax.broadcasted_iota(jnp.int32, (LANES, LANES), 1)
    upper = jnp.where(rows <= cols, 1.0, 0.0).astype(F32)
    bias = b_ref[...]

    def body(n, carry):
        off = pl.multiple_of(n * LANES, LANES)
        z = f_ref[:, pl.ds(off, LANES)] + bias
        logf = jnp.minimum(z, 0.0) - jnp.log1p(jnp.exp(-jnp.abs(z)))
        c = jnp.dot(logf, upper, preferred_element_type=F32,
                    precision=lax.Precision.HIGHEST) + carry
        c_ref[:, pl.ds(off, LANES)] = c * LOG2E
        return carry + jnp.sum(logf, axis=-1, keepdims=True)

    lax.fori_loop(0, s // LANES, body, jnp.zeros((nh, 1), F32))


def _decay_cumsum(f_t, bias):
    nh, s = f_t.shape
    return pl.pallas_call(
        _decay_cumsum_kernel,
        out_shape=jax.ShapeDtypeStruct((nh, s), F32),
        name="decay_cumsum",
    )(f_t, bias.reshape(nh, 1))


def _t5_bucket_np(dist):
    n = np.maximum(dist, 0)
    nf = np.maximum(n, 1).astype(np.float32)
    ratio = np.log(nf / np.float32(MAX_EXACT)) / np.float32(math.log(MAX_DISTANCE / MAX_EXACT))
    large = MAX_EXACT + (ratio.astype(np.float32) * np.float32(N_BUCKETS - MAX_EXACT)).astype(np.int32)
    large = np.minimum(large, N_BUCKETS - 1)
    return np.where(n < MAX_EXACT, n, large).astype(np.int32)


def _near_tile_count(t):
    d = np.arange(0, 4 * MAX_DISTANCE, dtype=np.int64)
    not_last = np.nonzero(_t5_bucket_np(d) != N_BUCKETS - 1)[0]
    d_sat = int(not_last.max()) + 1
    n = 1
    while (n - 1) * t + 1 < d_sat:
        n += 1
    return n


def _bucket_tiles_np(t, n_near):
    r = np.arange(t)[:, None]
    c = np.arange(t)[None, :]
    tiles = []
    for delta in range(n_near):
        dist = delta * t + r - c
        tiles.append(np.where(dist >= 0, _t5_bucket_np(dist), -1))
    tiles.append(np.full((t, t), N_BUCKETS - 1))
    tiles.append(np.full((t, t), -1))
    return np.stack(tiles).astype(np.int32)


def _bias_tiles_kernel(tab_ref, idx_ref, o_ref):
    h = pl.program_id(0)
    idx = idx_ref[...]
    out = jnp.full(idx.shape, NEG_INF, F32)
    for b in range(N_BUCKETS):
        out = jnp.where(idx == b, tab_ref[b, h] * LOG2E, out)
    o_ref[...] = out


def _bias_tiles(rel_bias, t, n_near):
    nh = rel_bias.shape[1]
    idx = jnp.asarray(_bucket_tiles_np(t, n_near))
    n_tiles = idx.shape[0]
    return pl.pallas_call(
        _bias_tiles_kernel,
        out_shape=jax.ShapeDtypeStruct((nh, n_tiles, t, t), F32),
        grid=(nh, n_tiles),
        in_specs=[pl.BlockSpec(memory_space=pltpu.SMEM),
                  pl.BlockSpec((None, t, t), lambda h, n: (n, 0, 0))],
        out_specs=pl.BlockSpec((None, None, t, t), lambda h, n: (h, n, 0, 0)),
        compiler_params=_cparams(2),
        name="bias_tiles",
    )(rel_bias, idx)


def _online_softmax_step(s, v, carry):
    m, l, acc = carry
    m_new = jnp.maximum(m, jnp.max(s, axis=-1, keepdims=True))
    alpha = jnp.exp2(m - m_new)
    p = jnp.exp2(s - m_new)
    l = alpha * l + jnp.sum(p, axis=-1, keepdims=True)
    acc = alpha * acc + jnp.dot(p.astype(BF16), v, preferred_element_type=F32)
    return m_new, l, acc


def _prescale(q, c):
    return (q.astype(F32) * c).astype(BF16)


def _near_bias(bias_ref, blocks, j, tb, tk, n_near):
    nsub = tk // tb
    rows = []
    for i in blocks:
        tiles = []
        for b in range(nsub):
            delta = i - (j * nsub + b)
            tiles.append(bias_ref[jnp.where(delta < 0, n_near + 1, jnp.minimum(delta, n_near))])
        rows.append(jnp.concatenate(tiles, axis=1))
    return rows[0] if len(rows) == 1 else jnp.concatenate(rows, axis=0)


def _biased_sweep(i, t, tk, scores, process, init):
    jd = (i * t) // tk
    carry = process(scores(jd), jd, init, True)
    lo = jnp.maximum(jd - 1, 0)
    carry = lax.fori_loop(lo, jd, lambda j, c: process(scores(j), j, c, True), carry)
    return lax.fori_loop(0, lo, lambda j, c: process(scores(j), j, c, False), carry)


def _softmax_init(rows):
    return (jnp.full((rows, 1), NEG_INF, F32), jnp.zeros((rows, 1), F32),
            jnp.zeros((rows, HEAD_DIM), F32))


def _fox_kernel(cend_ref, cstart_ref, q_ref, k_ref, v_ref, c_ref, ccol_ref, o_ref, knorm_ref,
                *, t, tk, scale):
    h = pl.program_id(0)
    i = pl.program_id(1)

    @pl.when(i == 0)
    def _():
        knorm_ref[...] = jnp.broadcast_to(_max_key_norm(k_ref, tk, 0, HEAD_DIM), knorm_ref.shape)

    q = _prescale(q_ref[...], scale * LOG2E)
    rows = lax.broadcasted_iota(jnp.int32, (t, tk), 0)
    cols = lax.broadcasted_iota(jnp.int32, (t, tk), 1)

    def scores(j):
        off = pl.multiple_of(j * tk, tk)
        return _nt_dot(q, k_ref[pl.ds(off, tk), :]) - c_ref[:, pl.ds(off, tk)]

    def values(j):
        return v_ref[pl.ds(pl.multiple_of(j * tk, tk), tk), :]

    def process(s, j, carry):
        return _online_softmax_step(s, values(j), carry)

    jd = (i * t) // tk
    s_diag = jnp.where(cols <= rows + (i * t - jd * tk), scores(jd), NEG_INF)

    qf = q.astype(F32)
    qnorm = jnp.sqrt(jnp.sum(qf * qf, axis=-1, keepdims=True))
    bound = qnorm * knorm_ref[0:1, 0:1] * (1.0 + 2.0 ** -9) + 2.0 ** -6 - ccol_ref[...]
    slack = jnp.max(bound - jnp.max(s_diag, axis=-1, keepdims=True))

    def shifted_sweep(_):
        def step(s, j, carry):
            l, acc = carry
            p = jnp.exp2(s - bound)
            return (l + jnp.sum(p, axis=-1, keepdims=True),
                    acc + jnp.dot(p.astype(BF16), values(j), preferred_element_type=F32))

        c_first_row = cstart_ref[h, i]
        first_live = jnp.int32(0)
        for j in range(cend_ref.shape[1]):
            dead = jnp.logical_and(j < jd, c_first_row - cend_ref[h, j] < FLUSH_EXPONENT)
            first_live = first_live + dead.astype(jnp.int32)
        carry = step(s_diag, jd, _softmax_init(t)[1:])
        return _paired_loop(jd - first_live, lambda n, c: step(scores(first_live + n), first_live + n, c),
                            carry)

    def running_max_sweep(_):
        carry = _online_softmax_step(s_diag, values(jd), _softmax_init(t))
        return lax.fori_loop(0, jd, lambda j, c: process(scores(j), j, c), carry)[1:]

    l, acc = lax.cond(slack <= SHIFT_SLACK, shifted_sweep, running_max_sweep, 0)
    o_ref[...] = (acc / l).astype(o_ref.dtype)


def _moba_kernel(tab_ref, q_ref, k_ref, v_ref, bias_ref, et_ref, o_ref, kmean_ref, knorm_ref,
                 *, tb, nb, tk, nblk, n_near, scale):
    h = pl.program_id(0)
    g = pl.program_id(1)
    t = nb * tb
    nbp = kmean_ref.shape[0]

    @pl.when(g == 0)
    def _():
        kmean_ref[...] = jnp.zeros(kmean_ref.shape, F32)

        def mean_body(n, _):
            off = pl.multiple_of(n * tb, tb)
            kb = k_ref[pl.ds(off, tb), :].astype(F32)
            kmean_ref[pl.ds(n, 1), :] = jnp.sum(kb, axis=0, keepdims=True) * (1.0 / tb)
            return 0

        lax.fori_loop(0, nblk, mean_body, 0)
        knorm_ref[...] = jnp.broadcast_to(_max_key_norm(k_ref, tk, 0, HEAD_DIM), knorm_ref.shape)

    q = q_ref[...]
    km_hi, km_lo = _split_bf16(kmean_ref[...])
    gate = _nt_dot(q, km_hi) + _nt_dot(q, km_lo)
    blk = lax.broadcasted_iota(jnp.int32, (t, nbp), 1)
    own = g * nb + lax.broadcasted_iota(jnp.int32, (t, 1), 0) // tb
    eligible = blk < own
    sel = jnp.zeros((t, nbp), F32)
    for r in range(MOBA_TOPK):
        cand = jnp.where(eligible, jnp.where(sel > 0.0, -jnp.inf, gate), -jnp.inf)
        vmax = jnp.max(cand, axis=-1, keepdims=True)
        first = jnp.min(jnp.where(cand == vmax, blk, nbp), axis=-1, keepdims=True)
        take = jnp.where(r < own, 1.0, 0.0)
        sel = jnp.maximum(sel, jnp.where(blk == first, take, 0.0))

    sel = jnp.maximum(sel, jnp.where(blk == own, 1.0, 0.0))
    penalty = jnp.where(sel > 0.0, 0.0, NEG_INF).astype(BF16)
    q_aug = jnp.concatenate([_prescale(q, scale * LOG2E), penalty], axis=1)
    b_far = tab_ref[N_BUCKETS - 1, h] * LOG2E

    def scores(j):
        off = pl.multiple_of(j * tk, tk)
        k_aug = jnp.concatenate([k_ref[pl.ds(off, tk), :], et_ref[pl.ds(off, tk), :]], axis=1)
        return _nt_dot(q_aug, k_aug)

    def values(j):
        return v_ref[pl.ds(pl.multiple_of(j * tk, tk), tk), :]

    def near_bias(j):
        return _near_bias(bias_ref, [g * nb + b for b in range(nb)], j, tb, tk, n_near)

    def process(s, j, carry, near):
        return _online_softmax_step(s + (near_bias(j) if near else b_far), values(j), carry)

    qf = q_aug[:, :HEAD_DIM].astype(F32)
    qnorm = jnp.sqrt(jnp.sum(qf * qf, axis=-1, keepdims=True))
    bound = _logit_bound(qnorm, knorm_ref[0:1, 0:1], tab_ref, h)
    l, acc = _shifted_or_running_sweep(g, t, tk, scores, near_bias, values, process, bound, b_far, t)
    o_ref[...] = (acc / l).astype(o_ref.dtype)


def _paired_loop(n, body, carry):
    carry = lax.fori_loop(0, n // 2, lambda p, c: body(2 * p + 1, body(2 * p, c)), carry)
    return lax.fori_loop(0, n % 2, lambda _, c: body(n - 1, c), carry)


def _logit_bound(qnorm, knorm, tab_ref, h):
    b_max = tab_ref[0, h]
    for b in range(1, N_BUCKETS):
        b_max = jnp.maximum(b_max, tab_ref[b, h])
    return qnorm * knorm * (1.0 + 2.0 ** -9) + (b_max * LOG2E + 2.0 ** -6)


def _shifted_or_running_sweep(i, t, tk, scores, near_bias, values, process, bound, b_far, rows):
    jd = (i * t) // tk
    s_diag = scores(jd) + near_bias(jd)
    slack = jnp.max(bound - jnp.max(s_diag, axis=-1, keepdims=True))

    def shifted_sweep(_):
        def step(s, j, carry, shift):
            l, acc = carry
            p = jnp.exp2(s - shift)
            return (l + jnp.sum(p, axis=-1, keepdims=True),
                    acc + jnp.dot(p.astype(BF16), values(j), preferred_element_type=F32))

        carry = step(s_diag, jd, _softmax_init(rows)[1:], bound)
        lo = jnp.maximum(jd - 1, 0)
        carry = lax.fori_loop(lo, jd, lambda j, c: step(scores(j) + near_bias(j), j, c, bound), carry)
        far_shift = bound - b_far
        return _paired_loop(lo, lambda j, c: step(scores(j), j, c, far_shift), carry)

    def running_max_sweep(_):
        return _biased_sweep(i, t, tk, scores, process, _softmax_init(rows))[1:]

    return lax.cond(slack <= SHIFT_SLACK, shifted_sweep, running_max_sweep, 0)


def _max_key_norm(k_ref, tk, lane_lo, lane_hi):
    klane = lax.broadcasted_iota(jnp.int32, (tk, HEAD_DIM), 1)
    keep = jnp.logical_and(klane >= lane_lo, klane < lane_hi)

    def body(n, c):
        kt = k_ref[pl.ds(pl.multiple_of(n * tk, tk), tk), :].astype(F32)
        sq = jnp.sum(jnp.where(keep, kt * kt, 0.0), axis=-1, keepdims=True)
        return jnp.maximum(c, jnp.max(sq, axis=0, keepdims=True))

    return jnp.sqrt(lax.fori_loop(0, k_ref.shape[0] // tk, body, jnp.zeros((1, 1), F32)))


def _diff_kernel(tab_ref, q_ref, k_ref, v_ref, bias_ref, lq_ref, lk_ref, g_ref, o_ref, knorm_ref,
                 *, tb, nb, tk, n_near, scale, lambda_init):
    h = pl.program_id(0)
    i = pl.program_id(1)
    t = nb * tb
    q = _prescale(q_ref[...], scale * LOG2E)
    lane = lax.broadcasted_iota(jnp.int32, (t, HEAD_DIM), 1)
    zero = jnp.zeros_like(q)
    q2 = jnp.concatenate([jnp.where(lane < DIFF_QK_DIM, q, zero),
                          jnp.where(lane >= DIFF_QK_DIM, q, zero)], axis=0)
    b_far = tab_ref[N_BUCKETS - 1, h] * LOG2E

    @pl.when(i == 0)
    def _():
        knorm_ref[0:1, :] = jnp.broadcast_to(_max_key_norm(k_ref, tk, 0, DIFF_QK_DIM), (1, LANES))
        knorm_ref[1:2, :] = jnp.broadcast_to(_max_key_norm(k_ref, tk, DIFF_QK_DIM, HEAD_DIM), (1, LANES))

    def scores(j):
        return _nt_dot(q2, k_ref[pl.ds(pl.multiple_of(j * tk, tk), tk), :])

    def values(j):
        return v_ref[pl.ds(pl.multiple_of(j * tk, tk), tk), :]

    def near_bias(j):
        b = _near_bias(bias_ref, [i * nb + b for b in range(nb)], j, tb, tk, n_near)
        return jnp.concatenate([b, b], axis=0)

    def process(s, j, carry, near):
        return _online_softmax_step(s + (near_bias(j) if near else b_far), values(j), carry)

    q2f = q2.astype(F32)
    qnorm = jnp.sqrt(jnp.sum(q2f * q2f, axis=-1, keepdims=True))
    knorm = jnp.concatenate([jnp.broadcast_to(knorm_ref[0:1, 0:1], (t, 1)),
                             jnp.broadcast_to(knorm_ref[1:2, 0:1], (t, 1))], axis=0)
    bound = _logit_bound(qnorm, knorm, tab_ref, h)
    l, acc = _shifted_or_running_sweep(i, t, tk, scores, near_bias, values, process, bound, b_far, 2 * t)
    o12 = acc / l
    lam_e = jnp.exp(jnp.sum(lq_ref[...] * lk_ref[...], axis=-1, keepdims=True))
    lam = lam_e[0:1, :] - lam_e[1:2, :] + lambda_init
    o = o12[:t, :] - lam * o12[t:, :]
    o = o * lax.rsqrt(jnp.mean(o * o, axis=-1, keepdims=True) + LN_EPS) * g_ref[...]
    o_ref[...] = (o * (1.0 - lambda_init)).astype(o_ref.dtype)


def _sb_kernel(q_ref, k_ref, v_ref, o_ref, *, t, nq, scale):
    g = pl.program_id(1)

    def later_matrix(n):
        r = lax.broadcasted_iota(jnp.int32, (n, n), 0)
        c = lax.broadcasted_iota(jnp.int32, (n, n), 1)
        return jnp.where(r > c, 1.0, 0.0).astype(BF16)

    def block(q, off, nk, rem, acc, shift):
        k = k_ref[pl.ds(off, nk), :]
        v = v_ref[pl.ds(off, nk), :]
        z = _nt_dot(q, k) * scale
        soft = jnp.log1p(jnp.exp(-jnp.abs(z)))
        log_sig = jnp.minimum(z, 0.0) - soft
        log_rem = jnp.minimum(-z, 0.0) - soft
        if shift is not None:
            rows = lax.broadcasted_iota(jnp.int32, (t, nk), 0)
            cols = lax.broadcasted_iota(jnp.int32, (t, nk), 1)
            mask = cols < rows + shift
            log_rem = jnp.where(mask, log_rem, 0.0)
        hi, lo = _split_bf16(log_rem)
        later = later_matrix(nk)
        after = (jnp.dot(hi, later, preferred_element_type=F32)
                 + jnp.dot(lo, later, preferred_element_type=F32))
        a = jnp.exp(log_sig + after + rem)
        if shift is not None:
            a = jnp.where(mask, a, 0.0)
        acc = acc + jnp.dot(a.astype(BF16), v, preferred_element_type=F32)
        rem = rem + jnp.sum(log_rem, axis=-1, keepdims=True)
        return rem, acc

    def live(rem):
        return (jnp.max(rem) > SB_CUTOFF).astype(jnp.int32)

    state = []
    for u in range(nq):
        i = g * nq + u
        q = q_ref[u * t:(u + 1) * t, :]
        first = jnp.maximum(i - 1, 0)
        rem, acc = block(q, pl.multiple_of(first * t, t), 2 * t, jnp.zeros((t, 1), F32),
                         jnp.zeros((t, HEAD_DIM), F32), (i - first) * t)
        state.append((i, q, rem, acc))

    for u, (i, q, rem, acc) in enumerate(state):
        def cond(c):
            j, go, _, _ = c
            return jnp.logical_and(j >= 0, go > 0)

        def body(c, q=q):
            j, _, rem, acc = c
            rem, acc = block(q, pl.multiple_of(j * t, t), t, rem, acc, None)
            return j - 1, live(rem), rem, acc

        _, _, _, acc = lax.while_loop(cond, body, (i - 2, live(rem), rem, acc))
        o_ref[u * t:(u + 1) * t, :] = acc.astype(o_ref.dtype)


def _attention_call(kernel, h, out_cols, col0, extra_in, extra_specs, scratch, name, t):
    s = h.shape[0]
    nh = N_HEADS_GROUP
    in_specs = list(extra_specs[0]) + [
        pl.BlockSpec((t, HEAD_DIM), lambda hd, i: (i, col0 + hd)),
        pl.BlockSpec((s, HEAD_DIM), lambda hd, i: (0, col0 + nh + hd)),
        pl.BlockSpec((s, HEAD_DIM), lambda hd, i: (0, col0 + 2 * nh + hd)),
    ] + list(extra_specs[1])
    args = list(extra_in[0]) + [h, h, h] + list(extra_in[1])
    return pl.pallas_call(
        kernel,
        out_shape=jax.ShapeDtypeStruct((s, out_cols), BF16),
        grid=(nh, s // t),
        in_specs=in_specs,
        out_specs=pl.BlockSpec((t, HEAD_DIM), lambda hd, i: (i, hd)),
        scratch_shapes=scratch,
        compiler_params=_cparams(2, VMEM_LIMIT),
        name=name,
    )(*args)


def _route(x, w, b):
    logits = jnp.dot(x, w, preferred_element_type=F32, precision=lax.Precision.HIGHEST) + b
    tm, n = logits.shape
    lane = lax.broadcasted_iota(jnp.int32, (tm, n), 1)
    real = lane < N_EXPERTS
    logits = jnp.where(real, logits, -jnp.inf)
    e = jnp.exp(logits - jnp.max(logits, axis=-1, keepdims=True))
    aff = e / jnp.sum(e, axis=-1, keepdims=True)
    group = lane // EXPERTS_PER_GROUP
    best = jnp.full((tm, 1), -jnp.inf, F32)
    g_sel = jnp.zeros((tm, 1), jnp.int32)
    for g in range(N_GROUPS):
        gmax = jnp.max(jnp.where(group == g, aff, -jnp.inf), axis=-1, keepdims=True)
        better = gmax > best
        g_sel = jnp.where(better, g, g_sel)
        best = jnp.where(better, gmax, best)
    in_group = jnp.logical_and(group == g_sel, real)
    cand = jnp.where(in_group, aff, -jnp.inf)
    v1 = jnp.max(cand, axis=-1, keepdims=True)
    i1 = jnp.min(jnp.where(cand == v1, lane, n), axis=-1, keepdims=True)
    cand2 = jnp.where(lane == i1, -jnp.inf, cand)
    v2 = jnp.max(cand2, axis=-1, keepdims=True)
    i2 = jnp.min(jnp.where(cand2 == v2, lane, n), axis=-1, keepdims=True)
    tot = v1 + v2
    chosen = jnp.where(jnp.logical_or(lane == i1, lane == i2), 1.0, 0.0)
    r = lax.broadcasted_iota(jnp.int32, (tm, tm), 0)
    c = lax.broadcasted_iota(jnp.int32, (tm, tm), 1)
    earlier = jnp.where(c < r, 1.0, 0.0).astype(BF16)
    before = jnp.dot(earlier, chosen.astype(BF16), preferred_element_type=F32)
    rank1 = jnp.sum(jnp.where(lane == i1, before, 0.0), axis=-1, keepdims=True)
    rank2 = jnp.sum(jnp.where(lane == i2, before, 0.0), axis=-1, keepdims=True)
    route = jnp.where(lane == 0, v1 / tot,
                      jnp.where(lane == 1, v2 / tot,
                                jnp.where(lane == 2, i1.astype(F32),
                                          jnp.where(lane == 3, i2.astype(F32),
                                                    jnp.where(lane == 4, rank1,
                                                              jnp.where(lane == 5, rank2, 0.0))))))
    return route, jnp.sum(chosen, axis=0, keepdims=True)


def _dispatch_plan(route, counts, tm):
    s = route.shape[0]
    nt = s // tm
    n_tiles = (MOE_TOP_K * s) // tm + N_EXPERTS
    ids = jnp.arange(N_EXPERTS, dtype=jnp.int32)
    expert = route[:, 2:2 + MOE_TOP_K].astype(jnp.int32).reshape(nt, tm, MOE_TOP_K)
    rank = route[:, 4:4 + MOE_TOP_K].astype(jnp.int32).reshape(nt, tm, MOE_TOP_K)
    cnt = counts[:, 0, :N_EXPERTS].astype(jnp.int32)
    tile_base = jnp.cumsum(cnt, axis=0) - cnt
    padded = -(-jnp.sum(cnt, axis=0) // tm) * tm
    seg_end = jnp.cumsum(padded)
    base = (seg_end - padded)[None, :] + tile_base
    pick = expert[..., None] == ids
    pair_row = (jnp.sum(jnp.where(pick, base[:, None, None, :], 0), axis=-1) + rank).reshape(s, MOE_TOP_K)
    tile_start = jnp.arange(n_tiles, dtype=jnp.int32) * tm
    tile_expert = jnp.minimum(jnp.sum((tile_start[:, None] >= seg_end[None, :]).astype(jnp.int32), axis=1),
                              N_EXPERTS - 1).astype(jnp.int32)
    tile_live = (tile_start < seg_end[-1]).astype(jnp.int32)
    row_token = (jnp.arange(n_tiles * tm, dtype=jnp.int32) % s).at[pair_row.reshape(-1)].set(
        jnp.arange(MOE_TOP_K * s, dtype=jnp.int32) // MOE_TOP_K)
    return pair_row, row_token, tile_expert, tile_live


def _moe_group_kernel(te_ref, live_ref, x_ref, wg_ref, wu_ref, wd_ref, o_ref, xb_ref, acc_ref):
    g = pl.program_id(0)
    f = pl.program_id(1)
    last = f == pl.num_programs(1) - 1
    live = live_ref[g] > 0

    @pl.when(jnp.logical_and(live, f == 0))
    def _():
        lo, hi = _unpack_bf16_pairs(x_ref[...])
        xb_ref[...] = jnp.concatenate([lo.astype(BF16), hi.astype(BF16)], axis=1)

    @pl.when(jnp.logical_and(jnp.logical_not(live), last))
    def _():
        o_ref[...] = jnp.zeros(o_ref.shape, o_ref.dtype)

    @pl.when(live)
    def _():
        xb = xb_ref[...]
        a = jnp.dot(xb, wg_ref[...], preferred_element_type=F32)
        u = jnp.dot(xb, wu_ref[...], preferred_element_type=F32)
        hid = ((a * jax.nn.sigmoid(a)) * u).astype(BF16)
        y = jnp.dot(hid, wd_ref[...], preferred_element_type=F32)

        @pl.when(f == 0)
        def _():
            acc_ref[...] = y

        @pl.when(f > 0)
        def _():
            acc_ref[...] += y

        @pl.when(last)
        def _():
            o_ref[...] = _pack_bf16_pairs(acc_ref[...])


def _moe_group_ffn(xs, tile_expert, tile_live, wg, wu, wd, tm, tf):
    rows, half = xs.shape
    d = 2 * half
    fdim = wg.shape[2]
    grid_spec = pltpu.PrefetchScalarGridSpec(
        num_scalar_prefetch=2,
        grid=(rows // tm, fdim // tf),
        in_specs=[pl.BlockSpec((tm, half), lambda g, f, te, lv: (g, 0)),
                  pl.BlockSpec((None, d, tf), lambda g, f, te, lv: (te[g], 0, f)),
                  pl.BlockSpec((None, d, tf), lambda g, f, te, lv: (te[g], 0, f)),
                  pl.BlockSpec((None, tf, d), lambda g, f, te, lv: (te[g], f, 0))],
        out_specs=pl.BlockSpec((tm, half), lambda g, f, te, lv: (g, 0)),
        scratch_shapes=[pltpu.VMEM((tm, d), BF16), pltpu.VMEM((tm, d), F32)])
    return pl.pallas_call(
        _moe_group_kernel,
        out_shape=jax.ShapeDtypeStruct((rows, half), jnp.uint32),
        grid_spec=grid_spec,
        compiler_params=_cparams(2, VMEM_LIMIT),
        name="moe_group_ffn",
    )(tile_expert, tile_live, xs, wg, wu, wd)


def _moe_combine_ln_kernel(x_ref, r_ref, y0_ref, y1_ref, g_ref, b_ref, o_ref):
    r = r_ref[...]
    lo0, hi0 = _unpack_bf16_pairs(y0_ref[...])
    lo1, hi1 = _unpack_bf16_pairs(y1_ref[...])
    ff = jnp.concatenate([r[:, 0:1] * lo0 + r[:, 1:2] * lo1, r[:, 0:1] * hi0 + r[:, 1:2] * hi1], axis=1)
    o_ref[...] = _layer_norm(ALPHA * x_ref[...] + ff, g_ref[...], b_ref[...])


def _moe_combine_ln(x, route, y, g, b, tm):
    s, d = x.shape
    nt = s // tm
    return pl.pallas_call(
        _moe_combine_ln_kernel,
        out_shape=jax.ShapeDtypeStruct((s, d), F32),
        grid=(nt,),
        in_specs=[pl.BlockSpec((tm, d), lambda i: (i, 0)),
                  pl.BlockSpec((tm, LANES), lambda i: (i, 0)),
                  pl.BlockSpec((tm, d // 2), lambda i: (i, 0)),
                  pl.BlockSpec((tm, d // 2), lambda i: (i + nt, 0)),
                  pl.BlockSpec((1, d), lambda i: (0, 0)),
                  pl.BlockSpec((1, d), lambda i: (0, 0))],
        out_specs=pl.BlockSpec((tm, d), lambda i: (i, 0)),
        compiler_params=_cparams(1, VMEM_LIMIT),
        name="moe_combine_ln",
    )(x, route, y, y, g.reshape(1, d), b.reshape(1, d))


def _sc_row_gather(src, idx):
    n = idx.shape[0]
    d = src.shape[1]
    workers = SC_CORES * SC_SUBCORES
    per_worker = n // workers
    assert n % (workers * SC_GATHER_ROWS) == 0
    mesh = plsc.VectorSubcoreMesh(core_axis_name="c", subcore_axis_name="s")

    ch = SC_GATHER_ROWS
    n_chunks = per_worker // ch
    assert n_chunks % 2 == 0

    @functools.partial(
        pl.kernel, mesh=mesh, out_type=jax.ShapeDtypeStruct((n, d), src.dtype),
        scratch_types=[pltpu.VMEM((per_worker,), jnp.int32),
                       pltpu.VMEM((ch, d), src.dtype), pltpu.VMEM((ch, d), src.dtype),
                       pltpu.SemaphoreType.DMA, pltpu.SemaphoreType.DMA],
        name="sc_row_gather")
    def gather(src_hbm, idx_hbm, out_hbm, idx_v, rows_a, rows_b, sem_a, sem_b):
        base = (lax.axis_index("s") * SC_CORES + lax.axis_index("c")) * per_worker
        pltpu.sync_copy(idx_hbm.at[pl.ds(base, per_worker)], idx_v)

        def start(c, rows, sem):
            pltpu.async_copy(src_hbm.at[idx_v.at[pl.ds(c * ch, ch)]], rows, sem)

        def finish(c, rows, sem):
            pltpu.make_async_copy(src_hbm.at[pl.ds(0, ch)], rows, sem).wait()
            pltpu.sync_copy(rows, out_hbm.at[pl.ds(base + c * ch, ch)])

        start(0, rows_a, sem_a)

        @pl.loop(0, n_chunks, step=2)
        def _(c):
            start(c + 1, rows_b, sem_b)
            finish(c, rows_a, sem_a)

            @pl.when(c + 2 < n_chunks)
            def _():
                start(c + 2, rows_a, sem_a)

            finish(c + 1, rows_b, sem_b)

    return gather(src, idx)


def _moe_ln(x, x_packed, route, counts, wg, wu, wd, g, b, tm):
    pair_row, row_token, tile_expert, tile_live = _dispatch_plan(route, counts, tm)
    xs = _sc_row_gather(x_packed, row_token)
    ys = _moe_group_ffn(xs, tile_expert, tile_live, wg, wu, wd, tm, D_EXPERT // 2)
    slot_major = pair_row.T.reshape(-1)
    y = _sc_row_gather(ys, slot_major)
    return _moe_combine_ln(x, route, y, g, b, tm)


def kernel(x, w_in_even, w_out_even, forget_bias, w_in_odd, w_out_odd, lambda_q, lambda_k,
           subln_gain, rel_bias, w_router, b_router, w_gate, w_up, w_down, ln_gain, ln_bias):
    bsz, s, d = x.shape
    assert bsz == 1 and d == D_MODEL and s % MOBA_BLOCK == 0
    t = ATT_TILE
    nh = N_HEADS_GROUP
    gw = GROUP_WIDTH
    tm_proj = min(1024, s)
    tm_row = min(512, s)
    tk = min(ATT_KEYS, s)
    n_near = _near_tile_count(t)
    assert s % tk == 0 and tk // t + 1 >= n_near
    nblk = s // MOBA_BLOCK
    nbp = -(-nblk // LANES) * LANES

    x2 = x.reshape(s, d)
    bias_tiles = _bias_tiles(rel_bias, t, n_near)
    tab_spec = pl.BlockSpec(memory_space=pltpu.SMEM)
    bias_spec = pl.BlockSpec((None, n_near + 2, t, t), lambda hd, i: (hd, 0, 0, 0))
    block_onehot = (jnp.arange(s, dtype=jnp.int32)[:, None] // MOBA_BLOCK
                    == jnp.arange(nbp, dtype=jnp.int32)[None, :]).astype(BF16)

    for layer in range(DEPTH):
        li = layer // 2
        if layer % 2 == 0:
            w = w_in_even[li]
            fcol = 3 * gw
            w_main = jnp.concatenate([w[:, :fcol], w[:, fcol + nh:]], axis=1).astype(BF16)
            w_f = jnp.zeros((d, LANES), BF16).at[:, :nh].set(w[:, fcol:fcol + nh].astype(BF16))
            h = _in_proj(x2, w_main, BF16, tm_proj, 512)
            f = _in_proj(x2, w_f, F32, tm_proj, LANES)
            c = _decay_cumsum(f[:, :nh].T, forget_bias[li])
            oa = _attention_call(
                functools.partial(_fox_kernel, t=FOX_TQ, tk=min(FOX_TK, s), scale=HEAD_DIM ** -0.5), h, gw, 0,
                ([c[:, min(FOX_TK, s) - 1::min(FOX_TK, s)], c[:, ::FOX_TQ]],
                 [c.reshape(nh, 1, s), c.reshape(nh, s, 1)]),
                ([tab_spec, tab_spec],
                 [pl.BlockSpec((None, 1, s), lambda hd, i: (hd, 0, 0)),
                  pl.BlockSpec((None, FOX_TQ, 1), lambda hd, i: (hd, i, 0))]),
                [pltpu.VMEM((SUBLANES, LANES), F32)],
                "fox_attention", FOX_TQ)
            ob = _attention_call(
                functools.partial(_moba_kernel, tb=t, nb=MOBA_Q_BLOCKS, tk=tk, nblk=nblk, n_near=n_near,
                                  scale=HEAD_DIM ** -0.5),
                h, gw, 3 * nh,
                ([rel_bias], [bias_tiles, block_onehot]),
                ([tab_spec], [bias_spec, pl.BlockSpec((s, nbp), lambda hd, i: (0, 0))]),
                [pltpu.VMEM((nbp, HEAD_DIM), F32), pltpu.VMEM((SUBLANES, LANES), F32)],
                "moba_attention", MOBA_Q_BLOCKS * t)
            w_out = w_out_even[li]
        else:
            lambda_init = 0.8 - 0.6 * math.exp(-0.3 * layer)
            h = _in_proj(x2, w_in_odd[li].astype(BF16), BF16, tm_proj, 512)
            small = lambda shape: pl.BlockSpec(shape, lambda hd, i: (0, 0))
            oa = _attention_call(
                functools.partial(_diff_kernel, tb=t, nb=DIFF_Q_BLOCKS, tk=tk, n_near=n_near,
                                  scale=DIFF_QK_DIM ** -0.5, lambda_init=lambda_init),
                h, gw, 0,
                ([rel_bias], [bias_tiles, lambda_q[li], lambda_k[li], subln_gain[li].reshape(1, HEAD_DIM)]),
                ([tab_spec], [bias_spec, small((2, DIFF_QK_DIM)), small((2, DIFF_QK_DIM)),
                              small((1, HEAD_DIM))]),
                [pltpu.VMEM((SUBLANES, LANES), F32)],
                "diff_attention", DIFF_Q_BLOCKS * t)
            ob = _attention_call(
                functools.partial(_sb_kernel, t=t, nq=SB_Q_TILES, scale=HEAD_DIM ** -0.5), h, gw, 3 * nh,
                ([], []), ([], []), [], "stickbreak_attention", SB_Q_TILES * t)
            w_out = w_out_odd[li]
        x2, x2_packed = _out_proj_ln(oa, ob, w_out.astype(BF16), x2, ln_gain[layer, 0], ln_bias[layer, 0],
                                     tm_row)
        route, counts = _router(x2, w_router, b_router, tm_row)
        x2 = _moe_ln(x2, x2_packed, route, counts, w_gate[layer].astype(BF16), w_up[layer].astype(BF16),
                     w_down[layer].astype(BF16), ln_gain[layer, 1], ln_bias[layer, 1], tm_row)
    return x2.reshape(bsz, s, d)
```

```python
import functools
import math

import numpy as np
import jax
import jax.numpy as jnp
from jax import lax
from jax.experimental import pallas as pl
from jax.experimental.pallas import tpu as pltpu
from jax.experimental.pallas import tpu_sc as plsc

F32 = jnp.float32
BF16 = jnp.bfloat16

D_MODEL = 2048
DEPTH = 2
HEAD_DIM = 128
N_HEADS_GROUP = 8
GROUP_WIDTH = N_HEADS_GROUP * HEAD_DIM
DIFF_QK_DIM = HEAD_DIM // 2
MOBA_BLOCK = 256
MOBA_TOPK = 3
N_BUCKETS = 32
MAX_EXACT = N_BUCKETS // 2
MAX_DISTANCE = 1024
N_EXPERTS = 16
N_GROUPS = 4
EXPERTS_PER_GROUP = N_EXPERTS // N_GROUPS
MOE_TOP_K = 2
SC_CORES = 2
SC_SUBCORES = 16
SC_GATHER_ROWS = 32
SUBLANES = 8
D_EXPERT = D_MODEL // 2
ALPHA = (2.0 * DEPTH) ** 0.25
LN_EPS = 1e-5
NEG_INF = -1e30
LOG2E = math.log2(math.e)
LANES = 128
ATT_TILE = 256
ATT_KEYS = 1024
FOX_TQ = 1024
FOX_TK = 1024
MOBA_Q_BLOCKS = 4
SB_Q_TILES = 2
DIFF_Q_BLOCKS = 4
SHIFT_SLACK = 96.0
FLUSH_EXPONENT = -130.0
SB_CUTOFF = -104.0
VMEM_LIMIT = 56 * 1024 * 1024


def _cparams(n_axes, vmem=None):
    return pltpu.CompilerParams(dimension_semantics=("arbitrary",) * n_axes,
                                vmem_limit_bytes=vmem)


def _nt_dot(a, b):
    return lax.dot_general(a, b, (((1,), (1,)), ((), ())), preferred_element_type=F32)


def _split_bf16(x):
    hi = x.astype(BF16)
    lo = (x - hi.astype(F32)).astype(BF16)
    return hi, lo


def _pack_bf16_pairs(x):
    n = x.shape[1] // 2
    lo = pltpu.bitcast(x[:, :n].astype(BF16).astype(F32), jnp.uint32)
    hi = pltpu.bitcast(x[:, n:].astype(BF16).astype(F32), jnp.uint32)
    return jnp.bitwise_or(lax.shift_right_logical(lo, jnp.uint32(16)),
                          jnp.bitwise_and(hi, jnp.uint32(0xFFFF0000)))


def _unpack_bf16_pairs(w):
    lo = pltpu.bitcast(lax.shift_left(w, jnp.uint32(16)), F32)
    hi = pltpu.bitcast(jnp.bitwise_and(w, jnp.uint32(0xFFFF0000)), F32)
    return lo, hi


def _layer_norm(z, g, b):
    mu = jnp.mean(z, axis=-1, keepdims=True)
    zc = z - mu
    var = jnp.mean(zc * zc, axis=-1, keepdims=True)
    return zc * lax.rsqrt(var + LN_EPS) * g + b


def _inproj_kernel(x_ref, w_ref, o_ref, xb_ref):
    @pl.when(pl.program_id(1) == 0)
    def _():
        xb_ref[...] = x_ref[...].astype(BF16)

    o_ref[...] = jnp.dot(xb_ref[...], w_ref[...], preferred_element_type=F32).astype(o_ref.dtype)


def _in_proj(x, w, out_dtype, tm, tn):
    s, d = x.shape
    n = w.shape[1]
    return pl.pallas_call(
        _inproj_kernel,
        out_shape=jax.ShapeDtypeStruct((s, n), out_dtype),
        grid=(s // tm, n // tn),
        in_specs=[pl.BlockSpec((tm, d), lambda i, j: (i, 0)),
                  pl.BlockSpec((d, tn), lambda i, j: (0, j))],
        out_specs=pl.BlockSpec((tm, tn), lambda i, j: (i, j)),
        scratch_shapes=[pltpu.VMEM((tm, d), BF16)],
        compiler_params=_cparams(2, VMEM_LIMIT),
        name="in_proj",
    )(x, w)


def _outproj_ln_kernel(oa_ref, ob_ref, wa_ref, wb_ref, x_ref, g_ref, b_ref, o_ref, packed_ref):
    y = jnp.dot(oa_ref[...], wa_ref[...], preferred_element_type=F32)
    y = y + jnp.dot(ob_ref[...], wb_ref[...], preferred_element_type=F32)
    x1 = _layer_norm(ALPHA * x_ref[...] + y, g_ref[...], b_ref[...])
    o_ref[...] = x1
    packed_ref[...] = _pack_bf16_pairs(x1)


def _out_proj_ln(oa, ob, w, x, g, b, tm):
    s, d = x.shape
    gw = oa.shape[1]
    return pl.pallas_call(
        _outproj_ln_kernel,
        out_shape=(jax.ShapeDtypeStruct((s, d), F32), jax.ShapeDtypeStruct((s, d // 2), jnp.uint32)),
        grid=(s // tm,),
        in_specs=[pl.BlockSpec((tm, gw), lambda i: (i, 0)),
                  pl.BlockSpec((tm, gw), lambda i: (i, 0)),
                  pl.BlockSpec((gw, d), lambda i: (0, 0)),
                  pl.BlockSpec((gw, d), lambda i: (1, 0)),
                  pl.BlockSpec((tm, d), lambda i: (i, 0)),
                  pl.BlockSpec((1, d), lambda i: (0, 0)),
                  pl.BlockSpec((1, d), lambda i: (0, 0))],
        out_specs=(pl.BlockSpec((tm, d), lambda i: (i, 0)),
                   pl.BlockSpec((tm, d // 2), lambda i: (i, 0))),
        compiler_params=_cparams(1, VMEM_LIMIT),
        name="out_proj_ln",
    )(oa, ob, w, w, x, g.reshape(1, d), b.reshape(1, d))


def _router_kernel(x_ref, w_ref, b_ref, route_ref, count_ref):
    route, counts = _route(x_ref[...], w_ref[...], b_ref[...])
    route_ref[...] = route
    count_ref[...] = jnp.broadcast_to(counts, count_ref.shape)


def _router(x, w_router, b_router, tm):
    s, d = x.shape
    wr = jnp.zeros((d, LANES), F32).at[:, :N_EXPERTS].set(w_router)
    br = jnp.zeros((1, LANES), F32).at[0, :N_EXPERTS].set(b_router)
    return pl.pallas_call(
        _router_kernel,
        out_shape=(jax.ShapeDtypeStruct((s, LANES), F32),
                   jax.ShapeDtypeStruct((s // tm, SUBLANES, LANES), F32)),
        grid=(s // tm,),
        in_specs=[pl.BlockSpec((tm, d), lambda i: (i, 0)),
                  pl.BlockSpec((d, LANES), lambda i: (0, 0)),
                  pl.BlockSpec((1, LANES), lambda i: (0, 0))],
        out_specs=(pl.BlockSpec((tm, LANES), lambda i: (i, 0)),
                   pl.BlockSpec((None, SUBLANES, LANES), lambda i: (i, 0, 0))),
        compiler_params=_cparams(1, VMEM_LIMIT),
        name="router",
    )(x, wr, br)


def _decay_cumsum_kernel(f_ref, b_ref, c_ref):
    nh, s = f_ref.shape
    rows = lax.broadcasted_iota(jnp.int32, (LANES, LANES), 0)
    cols = lax.broadcasted_iota(jnp.int32, (LANES, LANES), 1)
    upper = jnp.where(rows <= cols, 1.0, 0.0).astype(F32)
    bias = b_ref[...]

    def body(n, carry):
        off = pl.multiple_of(n * LANES, LANES)
        z = f_ref[:, pl.ds(off, LANES)] + bias
        logf = jnp.minimum(z, 0.0) - jnp.log1p(jnp.exp(-jnp.abs(z)))
        c = jnp.dot(logf, upper, preferred_element_type=F32,
                    precision=lax.Precision.HIGHEST) + carry
        c_ref[:, pl.ds(off, LANES)] = c * LOG2E
        return carry + jnp.sum(logf, axis=-1, keepdims=True)

    lax.fori_loop(0, s // LANES, body, jnp.zeros((nh, 1), F32))


def _decay_cumsum(f_t, bias):
    nh, s = f_t.shape
    return pl.pallas_call(
        _decay_cumsum_kernel,
        out_shape=jax.ShapeDtypeStruct((nh, s), F32),
        name="decay_cumsum",
    )(f_t, bias.reshape(nh, 1))


def _t5_bucket_np(dist):
    n = np.maximum(dist, 0)
    nf = np.maximum(n, 1).astype(np.float32)
    ratio = np.log(nf / np.float32(MAX_EXACT)) / np.float32(math.log(MAX_DISTANCE / MAX_EXACT))
    large = MAX_EXACT + (ratio.astype(np.float32) * np.float32(N_BUCKETS - MAX_EXACT)).astype(np.int32)
    large = np.minimum(large, N_BUCKETS - 1)
    return np.where(n < MAX_EXACT, n, large).astype(np.int32)


def _near_tile_count(t):
    d = np.arange(0, 4 * MAX_DISTANCE, dtype=np.int64)
    not_last = np.nonzero(_t5_bucket_np(d) != N_BUCKETS - 1)[0]
    d_sat = int(not_last.max()) + 1
    n = 1
    while (n - 1) * t + 1 < d_sat:
        n += 1
    return n


def _bucket_tiles_np(t, n_near):
    r = np.arange(t)[:, None]
    c = np.arange(t)[None, :]
    tiles = []
    for delta in range(n_near):
        dist = delta * t + r - c
        tiles.append(np.where(dist >= 0, _t5_bucket_np(dist), -1))
    tiles.append(np.full((t, t), N_BUCKETS - 1))
    tiles.append(np.full((t, t), -1))
    return np.stack(tiles).astype(np.int32)


def _bias_tiles_kernel(tab_ref, idx_ref, o_ref):
    h = pl.program_id(0)
    idx = idx_ref[...]
    out = jnp.full(idx.shape, NEG_INF, F32)
    for b in range(N_BUCKETS):
        out = jnp.where(idx == b, tab_ref[b, h] * LOG2E, out)
    o_ref[...] = out


def _bias_tiles(rel_bias, t, n_near):
    nh = rel_bias.shape[1]
    idx = jnp.asarray(_bucket_tiles_np(t, n_near))
    n_tiles = idx.shape[0]
    return pl.pallas_call(
        _bias_tiles_kernel,
        out_shape=jax.ShapeDtypeStruct((nh, n_tiles, t, t), F32),
        grid=(nh, n_tiles),
        in_specs=[pl.BlockSpec(memory_space=pltpu.SMEM),
                  pl.BlockSpec((None, t, t), lambda h, n: (n, 0, 0))],
        out_specs=pl.BlockSpec((None, None, t, t), lambda h, n: (h, n, 0, 0)),
        compiler_params=_cparams(2),
        name="bias_tiles",
    )(rel_bias, idx)


def _online_softmax_step(s, v, carry):
    m, l, acc = carry
    m_new = jnp.maximum(m, jnp.max(s, axis=-1, keepdims=True))
    alpha = jnp.exp2(m - m_new)
    p = jnp.exp2(s - m_new)
    l = alpha * l + jnp.sum(p, axis=-1, keepdims=True)
    acc = alpha * acc + jnp.dot(p.astype(BF16), v, preferred_element_type=F32)
    return m_new, l, acc


def _prescale(q, c):
    return (q.astype(F32) * c).astype(BF16)


def _near_bias(bias_ref, blocks, j, tb, tk, n_near):
    nsub = tk // tb
    rows = []
    for i in blocks:
        tiles = []
        for b in range(nsub):
            delta = i - (j * nsub + b)
            tiles.append(bias_ref[jnp.where(delta < 0, n_near + 1, jnp.minimum(delta, n_near))])
        rows.append(jnp.concatenate(tiles, axis=1))
    return rows[0] if len(rows) == 1 else jnp.concatenate(rows, axis=0)


def _biased_sweep(i, t, tk, scores, process, init):
    jd = (i * t) // tk
    carry = process(scores(jd), jd, init, True)
    lo = jnp.maximum(jd - 1, 0)
    carry = lax.fori_loop(lo, jd, lambda j, c: process(scores(j), j, c, True), carry)
    return lax.fori_loop(0, lo, lambda j, c: process(scores(j), j, c, False), carry)


def _softmax_init(rows):
    return (jnp.full((rows, 1), NEG_INF, F32), jnp.zeros((rows, 1), F32),
            jnp.zeros((rows, HEAD_DIM), F32))


def _fox_kernel(cend_ref, cstart_ref, q_ref, k_ref, v_ref, c_ref, ccol_ref, o_ref, knorm_ref,
                *, t, tk, scale):
    h = pl.program_id(0)
    i = pl.program_id(1)

    @pl.when(i == 0)
    def _():
        knorm_ref[...] = jnp.broadcast_to(_max_key_norm(k_ref, tk, 0, HEAD_DIM), knorm_ref.shape)

    q = _prescale(q_ref[...], scale * LOG2E)
    rows = lax.broadcasted_iota(jnp.int32, (t, tk), 0)
    cols = lax.broadcasted_iota(jnp.int32, (t, tk), 1)

    def scores(j):
        off = pl.multiple_of(j * tk, tk)
        return _nt_dot(q, k_ref[pl.ds(off, tk), :]) - c_ref[:, pl.ds(off, tk)]

    def values(j):
        return v_ref[pl.ds(pl.multiple_of(j * tk, tk), tk), :]

    def process(s, j, carry):
        return _online_softmax_step(s, values(j), carry)

    jd = (i * t) // tk
    s_diag = jnp.where(cols <= rows + (i * t - jd * tk), scores(jd), NEG_INF)

    qf = q.astype(F32)
    qnorm = jnp.sqrt(jnp.sum(qf * qf, axis=-1, keepdims=True))
    bound = qnorm * knorm_ref[0:1, 0:1] * (1.0 + 2.0 ** -9) + 2.0 ** -6 - ccol_ref[...]
    slack = jnp.max(bound - jnp.max(s_diag, axis=-1, keepdims=True))

    def shifted_sweep(_):
        def step(s, j, carry):
            l, acc = carry
            p = jnp.exp2(s - bound)
            return (l + jnp.sum(p, axis=-1, keepdims=True),
                    acc + jnp.dot(p.astype(BF16), values(j), preferred_element_type=F32))

        c_first_row = cstart_ref[h, i]
        first_live = jnp.int32(0)
        for j in range(cend_ref.shape[1]):
            dead = jnp.logical_and(j < jd, c_first_row - cend_ref[h, j] < FLUSH_EXPONENT)
            first_live = first_live + dead.astype(jnp.int32)
        carry = step(s_diag, jd, _softmax_init(t)[1:])
        return _paired_loop(jd - first_live, lambda n, c: step(scores(first_live + n), first_live + n, c),
                            carry)

    def running_max_sweep(_):
        carry = _online_softmax_step(s_diag, values(jd), _softmax_init(t))
        return lax.fori_loop(0, jd, lambda j, c: process(scores(j), j, c), carry)[1:]

    l, acc = lax.cond(slack <= SHIFT_SLACK, shifted_sweep, running_max_sweep, 0)
    o_ref[...] = (acc / l).astype(o_ref.dtype)


def _moba_kernel(tab_ref, q_ref, k_ref, v_ref, bias_ref, et_ref, o_ref, kmean_ref, knorm_ref,
                 *, tb, nb, tk, nblk, n_near, scale):
    h = pl.program_id(0)
    g = pl.program_id(1)
    t = nb * tb
    nbp = kmean_ref.shape[0]

    @pl.when(g == 0)
    def _():
        kmean_ref[...] = jnp.zeros(kmean_ref.shape, F32)

        def mean_body(n, _):
            off = pl.multiple_of(n * tb, tb)
            kb = k_ref[pl.ds(off, tb), :].astype(F32)
            kmean_ref[pl.ds(n, 1), :] = jnp.sum(kb, axis=0, keepdims=True) * (1.0 / tb)
            return 0

        lax.fori_loop(0, nblk, mean_body, 0)
        knorm_ref[...] = jnp.broadcast_to(_max_key_norm(k_ref, tk, 0, HEAD_DIM), knorm_ref.shape)

    q = q_ref[...]
    km_hi, km_lo = _split_bf16(kmean_ref[...])
    gate = _nt_dot(q, km_hi) + _nt_dot(q, km_lo)
    blk = lax.broadcasted_iota(jnp.int32, (t, nbp), 1)
    own = g * nb + lax.broadcasted_iota(jnp.int32, (t, 1), 0) // tb
    eligible = blk < own
    sel = jnp.zeros((t, nbp), F32)
    for r in range(MOBA_TOPK):
        cand = jnp.where(eligible, jnp.where(sel > 0.0, -jnp.inf, gate), -jnp.inf)
        vmax = jnp.max(cand, axis=-1, keepdims=True)
        first = jnp.min(jnp.where(cand == vmax, blk, nbp), axis=-1, keepdims=True)
        take = jnp.where(r < own, 1.0, 0.0)
        sel = jnp.maximum(sel, jnp.where(blk == first, take, 0.0))

    sel = jnp.maximum(sel, jnp.where(blk == own, 1.0, 0.0))
    penalty = jnp.where(sel > 0.0, 0.0, NEG_INF).astype(BF16)
    q_aug = jnp.concatenate([_prescale(q, scale * LOG2E), penalty], axis=1)
    b_far = tab_ref[N_BUCKETS - 1, h] * LOG2E

    def scores(j):
        off = pl.multiple_of(j * tk, tk)
        k_aug = jnp.concatenate([k_ref[pl.ds(off, tk), :], et_ref[pl.ds(off, tk), :]], axis=1)
        return _nt_dot(q_aug, k_aug)

    def values(j):
        return v_ref[pl.ds(pl.multiple_of(j * tk, tk), tk), :]

    def near_bias(j):
        return _near_bias(bias_ref, [g * nb + b for b in range(nb)], j, tb, tk, n_near)

    def process(s, j, carry, near):
        return _online_softmax_step(s + (near_bias(j) if near else b_far), values(j), carry)

    qf = q_aug[:, :HEAD_DIM].astype(F32)
    qnorm = jnp.sqrt(jnp.sum(qf * qf, axis=-1, keepdims=True))
    bound = _logit_bound(qnorm, knorm_ref[0:1, 0:1], tab_ref, h)
    l, acc = _shifted_or_running_sweep(g, t, tk, scores, near_bias, values, process, bound, b_far, t)
    o_ref[...] = (acc / l).astype(o_ref.dtype)


def _paired_loop(n, body, carry):
    carry = lax.fori_loop(0, n // 2, lambda p, c: body(2 * p + 1, body(2 * p, c)), carry)
    return lax.fori_loop(0, n % 2, lambda _, c: body(n - 1, c), carry)


def _logit_bound(qnorm, knorm, tab_ref, h):
    b_max = tab_ref[0, h]
    for b in range(1, N_BUCKETS):
        b_max = jnp.maximum(b_max, tab_ref[b, h])
    return qnorm * knorm * (1.0 + 2.0 ** -9) + (b_max * LOG2E + 2.0 ** -6)


def _shifted_or_running_sweep(i, t, tk, scores, near_bias, values, process, bound, b_far, rows):
    jd = (i * t) // tk
    s_diag = scores(jd) + near_bias(jd)
    slack = jnp.max(bound - jnp.max(s_diag, axis=-1, keepdims=True))

    def shifted_sweep(_):
        def step(s, j, carry, shift):
            l, acc = carry
            p = jnp.exp2(s - shift)
            return (l + jnp.sum(p, axis=-1, keepdims=True),
                    acc + jnp.dot(p.astype(BF16), values(j), preferred_element_type=F32))

        carry = step(s_diag, jd, _softmax_init(rows)[1:], bound)
        lo = jnp.maximum(jd - 1, 0)
        carry = lax.fori_loop(lo, jd, lambda j, c: step(scores(j) + near_bias(j), j, c, bound), carry)
        far_shift = bound - b_far
        return _paired_loop(lo, lambda j, c: step(scores(j), j, c, far_shift), carry)

    def running_max_sweep(_):
        return _biased_sweep(i, t, tk, scores, process, _softmax_init(rows))[1:]

    return lax.cond(slack <= SHIFT_SLACK, shifted_sweep, running_max_sweep, 0)


def _max_key_norm(k_ref, tk, lane_lo, lane_hi):
    klane = lax.broadcasted_iota(jnp.int32, (tk, HEAD_DIM), 1)
    keep = jnp.logical_and(klane >= lane_lo, klane < lane_hi)

    def body(n, c):
        kt = k_ref[pl.ds(pl.multiple_of(n * tk, tk), tk), :].astype(F32)
        sq = jnp.sum(jnp.where(keep, kt * kt, 0.0), axis=-1, keepdims=True)
        return jnp.maximum(c, jnp.max(sq, axis=0, keepdims=True))

    return jnp.sqrt(lax.fori_loop(0, k_ref.shape[0] // tk, body, jnp.zeros((1, 1), F32)))


def _diff_kernel(tab_ref, q_ref, k_ref, v_ref, bias_ref, lq_ref, lk_ref, g_ref, o_ref, knorm_ref,
                 *, tb, nb, tk, n_near, scale, lambda_init):
    h = pl.program_id(0)
    i = pl.program_id(1)
    t = nb * tb
    q = _prescale(q_ref[...], scale * LOG2E)
    lane = lax.broadcasted_iota(jnp.int32, (t, HEAD_DIM), 1)
    zero = jnp.zeros_like(q)
    q2 = jnp.concatenate([jnp.where(lane < DIFF_QK_DIM, q, zero),
                          jnp.where(lane >= DIFF_QK_DIM, q, zero)], axis=0)
    b_far = tab_ref[N_BUCKETS - 1, h] * LOG2E

    @pl.when(i == 0)
    def _():
        knorm_ref[0:1, :] = jnp.broadcast_to(_max_key_norm(k_ref, tk, 0, DIFF_QK_DIM), (1, LANES))
        knorm_ref[1:2, :] = jnp.broadcast_to(_max_key_norm(k_ref, tk, DIFF_QK_DIM, HEAD_DIM), (1, LANES))

    def scores(j):
        return _nt_dot(q2, k_ref[pl.ds(pl.multiple_of(j * tk, tk), tk), :])

    def values(j):
        return v_ref[pl.ds(pl.multiple_of(j * tk, tk), tk), :]

    def near_bias(j):
        b = _near_bias(bias_ref, [i * nb + b for b in range(nb)], j, tb, tk, n_near)
        return jnp.concatenate([b, b], axis=0)

    def process(s, j, carry, near):
        return _online_softmax_step(s + (near_bias(j) if near else b_far), values(j), carry)

    q2f = q2.astype(F32)
    qnorm = jnp.sqrt(jnp.sum(q2f * q2f, axis=-1, keepdims=True))
    knorm = jnp.concatenate([jnp.broadcast_to(knorm_ref[0:1, 0:1], (t, 1)),
                             jnp.broadcast_to(knorm_ref[1:2, 0:1], (t, 1))], axis=0)
    bound = _logit_bound(qnorm, knorm, tab_ref, h)
    l, acc = _shifted_or_running_sweep(i, t, tk, scores, near_bias, values, process, bound, b_far, 2 * t)
    o12 = acc / l
    lam_e = jnp.exp(jnp.sum(lq_ref[...] * lk_ref[...], axis=-1, keepdims=True))
    lam = lam_e[0:1, :] - lam_e[1:2, :] + lambda_init
    o = o12[:t, :] - lam * o12[t:, :]
    o = o * lax.rsqrt(jnp.mean(o * o, axis=-1, keepdims=True) + LN_EPS) * g_ref[...]
    o_ref[...] = (o * (1.0 - lambda_init)).astype(o_ref.dtype)


def _sb_kernel(q_ref, k_ref, v_ref, o_ref, *, t, nq, scale):
    g = pl.program_id(1)

    def later_matrix(n):
        r = lax.broadcasted_iota(jnp.int32, (n, n), 0)
        c = lax.broadcasted_iota(jnp.int32, (n, n), 1)
        return jnp.where(r > c, 1.0, 0.0).astype(BF16)

    def block(q, off, nk, rem, acc, shift):
        k = k_ref[pl.ds(off, nk), :]
        v = v_ref[pl.ds(off, nk), :]
        z = _nt_dot(q, k) * scale
        soft = jnp.log1p(jnp.exp(-jnp.abs(z)))
        log_sig = jnp.minimum(z, 0.0) - soft
        log_rem = jnp.minimum(-z, 0.0) - soft
        if shift is not None:
            rows = lax.broadcasted_iota(jnp.int32, (t, nk), 0)
            cols = lax.broadcasted_iota(jnp.int32, (t, nk), 1)
            mask = cols < rows + shift
            log_rem = jnp.where(mask, log_rem, 0.0)
        hi, lo = _split_bf16(log_rem)
        later = later_matrix(nk)
        after = (jnp.dot(hi, later, preferred_element_type=F32)
                 + jnp.dot(lo, later, preferred_element_type=F32))
        a = jnp.exp(log_sig + after + rem)
        if shift is not None:
            a = jnp.where(mask, a, 0.0)
        acc = acc + jnp.dot(a.astype(BF16), v, preferred_element_type=F32)
        rem = rem + jnp.sum(log_rem, axis=-1, keepdims=True)
        return rem, acc

    def live(rem):
        return (jnp.max(rem) > SB_CUTOFF).astype(jnp.int32)

    state = []
    for u in range(nq):
        i = g * nq + u
        q = q_ref[u * t:(u + 1) * t, :]
        first = jnp.maximum(i - 1, 0)
        rem, acc = block(q, pl.multiple_of(first * t, t), 2 * t, jnp.zeros((t, 1), F32),
                         jnp.zeros((t, HEAD_DIM), F32), (i - first) * t)
        state.append((i, q, rem, acc))

    for u, (i, q, rem, acc) in enumerate(state):
        def cond(c):
            j, go, _, _ = c
            return jnp.logical_and(j >= 0, go > 0)

        def body(c, q=q):
            j, _, rem, acc = c
            rem, acc = block(q, pl.multiple_of(j * t, t), t, rem, acc, None)
            return j - 1, live(rem), rem, acc

        _, _, _, acc = lax.while_loop(cond, body, (i - 2, live(rem), rem, acc))
        o_ref[u * t:(u + 1) * t, :] = acc.astype(o_ref.dtype)


def _attention_call(kernel, h, out_cols, col0, extra_in, extra_specs, scratch, name, t):
    s = h.shape[0]
    nh = N_HEADS_GROUP
    in_specs = list(extra_specs[0]) + [
        pl.BlockSpec((t, HEAD_DIM), lambda hd, i: (i, col0 + hd)),
        pl.BlockSpec((s, HEAD_DIM), lambda hd, i: (0, col0 + nh + hd)),
        pl.BlockSpec((s, HEAD_DIM), lambda hd, i: (0, col0 + 2 * nh + hd)),
    ] + list(extra_specs[1])
    args = list(extra_in[0]) + [h, h, h] + list(extra_in[1])
    return pl.pallas_call(
        kernel,
        out_shape=jax.ShapeDtypeStruct((s, out_cols), BF16),
        grid=(nh, s // t),
        in_specs=in_specs,
        out_specs=pl.BlockSpec((t, HEAD_DIM), lambda hd, i: (i, hd)),
        scratch_shapes=scratch,
        compiler_params=_cparams(2, VMEM_LIMIT),
        name=name,
    )(*args)


def _route(x, w, b):
    x_hi, x_lo = _split_bf16(x)
    w_hi, w_lo = _split_bf16(w)
    logits = (jnp.dot(x_hi, w_hi, preferred_element_type=F32) + jnp.dot(x_lo, w_hi, preferred_element_type=F32)
              + jnp.dot(x_hi, w_lo, preferred_element_type=F32)) + b
    tm, n = logits.shape
    lane = lax.broadcasted_iota(jnp.int32, (tm, n), 1)
    real = lane < N_EXPERTS
    logits = jnp.where(real, logits, -jnp.inf)
    e = jnp.exp(logits - jnp.max(logits, axis=-1, keepdims=True))
    aff = e / jnp.sum(e, axis=-1, keepdims=True)
    group = lane // EXPERTS_PER_GROUP
    best = jnp.full((tm, 1), -jnp.inf, F32)
    g_sel = jnp.zeros((tm, 1), jnp.int32)
    for g in range(N_GROUPS):
        gmax = jnp.max(jnp.where(group == g, aff, -jnp.inf), axis=-1, keepdims=True)
        better = gmax > best
        g_sel = jnp.where(better, g, g_sel)
        best = jnp.where(better, gmax, best)
    in_group = jnp.logical_and(group == g_sel, real)
    cand = jnp.where(in_group, aff, -jnp.inf)
    v1 = jnp.max(cand, axis=-1, keepdims=True)
    i1 = jnp.min(jnp.where(cand == v1, lane, n), axis=-1, keepdims=True)
    cand2 = jnp.where(lane == i1, -jnp.inf, cand)
    v2 = jnp.max(cand2, axis=-1, keepdims=True)
    i2 = jnp.min(jnp.where(cand2 == v2, lane, n), axis=-1, keepdims=True)
    tot = v1 + v2
    chosen = jnp.where(jnp.logical_or(lane == i1, lane == i2), 1.0, 0.0)
    r = lax.broadcasted_iota(jnp.int32, (tm, tm), 0)
    c = lax.broadcasted_iota(jnp.int32, (tm, tm), 1)
    earlier = jnp.where(c < r, 1.0, 0.0).astype(BF16)
    before = jnp.dot(earlier, chosen.astype(BF16), preferred_element_type=F32)
    rank1 = jnp.sum(jnp.where(lane == i1, before, 0.0), axis=-1, keepdims=True)
    rank2 = jnp.sum(jnp.where(lane == i2, before, 0.0), axis=-1, keepdims=True)
    route = jnp.where(lane == 0, v1 / tot,
                      jnp.where(lane == 1, v2 / tot,
                                jnp.where(lane == 2, i1.astype(F32),
                                          jnp.where(lane == 3, i2.astype(F32),
                                                    jnp.where(lane == 4, rank1,
                                                              jnp.where(lane == 5, rank2, 0.0))))))
    return route, jnp.sum(chosen, axis=0, keepdims=True)


def _dispatch_plan(route, counts, tm):
    s = route.shape[0]
    nt = s // tm
    n_tiles = (MOE_TOP_K * s) // tm + N_EXPERTS
    ids = jnp.arange(N_EXPERTS, dtype=jnp.int32)
    expert = route[:, 2:2 + MOE_TOP_K].astype(jnp.int32).reshape(nt, tm, MOE_TOP_K)
    rank = route[:, 4:4 + MOE_TOP_K].astype(jnp.int32).reshape(nt, tm, MOE_TOP_K)
    cnt = counts[:, 0, :N_EXPERTS].astype(jnp.int32)
    tile_base = jnp.cumsum(cnt, axis=0) - cnt
    padded = -(-jnp.sum(cnt, axis=0) // tm) * tm
    seg_end = jnp.cumsum(padded)
    base = (seg_end - padded)[None, :] + tile_base
    pick = expert[..., None] == ids
    pair_row = (jnp.sum(jnp.where(pick, base[:, None, None, :], 0), axis=-1) + rank).reshape(s, MOE_TOP_K)
    tile_start = jnp.arange(n_tiles, dtype=jnp.int32) * tm
    tile_expert = jnp.minimum(jnp.sum((tile_start[:, None] >= seg_end[None, :]).astype(jnp.int32), axis=1),
                              N_EXPERTS - 1).astype(jnp.int32)
    tile_live = (tile_start < seg_end[-1]).astype(jnp.int32)
    row_token = (jnp.arange(n_tiles * tm, dtype=jnp.int32) % s).at[pair_row.reshape(-1)].set(
        jnp.arange(MOE_TOP_K * s, dtype=jnp.int32) // MOE_TOP_K)
    return pair_row, row_token, tile_expert, tile_live


def _moe_group_kernel(te_ref, live_ref, x_ref, wg_ref, wu_ref, wd_ref, o_ref, xb_ref, acc_ref):
    g = pl.program_id(0)
    f = pl.program_id(1)
    last = f == pl.num_programs(1) - 1
    live = live_ref[g] > 0

    @pl.when(jnp.logical_and(live, f == 0))
    def _():
        lo, hi = _unpack_bf16_pairs(x_ref[...])
        xb_ref[...] = jnp.concatenate([lo.astype(BF16), hi.astype(BF16)], axis=1)

    @pl.when(jnp.logical_and(jnp.logical_not(live), last))
    def _():
        o_ref[...] = jnp.zeros(o_ref.shape, o_ref.dtype)

    @pl.when(live)
    def _():
        xb = xb_ref[...]
        a = jnp.dot(xb, wg_ref[...], preferred_element_type=F32)
        u = jnp.dot(xb, wu_ref[...], preferred_element_type=F32)
        hid = ((a * jax.nn.sigmoid(a)) * u).astype(BF16)
        y = jnp.dot(hid, wd_ref[...], preferred_element_type=F32)

        @pl.when(f == 0)
        def _():
            acc_ref[...] = y

        @pl.when(f > 0)
        def _():
            acc_ref[...] += y

        @pl.when(last)
        def _():
            o_ref[...] = _pack_bf16_pairs(acc_ref[...])


def _moe_group_ffn(xs, tile_expert, tile_live, wg, wu, wd, tm, tf):
    rows, half = xs.shape
    d = 2 * half
    fdim = wg.shape[2]
    grid_spec = pltpu.PrefetchScalarGridSpec(
        num_scalar_prefetch=2,
        grid=(rows // tm, fdim // tf),
        in_specs=[pl.BlockSpec((tm, half), lambda g, f, te, lv: (g, 0)),
                  pl.BlockSpec((None, d, tf), lambda g, f, te, lv: (te[g], 0, f)),
                  pl.BlockSpec((None, d, tf), lambda g, f, te, lv: (te[g], 0, f)),
                  pl.BlockSpec((None, tf, d), lambda g, f, te, lv: (te[g], f, 0))],
        out_specs=pl.BlockSpec((tm, half), lambda g, f, te, lv: (g, 0)),
        scratch_shapes=[pltpu.VMEM((tm, d), BF16), pltpu.VMEM((tm, d), F32)])
    return pl.pallas_call(
        _moe_group_kernel,
        out_shape=jax.ShapeDtypeStruct((rows, half), jnp.uint32),
        grid_spec=grid_spec,
        compiler_params=_cparams(2, VMEM_LIMIT),
        name="moe_group_ffn",
    )(tile_expert, tile_live, xs, wg, wu, wd)


def _moe_combine_ln_kernel(x_ref, r_ref, y0_ref, y1_ref, g_ref, b_ref, o_ref):
    r = r_ref[...]
    lo0, hi0 = _unpack_bf16_pairs(y0_ref[...])
    lo1, hi1 = _unpack_bf16_pairs(y1_ref[...])
    ff = jnp.concatenate([r[:, 0:1] * lo0 + r[:, 1:2] * lo1, r[:, 0:1] * hi0 + r[:, 1:2] * hi1], axis=1)
    o_ref[...] = _layer_norm(ALPHA * x_ref[...] + ff, g_ref[...], b_ref[...])


def _moe_combine_ln(x, route, y, g, b, tm):
    s, d = x.shape
    nt = s // tm
    return pl.pallas_call(
        _moe_combine_ln_kernel,
        out_shape=jax.ShapeDtypeStruct((s, d), F32),
        grid=(nt,),
        in_specs=[pl.BlockSpec((tm, d), lambda i: (i, 0)),
                  pl.BlockSpec((tm, LANES), lambda i: (i, 0)),
                  pl.BlockSpec((tm, d // 2), lambda i: (i, 0)),
                  pl.BlockSpec((tm, d // 2), lambda i: (i + nt, 0)),
                  pl.BlockSpec((1, d), lambda i: (0, 0)),
                  pl.BlockSpec((1, d), lambda i: (0, 0))],
        out_specs=pl.BlockSpec((tm, d), lambda i: (i, 0)),
        compiler_params=_cparams(1, VMEM_LIMIT),
        name="moe_combine_ln",
    )(x, route, y, y, g.reshape(1, d), b.reshape(1, d))


def _sc_row_gather(src, idx):
    n = idx.shape[0]
    d = src.shape[1]
    workers = SC_CORES * SC_SUBCORES
    per_worker = n // workers
    assert n % (workers * SC_GATHER_ROWS) == 0
    mesh = plsc.VectorSubcoreMesh(core_axis_name="c", subcore_axis_name="s")

    ch = SC_GATHER_ROWS
    n_chunks = per_worker // ch
    assert n_chunks % 2 == 0

    @functools.partial(
        pl.kernel, mesh=mesh, out_type=jax.ShapeDtypeStruct((n, d), src.dtype),
        scratch_types=[pltpu.VMEM((per_worker,), jnp.int32),
                       pltpu.VMEM((ch, d), src.dtype), pltpu.VMEM((ch, d), src.dtype),
                       pltpu.SemaphoreType.DMA, pltpu.SemaphoreType.DMA],
        name="sc_row_gather")
    def gather(src_hbm, idx_hbm, out_hbm, idx_v, rows_a, rows_b, sem_a, sem_b):
        base = (lax.axis_index("s") * SC_CORES + lax.axis_index("c")) * per_worker
        pltpu.sync_copy(idx_hbm.at[pl.ds(base, per_worker)], idx_v)

        def start(c, rows, sem):
            pltpu.async_copy(src_hbm.at[idx_v.at[pl.ds(c * ch, ch)]], rows, sem)

        def finish(c, rows, sem):
            pltpu.make_async_copy(src_hbm.at[pl.ds(0, ch)], rows, sem).wait()
            pltpu.sync_copy(rows, out_hbm.at[pl.ds(base + c * ch, ch)])

        start(0, rows_a, sem_a)

        @pl.loop(0, n_chunks, step=2)
        def _(c):
            start(c + 1, rows_b, sem_b)
            finish(c, rows_a, sem_a)

            @pl.when(c + 2 < n_chunks)
            def _():
                start(c + 2, rows_a, sem_a)

            finish(c + 1, rows_b, sem_b)

    return gather(src, idx)


def _moe_ln(x, x_packed, route, counts, wg, wu, wd, g, b, tm):
    pair_row, row_token, tile_expert, tile_live = _dispatch_plan(route, counts, tm)
    xs = _sc_row_gather(x_packed, row_token)
    ys = _moe_group_ffn(xs, tile_expert, tile_live, wg, wu, wd, tm, D_EXPERT // 2)
    slot_major = pair_row.T.reshape(-1)
    y = _sc_row_gather(ys, slot_major)
    return _moe_combine_ln(x, route, y, g, b, tm)


def kernel(x, w_in_even, w_out_even, forget_bias, w_in_odd, w_out_odd, lambda_q, lambda_k,
           subln_gain, rel_bias, w_router, b_router, w_gate, w_up, w_down, ln_gain, ln_bias):
    bsz, s, d = x.shape
    assert bsz == 1 and d == D_MODEL and s % MOBA_BLOCK == 0
    t = ATT_TILE
    nh = N_HEADS_GROUP
    gw = GROUP_WIDTH
    tm_proj = min(1024, s)
    tm_row = min(512, s)
    tk = min(ATT_KEYS, s)
    n_near = _near_tile_count(t)
    assert s % tk == 0 and tk // t + 1 >= n_near
    nblk = s // MOBA_BLOCK
    nbp = -(-nblk // LANES) * LANES

    x2 = x.reshape(s, d)
    bias_tiles = _bias_tiles(rel_bias, t, n_near)
    tab_spec = pl.BlockSpec(memory_space=pltpu.SMEM)
    bias_spec = pl.BlockSpec((None, n_near + 2, t, t), lambda hd, i: (hd, 0, 0, 0))
    block_onehot = (jnp.arange(s, dtype=jnp.int32)[:, None] // MOBA_BLOCK
                    == jnp.arange(nbp, dtype=jnp.int32)[None, :]).astype(BF16)

    for layer in range(DEPTH):
        li = layer // 2
        if layer % 2 == 0:
            w = w_in_even[li]
            fcol = 3 * gw
            w_main = jnp.concatenate([w[:, :fcol], w[:, fcol + nh:]], axis=1).astype(BF16)
            w_f = jnp.zeros((d, LANES), BF16).at[:, :nh].set(w[:, fcol:fcol + nh].astype(BF16))
            h = _in_proj(x2, w_main, BF16, tm_proj, 512)
            f = _in_proj(x2, w_f, F32, tm_proj, LANES)
            c = _decay_cumsum(f[:, :nh].T, forget_bias[li])
            oa = _attention_call(
                functools.partial(_fox_kernel, t=FOX_TQ, tk=min(FOX_TK, s), scale=HEAD_DIM ** -0.5), h, gw, 0,
                ([c[:, min(FOX_TK, s) - 1::min(FOX_TK, s)], c[:, ::FOX_TQ]],
                 [c.reshape(nh, 1, s), c.reshape(nh, s, 1)]),
                ([tab_spec, tab_spec],
                 [pl.BlockSpec((None, 1, s), lambda hd, i: (hd, 0, 0)),
                  pl.BlockSpec((None, FOX_TQ, 1), lambda hd, i: (hd, i, 0))]),
                [pltpu.VMEM((SUBLANES, LANES), F32)],
                "fox_attention", FOX_TQ)
            ob = _attention_call(
                functools.partial(_moba_kernel, tb=t, nb=MOBA_Q_BLOCKS, tk=tk, nblk=nblk, n_near=n_near,
                                  scale=HEAD_DIM ** -0.5),
                h, gw, 3 * nh,
                ([rel_bias], [bias_tiles, block_onehot]),
                ([tab_spec], [bias_spec, pl.BlockSpec((s, nbp), lambda hd, i: (0, 0))]),
                [pltpu.VMEM((nbp, HEAD_DIM), F32), pltpu.VMEM((SUBLANES, LANES), F32)],
                "moba_attention", MOBA_Q_BLOCKS * t)
            w_out = w_out_even[li]
        else:
            lambda_init = 0.8 - 0.6 * math.exp(-0.3 * layer)
            h = _in_proj(x2, w_in_odd[li].astype(BF16), BF16, tm_proj, 512)
            small = lambda shape: pl.BlockSpec(shape, lambda hd, i: (0, 0))
            oa = _attention_call(
                functools.partial(_diff_kernel, tb=t, nb=DIFF_Q_BLOCKS, tk=tk, n_near=n_near,
                                  scale=DIFF_QK_DIM ** -0.5, lambda_init=lambda_init),
                h, gw, 0,
                ([rel_bias], [bias_tiles, lambda_q[li], lambda_k[li], subln_gain[li].reshape(1, HEAD_DIM)]),
                ([tab_spec], [bias_spec, small((2, DIFF_QK_DIM)), small((2, DIFF_QK_DIM)),
                              small((1, HEAD_DIM))]),
                [pltpu.VMEM((SUBLANES, LANES), F32)],
                "diff_attention", DIFF_Q_BLOCKS * t)
            ob = _attention_call(
                functools.partial(_sb_kernel, t=t, nq=SB_Q_TILES, scale=HEAD_DIM ** -0.5), h, gw, 3 * nh,
                ([], []), ([], []), [], "stickbreak_attention", SB_Q_TILES * t)
            w_out = w_out_odd[li]
        x2, x2_packed = _out_proj_ln(oa, ob, w_out.astype(BF16), x2, ln_gain[layer, 0], ln_bias[layer, 0],
                                     tm_row)
        route, counts = _router(x2, w_router, b_router, tm_row)
        x2 = _moe_ln(x2, x2_packed, route, counts, w_gate[layer].astype(BF16), w_up[layer].astype(BF16),
                     w_down[layer].astype(BF16), ln_gain[layer, 1], ln_bias[layer, 1], tm_row)
    return x2.reshape(bsz, s, d)
```

```python
import functools
import math

import numpy as np
import jax
import jax.numpy as jnp
from jax import lax
from jax.experimental import pallas as pl
from jax.experimental.pallas import tpu as pltpu
from jax.experimental.pallas import tpu_sc as plsc

F32 = jnp.float32
BF16 = jnp.bfloat16

D_MODEL = 2048
DEPTH = 2
HEAD_DIM = 128
N_HEADS_GROUP = 8
GROUP_WIDTH = N_HEADS_GROUP * HEAD_DIM
DIFF_QK_DIM = HEAD_DIM // 2
MOBA_BLOCK = 256
MOBA_TOPK = 3
N_BUCKETS = 32
MAX_EXACT = N_BUCKETS // 2
MAX_DISTANCE = 1024
N_EXPERTS = 16
N_GROUPS = 4
EXPERTS_PER_GROUP = N_EXPERTS // N_GROUPS
MOE_TOP_K = 2
SC_CORES = 2
SC_SUBCORES = 16
SC_GATHER_ROWS = 32
SUBLANES = 8
D_EXPERT = D_MODEL // 2
MOE_FF_TILE = D_EXPERT
ALPHA = (2.0 * DEPTH) ** 0.25
LN_EPS = 1e-5
NEG_INF = -1e30
LOG2E = math.log2(math.e)
LANES = 128
ATT_TILE = 256
ATT_KEYS = 1024
FOX_TQ = 1024
FOX_TK = 1024
MOBA_Q_BLOCKS = 4
SB_Q_TILES = 2
DIFF_Q_BLOCKS = 4
SHIFT_SLACK = 96.0
FLUSH_EXPONENT = -130.0
SB_CUTOFF = -104.0
VMEM_LIMIT = 56 * 1024 * 1024


def _cparams(n_axes, vmem=None):
    return pltpu.CompilerParams(dimension_semantics=("arbitrary",) * n_axes,
                                vmem_limit_bytes=vmem)


def _nt_dot(a, b):
    return lax.dot_general(a, b, (((1,), (1,)), ((), ())), preferred_element_type=F32)


def _split_bf16(x):
    hi = x.astype(BF16)
    lo = (x - hi.astype(F32)).astype(BF16)
    return hi, lo


def _pack_bf16_pairs(x):
    n = x.shape[1] // 2
    lo = pltpu.bitcast(x[:, :n].astype(BF16).astype(F32), jnp.uint32)
    hi = pltpu.bitcast(x[:, n:].astype(BF16).astype(F32), jnp.uint32)
    return jnp.bitwise_or(lax.shift_right_logical(lo, jnp.uint32(16)),
                          jnp.bitwise_and(hi, jnp.uint32(0xFFFF0000)))


def _unpack_bf16_pairs(w):
    lo = pltpu.bitcast(lax.shift_left(w, jnp.uint32(16)), F32)
    hi = pltpu.bitcast(jnp.bitwise_and(w, jnp.uint32(0xFFFF0000)), F32)
    return lo, hi


def _layer_norm(z, g, b):
    mu = jnp.mean(z, axis=-1, keepdims=True)
    zc = z - mu
    var = jnp.mean(zc * zc, axis=-1, keepdims=True)
    return zc * lax.rsqrt(var + LN_EPS) * g + b


def _inproj_kernel(x_ref, w_ref, o_ref, xb_ref):
    @pl.when(pl.program_id(1) == 0)
    def _():
        xb_ref[...] = x_ref[...].astype(BF16)

    o_ref[...] = jnp.dot(xb_ref[...], w_ref[...], preferred_element_type=F32).astype(o_ref.dtype)


def _in_proj(x, w, out_dtype, tm, tn):
    s, d = x.shape
    n = w.shape[1]
    return pl.pallas_call(
        _inproj_kernel,
        out_shape=jax.ShapeDtypeStruct((s, n), out_dtype),
        grid=(s // tm, n // tn),
        in_specs=[pl.BlockSpec((tm, d), lambda i, j: (i, 0)),
                  pl.BlockSpec((d, tn), lambda i, j: (0, j))],
        out_specs=pl.BlockSpec((tm, tn), lambda i, j: (i, j)),
        scratch_shapes=[pltpu.VMEM((tm, d), BF16)],
        compiler_params=_cparams(2, VMEM_LIMIT),
        name="in_proj",
    )(x, w)


def _outproj_ln_kernel(oa_ref, ob_ref, wa_ref, wb_ref, x_ref, g_ref, b_ref, o_ref, packed_ref):
    y = jnp.dot(oa_ref[...], wa_ref[...], preferred_element_type=F32)
    y = y + jnp.dot(ob_ref[...], wb_ref[...], preferred_element_type=F32)
    x1 = _layer_norm(ALPHA * x_ref[...] + y, g_ref[...], b_ref[...])
    o_ref[...] = x1
    packed_ref[...] = _pack_bf16_pairs(x1)


def _out_proj_ln(oa, ob, w, x, g, b, tm):
    s, d = x.shape
    gw = oa.shape[1]
    return pl.pallas_call(
        _outproj_ln_kernel,
        out_shape=(jax.ShapeDtypeStruct((s, d), F32), jax.ShapeDtypeStruct((s, d // 2), jnp.uint32)),
        grid=(s // tm,),
        in_specs=[pl.BlockSpec((tm, gw), lambda i: (i, 0)),
                  pl.BlockSpec((tm, gw), lambda i: (i, 0)),
                  pl.BlockSpec((gw, d), lambda i: (0, 0)),
                  pl.BlockSpec((gw, d), lambda i: (1, 0)),
                  pl.BlockSpec((tm, d), lambda i: (i, 0)),
                  pl.BlockSpec((1, d), lambda i: (0, 0)),
                  pl.BlockSpec((1, d), lambda i: (0, 0))],
        out_specs=(pl.BlockSpec((tm, d), lambda i: (i, 0)),
                   pl.BlockSpec((tm, d // 2), lambda i: (i, 0))),
        compiler_params=_cparams(1, VMEM_LIMIT),
        name="out_proj_ln",
    )(oa, ob, w, w, x, g.reshape(1, d), b.reshape(1, d))


def _router_kernel(x_ref, w_ref, b_ref, route_ref, count_ref):
    route, counts = _route(x_ref[...], w_ref[...], b_ref[...])
    route_ref[...] = route
    count_ref[...] = jnp.broadcast_to(counts, count_ref.shape)


def _router(x, w_router, b_router, tm):
    s, d = x.shape
    wr = jnp.zeros((d, LANES), F32).at[:, :N_EXPERTS].set(w_router)
    br = jnp.zeros((1, LANES), F32).at[0, :N_EXPERTS].set(b_router)
    return pl.pallas_call(
        _router_kernel,
        out_shape=(jax.ShapeDtypeStruct((s, LANES), F32),
                   jax.ShapeDtypeStruct((s // tm, SUBLANES, LANES), F32)),
        grid=(s // tm,),
        in_specs=[pl.BlockSpec((tm, d), lambda i: (i, 0)),
                  pl.BlockSpec((d, LANES), lambda i: (0, 0)),
                  pl.BlockSpec((1, LANES), lambda i: (0, 0))],
        out_specs=(pl.BlockSpec((tm, LANES), lambda i: (i, 0)),
                   pl.BlockSpec((None, SUBLANES, LANES), lambda i: (i, 0, 0))),
        compiler_params=_cparams(1, VMEM_LIMIT),
        name="router",
    )(x, wr, br)


def _decay_cumsum_kernel(f_ref, b_ref, c_ref):
    nh, s = f_ref.shape
    rows = lax.broadcasted_iota(jnp.int32, (LANES, LANES), 0)
    cols = lax.broadcasted_iota(jnp.int32, (LANES, LANES), 1)
    upper = jnp.where(rows <= cols, 1.0, 0.0).astype(F32)
    bias = b_ref[...]

    def body(n, carry):
        off = pl.multiple_of(n * LANES, LANES)
        z = f_ref[:, pl.ds(off, LANES)] + bias
        logf = jnp.minimum(z, 0.0) - jnp.log1p(jnp.exp(-jnp.abs(z)))
        c = jnp.dot(logf, upper, preferred_element_type=F32,
                    precision=lax.Precision.HIGHEST) + carry
        c_ref[:, pl.ds(off, LANES)] = c * LOG2E
        return carry + jnp.sum(logf, axis=-1, keepdims=True)

    lax.fori_loop(0, s // LANES, body, jnp.zeros((nh, 1), F32))


def _decay_cumsum(f_t, bias):
    nh, s = f_t.shape
    return pl.pallas_call(
        _decay_cumsum_kernel,
        out_shape=jax.ShapeDtypeStruct((nh, s), F32),
        name="decay_cumsum",
    )(f_t, bias.reshape(nh, 1))


def _t5_bucket_np(dist):
    n = np.maximum(dist, 0)
    nf = np.maximum(n, 1).astype(np.float32)
    ratio = np.log(nf / np.float32(MAX_EXACT)) / np.float32(math.log(MAX_DISTANCE / MAX_EXACT))
    large = MAX_EXACT + (ratio.astype(np.float32) * np.float32(N_BUCKETS - MAX_EXACT)).astype(np.int32)
    large = np.minimum(large, N_BUCKETS - 1)
    return np.where(n < MAX_EXACT, n, large).astype(np.int32)


def _near_tile_count(t):
    d = np.arange(0, 4 * MAX_DISTANCE, dtype=np.int64)
    not_last = np.nonzero(_t5_bucket_np(d) != N_BUCKETS - 1)[0]
    d_sat = int(not_last.max()) + 1
    n = 1
    while (n - 1) * t + 1 < d_sat:
        n += 1
    return n


def _bucket_tiles_np(t, n_near):
    r = np.arange(t)[:, None]
    c = np.arange(t)[None, :]
    tiles = []
    for delta in range(n_near):
        dist = delta * t + r - c
        tiles.append(np.where(dist >= 0, _t5_bucket_np(dist), -1))
    tiles.append(np.full((t, t), N_BUCKETS - 1))
    tiles.append(np.full((t, t), -1))
    return np.stack(tiles).astype(np.int32)


def _bias_tiles_kernel(tab_ref, idx_ref, o_ref):
    h = pl.program_id(0)
    idx = idx_ref[...]
    out = jnp.full(idx.shape, NEG_INF, F32)
    for b in range(N_BUCKETS):
        out = jnp.where(idx == b, tab_ref[b, h] * LOG2E, out)
    o_ref[...] = out


def _bias_tiles(rel_bias, t, n_near):
    nh = rel_bias.shape[1]
    idx = jnp.asarray(_bucket_tiles_np(t, n_near))
    n_tiles = idx.shape[0]
    return pl.pallas_call(
        _bias_tiles_kernel,
        out_shape=jax.ShapeDtypeStruct((nh, n_tiles, t, t), F32),
        grid=(nh, n_tiles),
        in_specs=[pl.BlockSpec(memory_space=pltpu.SMEM),
                  pl.BlockSpec((None, t, t), lambda h, n: (n, 0, 0))],
        out_specs=pl.BlockSpec((None, None, t, t), lambda h, n: (h, n, 0, 0)),
        compiler_params=_cparams(2),
        name="bias_tiles",
    )(rel_bias, idx)


def _online_softmax_step(s, v, carry):
    m, l, acc = carry
    m_new = jnp.maximum(m, jnp.max(s, axis=-1, keepdims=True))
    alpha = jnp.exp2(m - m_new)
    p = jnp.exp2(s - m_new)
    l = alpha * l + jnp.sum(p, axis=-1, keepdims=True)
    acc = alpha * acc + jnp.dot(p.astype(BF16), v, preferred_element_type=F32)
    return m_new, l, acc


def _prescale(q, c):
    return (q.astype(F32) * c).astype(BF16)


def _near_bias(bias_ref, blocks, j, tb, tk, n_near):
    nsub = tk // tb
    rows = []
    for i in blocks:
        tiles = []
        for b in range(nsub):
            delta = i - (j * nsub + b)
            tiles.append(bias_ref[jnp.where(delta < 0, n_near + 1, jnp.minimum(delta, n_near))])
        rows.append(jnp.concatenate(tiles, axis=1))
    return rows[0] if len(rows) == 1 else jnp.concatenate(rows, axis=0)


def _biased_sweep(i, t, tk, scores, process, init):
    jd = (i * t) // tk
    carry = process(scores(jd), jd, init, True)
    lo = jnp.maximum(jd - 1, 0)
    carry = lax.fori_loop(lo, jd, lambda j, c: process(scores(j), j, c, True), carry)
    return lax.fori_loop(0, lo, lambda j, c: process(scores(j), j, c, False), carry)


def _softmax_init(rows):
    return (jnp.full((rows, 1), NEG_INF, F32), jnp.zeros((rows, 1), F32),
            jnp.zeros((rows, HEAD_DIM), F32))


def _fox_kernel(cend_ref, cstart_ref, q_ref, k_ref, v_ref, c_ref, ccol_ref, o_ref, knorm_ref,
                *, t, tk, scale):
    h = pl.program_id(0)
    i = pl.program_id(1)

    @pl.when(i == 0)
    def _():
        knorm_ref[...] = jnp.broadcast_to(_max_key_norm(k_ref, tk, 0, HEAD_DIM), knorm_ref.shape)

    q = _prescale(q_ref[...], scale * LOG2E)
    rows = lax.broadcasted_iota(jnp.int32, (t, tk), 0)
    cols = lax.broadcasted_iota(jnp.int32, (t, tk), 1)

    def scores(j):
        off = pl.multiple_of(j * tk, tk)
        return _nt_dot(q, k_ref[pl.ds(off, tk), :]) - c_ref[:, pl.ds(off, tk)]

    def values(j):
        return v_ref[pl.ds(pl.multiple_of(j * tk, tk), tk), :]

    def process(s, j, carry):
        return _online_softmax_step(s, values(j), carry)

    jd = (i * t) // tk
    s_diag = jnp.where(cols <= rows + (i * t - jd * tk), scores(jd), NEG_INF)

    qf = q.astype(F32)
    qnorm = jnp.sqrt(jnp.sum(qf * qf, axis=-1, keepdims=True))
    bound = qnorm * knorm_ref[0:1, 0:1] * (1.0 + 2.0 ** -9) + 2.0 ** -6 - ccol_ref[...]
    slack = jnp.max(bound - jnp.max(s_diag, axis=-1, keepdims=True))

    def shifted_sweep(_):
        def step(s, j, carry):
            l, acc = carry
            p = jnp.exp2(s - bound)
            return (l + jnp.sum(p, axis=-1, keepdims=True),
                    acc + jnp.dot(p.astype(BF16), values(j), preferred_element_type=F32))

        c_first_row = cstart_ref[h, i]
        first_live = jnp.int32(0)
        for j in range(cend_ref.shape[1]):
            dead = jnp.logical_and(j < jd, c_first_row - cend_ref[h, j] < FLUSH_EXPONENT)
            first_live = first_live + dead.astype(jnp.int32)
        carry = step(s_diag, jd, _softmax_init(t)[1:])
        return _paired_loop(jd - first_live, lambda n, c: step(scores(first_live + n), first_live + n, c),
                            carry)

    def running_max_sweep(_):
        carry = _online_softmax_step(s_diag, values(jd), _softmax_init(t))
        return lax.fori_loop(0, jd, lambda j, c: process(scores(j), j, c), carry)[1:]

    l, acc = lax.cond(slack <= SHIFT_SLACK, shifted_sweep, running_max_sweep, 0)
    o_ref[...] = (acc / l).astype(o_ref.dtype)


def _moba_kernel(tab_ref, q_ref, k_ref, v_ref, bias_ref, et_ref, o_ref, kmean_ref, knorm_ref,
                 *, tb, nb, tk, nblk, n_near, scale):
    h = pl.program_id(0)
    g = pl.program_id(1)
    t = nb * tb
    nbp = kmean_ref.shape[0]

    @pl.when(g == 0)
    def _():
        kmean_ref[...] = jnp.zeros(kmean_ref.shape, F32)

        def mean_body(n, _):
            off = pl.multiple_of(n * tb, tb)
            kb = k_ref[pl.ds(off, tb), :].astype(F32)
            kmean_ref[pl.ds(n, 1), :] = jnp.sum(kb, axis=0, keepdims=True) * (1.0 / tb)
            return 0

        lax.fori_loop(0, nblk, mean_body, 0)
        knorm_ref[...] = jnp.broadcast_to(_max_key_norm(k_ref, tk, 0, HEAD_DIM), knorm_ref.shape)

    q = q_ref[...]
    km_hi, km_lo = _split_bf16(kmean_ref[...])
    gate = _nt_dot(q, km_hi) + _nt_dot(q, km_lo)
    blk = lax.broadcasted_iota(jnp.int32, (t, nbp), 1)
    own = g * nb + lax.broadcasted_iota(jnp.int32, (t, 1), 0) // tb
    eligible = blk < own
    sel = jnp.zeros((t, nbp), F32)
    for r in range(MOBA_TOPK):
        cand = jnp.where(eligible, jnp.where(sel > 0.0, -jnp.inf, gate), -jnp.inf)
        vmax = jnp.max(cand, axis=-1, keepdims=True)
        first = jnp.min(jnp.where(cand == vmax, blk, nbp), axis=-1, keepdims=True)
        take = jnp.where(r < own, 1.0, 0.0)
        sel = jnp.maximum(sel, jnp.where(blk == first, take, 0.0))

    sel = jnp.maximum(sel, jnp.where(blk == own, 1.0, 0.0))
    penalty = jnp.where(sel > 0.0, 0.0, NEG_INF).astype(BF16)
    q_aug = jnp.concatenate([_prescale(q, scale * LOG2E), penalty], axis=1)
    b_far = tab_ref[N_BUCKETS - 1, h] * LOG2E

    def scores(j):
        off = pl.multiple_of(j * tk, tk)
        k_aug = jnp.concatenate([k_ref[pl.ds(off, tk), :], et_ref[pl.ds(off, tk), :]], axis=1)
        return _nt_dot(q_aug, k_aug)

    def values(j):
        return v_ref[pl.ds(pl.multiple_of(j * tk, tk), tk), :]

    def near_bias(j):
        return _near_bias(bias_ref, [g * nb + b for b in range(nb)], j, tb, tk, n_near)

    def process(s, j, carry, near):
        return _online_softmax_step(s + (near_bias(j) if near else b_far), values(j), carry)

    qf = q_aug[:, :HEAD_DIM].astype(F32)
    qnorm = jnp.sqrt(jnp.sum(qf * qf, axis=-1, keepdims=True))
    bound = _logit_bound(qnorm, knorm_ref[0:1, 0:1], tab_ref, h)
    l, acc = _shifted_or_running_sweep(g, t, tk, scores, near_bias, values, process, bound, b_far, t)
    o_ref[...] = (acc / l).astype(o_ref.dtype)


def _paired_loop(n, body, carry):
    carry = lax.fori_loop(0, n // 2, lambda p, c: body(2 * p + 1, body(2 * p, c)), carry)
    return lax.fori_loop(0, n % 2, lambda _, c: body(n - 1, c), carry)


def _logit_bound(qnorm, knorm, tab_ref, h):
    b_max = tab_ref[0, h]
    for b in range(1, N_BUCKETS):
        b_max = jnp.maximum(b_max, tab_ref[b, h])
    return qnorm * knorm * (1.0 + 2.0 ** -9) + (b_max * LOG2E + 2.0 ** -6)


def _shifted_or_running_sweep(i, t, tk, scores, near_bias, values, process, bound, b_far, rows):
    jd = (i * t) // tk
    s_diag = scores(jd) + near_bias(jd)
    slack = jnp.max(bound - jnp.max(s_diag, axis=-1, keepdims=True))

    def shifted_sweep(_):
        def step(s, j, carry, shift):
            l, acc = carry
            p = jnp.exp2(s - shift)
            return (l + jnp.sum(p, axis=-1, keepdims=True),
                    acc + jnp.dot(p.astype(BF16), values(j), preferred_element_type=F32))

        carry = step(s_diag, jd, _softmax_init(rows)[1:], bound)
        lo = jnp.maximum(jd - 1, 0)
        carry = lax.fori_loop(lo, jd, lambda j, c: step(scores(j) + near_bias(j), j, c, bound), carry)
        far_shift = bound - b_far
        return _paired_loop(lo, lambda j, c: step(scores(j), j, c, far_shift), carry)

    def running_max_sweep(_):
        return _biased_sweep(i, t, tk, scores, process, _softmax_init(rows))[1:]

    return lax.cond(slack <= SHIFT_SLACK, shifted_sweep, running_max_sweep, 0)


def _max_key_norm(k_ref, tk, lane_lo, lane_hi):
    klane = lax.broadcasted_iota(jnp.int32, (tk, HEAD_DIM), 1)
    keep = jnp.logical_and(klane >= lane_lo, klane < lane_hi)

    def body(n, c):
        kt = k_ref[pl.ds(pl.multiple_of(n * tk, tk), tk), :].astype(F32)
        sq = jnp.sum(jnp.where(keep, kt * kt, 0.0), axis=-1, keepdims=True)
        return jnp.maximum(c, jnp.max(sq, axis=0, keepdims=True))

    return jnp.sqrt(lax.fori_loop(0, k_ref.shape[0] // tk, body, jnp.zeros((1, 1), F32)))


def _diff_kernel(tab_ref, q_ref, k_ref, v_ref, bias_ref, lq_ref, lk_ref, g_ref, o_ref, knorm_ref,
                 *, tb, nb, tk, n_near, scale, lambda_init):
    h = pl.program_id(0)
    i = pl.program_id(1)
    t = nb * tb
    q = _prescale(q_ref[...], scale * LOG2E)
    lane = lax.broadcasted_iota(jnp.int32, (t, HEAD_DIM), 1)
    zero = jnp.zeros_like(q)
    q2 = jnp.concatenate([jnp.where(lane < DIFF_QK_DIM, q, zero),
                          jnp.where(lane >= DIFF_QK_DIM, q, zero)], axis=0)
    b_far = tab_ref[N_BUCKETS - 1, h] * LOG2E

    @pl.when(i == 0)
    def _():
        knorm_ref[0:1, :] = jnp.broadcast_to(_max_key_norm(k_ref, tk, 0, DIFF_QK_DIM), (1, LANES))
        knorm_ref[1:2, :] = jnp.broadcast_to(_max_key_norm(k_ref, tk, DIFF_QK_DIM, HEAD_DIM), (1, LANES))

    def scores(j):
        return _nt_dot(q2, k_ref[pl.ds(pl.multiple_of(j * tk, tk), tk), :])

    def values(j):
        return v_ref[pl.ds(pl.multiple_of(j * tk, tk), tk), :]

    def near_bias(j):
        b = _near_bias(bias_ref, [i * nb + b for b in range(nb)], j, tb, tk, n_near)
        return jnp.concatenate([b, b], axis=0)

    def process(s, j, carry, near):
        return _online_softmax_step(s + (near_bias(j) if near else b_far), values(j), carry)

    q2f = q2.astype(F32)
    qnorm = jnp.sqrt(jnp.sum(q2f * q2f, axis=-1, keepdims=True))
    knorm = jnp.concatenate([jnp.broadcast_to(knorm_ref[0:1, 0:1], (t, 1)),
                             jnp.broadcast_to(knorm_ref[1:2, 0:1], (t, 1))], axis=0)
    bound = _logit_bound(qnorm, knorm, tab_ref, h)
    l, acc = _shifted_or_running_sweep(i, t, tk, scores, near_bias, values, process, bound, b_far, 2 * t)
    o12 = acc / l
    lam_e = jnp.exp(jnp.sum(lq_ref[...] * lk_ref[...], axis=-1, keepdims=True))
    lam = lam_e[0:1, :] - lam_e[1:2, :] + lambda_init
    o = o12[:t, :] - lam * o12[t:, :]
    o = o * lax.rsqrt(jnp.mean(o * o, axis=-1, keepdims=True) + LN_EPS) * g_ref[...]
    o_ref[...] = (o * (1.0 - lambda_init)).astype(o_ref.dtype)


def _sb_kernel(q_ref, k_ref, v_ref, o_ref, *, t, nq, scale):
    g = pl.program_id(1)

    def later_matrix(n):
        r = lax.broadcasted_iota(jnp.int32, (n, n), 0)
        c = lax.broadcasted_iota(jnp.int32, (n, n), 1)
        return jnp.where(r > c, 1.0, 0.0).astype(BF16)

    def block(q, off, nk, rem, acc, shift):
        k = k_ref[pl.ds(off, nk), :]
        v = v_ref[pl.ds(off, nk), :]
        z = _nt_dot(q, k) * scale
        soft = jnp.log1p(jnp.exp(-jnp.abs(z)))
        log_sig = jnp.minimum(z, 0.0) - soft
        log_rem = jnp.minimum(-z, 0.0) - soft
        if shift is not None:
            rows = lax.broadcasted_iota(jnp.int32, (t, nk), 0)
            cols = lax.broadcasted_iota(jnp.int32, (t, nk), 1)
            mask = cols < rows + shift
            log_rem = jnp.where(mask, log_rem, 0.0)
        hi, lo = _split_bf16(log_rem)
        later = later_matrix(nk)
        after = (jnp.dot(hi, later, preferred_element_type=F32)
                 + jnp.dot(lo, later, preferred_element_type=F32))
        a = jnp.exp(log_sig + after + rem)
        if shift is not None:
            a = jnp.where(mask, a, 0.0)
        acc = acc + jnp.dot(a.astype(BF16), v, preferred_element_type=F32)
        rem = rem + jnp.sum(log_rem, axis=-1, keepdims=True)
        return rem, acc

    def live(rem):
        return (jnp.max(rem) > SB_CUTOFF).astype(jnp.int32)

    state = []
    for u in range(nq):
        i = g * nq + u
        q = q_ref[u * t:(u + 1) * t, :]
        first = jnp.maximum(i - 1, 0)
        rem, acc = block(q, pl.multiple_of(first * t, t), 2 * t, jnp.zeros((t, 1), F32),
                         jnp.zeros((t, HEAD_DIM), F32), (i - first) * t)
        state.append((i, q, rem, acc))

    for u, (i, q, rem, acc) in enumerate(state):
        def cond(c):
            j, go, _, _ = c
            return jnp.logical_and(j >= 0, go > 0)

        def body(c, q=q):
            j, _, rem, acc = c
            rem, acc = block(q, pl.multiple_of(j * t, t), t, rem, acc, None)
            return j - 1, live(rem), rem, acc

        _, _, _, acc = lax.while_loop(cond, body, (i - 2, live(rem), rem, acc))
        o_ref[u * t:(u + 1) * t, :] = acc.astype(o_ref.dtype)


def _attention_call(kernel, h, out_cols, col0, extra_in, extra_specs, scratch, name, t):
    s = h.shape[0]
    nh = N_HEADS_GROUP
    in_specs = list(extra_specs[0]) + [
        pl.BlockSpec((t, HEAD_DIM), lambda hd, i: (i, col0 + hd)),
        pl.BlockSpec((s, HEAD_DIM), lambda hd, i: (0, col0 + nh + hd)),
        pl.BlockSpec((s, HEAD_DIM), lambda hd, i: (0, col0 + 2 * nh + hd)),
    ] + list(extra_specs[1])
    args = list(extra_in[0]) + [h, h, h] + list(extra_in[1])
    return pl.pallas_call(
        kernel,
        out_shape=jax.ShapeDtypeStruct((s, out_cols), BF16),
        grid=(nh, s // t),
        in_specs=in_specs,
        out_specs=pl.BlockSpec((t, HEAD_DIM), lambda hd, i: (i, hd)),
        scratch_shapes=scratch,
        compiler_params=_cparams(2, VMEM_LIMIT),
        name=name,
    )(*args)


def _route(x, w, b):
    x_hi, x_lo = _split_bf16(x)
    w_hi, w_lo = _split_bf16(w)
    logits = (jnp.dot(x_hi, w_hi, preferred_element_type=F32) + jnp.dot(x_lo, w_hi, preferred_element_type=F32)
              + jnp.dot(x_hi, w_lo, preferred_element_type=F32)) + b
    tm, n = logits.shape
    lane = lax.broadcasted_iota(jnp.int32, (tm, n), 1)
    real = lane < N_EXPERTS
    logits = jnp.where(real, logits, -jnp.inf)
    e = jnp.exp(logits - jnp.max(logits, axis=-1, keepdims=True))
    aff = e / jnp.sum(e, axis=-1, keepdims=True)
    group = lane // EXPERTS_PER_GROUP
    best = jnp.full((tm, 1), -jnp.inf, F32)
    g_sel = jnp.zeros((tm, 1), jnp.int32)
    for g in range(N_GROUPS):
        gmax = jnp.max(jnp.where(group == g, aff, -jnp.inf), axis=-1, keepdims=True)
        better = gmax > best
        g_sel = jnp.where(better, g, g_sel)
        best = jnp.where(better, gmax, best)
    in_group = jnp.logical_and(group == g_sel, real)
    cand = jnp.where(in_group, aff, -jnp.inf)
    v1 = jnp.max(cand, axis=-1, keepdims=True)
    i1 = jnp.min(jnp.where(cand == v1, lane, n), axis=-1, keepdims=True)
    cand2 = jnp.where(lane == i1, -jnp.inf, cand)
    v2 = jnp.max(cand2, axis=-1, keepdims=True)
    i2 = jnp.min(jnp.where(cand2 == v2, lane, n), axis=-1, keepdims=True)
    tot = v1 + v2
    chosen = jnp.where(jnp.logical_or(lane == i1, lane == i2), 1.0, 0.0)
    r = lax.broadcasted_iota(jnp.int32, (tm, tm), 0)
    c = lax.broadcasted_iota(jnp.int32, (tm, tm), 1)
    earlier = jnp.where(c < r, 1.0, 0.0).astype(BF16)
    before = jnp.dot(earlier, chosen.astype(BF16), preferred_element_type=F32)
    rank1 = jnp.sum(jnp.where(lane == i1, before, 0.0), axis=-1, keepdims=True)
    rank2 = jnp.sum(jnp.where(lane == i2, before, 0.0), axis=-1, keepdims=True)
    route = jnp.where(lane == 0, v1 / tot,
                      jnp.where(lane == 1, v2 / tot,
                                jnp.where(lane == 2, i1.astype(F32),
                                          jnp.where(lane == 3, i2.astype(F32),
                                                    jnp.where(lane == 4, rank1,
                                                              jnp.where(lane == 5, rank2, 0.0))))))
    return route, jnp.sum(chosen, axis=0, keepdims=True)


def _dispatch_plan(route, counts, tm):
    s = route.shape[0]
    nt = s // tm
    n_tiles = (MOE_TOP_K * s) // tm + N_EXPERTS
    ids = jnp.arange(N_EXPERTS, dtype=jnp.int32)
    expert = route[:, 2:2 + MOE_TOP_K].astype(jnp.int32).reshape(nt, tm, MOE_TOP_K)
    rank = route[:, 4:4 + MOE_TOP_K].astype(jnp.int32).reshape(nt, tm, MOE_TOP_K)
    cnt = counts[:, 0, :N_EXPERTS].astype(jnp.int32)
    tile_base = jnp.cumsum(cnt, axis=0) - cnt
    padded = -(-jnp.sum(cnt, axis=0) // tm) * tm
    seg_end = jnp.cumsum(padded)
    base = (seg_end - padded)[None, :] + tile_base
    pick = expert[..., None] == ids
    pair_row = (jnp.sum(jnp.where(pick, base[:, None, None, :], 0), axis=-1) + rank).reshape(s, MOE_TOP_K)
    tile_start = jnp.arange(n_tiles, dtype=jnp.int32) * tm
    tile_expert = jnp.minimum(jnp.sum((tile_start[:, None] >= seg_end[None, :]).astype(jnp.int32), axis=1),
                              N_EXPERTS - 1).astype(jnp.int32)
    tile_live = (tile_start < seg_end[-1]).astype(jnp.int32)
    row_token = (jnp.arange(n_tiles * tm, dtype=jnp.int32) % s).at[pair_row.reshape(-1)].set(
        jnp.arange(MOE_TOP_K * s, dtype=jnp.int32) // MOE_TOP_K)
    return pair_row, row_token, tile_expert, tile_live


def _moe_group_kernel(te_ref, live_ref, x_ref, wg_ref, wu_ref, wd_ref, o_ref, xb_ref, acc_ref):
    g = pl.program_id(0)
    f = pl.program_id(1)
    last = f == pl.num_programs(1) - 1
    live = live_ref[g] > 0

    @pl.when(jnp.logical_and(live, f == 0))
    def _():
        lo, hi = _unpack_bf16_pairs(x_ref[...])
        xb_ref[...] = jnp.concatenate([lo.astype(BF16), hi.astype(BF16)], axis=1)

    @pl.when(jnp.logical_and(jnp.logical_not(live), last))
    def _():
        o_ref[...] = jnp.zeros(o_ref.shape, o_ref.dtype)

    @pl.when(live)
    def _():
        xb = xb_ref[...]
        a = jnp.dot(xb, wg_ref[...], preferred_element_type=F32)
        u = jnp.dot(xb, wu_ref[...], preferred_element_type=F32)
        hid = ((a * jax.nn.sigmoid(a)) * u).astype(BF16)
        y = jnp.dot(hid, wd_ref[...], preferred_element_type=F32)

        @pl.when(f == 0)
        def _():
            acc_ref[...] = y

        @pl.when(f > 0)
        def _():
            acc_ref[...] += y

        @pl.when(last)
        def _():
            o_ref[...] = _pack_bf16_pairs(acc_ref[...])


def _moe_group_ffn(xs, tile_expert, tile_live, wg, wu, wd, tm, tf):
    rows, half = xs.shape
    d = 2 * half
    fdim = wg.shape[2]
    grid_spec = pltpu.PrefetchScalarGridSpec(
        num_scalar_prefetch=2,
        grid=(rows // tm, fdim // tf),
        in_specs=[pl.BlockSpec((tm, half), lambda g, f, te, lv: (g, 0)),
                  pl.BlockSpec((None, d, tf), lambda g, f, te, lv: (te[g], 0, f)),
                  pl.BlockSpec((None, d, tf), lambda g, f, te, lv: (te[g], 0, f)),
                  pl.BlockSpec((None, tf, d), lambda g, f, te, lv: (te[g], f, 0))],
        out_specs=pl.BlockSpec((tm, half), lambda g, f, te, lv: (g, 0)),
        scratch_shapes=[pltpu.VMEM((tm, d), BF16), pltpu.VMEM((tm, d), F32)])
    return pl.pallas_call(
        _moe_group_kernel,
        out_shape=jax.ShapeDtypeStruct((rows, half), jnp.uint32),
        grid_spec=grid_spec,
        compiler_params=_cparams(2, VMEM_LIMIT),
        name="moe_group_ffn",
    )(tile_expert, tile_live, xs, wg, wu, wd)


def _moe_combine_ln_kernel(x_ref, r_ref, y0_ref, y1_ref, g_ref, b_ref, o_ref):
    r = r_ref[...]
    lo0, hi0 = _unpack_bf16_pairs(y0_ref[...])
    lo1, hi1 = _unpack_bf16_pairs(y1_ref[...])
    ff = jnp.concatenate([r[:, 0:1] * lo0 + r[:, 1:2] * lo1, r[:, 0:1] * hi0 + r[:, 1:2] * hi1], axis=1)
    o_ref[...] = _layer_norm(ALPHA * x_ref[...] + ff, g_ref[...], b_ref[...])


def _moe_combine_ln(x, route, y, g, b, tm):
    s, d = x.shape
    nt = s // tm
    return pl.pallas_call(
        _moe_combine_ln_kernel,
        out_shape=jax.ShapeDtypeStruct((s, d), F32),
        grid=(nt,),
        in_specs=[pl.BlockSpec((tm, d), lambda i: (i, 0)),
                  pl.BlockSpec((tm, LANES), lambda i: (i, 0)),
                  pl.BlockSpec((tm, d // 2), lambda i: (i, 0)),
                  pl.BlockSpec((tm, d // 2), lambda i: (i + nt, 0)),
                  pl.BlockSpec((1, d), lambda i: (0, 0)),
                  pl.BlockSpec((1, d), lambda i: (0, 0))],
        out_specs=pl.BlockSpec((tm, d), lambda i: (i, 0)),
        compiler_params=_cparams(1, VMEM_LIMIT),
        name="moe_combine_ln",
    )(x, route, y, y, g.reshape(1, d), b.reshape(1, d))


def _sc_row_gather(src, idx):
    n = idx.shape[0]
    d = src.shape[1]
    workers = SC_CORES * SC_SUBCORES
    per_worker = n // workers
    assert n % (workers * SC_GATHER_ROWS) == 0
    mesh = plsc.VectorSubcoreMesh(core_axis_name="c", subcore_axis_name="s")

    ch = SC_GATHER_ROWS
    n_chunks = per_worker // ch
    assert n_chunks % 2 == 0

    @functools.partial(
        pl.kernel, mesh=mesh, out_type=jax.ShapeDtypeStruct((n, d), src.dtype),
        scratch_types=[pltpu.VMEM((per_worker,), jnp.int32),
                       pltpu.VMEM((ch, d), src.dtype), pltpu.VMEM((ch, d), src.dtype),
                       pltpu.SemaphoreType.DMA, pltpu.SemaphoreType.DMA],
        name="sc_row_gather")
    def gather(src_hbm, idx_hbm, out_hbm, idx_v, rows_a, rows_b, sem_a, sem_b):
        base = (lax.axis_index("s") * SC_CORES + lax.axis_index("c")) * per_worker
        pltpu.sync_copy(idx_hbm.at[pl.ds(base, per_worker)], idx_v)

        def start(c, rows, sem):
            pltpu.async_copy(src_hbm.at[idx_v.at[pl.ds(c * ch, ch)]], rows, sem)

        def finish(c, rows, sem):
            pltpu.make_async_copy(src_hbm.at[pl.ds(0, ch)], rows, sem).wait()
            pltpu.sync_copy(rows, out_hbm.at[pl.ds(base + c * ch, ch)])

        start(0, rows_a, sem_a)

        @pl.loop(0, n_chunks, step=2)
        def _(c):
            start(c + 1, rows_b, sem_b)
            finish(c, rows_a, sem_a)

            @pl.when(c + 2 < n_chunks)
            def _():
                start(c + 2, rows_a, sem_a)

            finish(c + 1, rows_b, sem_b)

    return gather(src, idx)


def _moe_ln(x, x_packed, route, counts, wg, wu, wd, g, b, tm):
    pair_row, row_token, tile_expert, tile_live = _dispatch_plan(route, counts, tm)
    xs = _sc_row_gather(x_packed, row_token)
    ys = _moe_group_ffn(xs, tile_expert, tile_live, wg, wu, wd, tm, MOE_FF_TILE)
    slot_major = pair_row.T.reshape(-1)
    y = _sc_row_gather(ys, slot_major)
    return _moe_combine_ln(x, route, y, g, b, tm)


def kernel(x, w_in_even, w_out_even, forget_bias, w_in_odd, w_out_odd, lambda_q, lambda_k,
           subln_gain, rel_bias, w_router, b_router, w_gate, w_up, w_down, ln_gain, ln_bias):
    bsz, s, d = x.shape
    assert bsz == 1 and d == D_MODEL and s % MOBA_BLOCK == 0
    t = ATT_TILE
    nh = N_HEADS_GROUP
    gw = GROUP_WIDTH
    tm_proj = min(1024, s)
    tm_row = min(512, s)
    tk = min(ATT_KEYS, s)
    n_near = _near_tile_count(t)
    assert s % tk == 0 and tk // t + 1 >= n_near
    nblk = s // MOBA_BLOCK
    nbp = -(-nblk // LANES) * LANES

    x2 = x.reshape(s, d)
    bias_tiles = _bias_tiles(rel_bias, t, n_near)
    tab_spec = pl.BlockSpec(memory_space=pltpu.SMEM)
    bias_spec = pl.BlockSpec((None, n_near + 2, t, t), lambda hd, i: (hd, 0, 0, 0))
    block_onehot = (jnp.arange(s, dtype=jnp.int32)[:, None] // MOBA_BLOCK
                    == jnp.arange(nbp, dtype=jnp.int32)[None, :]).astype(BF16)

    for layer in range(DEPTH):
        li = layer // 2
        if layer % 2 == 0:
            w = w_in_even[li]
            fcol = 3 * gw
            w_main = jnp.concatenate([w[:, :fcol], w[:, fcol + nh:]], axis=1).astype(BF16)
            w_f = jnp.zeros((d, LANES), BF16).at[:, :nh].set(w[:, fcol:fcol + nh].astype(BF16))
            h = _in_proj(x2, w_main, BF16, tm_proj, 512)
            f = _in_proj(x2, w_f, F32, tm_proj, LANES)
            c = _decay_cumsum(f[:, :nh].T, forget_bias[li])
            oa = _attention_call(
                functools.partial(_fox_kernel, t=FOX_TQ, tk=min(FOX_TK, s), scale=HEAD_DIM ** -0.5), h, gw, 0,
                ([c[:, min(FOX_TK, s) - 1::min(FOX_TK, s)], c[:, ::FOX_TQ]],
                 [c.reshape(nh, 1, s), c.reshape(nh, s, 1)]),
                ([tab_spec, tab_spec],
                 [pl.BlockSpec((None, 1, s), lambda hd, i: (hd, 0, 0)),
                  pl.BlockSpec((None, FOX_TQ, 1), lambda hd, i: (hd, i, 0))]),
                [pltpu.VMEM((SUBLANES, LANES), F32)],
                "fox_attention", FOX_TQ)
            ob = _attention_call(
                functools.partial(_moba_kernel, tb=t, nb=MOBA_Q_BLOCKS, tk=tk, nblk=nblk, n_near=n_near,
                                  scale=HEAD_DIM ** -0.5),
                h, gw, 3 * nh,
                ([rel_bias], [bias_tiles, block_onehot]),
                ([tab_spec], [bias_spec, pl.BlockSpec((s, nbp), lambda hd, i: (0, 0))]),
                [pltpu.VMEM((nbp, HEAD_DIM), F32), pltpu.VMEM((SUBLANES, LANES), F32)],
                "moba_attention", MOBA_Q_BLOCKS * t)
            w_out = w_out_even[li]
        else:
            lambda_init = 0.8 - 0.6 * math.exp(-0.3 * layer)
            h = _in_proj(x2, w_in_odd[li].astype(BF16), BF16, tm_proj, 512)
            small = lambda shape: pl.BlockSpec(shape, lambda hd, i: (0, 0))
            oa = _attention_call(
                functools.partial(_diff_kernel, tb=t, nb=DIFF_Q_BLOCKS, tk=tk, n_near=n_near,
                                  scale=DIFF_QK_DIM ** -0.5, lambda_init=lambda_init),
                h, gw, 0,
                ([rel_bias], [bias_tiles, lambda_q[li], lambda_k[li], subln_gain[li].reshape(1, HEAD_DIM)]),
                ([tab_spec], [bias_spec, small((2, DIFF_QK_DIM)), small((2, DIFF_QK_DIM)),
                              small((1, HEAD_DIM))]),
                [pltpu.VMEM((SUBLANES, LANES), F32)],
                "diff_attention", DIFF_Q_BLOCKS * t)
            ob = _attention_call(
                functools.partial(_sb_kernel, t=t, nq=SB_Q_TILES, scale=HEAD_DIM ** -0.5), h, gw, 3 * nh,
                ([], []), ([], []), [], "stickbreak_attention", SB_Q_TILES * t)
            w_out = w_out_odd[li]
        x2, x2_packed = _out_proj_ln(oa, ob, w_out.astype(BF16), x2, ln_gain[layer, 0], ln_bias[layer, 0],
                                     tm_row)
        route, counts = _router(x2, w_router, b_router, tm_row)
        x2 = _moe_ln(x2, x2_packed, route, counts, w_gate[layer].astype(BF16), w_up[layer].astype(BF16),
                     w_down[layer].astype(BF16), ln_gain[layer, 1], ln_bias[layer, 1], tm_row)
    return x2.reshape(bsz, s, d)
```

```python
import functools
import math

import numpy as np
import jax
import jax.numpy as jnp
from jax import lax
from jax.experimental import pallas as pl
from jax.experimental.pallas import tpu as pltpu
from jax.experimental.pallas import tpu_sc as plsc

F32 = jnp.float32
BF16 = jnp.bfloat16

D_MODEL = 2048
DEPTH = 2
HEAD_DIM = 128
N_HEADS_GROUP = 8
GROUP_WIDTH = N_HEADS_GROUP * HEAD_DIM
DIFF_QK_DIM = HEAD_DIM // 2
MOBA_BLOCK = 256
MOBA_TOPK = 3
N_BUCKETS = 32
MAX_EXACT = N_BUCKETS // 2
MAX_DISTANCE = 1024
N_EXPERTS = 16
N_GROUPS = 4
EXPERTS_PER_GROUP = N_EXPERTS // N_GROUPS
MOE_TOP_K = 2
SC_CORES = 2
SC_SUBCORES = 16
SC_GATHER_ROWS = 32
SUBLANES = 8
D_EXPERT = D_MODEL // 2
PROJ_TN = 1024
MOE_FF_TILE = D_EXPERT
ALPHA = (2.0 * DEPTH) ** 0.25
LN_EPS = 1e-5
NEG_INF = -1e30
LOG2E = math.log2(math.e)
LANES = 128
ATT_TILE = 256
ATT_KEYS = 1024
FOX_TQ = 1024
FOX_TK = 1024
MOBA_Q_BLOCKS = 4
SB_Q_TILES = 2
DIFF_Q_BLOCKS = 4
SHIFT_SLACK = 96.0
FLUSH_EXPONENT = -130.0
SB_CUTOFF = -104.0
VMEM_LIMIT = 56 * 1024 * 1024


def _cparams(n_axes, vmem=None):
    return pltpu.CompilerParams(dimension_semantics=("arbitrary",) * n_axes,
                                vmem_limit_bytes=vmem)


def _nt_dot(a, b):
    return lax.dot_general(a, b, (((1,), (1,)), ((), ())), preferred_element_type=F32)


def _split_bf16(x):
    hi = x.astype(BF16)
    lo = (x - hi.astype(F32)).astype(BF16)
    return hi, lo


def _pack_bf16_pairs(x):
    n = x.shape[1] // 2
    lo = pltpu.bitcast(x[:, :n].astype(BF16).astype(F32), jnp.uint32)
    hi = pltpu.bitcast(x[:, n:].astype(BF16).astype(F32), jnp.uint32)
    return jnp.bitwise_or(lax.shift_right_logical(lo, jnp.uint32(16)),
                          jnp.bitwise_and(hi, jnp.uint32(0xFFFF0000)))


def _unpack_bf16_pairs(w):
    lo = pltpu.bitcast(lax.shift_left(w, jnp.uint32(16)), F32)
    hi = pltpu.bitcast(jnp.bitwise_and(w, jnp.uint32(0xFFFF0000)), F32)
    return lo, hi


def _layer_norm(z, g, b):
    mu = jnp.mean(z, axis=-1, keepdims=True)
    zc = z - mu
    var = jnp.mean(zc * zc, axis=-1, keepdims=True)
    return zc * lax.rsqrt(var + LN_EPS) * g + b


def _inproj_kernel(x_ref, w_ref, o_ref, xb_ref):
    @pl.when(pl.program_id(1) == 0)
    def _():
        xb_ref[...] = x_ref[...].astype(BF16)

    o_ref[...] = jnp.dot(xb_ref[...], w_ref[...], preferred_element_type=F32).astype(o_ref.dtype)


def _in_proj(x, w, out_dtype, tm, tn):
    s, d = x.shape
    n = w.shape[1]
    return pl.pallas_call(
        _inproj_kernel,
        out_shape=jax.ShapeDtypeStruct((s, n), out_dtype),
        grid=(s // tm, n // tn),
        in_specs=[pl.BlockSpec((tm, d), lambda i, j: (i, 0)),
                  pl.BlockSpec((d, tn), lambda i, j: (0, j))],
        out_specs=pl.BlockSpec((tm, tn), lambda i, j: (i, j)),
        scratch_shapes=[pltpu.VMEM((tm, d), BF16)],
        compiler_params=_cparams(2, VMEM_LIMIT),
        name="in_proj",
    )(x, w)


def _outproj_ln_kernel(oa_ref, ob_ref, wa_ref, wb_ref, x_ref, g_ref, b_ref, o_ref, packed_ref):
    y = jnp.dot(oa_ref[...], wa_ref[...], preferred_element_type=F32)
    y = y + jnp.dot(ob_ref[...], wb_ref[...], preferred_element_type=F32)
    x1 = _layer_norm(ALPHA * x_ref[...] + y, g_ref[...], b_ref[...])
    o_ref[...] = x1
    packed_ref[...] = _pack_bf16_pairs(x1)


def _out_proj_ln(oa, ob, w, x, g, b, tm):
    s, d = x.shape
    gw = oa.shape[1]
    return pl.pallas_call(
        _outproj_ln_kernel,
        out_shape=(jax.ShapeDtypeStruct((s, d), F32), jax.ShapeDtypeStruct((s, d // 2), jnp.uint32)),
        grid=(s // tm,),
        in_specs=[pl.BlockSpec((tm, gw), lambda i: (i, 0)),
                  pl.BlockSpec((tm, gw), lambda i: (i, 0)),
                  pl.BlockSpec((gw, d), lambda i: (0, 0)),
                  pl.BlockSpec((gw, d), lambda i: (1, 0)),
                  pl.BlockSpec((tm, d), lambda i: (i, 0)),
                  pl.BlockSpec((1, d), lambda i: (0, 0)),
                  pl.BlockSpec((1, d), lambda i: (0, 0))],
        out_specs=(pl.BlockSpec((tm, d), lambda i: (i, 0)),
                   pl.BlockSpec((tm, d // 2), lambda i: (i, 0))),
        compiler_params=_cparams(1, VMEM_LIMIT),
        name="out_proj_ln",
    )(oa, ob, w, w, x, g.reshape(1, d), b.reshape(1, d))


def _router_kernel(x_ref, w_ref, b_ref, route_ref, count_ref):
    route, counts = _route(x_ref[...], w_ref[...], b_ref[...])
    route_ref[...] = route
    count_ref[...] = jnp.broadcast_to(counts, count_ref.shape)


def _router(x, w_router, b_router, tm):
    s, d = x.shape
    wr = jnp.zeros((d, LANES), F32).at[:, :N_EXPERTS].set(w_router)
    br = jnp.zeros((1, LANES), F32).at[0, :N_EXPERTS].set(b_router)
    return pl.pallas_call(
        _router_kernel,
        out_shape=(jax.ShapeDtypeStruct((s, LANES), F32),
                   jax.ShapeDtypeStruct((s // tm, SUBLANES, LANES), F32)),
        grid=(s // tm,),
        in_specs=[pl.BlockSpec((tm, d), lambda i: (i, 0)),
                  pl.BlockSpec((d, LANES), lambda i: (0, 0)),
                  pl.BlockSpec((1, LANES), lambda i: (0, 0))],
        out_specs=(pl.BlockSpec((tm, LANES), lambda i: (i, 0)),
                   pl.BlockSpec((None, SUBLANES, LANES), lambda i: (i, 0, 0))),
        compiler_params=_cparams(1, VMEM_LIMIT),
        name="router",
    )(x, wr, br)


def _decay_cumsum_kernel(f_ref, b_ref, c_ref):
    nh, s = f_ref.shape
    rows = lax.broadcasted_iota(jnp.int32, (LANES, LANES), 0)
    cols = lax.broadcasted_iota(jnp.int32, (LANES, LANES), 1)
    upper = jnp.where(rows <= cols, 1.0, 0.0).astype(F32)
    bias = b_ref[...]

    def body(n, carry):
        off = pl.multiple_of(n * LANES, LANES)
        z = f_ref[:, pl.ds(off, LANES)] + bias
        logf = jnp.minimum(z, 0.0) - jnp.log1p(jnp.exp(-jnp.abs(z)))
        c = jnp.dot(logf, upper, preferred_element_type=F32,
                    precision=lax.Precision.HIGHEST) + carry
        c_ref[:, pl.ds(off, LANES)] = c * LOG2E
        return carry + jnp.sum(logf, axis=-1, keepdims=True)

    lax.fori_loop(0, s // LANES, body, jnp.zeros((nh, 1), F32))


def _decay_cumsum(f_t, bias):
    nh, s = f_t.shape
    return pl.pallas_call(
        _decay_cumsum_kernel,
        out_shape=jax.ShapeDtypeStruct((nh, s), F32),
        name="decay_cumsum",
    )(f_t, bias.reshape(nh, 1))


def _t5_bucket_np(dist):
    n = np.maximum(dist, 0)
    nf = np.maximum(n, 1).astype(np.float32)
    ratio = np.log(nf / np.float32(MAX_EXACT)) / np.float32(math.log(MAX_DISTANCE / MAX_EXACT))
    large = MAX_EXACT + (ratio.astype(np.float32) * np.float32(N_BUCKETS - MAX_EXACT)).astype(np.int32)
    large = np.minimum(large, N_BUCKETS - 1)
    return np.where(n < MAX_EXACT, n, large).astype(np.int32)


def _near_tile_count(t):
    d = np.arange(0, 4 * MAX_DISTANCE, dtype=np.int64)
    not_last = np.nonzero(_t5_bucket_np(d) != N_BUCKETS - 1)[0]
    d_sat = int(not_last.max()) + 1
    n = 1
    while (n - 1) * t + 1 < d_sat:
        n += 1
    return n


def _bucket_tiles_np(t, n_near):
    r = np.arange(t)[:, None]
    c = np.arange(t)[None, :]
    tiles = []
    for delta in range(n_near):
        dist = delta * t + r - c
        tiles.append(np.where(dist >= 0, _t5_bucket_np(dist), -1))
    tiles.append(np.full((t, t), N_BUCKETS - 1))
    tiles.append(np.full((t, t), -1))
    return np.stack(tiles).astype(np.int32)


def _bias_tiles_kernel(tab_ref, idx_ref, o_ref):
    h = pl.program_id(0)
    idx = idx_ref[...]
    out = jnp.full(idx.shape, NEG_INF, F32)
    for b in range(N_BUCKETS):
        out = jnp.where(idx == b, tab_ref[b, h] * LOG2E, out)
    o_ref[...] = out


def _bias_tiles(rel_bias, t, n_near):
    nh = rel_bias.shape[1]
    idx = jnp.asarray(_bucket_tiles_np(t, n_near))
    n_tiles = idx.shape[0]
    return pl.pallas_call(
        _bias_tiles_kernel,
        out_shape=jax.ShapeDtypeStruct((nh, n_tiles, t, t), F32),
        grid=(nh, n_tiles),
        in_specs=[pl.BlockSpec(memory_space=pltpu.SMEM),
                  pl.BlockSpec((None, t, t), lambda h, n: (n, 0, 0))],
        out_specs=pl.BlockSpec((None, None, t, t), lambda h, n: (h, n, 0, 0)),
        compiler_params=_cparams(2),
        name="bias_tiles",
    )(rel_bias, idx)


def _online_softmax_step(s, v, carry):
    m, l, acc = carry
    m_new = jnp.maximum(m, jnp.max(s, axis=-1, keepdims=True))
    alpha = jnp.exp2(m - m_new)
    p = jnp.exp2(s - m_new)
    l = alpha * l + jnp.sum(p, axis=-1, keepdims=True)
    acc = alpha * acc + jnp.dot(p.astype(BF16), v, preferred_element_type=F32)
    return m_new, l, acc


def _prescale(q, c):
    return (q.astype(F32) * c).astype(BF16)


def _near_bias(bias_ref, blocks, j, tb, tk, n_near):
    nsub = tk // tb
    rows = []
    for i in blocks:
        tiles = []
        for b in range(nsub):
            delta = i - (j * nsub + b)
            tiles.append(bias_ref[jnp.where(delta < 0, n_near + 1, jnp.minimum(delta, n_near))])
        rows.append(jnp.concatenate(tiles, axis=1))
    return rows[0] if len(rows) == 1 else jnp.concatenate(rows, axis=0)


def _biased_sweep(i, t, tk, scores, process, init):
    jd = (i * t) // tk
    carry = process(scores(jd), jd, init, True)
    lo = jnp.maximum(jd - 1, 0)
    carry = lax.fori_loop(lo, jd, lambda j, c: process(scores(j), j, c, True), carry)
    return lax.fori_loop(0, lo, lambda j, c: process(scores(j), j, c, False), carry)


def _softmax_init(rows):
    return (jnp.full((rows, 1), NEG_INF, F32), jnp.zeros((rows, 1), F32),
            jnp.zeros((rows, HEAD_DIM), F32))


def _fox_kernel(cend_ref, cstart_ref, q_ref, k_ref, v_ref, c_ref, ccol_ref, o_ref, knorm_ref,
                *, t, tk, scale):
    h = pl.program_id(0)
    i = pl.program_id(1)

    @pl.when(i == 0)
    def _():
        knorm_ref[...] = jnp.broadcast_to(_max_key_norm(k_ref, tk, 0, HEAD_DIM), knorm_ref.shape)

    q = _prescale(q_ref[...], scale * LOG2E)
    rows = lax.broadcasted_iota(jnp.int32, (t, tk), 0)
    cols = lax.broadcasted_iota(jnp.int32, (t, tk), 1)

    def scores(j):
        off = pl.multiple_of(j * tk, tk)
        return _nt_dot(q, k_ref[pl.ds(off, tk), :]) - c_ref[:, pl.ds(off, tk)]

    def values(j):
        return v_ref[pl.ds(pl.multiple_of(j * tk, tk), tk), :]

    def process(s, j, carry):
        return _online_softmax_step(s, values(j), carry)

    jd = (i * t) // tk
    s_diag = jnp.where(cols <= rows + (i * t - jd * tk), scores(jd), NEG_INF)

    qf = q.astype(F32)
    qnorm = jnp.sqrt(jnp.sum(qf * qf, axis=-1, keepdims=True))
    bound = qnorm * knorm_ref[0:1, 0:1] * (1.0 + 2.0 ** -9) + 2.0 ** -6 - ccol_ref[...]
    slack = jnp.max(bound - jnp.max(s_diag, axis=-1, keepdims=True))

    def shifted_sweep(_):
        def step(s, j, carry):
            l, acc = carry
            p = jnp.exp2(s - bound)
            return (l + jnp.sum(p, axis=-1, keepdims=True),
                    acc + jnp.dot(p.astype(BF16), values(j), preferred_element_type=F32))

        c_first_row = cstart_ref[h, i]
        first_live = jnp.int32(0)
        for j in range(cend_ref.shape[1]):
            dead = jnp.logical_and(j < jd, c_first_row - cend_ref[h, j] < FLUSH_EXPONENT)
            first_live = first_live + dead.astype(jnp.int32)
        carry = step(s_diag, jd, _softmax_init(t)[1:])
        return _paired_loop(jd - first_live, lambda n, c: step(scores(first_live + n), first_live + n, c),
                            carry)

    def running_max_sweep(_):
        carry = _online_softmax_step(s_diag, values(jd), _softmax_init(t))
        return lax.fori_loop(0, jd, lambda j, c: process(scores(j), j, c), carry)[1:]

    l, acc = lax.cond(slack <= SHIFT_SLACK, shifted_sweep, running_max_sweep, 0)
    o_ref[...] = (acc / l).astype(o_ref.dtype)


def _moba_kernel(tab_ref, q_ref, k_ref, v_ref, bias_ref, et_ref, o_ref, kmean_ref, knorm_ref,
                 *, tb, nb, tk, nblk, n_near, scale):
    h = pl.program_id(0)
    g = pl.program_id(1)
    t = nb * tb
    nbp = kmean_ref.shape[0]

    @pl.when(g == 0)
    def _():
        kmean_ref[...] = jnp.zeros(kmean_ref.shape, F32)

        def mean_body(n, _):
            off = pl.multiple_of(n * tb, tb)
            kb = k_ref[pl.ds(off, tb), :].astype(F32)
            kmean_ref[pl.ds(n, 1), :] = jnp.sum(kb, axis=0, keepdims=True) * (1.0 / tb)
            return 0

        lax.fori_loop(0, nblk, mean_body, 0)
        knorm_ref[...] = jnp.broadcast_to(_max_key_norm(k_ref, tk, 0, HEAD_DIM), knorm_ref.shape)

    q = q_ref[...]
    km_hi, km_lo = _split_bf16(kmean_ref[...])
    gate = _nt_dot(q, km_hi) + _nt_dot(q, km_lo)
    blk = lax.broadcasted_iota(jnp.int32, (t, nbp), 1)
    own = g * nb + lax.broadcasted_iota(jnp.int32, (t, 1), 0) // tb
    eligible = blk < own
    sel = jnp.zeros((t, nbp), F32)
    for r in range(MOBA_TOPK):
        cand = jnp.where(eligible, jnp.where(sel > 0.0, -jnp.inf, gate), -jnp.inf)
        vmax = jnp.max(cand, axis=-1, keepdims=True)
        first = jnp.min(jnp.where(cand == vmax, blk, nbp), axis=-1, keepdims=True)
        take = jnp.where(r < own, 1.0, 0.0)
        sel = jnp.maximum(sel, jnp.where(blk == first, take, 0.0))

    sel = jnp.maximum(sel, jnp.where(blk == own, 1.0, 0.0))
    penalty = jnp.where(sel > 0.0, 0.0, NEG_INF).astype(BF16)
    q_aug = jnp.concatenate([_prescale(q, scale * LOG2E), penalty], axis=1)
    b_far = tab_ref[N_BUCKETS - 1, h] * LOG2E

    def scores(j):
        off = pl.multiple_of(j * tk, tk)
        k_aug = jnp.concatenate([k_ref[pl.ds(off, tk), :], et_ref[pl.ds(off, tk), :]], axis=1)
        return _nt_dot(q_aug, k_aug)

    def values(j):
        return v_ref[pl.ds(pl.multiple_of(j * tk, tk), tk), :]

    def near_bias(j):
        return _near_bias(bias_ref, [g * nb + b for b in range(nb)], j, tb, tk, n_near)

    def process(s, j, carry, near):
        return _online_softmax_step(s + (near_bias(j) if near else b_far), values(j), carry)

    qf = q_aug[:, :HEAD_DIM].astype(F32)
    qnorm = jnp.sqrt(jnp.sum(qf * qf, axis=-1, keepdims=True))
    bound = _logit_bound(qnorm, knorm_ref[0:1, 0:1], tab_ref, h)
    l, acc = _shifted_or_running_sweep(g, t, tk, scores, near_bias, values, process, bound, b_far, t)
    o_ref[...] = (acc / l).astype(o_ref.dtype)


def _paired_loop(n, body, carry):
    carry = lax.fori_loop(0, n // 2, lambda p, c: body(2 * p + 1, body(2 * p, c)), carry)
    return lax.fori_loop(0, n % 2, lambda _, c: body(n - 1, c), carry)


def _logit_bound(qnorm, knorm, tab_ref, h):
    b_max = tab_ref[0, h]
    for b in range(1, N_BUCKETS):
        b_max = jnp.maximum(b_max, tab_ref[b, h])
    return qnorm * knorm * (1.0 + 2.0 ** -9) + (b_max * LOG2E + 2.0 ** -6)


def _shifted_or_running_sweep(i, t, tk, scores, near_bias, values, process, bound, b_far, rows):
    jd = (i * t) // tk
    s_diag = scores(jd) + near_bias(jd)
    slack = jnp.max(bound - jnp.max(s_diag, axis=-1, keepdims=True))

    def shifted_sweep(_):
        def step(s, j, carry, shift):
            l, acc = carry
            p = jnp.exp2(s - shift)
            return (l + jnp.sum(p, axis=-1, keepdims=True),
                    acc + jnp.dot(p.astype(BF16), values(j), preferred_element_type=F32))

        carry = step(s_diag, jd, _softmax_init(rows)[1:], bound)
        lo = jnp.maximum(jd - 1, 0)
        carry = lax.fori_loop(lo, jd, lambda j, c: step(scores(j) + near_bias(j), j, c, bound), carry)
        far_shift = bound - b_far
        return _paired_loop(lo, lambda j, c: step(scores(j), j, c, far_shift), carry)

    def running_max_sweep(_):
        return _biased_sweep(i, t, tk, scores, process, _softmax_init(rows))[1:]

    return lax.cond(slack <= SHIFT_SLACK, shifted_sweep, running_max_sweep, 0)


def _max_key_norm(k_ref, tk, lane_lo, lane_hi):
    klane = lax.broadcasted_iota(jnp.int32, (tk, HEAD_DIM), 1)
    keep = jnp.logical_and(klane >= lane_lo, klane < lane_hi)

    def body(n, c):
        kt = k_ref[pl.ds(pl.multiple_of(n * tk, tk), tk), :].astype(F32)
        sq = jnp.sum(jnp.where(keep, kt * kt, 0.0), axis=-1, keepdims=True)
        return jnp.maximum(c, jnp.max(sq, axis=0, keepdims=True))

    return jnp.sqrt(lax.fori_loop(0, k_ref.shape[0] // tk, body, jnp.zeros((1, 1), F32)))


def _diff_kernel(tab_ref, q_ref, k_ref, v_ref, bias_ref, lq_ref, lk_ref, g_ref, o_ref, knorm_ref,
                 *, tb, nb, tk, n_near, scale, lambda_init):
    h = pl.program_id(0)
    i = pl.program_id(1)
    t = nb * tb
    q = _prescale(q_ref[...], scale * LOG2E)
    lane = lax.broadcasted_iota(jnp.int32, (t, HEAD_DIM), 1)
    zero = jnp.zeros_like(q)
    q2 = jnp.concatenate([jnp.where(lane < DIFF_QK_DIM, q, zero),
                          jnp.where(lane >= DIFF_QK_DIM, q, zero)], axis=0)
    b_far = tab_ref[N_BUCKETS - 1, h] * LOG2E

    @pl.when(i == 0)
    def _():
        knorm_ref[0:1, :] = jnp.broadcast_to(_max_key_norm(k_ref, tk, 0, DIFF_QK_DIM), (1, LANES))
        knorm_ref[1:2, :] = jnp.broadcast_to(_max_key_norm(k_ref, tk, DIFF_QK_DIM, HEAD_DIM), (1, LANES))

    def scores(j):
        return _nt_dot(q2, k_ref[pl.ds(pl.multiple_of(j * tk, tk), tk), :])

    def values(j):
        return v_ref[pl.ds(pl.multiple_of(j * tk, tk), tk), :]

    def near_bias(j):
        b = _near_bias(bias_ref, [i * nb + b for b in range(nb)], j, tb, tk, n_near)
        return jnp.concatenate([b, b], axis=0)

    def process(s, j, carry, near):
        return _online_softmax_step(s + (near_bias(j) if near else b_far), values(j), carry)

    q2f = q2.astype(F32)
    qnorm = jnp.sqrt(jnp.sum(q2f * q2f, axis=-1, keepdims=True))
    knorm = jnp.concatenate([jnp.broadcast_to(knorm_ref[0:1, 0:1], (t, 1)),
                             jnp.broadcast_to(knorm_ref[1:2, 0:1], (t, 1))], axis=0)
    bound = _logit_bound(qnorm, knorm, tab_ref, h)
    l, acc = _shifted_or_running_sweep(i, t, tk, scores, near_bias, values, process, bound, b_far, 2 * t)
    o12 = acc / l
    lam_e = jnp.exp(jnp.sum(lq_ref[...] * lk_ref[...], axis=-1, keepdims=True))
    lam = lam_e[0:1, :] - lam_e[1:2, :] + lambda_init
    o = o12[:t, :] - lam * o12[t:, :]
    o = o * lax.rsqrt(jnp.mean(o * o, axis=-1, keepdims=True) + LN_EPS) * g_ref[...]
    o_ref[...] = (o * (1.0 - lambda_init)).astype(o_ref.dtype)


def _sb_kernel(q_ref, k_ref, v_ref, o_ref, *, t, nq, scale):
    g = pl.program_id(1)

    def later_matrix(n):
        r = lax.broadcasted_iota(jnp.int32, (n, n), 0)
        c = lax.broadcasted_iota(jnp.int32, (n, n), 1)
        return jnp.where(r > c, 1.0, 0.0).astype(BF16)

    def block(q, off, nk, rem, acc, shift):
        k = k_ref[pl.ds(off, nk), :]
        v = v_ref[pl.ds(off, nk), :]
        z = _nt_dot(q, k) * scale
        soft = jnp.log1p(jnp.exp(-jnp.abs(z)))
        log_sig = jnp.minimum(z, 0.0) - soft
        log_rem = jnp.minimum(-z, 0.0) - soft
        if shift is not None:
            rows = lax.broadcasted_iota(jnp.int32, (t, nk), 0)
            cols = lax.broadcasted_iota(jnp.int32, (t, nk), 1)
            mask = cols < rows + shift
            log_rem = jnp.where(mask, log_rem, 0.0)
        hi, lo = _split_bf16(log_rem)
        later = later_matrix(nk)
        after = (jnp.dot(hi, later, preferred_element_type=F32)
                 + jnp.dot(lo, later, preferred_element_type=F32))
        a = jnp.exp(log_sig + after + rem)
        if shift is not None:
            a = jnp.where(mask, a, 0.0)
        acc = acc + jnp.dot(a.astype(BF16), v, preferred_element_type=F32)
        rem = rem + jnp.sum(log_rem, axis=-1, keepdims=True)
        return rem, acc

    def live(rem):
        return (jnp.max(rem) > SB_CUTOFF).astype(jnp.int32)

    state = []
    for u in range(nq):
        i = g * nq + u
        q = q_ref[u * t:(u + 1) * t, :]
        first = jnp.maximum(i - 1, 0)
        rem, acc = block(q, pl.multiple_of(first * t, t), 2 * t, jnp.zeros((t, 1), F32),
                         jnp.zeros((t, HEAD_DIM), F32), (i - first) * t)
        state.append((i, q, rem, acc))

    for u, (i, q, rem, acc) in enumerate(state):
        def cond(c):
            j, go, _, _ = c
            return jnp.logical_and(j >= 0, go > 0)

        def body(c, q=q):
            j, _, rem, acc = c
            rem, acc = block(q, pl.multiple_of(j * t, t), t, rem, acc, None)
            return j - 1, live(rem), rem, acc

        _, _, _, acc = lax.while_loop(cond, body, (i - 2, live(rem), rem, acc))
        o_ref[u * t:(u + 1) * t, :] = acc.astype(o_ref.dtype)


def _attention_call(kernel, h, out_cols, col0, extra_in, extra_specs, scratch, name, t):
    s = h.shape[0]
    nh = N_HEADS_GROUP
    in_specs = list(extra_specs[0]) + [
        pl.BlockSpec((t, HEAD_DIM), lambda hd, i: (i, col0 + hd)),
        pl.BlockSpec((s, HEAD_DIM), lambda hd, i: (0, col0 + nh + hd)),
        pl.BlockSpec((s, HEAD_DIM), lambda hd, i: (0, col0 + 2 * nh + hd)),
    ] + list(extra_specs[1])
    args = list(extra_in[0]) + [h, h, h] + list(extra_in[1])
    return pl.pallas_call(
        kernel,
        out_shape=jax.ShapeDtypeStruct((s, out_cols), BF16),
        grid=(nh, s // t),
        in_specs=in_specs,
        out_specs=pl.BlockSpec((t, HEAD_DIM), lambda hd, i: (i, hd)),
        scratch_shapes=scratch,
        compiler_params=_cparams(2, VMEM_LIMIT),
        name=name,
    )(*args)


def _route(x, w, b):
    x_hi, x_lo = _split_bf16(x)
    w_hi, w_lo = _split_bf16(w)
    logits = (jnp.dot(x_hi, w_hi, preferred_element_type=F32) + jnp.dot(x_lo, w_hi, preferred_element_type=F32)
              + jnp.dot(x_hi, w_lo, preferred_element_type=F32)) + b
    tm, n = logits.shape
    lane = lax.broadcasted_iota(jnp.int32, (tm, n), 1)
    real = lane < N_EXPERTS
    logits = jnp.where(real, logits, -jnp.inf)
    e = jnp.exp(logits - jnp.max(logits, axis=-1, keepdims=True))
    aff = e / jnp.sum(e, axis=-1, keepdims=True)
    group = lane // EXPERTS_PER_GROUP
    best = jnp.full((tm, 1), -jnp.inf, F32)
    g_sel = jnp.zeros((tm, 1), jnp.int32)
    for g in range(N_GROUPS):
        gmax = jnp.max(jnp.where(group == g, aff, -jnp.inf), axis=-1, keepdims=True)
        better = gmax > best
        g_sel = jnp.where(better, g, g_sel)
        best = jnp.where(better, gmax, best)
    in_group = jnp.logical_and(group == g_sel, real)
    cand = jnp.where(in_group, aff, -jnp.inf)
    v1 = jnp.max(cand, axis=-1, keepdims=True)
    i1 = jnp.min(jnp.where(cand == v1, lane, n), axis=-1, keepdims=True)
    cand2 = jnp.where(lane == i1, -jnp.inf, cand)
    v2 = jnp.max(cand2, axis=-1, keepdims=True)
    i2 = jnp.min(jnp.where(cand2 == v2, lane, n), axis=-1, keepdims=True)
    tot = v1 + v2
    chosen = jnp.where(jnp.logical_or(lane == i1, lane == i2), 1.0, 0.0)
    r = lax.broadcasted_iota(jnp.int32, (tm, tm), 0)
    c = lax.broadcasted_iota(jnp.int32, (tm, tm), 1)
    earlier = jnp.where(c < r, 1.0, 0.0).astype(BF16)
    before = jnp.dot(earlier, chosen.astype(BF16), preferred_element_type=F32)
    rank1 = jnp.sum(jnp.where(lane == i1, before, 0.0), axis=-1, keepdims=True)
    rank2 = jnp.sum(jnp.where(lane == i2, before, 0.0), axis=-1, keepdims=True)
    route = jnp.where(lane == 0, v1 / tot,
                      jnp.where(lane == 1, v2 / tot,
                                jnp.where(lane == 2, i1.astype(F32),
                                          jnp.where(lane == 3, i2.astype(F32),
                                                    jnp.where(lane == 4, rank1,
                                                              jnp.where(lane == 5, rank2, 0.0))))))
    return route, jnp.sum(chosen, axis=0, keepdims=True)


def _dispatch_plan(route, counts, tm):
    s = route.shape[0]
    nt = s // tm
    n_tiles = (MOE_TOP_K * s) // tm + N_EXPERTS
    ids = jnp.arange(N_EXPERTS, dtype=jnp.int32)
    expert = route[:, 2:2 + MOE_TOP_K].astype(jnp.int32).reshape(nt, tm, MOE_TOP_K)
    rank = route[:, 4:4 + MOE_TOP_K].astype(jnp.int32).reshape(nt, tm, MOE_TOP_K)
    cnt = counts[:, 0, :N_EXPERTS].astype(jnp.int32)
    tile_base = jnp.cumsum(cnt, axis=0) - cnt
    padded = -(-jnp.sum(cnt, axis=0) // tm) * tm
    seg_end = jnp.cumsum(padded)
    base = (seg_end - padded)[None, :] + tile_base
    pick = expert[..., None] == ids
    pair_row = (jnp.sum(jnp.where(pick, base[:, None, None, :], 0), axis=-1) + rank).reshape(s, MOE_TOP_K)
    tile_start = jnp.arange(n_tiles, dtype=jnp.int32) * tm
    tile_expert = jnp.minimum(jnp.sum((tile_start[:, None] >= seg_end[None, :]).astype(jnp.int32), axis=1),
                              N_EXPERTS - 1).astype(jnp.int32)
    tile_live = (tile_start < seg_end[-1]).astype(jnp.int32)
    row_token = (jnp.arange(n_tiles * tm, dtype=jnp.int32) % s).at[pair_row.reshape(-1)].set(
        jnp.arange(MOE_TOP_K * s, dtype=jnp.int32) // MOE_TOP_K)
    return pair_row, row_token, tile_expert, tile_live


def _moe_group_kernel(te_ref, live_ref, x_ref, wg_ref, wu_ref, wd_ref, o_ref, xb_ref, acc_ref):
    g = pl.program_id(0)
    f = pl.program_id(1)
    last = f == pl.num_programs(1) - 1
    live = live_ref[g] > 0

    @pl.when(jnp.logical_and(live, f == 0))
    def _():
        lo, hi = _unpack_bf16_pairs(x_ref[...])
        xb_ref[...] = jnp.concatenate([lo.astype(BF16), hi.astype(BF16)], axis=1)

    @pl.when(jnp.logical_and(jnp.logical_not(live), last))
    def _():
        o_ref[...] = jnp.zeros(o_ref.shape, o_ref.dtype)

    @pl.when(live)
    def _():
        xb = xb_ref[...]
        a = jnp.dot(xb, wg_ref[...], preferred_element_type=F32)
        u = jnp.dot(xb, wu_ref[...], preferred_element_type=F32)
        hid = ((a * jax.nn.sigmoid(a)) * u).astype(BF16)
        y = jnp.dot(hid, wd_ref[...], preferred_element_type=F32)

        @pl.when(f == 0)
        def _():
            acc_ref[...] = y

        @pl.when(f > 0)
        def _():
            acc_ref[...] += y

        @pl.when(last)
        def _():
            o_ref[...] = _pack_bf16_pairs(acc_ref[...])


def _moe_group_ffn(xs, tile_expert, tile_live, wg, wu, wd, tm, tf):
    rows, half = xs.shape
    d = 2 * half
    fdim = wg.shape[2]
    grid_spec = pltpu.PrefetchScalarGridSpec(
        num_scalar_prefetch=2,
        grid=(rows // tm, fdim // tf),
        in_specs=[pl.BlockSpec((tm, half), lambda g, f, te, lv: (g, 0)),
                  pl.BlockSpec((None, d, tf), lambda g, f, te, lv: (te[g], 0, f)),
                  pl.BlockSpec((None, d, tf), lambda g, f, te, lv: (te[g], 0, f)),
                  pl.BlockSpec((None, tf, d), lambda g, f, te, lv: (te[g], f, 0))],
        out_specs=pl.BlockSpec((tm, half), lambda g, f, te, lv: (g, 0)),
        scratch_shapes=[pltpu.VMEM((tm, d), BF16), pltpu.VMEM((tm, d), F32)])
    return pl.pallas_call(
        _moe_group_kernel,
        out_shape=jax.ShapeDtypeStruct((rows, half), jnp.uint32),
        grid_spec=grid_spec,
        compiler_params=_cparams(2, VMEM_LIMIT),
        name="moe_group_ffn",
    )(tile_expert, tile_live, xs, wg, wu, wd)


def _moe_combine_ln_kernel(x_ref, r_ref, y0_ref, y1_ref, g_ref, b_ref, o_ref):
    r = r_ref[...]
    lo0, hi0 = _unpack_bf16_pairs(y0_ref[...])
    lo1, hi1 = _unpack_bf16_pairs(y1_ref[...])
    ff = jnp.concatenate([r[:, 0:1] * lo0 + r[:, 1:2] * lo1, r[:, 0:1] * hi0 + r[:, 1:2] * hi1], axis=1)
    o_ref[...] = _layer_norm(ALPHA * x_ref[...] + ff, g_ref[...], b_ref[...])


def _moe_combine_ln(x, route, y, g, b, tm):
    s, d = x.shape
    nt = s // tm
    return pl.pallas_call(
        _moe_combine_ln_kernel,
        out_shape=jax.ShapeDtypeStruct((s, d), F32),
        grid=(nt,),
        in_specs=[pl.BlockSpec((tm, d), lambda i: (i, 0)),
                  pl.BlockSpec((tm, LANES), lambda i: (i, 0)),
                  pl.BlockSpec((tm, d // 2), lambda i: (i, 0)),
                  pl.BlockSpec((tm, d // 2), lambda i: (i + nt, 0)),
                  pl.BlockSpec((1, d), lambda i: (0, 0)),
                  pl.BlockSpec((1, d), lambda i: (0, 0))],
        out_specs=pl.BlockSpec((tm, d), lambda i: (i, 0)),
        compiler_params=_cparams(1, VMEM_LIMIT),
        name="moe_combine_ln",
    )(x, route, y, y, g.reshape(1, d), b.reshape(1, d))


def _sc_row_gather(src, idx):
    n = idx.shape[0]
    d = src.shape[1]
    workers = SC_CORES * SC_SUBCORES
    per_worker = n // workers
    assert n % (workers * SC_GATHER_ROWS) == 0
    mesh = plsc.VectorSubcoreMesh(core_axis_name="c", subcore_axis_name="s")

    ch = SC_GATHER_ROWS
    n_chunks = per_worker // ch
    assert n_chunks % 2 == 0

    @functools.partial(
        pl.kernel, mesh=mesh, out_type=jax.ShapeDtypeStruct((n, d), src.dtype),
        scratch_types=[pltpu.VMEM((per_worker,), jnp.int32),
                       pltpu.VMEM((ch, d), src.dtype), pltpu.VMEM((ch, d), src.dtype),
                       pltpu.SemaphoreType.DMA, pltpu.SemaphoreType.DMA],
        name="sc_row_gather")
    def gather(src_hbm, idx_hbm, out_hbm, idx_v, rows_a, rows_b, sem_a, sem_b):
        base = (lax.axis_index("s") * SC_CORES + lax.axis_index("c")) * per_worker
        pltpu.sync_copy(idx_hbm.at[pl.ds(base, per_worker)], idx_v)

        def start(c, rows, sem):
            pltpu.async_copy(src_hbm.at[idx_v.at[pl.ds(c * ch, ch)]], rows, sem)

        def finish(c, rows, sem):
            pltpu.make_async_copy(src_hbm.at[pl.ds(0, ch)], rows, sem).wait()
            pltpu.sync_copy(rows, out_hbm.at[pl.ds(base + c * ch, ch)])

        start(0, rows_a, sem_a)

        @pl.loop(0, n_chunks, step=2)
        def _(c):
            start(c + 1, rows_b, sem_b)
            finish(c, rows_a, sem_a)

            @pl.when(c + 2 < n_chunks)
            def _():
                start(c + 2, rows_a, sem_a)

            finish(c + 1, rows_b, sem_b)

    return gather(src, idx)


def _moe_ln(x, x_packed, route, counts, wg, wu, wd, g, b, tm):
    pair_row, row_token, tile_expert, tile_live = _dispatch_plan(route, counts, tm)
    xs = _sc_row_gather(x_packed, row_token)
    ys = _moe_group_ffn(xs, tile_expert, tile_live, wg, wu, wd, tm, MOE_FF_TILE)
    slot_major = pair_row.T.reshape(-1)
    y = _sc_row_gather(ys, slot_major)
    return _moe_combine_ln(x, route, y, g, b, tm)


def kernel(x, w_in_even, w_out_even, forget_bias, w_in_odd, w_out_odd, lambda_q, lambda_k,
           subln_gain, rel_bias, w_router, b_router, w_gate, w_up, w_down, ln_gain, ln_bias):
    bsz, s, d = x.shape
    assert bsz == 1 and d == D_MODEL and s % MOBA_BLOCK == 0
    t = ATT_TILE
    nh = N_HEADS_GROUP
    gw = GROUP_WIDTH
    tm_proj = min(1024, s)
    tm_row = min(512, s)
    tk = min(ATT_KEYS, s)
    n_near = _near_tile_count(t)
    assert s % tk == 0 and tk // t + 1 >= n_near
    nblk = s // MOBA_BLOCK
    nbp = -(-nblk // LANES) * LANES

    x2 = x.reshape(s, d)
    bias_tiles = _bias_tiles(rel_bias, t, n_near)
    tab_spec = pl.BlockSpec(memory_space=pltpu.SMEM)
    bias_spec = pl.BlockSpec((None, n_near + 2, t, t), lambda hd, i: (hd, 0, 0, 0))
    block_onehot = (jnp.arange(s, dtype=jnp.int32)[:, None] // MOBA_BLOCK
                    == jnp.arange(nbp, dtype=jnp.int32)[None, :]).astype(BF16)

    for layer in range(DEPTH):
        li = layer // 2
        if layer % 2 == 0:
            w = w_in_even[li]
            fcol = 3 * gw
            w_main = jnp.concatenate([w[:, :fcol], w[:, fcol + nh:]], axis=1).astype(BF16)
            w_f = jnp.zeros((d, LANES), BF16).at[:, :nh].set(w[:, fcol:fcol + nh].astype(BF16))
            h = _in_proj(x2, w_main, BF16, tm_proj, PROJ_TN)
            f = _in_proj(x2, w_f, F32, tm_proj, LANES)
            c = _decay_cumsum(f[:, :nh].T, forget_bias[li])
            oa = _attention_call(
                functools.partial(_fox_kernel, t=FOX_TQ, tk=min(FOX_TK, s), scale=HEAD_DIM ** -0.5), h, gw, 0,
                ([c[:, min(FOX_TK, s) - 1::min(FOX_TK, s)], c[:, ::FOX_TQ]],
                 [c.reshape(nh, 1, s), c.reshape(nh, s, 1)]),
                ([tab_spec, tab_spec],
                 [pl.BlockSpec((None, 1, s), lambda hd, i: (hd, 0, 0)),
                  pl.BlockSpec((None, FOX_TQ, 1), lambda hd, i: (hd, i, 0))]),
                [pltpu.VMEM((SUBLANES, LANES), F32)],
                "fox_attention", FOX_TQ)
            ob = _attention_call(
                functools.partial(_moba_kernel, tb=t, nb=MOBA_Q_BLOCKS, tk=tk, nblk=nblk, n_near=n_near,
                                  scale=HEAD_DIM ** -0.5),
                h, gw, 3 * nh,
                ([rel_bias], [bias_tiles, block_onehot]),
                ([tab_spec], [bias_spec, pl.BlockSpec((s, nbp), lambda hd, i: (0, 0))]),
                [pltpu.VMEM((nbp, HEAD_DIM), F32), pltpu.VMEM((SUBLANES, LANES), F32)],
                "moba_attention", MOBA_Q_BLOCKS * t)
            w_out = w_out_even[li]
        else:
            lambda_init = 0.8 - 0.6 * math.exp(-0.3 * layer)
            h = _in_proj(x2, w_in_odd[li].astype(BF16), BF16, tm_proj, PROJ_TN)
            small = lambda shape: pl.BlockSpec(shape, lambda hd, i: (0, 0))
            oa = _attention_call(
                functools.partial(_diff_kernel, tb=t, nb=DIFF_Q_BLOCKS, tk=tk, n_near=n_near,
                                  scale=DIFF_QK_DIM ** -0.5, lambda_init=lambda_init),
                h, gw, 0,
                ([rel_bias], [bias_tiles, lambda_q[li], lambda_k[li], subln_gain[li].reshape(1, HEAD_DIM)]),
                ([tab_spec], [bias_spec, small((2, DIFF_QK_DIM)), small((2, DIFF_QK_DIM)),
                              small((1, HEAD_DIM))]),
                [pltpu.VMEM((SUBLANES, LANES), F32)],
                "diff_attention", DIFF_Q_BLOCKS * t)
            ob = _attention_call(
                functools.partial(_sb_kernel, t=t, nq=SB_Q_TILES, scale=HEAD_DIM ** -0.5), h, gw, 3 * nh,
                ([], []), ([], []), [], "stickbreak_attention", SB_Q_TILES * t)
            w_out = w_out_odd[li]
        x2, x2_packed = _out_proj_ln(oa, ob, w_out.astype(BF16), x2, ln_gain[layer, 0], ln_bias[layer, 0],
                                     tm_row)
        route, counts = _router(x2, w_router, b_router, tm_row)
        x2 = _moe_ln(x2, x2_packed, route, counts, w_gate[layer].astype(BF16), w_up[layer].astype(BF16),
                     w_down[layer].astype(BF16), ln_gain[layer, 1], ln_bias[layer, 1], tm_row)
    return x2.reshape(bsz, s, d)
```

```python
import functools
import math

import numpy as np
import jax
import jax.numpy as jnp
from jax import lax
from jax.experimental import pallas as pl
from jax.experimental.pallas import tpu as pltpu
from jax.experimental.pallas import tpu_sc as plsc

F32 = jnp.float32
BF16 = jnp.bfloat16

D_MODEL = 2048
DEPTH = 2
HEAD_DIM = 128
N_HEADS_GROUP = 8
GROUP_WIDTH = N_HEADS_GROUP * HEAD_DIM
DIFF_QK_DIM = HEAD_DIM // 2
MOBA_BLOCK = 256
MOBA_TOPK = 3
N_BUCKETS = 32
MAX_EXACT = N_BUCKETS // 2
MAX_DISTANCE = 1024
N_EXPERTS = 16
N_GROUPS = 4
EXPERTS_PER_GROUP = N_EXPERTS // N_GROUPS
MOE_TOP_K = 2
SC_CORES = 2
SC_SUBCORES = 16
SC_GATHER_ROWS = 32
SUBLANES = 8
D_EXPERT = D_MODEL // 2
PROJ_TN = 1024
MOE_FF_TILE = D_EXPERT
ALPHA = (2.0 * DEPTH) ** 0.25
LN_EPS = 1e-5
NEG_INF = -1e30
LOG2E = math.log2(math.e)
LANES = 128
ATT_TILE = 256
ATT_KEYS = 1024
FOX_TQ = 1024
FOX_TK = 1024
MOBA_Q_BLOCKS = 4
SB_Q_TILES = 2
DIFF_Q_BLOCKS = 4
SHIFT_SLACK = 96.0
FLUSH_EXPONENT = -130.0
SB_CUTOFF = -104.0
VMEM_LIMIT = 56 * 1024 * 1024


def _cparams(n_axes, vmem=None):
    return pltpu.CompilerParams(dimension_semantics=("arbitrary",) * n_axes,
                                vmem_limit_bytes=vmem)


def _nt_dot(a, b):
    return lax.dot_general(a, b, (((1,), (1,)), ((), ())), preferred_element_type=F32)


def _split_bf16(x):
    hi = x.astype(BF16)
    lo = (x - hi.astype(F32)).astype(BF16)
    return hi, lo


def _pack_bf16_pairs(x):
    n = x.shape[1] // 2
    lo = pltpu.bitcast(x[:, :n].astype(BF16).astype(F32), jnp.uint32)
    hi = pltpu.bitcast(x[:, n:].astype(BF16).astype(F32), jnp.uint32)
    return jnp.bitwise_or(lax.shift_right_logical(lo, jnp.uint32(16)),
                          jnp.bitwise_and(hi, jnp.uint32(0xFFFF0000)))


def _unpack_bf16_pairs(w):
    lo = pltpu.bitcast(lax.shift_left(w, jnp.uint32(16)), F32)
    hi = pltpu.bitcast(jnp.bitwise_and(w, jnp.uint32(0xFFFF0000)), F32)
    return lo, hi


def _layer_norm(z, g, b):
    mu = jnp.mean(z, axis=-1, keepdims=True)
    zc = z - mu
    var = jnp.mean(zc * zc, axis=-1, keepdims=True)
    return zc * lax.rsqrt(var + LN_EPS) * g + b


def _inproj_kernel(x_ref, w_ref, o_ref, xb_ref):
    @pl.when(pl.program_id(1) == 0)
    def _():
        xb_ref[...] = x_ref[...].astype(BF16)

    o_ref[...] = jnp.dot(xb_ref[...], w_ref[...], preferred_element_type=F32).astype(o_ref.dtype)


def _in_proj(x, w, out_dtype, tm, tn):
    s, d = x.shape
    n = w.shape[1]
    return pl.pallas_call(
        _inproj_kernel,
        out_shape=jax.ShapeDtypeStruct((s, n), out_dtype),
        grid=(s // tm, n // tn),
        in_specs=[pl.BlockSpec((tm, d), lambda i, j: (i, 0)),
                  pl.BlockSpec((d, tn), lambda i, j: (0, j))],
        out_specs=pl.BlockSpec((tm, tn), lambda i, j: (i, j)),
        scratch_shapes=[pltpu.VMEM((tm, d), BF16)],
        compiler_params=_cparams(2, VMEM_LIMIT),
        name="in_proj",
    )(x, w)


def _outproj_ln_kernel(oa_ref, ob_ref, w_ref, x_ref, g_ref, b_ref, o_ref, packed_ref):
    half = o_ref.shape[0] // 2
    for r in (slice(0, half), slice(half, 2 * half)):
        o = jnp.concatenate([oa_ref[r, :], ob_ref[r, :]], axis=1)
        y = jnp.dot(o, w_ref[...], preferred_element_type=F32)
        x1 = _layer_norm(ALPHA * x_ref[r, :] + y, g_ref[...], b_ref[...])
        o_ref[r, :] = x1
        packed_ref[r, :] = _pack_bf16_pairs(x1)


def _out_proj_ln(oa, ob, w, x, g, b, tm):
    s, d = x.shape
    gw = oa.shape[1]
    return pl.pallas_call(
        _outproj_ln_kernel,
        out_shape=(jax.ShapeDtypeStruct((s, d), F32), jax.ShapeDtypeStruct((s, d // 2), jnp.uint32)),
        grid=(s // tm,),
        in_specs=[pl.BlockSpec((tm, gw), lambda i: (i, 0)),
                  pl.BlockSpec((tm, gw), lambda i: (i, 0)),
                  pl.BlockSpec((2 * gw, d), lambda i: (0, 0)),
                  pl.BlockSpec((tm, d), lambda i: (i, 0)),
                  pl.BlockSpec((1, d), lambda i: (0, 0)),
                  pl.BlockSpec((1, d), lambda i: (0, 0))],
        out_specs=(pl.BlockSpec((tm, d), lambda i: (i, 0)),
                   pl.BlockSpec((tm, d // 2), lambda i: (i, 0))),
        compiler_params=_cparams(1, VMEM_LIMIT),
        name="out_proj_ln",
    )(oa, ob, w, x, g.reshape(1, d), b.reshape(1, d))


def _router_kernel(x_ref, w_ref, b_ref, route_ref, count_ref):
    route, counts = _route(x_ref[...], w_ref[...], b_ref[...])
    route_ref[...] = route
    count_ref[...] = jnp.broadcast_to(counts, count_ref.shape)


def _router(x, w_router, b_router, tm):
    s, d = x.shape
    wr = jnp.zeros((d, LANES), F32).at[:, :N_EXPERTS].set(w_router)
    br = jnp.zeros((1, LANES), F32).at[0, :N_EXPERTS].set(b_router)
    return pl.pallas_call(
        _router_kernel,
        out_shape=(jax.ShapeDtypeStruct((s, LANES), F32),
                   jax.ShapeDtypeStruct((s // tm, SUBLANES, LANES), F32)),
        grid=(s // tm,),
        in_specs=[pl.BlockSpec((tm, d), lambda i: (i, 0)),
                  pl.BlockSpec((d, LANES), lambda i: (0, 0)),
                  pl.BlockSpec((1, LANES), lambda i: (0, 0))],
        out_specs=(pl.BlockSpec((tm, LANES), lambda i: (i, 0)),
                   pl.BlockSpec((None, SUBLANES, LANES), lambda i: (i, 0, 0))),
        compiler_params=_cparams(1, VMEM_LIMIT),
        name="router",
    )(x, wr, br)


def _decay_cumsum_kernel(f_ref, b_ref, c_ref):
    nh, s = f_ref.shape
    rows = lax.broadcasted_iota(jnp.int32, (LANES, LANES), 0)
    cols = lax.broadcasted_iota(jnp.int32, (LANES, LANES), 1)
    upper = jnp.where(rows <= cols, 1.0, 0.0).astype(F32)
    bias = b_ref[...]

    def body(n, carry):
        off = pl.multiple_of(n * LANES, LANES)
        z = f_ref[:, pl.ds(off, LANES)] + bias
        logf = jnp.minimum(z, 0.0) - jnp.log1p(jnp.exp(-jnp.abs(z)))
        c = jnp.dot(logf, upper, preferred_element_type=F32,
                    precision=lax.Precision.HIGHEST) + carry
        c_ref[:, pl.ds(off, LANES)] = c * LOG2E
        return carry + jnp.sum(logf, axis=-1, keepdims=True)

    lax.fori_loop(0, s // LANES, body, jnp.zeros((nh, 1), F32))


def _decay_cumsum(f_t, bias):
    nh, s = f_t.shape
    return pl.pallas_call(
        _decay_cumsum_kernel,
        out_shape=jax.ShapeDtypeStruct((nh, s), F32),
        name="decay_cumsum",
    )(f_t, bias.reshape(nh, 1))


def _t5_bucket_np(dist):
    n = np.maximum(dist, 0)
    nf = np.maximum(n, 1).astype(np.float32)
    ratio = np.log(nf / np.float32(MAX_EXACT)) / np.float32(math.log(MAX_DISTANCE / MAX_EXACT))
    large = MAX_EXACT + (ratio.astype(np.float32) * np.float32(N_BUCKETS - MAX_EXACT)).astype(np.int32)
    large = np.minimum(large, N_BUCKETS - 1)
    return np.where(n < MAX_EXACT, n, large).astype(np.int32)


def _near_tile_count(t):
    d = np.arange(0, 4 * MAX_DISTANCE, dtype=np.int64)
    not_last = np.nonzero(_t5_bucket_np(d) != N_BUCKETS - 1)[0]
    d_sat = int(not_last.max()) + 1
    n = 1
    while (n - 1) * t + 1 < d_sat:
        n += 1
    return n


def _bucket_tiles_np(t, n_near):
    r = np.arange(t)[:, None]
    c = np.arange(t)[None, :]
    tiles = []
    for delta in range(n_near):
        dist = delta * t + r - c
        tiles.append(np.where(dist >= 0, _t5_bucket_np(dist), -1))
    tiles.append(np.full((t, t), N_BUCKETS - 1))
    tiles.append(np.full((t, t), -1))
    return np.stack(tiles).astype(np.int32)


def _bias_tiles_kernel(tab_ref, idx_ref, o_ref):
    h = pl.program_id(0)
    idx = idx_ref[...]
    out = jnp.full(idx.shape, NEG_INF, F32)
    for b in range(N_BUCKETS):
        out = jnp.where(idx == b, tab_ref[b, h] * LOG2E, out)
    o_ref[...] = out


def _bias_tiles(rel_bias, t, n_near):
    nh = rel_bias.shape[1]
    idx = jnp.asarray(_bucket_tiles_np(t, n_near))
    n_tiles = idx.shape[0]
    return pl.pallas_call(
        _bias_tiles_kernel,
        out_shape=jax.ShapeDtypeStruct((nh, n_tiles, t, t), F32),
        grid=(nh, n_tiles),
        in_specs=[pl.BlockSpec(memory_space=pltpu.SMEM),
                  pl.BlockSpec((None, t, t), lambda h, n: (n, 0, 0))],
        out_specs=pl.BlockSpec((None, None, t, t), lambda h, n: (h, n, 0, 0)),
        compiler_params=_cparams(2),
        name="bias_tiles",
    )(rel_bias, idx)


def _online_softmax_step(s, v, carry):
    m, l, acc = carry
    m_new = jnp.maximum(m, jnp.max(s, axis=-1, keepdims=True))
    alpha = jnp.exp2(m - m_new)
    p = jnp.exp2(s - m_new)
    l = alpha * l + jnp.sum(p, axis=-1, keepdims=True)
    acc = alpha * acc + jnp.dot(p.astype(BF16), v, preferred_element_type=F32)
    return m_new, l, acc


def _prescale(q, c):
    return (q.astype(F32) * c).astype(BF16)


def _near_bias(bias_ref, blocks, j, tb, tk, n_near):
    nsub = tk // tb
    rows = []
    for i in blocks:
        tiles = []
        for b in range(nsub):
            delta = i - (j * nsub + b)
            tiles.append(bias_ref[jnp.where(delta < 0, n_near + 1, jnp.minimum(delta, n_near))])
        rows.append(jnp.concatenate(tiles, axis=1))
    return rows[0] if len(rows) == 1 else jnp.concatenate(rows, axis=0)


def _biased_sweep(i, t, tk, scores, process, init):
    jd = (i * t) // tk
    carry = process(scores(jd), jd, init, True)
    lo = jnp.maximum(jd - 1, 0)
    carry = lax.fori_loop(lo, jd, lambda j, c: process(scores(j), j, c, True), carry)
    return lax.fori_loop(0, lo, lambda j, c: process(scores(j), j, c, False), carry)


def _softmax_init(rows):
    return (jnp.full((rows, 1), NEG_INF, F32), jnp.zeros((rows, 1), F32),
            jnp.zeros((rows, HEAD_DIM), F32))


def _fox_kernel(cend_ref, cstart_ref, q_ref, k_ref, v_ref, c_ref, ccol_ref, o_ref, knorm_ref,
                *, t, tk, scale):
    h = pl.program_id(0)
    i = pl.program_id(1)

    @pl.when(i == 0)
    def _():
        knorm_ref[...] = jnp.broadcast_to(_max_key_norm(k_ref, tk, 0, HEAD_DIM), knorm_ref.shape)

    q = _prescale(q_ref[...], scale * LOG2E)
    rows = lax.broadcasted_iota(jnp.int32, (t, tk), 0)
    cols = lax.broadcasted_iota(jnp.int32, (t, tk), 1)

    def scores(j):
        off = pl.multiple_of(j * tk, tk)
        return _nt_dot(q, k_ref[pl.ds(off, tk), :]) - c_ref[:, pl.ds(off, tk)]

    def values(j):
        return v_ref[pl.ds(pl.multiple_of(j * tk, tk), tk), :]

    def process(s, j, carry):
        return _online_softmax_step(s, values(j), carry)

    jd = (i * t) // tk
    s_diag = jnp.where(cols <= rows + (i * t - jd * tk), scores(jd), NEG_INF)

    qf = q.astype(F32)
    qnorm = jnp.sqrt(jnp.sum(qf * qf, axis=-1, keepdims=True))
    bound = qnorm * knorm_ref[0:1, 0:1] * (1.0 + 2.0 ** -9) + 2.0 ** -6 - ccol_ref[...]
    slack = jnp.max(bound - jnp.max(s_diag, axis=-1, keepdims=True))

    def shifted_sweep(_):
        def step(s, j, carry):
            l, acc = carry
            p = jnp.exp2(s - bound)
            return (l + jnp.sum(p, axis=-1, keepdims=True),
                    acc + jnp.dot(p.astype(BF16), values(j), preferred_element_type=F32))

        c_first_row = cstart_ref[h, i]
        first_live = jnp.int32(0)
        for j in range(cend_ref.shape[1]):
            dead = jnp.logical_and(j < jd, c_first_row - cend_ref[h, j] < FLUSH_EXPONENT)
            first_live = first_live + dead.astype(jnp.int32)
        carry = step(s_diag, jd, _softmax_init(t)[1:])
        return _paired_loop(jd - first_live, lambda n, c: step(scores(first_live + n), first_live + n, c),
                            carry)

    def running_max_sweep(_):
        carry = _online_softmax_step(s_diag, values(jd), _softmax_init(t))
        return lax.fori_loop(0, jd, lambda j, c: process(scores(j), j, c), carry)[1:]

    l, acc = lax.cond(slack <= SHIFT_SLACK, shifted_sweep, running_max_sweep, 0)
    o_ref[...] = (acc / l).astype(o_ref.dtype)


def _moba_kernel(tab_ref, q_ref, k_ref, v_ref, bias_ref, et_ref, o_ref, kmean_ref, knorm_ref,
                 *, tb, nb, tk, nblk, n_near, scale):
    h = pl.program_id(0)
    g = pl.program_id(1)
    t = nb * tb
    nbp = kmean_ref.shape[0]

    @pl.when(g == 0)
    def _():
        kmean_ref[...] = jnp.zeros(kmean_ref.shape, F32)

        def mean_body(n, _):
            off = pl.multiple_of(n * tb, tb)
            kb = k_ref[pl.ds(off, tb), :].astype(F32)
            kmean_ref[pl.ds(n, 1), :] = jnp.sum(kb, axis=0, keepdims=True) * (1.0 / tb)
            return 0

        lax.fori_loop(0, nblk, mean_body, 0)
        knorm_ref[...] = jnp.broadcast_to(_max_key_norm(k_ref, tk, 0, HEAD_DIM), knorm_ref.shape)

    q = q_ref[...]
    km_hi, km_lo = _split_bf16(kmean_ref[...])
    gate = _nt_dot(q, km_hi) + _nt_dot(q, km_lo)
    blk = lax.broadcasted_iota(jnp.int32, (t, nbp), 1)
    own = g * nb + lax.broadcasted_iota(jnp.int32, (t, 1), 0) // tb
    eligible = blk < own
    sel = jnp.zeros((t, nbp), F32)
    for r in range(MOBA_TOPK):
        cand = jnp.where(eligible, jnp.where(sel > 0.0, -jnp.inf, gate), -jnp.inf)
        vmax = jnp.max(cand, axis=-1, keepdims=True)
        first = jnp.min(jnp.where(cand == vmax, blk, nbp), axis=-1, keepdims=True)
        take = jnp.where(r < own, 1.0, 0.0)
        sel = jnp.maximum(sel, jnp.where(blk == first, take, 0.0))

    sel = jnp.maximum(sel, jnp.where(blk == own, 1.0, 0.0))
    penalty = jnp.where(sel > 0.0, 0.0, NEG_INF).astype(BF16)
    q_aug = jnp.concatenate([_prescale(q, scale * LOG2E), penalty], axis=1)
    b_far = tab_ref[N_BUCKETS - 1, h] * LOG2E

    def scores(j):
        off = pl.multiple_of(j * tk, tk)
        k_aug = jnp.concatenate([k_ref[pl.ds(off, tk), :], et_ref[pl.ds(off, tk), :]], axis=1)
        return _nt_dot(q_aug, k_aug)

    def values(j):
        return v_ref[pl.ds(pl.multiple_of(j * tk, tk), tk), :]

    def near_bias(j):
        return _near_bias(bias_ref, [g * nb + b for b in range(nb)], j, tb, tk, n_near)

    def process(s, j, carry, near):
        return _online_softmax_step(s + (near_bias(j) if near else b_far), values(j), carry)

    qf = q_aug[:, :HEAD_DIM].astype(F32)
    qnorm = jnp.sqrt(jnp.sum(qf * qf, axis=-1, keepdims=True))
    bound = _logit_bound(qnorm, knorm_ref[0:1, 0:1], tab_ref, h)
    l, acc = _shifted_or_running_sweep(g, t, tk, scores, near_bias, values, process, bound, b_far, t)
    o_ref[...] = (acc / l).astype(o_ref.dtype)


def _paired_loop(n, body, carry):
    carry = lax.fori_loop(0, n // 2, lambda p, c: body(2 * p + 1, body(2 * p, c)), carry)
    return lax.fori_loop(0, n % 2, lambda _, c: body(n - 1, c), carry)


def _logit_bound(qnorm, knorm, tab_ref, h):
    b_max = tab_ref[0, h]
    for b in range(1, N_BUCKETS):
        b_max = jnp.maximum(b_max, tab_ref[b, h])
    return qnorm * knorm * (1.0 + 2.0 ** -9) + (b_max * LOG2E + 2.0 ** -6)


def _shifted_or_running_sweep(i, t, tk, scores, near_bias, values, process, bound, b_far, rows):
    jd = (i * t) // tk
    s_diag = scores(jd) + near_bias(jd)
    slack = jnp.max(bound - jnp.max(s_diag, axis=-1, keepdims=True))

    def shifted_sweep(_):
        def step(s, j, carry, shift):
            l, acc = carry
            p = jnp.exp2(s - shift)
            return (l + jnp.sum(p, axis=-1, keepdims=True),
                    acc + jnp.dot(p.astype(BF16), values(j), preferred_element_type=F32))

        carry = step(s_diag, jd, _softmax_init(rows)[1:], bound)
        lo = jnp.maximum(jd - 1, 0)
        carry = lax.fori_loop(lo, jd, lambda j, c: step(scores(j) + near_bias(j), j, c, bound), carry)
        far_shift = bound - b_far
        return _paired_loop(lo, lambda j, c: step(scores(j), j, c, far_shift), carry)

    def running_max_sweep(_):
        return _biased_sweep(i, t, tk, scores, process, _softmax_init(rows))[1:]

    return lax.cond(slack <= SHIFT_SLACK, shifted_sweep, running_max_sweep, 0)


def _max_key_norm(k_ref, tk, lane_lo, lane_hi):
    klane = lax.broadcasted_iota(jnp.int32, (tk, HEAD_DIM), 1)
    keep = jnp.logical_and(klane >= lane_lo, klane < lane_hi)

    def body(n, c):
        kt = k_ref[pl.ds(pl.multiple_of(n * tk, tk), tk), :].astype(F32)
        sq = jnp.sum(jnp.where(keep, kt * kt, 0.0), axis=-1, keepdims=True)
        return jnp.maximum(c, jnp.max(sq, axis=0, keepdims=True))

    return jnp.sqrt(lax.fori_loop(0, k_ref.shape[0] // tk, body, jnp.zeros((1, 1), F32)))


def _diff_kernel(tab_ref, q_ref, k_ref, v_ref, bias_ref, lq_ref, lk_ref, g_ref, o_ref, knorm_ref,
                 *, tb, nb, tk, n_near, scale, lambda_init):
    h = pl.program_id(0)
    i = pl.program_id(1)
    t = nb * tb
    q = _prescale(q_ref[...], scale * LOG2E)
    lane = lax.broadcasted_iota(jnp.int32, (t, HEAD_DIM), 1)
    zero = jnp.zeros_like(q)
    q2 = jnp.concatenate([jnp.where(lane < DIFF_QK_DIM, q, zero),
                          jnp.where(lane >= DIFF_QK_DIM, q, zero)], axis=0)
    b_far = tab_ref[N_BUCKETS - 1, h] * LOG2E

    @pl.when(i == 0)
    def _():
        knorm_ref[0:1, :] = jnp.broadcast_to(_max_key_norm(k_ref, tk, 0, DIFF_QK_DIM), (1, LANES))
        knorm_ref[1:2, :] = jnp.broadcast_to(_max_key_norm(k_ref, tk, DIFF_QK_DIM, HEAD_DIM), (1, LANES))

    def scores(j):
        return _nt_dot(q2, k_ref[pl.ds(pl.multiple_of(j * tk, tk), tk), :])

    def values(j):
        return v_ref[pl.ds(pl.multiple_of(j * tk, tk), tk), :]

    def near_bias(j):
        b = _near_bias(bias_ref, [i * nb + b for b in range(nb)], j, tb, tk, n_near)
        return jnp.concatenate([b, b], axis=0)

    def process(s, j, carry, near):
        return _online_softmax_step(s + (near_bias(j) if near else b_far), values(j), carry)

    q2f = q2.astype(F32)
    qnorm = jnp.sqrt(jnp.sum(q2f * q2f, axis=-1, keepdims=True))
    knorm = jnp.concatenate([jnp.broadcast_to(knorm_ref[0:1, 0:1], (t, 1)),
                             jnp.broadcast_to(knorm_ref[1:2, 0:1], (t, 1))], axis=0)
    bound = _logit_bound(qnorm, knorm, tab_ref, h)
    l, acc = _shifted_or_running_sweep(i, t, tk, scores, near_bias, values, process, bound, b_far, 2 * t)
    o12 = acc / l
    lam_e = jnp.exp(jnp.sum(lq_ref[...] * lk_ref[...], axis=-1, keepdims=True))
    lam = lam_e[0:1, :] - lam_e[1:2, :] + lambda_init
    o = o12[:t, :] - lam * o12[t:, :]
    o = o * lax.rsqrt(jnp.mean(o * o, axis=-1, keepdims=True) + LN_EPS) * g_ref[...]
    o_ref[...] = (o * (1.0 - lambda_init)).astype(o_ref.dtype)


def _sb_kernel(q_ref, k_ref, v_ref, o_ref, *, t, nq, scale):
    g = pl.program_id(1)

    def later_matrix(n):
        r = lax.broadcasted_iota(jnp.int32, (n, n), 0)
        c = lax.broadcasted_iota(jnp.int32, (n, n), 1)
        return jnp.where(r > c, 1.0, 0.0).astype(BF16)

    def block(q, off, nk, rem, acc, shift):
        k = k_ref[pl.ds(off, nk), :]
        v = v_ref[pl.ds(off, nk), :]
        z = _nt_dot(q, k) * scale
        soft = jnp.log1p(jnp.exp(-jnp.abs(z)))
        log_sig = jnp.minimum(z, 0.0) - soft
        log_rem = jnp.minimum(-z, 0.0) - soft
        if shift is not None:
            rows = lax.broadcasted_iota(jnp.int32, (t, nk), 0)
            cols = lax.broadcasted_iota(jnp.int32, (t, nk), 1)
            mask = cols < rows + shift
            log_rem = jnp.where(mask, log_rem, 0.0)
        hi, lo = _split_bf16(log_rem)
        later = later_matrix(nk)
        after = (jnp.dot(hi, later, preferred_element_type=F32)
                 + jnp.dot(lo, later, preferred_element_type=F32))
        a = jnp.exp(log_sig + after + rem)
        if shift is not None:
            a = jnp.where(mask, a, 0.0)
        acc = acc + jnp.dot(a.astype(BF16), v, preferred_element_type=F32)
        rem = rem + jnp.sum(log_rem, axis=-1, keepdims=True)
        return rem, acc

    def live(rem):
        return (jnp.max(rem) > SB_CUTOFF).astype(jnp.int32)

    state = []
    for u in range(nq):
        i = g * nq + u
        q = q_ref[u * t:(u + 1) * t, :]
        first = jnp.maximum(i - 1, 0)
        rem, acc = block(q, pl.multiple_of(first * t, t), 2 * t, jnp.zeros((t, 1), F32),
                         jnp.zeros((t, HEAD_DIM), F32), (i - first) * t)
        state.append((i, q, rem, acc))

    for u, (i, q, rem, acc) in enumerate(state):
        def cond(c):
            j, go, _, _ = c
            return jnp.logical_and(j >= 0, go > 0)

        def body(c, q=q):
            j, _, rem, acc = c
            rem, acc = block(q, pl.multiple_of(j * t, t), t, rem, acc, None)
            return j - 1, live(rem), rem, acc

        _, _, _, acc = lax.while_loop(cond, body, (i - 2, live(rem), rem, acc))
        o_ref[u * t:(u + 1) * t, :] = acc.astype(o_ref.dtype)


def _attention_call(kernel, h, out_cols, col0, extra_in, extra_specs, scratch, name, t):
    s = h.shape[0]
    nh = N_HEADS_GROUP
    in_specs = list(extra_specs[0]) + [
        pl.BlockSpec((t, HEAD_DIM), lambda hd, i: (i, col0 + hd)),
        pl.BlockSpec((s, HEAD_DIM), lambda hd, i: (0, col0 + nh + hd)),
        pl.BlockSpec((s, HEAD_DIM), lambda hd, i: (0, col0 + 2 * nh + hd)),
    ] + list(extra_specs[1])
    args = list(extra_in[0]) + [h, h, h] + list(extra_in[1])
    return pl.pallas_call(
        kernel,
        out_shape=jax.ShapeDtypeStruct((s, out_cols), BF16),
        grid=(nh, s // t),
        in_specs=in_specs,
        out_specs=pl.BlockSpec((t, HEAD_DIM), lambda hd, i: (i, hd)),
        scratch_shapes=scratch,
        compiler_params=_cparams(2, VMEM_LIMIT),
        name=name,
    )(*args)


def _route(x, w, b):
    x_hi, x_lo = _split_bf16(x)
    w_hi, w_lo = _split_bf16(w)
    logits = (jnp.dot(x_hi, w_hi, preferred_element_type=F32) + jnp.dot(x_lo, w_hi, preferred_element_type=F32)
              + jnp.dot(x_hi, w_lo, preferred_element_type=F32)) + b
    tm, n = logits.shape
    lane = lax.broadcasted_iota(jnp.int32, (tm, n), 1)
    real = lane < N_EXPERTS
    logits = jnp.where(real, logits, -jnp.inf)
    e = jnp.exp(logits - jnp.max(logits, axis=-1, keepdims=True))
    aff = e / jnp.sum(e, axis=-1, keepdims=True)
    group = lane // EXPERTS_PER_GROUP
    best = jnp.full((tm, 1), -jnp.inf, F32)
    g_sel = jnp.zeros((tm, 1), jnp.int32)
    for g in range(N_GROUPS):
        gmax = jnp.max(jnp.where(group == g, aff, -jnp.inf), axis=-1, keepdims=True)
        better = gmax > best
        g_sel = jnp.where(better, g, g_sel)
        best = jnp.where(better, gmax, best)
    in_group = jnp.logical_and(group == g_sel, real)
    cand = jnp.where(in_group, aff, -jnp.inf)
    v1 = jnp.max(cand, axis=-1, keepdims=True)
    i1 = jnp.min(jnp.where(cand == v1, lane, n), axis=-1, keepdims=True)
    cand2 = jnp.where(lane == i1, -jnp.inf, cand)
    v2 = jnp.max(cand2, axis=-1, keepdims=True)
    i2 = jnp.min(jnp.where(cand2 == v2, lane, n), axis=-1, keepdims=True)
    tot = v1 + v2
    chosen = jnp.where(jnp.logical_or(lane == i1, lane == i2), 1.0, 0.0)
    r = lax.broadcasted_iota(jnp.int32, (tm, tm), 0)
    c = lax.broadcasted_iota(jnp.int32, (tm, tm), 1)
    earlier = jnp.where(c < r, 1.0, 0.0).astype(BF16)
    before = jnp.dot(earlier, chosen.astype(BF16), preferred_element_type=F32)
    rank1 = jnp.sum(jnp.where(lane == i1, before, 0.0), axis=-1, keepdims=True)
    rank2 = jnp.sum(jnp.where(lane == i2, before, 0.0), axis=-1, keepdims=True)
    route = jnp.where(lane == 0, v1 / tot,
                      jnp.where(lane == 1, v2 / tot,
                                jnp.where(lane == 2, i1.astype(F32),
                                          jnp.where(lane == 3, i2.astype(F32),
                                                    jnp.where(lane == 4, rank1,
                                                              jnp.where(lane == 5, rank2, 0.0))))))
    return route, jnp.sum(chosen, axis=0, keepdims=True)


def _dispatch_plan(route, counts, tm):
    s = route.shape[0]
    nt = s // tm
    n_tiles = (MOE_TOP_K * s) // tm + N_EXPERTS
    ids = jnp.arange(N_EXPERTS, dtype=jnp.int32)
    expert = route[:, 2:2 + MOE_TOP_K].astype(jnp.int32).reshape(nt, tm, MOE_TOP_K)
    rank = route[:, 4:4 + MOE_TOP_K].astype(jnp.int32).reshape(nt, tm, MOE_TOP_K)
    cnt = counts[:, 0, :N_EXPERTS].astype(jnp.int32)
    tile_base = jnp.cumsum(cnt, axis=0) - cnt
    padded = -(-jnp.sum(cnt, axis=0) // tm) * tm
    seg_end = jnp.cumsum(padded)
    base = (seg_end - padded)[None, :] + tile_base
    pick = expert[..., None] == ids
    pair_row = (jnp.sum(jnp.where(pick, base[:, None, None, :], 0), axis=-1) + rank).reshape(s, MOE_TOP_K)
    tile_start = jnp.arange(n_tiles, dtype=jnp.int32) * tm
    tile_expert = jnp.minimum(jnp.sum((tile_start[:, None] >= seg_end[None, :]).astype(jnp.int32), axis=1),
                              N_EXPERTS - 1).astype(jnp.int32)
    tile_live = (tile_start < seg_end[-1]).astype(jnp.int32)
    row_token = (jnp.arange(n_tiles * tm, dtype=jnp.int32) % s).at[pair_row.reshape(-1)].set(
        jnp.arange(MOE_TOP_K * s, dtype=jnp.int32) // MOE_TOP_K)
    return pair_row, row_token, tile_expert, tile_live


def _moe_group_kernel(te_ref, live_ref, x_ref, wg_ref, wu_ref, wd_ref, o_ref, xb_ref, acc_ref):
    g = pl.program_id(0)
    f = pl.program_id(1)
    last = f == pl.num_programs(1) - 1
    live = live_ref[g] > 0

    @pl.when(jnp.logical_and(live, f == 0))
    def _():
        lo, hi = _unpack_bf16_pairs(x_ref[...])
        xb_ref[...] = jnp.concatenate([lo.astype(BF16), hi.astype(BF16)], axis=1)

    @pl.when(jnp.logical_and(jnp.logical_not(live), last))
    def _():
        o_ref[...] = jnp.zeros(o_ref.shape, o_ref.dtype)

    @pl.when(live)
    def _():
        xb = xb_ref[...]
        a = jnp.dot(xb, wg_ref[...], preferred_element_type=F32)
        u = jnp.dot(xb, wu_ref[...], preferred_element_type=F32)
        hid = ((a * jax.nn.sigmoid(a)) * u).astype(BF16)
        y = jnp.dot(hid, wd_ref[...], preferred_element_type=F32)

        @pl.when(f == 0)
        def _():
            acc_ref[...] = y

        @pl.when(f > 0)
        def _():
            acc_ref[...] += y

        @pl.when(last)
        def _():
            o_ref[...] = _pack_bf16_pairs(acc_ref[...])


def _moe_group_ffn(xs, tile_expert, tile_live, wg, wu, wd, tm, tf):
    rows, half = xs.shape
    d = 2 * half
    fdim = wg.shape[2]
    grid_spec = pltpu.PrefetchScalarGridSpec(
        num_scalar_prefetch=2,
        grid=(rows // tm, fdim // tf),
        in_specs=[pl.BlockSpec((tm, half), lambda g, f, te, lv: (g, 0)),
                  pl.BlockSpec((None, d, tf), lambda g, f, te, lv: (te[g], 0, f)),
                  pl.BlockSpec((None, d, tf), lambda g, f, te, lv: (te[g], 0, f)),
                  pl.BlockSpec((None, tf, d), lambda g, f, te, lv: (te[g], f, 0))],
        out_specs=pl.BlockSpec((tm, half), lambda g, f, te, lv: (g, 0)),
        scratch_shapes=[pltpu.VMEM((tm, d), BF16), pltpu.VMEM((tm, d), F32)])
    return pl.pallas_call(
        _moe_group_kernel,
        out_shape=jax.ShapeDtypeStruct((rows, half), jnp.uint32),
        grid_spec=grid_spec,
        compiler_params=_cparams(2, VMEM_LIMIT),
        name="moe_group_ffn",
    )(tile_expert, tile_live, xs, wg, wu, wd)


def _moe_combine_ln_kernel(x_ref, r_ref, y0_ref, y1_ref, g_ref, b_ref, o_ref):
    r = r_ref[...]
    lo0, hi0 = _unpack_bf16_pairs(y0_ref[...])
    lo1, hi1 = _unpack_bf16_pairs(y1_ref[...])
    ff = jnp.concatenate([r[:, 0:1] * lo0 + r[:, 1:2] * lo1, r[:, 0:1] * hi0 + r[:, 1:2] * hi1], axis=1)
    o_ref[...] = _layer_norm(ALPHA * x_ref[...] + ff, g_ref[...], b_ref[...])


def _moe_combine_ln(x, route, y, g, b, tm):
    s, d = x.shape
    nt = s // tm
    return pl.pallas_call(
        _moe_combine_ln_kernel,
        out_shape=jax.ShapeDtypeStruct((s, d), F32),
        grid=(nt,),
        in_specs=[pl.BlockSpec((tm, d), lambda i: (i, 0)),
                  pl.BlockSpec((tm, LANES), lambda i: (i, 0)),
                  pl.BlockSpec((tm, d // 2), lambda i: (i, 0)),
                  pl.BlockSpec((tm, d // 2), lambda i: (i + nt, 0)),
                  pl.BlockSpec((1, d), lambda i: (0, 0)),
                  pl.BlockSpec((1, d), lambda i: (0, 0))],
        out_specs=pl.BlockSpec((tm, d), lambda i: (i, 0)),
        compiler_params=_cparams(1, VMEM_LIMIT),
        name="moe_combine_ln",
    )(x, route, y, y, g.reshape(1, d), b.reshape(1, d))


def _sc_row_gather(src, idx):
    n = idx.shape[0]
    d = src.shape[1]
    workers = SC_CORES * SC_SUBCORES
    per_worker = n // workers
    assert n % (workers * SC_GATHER_ROWS) == 0
    mesh = plsc.VectorSubcoreMesh(core_axis_name="c", subcore_axis_name="s")

    ch = SC_GATHER_ROWS
    n_chunks = per_worker // ch
    assert n_chunks % 2 == 0

    @functools.partial(
        pl.kernel, mesh=mesh, out_type=jax.ShapeDtypeStruct((n, d), src.dtype),
        scratch_types=[pltpu.VMEM((per_worker,), jnp.int32),
                       pltpu.VMEM((ch, d), src.dtype), pltpu.VMEM((ch, d), src.dtype),
                       pltpu.SemaphoreType.DMA, pltpu.SemaphoreType.DMA],
        name="sc_row_gather")
    def gather(src_hbm, idx_hbm, out_hbm, idx_v, rows_a, rows_b, sem_a, sem_b):
        base = (lax.axis_index("s") * SC_CORES + lax.axis_index("c")) * per_worker
        pltpu.sync_copy(idx_hbm.at[pl.ds(base, per_worker)], idx_v)

        def start(c, rows, sem):
            pltpu.async_copy(src_hbm.at[idx_v.at[pl.ds(c * ch, ch)]], rows, sem)

        def finish(c, rows, sem):
            pltpu.make_async_copy(src_hbm.at[pl.ds(0, ch)], rows, sem).wait()
            pltpu.sync_copy(rows, out_hbm.at[pl.ds(base + c * ch, ch)])

        start(0, rows_a, sem_a)

        @pl.loop(0, n_chunks, step=2)
        def _(c):
            start(c + 1, rows_b, sem_b)
            finish(c, rows_a, sem_a)

            @pl.when(c + 2 < n_chunks)
            def _():
                start(c + 2, rows_a, sem_a)

            finish(c + 1, rows_b, sem_b)

    return gather(src, idx)


def _moe_ln(x, x_packed, route, counts, wg, wu, wd, g, b, tm):
    pair_row, row_token, tile_expert, tile_live = _dispatch_plan(route, counts, tm)
    xs = _sc_row_gather(x_packed, row_token)
    ys = _moe_group_ffn(xs, tile_expert, tile_live, wg, wu, wd, tm, MOE_FF_TILE)
    slot_major = pair_row.T.reshape(-1)
    y = _sc_row_gather(ys, slot_major)
    return _moe_combine_ln(x, route, y, g, b, tm)


def kernel(x, w_in_even, w_out_even, forget_bias, w_in_odd, w_out_odd, lambda_q, lambda_k,
           subln_gain, rel_bias, w_router, b_router, w_gate, w_up, w_down, ln_gain, ln_bias):
    bsz, s, d = x.shape
    assert bsz == 1 and d == D_MODEL and s % MOBA_BLOCK == 0
    t = ATT_TILE
    nh = N_HEADS_GROUP
    gw = GROUP_WIDTH
    tm_proj = min(1024, s)
    tm_row = min(512, s)
    tk = min(ATT_KEYS, s)
    n_near = _near_tile_count(t)
    assert s % tk == 0 and tk // t + 1 >= n_near
    assert all(tk % (n * t) == 0 for n in (MOBA_Q_BLOCKS, DIFF_Q_BLOCKS)) and min(FOX_TK, s) % FOX_TQ == 0
    nblk = s // MOBA_BLOCK
    nbp = -(-nblk // LANES) * LANES

    x2 = x.reshape(s, d)
    bias_tiles = _bias_tiles(rel_bias, t, n_near)
    tab_spec = pl.BlockSpec(memory_space=pltpu.SMEM)
    bias_spec = pl.BlockSpec((None, n_near + 2, t, t), lambda hd, i: (hd, 0, 0, 0))
    block_onehot = (jnp.arange(s, dtype=jnp.int32)[:, None] // MOBA_BLOCK
                    == jnp.arange(nbp, dtype=jnp.int32)[None, :]).astype(BF16)

    for layer in range(DEPTH):
        li = layer // 2
        if layer % 2 == 0:
            w = w_in_even[li]
            fcol = 3 * gw
            w_main = jnp.concatenate([w[:, :fcol], w[:, fcol + nh:]], axis=1).astype(BF16)
            w_f = jnp.zeros((d, LANES), BF16).at[:, :nh].set(w[:, fcol:fcol + nh].astype(BF16))
            h = _in_proj(x2, w_main, BF16, tm_proj, PROJ_TN)
            f = _in_proj(x2, w_f, F32, tm_proj, LANES)
            c = _decay_cumsum(f[:, :nh].T, forget_bias[li])
            oa = _attention_call(
                functools.partial(_fox_kernel, t=FOX_TQ, tk=min(FOX_TK, s), scale=HEAD_DIM ** -0.5), h, gw, 0,
                ([c[:, min(FOX_TK, s) - 1::min(FOX_TK, s)], c[:, ::FOX_TQ]],
                 [c.reshape(nh, 1, s), c.reshape(nh, s, 1)]),
                ([tab_spec, tab_spec],
                 [pl.BlockSpec((None, 1, s), lambda hd, i: (hd, 0, 0)),
                  pl.BlockSpec((None, FOX_TQ, 1), lambda hd, i: (hd, i, 0))]),
                [pltpu.VMEM((SUBLANES, LANES), F32)],
                "fox_attention", FOX_TQ)
            ob = _attention_call(
                functools.partial(_moba_kernel, tb=t, nb=MOBA_Q_BLOCKS, tk=tk, nblk=nblk, n_near=n_near,
                                  scale=HEAD_DIM ** -0.5),
                h, gw, 3 * nh,
                ([rel_bias], [bias_tiles, block_onehot]),
                ([tab_spec], [bias_spec, pl.BlockSpec((s, nbp), lambda hd, i: (0, 0))]),
                [pltpu.VMEM((nbp, HEAD_DIM), F32), pltpu.VMEM((SUBLANES, LANES), F32)],
                "moba_attention", MOBA_Q_BLOCKS * t)
            w_out = w_out_even[li]
        else:
            lambda_init = 0.8 - 0.6 * math.exp(-0.3 * layer)
            h = _in_proj(x2, w_in_odd[li].astype(BF16), BF16, tm_proj, PROJ_TN)
            small = lambda shape: pl.BlockSpec(shape, lambda hd, i: (0, 0))
            oa = _attention_call(
                functools.partial(_diff_kernel, tb=t, nb=DIFF_Q_BLOCKS, tk=tk, n_near=n_near,
                                  scale=DIFF_QK_DIM ** -0.5, lambda_init=lambda_init),
                h, gw, 0,
                ([rel_bias], [bias_tiles, lambda_q[li], lambda_k[li], subln_gain[li].reshape(1, HEAD_DIM)]),
                ([tab_spec], [bias_spec, small((2, DIFF_QK_DIM)), small((2, DIFF_QK_DIM)),
                              small((1, HEAD_DIM))]),
                [pltpu.VMEM((SUBLANES, LANES), F32)],
                "diff_attention", DIFF_Q_BLOCKS * t)
            ob = _attention_call(
                functools.partial(_sb_kernel, t=t, nq=SB_Q_TILES, scale=HEAD_DIM ** -0.5), h, gw, 3 * nh,
                ([], []), ([], []), [], "stickbreak_attention", SB_Q_TILES * t)
            w_out = w_out_odd[li]
        x2, x2_packed = _out_proj_ln(oa, ob, w_out.astype(BF16), x2, ln_gain[layer, 0], ln_bias[layer, 0],
                                     tm_row)
        route, counts = _router(x2, w_router, b_router, tm_row)
        x2 = _moe_ln(x2, x2_packed, route, counts, w_gate[layer].astype(BF16), w_up[layer].astype(BF16),
                     w_down[layer].astype(BF16), ln_gain[layer, 1], ln_bias[layer, 1], tm_row)
    return x2.reshape(bsz, s, d)
```

```python
import functools
import math

import numpy as np
import jax
import jax.numpy as jnp
from jax import lax
from jax.experimental import pallas as pl
from jax.experimental.pallas import tpu as pltpu
from jax.experimental.pallas import tpu_sc as plsc

F32 = jnp.float32
BF16 = jnp.bfloat16

D_MODEL = 2048
DEPTH = 2
HEAD_DIM = 128
N_HEADS_GROUP = 8
GROUP_WIDTH = N_HEADS_GROUP * HEAD_DIM
DIFF_QK_DIM = HEAD_DIM // 2
MOBA_BLOCK = 256
MOBA_TOPK = 3
N_BUCKETS = 32
MAX_EXACT = N_BUCKETS // 2
MAX_DISTANCE = 1024
N_EXPERTS = 16
N_GROUPS = 4
EXPERTS_PER_GROUP = N_EXPERTS // N_GROUPS
MOE_TOP_K = 2
SC_CORES = 2
SC_SUBCORES = 16
SC_GATHER_ROWS = 32
SUBLANES = 8
D_EXPERT = D_MODEL // 2
PROJ_TN = 1024
MOE_FF_TILE = D_EXPERT
ALPHA = (2.0 * DEPTH) ** 0.25
LN_EPS = 1e-5
NEG_INF = -1e30
LOG2E = math.log2(math.e)
LANES = 128
ATT_TILE = 256
ATT_KEYS = 1024
FOX_TQ = 1024
FOX_TK = 1024
MOBA_Q_BLOCKS = 4
SB_Q_TILES = 2
DIFF_Q_BLOCKS = 4
SHIFT_SLACK = 96.0
FLUSH_EXPONENT = -130.0
SB_CUTOFF = -104.0
VMEM_LIMIT = 56 * 1024 * 1024


def _cparams(n_axes, vmem=None):
    return pltpu.CompilerParams(dimension_semantics=("arbitrary",) * n_axes,
                                vmem_limit_bytes=vmem)


def _nt_dot(a, b):
    return lax.dot_general(a, b, (((1,), (1,)), ((), ())), preferred_element_type=F32)


def _split_bf16(x):
    hi = x.astype(BF16)
    lo = (x - hi.astype(F32)).astype(BF16)
    return hi, lo


def _pack_bf16_pairs(x):
    n = x.shape[1] // 2
    lo = pltpu.bitcast(x[:, :n].astype(BF16).astype(F32), jnp.uint32)
    hi = pltpu.bitcast(x[:, n:].astype(BF16).astype(F32), jnp.uint32)
    return jnp.bitwise_or(lax.shift_right_logical(lo, jnp.uint32(16)),
                          jnp.bitwise_and(hi, jnp.uint32(0xFFFF0000)))


def _unpack_bf16_pairs(w):
    lo = pltpu.bitcast(lax.shift_left(w, jnp.uint32(16)), F32)
    hi = pltpu.bitcast(jnp.bitwise_and(w, jnp.uint32(0xFFFF0000)), F32)
    return lo, hi


def _layer_norm(z, g, b):
    mu = jnp.mean(z, axis=-1, keepdims=True)
    zc = z - mu
    var = jnp.mean(zc * zc, axis=-1, keepdims=True)
    return zc * lax.rsqrt(var + LN_EPS) * g + b


def _inproj_kernel(x_ref, w_ref, o_ref, xb_ref):
    @pl.when(pl.program_id(1) == 0)
    def _():
        xb_ref[...] = x_ref[...].astype(BF16)

    o_ref[...] = jnp.dot(xb_ref[...], w_ref[...], preferred_element_type=F32).astype(o_ref.dtype)


def _in_proj(x, w, out_dtype, tm, tn):
    s, d = x.shape
    n = w.shape[1]
    return pl.pallas_call(
        _inproj_kernel,
        out_shape=jax.ShapeDtypeStruct((s, n), out_dtype),
        grid=(s // tm, n // tn),
        in_specs=[pl.BlockSpec((tm, d), lambda i, j: (i, 0)),
                  pl.BlockSpec((d, tn), lambda i, j: (0, j))],
        out_specs=pl.BlockSpec((tm, tn), lambda i, j: (i, j)),
        scratch_shapes=[pltpu.VMEM((tm, d), BF16)],
        compiler_params=_cparams(2, VMEM_LIMIT),
        name="in_proj",
    )(x, w)


def _outproj_ln_kernel(oa_ref, ob_ref, w_ref, x_ref, g_ref, b_ref, o_ref, packed_ref):
    half = o_ref.shape[0] // 2
    for r in (slice(0, half), slice(half, 2 * half)):
        o = jnp.concatenate([oa_ref[r, :], ob_ref[r, :]], axis=1)
        y = jnp.dot(o, w_ref[...], preferred_element_type=F32)
        x1 = _layer_norm(ALPHA * x_ref[r, :] + y, g_ref[...], b_ref[...])
        o_ref[r, :] = x1
        packed_ref[r, :] = _pack_bf16_pairs(x1)


def _out_proj_ln(oa, ob, w, x, g, b, tm):
    s, d = x.shape
    gw = oa.shape[1]
    return pl.pallas_call(
        _outproj_ln_kernel,
        out_shape=(jax.ShapeDtypeStruct((s, d), F32), jax.ShapeDtypeStruct((s, d // 2), jnp.uint32)),
        grid=(s // tm,),
        in_specs=[pl.BlockSpec((tm, gw), lambda i: (i, 0)),
                  pl.BlockSpec((tm, gw), lambda i: (i, 0)),
                  pl.BlockSpec((2 * gw, d), lambda i: (0, 0)),
                  pl.BlockSpec((tm, d), lambda i: (i, 0)),
                  pl.BlockSpec((1, d), lambda i: (0, 0)),
                  pl.BlockSpec((1, d), lambda i: (0, 0))],
        out_specs=(pl.BlockSpec((tm, d), lambda i: (i, 0)),
                   pl.BlockSpec((tm, d // 2), lambda i: (i, 0))),
        compiler_params=_cparams(1, VMEM_LIMIT),
        name="out_proj_ln",
    )(oa, ob, w, x, g.reshape(1, d), b.reshape(1, d))


def _router_kernel(x_ref, w_ref, b_ref, route_ref, count_ref):
    route, counts = _route(x_ref[...], w_ref[...], b_ref[...])
    route_ref[...] = route
    count_ref[...] = jnp.broadcast_to(counts, count_ref.shape)


def _router(x, w_router, b_router, tm):
    s, d = x.shape
    wr = jnp.zeros((d, LANES), F32).at[:, :N_EXPERTS].set(w_router)
    br = jnp.zeros((1, LANES), F32).at[0, :N_EXPERTS].set(b_router)
    return pl.pallas_call(
        _router_kernel,
        out_shape=(jax.ShapeDtypeStruct((s, LANES), F32),
                   jax.ShapeDtypeStruct((s // tm, SUBLANES, LANES), F32)),
        grid=(s // tm,),
        in_specs=[pl.BlockSpec((tm, d), lambda i: (i, 0)),
                  pl.BlockSpec((d, LANES), lambda i: (0, 0)),
                  pl.BlockSpec((1, LANES), lambda i: (0, 0))],
        out_specs=(pl.BlockSpec((tm, LANES), lambda i: (i, 0)),
                   pl.BlockSpec((None, SUBLANES, LANES), lambda i: (i, 0, 0))),
        compiler_params=_cparams(1, VMEM_LIMIT),
        name="router",
    )(x, wr, br)


def _decay_cumsum_kernel(f_ref, b_ref, c_ref):
    nh, s = f_ref.shape
    rows = lax.broadcasted_iota(jnp.int32, (LANES, LANES), 0)
    cols = lax.broadcasted_iota(jnp.int32, (LANES, LANES), 1)
    upper = jnp.where(rows <= cols, 1.0, 0.0).astype(F32)
    bias = b_ref[...]

    def body(n, carry):
        off = pl.multiple_of(n * LANES, LANES)
        z = f_ref[:, pl.ds(off, LANES)] + bias
        logf = jnp.minimum(z, 0.0) - jnp.log1p(jnp.exp(-jnp.abs(z)))
        c = jnp.dot(logf, upper, preferred_element_type=F32,
                    precision=lax.Precision.HIGHEST) + carry
        c_ref[:, pl.ds(off, LANES)] = c * LOG2E
        return carry + jnp.sum(logf, axis=-1, keepdims=True)

    lax.fori_loop(0, s // LANES, body, jnp.zeros((nh, 1), F32))


def _decay_cumsum(f_t, bias):
    nh, s = f_t.shape
    return pl.pallas_call(
        _decay_cumsum_kernel,
        out_shape=jax.ShapeDtypeStruct((nh, s), F32),
        name="decay_cumsum",
    )(f_t, bias.reshape(nh, 1))


def _t5_bucket_np(dist):
    n = np.maximum(dist, 0)
    nf = np.maximum(n, 1).astype(np.float32)
    ratio = np.log(nf / np.float32(MAX_EXACT)) / np.float32(math.log(MAX_DISTANCE / MAX_EXACT))
    large = MAX_EXACT + (ratio.astype(np.float32) * np.float32(N_BUCKETS - MAX_EXACT)).astype(np.int32)
    large = np.minimum(large, N_BUCKETS - 1)
    return np.where(n < MAX_EXACT, n, large).astype(np.int32)


def _near_tile_count(t):
    d = np.arange(0, 4 * MAX_DISTANCE, dtype=np.int64)
    not_last = np.nonzero(_t5_bucket_np(d) != N_BUCKETS - 1)[0]
    d_sat = int(not_last.max()) + 1
    n = 1
    while (n - 1) * t + 1 < d_sat:
        n += 1
    return n


def _bucket_tiles_np(t, n_near):
    r = np.arange(t)[:, None]
    c = np.arange(t)[None, :]
    tiles = []
    for delta in range(n_near):
        dist = delta * t + r - c
        tiles.append(np.where(dist >= 0, _t5_bucket_np(dist), -1))
    tiles.append(np.full((t, t), N_BUCKETS - 1))
    tiles.append(np.full((t, t), -1))
    return np.stack(tiles).astype(np.int32)


def _bias_tiles_kernel(tab_ref, idx_ref, o_ref):
    h = pl.program_id(0)
    idx = idx_ref[...]
    out = jnp.full(idx.shape, NEG_INF, F32)
    for b in range(N_BUCKETS):
        out = jnp.where(idx == b, tab_ref[b, h] * LOG2E, out)
    o_ref[...] = out


def _bias_tiles(rel_bias, t, n_near):
    nh = rel_bias.shape[1]
    idx = jnp.asarray(_bucket_tiles_np(t, n_near))
    n_tiles = idx.shape[0]
    return pl.pallas_call(
        _bias_tiles_kernel,
        out_shape=jax.ShapeDtypeStruct((nh, n_tiles, t, t), F32),
        grid=(nh, n_tiles),
        in_specs=[pl.BlockSpec(memory_space=pltpu.SMEM),
                  pl.BlockSpec((None, t, t), lambda h, n: (n, 0, 0))],
        out_specs=pl.BlockSpec((None, None, t, t), lambda h, n: (h, n, 0, 0)),
        compiler_params=_cparams(2),
        name="bias_tiles",
    )(rel_bias, idx)


def _online_softmax_step(s, v, carry):
    m, l, acc = carry
    m_new = jnp.maximum(m, jnp.max(s, axis=-1, keepdims=True))
    alpha = jnp.exp2(m - m_new)
    p = jnp.exp2(s - m_new)
    l = alpha * l + jnp.sum(p, axis=-1, keepdims=True)
    acc = alpha * acc + jnp.dot(p.astype(BF16), v, preferred_element_type=F32)
    return m_new, l, acc


def _prescale(q, c):
    return (q.astype(F32) * c).astype(BF16)


def _near_bias(bias_ref, blocks, j, tb, tk, n_near):
    nsub = tk // tb
    rows = []
    for i in blocks:
        tiles = []
        for b in range(nsub):
            delta = i - (j * nsub + b)
            tiles.append(bias_ref[jnp.where(delta < 0, n_near + 1, jnp.minimum(delta, n_near))])
        rows.append(jnp.concatenate(tiles, axis=1))
    return rows[0] if len(rows) == 1 else jnp.concatenate(rows, axis=0)


def _biased_sweep(i, t, tk, scores, process, init):
    jd = (i * t) // tk
    carry = process(scores(jd), jd, init, True)
    lo = jnp.maximum(jd - 1, 0)
    carry = lax.fori_loop(lo, jd, lambda j, c: process(scores(j), j, c, True), carry)
    return lax.fori_loop(0, lo, lambda j, c: process(scores(j), j, c, False), carry)


def _softmax_init(rows):
    return (jnp.full((rows, 1), NEG_INF, F32), jnp.zeros((rows, 1), F32),
            jnp.zeros((rows, HEAD_DIM), F32))


def _fox_kernel(cend_ref, cstart_ref, q_ref, k_ref, v_ref, c_ref, ccol_ref, o_ref, knorm_ref,
                *, t, tk, scale):
    h = pl.program_id(0)
    i = pl.program_id(1)

    @pl.when(i == 0)
    def _():
        knorm_ref[...] = jnp.broadcast_to(_max_key_norm(k_ref, tk, 0, HEAD_DIM), knorm_ref.shape)

    q = _prescale(q_ref[...], scale * LOG2E)
    rows = lax.broadcasted_iota(jnp.int32, (t, tk), 0)
    cols = lax.broadcasted_iota(jnp.int32, (t, tk), 1)

    def scores(j):
        off = pl.multiple_of(j * tk, tk)
        return _nt_dot(q, k_ref[pl.ds(off, tk), :]) - c_ref[:, pl.ds(off, tk)]

    def values(j):
        return v_ref[pl.ds(pl.multiple_of(j * tk, tk), tk), :]

    def process(s, j, carry):
        return _online_softmax_step(s, values(j), carry)

    jd = (i * t) // tk

    def diag_scores():
        return jnp.where(cols <= rows + (i * t - jd * tk), scores(jd), NEG_INF)

    qf = q.astype(F32)
    qnorm = jnp.sqrt(jnp.sum(qf * qf, axis=-1, keepdims=True))
    bound = qnorm * knorm_ref[0:1, 0:1] * (1.0 + 2.0 ** -9) + 2.0 ** -6 - ccol_ref[...]
    row_floor = _diagonal_logit(qf, k_ref, pl.multiple_of(i * t, t), t) - ccol_ref[...]
    slack = jnp.max(bound - row_floor)

    def shifted_sweep(_):
        def step(s, j, carry):
            l, acc = carry
            p = jnp.exp2(s - bound)
            return (l + jnp.sum(p, axis=-1, keepdims=True),
                    acc + jnp.dot(p.astype(BF16), values(j), preferred_element_type=F32))

        c_first_row = cstart_ref[h, i]
        first_live = jnp.int32(0)
        for j in range(cend_ref.shape[1]):
            dead = jnp.logical_and(j < jd, c_first_row - cend_ref[h, j] < FLUSH_EXPONENT)
            first_live = first_live + dead.astype(jnp.int32)
        carry = step(diag_scores(), jd, _softmax_init(t)[1:])
        return _paired_loop(jd - first_live, lambda n, c: step(scores(first_live + n), first_live + n, c),
                            carry)

    def running_max_sweep(_):
        carry = _online_softmax_step(diag_scores(), values(jd), _softmax_init(t))
        return lax.fori_loop(0, jd, lambda j, c: process(scores(j), j, c), carry)[1:]

    l, acc = lax.cond(slack <= SHIFT_SLACK, shifted_sweep, running_max_sweep, 0)
    o_ref[...] = (acc / l).astype(o_ref.dtype)


def _moba_kernel(tab_ref, q_ref, k_ref, v_ref, bias_ref, et_ref, o_ref, kmean_ref, knorm_ref,
                 *, tb, nb, tk, nblk, n_near, scale):
    h = pl.program_id(0)
    g = pl.program_id(1)
    t = nb * tb
    nbp = kmean_ref.shape[0]

    @pl.when(g == 0)
    def _():
        kmean_ref[...] = jnp.zeros(kmean_ref.shape, F32)

        def mean_body(n, _):
            off = pl.multiple_of(n * tb, tb)
            kb = k_ref[pl.ds(off, tb), :].astype(F32)
            kmean_ref[pl.ds(n, 1), :] = jnp.sum(kb, axis=0, keepdims=True) * (1.0 / tb)
            return 0

        lax.fori_loop(0, nblk, mean_body, 0)
        knorm_ref[...] = jnp.broadcast_to(_max_key_norm(k_ref, tk, 0, HEAD_DIM), knorm_ref.shape)

    q = q_ref[...]
    km_hi, km_lo = _split_bf16(kmean_ref[...])
    gate = _nt_dot(q, km_hi) + _nt_dot(q, km_lo)
    blk = lax.broadcasted_iota(jnp.int32, (t, nbp), 1)
    own = g * nb + lax.broadcasted_iota(jnp.int32, (t, 1), 0) // tb
    eligible = blk < own
    sel = jnp.zeros((t, nbp), F32)
    for r in range(MOBA_TOPK):
        cand = jnp.where(eligible, jnp.where(sel > 0.0, -jnp.inf, gate), -jnp.inf)
        vmax = jnp.max(cand, axis=-1, keepdims=True)
        first = jnp.min(jnp.where(cand == vmax, blk, nbp), axis=-1, keepdims=True)
        take = jnp.where(r < own, 1.0, 0.0)
        sel = jnp.maximum(sel, jnp.where(blk == first, take, 0.0))

    sel = jnp.maximum(sel, jnp.where(blk == own, 1.0, 0.0))
    penalty = jnp.where(sel > 0.0, 0.0, NEG_INF).astype(BF16)
    q_aug = jnp.concatenate([_prescale(q, scale * LOG2E), penalty], axis=1)
    b_far = tab_ref[N_BUCKETS - 1, h] * LOG2E

    def scores(j):
        off = pl.multiple_of(j * tk, tk)
        k_aug = jnp.concatenate([k_ref[pl.ds(off, tk), :], et_ref[pl.ds(off, tk), :]], axis=1)
        return _nt_dot(q_aug, k_aug)

    def values(j):
        return v_ref[pl.ds(pl.multiple_of(j * tk, tk), tk), :]

    def near_bias(j):
        return _near_bias(bias_ref, [g * nb + b for b in range(nb)], j, tb, tk, n_near)

    def process(s, j, carry, near):
        return _online_softmax_step(s + (near_bias(j) if near else b_far), values(j), carry)

    qf = q_aug[:, :HEAD_DIM].astype(F32)
    qnorm = jnp.sqrt(jnp.sum(qf * qf, axis=-1, keepdims=True))
    bound = _logit_bound(qnorm, knorm_ref[0:1, 0:1], tab_ref, h)
    row_floor = _diagonal_logit(qf, k_ref, pl.multiple_of(g * t, t), t) + tab_ref[0, h] * LOG2E
    l, acc = _shifted_or_running_sweep(g, t, tk, scores, near_bias, values, process, bound, row_floor,
                                       b_far, t)
    o_ref[...] = (acc / l).astype(o_ref.dtype)


def _paired_loop(n, body, carry):
    carry = lax.fori_loop(0, n // 2, lambda p, c: body(2 * p + 1, body(2 * p, c)), carry)
    return lax.fori_loop(0, n % 2, lambda _, c: body(n - 1, c), carry)


def _logit_bound(qnorm, knorm, tab_ref, h):
    b_max = tab_ref[0, h]
    for b in range(1, N_BUCKETS):
        b_max = jnp.maximum(b_max, tab_ref[b, h])
    return qnorm * knorm * (1.0 + 2.0 ** -9) + (b_max * LOG2E + 2.0 ** -6)


def _shifted_or_running_sweep(i, t, tk, scores, near_bias, values, process, bound, row_floor, b_far, rows):
    jd = (i * t) // tk
    slack = jnp.max(bound - row_floor)

    def shifted_sweep(_):
        def step(s, j, carry, shift):
            l, acc = carry
            p = jnp.exp2(s - shift)
            return (l + jnp.sum(p, axis=-1, keepdims=True),
                    acc + jnp.dot(p.astype(BF16), values(j), preferred_element_type=F32))

        carry = step(scores(jd) + near_bias(jd), jd, _softmax_init(rows)[1:], bound)
        lo = jnp.maximum(jd - 1, 0)
        carry = lax.fori_loop(lo, jd, lambda j, c: step(scores(j) + near_bias(j), j, c, bound), carry)
        far_shift = bound - b_far
        return _paired_loop(lo, lambda j, c: step(scores(j), j, c, far_shift), carry)

    def running_max_sweep(_):
        return _biased_sweep(i, t, tk, scores, process, _softmax_init(rows))[1:]

    return lax.cond(slack <= SHIFT_SLACK, shifted_sweep, running_max_sweep, 0)


def _diagonal_logit(qf, k_ref, row0, t):
    kf = k_ref[pl.ds(row0, t), :].astype(F32)
    if qf.shape[0] != t:
        kf = jnp.concatenate([kf] * (qf.shape[0] // t), axis=0)
    return jnp.sum(qf * kf, axis=-1, keepdims=True)


def _max_key_norm(k_ref, tk, lane_lo, lane_hi):
    klane = lax.broadcasted_iota(jnp.int32, (tk, HEAD_DIM), 1)
    keep = jnp.logical_and(klane >= lane_lo, klane < lane_hi)

    def body(n, c):
        kt = k_ref[pl.ds(pl.multiple_of(n * tk, tk), tk), :].astype(F32)
        sq = jnp.sum(jnp.where(keep, kt * kt, 0.0), axis=-1, keepdims=True)
        return jnp.maximum(c, jnp.max(sq, axis=0, keepdims=True))

    return jnp.sqrt(lax.fori_loop(0, k_ref.shape[0] // tk, body, jnp.zeros((1, 1), F32)))


def _diff_kernel(tab_ref, q_ref, k_ref, v_ref, bias_ref, lq_ref, lk_ref, g_ref, o_ref, knorm_ref,
                 *, tb, nb, tk, n_near, scale, lambda_init):
    h = pl.program_id(0)
    i = pl.program_id(1)
    t = nb * tb
    q = _prescale(q_ref[...], scale * LOG2E)
    lane = lax.broadcasted_iota(jnp.int32, (t, HEAD_DIM), 1)
    zero = jnp.zeros_like(q)
    q2 = jnp.concatenate([jnp.where(lane < DIFF_QK_DIM, q, zero),
                          jnp.where(lane >= DIFF_QK_DIM, q, zero)], axis=0)
    b_far = tab_ref[N_BUCKETS - 1, h] * LOG2E

    @pl.when(i == 0)
    def _():
        knorm_ref[0:1, :] = jnp.broadcast_to(_max_key_norm(k_ref, tk, 0, DIFF_QK_DIM), (1, LANES))
        knorm_ref[1:2, :] = jnp.broadcast_to(_max_key_norm(k_ref, tk, DIFF_QK_DIM, HEAD_DIM), (1, LANES))

    def scores(j):
        return _nt_dot(q2, k_ref[pl.ds(pl.multiple_of(j * tk, tk), tk), :])

    def values(j):
        return v_ref[pl.ds(pl.multiple_of(j * tk, tk), tk), :]

    def near_bias(j):
        b = _near_bias(bias_ref, [i * nb + b for b in range(nb)], j, tb, tk, n_near)
        return jnp.concatenate([b, b], axis=0)

    def process(s, j, carry, near):
        return _online_softmax_step(s + (near_bias(j) if near else b_far), values(j), carry)

    q2f = q2.astype(F32)
    qnorm = jnp.sqrt(jnp.sum(q2f * q2f, axis=-1, keepdims=True))
    knorm = jnp.concatenate([jnp.broadcast_to(knorm_ref[0:1, 0:1], (t, 1)),
                             jnp.broadcast_to(knorm_ref[1:2, 0:1], (t, 1))], axis=0)
    bound = _logit_bound(qnorm, knorm, tab_ref, h)
    row_floor = _diagonal_logit(q2f, k_ref, pl.multiple_of(i * t, t), t) + tab_ref[0, h] * LOG2E
    l, acc = _shifted_or_running_sweep(i, t, tk, scores, near_bias, values, process, bound, row_floor,
                                       b_far, 2 * t)
    o12 = acc / l
    lam_e = jnp.exp(jnp.sum(lq_ref[...] * lk_ref[...], axis=-1, keepdims=True))
    lam = lam_e[0:1, :] - lam_e[1:2, :] + lambda_init
    o = o12[:t, :] - lam * o12[t:, :]
    o = o * lax.rsqrt(jnp.mean(o * o, axis=-1, keepdims=True) + LN_EPS) * g_ref[...]
    o_ref[...] = (o * (1.0 - lambda_init)).astype(o_ref.dtype)


def _sb_kernel(q_ref, k_ref, v_ref, o_ref, *, t, nq, scale):
    g = pl.program_id(1)

    def later_matrix(n):
        r = lax.broadcasted_iota(jnp.int32, (n, n), 0)
        c = lax.broadcasted_iota(jnp.int32, (n, n), 1)
        return jnp.where(r > c, 1.0, 0.0).astype(BF16)

    def block(q, off, nk, rem, acc, shift):
        k = k_ref[pl.ds(off, nk), :]
        v = v_ref[pl.ds(off, nk), :]
        z = _nt_dot(q, k) * scale
        soft = jnp.log1p(jnp.exp(-jnp.abs(z)))
        log_sig = jnp.minimum(z, 0.0) - soft
        log_rem = jnp.minimum(-z, 0.0) - soft
        if shift is not None:
            rows = lax.broadcasted_iota(jnp.int32, (t, nk), 0)
            cols = lax.broadcasted_iota(jnp.int32, (t, nk), 1)
            mask = cols < rows + shift
            log_rem = jnp.where(mask, log_rem, 0.0)
        hi, lo = _split_bf16(log_rem)
        later = later_matrix(nk)
        after = (jnp.dot(hi, later, preferred_element_type=F32)
                 + jnp.dot(lo, later, preferred_element_type=F32))
        a = jnp.exp(log_sig + after + rem)
        if shift is not None:
            a = jnp.where(mask, a, 0.0)
        acc = acc + jnp.dot(a.astype(BF16), v, preferred_element_type=F32)
        rem = rem + jnp.sum(log_rem, axis=-1, keepdims=True)
        return rem, acc

    def live(rem):
        return (jnp.max(rem) > SB_CUTOFF).astype(jnp.int32)

    state = []
    for u in range(nq):
        i = g * nq + u
        q = q_ref[u * t:(u + 1) * t, :]
        first = jnp.maximum(i - 1, 0)
        rem, acc = block(q, pl.multiple_of(first * t, t), 2 * t, jnp.zeros((t, 1), F32),
                         jnp.zeros((t, HEAD_DIM), F32), (i - first) * t)
        state.append((i, q, rem, acc))

    for u, (i, q, rem, acc) in enumerate(state):
        def cond(c):
            j, go, _, _ = c
            return jnp.logical_and(j >= 0, go > 0)

        def body(c, q=q):
            j, _, rem, acc = c
            rem, acc = block(q, pl.multiple_of(j * t, t), t, rem, acc, None)
            return j - 1, live(rem), rem, acc

        _, _, _, acc = lax.while_loop(cond, body, (i - 2, live(rem), rem, acc))
        o_ref[u * t:(u + 1) * t, :] = acc.astype(o_ref.dtype)


def _attention_call(kernel, h, out_cols, col0, extra_in, extra_specs, scratch, name, t):
    s = h.shape[0]
    nh = N_HEADS_GROUP
    in_specs = list(extra_specs[0]) + [
        pl.BlockSpec((t, HEAD_DIM), lambda hd, i: (i, col0 + hd)),
        pl.BlockSpec((s, HEAD_DIM), lambda hd, i: (0, col0 + nh + hd)),
        pl.BlockSpec((s, HEAD_DIM), lambda hd, i: (0, col0 + 2 * nh + hd)),
    ] + list(extra_specs[1])
    args = list(extra_in[0]) + [h, h, h] + list(extra_in[1])
    return pl.pallas_call(
        kernel,
        out_shape=jax.ShapeDtypeStruct((s, out_cols), BF16),
        grid=(nh, s // t),
        in_specs=in_specs,
        out_specs=pl.BlockSpec((t, HEAD_DIM), lambda hd, i: (i, hd)),
        scratch_shapes=scratch,
        compiler_params=_cparams(2, VMEM_LIMIT),
        name=name,
    )(*args)


def _route(x, w, b):
    x_hi, x_lo = _split_bf16(x)
    w_hi, w_lo = _split_bf16(w)
    logits = (jnp.dot(x_hi, w_hi, preferred_element_type=F32) + jnp.dot(x_lo, w_hi, preferred_element_type=F32)
              + jnp.dot(x_hi, w_lo, preferred_element_type=F32)) + b
    tm, n = logits.shape
    lane = lax.broadcasted_iota(jnp.int32, (tm, n), 1)
    real = lane < N_EXPERTS
    logits = jnp.where(real, logits, -jnp.inf)
    e = jnp.exp(logits - jnp.max(logits, axis=-1, keepdims=True))
    aff = e / jnp.sum(e, axis=-1, keepdims=True)
    group = lane // EXPERTS_PER_GROUP
    best = jnp.full((tm, 1), -jnp.inf, F32)
    g_sel = jnp.zeros((tm, 1), jnp.int32)
    for g in range(N_GROUPS):
        gmax = jnp.max(jnp.where(group == g, aff, -jnp.inf), axis=-1, keepdims=True)
        better = gmax > best
        g_sel = jnp.where(better, g, g_sel)
        best = jnp.where(better, gmax, best)
    in_group = jnp.logical_and(group == g_sel, real)
    cand = jnp.where(in_group, aff, -jnp.inf)
    v1 = jnp.max(cand, axis=-1, keepdims=True)
    i1 = jnp.min(jnp.where(cand == v1, lane, n), axis=-1, keepdims=True)
    cand2 = jnp.where(lane == i1, -jnp.inf, cand)
    v2 = jnp.max(cand2, axis=-1, keepdims=True)
    i2 = jnp.min(jnp.where(cand2 == v2, lane, n), axis=-1, keepdims=True)
    tot = v1 + v2
    chosen = jnp.where(jnp.logical_or(lane == i1, lane == i2), 1.0, 0.0)
    r = lax.broadcasted_iota(jnp.int32, (tm, tm), 0)
    c = lax.broadcasted_iota(jnp.int32, (tm, tm), 1)
    earlier = jnp.where(c < r, 1.0, 0.0).astype(BF16)
    before = jnp.dot(earlier, chosen.astype(BF16), preferred_element_type=F32)
    rank1 = jnp.sum(jnp.where(lane == i1, before, 0.0), axis=-1, keepdims=True)
    rank2 = jnp.sum(jnp.where(lane == i2, before, 0.0), axis=-1, keepdims=True)
    route = jnp.where(lane == 0, v1 / tot,
                      jnp.where(lane == 1, v2 / tot,
                                jnp.where(lane == 2, i1.astype(F32),
                                          jnp.where(lane == 3, i2.astype(F32),
                                                    jnp.where(lane == 4, rank1,
                                                              jnp.where(lane == 5, rank2, 0.0))))))
    return route, jnp.sum(chosen, axis=0, keepdims=True)


def _dispatch_plan(route, counts, tm):
    s = route.shape[0]
    nt = s // tm
    n_tiles = (MOE_TOP_K * s) // tm + N_EXPERTS
    ids = jnp.arange(N_EXPERTS, dtype=jnp.int32)
    expert = route[:, 2:2 + MOE_TOP_K].astype(jnp.int32).reshape(nt, tm, MOE_TOP_K)
    rank = route[:, 4:4 + MOE_TOP_K].astype(jnp.int32).reshape(nt, tm, MOE_TOP_K)
    cnt = counts[:, 0, :N_EXPERTS].astype(jnp.int32)
    tile_base = jnp.cumsum(cnt, axis=0) - cnt
    padded = -(-jnp.sum(cnt, axis=0) // tm) * tm
    seg_end = jnp.cumsum(padded)
    base = (seg_end - padded)[None, :] + tile_base
    pick = expert[..., None] == ids
    pair_row = (jnp.sum(jnp.where(pick, base[:, None, None, :], 0), axis=-1) + rank).reshape(s, MOE_TOP_K)
    tile_start = jnp.arange(n_tiles, dtype=jnp.int32) * tm
    tile_expert = jnp.minimum(jnp.sum((tile_start[:, None] >= seg_end[None, :]).astype(jnp.int32), axis=1),
                              N_EXPERTS - 1).astype(jnp.int32)
    tile_live = (tile_start < seg_end[-1]).astype(jnp.int32)
    row_token = (jnp.arange(n_tiles * tm, dtype=jnp.int32) % s).at[pair_row.reshape(-1)].set(
        jnp.arange(MOE_TOP_K * s, dtype=jnp.int32) // MOE_TOP_K)
    return pair_row, row_token, tile_expert, tile_live


def _moe_group_kernel(te_ref, live_ref, x_ref, wg_ref, wu_ref, wd_ref, o_ref, xb_ref, acc_ref):
    g = pl.program_id(0)
    f = pl.program_id(1)
    last = f == pl.num_programs(1) - 1
    live = live_ref[g] > 0

    @pl.when(jnp.logical_and(live, f == 0))
    def _():
        lo, hi = _unpack_bf16_pairs(x_ref[...])
        xb_ref[...] = jnp.concatenate([lo.astype(BF16), hi.astype(BF16)], axis=1)

    @pl.when(jnp.logical_and(jnp.logical_not(live), last))
    def _():
        o_ref[...] = jnp.zeros(o_ref.shape, o_ref.dtype)

    @pl.when(live)
    def _():
        xb = xb_ref[...]
        a = jnp.dot(xb, wg_ref[...], preferred_element_type=F32)
        u = jnp.dot(xb, wu_ref[...], preferred_element_type=F32)
        hid = ((a * jax.nn.sigmoid(a)) * u).astype(BF16)
        y = jnp.dot(hid, wd_ref[...], preferred_element_type=F32)

        @pl.when(f == 0)
        def _():
            acc_ref[...] = y

        @pl.when(f > 0)
        def _():
            acc_ref[...] += y

        @pl.when(last)
        def _():
            o_ref[...] = _pack_bf16_pairs(acc_ref[...])


def _moe_group_ffn(xs, tile_expert, tile_live, wg, wu, wd, tm, tf):
    rows, half = xs.shape
    d = 2 * half
    fdim = wg.shape[2]
    grid_spec = pltpu.PrefetchScalarGridSpec(
        num_scalar_prefetch=2,
        grid=(rows // tm, fdim // tf),
        in_specs=[pl.BlockSpec((tm, half), lambda g, f, te, lv: (g, 0)),
                  pl.BlockSpec((None, d, tf), lambda g, f, te, lv: (te[g], 0, f)),
                  pl.BlockSpec((None, d, tf), lambda g, f, te, lv: (te[g], 0, f)),
                  pl.BlockSpec((None, tf, d), lambda g, f, te, lv: (te[g], f, 0))],
        out_specs=pl.BlockSpec((tm, half), lambda g, f, te, lv: (g, 0)),
        scratch_shapes=[pltpu.VMEM((tm, d), BF16), pltpu.VMEM((tm, d), F32)])
    return pl.pallas_call(
        _moe_group_kernel,
        out_shape=jax.ShapeDtypeStruct((rows, half), jnp.uint32),
        grid_spec=grid_spec,
        compiler_params=_cparams(2, VMEM_LIMIT),
        name="moe_group_ffn",
    )(tile_expert, tile_live, xs, wg, wu, wd)


def _moe_combine_ln_kernel(x_ref, r_ref, y0_ref, y1_ref, g_ref, b_ref, o_ref):
    r = r_ref[...]
    lo0, hi0 = _unpack_bf16_pairs(y0_ref[...])
    lo1, hi1 = _unpack_bf16_pairs(y1_ref[...])
    ff = jnp.concatenate([r[:, 0:1] * lo0 + r[:, 1:2] * lo1, r[:, 0:1] * hi0 + r[:, 1:2] * hi1], axis=1)
    o_ref[...] = _layer_norm(ALPHA * x_ref[...] + ff, g_ref[...], b_ref[...])


def _moe_combine_ln(x, route, y, g, b, tm):
    s, d = x.shape
    nt = s // tm
    return pl.pallas_call(
        _moe_combine_ln_kernel,
        out_shape=jax.ShapeDtypeStruct((s, d), F32),
        grid=(nt,),
        in_specs=[pl.BlockSpec((tm, d), lambda i: (i, 0)),
                  pl.BlockSpec((tm, LANES), lambda i: (i, 0)),
                  pl.BlockSpec((tm, d // 2), lambda i: (i, 0)),
                  pl.BlockSpec((tm, d // 2), lambda i: (i + nt, 0)),
                  pl.BlockSpec((1, d), lambda i: (0, 0)),
                  pl.BlockSpec((1, d), lambda i: (0, 0))],
        out_specs=pl.BlockSpec((tm, d), lambda i: (i, 0)),
        compiler_params=_cparams(1, VMEM_LIMIT),
        name="moe_combine_ln",
    )(x, route, y, y, g.reshape(1, d), b.reshape(1, d))


def _sc_row_gather(src, idx):
    n = idx.shape[0]
    d = src.shape[1]
    workers = SC_CORES * SC_SUBCORES
    per_worker = n // workers
    assert n % (workers * SC_GATHER_ROWS) == 0
    mesh = plsc.VectorSubcoreMesh(core_axis_name="c", subcore_axis_name="s")

    ch = SC_GATHER_ROWS
    n_chunks = per_worker // ch
    assert n_chunks % 2 == 0

    @functools.partial(
        pl.kernel, mesh=mesh, out_type=jax.ShapeDtypeStruct((n, d), src.dtype),
        scratch_types=[pltpu.VMEM((per_worker,), jnp.int32),
                       pltpu.VMEM((ch, d), src.dtype), pltpu.VMEM((ch, d), src.dtype),
                       pltpu.SemaphoreType.DMA, pltpu.SemaphoreType.DMA],
        name="sc_row_gather")
    def gather(src_hbm, idx_hbm, out_hbm, idx_v, rows_a, rows_b, sem_a, sem_b):
        base = (lax.axis_index("s") * SC_CORES + lax.axis_index("c")) * per_worker
        pltpu.sync_copy(idx_hbm.at[pl.ds(base, per_worker)], idx_v)

        def start(c, rows, sem):
            pltpu.async_copy(src_hbm.at[idx_v.at[pl.ds(c * ch, ch)]], rows, sem)

        def finish(c, rows, sem):
            pltpu.make_async_copy(src_hbm.at[pl.ds(0, ch)], rows, sem).wait()
            pltpu.sync_copy(rows, out_hbm.at[pl.ds(base + c * ch, ch)])

        start(0, rows_a, sem_a)

        @pl.loop(0, n_chunks, step=2)
        def _(c):
            start(c + 1, rows_b, sem_b)
            finish(c, rows_a, sem_a)

            @pl.when(c + 2 < n_chunks)
            def _():
                start(c + 2, rows_a, sem_a)

            finish(c + 1, rows_b, sem_b)

    return gather(src, idx)


def _moe_ln(x, x_packed, route, counts, wg, wu, wd, g, b, tm):
    pair_row, row_token, tile_expert, tile_live = _dispatch_plan(route, counts, tm)
    xs = _sc_row_gather(x_packed, row_token)
    ys = _moe_group_ffn(xs, tile_expert, tile_live, wg, wu, wd, tm, MOE_FF_TILE)
    slot_major = pair_row.T.reshape(-1)
    y = _sc_row_gather(ys, slot_major)
    return _moe_combine_ln(x, route, y, g, b, tm)


def kernel(x, w_in_even, w_out_even, forget_bias, w_in_odd, w_out_odd, lambda_q, lambda_k,
           subln_gain, rel_bias, w_router, b_router, w_gate, w_up, w_down, ln_gain, ln_bias):
    bsz, s, d = x.shape
    assert bsz == 1 and d == D_MODEL and s % MOBA_BLOCK == 0
    t = ATT_TILE
    nh = N_HEADS_GROUP
    gw = GROUP_WIDTH
    tm_proj = min(1024, s)
    tm_row = min(512, s)
    tk = min(ATT_KEYS, s)
    n_near = _near_tile_count(t)
    assert s % tk == 0 and tk // t + 1 >= n_near
    assert all(tk % (n * t) == 0 for n in (MOBA_Q_BLOCKS, DIFF_Q_BLOCKS)) and min(FOX_TK, s) % FOX_TQ == 0
    nblk = s // MOBA_BLOCK
    nbp = -(-nblk // LANES) * LANES

    x2 = x.reshape(s, d)
    bias_tiles = _bias_tiles(rel_bias, t, n_near)
    tab_spec = pl.BlockSpec(memory_space=pltpu.SMEM)
    bias_spec = pl.BlockSpec((None, n_near + 2, t, t), lambda hd, i: (hd, 0, 0, 0))
    block_onehot = (jnp.arange(s, dtype=jnp.int32)[:, None] // MOBA_BLOCK
                    == jnp.arange(nbp, dtype=jnp.int32)[None, :]).astype(BF16)

    for layer in range(DEPTH):
        li = layer // 2
        if layer % 2 == 0:
            w = w_in_even[li]
            fcol = 3 * gw
            w_main = jnp.concatenate([w[:, :fcol], w[:, fcol + nh:]], axis=1).astype(BF16)
            w_f = jnp.zeros((d, LANES), BF16).at[:, :nh].set(w[:, fcol:fcol + nh].astype(BF16))
            h = _in_proj(x2, w_main, BF16, tm_proj, PROJ_TN)
            f = _in_proj(x2, w_f, F32, tm_proj, LANES)
            c = _decay_cumsum(f[:, :nh].T, forget_bias[li])
            oa = _attention_call(
                functools.partial(_fox_kernel, t=FOX_TQ, tk=min(FOX_TK, s), scale=HEAD_DIM ** -0.5), h, gw, 0,
                ([c[:, min(FOX_TK, s) - 1::min(FOX_TK, s)], c[:, ::FOX_TQ]],
                 [c.reshape(nh, 1, s), c.reshape(nh, s, 1)]),
                ([tab_spec, tab_spec],
                 [pl.BlockSpec((None, 1, s), lambda hd, i: (hd, 0, 0)),
                  pl.BlockSpec((None, FOX_TQ, 1), lambda hd, i: (hd, i, 0))]),
                [pltpu.VMEM((SUBLANES, LANES), F32)],
                "fox_attention", FOX_TQ)
            ob = _attention_call(
                functools.partial(_moba_kernel, tb=t, nb=MOBA_Q_BLOCKS, tk=tk, nblk=nblk, n_near=n_near,
                                  scale=HEAD_DIM ** -0.5),
                h, gw, 3 * nh,
                ([rel_bias], [bias_tiles, block_onehot]),
                ([tab_spec], [bias_spec, pl.BlockSpec((s, nbp), lambda hd, i: (0, 0))]),
                [pltpu.VMEM((nbp, HEAD_DIM), F32), pltpu.VMEM((SUBLANES, LANES), F32)],
                "moba_attention", MOBA_Q_BLOCKS * t)
            w_out = w_out_even[li]
        else:
            lambda_init = 0.8 - 0.6 * math.exp(-0.3 * layer)
            h = _in_proj(x2, w_in_odd[li].astype(BF16), BF16, tm_proj, PROJ_TN)
            small = lambda shape: pl.BlockSpec(shape, lambda hd, i: (0, 0))
            oa = _attention_call(
                functools.partial(_diff_kernel, tb=t, nb=DIFF_Q_BLOCKS, tk=tk, n_near=n_near,
                                  scale=DIFF_QK_DIM ** -0.5, lambda_init=lambda_init),
                h, gw, 0,
                ([rel_bias], [bias_tiles, lambda_q[li], lambda_k[li], subln_gain[li].reshape(1, HEAD_DIM)]),
                ([tab_spec], [bias_spec, small((2, DIFF_QK_DIM)), small((2, DIFF_QK_DIM)),
                              small((1, HEAD_DIM))]),
                [pltpu.VMEM((SUBLANES, LANES), F32)],
                "diff_attention", DIFF_Q_BLOCKS * t)
            ob = _attention_call(
                functools.partial(_sb_kernel, t=t, nq=SB_Q_TILES, scale=HEAD_DIM ** -0.5), h, gw, 3 * nh,
                ([], []), ([], []), [], "stickbreak_attention", SB_Q_TILES * t)
            w_out = w_out_odd[li]
        x2, x2_packed = _out_proj_ln(oa, ob, w_out.astype(BF16), x2, ln_gain[layer, 0], ln_bias[layer, 0],
                                     tm_row)
        route, counts = _router(x2, w_router, b_router, tm_row)
        x2 = _moe_ln(x2, x2_packed, route, counts, w_gate[layer].astype(BF16), w_up[layer].astype(BF16),
                     w_down[layer].astype(BF16), ln_gain[layer, 1], ln_bias[layer, 1], tm_row)
    return x2.reshape(bsz, s, d)
```

```python
import functools
import math

import numpy as np
import jax
import jax.numpy as jnp
from jax import lax
from jax.experimental import pallas as pl
from jax.experimental.pallas import tpu as pltpu
from jax.experimental.pallas import tpu_sc as plsc

F32 = jnp.float32
BF16 = jnp.bfloat16

D_MODEL = 2048
DEPTH = 2
HEAD_DIM = 128
N_HEADS_GROUP = 8
GROUP_WIDTH = N_HEADS_GROUP * HEAD_DIM
DIFF_QK_DIM = HEAD_DIM // 2
MOBA_BLOCK = 256
MOBA_TOPK = 3
N_BUCKETS = 32
MAX_EXACT = N_BUCKETS // 2
MAX_DISTANCE = 1024
N_EXPERTS = 16
N_GROUPS = 4
EXPERTS_PER_GROUP = N_EXPERTS // N_GROUPS
MOE_TOP_K = 2
SC_CORES = 2
SC_SUBCORES = 16
SC_GATHER_ROWS = 32
SUBLANES = 8
D_EXPERT = D_MODEL // 2
PROJ_TN = 1024
MOE_FF_TILE = D_EXPERT
ALPHA = (2.0 * DEPTH) ** 0.25
LN_EPS = 1e-5
NEG_INF = -1e30
LOG2E = math.log2(math.e)
LANES = 128
ATT_TILE = 256
ATT_KEYS = 1024
FOX_TQ = 1024
FOX_TK = 1024
MOBA_Q_BLOCKS = 4
SB_Q_TILES = 2
DIFF_Q_BLOCKS = 4
SHIFT_SLACK = 96.0
FLUSH_EXPONENT = -130.0
SB_CUTOFF = -104.0
VMEM_LIMIT = 56 * 1024 * 1024


def _cparams(n_axes, vmem=None):
    return pltpu.CompilerParams(dimension_semantics=("arbitrary",) * n_axes,
                                vmem_limit_bytes=vmem)


def _nt_dot(a, b):
    return lax.dot_general(a, b, (((1,), (1,)), ((), ())), preferred_element_type=F32)


def _split_bf16(x):
    hi = x.astype(BF16)
    lo = (x - hi.astype(F32)).astype(BF16)
    return hi, lo


def _pack_bf16_pairs(x):
    n = x.shape[1] // 2
    lo = pltpu.bitcast(x[:, :n].astype(BF16).astype(F32), jnp.uint32)
    hi = pltpu.bitcast(x[:, n:].astype(BF16).astype(F32), jnp.uint32)
    return jnp.bitwise_or(lax.shift_right_logical(lo, jnp.uint32(16)),
                          jnp.bitwise_and(hi, jnp.uint32(0xFFFF0000)))


def _unpack_bf16_pairs(w):
    lo = pltpu.bitcast(lax.shift_left(w, jnp.uint32(16)), F32)
    hi = pltpu.bitcast(jnp.bitwise_and(w, jnp.uint32(0xFFFF0000)), F32)
    return lo, hi


def _layer_norm(z, g, b):
    mu = jnp.mean(z, axis=-1, keepdims=True)
    zc = z - mu
    var = jnp.mean(zc * zc, axis=-1, keepdims=True)
    return zc * lax.rsqrt(var + LN_EPS) * g + b


def _inproj_kernel(x_ref, w_ref, o_ref, xb_ref):
    @pl.when(pl.program_id(1) == 0)
    def _():
        xb_ref[...] = x_ref[...].astype(BF16)

    o_ref[...] = jnp.dot(xb_ref[...], w_ref[...], preferred_element_type=F32).astype(o_ref.dtype)


def _in_proj(x, w, out_dtype, tm, tn):
    s, d = x.shape
    n = w.shape[1]
    return pl.pallas_call(
        _inproj_kernel,
        out_shape=jax.ShapeDtypeStruct((s, n), out_dtype),
        grid=(s // tm, n // tn),
        in_specs=[pl.BlockSpec((tm, d), lambda i, j: (i, 0)),
                  pl.BlockSpec((d, tn), lambda i, j: (0, j))],
        out_specs=pl.BlockSpec((tm, tn), lambda i, j: (i, j)),
        scratch_shapes=[pltpu.VMEM((tm, d), BF16)],
        compiler_params=_cparams(2, VMEM_LIMIT),
        name="in_proj",
    )(x, w)


def _outproj_ln_kernel(oa_ref, ob_ref, w_ref, x_ref, g_ref, b_ref, o_ref, packed_ref):
    half = o_ref.shape[0] // 2
    for r in (slice(0, half), slice(half, 2 * half)):
        o = jnp.concatenate([oa_ref[r, :], ob_ref[r, :]], axis=1)
        y = jnp.dot(o, w_ref[...], preferred_element_type=F32)
        x1 = _layer_norm(ALPHA * x_ref[r, :] + y, g_ref[...], b_ref[...])
        o_ref[r, :] = x1
        packed_ref[r, :] = _pack_bf16_pairs(x1)


def _out_proj_ln(oa, ob, w, x, g, b, tm):
    s, d = x.shape
    gw = oa.shape[1]
    return pl.pallas_call(
        _outproj_ln_kernel,
        out_shape=(jax.ShapeDtypeStruct((s, d), F32), jax.ShapeDtypeStruct((s, d // 2), jnp.uint32)),
        grid=(s // tm,),
        in_specs=[pl.BlockSpec((tm, gw), lambda i: (i, 0)),
                  pl.BlockSpec((tm, gw), lambda i: (i, 0)),
                  pl.BlockSpec((2 * gw, d), lambda i: (0, 0)),
                  pl.BlockSpec((tm, d), lambda i: (i, 0)),
                  pl.BlockSpec((1, d), lambda i: (0, 0)),
                  pl.BlockSpec((1, d), lambda i: (0, 0))],
        out_specs=(pl.BlockSpec((tm, d), lambda i: (i, 0)),
                   pl.BlockSpec((tm, d // 2), lambda i: (i, 0))),
        compiler_params=_cparams(1, VMEM_LIMIT),
        name="out_proj_ln",
    )(oa, ob, w, x, g.reshape(1, d), b.reshape(1, d))


def _router_kernel(x_ref, w_ref, b_ref, route_ref, count_ref):
    route, counts = _route(x_ref[...], w_ref[...], b_ref[...])
    route_ref[...] = route
    count_ref[...] = jnp.broadcast_to(counts, count_ref.shape)


def _router(x, w_router, b_router, tm):
    s, d = x.shape
    wr = jnp.zeros((d, LANES), F32).at[:, :N_EXPERTS].set(w_router)
    br = jnp.zeros((1, LANES), F32).at[0, :N_EXPERTS].set(b_router)
    return pl.pallas_call(
        _router_kernel,
        out_shape=(jax.ShapeDtypeStruct((s, LANES), F32),
                   jax.ShapeDtypeStruct((s // tm, SUBLANES, LANES), F32)),
        grid=(s // tm,),
        in_specs=[pl.BlockSpec((tm, d), lambda i: (i, 0)),
                  pl.BlockSpec((d, LANES), lambda i: (0, 0)),
                  pl.BlockSpec((1, LANES), lambda i: (0, 0))],
        out_specs=(pl.BlockSpec((tm, LANES), lambda i: (i, 0)),
                   pl.BlockSpec((None, SUBLANES, LANES), lambda i: (i, 0, 0))),
        compiler_params=_cparams(1, VMEM_LIMIT),
        name="router",
    )(x, wr, br)


def _decay_cumsum_kernel(f_ref, b_ref, c_ref):
    nh, s = f_ref.shape
    rows = lax.broadcasted_iota(jnp.int32, (LANES, LANES), 0)
    cols = lax.broadcasted_iota(jnp.int32, (LANES, LANES), 1)
    upper = jnp.where(rows <= cols, 1.0, 0.0).astype(F32)
    bias = b_ref[...]

    def body(n, carry):
        off = pl.multiple_of(n * LANES, LANES)
        z = f_ref[:, pl.ds(off, LANES)] + bias
        logf = jnp.minimum(z, 0.0) - jnp.log1p(jnp.exp(-jnp.abs(z)))
        c = jnp.dot(logf, upper, preferred_element_type=F32,
                    precision=lax.Precision.HIGHEST) + carry
        c_ref[:, pl.ds(off, LANES)] = c * LOG2E
        return carry + jnp.sum(logf, axis=-1, keepdims=True)

    lax.fori_loop(0, s // LANES, body, jnp.zeros((nh, 1), F32))


def _decay_cumsum(f_t, bias):
    nh, s = f_t.shape
    return pl.pallas_call(
        _decay_cumsum_kernel,
        out_shape=jax.ShapeDtypeStruct((nh, s), F32),
        name="decay_cumsum",
    )(f_t, bias.reshape(nh, 1))


def _t5_bucket_np(dist):
    n = np.maximum(dist, 0)
    nf = np.maximum(n, 1).astype(np.float32)
    ratio = np.log(nf / np.float32(MAX_EXACT)) / np.float32(math.log(MAX_DISTANCE / MAX_EXACT))
    large = MAX_EXACT + (ratio.astype(np.float32) * np.float32(N_BUCKETS - MAX_EXACT)).astype(np.int32)
    large = np.minimum(large, N_BUCKETS - 1)
    return np.where(n < MAX_EXACT, n, large).astype(np.int32)


def _near_tile_count(t):
    d = np.arange(0, 4 * MAX_DISTANCE, dtype=np.int64)
    not_last = np.nonzero(_t5_bucket_np(d) != N_BUCKETS - 1)[0]
    d_sat = int(not_last.max()) + 1
    n = 1
    while (n - 1) * t + 1 < d_sat:
        n += 1
    return n


def _bucket_tiles_np(t, n_near):
    r = np.arange(t)[:, None]
    c = np.arange(t)[None, :]
    tiles = []
    for delta in range(n_near):
        dist = delta * t + r - c
        tiles.append(np.where(dist >= 0, _t5_bucket_np(dist), -1))
    tiles.append(np.full((t, t), N_BUCKETS - 1))
    tiles.append(np.full((t, t), -1))
    return np.stack(tiles).astype(np.int32)


def _bias_tiles_kernel(tab_ref, idx_ref, o_ref):
    h = pl.program_id(0)
    idx = idx_ref[...]
    out = jnp.full(idx.shape, NEG_INF, F32)
    for b in range(N_BUCKETS):
        out = jnp.where(idx == b, tab_ref[b, h] * LOG2E, out)
    o_ref[...] = out


def _bias_tiles(rel_bias, t, n_near):
    nh = rel_bias.shape[1]
    idx = jnp.asarray(_bucket_tiles_np(t, n_near))
    n_tiles = idx.shape[0]
    return pl.pallas_call(
        _bias_tiles_kernel,
        out_shape=jax.ShapeDtypeStruct((nh, n_tiles, t, t), F32),
        grid=(nh, n_tiles),
        in_specs=[pl.BlockSpec(memory_space=pltpu.SMEM),
                  pl.BlockSpec((None, t, t), lambda h, n: (n, 0, 0))],
        out_specs=pl.BlockSpec((None, None, t, t), lambda h, n: (h, n, 0, 0)),
        compiler_params=_cparams(2),
        name="bias_tiles",
    )(rel_bias, idx)


def _online_softmax_step(s, v, carry):
    m, l, acc = carry
    m_new = jnp.maximum(m, jnp.max(s, axis=-1, keepdims=True))
    alpha = jnp.exp2(m - m_new)
    p = jnp.exp2(s - m_new)
    l = alpha * l + jnp.sum(p, axis=-1, keepdims=True)
    acc = alpha * acc + jnp.dot(p.astype(BF16), v, preferred_element_type=F32)
    return m_new, l, acc


def _prescale(q, c):
    return (q.astype(F32) * c).astype(BF16)


def _near_bias(bias_ref, blocks, j, tb, tk, n_near):
    nsub = tk // tb
    rows = []
    for i in blocks:
        tiles = []
        for b in range(nsub):
            delta = i - (j * nsub + b)
            tiles.append(bias_ref[jnp.where(delta < 0, n_near + 1, jnp.minimum(delta, n_near))])
        rows.append(jnp.concatenate(tiles, axis=1))
    return rows[0] if len(rows) == 1 else jnp.concatenate(rows, axis=0)


def _biased_sweep(i, t, tk, scores, process, init):
    jd = (i * t) // tk
    carry = process(scores(jd), jd, init, True)
    lo = jnp.maximum(jd - 1, 0)
    carry = lax.fori_loop(lo, jd, lambda j, c: process(scores(j), j, c, True), carry)
    return lax.fori_loop(0, lo, lambda j, c: process(scores(j), j, c, False), carry)


def _softmax_init(rows):
    return (jnp.full((rows, 1), NEG_INF, F32), jnp.zeros((rows, 1), F32),
            jnp.zeros((rows, HEAD_DIM), F32))


def _fox_kernel(cend_ref, cstart_ref, q_ref, k_ref, v_ref, c_ref, ccol_ref, o_ref, knorm_ref,
                *, t, tk, scale):
    h = pl.program_id(0)
    i = pl.program_id(1)

    @pl.when(i == 0)
    def _():
        knorm_ref[...] = jnp.broadcast_to(_max_key_norm(k_ref, tk, 0, HEAD_DIM), knorm_ref.shape)

    q = _prescale(q_ref[...], scale * LOG2E)
    rows = lax.broadcasted_iota(jnp.int32, (t, tk), 0)
    cols = lax.broadcasted_iota(jnp.int32, (t, tk), 1)

    def scores(j):
        off = pl.multiple_of(j * tk, tk)
        return _nt_dot(q, k_ref[pl.ds(off, tk), :]) - c_ref[:, pl.ds(off, tk)]

    def values(j):
        return v_ref[pl.ds(pl.multiple_of(j * tk, tk), tk), :]

    def process(s, j, carry):
        return _online_softmax_step(s, values(j), carry)

    jd = (i * t) // tk

    def diag_scores():
        return jnp.where(cols <= rows + (i * t - jd * tk), scores(jd), NEG_INF)

    qf = q.astype(F32)
    qnorm = jnp.sqrt(jnp.sum(qf * qf, axis=-1, keepdims=True))
    bound = qnorm * knorm_ref[0:1, 0:1] * (1.0 + 2.0 ** -9) + 2.0 ** -6 - ccol_ref[...]
    row_floor = _diagonal_logit(qf, k_ref, pl.multiple_of(i * t, t), t) - ccol_ref[...]
    slack = jnp.max(bound - row_floor)

    def shifted_sweep(_):
        def step(s, j, carry):
            l, acc = carry
            p = jnp.exp2(s - bound)
            return (l + jnp.sum(p, axis=-1, keepdims=True),
                    acc + jnp.dot(p.astype(BF16), values(j), preferred_element_type=F32))

        c_first_row = cstart_ref[h, i]
        first_live = jnp.int32(0)
        for j in range(cend_ref.shape[1]):
            dead = jnp.logical_and(j < jd, c_first_row - cend_ref[h, j] < FLUSH_EXPONENT)
            first_live = first_live + dead.astype(jnp.int32)
        carry = step(diag_scores(), jd, _softmax_init(t)[1:])
        return _paired_loop(jd - first_live, lambda n, c: step(scores(first_live + n), first_live + n, c),
                            carry)

    def running_max_sweep(_):
        carry = _online_softmax_step(diag_scores(), values(jd), _softmax_init(t))
        return lax.fori_loop(0, jd, lambda j, c: process(scores(j), j, c), carry)[1:]

    l, acc = lax.cond(slack <= SHIFT_SLACK, shifted_sweep, running_max_sweep, 0)
    o_ref[...] = (acc / l).astype(o_ref.dtype)


def _moba_kernel(tab_ref, q_ref, k_ref, v_ref, bias_ref, et_ref, o_ref, kmean_ref, knorm_ref,
                 *, tb, nb, tk, nblk, n_near, scale):
    h = pl.program_id(0)
    g = pl.program_id(1)
    t = nb * tb
    nbp = kmean_ref.shape[0]

    @pl.when(g == 0)
    def _():
        kmean_ref[...] = jnp.zeros(kmean_ref.shape, F32)

        def mean_body(n, _):
            off = pl.multiple_of(n * tb, tb)
            kb = k_ref[pl.ds(off, tb), :].astype(F32)
            kmean_ref[pl.ds(n, 1), :] = jnp.sum(kb, axis=0, keepdims=True) * (1.0 / tb)
            return 0

        lax.fori_loop(0, nblk, mean_body, 0)
        knorm_ref[...] = jnp.broadcast_to(_max_key_norm(k_ref, tk, 0, HEAD_DIM), knorm_ref.shape)

    q = q_ref[...]
    km_hi, km_lo = _split_bf16(kmean_ref[...])
    gate = _nt_dot(q, km_hi) + _nt_dot(q, km_lo)
    blk = lax.broadcasted_iota(jnp.int32, (t, nbp), 1)
    own = g * nb + lax.broadcasted_iota(jnp.int32, (t, 1), 0) // tb
    eligible = blk < own
    sel = jnp.zeros((t, nbp), F32)
    for r in range(MOBA_TOPK):
        cand = jnp.where(eligible, jnp.where(sel > 0.0, -jnp.inf, gate), -jnp.inf)
        vmax = jnp.max(cand, axis=-1, keepdims=True)
        first = jnp.min(jnp.where(cand == vmax, blk, nbp), axis=-1, keepdims=True)
        take = jnp.where(r < own, 1.0, 0.0)
        sel = jnp.maximum(sel, jnp.where(blk == first, take, 0.0))

    sel = jnp.maximum(sel, jnp.where(blk == own, 1.0, 0.0))
    penalty = jnp.where(sel > 0.0, 0.0, NEG_INF).astype(BF16)
    q_aug = jnp.concatenate([_prescale(q, scale * LOG2E), penalty], axis=1)
    b_far = tab_ref[N_BUCKETS - 1, h] * LOG2E

    def scores(j):
        off = pl.multiple_of(j * tk, tk)
        k_aug = jnp.concatenate([k_ref[pl.ds(off, tk), :], et_ref[pl.ds(off, tk), :]], axis=1)
        return _nt_dot(q_aug, k_aug)

    def values(j):
        return v_ref[pl.ds(pl.multiple_of(j * tk, tk), tk), :]

    def near_bias(j):
        return _near_bias(bias_ref, [g * nb + b for b in range(nb)], j, tb, tk, n_near)

    def process(s, j, carry, near):
        return _online_softmax_step(s + (near_bias(j) if near else b_far), values(j), carry)

    qf = q_aug[:, :HEAD_DIM].astype(F32)
    qnorm = jnp.sqrt(jnp.sum(qf * qf, axis=-1, keepdims=True))
    bound = _logit_bound(qnorm, knorm_ref[0:1, 0:1], tab_ref, h)
    row_floor = _diagonal_logit(qf, k_ref, pl.multiple_of(g * t, t), t) + tab_ref[0, h] * LOG2E
    l, acc = _shifted_or_running_sweep(g, t, tk, scores, near_bias, values, process, bound, row_floor,
                                       b_far, t)
    o_ref[...] = (acc / l).astype(o_ref.dtype)


def _paired_loop(n, body, carry):
    carry = lax.fori_loop(0, n // 2, lambda p, c: body(2 * p + 1, body(2 * p, c)), carry)
    return lax.fori_loop(0, n % 2, lambda _, c: body(n - 1, c), carry)


def _logit_bound(qnorm, knorm, tab_ref, h):
    b_max = tab_ref[0, h]
    for b in range(1, N_BUCKETS):
        b_max = jnp.maximum(b_max, tab_ref[b, h])
    return qnorm * knorm * (1.0 + 2.0 ** -9) + (b_max * LOG2E + 2.0 ** -6)


def _shifted_or_running_sweep(i, t, tk, scores, near_bias, values, process, bound, row_floor, b_far, rows):
    jd = (i * t) // tk
    slack = jnp.max(bound - row_floor)

    def shifted_sweep(_):
        def step(s, j, carry, shift):
            l, acc = carry
            p = jnp.exp2(s - shift)
            return (l + jnp.sum(p, axis=-1, keepdims=True),
                    acc + jnp.dot(p.astype(BF16), values(j), preferred_element_type=F32))

        carry = step(scores(jd) + near_bias(jd), jd, _softmax_init(rows)[1:], bound)
        lo = jnp.maximum(jd - 1, 0)
        carry = lax.fori_loop(lo, jd, lambda j, c: step(scores(j) + near_bias(j), j, c, bound), carry)
        far_shift = bound - b_far
        return _paired_loop(lo, lambda j, c: step(scores(j), j, c, far_shift), carry)

    def running_max_sweep(_):
        return _biased_sweep(i, t, tk, scores, process, _softmax_init(rows))[1:]

    return lax.cond(slack <= SHIFT_SLACK, shifted_sweep, running_max_sweep, 0)


def _diagonal_logit(qf, k_ref, row0, t):
    kf = k_ref[pl.ds(row0, t), :].astype(F32)
    if qf.shape[0] != t:
        kf = jnp.concatenate([kf] * (qf.shape[0] // t), axis=0)
    return jnp.sum(qf * kf, axis=-1, keepdims=True)


def _max_key_norm(k_ref, tk, lane_lo, lane_hi):
    klane = lax.broadcasted_iota(jnp.int32, (tk, HEAD_DIM), 1)
    keep = jnp.logical_and(klane >= lane_lo, klane < lane_hi)

    def body(n, c):
        kt = k_ref[pl.ds(pl.multiple_of(n * tk, tk), tk), :].astype(F32)
        sq = jnp.sum(jnp.where(keep, kt * kt, 0.0), axis=-1, keepdims=True)
        return jnp.maximum(c, jnp.max(sq, axis=0, keepdims=True))

    return jnp.sqrt(lax.fori_loop(0, k_ref.shape[0] // tk, body, jnp.zeros((1, 1), F32)))


def _diff_kernel(tab_ref, q_ref, k_ref, v_ref, bias_ref, lq_ref, lk_ref, g_ref, o_ref, knorm_ref,
                 *, tb, nb, tk, n_near, scale, lambda_init):
    h = pl.program_id(0)
    i = pl.program_id(1)
    t = nb * tb
    q = _prescale(q_ref[...], scale * LOG2E)
    lane = lax.broadcasted_iota(jnp.int32, (t, HEAD_DIM), 1)
    zero = jnp.zeros_like(q)
    q2 = jnp.concatenate([jnp.where(lane < DIFF_QK_DIM, q, zero),
                          jnp.where(lane >= DIFF_QK_DIM, q, zero)], axis=0)
    b_far = tab_ref[N_BUCKETS - 1, h] * LOG2E

    @pl.when(i == 0)
    def _():
        knorm_ref[0:1, :] = jnp.broadcast_to(_max_key_norm(k_ref, tk, 0, DIFF_QK_DIM), (1, LANES))
        knorm_ref[1:2, :] = jnp.broadcast_to(_max_key_norm(k_ref, tk, DIFF_QK_DIM, HEAD_DIM), (1, LANES))

    def scores(j):
        return _nt_dot(q2, k_ref[pl.ds(pl.multiple_of(j * tk, tk), tk), :])

    def values(j):
        return v_ref[pl.ds(pl.multiple_of(j * tk, tk), tk), :]

    def near_bias(j):
        b = _near_bias(bias_ref, [i * nb + b for b in range(nb)], j, tb, tk, n_near)
        return jnp.concatenate([b, b], axis=0)

    def process(s, j, carry, near):
        return _online_softmax_step(s + (near_bias(j) if near else b_far), values(j), carry)

    q2f = q2.astype(F32)
    qnorm = jnp.sqrt(jnp.sum(q2f * q2f, axis=-1, keepdims=True))
    knorm = jnp.concatenate([jnp.broadcast_to(knorm_ref[0:1, 0:1], (t, 1)),
                             jnp.broadcast_to(knorm_ref[1:2, 0:1], (t, 1))], axis=0)
    bound = _logit_bound(qnorm, knorm, tab_ref, h)
    row_floor = _diagonal_logit(q2f, k_ref, pl.multiple_of(i * t, t), t) + tab_ref[0, h] * LOG2E
    l, acc = _shifted_or_running_sweep(i, t, tk, scores, near_bias, values, process, bound, row_floor,
                                       b_far, 2 * t)
    o12 = acc / l
    lam_e = jnp.exp(jnp.sum(lq_ref[...] * lk_ref[...], axis=-1, keepdims=True))
    lam = lam_e[0:1, :] - lam_e[1:2, :] + lambda_init
    o = o12[:t, :] - lam * o12[t:, :]
    o = o * lax.rsqrt(jnp.mean(o * o, axis=-1, keepdims=True) + LN_EPS) * g_ref[...]
    o_ref[...] = (o * (1.0 - lambda_init)).astype(o_ref.dtype)


def _sb_kernel(q_ref, k_ref, v_ref, o_ref, *, t, nq, scale):
    g = pl.program_id(1)

    def later_matrix(n):
        r = lax.broadcasted_iota(jnp.int32, (n, n), 0)
        c = lax.broadcasted_iota(jnp.int32, (n, n), 1)
        return jnp.where(r > c, 1.0, 0.0).astype(BF16)

    def block(q, off, nk, rem, acc, shift):
        k = k_ref[pl.ds(off, nk), :]
        v = v_ref[pl.ds(off, nk), :]
        z = _nt_dot(q, k) * scale
        soft = jnp.log1p(jnp.exp(-jnp.abs(z)))
        log_sig = jnp.minimum(z, 0.0) - soft
        log_rem = jnp.minimum(-z, 0.0) - soft
        if shift is not None:
            rows = lax.broadcasted_iota(jnp.int32, (t, nk), 0)
            cols = lax.broadcasted_iota(jnp.int32, (t, nk), 1)
            mask = cols < rows + shift
            log_rem = jnp.where(mask, log_rem, 0.0)
        hi, lo = _split_bf16(log_rem)
        later = later_matrix(nk)
        after = (jnp.dot(hi, later, preferred_element_type=F32)
                 + jnp.dot(lo, later, preferred_element_type=F32))
        a = jnp.exp(log_sig + after + rem)
        if shift is not None:
            a = jnp.where(mask, a, 0.0)
        acc = acc + jnp.dot(a.astype(BF16), v, preferred_element_type=F32)
        rem = rem + jnp.sum(log_rem, axis=-1, keepdims=True)
        return rem, acc

    def live(rem):
        return (jnp.max(rem) > SB_CUTOFF).astype(jnp.int32)

    state = []
    for u in range(nq):
        i = g * nq + u
        q = q_ref[u * t:(u + 1) * t, :]
        first = jnp.maximum(i - 1, 0)
        rem, acc = block(q, pl.multiple_of(first * t, t), 2 * t, jnp.zeros((t, 1), F32),
                         jnp.zeros((t, HEAD_DIM), F32), (i - first) * t)
        state.append((i, q, rem, acc))

    for u, (i, q, rem, acc) in enumerate(state):
        def cond(c):
            j, go, _, _ = c
            return jnp.logical_and(j >= 0, go > 0)

        def body(c, q=q):
            j, _, rem, acc = c
            rem, acc = block(q, pl.multiple_of(j * t, t), t, rem, acc, None)
            return j - 1, live(rem), rem, acc

        _, _, _, acc = lax.while_loop(cond, body, (i - 2, live(rem), rem, acc))
        o_ref[u * t:(u + 1) * t, :] = acc.astype(o_ref.dtype)


def _attention_call(kernel, h, out_cols, col0, extra_in, extra_specs, scratch, name, t):
    s = h.shape[0]
    nh = N_HEADS_GROUP
    in_specs = list(extra_specs[0]) + [
        pl.BlockSpec((t, HEAD_DIM), lambda hd, i: (i, col0 + hd)),
        pl.BlockSpec((s, HEAD_DIM), lambda hd, i: (0, col0 + nh + hd)),
        pl.BlockSpec((s, HEAD_DIM), lambda hd, i: (0, col0 + 2 * nh + hd)),
    ] + list(extra_specs[1])
    args = list(extra_in[0]) + [h, h, h] + list(extra_in[1])
    return pl.pallas_call(
        kernel,
        out_shape=jax.ShapeDtypeStruct((s, out_cols), BF16),
        grid=(nh, s // t),
        in_specs=in_specs,
        out_specs=pl.BlockSpec((t, HEAD_DIM), lambda hd, i: (i, hd)),
        scratch_shapes=scratch,
        compiler_params=_cparams(2, VMEM_LIMIT),
        name=name,
    )(*args)


def _route(x, w, b):
    x_hi, x_lo = _split_bf16(x)
    w_hi, w_lo = _split_bf16(w)
    logits = (jnp.dot(x_hi, w_hi, preferred_element_type=F32) + jnp.dot(x_lo, w_hi, preferred_element_type=F32)
              + jnp.dot(x_hi, w_lo, preferred_element_type=F32)) + b
    tm, n = logits.shape
    lane = lax.broadcasted_iota(jnp.int32, (tm, n), 1)
    real = lane < N_EXPERTS
    logits = jnp.where(real, logits, -jnp.inf)
    e = jnp.exp(logits - jnp.max(logits, axis=-1, keepdims=True))
    aff = e / jnp.sum(e, axis=-1, keepdims=True)
    group = lane // EXPERTS_PER_GROUP
    best = jnp.full((tm, 1), -jnp.inf, F32)
    g_sel = jnp.zeros((tm, 1), jnp.int32)
    for g in range(N_GROUPS):
        gmax = jnp.max(jnp.where(group == g, aff, -jnp.inf), axis=-1, keepdims=True)
        better = gmax > best
        g_sel = jnp.where(better, g, g_sel)
        best = jnp.where(better, gmax, best)
    in_group = jnp.logical_and(group == g_sel, real)
    cand = jnp.where(in_group, aff, -jnp.inf)
    v1 = jnp.max(cand, axis=-1, keepdims=True)
    i1 = jnp.min(jnp.where(cand == v1, lane, n), axis=-1, keepdims=True)
    cand2 = jnp.where(lane == i1, -jnp.inf, cand)
    v2 = jnp.max(cand2, axis=-1, keepdims=True)
    i2 = jnp.min(jnp.where(cand2 == v2, lane, n), axis=-1, keepdims=True)
    tot = v1 + v2
    chosen = jnp.where(jnp.logical_or(lane == i1, lane == i2), 1.0, 0.0)
    r = lax.broadcasted_iota(jnp.int32, (tm, tm), 0)
    c = lax.broadcasted_iota(jnp.int32, (tm, tm), 1)
    earlier = jnp.where(c < r, 1.0, 0.0).astype(BF16)
    before = jnp.dot(earlier, chosen.astype(BF16), preferred_element_type=F32)
    rank1 = jnp.sum(jnp.where(lane == i1, before, 0.0), axis=-1, keepdims=True)
    rank2 = jnp.sum(jnp.where(lane == i2, before, 0.0), axis=-1, keepdims=True)
    route = jnp.where(lane == 0, v1 / tot,
                      jnp.where(lane == 1, v2 / tot,
                                jnp.where(lane == 2, i1.astype(F32),
                                          jnp.where(lane == 3, i2.astype(F32),
                                                    jnp.where(lane == 4, rank1,
                                                              jnp.where(lane == 5, rank2, 0.0))))))
    return route, jnp.sum(chosen, axis=0, keepdims=True)


def _dispatch_plan(route, counts, tm):
    s = route.shape[0]
    nt = s // tm
    n_tiles = (MOE_TOP_K * s) // tm + N_EXPERTS
    ids = jnp.arange(N_EXPERTS, dtype=jnp.int32)
    expert = route[:, 2:2 + MOE_TOP_K].astype(jnp.int32).reshape(nt, tm, MOE_TOP_K)
    rank = route[:, 4:4 + MOE_TOP_K].astype(jnp.int32).reshape(nt, tm, MOE_TOP_K)
    cnt = counts[:, 0, :N_EXPERTS].astype(jnp.int32)
    tile_base = jnp.cumsum(cnt, axis=0) - cnt
    padded = -(-jnp.sum(cnt, axis=0) // tm) * tm
    seg_end = jnp.cumsum(padded)
    base = (seg_end - padded)[None, :] + tile_base
    pick = expert[..., None] == ids
    pair_row = (jnp.sum(jnp.where(pick, base[:, None, None, :], 0), axis=-1) + rank).reshape(s, MOE_TOP_K)
    tile_start = jnp.arange(n_tiles, dtype=jnp.int32) * tm
    tile_expert = jnp.minimum(jnp.sum((tile_start[:, None] >= seg_end[None, :]).astype(jnp.int32), axis=1),
                              N_EXPERTS - 1).astype(jnp.int32)
    tile_live = (tile_start < seg_end[-1]).astype(jnp.int32)
    row_token = (jnp.arange(n_tiles * tm, dtype=jnp.int32) % s).at[pair_row.reshape(-1)].set(
        jnp.arange(MOE_TOP_K * s, dtype=jnp.int32) // MOE_TOP_K)
    return pair_row, row_token, tile_expert, tile_live


def _moe_group_kernel(te_ref, live_ref, x_ref, wg_ref, wu_ref, wd_ref, o_ref, xb_ref, acc_ref):
    g = pl.program_id(0)
    f = pl.program_id(1)
    last = f == pl.num_programs(1) - 1
    live = live_ref[g] > 0

    @pl.when(jnp.logical_and(live, f == 0))
    def _():
        lo, hi = _unpack_bf16_pairs(x_ref[...])
        xb_ref[...] = jnp.concatenate([lo.astype(BF16), hi.astype(BF16)], axis=1)

    @pl.when(jnp.logical_and(jnp.logical_not(live), last))
    def _():
        o_ref[...] = jnp.zeros(o_ref.shape, o_ref.dtype)

    @pl.when(live)
    def _():
        xb = xb_ref[...]
        a = jnp.dot(xb, wg_ref[...].astype(BF16), preferred_element_type=F32)
        u = jnp.dot(xb, wu_ref[...].astype(BF16), preferred_element_type=F32)
        hid = ((a * jax.nn.sigmoid(a)) * u).astype(BF16)
        y = jnp.dot(hid, wd_ref[...].astype(BF16), preferred_element_type=F32)

        @pl.when(f == 0)
        def _():
            acc_ref[...] = y

        @pl.when(f > 0)
        def _():
            acc_ref[...] += y

        @pl.when(last)
        def _():
            o_ref[...] = _pack_bf16_pairs(acc_ref[...])


def _moe_group_ffn(xs, tile_expert, tile_live, wg, wu, wd, tm, tf):
    rows, half = xs.shape
    d = 2 * half
    fdim = wg.shape[2]
    grid_spec = pltpu.PrefetchScalarGridSpec(
        num_scalar_prefetch=2,
        grid=(rows // tm, fdim // tf),
        in_specs=[pl.BlockSpec((tm, half), lambda g, f, te, lv: (g, 0)),
                  pl.BlockSpec((None, d, tf), lambda g, f, te, lv: (te[g], 0, f), pipeline_mode=pl.Buffered(1)),
                  pl.BlockSpec((None, d, tf), lambda g, f, te, lv: (te[g], 0, f), pipeline_mode=pl.Buffered(1)),
                  pl.BlockSpec((None, tf, d), lambda g, f, te, lv: (te[g], f, 0), pipeline_mode=pl.Buffered(1))],
        out_specs=pl.BlockSpec((tm, half), lambda g, f, te, lv: (g, 0)),
        scratch_shapes=[pltpu.VMEM((tm, d), BF16), pltpu.VMEM((tm, d), F32)])
    return pl.pallas_call(
        _moe_group_kernel,
        out_shape=jax.ShapeDtypeStruct((rows, half), jnp.uint32),
        grid_spec=grid_spec,
        compiler_params=_cparams(2, VMEM_LIMIT),
        name="moe_group_ffn",
    )(tile_expert, tile_live, xs, wg, wu, wd)


def _moe_combine_ln_kernel(x_ref, r_ref, y0_ref, y1_ref, g_ref, b_ref, o_ref):
    r = r_ref[...]
    lo0, hi0 = _unpack_bf16_pairs(y0_ref[...])
    lo1, hi1 = _unpack_bf16_pairs(y1_ref[...])
    ff = jnp.concatenate([r[:, 0:1] * lo0 + r[:, 1:2] * lo1, r[:, 0:1] * hi0 + r[:, 1:2] * hi1], axis=1)
    o_ref[...] = _layer_norm(ALPHA * x_ref[...] + ff, g_ref[...], b_ref[...])


def _moe_combine_ln(x, route, y, g, b, tm):
    s, d = x.shape
    nt = s // tm
    return pl.pallas_call(
        _moe_combine_ln_kernel,
        out_shape=jax.ShapeDtypeStruct((s, d), F32),
        grid=(nt,),
        in_specs=[pl.BlockSpec((tm, d), lambda i: (i, 0)),
                  pl.BlockSpec((tm, LANES), lambda i: (i, 0)),
                  pl.BlockSpec((tm, d // 2), lambda i: (i, 0)),
                  pl.BlockSpec((tm, d // 2), lambda i: (i + nt, 0)),
                  pl.BlockSpec((1, d), lambda i: (0, 0)),
                  pl.BlockSpec((1, d), lambda i: (0, 0))],
        out_specs=pl.BlockSpec((tm, d), lambda i: (i, 0)),
        compiler_params=_cparams(1, VMEM_LIMIT),
        name="moe_combine_ln",
    )(x, route, y, y, g.reshape(1, d), b.reshape(1, d))


def _sc_row_gather(src, idx):
    n = idx.shape[0]
    d = src.shape[1]
    workers = SC_CORES * SC_SUBCORES
    per_worker = n // workers
    assert n % (workers * SC_GATHER_ROWS) == 0
    mesh = plsc.VectorSubcoreMesh(core_axis_name="c", subcore_axis_name="s")

    ch = SC_GATHER_ROWS
    n_chunks = per_worker // ch
    assert n_chunks % 2 == 0

    @functools.partial(
        pl.kernel, mesh=mesh, out_type=jax.ShapeDtypeStruct((n, d), src.dtype),
        scratch_types=[pltpu.VMEM((per_worker,), jnp.int32),
                       pltpu.VMEM((ch, d), src.dtype), pltpu.VMEM((ch, d), src.dtype),
                       pltpu.SemaphoreType.DMA, pltpu.SemaphoreType.DMA],
        name="sc_row_gather")
    def gather(src_hbm, idx_hbm, out_hbm, idx_v, rows_a, rows_b, sem_a, sem_b):
        base = (lax.axis_index("s") * SC_CORES + lax.axis_index("c")) * per_worker
        pltpu.sync_copy(idx_hbm.at[pl.ds(base, per_worker)], idx_v)

        def start(c, rows, sem):
            pltpu.async_copy(src_hbm.at[idx_v.at[pl.ds(c * ch, ch)]], rows, sem)

        def finish(c, rows, sem):
            pltpu.make_async_copy(src_hbm.at[pl.ds(0, ch)], rows, sem).wait()
            pltpu.sync_copy(rows, out_hbm.at[pl.ds(base + c * ch, ch)])

        start(0, rows_a, sem_a)

        @pl.loop(0, n_chunks, step=2)
        def _(c):
            start(c + 1, rows_b, sem_b)
            finish(c, rows_a, sem_a)

            @pl.when(c + 2 < n_chunks)
            def _():
                start(c + 2, rows_a, sem_a)

            finish(c + 1, rows_b, sem_b)

    return gather(src, idx)


def _moe_ln(x, x_packed, route, counts, wg, wu, wd, g, b, tm):
    pair_row, row_token, tile_expert, tile_live = _dispatch_plan(route, counts, tm)
    xs = _sc_row_gather(x_packed, row_token)
    ys = _moe_group_ffn(xs, tile_expert, tile_live, wg, wu, wd, tm, MOE_FF_TILE)
    slot_major = pair_row.T.reshape(-1)
    y = _sc_row_gather(ys, slot_major)
    return _moe_combine_ln(x, route, y, g, b, tm)


def kernel(x, w_in_even, w_out_even, forget_bias, w_in_odd, w_out_odd, lambda_q, lambda_k,
           subln_gain, rel_bias, w_router, b_router, w_gate, w_up, w_down, ln_gain, ln_bias):
    bsz, s, d = x.shape
    assert bsz == 1 and d == D_MODEL and s % MOBA_BLOCK == 0
    t = ATT_TILE
    nh = N_HEADS_GROUP
    gw = GROUP_WIDTH
    tm_proj = min(1024, s)
    tm_row = min(512, s)
    tk = min(ATT_KEYS, s)
    n_near = _near_tile_count(t)
    assert s % tk == 0 and tk // t + 1 >= n_near
    assert all(tk % (n * t) == 0 for n in (MOBA_Q_BLOCKS, DIFF_Q_BLOCKS)) and min(FOX_TK, s) % FOX_TQ == 0
    nblk = s // MOBA_BLOCK
    nbp = -(-nblk // LANES) * LANES

    x2 = x.reshape(s, d)
    bias_tiles = _bias_tiles(rel_bias, t, n_near)
    tab_spec = pl.BlockSpec(memory_space=pltpu.SMEM)
    bias_spec = pl.BlockSpec((None, n_near + 2, t, t), lambda hd, i: (hd, 0, 0, 0))
    block_onehot = (jnp.arange(s, dtype=jnp.int32)[:, None] // MOBA_BLOCK
                    == jnp.arange(nbp, dtype=jnp.int32)[None, :]).astype(BF16)

    for layer in range(DEPTH):
        li = layer // 2
        if layer % 2 == 0:
            w = w_in_even[li]
            fcol = 3 * gw
            w_main = jnp.concatenate([w[:, :fcol], w[:, fcol + nh:]], axis=1).astype(BF16)
            w_f = jnp.zeros((d, LANES), BF16).at[:, :nh].set(w[:, fcol:fcol + nh].astype(BF16))
            h = _in_proj(x2, w_main, BF16, tm_proj, PROJ_TN)
            f = _in_proj(x2, w_f, F32, tm_proj, LANES)
            c = _decay_cumsum(f[:, :nh].T, forget_bias[li])
            oa = _attention_call(
                functools.partial(_fox_kernel, t=FOX_TQ, tk=min(FOX_TK, s), scale=HEAD_DIM ** -0.5), h, gw, 0,
                ([c[:, min(FOX_TK, s) - 1::min(FOX_TK, s)], c[:, ::FOX_TQ]],
                 [c.reshape(nh, 1, s), c.reshape(nh, s, 1)]),
                ([tab_spec, tab_spec],
                 [pl.BlockSpec((None, 1, s), lambda hd, i: (hd, 0, 0)),
                  pl.BlockSpec((None, FOX_TQ, 1), lambda hd, i: (hd, i, 0))]),
                [pltpu.VMEM((SUBLANES, LANES), F32)],
                "fox_attention", FOX_TQ)
            ob = _attention_call(
                functools.partial(_moba_kernel, tb=t, nb=MOBA_Q_BLOCKS, tk=tk, nblk=nblk, n_near=n_near,
                                  scale=HEAD_DIM ** -0.5),
                h, gw, 3 * nh,
                ([rel_bias], [bias_tiles, block_onehot]),
                ([tab_spec], [bias_spec, pl.BlockSpec((s, nbp), lambda hd, i: (0, 0))]),
                [pltpu.VMEM((nbp, HEAD_DIM), F32), pltpu.VMEM((SUBLANES, LANES), F32)],
                "moba_attention", MOBA_Q_BLOCKS * t)
            w_out = w_out_even[li]
        else:
            lambda_init = 0.8 - 0.6 * math.exp(-0.3 * layer)
            h = _in_proj(x2, w_in_odd[li].astype(BF16), BF16, tm_proj, PROJ_TN)
            small = lambda shape: pl.BlockSpec(shape, lambda hd, i: (0, 0))
            oa = _attention_call(
                functools.partial(_diff_kernel, tb=t, nb=DIFF_Q_BLOCKS, tk=tk, n_near=n_near,
                                  scale=DIFF_QK_DIM ** -0.5, lambda_init=lambda_init),
                h, gw, 0,
                ([rel_bias], [bias_tiles, lambda_q[li], lambda_k[li], subln_gain[li].reshape(1, HEAD_DIM)]),
                ([tab_spec], [bias_spec, small((2, DIFF_QK_DIM)), small((2, DIFF_QK_DIM)),
                              small((1, HEAD_DIM))]),
                [pltpu.VMEM((SUBLANES, LANES), F32)],
                "diff_attention", DIFF_Q_BLOCKS * t)
            ob = _attention_call(
                functools.partial(_sb_kernel, t=t, nq=SB_Q_TILES, scale=HEAD_DIM ** -0.5), h, gw, 3 * nh,
                ([], []), ([], []), [], "stickbreak_attention", SB_Q_TILES * t)
            w_out = w_out_odd[li]
        x2, x2_packed = _out_proj_ln(oa, ob, w_out.astype(BF16), x2, ln_gain[layer, 0], ln_bias[layer, 0],
                                     tm_row)
        route, counts = _router(x2, w_router, b_router, tm_row)
        x2 = _moe_ln(x2, x2_packed, route, counts, w_gate[layer], w_up[layer], w_down[layer],
                     ln_gain[layer, 1], ln_bias[layer, 1], tm_row)
    return x2.reshape(bsz, s, d)
```

```python
import functools
import math

import numpy as np
import jax
import jax.numpy as jnp
from jax import lax
from jax.experimental import pallas as pl
from jax.experimental.pallas import tpu as pltpu
from jax.experimental.pallas import tpu_sc as plsc

F32 = jnp.float32
BF16 = jnp.bfloat16

D_MODEL = 2048
DEPTH = 2
HEAD_DIM = 128
N_HEADS_GROUP = 8
GROUP_WIDTH = N_HEADS_GROUP * HEAD_DIM
DIFF_QK_DIM = HEAD_DIM // 2
MOBA_BLOCK = 256
MOBA_TOPK = 3
N_BUCKETS = 32
MAX_EXACT = N_BUCKETS // 2
MAX_DISTANCE = 1024
N_EXPERTS = 16
N_GROUPS = 4
EXPERTS_PER_GROUP = N_EXPERTS // N_GROUPS
MOE_TOP_K = 2
SC_CORES = 2
SC_SUBCORES = 16
SC_GATHER_ROWS = 32
SUBLANES = 8
D_EXPERT = D_MODEL // 2
PROJ_TN = 1024
MOE_FF_TILE = D_EXPERT
ALPHA = (2.0 * DEPTH) ** 0.25
LN_EPS = 1e-5
NEG_INF = -1e30
LOG2E = math.log2(math.e)
LANES = 128
ATT_TILE = 256
ATT_KEYS = 1024
FOX_TQ = 1024
FOX_TK = 1024
MOBA_Q_BLOCKS = 4
SB_Q_TILES = 2
DIFF_Q_BLOCKS = 4
SHIFT_SLACK = 96.0
FLUSH_EXPONENT = -130.0
SB_CUTOFF = -104.0
VMEM_LIMIT = 56 * 1024 * 1024


def _cparams(n_axes, vmem=None):
    return pltpu.CompilerParams(dimension_semantics=("arbitrary",) * n_axes,
                                vmem_limit_bytes=vmem)


def _nt_dot(a, b):
    return lax.dot_general(a, b, (((1,), (1,)), ((), ())), preferred_element_type=F32)


def _split_bf16(x):
    hi = x.astype(BF16)
    lo = (x - hi.astype(F32)).astype(BF16)
    return hi, lo


def _pack_bf16_pairs(x):
    n = x.shape[1] // 2
    lo = pltpu.bitcast(x[:, :n].astype(BF16).astype(F32), jnp.uint32)
    hi = pltpu.bitcast(x[:, n:].astype(BF16).astype(F32), jnp.uint32)
    return jnp.bitwise_or(lax.shift_right_logical(lo, jnp.uint32(16)),
                          jnp.bitwise_and(hi, jnp.uint32(0xFFFF0000)))


def _unpack_bf16_pairs(w):
    lo = pltpu.bitcast(lax.shift_left(w, jnp.uint32(16)), F32)
    hi = pltpu.bitcast(jnp.bitwise_and(w, jnp.uint32(0xFFFF0000)), F32)
    return lo, hi


def _layer_norm(z, g, b):
    mu = jnp.mean(z, axis=-1, keepdims=True)
    zc = z - mu
    var = jnp.mean(zc * zc, axis=-1, keepdims=True)
    return zc * lax.rsqrt(var + LN_EPS) * g + b


def _inproj_kernel(x_ref, w_ref, o_ref, xb_ref):
    @pl.when(pl.program_id(1) == 0)
    def _():
        xb_ref[...] = x_ref[...].astype(BF16)

    o_ref[...] = jnp.dot(xb_ref[...], w_ref[...], preferred_element_type=F32).astype(o_ref.dtype)


def _in_proj(x, w, out_dtype, tm, tn):
    s, d = x.shape
    n = w.shape[1]
    return pl.pallas_call(
        _inproj_kernel,
        out_shape=jax.ShapeDtypeStruct((s, n), out_dtype),
        grid=(s // tm, n // tn),
        in_specs=[pl.BlockSpec((tm, d), lambda i, j: (i, 0)),
                  pl.BlockSpec((d, tn), lambda i, j: (0, j))],
        out_specs=pl.BlockSpec((tm, tn), lambda i, j: (i, j)),
        scratch_shapes=[pltpu.VMEM((tm, d), BF16)],
        compiler_params=_cparams(2, VMEM_LIMIT),
        name="in_proj",
    )(x, w)


def _outproj_ln_kernel(oa_ref, ob_ref, w_ref, x_ref, g_ref, b_ref, o_ref, packed_ref):
    half = o_ref.shape[0] // 2
    for r in (slice(0, half), slice(half, 2 * half)):
        o = jnp.concatenate([oa_ref[r, :], ob_ref[r, :]], axis=1)
        y = jnp.dot(o, w_ref[...], preferred_element_type=F32)
        x1 = _layer_norm(ALPHA * x_ref[r, :] + y, g_ref[...], b_ref[...])
        o_ref[r, :] = x1
        packed_ref[r, :] = _pack_bf16_pairs(x1)


def _out_proj_ln(oa, ob, w, x, g, b, tm):
    s, d = x.shape
    gw = oa.shape[1]
    return pl.pallas_call(
        _outproj_ln_kernel,
        out_shape=(jax.ShapeDtypeStruct((s, d), F32), jax.ShapeDtypeStruct((s, d // 2), jnp.uint32)),
        grid=(s // tm,),
        in_specs=[pl.BlockSpec((tm, gw), lambda i: (i, 0)),
                  pl.BlockSpec((tm, gw), lambda i: (i, 0)),
                  pl.BlockSpec((2 * gw, d), lambda i: (0, 0)),
                  pl.BlockSpec((tm, d), lambda i: (i, 0)),
                  pl.BlockSpec((1, d), lambda i: (0, 0)),
                  pl.BlockSpec((1, d), lambda i: (0, 0))],
        out_specs=(pl.BlockSpec((tm, d), lambda i: (i, 0)),
                   pl.BlockSpec((tm, d // 2), lambda i: (i, 0))),
        compiler_params=_cparams(1, VMEM_LIMIT),
        name="out_proj_ln",
    )(oa, ob, w, x, g.reshape(1, d), b.reshape(1, d))


def _router_kernel(x_ref, w_ref, b_ref, route_ref, count_ref):
    route, counts = _route(x_ref[...], w_ref[...], b_ref[...])
    route_ref[...] = route
    count_ref[...] = jnp.broadcast_to(counts, count_ref.shape)


def _router(x, w_router, b_router, tm):
    s, d = x.shape
    wr = jnp.zeros((d, LANES), F32).at[:, :N_EXPERTS].set(w_router)
    br = jnp.zeros((1, LANES), F32).at[0, :N_EXPERTS].set(b_router)
    return pl.pallas_call(
        _router_kernel,
        out_shape=(jax.ShapeDtypeStruct((s, LANES), F32),
                   jax.ShapeDtypeStruct((s // tm, SUBLANES, LANES), F32)),
        grid=(s // tm,),
        in_specs=[pl.BlockSpec((tm, d), lambda i: (i, 0)),
                  pl.BlockSpec((d, LANES), lambda i: (0, 0)),
                  pl.BlockSpec((1, LANES), lambda i: (0, 0))],
        out_specs=(pl.BlockSpec((tm, LANES), lambda i: (i, 0)),
                   pl.BlockSpec((None, SUBLANES, LANES), lambda i: (i, 0, 0))),
        compiler_params=_cparams(1, VMEM_LIMIT),
        name="router",
    )(x, wr, br)


def _decay_cumsum_kernel(f_ref, b_ref, c_ref):
    nh, s = f_ref.shape
    rows = lax.broadcasted_iota(jnp.int32, (LANES, LANES), 0)
    cols = lax.broadcasted_iota(jnp.int32, (LANES, LANES), 1)
    upper = jnp.where(rows <= cols, 1.0, 0.0).astype(F32)
    bias = b_ref[...]

    def body(n, carry):
        off = pl.multiple_of(n * LANES, LANES)
        z = f_ref[:, pl.ds(off, LANES)] + bias
        logf = jnp.minimum(z, 0.0) - jnp.log1p(jnp.exp(-jnp.abs(z)))
        c = jnp.dot(logf, upper, preferred_element_type=F32,
                    precision=lax.Precision.HIGHEST) + carry
        c_ref[:, pl.ds(off, LANES)] = c * LOG2E
        return carry + jnp.sum(logf, axis=-1, keepdims=True)

    lax.fori_loop(0, s // LANES, body, jnp.zeros((nh, 1), F32))


def _decay_cumsum(f_t, bias):
    nh, s = f_t.shape
    return pl.pallas_call(
        _decay_cumsum_kernel,
        out_shape=jax.ShapeDtypeStruct((nh, s), F32),
        name="decay_cumsum",
    )(f_t, bias.reshape(nh, 1))


def _t5_bucket_np(dist):
    n = np.maximum(dist, 0)
    nf = np.maximum(n, 1).astype(np.float32)
    ratio = np.log(nf / np.float32(MAX_EXACT)) / np.float32(math.log(MAX_DISTANCE / MAX_EXACT))
    large = MAX_EXACT + (ratio.astype(np.float32) * np.float32(N_BUCKETS - MAX_EXACT)).astype(np.int32)
    large = np.minimum(large, N_BUCKETS - 1)
    return np.where(n < MAX_EXACT, n, large).astype(np.int32)


def _near_tile_count(t):
    d = np.arange(0, 4 * MAX_DISTANCE, dtype=np.int64)
    not_last = np.nonzero(_t5_bucket_np(d) != N_BUCKETS - 1)[0]
    d_sat = int(not_last.max()) + 1
    n = 1
    while (n - 1) * t + 1 < d_sat:
        n += 1
    return n


def _bucket_tiles_np(t, n_near):
    r = np.arange(t)[:, None]
    c = np.arange(t)[None, :]
    tiles = []
    for delta in range(n_near):
        dist = delta * t + r - c
        tiles.append(np.where(dist >= 0, _t5_bucket_np(dist), -1))
    tiles.append(np.full((t, t), N_BUCKETS - 1))
    tiles.append(np.full((t, t), -1))
    return np.stack(tiles).astype(np.int32)


def _bias_tiles_kernel(tab_ref, idx_ref, o_ref):
    h = pl.program_id(0)
    idx = idx_ref[...]
    out = jnp.full(idx.shape, NEG_INF, F32)
    for b in range(N_BUCKETS):
        out = jnp.where(idx == b, tab_ref[b, h] * LOG2E, out)
    o_ref[...] = out


def _bias_tiles(rel_bias, t, n_near):
    nh = rel_bias.shape[1]
    idx = jnp.asarray(_bucket_tiles_np(t, n_near))
    n_tiles = idx.shape[0]
    return pl.pallas_call(
        _bias_tiles_kernel,
        out_shape=jax.ShapeDtypeStruct((nh, n_tiles, t, t), F32),
        grid=(nh, n_tiles),
        in_specs=[pl.BlockSpec(memory_space=pltpu.SMEM),
                  pl.BlockSpec((None, t, t), lambda h, n: (n, 0, 0))],
        out_specs=pl.BlockSpec((None, None, t, t), lambda h, n: (h, n, 0, 0)),
        compiler_params=_cparams(2),
        name="bias_tiles",
    )(rel_bias, idx)


def _online_softmax_step(s, v, carry):
    m, l, acc = carry
    m_new = jnp.maximum(m, jnp.max(s, axis=-1, keepdims=True))
    alpha = jnp.exp2(m - m_new)
    p = jnp.exp2(s - m_new)
    l = alpha * l + jnp.sum(p, axis=-1, keepdims=True)
    acc = alpha * acc + jnp.dot(p.astype(BF16), v, preferred_element_type=F32)
    return m_new, l, acc


def _prescale(q, c):
    return (q.astype(F32) * c).astype(BF16)


def _near_bias(bias_ref, blocks, j, tb, tk, n_near):
    nsub = tk // tb
    rows = []
    for i in blocks:
        tiles = []
        for b in range(nsub):
            delta = i - (j * nsub + b)
            tiles.append(bias_ref[jnp.where(delta < 0, n_near + 1, jnp.minimum(delta, n_near))])
        rows.append(jnp.concatenate(tiles, axis=1))
    return rows[0] if len(rows) == 1 else jnp.concatenate(rows, axis=0)


def _biased_sweep(i, t, tk, scores, process, init):
    jd = (i * t) // tk
    carry = process(scores(jd), jd, init, True)
    lo = jnp.maximum(jd - 1, 0)
    carry = lax.fori_loop(lo, jd, lambda j, c: process(scores(j), j, c, True), carry)
    return lax.fori_loop(0, lo, lambda j, c: process(scores(j), j, c, False), carry)


def _softmax_init(rows):
    return (jnp.full((rows, 1), NEG_INF, F32), jnp.zeros((rows, 1), F32),
            jnp.zeros((rows, HEAD_DIM), F32))


def _fox_kernel(cend_ref, cstart_ref, q_ref, k_ref, v_ref, c_ref, ccol_ref, o_ref, knorm_ref,
                *, t, tk, scale):
    h = pl.program_id(0)
    i = pl.program_id(1)

    @pl.when(i == 0)
    def _():
        knorm_ref[...] = jnp.broadcast_to(_max_key_norm(k_ref, tk, 0, HEAD_DIM), knorm_ref.shape)

    q = _prescale(q_ref[...], scale * LOG2E)
    rows = lax.broadcasted_iota(jnp.int32, (t, tk), 0)
    cols = lax.broadcasted_iota(jnp.int32, (t, tk), 1)

    def scores(j):
        off = pl.multiple_of(j * tk, tk)
        return _nt_dot(q, k_ref[pl.ds(off, tk), :]) - c_ref[:, pl.ds(off, tk)]

    def values(j):
        return v_ref[pl.ds(pl.multiple_of(j * tk, tk), tk), :]

    def process(s, j, carry):
        return _online_softmax_step(s, values(j), carry)

    jd = (i * t) // tk

    def diag_scores():
        return jnp.where(cols <= rows + (i * t - jd * tk), scores(jd), NEG_INF)

    qf = q.astype(F32)
    qnorm = jnp.sqrt(jnp.sum(qf * qf, axis=-1, keepdims=True))
    bound = qnorm * knorm_ref[0:1, 0:1] * (1.0 + 2.0 ** -9) + 2.0 ** -6 - ccol_ref[...]
    row_floor = _diagonal_logit(qf, k_ref, pl.multiple_of(i * t, t), t) - ccol_ref[...]
    slack = jnp.max(bound - row_floor)

    def shifted_sweep(_):
        def step(s, j, carry):
            l, acc = carry
            p = jnp.exp2(s - bound)
            return (l + jnp.sum(p, axis=-1, keepdims=True),
                    acc + jnp.dot(p.astype(BF16), values(j), preferred_element_type=F32))

        c_first_row = cstart_ref[h, i]
        first_live = jnp.int32(0)
        for j in range(cend_ref.shape[1]):
            dead = jnp.logical_and(j < jd, c_first_row - cend_ref[h, j] < FLUSH_EXPONENT)
            first_live = first_live + dead.astype(jnp.int32)
        carry = step(diag_scores(), jd, _softmax_init(t)[1:])
        return _paired_loop(jd - first_live, lambda n, c: step(scores(first_live + n), first_live + n, c),
                            carry)

    def running_max_sweep(_):
        carry = _online_softmax_step(diag_scores(), values(jd), _softmax_init(t))
        return lax.fori_loop(0, jd, lambda j, c: process(scores(j), j, c), carry)[1:]

    shifted = shifted_sweep(0)
    l, acc = lax.cond(slack <= SHIFT_SLACK, lambda _: shifted, running_max_sweep, 0)
    o_ref[...] = (acc / l).astype(o_ref.dtype)


def _moba_kernel(tab_ref, q_ref, k_ref, v_ref, bias_ref, et_ref, o_ref, kmean_ref, knorm_ref,
                 *, tb, nb, tk, nblk, n_near, scale):
    h = pl.program_id(0)
    g = pl.program_id(1)
    t = nb * tb
    nbp = kmean_ref.shape[0]

    @pl.when(g == 0)
    def _():
        kmean_ref[...] = jnp.zeros(kmean_ref.shape, F32)

        def mean_body(n, _):
            off = pl.multiple_of(n * tb, tb)
            kb = k_ref[pl.ds(off, tb), :].astype(F32)
            kmean_ref[pl.ds(n, 1), :] = jnp.sum(kb, axis=0, keepdims=True) * (1.0 / tb)
            return 0

        lax.fori_loop(0, nblk, mean_body, 0)
        knorm_ref[...] = jnp.broadcast_to(_max_key_norm(k_ref, tk, 0, HEAD_DIM), knorm_ref.shape)

    q = q_ref[...]
    km_hi, km_lo = _split_bf16(kmean_ref[...])
    gate = _nt_dot(q, km_hi) + _nt_dot(q, km_lo)
    blk = lax.broadcasted_iota(jnp.int32, (t, nbp), 1)
    own = g * nb + lax.broadcasted_iota(jnp.int32, (t, 1), 0) // tb
    eligible = blk < own
    sel = jnp.zeros((t, nbp), F32)
    for r in range(MOBA_TOPK):
        cand = jnp.where(eligible, jnp.where(sel > 0.0, -jnp.inf, gate), -jnp.inf)
        vmax = jnp.max(cand, axis=-1, keepdims=True)
        first = jnp.min(jnp.where(cand == vmax, blk, nbp), axis=-1, keepdims=True)
        take = jnp.where(r < own, 1.0, 0.0)
        sel = jnp.maximum(sel, jnp.where(blk == first, take, 0.0))

    sel = jnp.maximum(sel, jnp.where(blk == own, 1.0, 0.0))
    penalty = jnp.where(sel > 0.0, 0.0, NEG_INF).astype(BF16)
    q_aug = jnp.concatenate([_prescale(q, scale * LOG2E), penalty], axis=1)
    b_far = tab_ref[N_BUCKETS - 1, h] * LOG2E

    def scores(j):
        off = pl.multiple_of(j * tk, tk)
        k_aug = jnp.concatenate([k_ref[pl.ds(off, tk), :], et_ref[pl.ds(off, tk), :]], axis=1)
        return _nt_dot(q_aug, k_aug)

    def values(j):
        return v_ref[pl.ds(pl.multiple_of(j * tk, tk), tk), :]

    def near_bias(j):
        return _near_bias(bias_ref, [g * nb + b for b in range(nb)], j, tb, tk, n_near)

    def process(s, j, carry, near):
        return _online_softmax_step(s + (near_bias(j) if near else b_far), values(j), carry)

    qf = q_aug[:, :HEAD_DIM].astype(F32)
    qnorm = jnp.sqrt(jnp.sum(qf * qf, axis=-1, keepdims=True))
    bound = _logit_bound(qnorm, knorm_ref[0:1, 0:1], tab_ref, h)
    row_floor = _diagonal_logit(qf, k_ref, pl.multiple_of(g * t, t), t) + tab_ref[0, h] * LOG2E
    l, acc = _shifted_or_running_sweep(g, t, tk, scores, near_bias, values, process, bound, row_floor,
                                       b_far, t)
    o_ref[...] = (acc / l).astype(o_ref.dtype)


def _paired_loop(n, body, carry):
    carry = lax.fori_loop(0, n // 2, lambda p, c: body(2 * p + 1, body(2 * p, c)), carry)
    return lax.fori_loop(0, n % 2, lambda _, c: body(n - 1, c), carry)


def _logit_bound(qnorm, knorm, tab_ref, h):
    b_max = tab_ref[0, h]
    for b in range(1, N_BUCKETS):
        b_max = jnp.maximum(b_max, tab_ref[b, h])
    return qnorm * knorm * (1.0 + 2.0 ** -9) + (b_max * LOG2E + 2.0 ** -6)


def _shifted_or_running_sweep(i, t, tk, scores, near_bias, values, process, bound, row_floor, b_far, rows):
    jd = (i * t) // tk
    slack = jnp.max(bound - row_floor)

    def shifted_sweep(_):
        def step(s, j, carry, shift):
            l, acc = carry
            p = jnp.exp2(s - shift)
            return (l + jnp.sum(p, axis=-1, keepdims=True),
                    acc + jnp.dot(p.astype(BF16), values(j), preferred_element_type=F32))

        carry = step(scores(jd) + near_bias(jd), jd, _softmax_init(rows)[1:], bound)
        lo = jnp.maximum(jd - 1, 0)
        carry = lax.fori_loop(lo, jd, lambda j, c: step(scores(j) + near_bias(j), j, c, bound), carry)
        far_shift = bound - b_far
        return _paired_loop(lo, lambda j, c: step(scores(j), j, c, far_shift), carry)

    def running_max_sweep(_):
        return _biased_sweep(i, t, tk, scores, process, _softmax_init(rows))[1:]

    shifted = shifted_sweep(0)
    return lax.cond(slack <= SHIFT_SLACK, lambda _: shifted, running_max_sweep, 0)


def _diagonal_logit(qf, k_ref, row0, t):
    kf = k_ref[pl.ds(row0, t), :].astype(F32)
    if qf.shape[0] != t:
        kf = jnp.concatenate([kf] * (qf.shape[0] // t), axis=0)
    return jnp.sum(qf * kf, axis=-1, keepdims=True)


def _max_key_norm(k_ref, tk, lane_lo, lane_hi):
    klane = lax.broadcasted_iota(jnp.int32, (tk, HEAD_DIM), 1)
    keep = jnp.logical_and(klane >= lane_lo, klane < lane_hi)

    def body(n, c):
        kt = k_ref[pl.ds(pl.multiple_of(n * tk, tk), tk), :].astype(F32)
        sq = jnp.sum(jnp.where(keep, kt * kt, 0.0), axis=-1, keepdims=True)
        return jnp.maximum(c, jnp.max(sq, axis=0, keepdims=True))

    return jnp.sqrt(lax.fori_loop(0, k_ref.shape[0] // tk, body, jnp.zeros((1, 1), F32)))


def _diff_kernel(tab_ref, q_ref, k_ref, v_ref, bias_ref, lq_ref, lk_ref, g_ref, o_ref, knorm_ref,
                 *, tb, nb, tk, n_near, scale, lambda_init):
    h = pl.program_id(0)
    i = pl.program_id(1)
    t = nb * tb
    q = _prescale(q_ref[...], scale * LOG2E)
    lane = lax.broadcasted_iota(jnp.int32, (t, HEAD_DIM), 1)
    zero = jnp.zeros_like(q)
    q2 = jnp.concatenate([jnp.where(lane < DIFF_QK_DIM, q, zero),
                          jnp.where(lane >= DIFF_QK_DIM, q, zero)], axis=0)
    b_far = tab_ref[N_BUCKETS - 1, h] * LOG2E

    @pl.when(i == 0)
    def _():
        knorm_ref[0:1, :] = jnp.broadcast_to(_max_key_norm(k_ref, tk, 0, DIFF_QK_DIM), (1, LANES))
        knorm_ref[1:2, :] = jnp.broadcast_to(_max_key_norm(k_ref, tk, DIFF_QK_DIM, HEAD_DIM), (1, LANES))

    def scores(j):
        return _nt_dot(q2, k_ref[pl.ds(pl.multiple_of(j * tk, tk), tk), :])

    def values(j):
        return v_ref[pl.ds(pl.multiple_of(j * tk, tk), tk), :]

    def near_bias(j):
        b = _near_bias(bias_ref, [i * nb + b for b in range(nb)], j, tb, tk, n_near)
        return jnp.concatenate([b, b], axis=0)

    def process(s, j, carry, near):
        return _online_softmax_step(s + (near_bias(j) if near else b_far), values(j), carry)

    q2f = q2.astype(F32)
    qnorm = jnp.sqrt(jnp.sum(q2f * q2f, axis=-1, keepdims=True))
    knorm = jnp.concatenate([jnp.broadcast_to(knorm_ref[0:1, 0:1], (t, 1)),
                             jnp.broadcast_to(knorm_ref[1:2, 0:1], (t, 1))], axis=0)
    bound = _logit_bound(qnorm, knorm, tab_ref, h)
    row_floor = _diagonal_logit(q2f, k_ref, pl.multiple_of(i * t, t), t) + tab_ref[0, h] * LOG2E
    l, acc = _shifted_or_running_sweep(i, t, tk, scores, near_bias, values, process, bound, row_floor,
                                       b_far, 2 * t)
    o12 = acc / l
    lam_e = jnp.exp(jnp.sum(lq_ref[...] * lk_ref[...], axis=-1, keepdims=True))
    lam = lam_e[0:1, :] - lam_e[1:2, :] + lambda_init
    o = o12[:t, :] - lam * o12[t:, :]
    o = o * lax.rsqrt(jnp.mean(o * o, axis=-1, keepdims=True) + LN_EPS) * g_ref[...]
    o_ref[...] = (o * (1.0 - lambda_init)).astype(o_ref.dtype)


def _sb_kernel(q_ref, k_ref, v_ref, o_ref, *, t, nq, scale):
    g = pl.program_id(1)

    def later_matrix(n):
        r = lax.broadcasted_iota(jnp.int32, (n, n), 0)
        c = lax.broadcasted_iota(jnp.int32, (n, n), 1)
        return jnp.where(r > c, 1.0, 0.0).astype(BF16)

    def block(q, off, nk, rem, acc, shift):
        k = k_ref[pl.ds(off, nk), :]
        v = v_ref[pl.ds(off, nk), :]
        z = _nt_dot(q, k) * scale
        soft = jnp.log1p(jnp.exp(-jnp.abs(z)))
        log_sig = jnp.minimum(z, 0.0) - soft
        log_rem = jnp.minimum(-z, 0.0) - soft
        if shift is not None:
            rows = lax.broadcasted_iota(jnp.int32, (t, nk), 0)
            cols = lax.broadcasted_iota(jnp.int32, (t, nk), 1)
            mask = cols < rows + shift
            log_rem = jnp.where(mask, log_rem, 0.0)
        hi, lo = _split_bf16(log_rem)
        later = later_matrix(nk)
        after = (jnp.dot(hi, later, preferred_element_type=F32)
                 + jnp.dot(lo, later, preferred_element_type=F32))
        a = jnp.exp(log_sig + after + rem)
        if shift is not None:
            a = jnp.where(mask, a, 0.0)
        acc = acc + jnp.dot(a.astype(BF16), v, preferred_element_type=F32)
        rem = rem + jnp.sum(log_rem, axis=-1, keepdims=True)
        return rem, acc

    def live(rem):
        return (jnp.max(rem) > SB_CUTOFF).astype(jnp.int32)

    state = []
    for u in range(nq):
        i = g * nq + u
        q = q_ref[u * t:(u + 1) * t, :]
        first = jnp.maximum(i - 1, 0)
        rem, acc = block(q, pl.multiple_of(first * t, t), 2 * t, jnp.zeros((t, 1), F32),
                         jnp.zeros((t, HEAD_DIM), F32), (i - first) * t)
        state.append((i, q, rem, acc))

    for u, (i, q, rem, acc) in enumerate(state):
        def cond(c):
            j, go, _, _ = c
            return jnp.logical_and(j >= 0, go > 0)

        def body(c, q=q):
            j, _, rem, acc = c
            rem, acc = block(q, pl.multiple_of(j * t, t), t, rem, acc, None)
            return j - 1, live(rem), rem, acc

        _, _, _, acc = lax.while_loop(cond, body, (i - 2, live(rem), rem, acc))
        o_ref[u * t:(u + 1) * t, :] = acc.astype(o_ref.dtype)


def _attention_call(kernel, h, out_cols, col0, extra_in, extra_specs, scratch, name, t):
    s = h.shape[0]
    nh = N_HEADS_GROUP
    in_specs = list(extra_specs[0]) + [
        pl.BlockSpec((t, HEAD_DIM), lambda hd, i: (i, col0 + hd)),
        pl.BlockSpec((s, HEAD_DIM), lambda hd, i: (0, col0 + nh + hd)),
        pl.BlockSpec((s, HEAD_DIM), lambda hd, i: (0, col0 + 2 * nh + hd)),
    ] + list(extra_specs[1])
    args = list(extra_in[0]) + [h, h, h] + list(extra_in[1])
    return pl.pallas_call(
        kernel,
        out_shape=jax.ShapeDtypeStruct((s, out_cols), BF16),
        grid=(nh, s // t),
        in_specs=in_specs,
        out_specs=pl.BlockSpec((t, HEAD_DIM), lambda hd, i: (i, hd)),
        scratch_shapes=scratch,
        compiler_params=_cparams(2, VMEM_LIMIT),
        name=name,
    )(*args)


def _route(x, w, b):
    x_hi, x_lo = _split_bf16(x)
    w_hi, w_lo = _split_bf16(w)
    logits = (jnp.dot(x_hi, w_hi, preferred_element_type=F32) + jnp.dot(x_lo, w_hi, preferred_element_type=F32)
              + jnp.dot(x_hi, w_lo, preferred_element_type=F32)) + b
    tm, n = logits.shape
    lane = lax.broadcasted_iota(jnp.int32, (tm, n), 1)
    real = lane < N_EXPERTS
    logits = jnp.where(real, logits, -jnp.inf)
    e = jnp.exp(logits - jnp.max(logits, axis=-1, keepdims=True))
    aff = e / jnp.sum(e, axis=-1, keepdims=True)
    group = lane // EXPERTS_PER_GROUP
    best = jnp.full((tm, 1), -jnp.inf, F32)
    g_sel = jnp.zeros((tm, 1), jnp.int32)
    for g in range(N_GROUPS):
        gmax = jnp.max(jnp.where(group == g, aff, -jnp.inf), axis=-1, keepdims=True)
        better = gmax > best
        g_sel = jnp.where(better, g, g_sel)
        best = jnp.where(better, gmax, best)
    in_group = jnp.logical_and(group == g_sel, real)
    cand = jnp.where(in_group, aff, -jnp.inf)
    v1 = jnp.max(cand, axis=-1, keepdims=True)
    i1 = jnp.min(jnp.where(cand == v1, lane, n), axis=-1, keepdims=True)
    cand2 = jnp.where(lane == i1, -jnp.inf, cand)
    v2 = jnp.max(cand2, axis=-1, keepdims=True)
    i2 = jnp.min(jnp.where(cand2 == v2, lane, n), axis=-1, keepdims=True)
    tot = v1 + v2
    chosen = jnp.where(jnp.logical_or(lane == i1, lane == i2), 1.0, 0.0)
    r = lax.broadcasted_iota(jnp.int32, (tm, tm), 0)
    c = lax.broadcasted_iota(jnp.int32, (tm, tm), 1)
    earlier = jnp.where(c < r, 1.0, 0.0).astype(BF16)
    before = jnp.dot(earlier, chosen.astype(BF16), preferred_element_type=F32)
    rank1 = jnp.sum(jnp.where(lane == i1, before, 0.0), axis=-1, keepdims=True)
    rank2 = jnp.sum(jnp.where(lane == i2, before, 0.0), axis=-1, keepdims=True)
    route = jnp.where(lane == 0, v1 / tot,
                      jnp.where(lane == 1, v2 / tot,
                                jnp.where(lane == 2, i1.astype(F32),
                                          jnp.where(lane == 3, i2.astype(F32),
                                                    jnp.where(lane == 4, rank1,
                                                              jnp.where(lane == 5, rank2, 0.0))))))
    return route, jnp.sum(chosen, axis=0, keepdims=True)


def _dispatch_plan(route, counts, tm):
    s = route.shape[0]
    nt = s // tm
    n_tiles = (MOE_TOP_K * s) // tm + N_EXPERTS
    ids = jnp.arange(N_EXPERTS, dtype=jnp.int32)
    expert = route[:, 2:2 + MOE_TOP_K].astype(jnp.int32).reshape(nt, tm, MOE_TOP_K)
    rank = route[:, 4:4 + MOE_TOP_K].astype(jnp.int32).reshape(nt, tm, MOE_TOP_K)
    cnt = counts[:, 0, :N_EXPERTS].astype(jnp.int32)
    tile_base = jnp.cumsum(cnt, axis=0) - cnt
    padded = -(-jnp.sum(cnt, axis=0) // tm) * tm
    seg_end = jnp.cumsum(padded)
    base = (seg_end - padded)[None, :] + tile_base
    pick = expert[..., None] == ids
    pair_row = (jnp.sum(jnp.where(pick, base[:, None, None, :], 0), axis=-1) + rank).reshape(s, MOE_TOP_K)
    tile_start = jnp.arange(n_tiles, dtype=jnp.int32) * tm
    tile_expert = jnp.minimum(jnp.sum((tile_start[:, None] >= seg_end[None, :]).astype(jnp.int32), axis=1),
                              N_EXPERTS - 1).astype(jnp.int32)
    tile_live = (tile_start < seg_end[-1]).astype(jnp.int32)
    row_token = (jnp.arange(n_tiles * tm, dtype=jnp.int32) % s).at[pair_row.reshape(-1)].set(
        jnp.arange(MOE_TOP_K * s, dtype=jnp.int32) // MOE_TOP_K)
    return pair_row, row_token, tile_expert, tile_live


def _moe_group_kernel(te_ref, live_ref, x_ref, wg_ref, wu_ref, wd_ref, o_ref, xb_ref, acc_ref):
    g = pl.program_id(0)
    f = pl.program_id(1)
    last = f == pl.num_programs(1) - 1
    live = live_ref[g] > 0

    @pl.when(jnp.logical_and(live, f == 0))
    def _():
        lo, hi = _unpack_bf16_pairs(x_ref[...])
        xb_ref[...] = jnp.concatenate([lo.astype(BF16), hi.astype(BF16)], axis=1)

    @pl.when(jnp.logical_and(jnp.logical_not(live), last))
    def _():
        o_ref[...] = jnp.zeros(o_ref.shape, o_ref.dtype)

    @pl.when(live)
    def _():
        xb = xb_ref[...]
        a = jnp.dot(xb, wg_ref[...], preferred_element_type=F32)
        u = jnp.dot(xb, wu_ref[...], preferred_element_type=F32)
        hid = ((a * jax.nn.sigmoid(a)) * u).astype(BF16)
        y = jnp.dot(hid, wd_ref[...], preferred_element_type=F32)

        @pl.when(f == 0)
        def _():
            acc_ref[...] = y

        @pl.when(f > 0)
        def _():
            acc_ref[...] += y

        @pl.when(last)
        def _():
            o_ref[...] = _pack_bf16_pairs(acc_ref[...])


def _moe_group_ffn(xs, tile_expert, tile_live, wg, wu, wd, tm, tf):
    rows, half = xs.shape
    d = 2 * half
    fdim = wg.shape[2]
    grid_spec = pltpu.PrefetchScalarGridSpec(
        num_scalar_prefetch=2,
        grid=(rows // tm, fdim // tf),
        in_specs=[pl.BlockSpec((tm, half), lambda g, f, te, lv: (g, 0)),
                  pl.BlockSpec((None, d, tf), lambda g, f, te, lv: (te[g], 0, f)),
                  pl.BlockSpec((None, d, tf), lambda g, f, te, lv: (te[g], 0, f)),
                  pl.BlockSpec((None, tf, d), lambda g, f, te, lv: (te[g], f, 0))],
        out_specs=pl.BlockSpec((tm, half), lambda g, f, te, lv: (g, 0)),
        scratch_shapes=[pltpu.VMEM((tm, d), BF16), pltpu.VMEM((tm, d), F32)])
    return pl.pallas_call(
        _moe_group_kernel,
        out_shape=jax.ShapeDtypeStruct((rows, half), jnp.uint32),
        grid_spec=grid_spec,
        compiler_params=_cparams(2, VMEM_LIMIT),
        name="moe_group_ffn",
    )(tile_expert, tile_live, xs, wg, wu, wd)


def _moe_combine_ln_kernel(x_ref, r_ref, y0_ref, y1_ref, g_ref, b_ref, o_ref):
    r = r_ref[...]
    lo0, hi0 = _unpack_bf16_pairs(y0_ref[...])
    lo1, hi1 = _unpack_bf16_pairs(y1_ref[...])
    ff = jnp.concatenate([r[:, 0:1] * lo0 + r[:, 1:2] * lo1, r[:, 0:1] * hi0 + r[:, 1:2] * hi1], axis=1)
    o_ref[...] = _layer_norm(ALPHA * x_ref[...] + ff, g_ref[...], b_ref[...])


def _moe_combine_ln(x, route, y, g, b, tm):
    s, d = x.shape
    nt = s // tm
    return pl.pallas_call(
        _moe_combine_ln_kernel,
        out_shape=jax.ShapeDtypeStruct((s, d), F32),
        grid=(nt,),
        in_specs=[pl.BlockSpec((tm, d), lambda i: (i, 0)),
                  pl.BlockSpec((tm, LANES), lambda i: (i, 0)),
                  pl.BlockSpec((tm, d // 2), lambda i: (i, 0)),
                  pl.BlockSpec((tm, d // 2), lambda i: (i + nt, 0)),
                  pl.BlockSpec((1, d), lambda i: (0, 0)),
                  pl.BlockSpec((1, d), lambda i: (0, 0))],
        out_specs=pl.BlockSpec((tm, d), lambda i: (i, 0)),
        compiler_params=_cparams(1, VMEM_LIMIT),
        name="moe_combine_ln",
    )(x, route, y, y, g.reshape(1, d), b.reshape(1, d))


def _sc_row_gather(src, idx):
    n = idx.shape[0]
    d = src.shape[1]
    workers = SC_CORES * SC_SUBCORES
    per_worker = n // workers
    assert n % (workers * SC_GATHER_ROWS) == 0
    mesh = plsc.VectorSubcoreMesh(core_axis_name="c", subcore_axis_name="s")

    ch = SC_GATHER_ROWS
    n_chunks = per_worker // ch
    assert n_chunks % 2 == 0

    @functools.partial(
        pl.kernel, mesh=mesh, out_type=jax.ShapeDtypeStruct((n, d), src.dtype),
        scratch_types=[pltpu.VMEM((per_worker,), jnp.int32),
                       pltpu.VMEM((ch, d), src.dtype), pltpu.VMEM((ch, d), src.dtype),
                       pltpu.SemaphoreType.DMA, pltpu.SemaphoreType.DMA],
        name="sc_row_gather")
    def gather(src_hbm, idx_hbm, out_hbm, idx_v, rows_a, rows_b, sem_a, sem_b):
        base = (lax.axis_index("s") * SC_CORES + lax.axis_index("c")) * per_worker
        pltpu.sync_copy(idx_hbm.at[pl.ds(base, per_worker)], idx_v)

        def start(c, rows, sem):
            pltpu.async_copy(src_hbm.at[idx_v.at[pl.ds(c * ch, ch)]], rows, sem)

        def finish(c, rows, sem):
            pltpu.make_async_copy(src_hbm.at[pl.ds(0, ch)], rows, sem).wait()
            pltpu.sync_copy(rows, out_hbm.at[pl.ds(base + c * ch, ch)])

        start(0, rows_a, sem_a)

        @pl.loop(0, n_chunks, step=2)
        def _(c):
            start(c + 1, rows_b, sem_b)
            finish(c, rows_a, sem_a)

            @pl.when(c + 2 < n_chunks)
            def _():
                start(c + 2, rows_a, sem_a)

            finish(c + 1, rows_b, sem_b)

    return gather(src, idx)


def _moe_ln(x, x_packed, route, counts, wg, wu, wd, g, b, tm):
    pair_row, row_token, tile_expert, tile_live = _dispatch_plan(route, counts, tm)
    xs = _sc_row_gather(x_packed, row_token)
    ys = _moe_group_ffn(xs, tile_expert, tile_live, wg, wu, wd, tm, MOE_FF_TILE)
    slot_major = pair_row.T.reshape(-1)
    y = _sc_row_gather(ys, slot_major)
    return _moe_combine_ln(x, route, y, g, b, tm)


def kernel(x, w_in_even, w_out_even, forget_bias, w_in_odd, w_out_odd, lambda_q, lambda_k,
           subln_gain, rel_bias, w_router, b_router, w_gate, w_up, w_down, ln_gain, ln_bias):
    bsz, s, d = x.shape
    assert bsz == 1 and d == D_MODEL and s % MOBA_BLOCK == 0
    t = ATT_TILE
    nh = N_HEADS_GROUP
    gw = GROUP_WIDTH
    tm_proj = min(1024, s)
    tm_row = min(512, s)
    tk = min(ATT_KEYS, s)
    n_near = _near_tile_count(t)
    assert s % tk == 0 and tk // t + 1 >= n_near
    assert all(tk % (n * t) == 0 for n in (MOBA_Q_BLOCKS, DIFF_Q_BLOCKS)) and min(FOX_TK, s) % FOX_TQ == 0
    nblk = s // MOBA_BLOCK
    nbp = -(-nblk // LANES) * LANES

    x2 = x.reshape(s, d)
    bias_tiles = _bias_tiles(rel_bias, t, n_near)
    tab_spec = pl.BlockSpec(memory_space=pltpu.SMEM)
    bias_spec = pl.BlockSpec((None, n_near + 2, t, t), lambda hd, i: (hd, 0, 0, 0))
    block_onehot = (jnp.arange(s, dtype=jnp.int32)[:, None] // MOBA_BLOCK
                    == jnp.arange(nbp, dtype=jnp.int32)[None, :]).astype(BF16)

    for layer in range(DEPTH):
        li = layer // 2
        if layer % 2 == 0:
            w = w_in_even[li]
            fcol = 3 * gw
            w_main = jnp.concatenate([w[:, :fcol], w[:, fcol + nh:]], axis=1).astype(BF16)
            w_f = jnp.zeros((d, LANES), BF16).at[:, :nh].set(w[:, fcol:fcol + nh].astype(BF16))
            h = _in_proj(x2, w_main, BF16, tm_proj, PROJ_TN)
            f = _in_proj(x2, w_f, F32, tm_proj, LANES)
            c = _decay_cumsum(f[:, :nh].T, forget_bias[li])
            oa = _attention_call(
                functools.partial(_fox_kernel, t=FOX_TQ, tk=min(FOX_TK, s), scale=HEAD_DIM ** -0.5), h, gw, 0,
                ([c[:, min(FOX_TK, s) - 1::min(FOX_TK, s)], c[:, ::FOX_TQ]],
                 [c.reshape(nh, 1, s), c.reshape(nh, s, 1)]),
                ([tab_spec, tab_spec],
                 [pl.BlockSpec((None, 1, s), lambda hd, i: (hd, 0, 0)),
                  pl.BlockSpec((None, FOX_TQ, 1), lambda hd, i: (hd, i, 0))]),
                [pltpu.VMEM((SUBLANES, LANES), F32)],
                "fox_attention", FOX_TQ)
            ob = _attention_call(
                functools.partial(_moba_kernel, tb=t, nb=MOBA_Q_BLOCKS, tk=tk, nblk=nblk, n_near=n_near,
                                  scale=HEAD_DIM ** -0.5),
                h, gw, 3 * nh,
                ([rel_bias], [bias_tiles, block_onehot]),
                ([tab_spec], [bias_spec, pl.BlockSpec((s, nbp), lambda hd, i: (0, 0))]),
                [pltpu.VMEM((nbp, HEAD_DIM), F32), pltpu.VMEM((SUBLANES, LANES), F32)],
                "moba_attention", MOBA_Q_BLOCKS * t)
            w_out = w_out_even[li]
        else:
            lambda_init = 0.8 - 0.6 * math.exp(-0.3 * layer)
            h = _in_proj(x2, w_in_odd[li].astype(BF16), BF16, tm_proj, PROJ_TN)
            small = lambda shape: pl.BlockSpec(shape, lambda hd, i: (0, 0))
            oa = _attention_call(
                functools.partial(_diff_kernel, tb=t, nb=DIFF_Q_BLOCKS, tk=tk, n_near=n_near,
                                  scale=DIFF_QK_DIM ** -0.5, lambda_init=lambda_init),
                h, gw, 0,
                ([rel_bias], [bias_tiles, lambda_q[li], lambda_k[li], subln_gain[li].reshape(1, HEAD_DIM)]),
                ([tab_spec], [bias_spec, small((2, DIFF_QK_DIM)), small((2, DIFF_QK_DIM)),
                              small((1, HEAD_DIM))]),
                [pltpu.VMEM((SUBLANES, LANES), F32)],
                "diff_attention", DIFF_Q_BLOCKS * t)
            ob = _attention_call(
                functools.partial(_sb_kernel, t=t, nq=SB_Q_TILES, scale=HEAD_DIM ** -0.5), h, gw, 3 * nh,
                ([], []), ([], []), [], "stickbreak_attention", SB_Q_TILES * t)
            w_out = w_out_odd[li]
        x2, x2_packed = _out_proj_ln(oa, ob, w_out.astype(BF16), x2, ln_gain[layer, 0], ln_bias[layer, 0],
                                     tm_row)
        route, counts = _router(x2, w_router, b_router, tm_row)
        x2 = _moe_ln(x2, x2_packed, route, counts, w_gate[layer].astype(BF16), w_up[layer].astype(BF16),
                     w_down[layer].astype(BF16), ln_gain[layer, 1], ln_bias[layer, 1], tm_row)
    return x2.reshape(bsz, s, d)
```

```python
import functools
import math

import numpy as np
import jax
import jax.numpy as jnp
from jax import lax
from jax.experimental import pallas as pl
from jax.experimental.pallas import tpu as pltpu
from jax.experimental.pallas import tpu_sc as plsc

F32 = jnp.float32
BF16 = jnp.bfloat16

D_MODEL = 2048
DEPTH = 2
HEAD_DIM = 128
N_HEADS_GROUP = 8
GROUP_WIDTH = N_HEADS_GROUP * HEAD_DIM
DIFF_QK_DIM = HEAD_DIM // 2
MOBA_BLOCK = 256
MOBA_TOPK = 3
N_BUCKETS = 32
MAX_EXACT = N_BUCKETS // 2
MAX_DISTANCE = 1024
N_EXPERTS = 16
N_GROUPS = 4
EXPERTS_PER_GROUP = N_EXPERTS // N_GROUPS
MOE_TOP_K = 2
SC_CORES = 2
SC_SUBCORES = 16
SC_GATHER_ROWS = 32
SUBLANES = 8
D_EXPERT = D_MODEL // 2
PROJ_TN = 1024
MOE_FF_TILE = D_EXPERT
ALPHA = (2.0 * DEPTH) ** 0.25
LN_EPS = 1e-5
NEG_INF = -1e30
LOG2E = math.log2(math.e)
LANES = 128
ATT_TILE = 256
ATT_KEYS = 1024
FOX_TQ = 1024
FOX_TK = 1024
MOBA_Q_BLOCKS = 4
SB_Q_TILES = 4
DIFF_Q_BLOCKS = 4
SHIFT_SLACK = 96.0
FLUSH_EXPONENT = -130.0
SB_CUTOFF = -104.0
VMEM_LIMIT = 56 * 1024 * 1024


def _cparams(n_axes, vmem=None):
    return pltpu.CompilerParams(dimension_semantics=("arbitrary",) * n_axes,
                                vmem_limit_bytes=vmem)


def _nt_dot(a, b):
    return lax.dot_general(a, b, (((1,), (1,)), ((), ())), preferred_element_type=F32)


def _split_bf16(x):
    hi = x.astype(BF16)
    lo = (x - hi.astype(F32)).astype(BF16)
    return hi, lo


def _pack_bf16_pairs(x):
    n = x.shape[1] // 2
    lo = pltpu.bitcast(x[:, :n].astype(BF16).astype(F32), jnp.uint32)
    hi = pltpu.bitcast(x[:, n:].astype(BF16).astype(F32), jnp.uint32)
    return jnp.bitwise_or(lax.shift_right_logical(lo, jnp.uint32(16)),
                          jnp.bitwise_and(hi, jnp.uint32(0xFFFF0000)))


def _unpack_bf16_pairs(w):
    lo = pltpu.bitcast(lax.shift_left(w, jnp.uint32(16)), F32)
    hi = pltpu.bitcast(jnp.bitwise_and(w, jnp.uint32(0xFFFF0000)), F32)
    return lo, hi


def _layer_norm(z, g, b):
    mu = jnp.mean(z, axis=-1, keepdims=True)
    zc = z - mu
    var = jnp.mean(zc * zc, axis=-1, keepdims=True)
    return zc * lax.rsqrt(var + LN_EPS) * g + b


def _inproj_kernel(x_ref, w_ref, o_ref, xb_ref):
    @pl.when(pl.program_id(1) == 0)
    def _():
        xb_ref[...] = x_ref[...].astype(BF16)

    o_ref[...] = jnp.dot(xb_ref[...], w_ref[...], preferred_element_type=F32).astype(o_ref.dtype)


def _in_proj(x, w, out_dtype, tm, tn):
    s, d = x.shape
    n = w.shape[1]
    return pl.pallas_call(
        _inproj_kernel,
        out_shape=jax.ShapeDtypeStruct((s, n), out_dtype),
        grid=(s // tm, n // tn),
        in_specs=[pl.BlockSpec((tm, d), lambda i, j: (i, 0)),
                  pl.BlockSpec((d, tn), lambda i, j: (0, j))],
        out_specs=pl.BlockSpec((tm, tn), lambda i, j: (i, j)),
        scratch_shapes=[pltpu.VMEM((tm, d), BF16)],
        compiler_params=_cparams(2, VMEM_LIMIT),
        name="in_proj",
    )(x, w)


def _outproj_ln_kernel(oa_ref, ob_ref, w_ref, x_ref, g_ref, b_ref, o_ref, packed_ref):
    half = o_ref.shape[0] // 2
    for r in (slice(0, half), slice(half, 2 * half)):
        o = jnp.concatenate([oa_ref[r, :], ob_ref[r, :]], axis=1)
        y = jnp.dot(o, w_ref[...], preferred_element_type=F32)
        x1 = _layer_norm(ALPHA * x_ref[r, :] + y, g_ref[...], b_ref[...])
        o_ref[r, :] = x1
        packed_ref[r, :] = _pack_bf16_pairs(x1)


def _out_proj_ln(oa, ob, w, x, g, b, tm):
    s, d = x.shape
    gw = oa.shape[1]
    return pl.pallas_call(
        _outproj_ln_kernel,
        out_shape=(jax.ShapeDtypeStruct((s, d), F32), jax.ShapeDtypeStruct((s, d // 2), jnp.uint32)),
        grid=(s // tm,),
        in_specs=[pl.BlockSpec((tm, gw), lambda i: (i, 0)),
                  pl.BlockSpec((tm, gw), lambda i: (i, 0)),
                  pl.BlockSpec((2 * gw, d), lambda i: (0, 0)),
                  pl.BlockSpec((tm, d), lambda i: (i, 0)),
                  pl.BlockSpec((1, d), lambda i: (0, 0)),
                  pl.BlockSpec((1, d), lambda i: (0, 0))],
        out_specs=(pl.BlockSpec((tm, d), lambda i: (i, 0)),
                   pl.BlockSpec((tm, d // 2), lambda i: (i, 0))),
        compiler_params=_cparams(1, VMEM_LIMIT),
        name="out_proj_ln",
    )(oa, ob, w, x, g.reshape(1, d), b.reshape(1, d))


def _router_kernel(x_ref, w_ref, b_ref, route_ref, count_ref):
    route, counts = _route(x_ref[...], w_ref[...], b_ref[...])
    route_ref[...] = route
    count_ref[...] = jnp.broadcast_to(counts, count_ref.shape)


def _router(x, w_router, b_router, tm):
    s, d = x.shape
    wr = jnp.zeros((d, LANES), F32).at[:, :N_EXPERTS].set(w_router)
    br = jnp.zeros((1, LANES), F32).at[0, :N_EXPERTS].set(b_router)
    return pl.pallas_call(
        _router_kernel,
        out_shape=(jax.ShapeDtypeStruct((s, LANES), F32),
                   jax.ShapeDtypeStruct((s // tm, SUBLANES, LANES), F32)),
        grid=(s // tm,),
        in_specs=[pl.BlockSpec((tm, d), lambda i: (i, 0)),
                  pl.BlockSpec((d, LANES), lambda i: (0, 0)),
                  pl.BlockSpec((1, LANES), lambda i: (0, 0))],
        out_specs=(pl.BlockSpec((tm, LANES), lambda i: (i, 0)),
                   pl.BlockSpec((None, SUBLANES, LANES), lambda i: (i, 0, 0))),
        compiler_params=_cparams(1, VMEM_LIMIT),
        name="router",
    )(x, wr, br)


def _decay_cumsum_kernel(f_ref, b_ref, c_ref):
    nh, s = f_ref.shape
    rows = lax.broadcasted_iota(jnp.int32, (LANES, LANES), 0)
    cols = lax.broadcasted_iota(jnp.int32, (LANES, LANES), 1)
    upper = jnp.where(rows <= cols, 1.0, 0.0).astype(F32)
    bias = b_ref[...]

    def body(n, carry):
        off = pl.multiple_of(n * LANES, LANES)
        z = f_ref[:, pl.ds(off, LANES)] + bias
        logf = jnp.minimum(z, 0.0) - jnp.log1p(jnp.exp(-jnp.abs(z)))
        c = jnp.dot(logf, upper, preferred_element_type=F32,
                    precision=lax.Precision.HIGHEST) + carry
        c_ref[:, pl.ds(off, LANES)] = c * LOG2E
        return carry + jnp.sum(logf, axis=-1, keepdims=True)

    lax.fori_loop(0, s // LANES, body, jnp.zeros((nh, 1), F32))


def _decay_cumsum(f_t, bias):
    nh, s = f_t.shape
    return pl.pallas_call(
        _decay_cumsum_kernel,
        out_shape=jax.ShapeDtypeStruct((nh, s), F32),
        name="decay_cumsum",
    )(f_t, bias.reshape(nh, 1))


def _t5_bucket_np(dist):
    n = np.maximum(dist, 0)
    nf = np.maximum(n, 1).astype(np.float32)
    ratio = np.log(nf / np.float32(MAX_EXACT)) / np.float32(math.log(MAX_DISTANCE / MAX_EXACT))
    large = MAX_EXACT + (ratio.astype(np.float32) * np.float32(N_BUCKETS - MAX_EXACT)).astype(np.int32)
    large = np.minimum(large, N_BUCKETS - 1)
    return np.where(n < MAX_EXACT, n, large).astype(np.int32)


def _near_tile_count(t):
    d = np.arange(0, 4 * MAX_DISTANCE, dtype=np.int64)
    not_last = np.nonzero(_t5_bucket_np(d) != N_BUCKETS - 1)[0]
    d_sat = int(not_last.max()) + 1
    n = 1
    while (n - 1) * t + 1 < d_sat:
        n += 1
    return n


def _bucket_tiles_np(t, n_near):
    r = np.arange(t)[:, None]
    c = np.arange(t)[None, :]
    tiles = []
    for delta in range(n_near):
        dist = delta * t + r - c
        tiles.append(np.where(dist >= 0, _t5_bucket_np(dist), -1))
    tiles.append(np.full((t, t), N_BUCKETS - 1))
    tiles.append(np.full((t, t), -1))
    return np.stack(tiles).astype(np.int32)


def _bias_tiles_kernel(tab_ref, idx_ref, o_ref):
    h = pl.program_id(0)
    idx = idx_ref[...]
    out = jnp.full(idx.shape, NEG_INF, F32)
    for b in range(N_BUCKETS):
        out = jnp.where(idx == b, tab_ref[b, h] * LOG2E, out)
    o_ref[...] = out


def _bias_tiles(rel_bias, t, n_near):
    nh = rel_bias.shape[1]
    idx = jnp.asarray(_bucket_tiles_np(t, n_near))
    n_tiles = idx.shape[0]
    return pl.pallas_call(
        _bias_tiles_kernel,
        out_shape=jax.ShapeDtypeStruct((nh, n_tiles, t, t), F32),
        grid=(nh, n_tiles),
        in_specs=[pl.BlockSpec(memory_space=pltpu.SMEM),
                  pl.BlockSpec((None, t, t), lambda h, n: (n, 0, 0))],
        out_specs=pl.BlockSpec((None, None, t, t), lambda h, n: (h, n, 0, 0)),
        compiler_params=_cparams(2),
        name="bias_tiles",
    )(rel_bias, idx)


def _online_softmax_step(s, v, carry):
    m, l, acc = carry
    m_new = jnp.maximum(m, jnp.max(s, axis=-1, keepdims=True))
    alpha = jnp.exp2(m - m_new)
    p = jnp.exp2(s - m_new)
    l = alpha * l + jnp.sum(p, axis=-1, keepdims=True)
    acc = alpha * acc + jnp.dot(p.astype(BF16), v, preferred_element_type=F32)
    return m_new, l, acc


def _prescale(q, c):
    return (q.astype(F32) * c).astype(BF16)


def _near_bias(bias_ref, blocks, j, tb, tk, n_near):
    nsub = tk // tb
    rows = []
    for i in blocks:
        tiles = []
        for b in range(nsub):
            delta = i - (j * nsub + b)
            tiles.append(bias_ref[jnp.where(delta < 0, n_near + 1, jnp.minimum(delta, n_near))])
        rows.append(jnp.concatenate(tiles, axis=1))
    return rows[0] if len(rows) == 1 else jnp.concatenate(rows, axis=0)


def _biased_sweep(i, t, tk, scores, process, init):
    jd = (i * t) // tk
    carry = process(scores(jd), jd, init, True)
    lo = jnp.maximum(jd - 1, 0)
    carry = lax.fori_loop(lo, jd, lambda j, c: process(scores(j), j, c, True), carry)
    return lax.fori_loop(0, lo, lambda j, c: process(scores(j), j, c, False), carry)


def _softmax_init(rows):
    return (jnp.full((rows, 1), NEG_INF, F32), jnp.zeros((rows, 1), F32),
            jnp.zeros((rows, HEAD_DIM), F32))


def _fox_kernel(cend_ref, cstart_ref, q_ref, k_ref, v_ref, c_ref, ccol_ref, o_ref, knorm_ref,
                *, t, tk, scale):
    h = pl.program_id(0)
    i = pl.program_id(1)

    @pl.when(i == 0)
    def _():
        knorm_ref[...] = jnp.broadcast_to(_max_key_norm(k_ref, tk, 0, HEAD_DIM), knorm_ref.shape)

    q = _prescale(q_ref[...], scale * LOG2E)
    rows = lax.broadcasted_iota(jnp.int32, (t, tk), 0)
    cols = lax.broadcasted_iota(jnp.int32, (t, tk), 1)

    def scores(j):
        off = pl.multiple_of(j * tk, tk)
        return _nt_dot(q, k_ref[pl.ds(off, tk), :]) - c_ref[:, pl.ds(off, tk)]

    def values(j):
        return v_ref[pl.ds(pl.multiple_of(j * tk, tk), tk), :]

    def process(s, j, carry):
        return _online_softmax_step(s, values(j), carry)

    jd = (i * t) // tk

    def diag_scores():
        return jnp.where(cols <= rows + (i * t - jd * tk), scores(jd), NEG_INF)

    qf = q.astype(F32)
    qnorm = jnp.sqrt(jnp.sum(qf * qf, axis=-1, keepdims=True))
    bound = qnorm * knorm_ref[0:1, 0:1] * (1.0 + 2.0 ** -9) + 2.0 ** -6 - ccol_ref[...]
    row_floor = _diagonal_logit(qf, k_ref, pl.multiple_of(i * t, t), t) - ccol_ref[...]
    slack = jnp.max(bound - row_floor)

    def shifted_sweep(_):
        def step(s, j, carry):
            l, acc = carry
            p = jnp.exp2(s - bound)
            return (l + jnp.sum(p, axis=-1, keepdims=True),
                    acc + jnp.dot(p.astype(BF16), values(j), preferred_element_type=F32))

        c_first_row = cstart_ref[h, i]
        first_live = jnp.int32(0)
        for j in range(cend_ref.shape[1]):
            dead = jnp.logical_and(j < jd, c_first_row - cend_ref[h, j] < FLUSH_EXPONENT)
            first_live = first_live + dead.astype(jnp.int32)
        carry = step(diag_scores(), jd, _softmax_init(t)[1:])
        return _paired_loop(jd - first_live, lambda n, c: step(scores(first_live + n), first_live + n, c),
                            carry)

    def running_max_sweep(_):
        carry = _online_softmax_step(diag_scores(), values(jd), _softmax_init(t))
        return lax.fori_loop(0, jd, lambda j, c: process(scores(j), j, c), carry)[1:]

    shifted = shifted_sweep(0)
    l, acc = lax.cond(slack <= SHIFT_SLACK, lambda _: shifted, running_max_sweep, 0)
    o_ref[...] = (acc / l).astype(o_ref.dtype)


def _moba_kernel(tab_ref, q_ref, k_ref, v_ref, bias_ref, et_ref, o_ref, kmean_ref, knorm_ref,
                 *, tb, nb, tk, nblk, n_near, scale):
    h = pl.program_id(0)
    g = pl.program_id(1)
    t = nb * tb
    nbp = kmean_ref.shape[0]

    @pl.when(g == 0)
    def _():
        kmean_ref[...] = jnp.zeros(kmean_ref.shape, F32)

        def mean_body(n, _):
            off = pl.multiple_of(n * tb, tb)
            kb = k_ref[pl.ds(off, tb), :].astype(F32)
            kmean_ref[pl.ds(n, 1), :] = jnp.sum(kb, axis=0, keepdims=True) * (1.0 / tb)
            return 0

        lax.fori_loop(0, nblk, mean_body, 0)
        knorm_ref[...] = jnp.broadcast_to(_max_key_norm(k_ref, tk, 0, HEAD_DIM), knorm_ref.shape)

    q = q_ref[...]
    km_hi, km_lo = _split_bf16(kmean_ref[...])
    gate = _nt_dot(q, km_hi) + _nt_dot(q, km_lo)
    blk = lax.broadcasted_iota(jnp.int32, (t, nbp), 1)
    own = g * nb + lax.broadcasted_iota(jnp.int32, (t, 1), 0) // tb
    eligible = blk < own
    sel = jnp.zeros((t, nbp), F32)
    for r in range(MOBA_TOPK):
        cand = jnp.where(eligible, jnp.where(sel > 0.0, -jnp.inf, gate), -jnp.inf)
        vmax = jnp.max(cand, axis=-1, keepdims=True)
        first = jnp.min(jnp.where(cand == vmax, blk, nbp), axis=-1, keepdims=True)
        take = jnp.where(r < own, 1.0, 0.0)
        sel = jnp.maximum(sel, jnp.where(blk == first, take, 0.0))

    sel = jnp.maximum(sel, jnp.where(blk == own, 1.0, 0.0))
    penalty = jnp.where(sel > 0.0, 0.0, NEG_INF).astype(BF16)
    q_aug = jnp.concatenate([_prescale(q, scale * LOG2E), penalty], axis=1)
    b_far = tab_ref[N_BUCKETS - 1, h] * LOG2E

    def scores(j):
        off = pl.multiple_of(j * tk, tk)
        k_aug = jnp.concatenate([k_ref[pl.ds(off, tk), :], et_ref[pl.ds(off, tk), :]], axis=1)
        return _nt_dot(q_aug, k_aug)

    def values(j):
        return v_ref[pl.ds(pl.multiple_of(j * tk, tk), tk), :]

    def near_bias(j):
        return _near_bias(bias_ref, [g * nb + b for b in range(nb)], j, tb, tk, n_near)

    def process(s, j, carry, near):
        return _online_softmax_step(s + (near_bias(j) if near else b_far), values(j), carry)

    qf = q_aug[:, :HEAD_DIM].astype(F32)
    qnorm = jnp.sqrt(jnp.sum(qf * qf, axis=-1, keepdims=True))
    bound = _logit_bound(qnorm, knorm_ref[0:1, 0:1], tab_ref, h)
    row_floor = _diagonal_logit(qf, k_ref, pl.multiple_of(g * t, t), t) + tab_ref[0, h] * LOG2E
    l, acc = _shifted_or_running_sweep(g, t, tk, scores, near_bias, values, process, bound, row_floor,
                                       b_far, t)
    o_ref[...] = (acc / l).astype(o_ref.dtype)


def _paired_loop(n, body, carry):
    carry = lax.fori_loop(0, n // 2, lambda p, c: body(2 * p + 1, body(2 * p, c)), carry)
    return lax.fori_loop(0, n % 2, lambda _, c: body(n - 1, c), carry)


def _logit_bound(qnorm, knorm, tab_ref, h):
    b_max = tab_ref[0, h]
    for b in range(1, N_BUCKETS):
        b_max = jnp.maximum(b_max, tab_ref[b, h])
    return qnorm * knorm * (1.0 + 2.0 ** -9) + (b_max * LOG2E + 2.0 ** -6)


def _shifted_or_running_sweep(i, t, tk, scores, near_bias, values, process, bound, row_floor, b_far, rows):
    jd = (i * t) // tk
    slack = jnp.max(bound - row_floor)

    def shifted_sweep(_):
        def step(s, j, carry, shift):
            l, acc = carry
            p = jnp.exp2(s - shift)
            return (l + jnp.sum(p, axis=-1, keepdims=True),
                    acc + jnp.dot(p.astype(BF16), values(j), preferred_element_type=F32))

        carry = step(scores(jd) + near_bias(jd), jd, _softmax_init(rows)[1:], bound)
        lo = jnp.maximum(jd - 1, 0)
        carry = lax.fori_loop(lo, jd, lambda j, c: step(scores(j) + near_bias(j), j, c, bound), carry)
        far_shift = bound - b_far
        return _paired_loop(lo, lambda j, c: step(scores(j), j, c, far_shift), carry)

    def running_max_sweep(_):
        return _biased_sweep(i, t, tk, scores, process, _softmax_init(rows))[1:]

    shifted = shifted_sweep(0)
    return lax.cond(slack <= SHIFT_SLACK, lambda _: shifted, running_max_sweep, 0)


def _diagonal_logit(qf, k_ref, row0, t):
    kf = k_ref[pl.ds(row0, t), :].astype(F32)
    if qf.shape[0] != t:
        kf = jnp.concatenate([kf] * (qf.shape[0] // t), axis=0)
    return jnp.sum(qf * kf, axis=-1, keepdims=True)


def _max_key_norm(k_ref, tk, lane_lo, lane_hi):
    klane = lax.broadcasted_iota(jnp.int32, (tk, HEAD_DIM), 1)
    keep = jnp.logical_and(klane >= lane_lo, klane < lane_hi)

    def body(n, c):
        kt = k_ref[pl.ds(pl.multiple_of(n * tk, tk), tk), :].astype(F32)
        sq = jnp.sum(jnp.where(keep, kt * kt, 0.0), axis=-1, keepdims=True)
        return jnp.maximum(c, jnp.max(sq, axis=0, keepdims=True))

    return jnp.sqrt(lax.fori_loop(0, k_ref.shape[0] // tk, body, jnp.zeros((1, 1), F32)))


def _diff_kernel(tab_ref, q_ref, k_ref, v_ref, bias_ref, lq_ref, lk_ref, g_ref, o_ref, knorm_ref,
                 *, tb, nb, tk, n_near, scale, lambda_init):
    h = pl.program_id(0)
    i = pl.program_id(1)
    t = nb * tb
    q = _prescale(q_ref[...], scale * LOG2E)
    lane = lax.broadcasted_iota(jnp.int32, (t, HEAD_DIM), 1)
    zero = jnp.zeros_like(q)
    q2 = jnp.concatenate([jnp.where(lane < DIFF_QK_DIM, q, zero),
                          jnp.where(lane >= DIFF_QK_DIM, q, zero)], axis=0)
    b_far = tab_ref[N_BUCKETS - 1, h] * LOG2E

    @pl.when(i == 0)
    def _():
        knorm_ref[0:1, :] = jnp.broadcast_to(_max_key_norm(k_ref, tk, 0, DIFF_QK_DIM), (1, LANES))
        knorm_ref[1:2, :] = jnp.broadcast_to(_max_key_norm(k_ref, tk, DIFF_QK_DIM, HEAD_DIM), (1, LANES))

    def scores(j):
        return _nt_dot(q2, k_ref[pl.ds(pl.multiple_of(j * tk, tk), tk), :])

    def values(j):
        return v_ref[pl.ds(pl.multiple_of(j * tk, tk), tk), :]

    def near_bias(j):
        b = _near_bias(bias_ref, [i * nb + b for b in range(nb)], j, tb, tk, n_near)
        return jnp.concatenate([b, b], axis=0)

    def process(s, j, carry, near):
        return _online_softmax_step(s + (near_bias(j) if near else b_far), values(j), carry)

    q2f = q2.astype(F32)
    qnorm = jnp.sqrt(jnp.sum(q2f * q2f, axis=-1, keepdims=True))
    knorm = jnp.concatenate([jnp.broadcast_to(knorm_ref[0:1, 0:1], (t, 1)),
                             jnp.broadcast_to(knorm_ref[1:2, 0:1], (t, 1))], axis=0)
    bound = _logit_bound(qnorm, knorm, tab_ref, h)
    row_floor = _diagonal_logit(q2f, k_ref, pl.multiple_of(i * t, t), t) + tab_ref[0, h] * LOG2E
    l, acc = _shifted_or_running_sweep(i, t, tk, scores, near_bias, values, process, bound, row_floor,
                                       b_far, 2 * t)
    o12 = acc / l
    lam_e = jnp.exp(jnp.sum(lq_ref[...] * lk_ref[...], axis=-1, keepdims=True))
    lam = lam_e[0:1, :] - lam_e[1:2, :] + lambda_init
    o = o12[:t, :] - lam * o12[t:, :]
    o = o * lax.rsqrt(jnp.mean(o * o, axis=-1, keepdims=True) + LN_EPS) * g_ref[...]
    o_ref[...] = (o * (1.0 - lambda_init)).astype(o_ref.dtype)


def _sb_kernel(q_ref, k_ref, v_ref, o_ref, *, t, nq, scale):
    g = pl.program_id(1)

    def later_matrix(n):
        r = lax.broadcasted_iota(jnp.int32, (n, n), 0)
        c = lax.broadcasted_iota(jnp.int32, (n, n), 1)
        return jnp.where(r > c, 1.0, 0.0).astype(BF16)

    def block(q, off, nk, rem, acc, shift):
        k = k_ref[pl.ds(off, nk), :]
        v = v_ref[pl.ds(off, nk), :]
        z = _nt_dot(q, k) * scale
        soft = jnp.log1p(jnp.exp(-jnp.abs(z)))
        log_sig = jnp.minimum(z, 0.0) - soft
        log_rem = jnp.minimum(-z, 0.0) - soft
        if shift is not None:
            rows = lax.broadcasted_iota(jnp.int32, (t, nk), 0)
            cols = lax.broadcasted_iota(jnp.int32, (t, nk), 1)
            mask = cols < rows + shift
            log_rem = jnp.where(mask, log_rem, 0.0)
        hi, lo = _split_bf16(log_rem)
        later = later_matrix(nk)
        after = (jnp.dot(hi, later, preferred_element_type=F32)
                 + jnp.dot(lo, later, preferred_element_type=F32))
        a = jnp.exp(log_sig + after + rem)
        if shift is not None:
            a = jnp.where(mask, a, 0.0)
        acc = acc + jnp.dot(a.astype(BF16), v, preferred_element_type=F32)
        rem = rem + jnp.sum(log_rem, axis=-1, keepdims=True)
        return rem, acc

    def live(rem):
        return (jnp.max(rem) > SB_CUTOFF).astype(jnp.int32)

    state = []
    for u in range(nq):
        i = g * nq + u
        q = q_ref[u * t:(u + 1) * t, :]
        first = jnp.maximum(i - 1, 0)
        rem, acc = block(q, pl.multiple_of(first * t, t), 2 * t, jnp.zeros((t, 1), F32),
                         jnp.zeros((t, HEAD_DIM), F32), (i - first) * t)
        state.append((i, q, rem, acc))

    for u, (i, q, rem, acc) in enumerate(state):
        def cond(c):
            j, go, _, _ = c
            return jnp.logical_and(j >= 0, go > 0)

        def body(c, q=q):
            j, _, rem, acc = c
            rem, acc = block(q, pl.multiple_of(j * t, t), t, rem, acc, None)
            return j - 1, live(rem), rem, acc

        _, _, _, acc = lax.while_loop(cond, body, (i - 2, live(rem), rem, acc))
        o_ref[u * t:(u + 1) * t, :] = acc.astype(o_ref.dtype)


def _attention_call(kernel, h, out_cols, col0, extra_in, extra_specs, scratch, name, t):
    s = h.shape[0]
    nh = N_HEADS_GROUP
    in_specs = list(extra_specs[0]) + [
        pl.BlockSpec((t, HEAD_DIM), lambda hd, i: (i, col0 + hd)),
        pl.BlockSpec((s, HEAD_DIM), lambda hd, i: (0, col0 + nh + hd)),
        pl.BlockSpec((s, HEAD_DIM), lambda hd, i: (0, col0 + 2 * nh + hd)),
    ] + list(extra_specs[1])
    args = list(extra_in[0]) + [h, h, h] + list(extra_in[1])
    return pl.pallas_call(
        kernel,
        out_shape=jax.ShapeDtypeStruct((s, out_cols), BF16),
        grid=(nh, s // t),
        in_specs=in_specs,
        out_specs=pl.BlockSpec((t, HEAD_DIM), lambda hd, i: (i, hd)),
        scratch_shapes=scratch,
        compiler_params=_cparams(2, VMEM_LIMIT),
        name=name,
    )(*args)


def _route(x, w, b):
    x_hi, x_lo = _split_bf16(x)
    w_hi, w_lo = _split_bf16(w)
    logits = (jnp.dot(x_hi, w_hi, preferred_element_type=F32) + jnp.dot(x_lo, w_hi, preferred_element_type=F32)
              + jnp.dot(x_hi, w_lo, preferred_element_type=F32)) + b
    tm, n = logits.shape
    lane = lax.broadcasted_iota(jnp.int32, (tm, n), 1)
    real = lane < N_EXPERTS
    logits = jnp.where(real, logits, -jnp.inf)
    e = jnp.exp(logits - jnp.max(logits, axis=-1, keepdims=True))
    aff = e / jnp.sum(e, axis=-1, keepdims=True)
    group = lane // EXPERTS_PER_GROUP
    best = jnp.full((tm, 1), -jnp.inf, F32)
    g_sel = jnp.zeros((tm, 1), jnp.int32)
    for g in range(N_GROUPS):
        gmax = jnp.max(jnp.where(group == g, aff, -jnp.inf), axis=-1, keepdims=True)
        better = gmax > best
        g_sel = jnp.where(better, g, g_sel)
        best = jnp.where(better, gmax, best)
    in_group = jnp.logical_and(group == g_sel, real)
    cand = jnp.where(in_group, aff, -jnp.inf)
    v1 = jnp.max(cand, axis=-1, keepdims=True)
    i1 = jnp.min(jnp.where(cand == v1, lane, n), axis=-1, keepdims=True)
    cand2 = jnp.where(lane == i1, -jnp.inf, cand)
    v2 = jnp.max(cand2, axis=-1, keepdims=True)
    i2 = jnp.min(jnp.where(cand2 == v2, lane, n), axis=-1, keepdims=True)
    tot = v1 + v2
    chosen = jnp.where(jnp.logical_or(lane == i1, lane == i2), 1.0, 0.0)
    r = lax.broadcasted_iota(jnp.int32, (tm, tm), 0)
    c = lax.broadcasted_iota(jnp.int32, (tm, tm), 1)
    earlier = jnp.where(c < r, 1.0, 0.0).astype(BF16)
    before = jnp.dot(earlier, chosen.astype(BF16), preferred_element_type=F32)
    rank1 = jnp.sum(jnp.where(lane == i1, before, 0.0), axis=-1, keepdims=True)
    rank2 = jnp.sum(jnp.where(lane == i2, before, 0.0), axis=-1, keepdims=True)
    route = jnp.where(lane == 0, v1 / tot,
                      jnp.where(lane == 1, v2 / tot,
                                jnp.where(lane == 2, i1.astype(F32),
                                          jnp.where(lane == 3, i2.astype(F32),
                                                    jnp.where(lane == 4, rank1,
                                                              jnp.where(lane == 5, rank2, 0.0))))))
    return route, jnp.sum(chosen, axis=0, keepdims=True)


def _dispatch_plan(route, counts, tm):
    s = route.shape[0]
    nt = s // tm
    n_tiles = (MOE_TOP_K * s) // tm + N_EXPERTS
    ids = jnp.arange(N_EXPERTS, dtype=jnp.int32)
    expert = route[:, 2:2 + MOE_TOP_K].astype(jnp.int32).reshape(nt, tm, MOE_TOP_K)
    rank = route[:, 4:4 + MOE_TOP_K].astype(jnp.int32).reshape(nt, tm, MOE_TOP_K)
    cnt = counts[:, 0, :N_EXPERTS].astype(jnp.int32)
    tile_base = jnp.cumsum(cnt, axis=0) - cnt
    padded = -(-jnp.sum(cnt, axis=0) // tm) * tm
    seg_end = jnp.cumsum(padded)
    base = (seg_end - padded)[None, :] + tile_base
    pick = expert[..., None] == ids
    pair_row = (jnp.sum(jnp.where(pick, base[:, None, None, :], 0), axis=-1) + rank).reshape(s, MOE_TOP_K)
    tile_start = jnp.arange(n_tiles, dtype=jnp.int32) * tm
    tile_expert = jnp.minimum(jnp.sum((tile_start[:, None] >= seg_end[None, :]).astype(jnp.int32), axis=1),
                              N_EXPERTS - 1).astype(jnp.int32)
    tile_live = (tile_start < seg_end[-1]).astype(jnp.int32)
    row_token = (jnp.arange(n_tiles * tm, dtype=jnp.int32) % s).at[pair_row.reshape(-1)].set(
        jnp.arange(MOE_TOP_K * s, dtype=jnp.int32) // MOE_TOP_K)
    return pair_row, row_token, tile_expert, tile_live


def _moe_group_kernel(te_ref, live_ref, x_ref, wg_ref, wu_ref, wd_ref, o_ref, xb_ref, acc_ref):
    g = pl.program_id(0)
    f = pl.program_id(1)
    last = f == pl.num_programs(1) - 1
    live = live_ref[g] > 0

    @pl.when(jnp.logical_and(live, f == 0))
    def _():
        lo, hi = _unpack_bf16_pairs(x_ref[...])
        xb_ref[...] = jnp.concatenate([lo.astype(BF16), hi.astype(BF16)], axis=1)

    @pl.when(jnp.logical_and(jnp.logical_not(live), last))
    def _():
        o_ref[...] = jnp.zeros(o_ref.shape, o_ref.dtype)

    @pl.when(live)
    def _():
        xb = xb_ref[...]
        a = jnp.dot(xb, wg_ref[...], preferred_element_type=F32)
        u = jnp.dot(xb, wu_ref[...], preferred_element_type=F32)
        hid = ((a * jax.nn.sigmoid(a)) * u).astype(BF16)
        y = jnp.dot(hid, wd_ref[...], preferred_element_type=F32)

        @pl.when(f == 0)
        def _():
            acc_ref[...] = y

        @pl.when(f > 0)
        def _():
            acc_ref[...] += y

        @pl.when(last)
        def _():
            o_ref[...] = _pack_bf16_pairs(acc_ref[...])


def _moe_group_ffn(xs, tile_expert, tile_live, wg, wu, wd, tm, tf):
    rows, half = xs.shape
    d = 2 * half
    fdim = wg.shape[2]
    grid_spec = pltpu.PrefetchScalarGridSpec(
        num_scalar_prefetch=2,
        grid=(rows // tm, fdim // tf),
        in_specs=[pl.BlockSpec((tm, half), lambda g, f, te, lv: (g, 0)),
                  pl.BlockSpec((None, d, tf), lambda g, f, te, lv: (te[g], 0, f)),
                  pl.BlockSpec((None, d, tf), lambda g, f, te, lv: (te[g], 0, f)),
                  pl.BlockSpec((None, tf, d), lambda g, f, te, lv: (te[g], f, 0))],
        out_specs=pl.BlockSpec((tm, half), lambda g, f, te, lv: (g, 0)),
        scratch_shapes=[pltpu.VMEM((tm, d), BF16), pltpu.VMEM((tm, d), F32)])
    return pl.pallas_call(
        _moe_group_kernel,
        out_shape=jax.ShapeDtypeStruct((rows, half), jnp.uint32),
        grid_spec=grid_spec,
        compiler_params=_cparams(2, VMEM_LIMIT),
        name="moe_group_ffn",
    )(tile_expert, tile_live, xs, wg, wu, wd)


def _moe_combine_ln_kernel(x_ref, r_ref, y0_ref, y1_ref, g_ref, b_ref, o_ref):
    r = r_ref[...]
    lo0, hi0 = _unpack_bf16_pairs(y0_ref[...])
    lo1, hi1 = _unpack_bf16_pairs(y1_ref[...])
    ff = jnp.concatenate([r[:, 0:1] * lo0 + r[:, 1:2] * lo1, r[:, 0:1] * hi0 + r[:, 1:2] * hi1], axis=1)
    o_ref[...] = _layer_norm(ALPHA * x_ref[...] + ff, g_ref[...], b_ref[...])


def _moe_combine_ln(x, route, y, g, b, tm):
    s, d = x.shape
    nt = s // tm
    return pl.pallas_call(
        _moe_combine_ln_kernel,
        out_shape=jax.ShapeDtypeStruct((s, d), F32),
        grid=(nt,),
        in_specs=[pl.BlockSpec((tm, d), lambda i: (i, 0)),
                  pl.BlockSpec((tm, LANES), lambda i: (i, 0)),
                  pl.BlockSpec((tm, d // 2), lambda i: (i, 0)),
                  pl.BlockSpec((tm, d // 2), lambda i: (i + nt, 0)),
                  pl.BlockSpec((1, d), lambda i: (0, 0)),
                  pl.BlockSpec((1, d), lambda i: (0, 0))],
        out_specs=pl.BlockSpec((tm, d), lambda i: (i, 0)),
        compiler_params=_cparams(1, VMEM_LIMIT),
        name="moe_combine_ln",
    )(x, route, y, y, g.reshape(1, d), b.reshape(1, d))


def _sc_row_gather(src, idx):
    n = idx.shape[0]
    d = src.shape[1]
    workers = SC_CORES * SC_SUBCORES
    per_worker = n // workers
    assert n % (workers * SC_GATHER_ROWS) == 0
    mesh = plsc.VectorSubcoreMesh(core_axis_name="c", subcore_axis_name="s")

    ch = SC_GATHER_ROWS
    n_chunks = per_worker // ch
    assert n_chunks % 2 == 0

    @functools.partial(
        pl.kernel, mesh=mesh, out_type=jax.ShapeDtypeStruct((n, d), src.dtype),
        scratch_types=[pltpu.VMEM((per_worker,), jnp.int32),
                       pltpu.VMEM((ch, d), src.dtype), pltpu.VMEM((ch, d), src.dtype),
                       pltpu.SemaphoreType.DMA, pltpu.SemaphoreType.DMA],
        name="sc_row_gather")
    def gather(src_hbm, idx_hbm, out_hbm, idx_v, rows_a, rows_b, sem_a, sem_b):
        base = (lax.axis_index("s") * SC_CORES + lax.axis_index("c")) * per_worker
        pltpu.sync_copy(idx_hbm.at[pl.ds(base, per_worker)], idx_v)

        def start(c, rows, sem):
            pltpu.async_copy(src_hbm.at[idx_v.at[pl.ds(c * ch, ch)]], rows, sem)

        def finish(c, rows, sem):
            pltpu.make_async_copy(src_hbm.at[pl.ds(0, ch)], rows, sem).wait()
            pltpu.sync_copy(rows, out_hbm.at[pl.ds(base + c * ch, ch)])

        start(0, rows_a, sem_a)

        @pl.loop(0, n_chunks, step=2)
        def _(c):
            start(c + 1, rows_b, sem_b)
            finish(c, rows_a, sem_a)

            @pl.when(c + 2 < n_chunks)
            def _():
                start(c + 2, rows_a, sem_a)

            finish(c + 1, rows_b, sem_b)

    return gather(src, idx)


def _moe_ln(x, x_packed, route, counts, wg, wu, wd, g, b, tm):
    pair_row, row_token, tile_expert, tile_live = _dispatch_plan(route, counts, tm)
    xs = _sc_row_gather(x_packed, row_token)
    ys = _moe_group_ffn(xs, tile_expert, tile_live, wg, wu, wd, tm, MOE_FF_TILE)
    slot_major = pair_row.T.reshape(-1)
    y = _sc_row_gather(ys, slot_major)
    return _moe_combine_ln(x, route, y, g, b, tm)


def kernel(x, w_in_even, w_out_even, forget_bias, w_in_odd, w_out_odd, lambda_q, lambda_k,
           subln_gain, rel_bias, w_router, b_router, w_gate, w_up, w_down, ln_gain, ln_bias):
    bsz, s, d = x.shape
    assert bsz == 1 and d == D_MODEL and s % MOBA_BLOCK == 0
    t = ATT_TILE
    nh = N_HEADS_GROUP
    gw = GROUP_WIDTH
    tm_proj = min(1024, s)
    tm_row = min(512, s)
    tk = min(ATT_KEYS, s)
    n_near = _near_tile_count(t)
    assert s % tk == 0 and tk // t + 1 >= n_near
    assert all(tk % (n * t) == 0 for n in (MOBA_Q_BLOCKS, DIFF_Q_BLOCKS)) and min(FOX_TK, s) % FOX_TQ == 0
    nblk = s // MOBA_BLOCK
    nbp = -(-nblk // LANES) * LANES

    x2 = x.reshape(s, d)
    bias_tiles = _bias_tiles(rel_bias, t, n_near)
    tab_spec = pl.BlockSpec(memory_space=pltpu.SMEM)
    bias_spec = pl.BlockSpec((None, n_near + 2, t, t), lambda hd, i: (hd, 0, 0, 0))
    block_onehot = (jnp.arange(s, dtype=jnp.int32)[:, None] // MOBA_BLOCK
                    == jnp.arange(nbp, dtype=jnp.int32)[None, :]).astype(BF16)

    for layer in range(DEPTH):
        li = layer // 2
        if layer % 2 == 0:
            w = w_in_even[li]
            fcol = 3 * gw
            w_main = jnp.concatenate([w[:, :fcol], w[:, fcol + nh:]], axis=1).astype(BF16)
            w_f = jnp.zeros((d, LANES), BF16).at[:, :nh].set(w[:, fcol:fcol + nh].astype(BF16))
            h = _in_proj(x2, w_main, BF16, tm_proj, PROJ_TN)
            f = _in_proj(x2, w_f, F32, tm_proj, LANES)
            c = _decay_cumsum(f[:, :nh].T, forget_bias[li])
            oa = _attention_call(
                functools.partial(_fox_kernel, t=FOX_TQ, tk=min(FOX_TK, s), scale=HEAD_DIM ** -0.5), h, gw, 0,
                ([c[:, min(FOX_TK, s) - 1::min(FOX_TK, s)], c[:, ::FOX_TQ]],
                 [c.reshape(nh, 1, s), c.reshape(nh, s, 1)]),
                ([tab_spec, tab_spec],
                 [pl.BlockSpec((None, 1, s), lambda hd, i: (hd, 0, 0)),
                  pl.BlockSpec((None, FOX_TQ, 1), lambda hd, i: (hd, i, 0))]),
                [pltpu.VMEM((SUBLANES, LANES), F32)],
                "fox_attention", FOX_TQ)
            ob = _attention_call(
                functools.partial(_moba_kernel, tb=t, nb=MOBA_Q_BLOCKS, tk=tk, nblk=nblk, n_near=n_near,
                                  scale=HEAD_DIM ** -0.5),
                h, gw, 3 * nh,
                ([rel_bias], [bias_tiles, block_onehot]),
                ([tab_spec], [bias_spec, pl.BlockSpec((s, nbp), lambda hd, i: (0, 0))]),
                [pltpu.VMEM((nbp, HEAD_DIM), F32), pltpu.VMEM((SUBLANES, LANES), F32)],
                "moba_attention", MOBA_Q_BLOCKS * t)
            w_out = w_out_even[li]
        else:
            lambda_init = 0.8 - 0.6 * math.exp(-0.3 * layer)
            h = _in_proj(x2, w_in_odd[li].astype(BF16), BF16, tm_proj, PROJ_TN)
            small = lambda shape: pl.BlockSpec(shape, lambda hd, i: (0, 0))
            oa = _attention_call(
                functools.partial(_diff_kernel, tb=t, nb=DIFF_Q_BLOCKS, tk=tk, n_near=n_near,
                                  scale=DIFF_QK_DIM ** -0.5, lambda_init=lambda_init),
                h, gw, 0,
                ([rel_bias], [bias_tiles, lambda_q[li], lambda_k[li], subln_gain[li].reshape(1, HEAD_DIM)]),
                ([tab_spec], [bias_spec, small((2, DIFF_QK_DIM)), small((2, DIFF_QK_DIM)),
                              small((1, HEAD_DIM))]),
                [pltpu.VMEM((SUBLANES, LANES), F32)],
                "diff_attention", DIFF_Q_BLOCKS * t)
            ob = _attention_call(
                functools.partial(_sb_kernel, t=t, nq=SB_Q_TILES, scale=HEAD_DIM ** -0.5), h, gw, 3 * nh,
                ([], []), ([], []), [], "stickbreak_attention", SB_Q_TILES * t)
            w_out = w_out_odd[li]
        x2, x2_packed = _out_proj_ln(oa, ob, w_out.astype(BF16), x2, ln_gain[layer, 0], ln_bias[layer, 0],
                                     tm_row)
        route, counts = _router(x2, w_router, b_router, tm_row)
        x2 = _moe_ln(x2, x2_packed, route, counts, w_gate[layer].astype(BF16), w_up[layer].astype(BF16),
                     w_down[layer].astype(BF16), ln_gain[layer, 1], ln_bias[layer, 1], tm_row)
    return x2.reshape(bsz, s, d)
```

```python
import functools
import math

import numpy as np
import jax
import jax.numpy as jnp
from jax import lax
from jax.experimental import pallas as pl
from jax.experimental.pallas import tpu as pltpu
from jax.experimental.pallas import tpu_sc as plsc

F32 = jnp.float32
BF16 = jnp.bfloat16

D_MODEL = 2048
DEPTH = 2
HEAD_DIM = 128
N_HEADS_GROUP = 8
GROUP_WIDTH = N_HEADS_GROUP * HEAD_DIM
DIFF_QK_DIM = HEAD_DIM // 2
MOBA_BLOCK = 256
MOBA_TOPK = 3
N_BUCKETS = 32
MAX_EXACT = N_BUCKETS // 2
MAX_DISTANCE = 1024
N_EXPERTS = 16
N_GROUPS = 4
EXPERTS_PER_GROUP = N_EXPERTS // N_GROUPS
MOE_TOP_K = 2
SC_CORES = 2
SC_SUBCORES = 16
SC_GATHER_ROWS = 32
SUBLANES = 8
D_EXPERT = D_MODEL // 2
PROJ_TN = 1536
MOE_FF_TILE = D_EXPERT
ALPHA = (2.0 * DEPTH) ** 0.25
LN_EPS = 1e-5
NEG_INF = -1e30
LOG2E = math.log2(math.e)
LANES = 128
ATT_TILE = 256
ATT_KEYS = 1024
FOX_TQ = 1024
FOX_TK = 1024
MOBA_Q_BLOCKS = 4
SB_Q_TILES = 4
DIFF_Q_BLOCKS = 4
SHIFT_SLACK = 96.0
FLUSH_EXPONENT = -130.0
SB_CUTOFF = -104.0
VMEM_LIMIT = 56 * 1024 * 1024


def _cparams(n_axes, vmem=None):
    return pltpu.CompilerParams(dimension_semantics=("arbitrary",) * n_axes,
                                vmem_limit_bytes=vmem)


def _nt_dot(a, b):
    return lax.dot_general(a, b, (((1,), (1,)), ((), ())), preferred_element_type=F32)


def _split_bf16(x):
    hi = x.astype(BF16)
    lo = (x - hi.astype(F32)).astype(BF16)
    return hi, lo


def _pack_bf16_pairs(x):
    n = x.shape[1] // 2
    lo = pltpu.bitcast(x[:, :n].astype(BF16).astype(F32), jnp.uint32)
    hi = pltpu.bitcast(x[:, n:].astype(BF16).astype(F32), jnp.uint32)
    return jnp.bitwise_or(lax.shift_right_logical(lo, jnp.uint32(16)),
                          jnp.bitwise_and(hi, jnp.uint32(0xFFFF0000)))


def _unpack_bf16_pairs(w):
    lo = pltpu.bitcast(lax.shift_left(w, jnp.uint32(16)), F32)
    hi = pltpu.bitcast(jnp.bitwise_and(w, jnp.uint32(0xFFFF0000)), F32)
    return lo, hi


def _layer_norm(z, g, b):
    mu = jnp.mean(z, axis=-1, keepdims=True)
    zc = z - mu
    var = jnp.mean(zc * zc, axis=-1, keepdims=True)
    return zc * lax.rsqrt(var + LN_EPS) * g + b


def _inproj_kernel(x_ref, w_ref, o_ref, xb_ref):
    @pl.when(pl.program_id(1) == 0)
    def _():
        xb_ref[...] = x_ref[...].astype(BF16)

    o_ref[...] = jnp.dot(xb_ref[...], w_ref[...], preferred_element_type=F32).astype(o_ref.dtype)


def _in_proj(x, w, out_dtype, tm, tn):
    s, d = x.shape
    n = w.shape[1]
    return pl.pallas_call(
        _inproj_kernel,
        out_shape=jax.ShapeDtypeStruct((s, n), out_dtype),
        grid=(s // tm, n // tn),
        in_specs=[pl.BlockSpec((tm, d), lambda i, j: (i, 0)),
                  pl.BlockSpec((d, tn), lambda i, j: (0, j))],
        out_specs=pl.BlockSpec((tm, tn), lambda i, j: (i, j)),
        scratch_shapes=[pltpu.VMEM((tm, d), BF16)],
        compiler_params=_cparams(2, VMEM_LIMIT),
        name="in_proj",
    )(x, w)


def _outproj_ln_kernel(oa_ref, ob_ref, w_ref, x_ref, g_ref, b_ref, o_ref, packed_ref):
    half = o_ref.shape[0] // 2
    for r in (slice(0, half), slice(half, 2 * half)):
        o = jnp.concatenate([oa_ref[r, :], ob_ref[r, :]], axis=1)
        y = jnp.dot(o, w_ref[...], preferred_element_type=F32)
        x1 = _layer_norm(ALPHA * x_ref[r, :] + y, g_ref[...], b_ref[...])
        o_ref[r, :] = x1
        packed_ref[r, :] = _pack_bf16_pairs(x1)


def _out_proj_ln(oa, ob, w, x, g, b, tm):
    s, d = x.shape
    gw = oa.shape[1]
    return pl.pallas_call(
        _outproj_ln_kernel,
        out_shape=(jax.ShapeDtypeStruct((s, d), F32), jax.ShapeDtypeStruct((s, d // 2), jnp.uint32)),
        grid=(s // tm,),
        in_specs=[pl.BlockSpec((tm, gw), lambda i: (i, 0)),
                  pl.BlockSpec((tm, gw), lambda i: (i, 0)),
                  pl.BlockSpec((2 * gw, d), lambda i: (0, 0)),
                  pl.BlockSpec((tm, d), lambda i: (i, 0)),
                  pl.BlockSpec((1, d), lambda i: (0, 0)),
                  pl.BlockSpec((1, d), lambda i: (0, 0))],
        out_specs=(pl.BlockSpec((tm, d), lambda i: (i, 0)),
                   pl.BlockSpec((tm, d // 2), lambda i: (i, 0))),
        compiler_params=_cparams(1, VMEM_LIMIT),
        name="out_proj_ln",
    )(oa, ob, w, x, g.reshape(1, d), b.reshape(1, d))


def _router_kernel(x_ref, w_ref, b_ref, route_ref, count_ref):
    route, counts = _route(x_ref[...], w_ref[...], b_ref[...])
    route_ref[...] = route
    count_ref[...] = jnp.broadcast_to(counts, count_ref.shape)


def _router(x, w_router, b_router, tm):
    s, d = x.shape
    wr = jnp.zeros((d, LANES), F32).at[:, :N_EXPERTS].set(w_router)
    br = jnp.zeros((1, LANES), F32).at[0, :N_EXPERTS].set(b_router)
    return pl.pallas_call(
        _router_kernel,
        out_shape=(jax.ShapeDtypeStruct((s, LANES), F32),
                   jax.ShapeDtypeStruct((s // tm, SUBLANES, LANES), F32)),
        grid=(s // tm,),
        in_specs=[pl.BlockSpec((tm, d), lambda i: (i, 0)),
                  pl.BlockSpec((d, LANES), lambda i: (0, 0)),
                  pl.BlockSpec((1, LANES), lambda i: (0, 0))],
        out_specs=(pl.BlockSpec((tm, LANES), lambda i: (i, 0)),
                   pl.BlockSpec((None, SUBLANES, LANES), lambda i: (i, 0, 0))),
        compiler_params=_cparams(1, VMEM_LIMIT),
        name="router",
    )(x, wr, br)


def _decay_cumsum_kernel(f_ref, b_ref, c_ref):
    nh, s = f_ref.shape
    rows = lax.broadcasted_iota(jnp.int32, (LANES, LANES), 0)
    cols = lax.broadcasted_iota(jnp.int32, (LANES, LANES), 1)
    upper = jnp.where(rows <= cols, 1.0, 0.0).astype(F32)
    bias = b_ref[...]

    def body(n, carry):
        off = pl.multiple_of(n * LANES, LANES)
        z = f_ref[:, pl.ds(off, LANES)] + bias
        logf = jnp.minimum(z, 0.0) - jnp.log1p(jnp.exp(-jnp.abs(z)))
        c = jnp.dot(logf, upper, preferred_element_type=F32,
                    precision=lax.Precision.HIGHEST) + carry
        c_ref[:, pl.ds(off, LANES)] = c * LOG2E
        return carry + jnp.sum(logf, axis=-1, keepdims=True)

    lax.fori_loop(0, s // LANES, body, jnp.zeros((nh, 1), F32))


def _decay_cumsum(f_t, bias):
    nh, s = f_t.shape
    return pl.pallas_call(
        _decay_cumsum_kernel,
        out_shape=jax.ShapeDtypeStruct((nh, s), F32),
        name="decay_cumsum",
    )(f_t, bias.reshape(nh, 1))


def _t5_bucket_np(dist):
    n = np.maximum(dist, 0)
    nf = np.maximum(n, 1).astype(np.float32)
    ratio = np.log(nf / np.float32(MAX_EXACT)) / np.float32(math.log(MAX_DISTANCE / MAX_EXACT))
    large = MAX_EXACT + (ratio.astype(np.float32) * np.float32(N_BUCKETS - MAX_EXACT)).astype(np.int32)
    large = np.minimum(large, N_BUCKETS - 1)
    return np.where(n < MAX_EXACT, n, large).astype(np.int32)


def _near_tile_count(t):
    d = np.arange(0, 4 * MAX_DISTANCE, dtype=np.int64)
    not_last = np.nonzero(_t5_bucket_np(d) != N_BUCKETS - 1)[0]
    d_sat = int(not_last.max()) + 1
    n = 1
    while (n - 1) * t + 1 < d_sat:
        n += 1
    return n


def _bucket_tiles_np(t, n_near):
    r = np.arange(t)[:, None]
    c = np.arange(t)[None, :]
    tiles = []
    for delta in range(n_near):
        dist = delta * t + r - c
        tiles.append(np.where(dist >= 0, _t5_bucket_np(dist), -1))
    tiles.append(np.full((t, t), N_BUCKETS - 1))
    tiles.append(np.full((t, t), -1))
    return np.stack(tiles).astype(np.int32)


def _bias_tiles_kernel(tab_ref, idx_ref, o_ref):
    h = pl.program_id(0)
    idx = idx_ref[...]
    out = jnp.full(idx.shape, NEG_INF, F32)
    for b in range(N_BUCKETS):
        out = jnp.where(idx == b, tab_ref[b, h] * LOG2E, out)
    o_ref[...] = out


def _bias_tiles(rel_bias, t, n_near):
    nh = rel_bias.shape[1]
    idx = jnp.asarray(_bucket_tiles_np(t, n_near))
    n_tiles = idx.shape[0]
    return pl.pallas_call(
        _bias_tiles_kernel,
        out_shape=jax.ShapeDtypeStruct((nh, n_tiles, t, t), F32),
        grid=(nh, n_tiles),
        in_specs=[pl.BlockSpec(memory_space=pltpu.SMEM),
                  pl.BlockSpec((None, t, t), lambda h, n: (n, 0, 0))],
        out_specs=pl.BlockSpec((None, None, t, t), lambda h, n: (h, n, 0, 0)),
        compiler_params=_cparams(2),
        name="bias_tiles",
    )(rel_bias, idx)


def _online_softmax_step(s, v, carry):
    m, l, acc = carry
    m_new = jnp.maximum(m, jnp.max(s, axis=-1, keepdims=True))
    alpha = jnp.exp2(m - m_new)
    p = jnp.exp2(s - m_new)
    l = alpha * l + jnp.sum(p, axis=-1, keepdims=True)
    acc = alpha * acc + jnp.dot(p.astype(BF16), v, preferred_element_type=F32)
    return m_new, l, acc


def _prescale(q, c):
    return (q.astype(F32) * c).astype(BF16)


def _near_bias(bias_ref, blocks, j, tb, tk, n_near):
    nsub = tk // tb
    rows = []
    for i in blocks:
        tiles = []
        for b in range(nsub):
            delta = i - (j * nsub + b)
            tiles.append(bias_ref[jnp.where(delta < 0, n_near + 1, jnp.minimum(delta, n_near))])
        rows.append(jnp.concatenate(tiles, axis=1))
    return rows[0] if len(rows) == 1 else jnp.concatenate(rows, axis=0)


def _biased_sweep(i, t, tk, scores, process, init):
    jd = (i * t) // tk
    carry = process(scores(jd), jd, init, True)
    lo = jnp.maximum(jd - 1, 0)
    carry = lax.fori_loop(lo, jd, lambda j, c: process(scores(j), j, c, True), carry)
    return lax.fori_loop(0, lo, lambda j, c: process(scores(j), j, c, False), carry)


def _softmax_init(rows):
    return (jnp.full((rows, 1), NEG_INF, F32), jnp.zeros((rows, 1), F32),
            jnp.zeros((rows, HEAD_DIM), F32))


def _fox_kernel(cend_ref, cstart_ref, q_ref, k_ref, v_ref, c_ref, ccol_ref, o_ref, knorm_ref,
                *, t, tk, scale):
    h = pl.program_id(0)
    i = pl.program_id(1)

    @pl.when(i == 0)
    def _():
        knorm_ref[...] = jnp.broadcast_to(_max_key_norm(k_ref, tk, 0, HEAD_DIM), knorm_ref.shape)

    q = _prescale(q_ref[...], scale * LOG2E)
    rows = lax.broadcasted_iota(jnp.int32, (t, tk), 0)
    cols = lax.broadcasted_iota(jnp.int32, (t, tk), 1)

    def scores(j):
        off = pl.multiple_of(j * tk, tk)
        return _nt_dot(q, k_ref[pl.ds(off, tk), :]) - c_ref[:, pl.ds(off, tk)]

    def values(j):
        return v_ref[pl.ds(pl.multiple_of(j * tk, tk), tk), :]

    def process(s, j, carry):
        return _online_softmax_step(s, values(j), carry)

    jd = (i * t) // tk

    def diag_scores():
        return jnp.where(cols <= rows + (i * t - jd * tk), scores(jd), NEG_INF)

    qf = q.astype(F32)
    qnorm = jnp.sqrt(jnp.sum(qf * qf, axis=-1, keepdims=True))
    bound = qnorm * knorm_ref[0:1, 0:1] * (1.0 + 2.0 ** -9) + 2.0 ** -6 - ccol_ref[...]
    row_floor = _diagonal_logit(qf, k_ref, pl.multiple_of(i * t, t), t) - ccol_ref[...]
    slack = jnp.max(bound - row_floor)

    def shifted_sweep(_):
        def step(s, j, carry):
            l, acc = carry
            p = jnp.exp2(s - bound)
            return (l + jnp.sum(p, axis=-1, keepdims=True),
                    acc + jnp.dot(p.astype(BF16), values(j), preferred_element_type=F32))

        c_first_row = cstart_ref[h, i]
        first_live = jnp.int32(0)
        for j in range(cend_ref.shape[1]):
            dead = jnp.logical_and(j < jd, c_first_row - cend_ref[h, j] < FLUSH_EXPONENT)
            first_live = first_live + dead.astype(jnp.int32)
        carry = step(diag_scores(), jd, _softmax_init(t)[1:])
        return _paired_loop(jd - first_live, lambda n, c: step(scores(first_live + n), first_live + n, c),
                            carry)

    def running_max_sweep(_):
        carry = _online_softmax_step(diag_scores(), values(jd), _softmax_init(t))
        return lax.fori_loop(0, jd, lambda j, c: process(scores(j), j, c), carry)[1:]

    shifted = shifted_sweep(0)
    l, acc = lax.cond(slack <= SHIFT_SLACK, lambda _: shifted, running_max_sweep, 0)
    o_ref[...] = (acc / l).astype(o_ref.dtype)


def _moba_kernel(tab_ref, q_ref, k_ref, v_ref, bias_ref, et_ref, o_ref, kmean_ref, knorm_ref,
                 *, tb, nb, tk, nblk, n_near, scale):
    h = pl.program_id(0)
    g = pl.program_id(1)
    t = nb * tb
    nbp = kmean_ref.shape[0]

    @pl.when(g == 0)
    def _():
        kmean_ref[...] = jnp.zeros(kmean_ref.shape, F32)

        def mean_body(n, _):
            off = pl.multiple_of(n * tb, tb)
            kb = k_ref[pl.ds(off, tb), :].astype(F32)
            kmean_ref[pl.ds(n, 1), :] = jnp.sum(kb, axis=0, keepdims=True) * (1.0 / tb)
            return 0

        lax.fori_loop(0, nblk, mean_body, 0)
        knorm_ref[...] = jnp.broadcast_to(_max_key_norm(k_ref, tk, 0, HEAD_DIM), knorm_ref.shape)

    q = q_ref[...]
    km_hi, km_lo = _split_bf16(kmean_ref[...])
    gate = _nt_dot(q, km_hi) + _nt_dot(q, km_lo)
    blk = lax.broadcasted_iota(jnp.int32, (t, nbp), 1)
    own = g * nb + lax.broadcasted_iota(jnp.int32, (t, 1), 0) // tb
    eligible = blk < own
    sel = jnp.zeros((t, nbp), F32)
    for r in range(MOBA_TOPK):
        cand = jnp.where(eligible, jnp.where(sel > 0.0, -jnp.inf, gate), -jnp.inf)
        vmax = jnp.max(cand, axis=-1, keepdims=True)
        first = jnp.min(jnp.where(cand == vmax, blk, nbp), axis=-1, keepdims=True)
        take = jnp.where(r < own, 1.0, 0.0)
        sel = jnp.maximum(sel, jnp.where(blk == first, take, 0.0))

    sel = jnp.maximum(sel, jnp.where(blk == own, 1.0, 0.0))
    penalty = jnp.where(sel > 0.0, 0.0, NEG_INF).astype(BF16)
    q_aug = jnp.concatenate([_prescale(q, scale * LOG2E), penalty], axis=1)
    b_far = tab_ref[N_BUCKETS - 1, h] * LOG2E

    def scores(j):
        off = pl.multiple_of(j * tk, tk)
        k_aug = jnp.concatenate([k_ref[pl.ds(off, tk), :], et_ref[pl.ds(off, tk), :]], axis=1)
        return _nt_dot(q_aug, k_aug)

    def values(j):
        return v_ref[pl.ds(pl.multiple_of(j * tk, tk), tk), :]

    def near_bias(j):
        return _near_bias(bias_ref, [g * nb + b for b in range(nb)], j, tb, tk, n_near)

    def process(s, j, carry, near):
        return _online_softmax_step(s + (near_bias(j) if near else b_far), values(j), carry)

    qf = q_aug[:, :HEAD_DIM].astype(F32)
    qnorm = jnp.sqrt(jnp.sum(qf * qf, axis=-1, keepdims=True))
    bound = _logit_bound(qnorm, knorm_ref[0:1, 0:1], tab_ref, h)
    row_floor = _diagonal_logit(qf, k_ref, pl.multiple_of(g * t, t), t) + tab_ref[0, h] * LOG2E
    l, acc = _shifted_or_running_sweep(g, t, tk, scores, near_bias, values, process, bound, row_floor,
                                       b_far, t)
    o_ref[...] = (acc / l).astype(o_ref.dtype)


def _paired_loop(n, body, carry):
    carry = lax.fori_loop(0, n // 2, lambda p, c: body(2 * p + 1, body(2 * p, c)), carry)
    return lax.fori_loop(0, n % 2, lambda _, c: body(n - 1, c), carry)


def _logit_bound(qnorm, knorm, tab_ref, h):
    b_max = tab_ref[0, h]
    for b in range(1, N_BUCKETS):
        b_max = jnp.maximum(b_max, tab_ref[b, h])
    return qnorm * knorm * (1.0 + 2.0 ** -9) + (b_max * LOG2E + 2.0 ** -6)


def _shifted_or_running_sweep(i, t, tk, scores, near_bias, values, process, bound, row_floor, b_far, rows):
    jd = (i * t) // tk
    slack = jnp.max(bound - row_floor)

    def shifted_sweep(_):
        def step(s, j, carry, shift):
            l, acc = carry
            p = jnp.exp2(s - shift)
            return (l + jnp.sum(p, axis=-1, keepdims=True),
                    acc + jnp.dot(p.astype(BF16), values(j), preferred_element_type=F32))

        carry = step(scores(jd) + near_bias(jd), jd, _softmax_init(rows)[1:], bound)
        lo = jnp.maximum(jd - 1, 0)
        carry = lax.fori_loop(lo, jd, lambda j, c: step(scores(j) + near_bias(j), j, c, bound), carry)
        far_shift = bound - b_far
        return _paired_loop(lo, lambda j, c: step(scores(j), j, c, far_shift), carry)

    def running_max_sweep(_):
        return _biased_sweep(i, t, tk, scores, process, _softmax_init(rows))[1:]

    shifted = shifted_sweep(0)
    return lax.cond(slack <= SHIFT_SLACK, lambda _: shifted, running_max_sweep, 0)


def _diagonal_logit(qf, k_ref, row0, t):
    kf = k_ref[pl.ds(row0, t), :].astype(F32)
    if qf.shape[0] != t:
        kf = jnp.concatenate([kf] * (qf.shape[0] // t), axis=0)
    return jnp.sum(qf * kf, axis=-1, keepdims=True)


def _max_key_norm(k_ref, tk, lane_lo, lane_hi):
    klane = lax.broadcasted_iota(jnp.int32, (tk, HEAD_DIM), 1)
    keep = jnp.logical_and(klane >= lane_lo, klane < lane_hi)

    def body(n, c):
        kt = k_ref[pl.ds(pl.multiple_of(n * tk, tk), tk), :].astype(F32)
        sq = jnp.sum(jnp.where(keep, kt * kt, 0.0), axis=-1, keepdims=True)
        return jnp.maximum(c, jnp.max(sq, axis=0, keepdims=True))

    return jnp.sqrt(lax.fori_loop(0, k_ref.shape[0] // tk, body, jnp.zeros((1, 1), F32)))


def _diff_kernel(tab_ref, q_ref, k_ref, v_ref, bias_ref, lq_ref, lk_ref, g_ref, o_ref, knorm_ref,
                 *, tb, nb, tk, n_near, scale, lambda_init):
    h = pl.program_id(0)
    i = pl.program_id(1)
    t = nb * tb
    q = _prescale(q_ref[...], scale * LOG2E)
    lane = lax.broadcasted_iota(jnp.int32, (t, HEAD_DIM), 1)
    zero = jnp.zeros_like(q)
    q2 = jnp.concatenate([jnp.where(lane < DIFF_QK_DIM, q, zero),
                          jnp.where(lane >= DIFF_QK_DIM, q, zero)], axis=0)
    b_far = tab_ref[N_BUCKETS - 1, h] * LOG2E

    @pl.when(i == 0)
    def _():
        knorm_ref[0:1, :] = jnp.broadcast_to(_max_key_norm(k_ref, tk, 0, DIFF_QK_DIM), (1, LANES))
        knorm_ref[1:2, :] = jnp.broadcast_to(_max_key_norm(k_ref, tk, DIFF_QK_DIM, HEAD_DIM), (1, LANES))

    def scores(j):
        return _nt_dot(q2, k_ref[pl.ds(pl.multiple_of(j * tk, tk), tk), :])

    def values(j):
        return v_ref[pl.ds(pl.multiple_of(j * tk, tk), tk), :]

    def near_bias(j):
        b = _near_bias(bias_ref, [i * nb + b for b in range(nb)], j, tb, tk, n_near)
        return jnp.concatenate([b, b], axis=0)

    def process(s, j, carry, near):
        return _online_softmax_step(s + (near_bias(j) if near else b_far), values(j), carry)

    q2f = q2.astype(F32)
    qnorm = jnp.sqrt(jnp.sum(q2f * q2f, axis=-1, keepdims=True))
    knorm = jnp.concatenate([jnp.broadcast_to(knorm_ref[0:1, 0:1], (t, 1)),
                             jnp.broadcast_to(knorm_ref[1:2, 0:1], (t, 1))], axis=0)
    bound = _logit_bound(qnorm, knorm, tab_ref, h)
    row_floor = _diagonal_logit(q2f, k_ref, pl.multiple_of(i * t, t), t) + tab_ref[0, h] * LOG2E
    l, acc = _shifted_or_running_sweep(i, t, tk, scores, near_bias, values, process, bound, row_floor,
                                       b_far, 2 * t)
    o12 = acc / l
    lam_e = jnp.exp(jnp.sum(lq_ref[...] * lk_ref[...], axis=-1, keepdims=True))
    lam = lam_e[0:1, :] - lam_e[1:2, :] + lambda_init
    o = o12[:t, :] - lam * o12[t:, :]
    o = o * lax.rsqrt(jnp.mean(o * o, axis=-1, keepdims=True) + LN_EPS) * g_ref[...]
    o_ref[...] = (o * (1.0 - lambda_init)).astype(o_ref.dtype)


def _sb_kernel(q_ref, k_ref, v_ref, o_ref, *, t, nq, scale):
    g = pl.program_id(1)

    def later_matrix(n):
        r = lax.broadcasted_iota(jnp.int32, (n, n), 0)
        c = lax.broadcasted_iota(jnp.int32, (n, n), 1)
        return jnp.where(r > c, 1.0, 0.0).astype(BF16)

    def block(q, off, nk, rem, acc, shift):
        k = k_ref[pl.ds(off, nk), :]
        v = v_ref[pl.ds(off, nk), :]
        z = _nt_dot(q, k) * scale
        soft = jnp.log1p(jnp.exp(-jnp.abs(z)))
        log_sig = jnp.minimum(z, 0.0) - soft
        log_rem = jnp.minimum(-z, 0.0) - soft
        if shift is not None:
            rows = lax.broadcasted_iota(jnp.int32, (t, nk), 0)
            cols = lax.broadcasted_iota(jnp.int32, (t, nk), 1)
            mask = cols < rows + shift
            log_rem = jnp.where(mask, log_rem, 0.0)
        hi, lo = _split_bf16(log_rem)
        later = later_matrix(nk)
        after = (jnp.dot(hi, later, preferred_element_type=F32)
                 + jnp.dot(lo, later, preferred_element_type=F32))
        a = jnp.exp(log_sig + after + rem)
        if shift is not None:
            a = jnp.where(mask, a, 0.0)
        acc = acc + jnp.dot(a.astype(BF16), v, preferred_element_type=F32)
        rem = rem + jnp.sum(log_rem, axis=-1, keepdims=True)
        return rem, acc

    def live(rem):
        return (jnp.max(rem) > SB_CUTOFF).astype(jnp.int32)

    state = []
    for u in range(nq):
        i = g * nq + u
        q = q_ref[u * t:(u + 1) * t, :]
        first = jnp.maximum(i - 1, 0)
        rem, acc = block(q, pl.multiple_of(first * t, t), 2 * t, jnp.zeros((t, 1), F32),
                         jnp.zeros((t, HEAD_DIM), F32), (i - first) * t)
        state.append((i, q, rem, acc))

    for u, (i, q, rem, acc) in enumerate(state):
        def cond(c):
            j, go, _, _ = c
            return jnp.logical_and(j >= 0, go > 0)

        def body(c, q=q):
            j, _, rem, acc = c
            rem, acc = block(q, pl.multiple_of(j * t, t), t, rem, acc, None)
            return j - 1, live(rem), rem, acc

        _, _, _, acc = lax.while_loop(cond, body, (i - 2, live(rem), rem, acc))
        o_ref[u * t:(u + 1) * t, :] = acc.astype(o_ref.dtype)


def _attention_call(kernel, h, out_cols, col0, extra_in, extra_specs, scratch, name, t):
    s = h.shape[0]
    nh = N_HEADS_GROUP
    in_specs = list(extra_specs[0]) + [
        pl.BlockSpec((t, HEAD_DIM), lambda hd, i: (i, col0 + hd)),
        pl.BlockSpec((s, HEAD_DIM), lambda hd, i: (0, col0 + nh + hd)),
        pl.BlockSpec((s, HEAD_DIM), lambda hd, i: (0, col0 + 2 * nh + hd)),
    ] + list(extra_specs[1])
    args = list(extra_in[0]) + [h, h, h] + list(extra_in[1])
    return pl.pallas_call(
        kernel,
        out_shape=jax.ShapeDtypeStruct((s, out_cols), BF16),
        grid=(nh, s // t),
        in_specs=in_specs,
        out_specs=pl.BlockSpec((t, HEAD_DIM), lambda hd, i: (i, hd)),
        scratch_shapes=scratch,
        compiler_params=_cparams(2, VMEM_LIMIT),
        name=name,
    )(*args)


def _route(x, w, b):
    x_hi, x_lo = _split_bf16(x)
    w_hi, w_lo = _split_bf16(w)
    logits = (jnp.dot(x_hi, w_hi, preferred_element_type=F32) + jnp.dot(x_lo, w_hi, preferred_element_type=F32)
              + jnp.dot(x_hi, w_lo, preferred_element_type=F32)) + b
    tm, n = logits.shape
    lane = lax.broadcasted_iota(jnp.int32, (tm, n), 1)
    real = lane < N_EXPERTS
    logits = jnp.where(real, logits, -jnp.inf)
    e = jnp.exp(logits - jnp.max(logits, axis=-1, keepdims=True))
    aff = e / jnp.sum(e, axis=-1, keepdims=True)
    group = lane // EXPERTS_PER_GROUP
    best = jnp.full((tm, 1), -jnp.inf, F32)
    g_sel = jnp.zeros((tm, 1), jnp.int32)
    for g in range(N_GROUPS):
        gmax = jnp.max(jnp.where(group == g, aff, -jnp.inf), axis=-1, keepdims=True)
        better = gmax > best
        g_sel = jnp.where(better, g, g_sel)
        best = jnp.where(better, gmax, best)
    in_group = jnp.logical_and(group == g_sel, real)
    cand = jnp.where(in_group, aff, -jnp.inf)
    v1 = jnp.max(cand, axis=-1, keepdims=True)
    i1 = jnp.min(jnp.where(cand == v1, lane, n), axis=-1, keepdims=True)
    cand2 = jnp.where(lane == i1, -jnp.inf, cand)
    v2 = jnp.max(cand2, axis=-1, keepdims=True)
    i2 = jnp.min(jnp.where(cand2 == v2, lane, n), axis=-1, keepdims=True)
    tot = v1 + v2
    chosen = jnp.where(jnp.logical_or(lane == i1, lane == i2), 1.0, 0.0)
    r = lax.broadcasted_iota(jnp.int32, (tm, tm), 0)
    c = lax.broadcasted_iota(jnp.int32, (tm, tm), 1)
    earlier = jnp.where(c < r, 1.0, 0.0).astype(BF16)
    before = jnp.dot(earlier, chosen.astype(BF16), preferred_element_type=F32)
    rank1 = jnp.sum(jnp.where(lane == i1, before, 0.0), axis=-1, keepdims=True)
    rank2 = jnp.sum(jnp.where(lane == i2, before, 0.0), axis=-1, keepdims=True)
    route = jnp.where(lane == 0, v1 / tot,
                      jnp.where(lane == 1, v2 / tot,
                                jnp.where(lane == 2, i1.astype(F32),
                                          jnp.where(lane == 3, i2.astype(F32),
                                                    jnp.where(lane == 4, rank1,
                                                              jnp.where(lane == 5, rank2, 0.0))))))
    return route, jnp.sum(chosen, axis=0, keepdims=True)


def _dispatch_plan(route, counts, tm):
    s = route.shape[0]
    nt = s // tm
    n_tiles = (MOE_TOP_K * s) // tm + N_EXPERTS
    ids = jnp.arange(N_EXPERTS, dtype=jnp.int32)
    expert = route[:, 2:2 + MOE_TOP_K].astype(jnp.int32).reshape(nt, tm, MOE_TOP_K)
    rank = route[:, 4:4 + MOE_TOP_K].astype(jnp.int32).reshape(nt, tm, MOE_TOP_K)
    cnt = counts[:, 0, :N_EXPERTS].astype(jnp.int32)
    tile_base = jnp.cumsum(cnt, axis=0) - cnt
    padded = -(-jnp.sum(cnt, axis=0) // tm) * tm
    seg_end = jnp.cumsum(padded)
    base = (seg_end - padded)[None, :] + tile_base
    pick = expert[..., None] == ids
    pair_row = (jnp.sum(jnp.where(pick, base[:, None, None, :], 0), axis=-1) + rank).reshape(s, MOE_TOP_K)
    tile_start = jnp.arange(n_tiles, dtype=jnp.int32) * tm
    tile_expert = jnp.minimum(jnp.sum((tile_start[:, None] >= seg_end[None, :]).astype(jnp.int32), axis=1),
                              N_EXPERTS - 1).astype(jnp.int32)
    tile_live = (tile_start < seg_end[-1]).astype(jnp.int32)
    row_token = (jnp.arange(n_tiles * tm, dtype=jnp.int32) % s).at[pair_row.reshape(-1)].set(
        jnp.arange(MOE_TOP_K * s, dtype=jnp.int32) // MOE_TOP_K)
    return pair_row, row_token, tile_expert, tile_live


def _moe_group_kernel(te_ref, live_ref, x_ref, wg_ref, wu_ref, wd_ref, o_ref, xb_ref, acc_ref):
    g = pl.program_id(0)
    f = pl.program_id(1)
    last = f == pl.num_programs(1) - 1
    live = live_ref[g] > 0

    @pl.when(jnp.logical_and(live, f == 0))
    def _():
        lo, hi = _unpack_bf16_pairs(x_ref[...])
        xb_ref[...] = jnp.concatenate([lo.astype(BF16), hi.astype(BF16)], axis=1)

    @pl.when(jnp.logical_and(jnp.logical_not(live), last))
    def _():
        o_ref[...] = jnp.zeros(o_ref.shape, o_ref.dtype)

    @pl.when(live)
    def _():
        xb = xb_ref[...]
        a = jnp.dot(xb, wg_ref[...], preferred_element_type=F32)
        u = jnp.dot(xb, wu_ref[...], preferred_element_type=F32)
        hid = ((a * jax.nn.sigmoid(a)) * u).astype(BF16)
        y = jnp.dot(hid, wd_ref[...], preferred_element_type=F32)

        @pl.when(f == 0)
        def _():
            acc_ref[...] = y

        @pl.when(f > 0)
        def _():
            acc_ref[...] += y

        @pl.when(last)
        def _():
            o_ref[...] = _pack_bf16_pairs(acc_ref[...])


def _moe_group_ffn(xs, tile_expert, tile_live, wg, wu, wd, tm, tf):
    rows, half = xs.shape
    d = 2 * half
    fdim = wg.shape[2]
    grid_spec = pltpu.PrefetchScalarGridSpec(
        num_scalar_prefetch=2,
        grid=(rows // tm, fdim // tf),
        in_specs=[pl.BlockSpec((tm, half), lambda g, f, te, lv: (g, 0)),
                  pl.BlockSpec((None, d, tf), lambda g, f, te, lv: (te[g], 0, f)),
                  pl.BlockSpec((None, d, tf), lambda g, f, te, lv: (te[g], 0, f)),
                  pl.BlockSpec((None, tf, d), lambda g, f, te, lv: (te[g], f, 0))],
        out_specs=pl.BlockSpec((tm, half), lambda g, f, te, lv: (g, 0)),
        scratch_shapes=[pltpu.VMEM((tm, d), BF16), pltpu.VMEM((tm, d), F32)])
    return pl.pallas_call(
        _moe_group_kernel,
        out_shape=jax.ShapeDtypeStruct((rows, half), jnp.uint32),
        grid_spec=grid_spec,
        compiler_params=_cparams(2, VMEM_LIMIT),
        name="moe_group_ffn",
    )(tile_expert, tile_live, xs, wg, wu, wd)


def _moe_combine_ln_kernel(x_ref, r_ref, y0_ref, y1_ref, g_ref, b_ref, o_ref):
    r = r_ref[...]
    lo0, hi0 = _unpack_bf16_pairs(y0_ref[...])
    lo1, hi1 = _unpack_bf16_pairs(y1_ref[...])
    ff = jnp.concatenate([r[:, 0:1] * lo0 + r[:, 1:2] * lo1, r[:, 0:1] * hi0 + r[:, 1:2] * hi1], axis=1)
    o_ref[...] = _layer_norm(ALPHA * x_ref[...] + ff, g_ref[...], b_ref[...])


def _moe_combine_ln(x, route, y, g, b, tm):
    s, d = x.shape
    nt = s // tm
    return pl.pallas_call(
        _moe_combine_ln_kernel,
        out_shape=jax.ShapeDtypeStruct((s, d), F32),
        grid=(nt,),
        in_specs=[pl.BlockSpec((tm, d), lambda i: (i, 0)),
                  pl.BlockSpec((tm, LANES), lambda i: (i, 0)),
                  pl.BlockSpec((tm, d // 2), lambda i: (i, 0)),
                  pl.BlockSpec((tm, d // 2), lambda i: (i + nt, 0)),
                  pl.BlockSpec((1, d), lambda i: (0, 0)),
                  pl.BlockSpec((1, d), lambda i: (0, 0))],
        out_specs=pl.BlockSpec((tm, d), lambda i: (i, 0)),
        compiler_params=_cparams(1, VMEM_LIMIT),
        name="moe_combine_ln",
    )(x, route, y, y, g.reshape(1, d), b.reshape(1, d))


def _sc_row_gather(src, idx):
    n = idx.shape[0]
    d = src.shape[1]
    workers = SC_CORES * SC_SUBCORES
    per_worker = n // workers
    assert n % (workers * SC_GATHER_ROWS) == 0
    mesh = plsc.VectorSubcoreMesh(core_axis_name="c", subcore_axis_name="s")

    ch = SC_GATHER_ROWS
    n_chunks = per_worker // ch
    assert n_chunks % 2 == 0

    @functools.partial(
        pl.kernel, mesh=mesh, out_type=jax.ShapeDtypeStruct((n, d), src.dtype),
        scratch_types=[pltpu.VMEM((per_worker,), jnp.int32),
                       pltpu.VMEM((ch, d), src.dtype), pltpu.VMEM((ch, d), src.dtype),
                       pltpu.SemaphoreType.DMA, pltpu.SemaphoreType.DMA],
        name="sc_row_gather")
    def gather(src_hbm, idx_hbm, out_hbm, idx_v, rows_a, rows_b, sem_a, sem_b):
        base = (lax.axis_index("s") * SC_CORES + lax.axis_index("c")) * per_worker
        pltpu.sync_copy(idx_hbm.at[pl.ds(base, per_worker)], idx_v)

        def start(c, rows, sem):
            pltpu.async_copy(src_hbm.at[idx_v.at[pl.ds(c * ch, ch)]], rows, sem)

        def finish(c, rows, sem):
            pltpu.make_async_copy(src_hbm.at[pl.ds(0, ch)], rows, sem).wait()
            pltpu.sync_copy(rows, out_hbm.at[pl.ds(base + c * ch, ch)])

        start(0, rows_a, sem_a)

        @pl.loop(0, n_chunks, step=2)
        def _(c):
            start(c + 1, rows_b, sem_b)
            finish(c, rows_a, sem_a)

            @pl.when(c + 2 < n_chunks)
            def _():
                start(c + 2, rows_a, sem_a)

            finish(c + 1, rows_b, sem_b)

    return gather(src, idx)


def _moe_ln(x, x_packed, route, counts, wg, wu, wd, g, b, tm):
    pair_row, row_token, tile_expert, tile_live = _dispatch_plan(route, counts, tm)
    xs = _sc_row_gather(x_packed, row_token)
    ys = _moe_group_ffn(xs, tile_expert, tile_live, wg, wu, wd, tm, MOE_FF_TILE)
    slot_major = pair_row.T.reshape(-1)
    y = _sc_row_gather(ys, slot_major)
    return _moe_combine_ln(x, route, y, g, b, tm)


def kernel(x, w_in_even, w_out_even, forget_bias, w_in_odd, w_out_odd, lambda_q, lambda_k,
           subln_gain, rel_bias, w_router, b_router, w_gate, w_up, w_down, ln_gain, ln_bias):
    bsz, s, d = x.shape
    assert bsz == 1 and d == D_MODEL and s % MOBA_BLOCK == 0
    t = ATT_TILE
    nh = N_HEADS_GROUP
    gw = GROUP_WIDTH
    tm_proj = min(1024, s)
    tm_row = min(512, s)
    tk = min(ATT_KEYS, s)
    n_near = _near_tile_count(t)
    assert s % tk == 0 and tk // t + 1 >= n_near
    assert all(tk % (n * t) == 0 for n in (MOBA_Q_BLOCKS, DIFF_Q_BLOCKS)) and min(FOX_TK, s) % FOX_TQ == 0
    nblk = s // MOBA_BLOCK
    nbp = -(-nblk // LANES) * LANES

    x2 = x.reshape(s, d)
    bias_tiles = _bias_tiles(rel_bias, t, n_near)
    tab_spec = pl.BlockSpec(memory_space=pltpu.SMEM)
    bias_spec = pl.BlockSpec((None, n_near + 2, t, t), lambda hd, i: (hd, 0, 0, 0))
    block_onehot = (jnp.arange(s, dtype=jnp.int32)[:, None] // MOBA_BLOCK
                    == jnp.arange(nbp, dtype=jnp.int32)[None, :]).astype(BF16)

    for layer in range(DEPTH):
        li = layer // 2
        if layer % 2 == 0:
            w = w_in_even[li]
            fcol = 3 * gw
            w_main = jnp.concatenate([w[:, :fcol], w[:, fcol + nh:]], axis=1).astype(BF16)
            w_f = jnp.zeros((d, LANES), BF16).at[:, :nh].set(w[:, fcol:fcol + nh].astype(BF16))
            h = _in_proj(x2, w_main, BF16, tm_proj, PROJ_TN)
            f = _in_proj(x2, w_f, F32, tm_proj, LANES)
            c = _decay_cumsum(f[:, :nh].T, forget_bias[li])
            oa = _attention_call(
                functools.partial(_fox_kernel, t=FOX_TQ, tk=min(FOX_TK, s), scale=HEAD_DIM ** -0.5), h, gw, 0,
                ([c[:, min(FOX_TK, s) - 1::min(FOX_TK, s)], c[:, ::FOX_TQ]],
                 [c.reshape(nh, 1, s), c.reshape(nh, s, 1)]),
                ([tab_spec, tab_spec],
                 [pl.BlockSpec((None, 1, s), lambda hd, i: (hd, 0, 0)),
                  pl.BlockSpec((None, FOX_TQ, 1), lambda hd, i: (hd, i, 0))]),
                [pltpu.VMEM((SUBLANES, LANES), F32)],
                "fox_attention", FOX_TQ)
            ob = _attention_call(
                functools.partial(_moba_kernel, tb=t, nb=MOBA_Q_BLOCKS, tk=tk, nblk=nblk, n_near=n_near,
                                  scale=HEAD_DIM ** -0.5),
                h, gw, 3 * nh,
                ([rel_bias], [bias_tiles, block_onehot]),
                ([tab_spec], [bias_spec, pl.BlockSpec((s, nbp), lambda hd, i: (0, 0))]),
                [pltpu.VMEM((nbp, HEAD_DIM), F32), pltpu.VMEM((SUBLANES, LANES), F32)],
                "moba_attention", MOBA_Q_BLOCKS * t)
            w_out = w_out_even[li]
        else:
            lambda_init = 0.8 - 0.6 * math.exp(-0.3 * layer)
            h = _in_proj(x2, w_in_odd[li].astype(BF16), BF16, tm_proj, PROJ_TN)
            small = lambda shape: pl.BlockSpec(shape, lambda hd, i: (0, 0))
            oa = _attention_call(
                functools.partial(_diff_kernel, tb=t, nb=DIFF_Q_BLOCKS, tk=tk, n_near=n_near,
                                  scale=DIFF_QK_DIM ** -0.5, lambda_init=lambda_init),
                h, gw, 0,
                ([rel_bias], [bias_tiles, lambda_q[li], lambda_k[li], subln_gain[li].reshape(1, HEAD_DIM)]),
                ([tab_spec], [bias_spec, small((2, DIFF_QK_DIM)), small((2, DIFF_QK_DIM)),
                              small((1, HEAD_DIM))]),
                [pltpu.VMEM((SUBLANES, LANES), F32)],
                "diff_attention", DIFF_Q_BLOCKS * t)
            ob = _attention_call(
                functools.partial(_sb_kernel, t=t, nq=SB_Q_TILES, scale=HEAD_DIM ** -0.5), h, gw, 3 * nh,
                ([], []), ([], []), [], "stickbreak_attention", SB_Q_TILES * t)
            w_out = w_out_odd[li]
        x2, x2_packed = _out_proj_ln(oa, ob, w_out.astype(BF16), x2, ln_gain[layer, 0], ln_bias[layer, 0],
                                     tm_row)
        route, counts = _router(x2, w_router, b_router, tm_row)
        x2 = _moe_ln(x2, x2_packed, route, counts, w_gate[layer].astype(BF16), w_up[layer].astype(BF16),
                     w_down[layer].astype(BF16), ln_gain[layer, 1], ln_bias[layer, 1], tm_row)
    return x2.reshape(bsz, s, d)
```
